```python
import math
import jax, jax.numpy as jnp
from jax import lax
import numpy as np

D_MODEL = 1024
BATCH = 16
SEQ = 2048
DEPTH = 4

D_FF = 2816
N_BRANCH = 3
BRANCH_WIDTH = D_MODEL // 2
POOL_WINDOWS = (2, 4, 8, 16)
POOL_GROUPS = len(POOL_WINDOWS)
POOL_GROUP_DIM = BRANCH_WIDTH // POOL_GROUPS
DN_HEAD_DIM = 128
DN_HEADS = BRANCH_WIDTH // DN_HEAD_DIM
DN_CONV = 4
DN_CHUNK = 64
SB_HEAD_DIM = 128
SB_HEADS = BRANCH_WIDTH // SB_HEAD_DIM
SB_BLOCK = 128
EPS = 1e-6

IN_SPLITS = (
    BRANCH_WIDTH,
    3 * BRANCH_WIDTH,
    BRANCH_WIDTH,
    DN_HEADS,
    DN_HEADS,
    3 * BRANCH_WIDTH,
    N_BRANCH * D_MODEL,
)
P_IN = sum(IN_SPLITS)

kernel_name = "hybrid_pool_deltanet_stickbreak_macaron"


def rms_norm(x, g):
    xf = x.astype(jnp.float32)
    y = xf * lax.rsqrt(jnp.mean(xf * xf, axis=-1, keepdims=True) + EPS)
    return (y * g.astype(jnp.float32)).astype(x.dtype)


def swiglu(h, w_gate, w_up, w_down):
    return (jax.nn.silu(h @ w_gate) * (h @ w_up)) @ w_down


def pool_mixer(u, w_group, scale):
    b, s, _ = u.shape
    uf = u.astype(jnp.float32)
    csum = jnp.cumsum(uf, axis=1)
    count = jnp.arange(1, s + 1, dtype=jnp.float32)[None, :, None]
    outs = []
    for gi, win in enumerate(POOL_WINDOWS):
        sl = slice(gi * POOL_GROUP_DIM, (gi + 1) * POOL_GROUP_DIM)
        c = csum[..., sl]
        c_lag = jnp.pad(c, ((0, 0), (win, 0), (0, 0)))[:, :s]
        mean = (c - c_lag) / jnp.minimum(count, float(win))
        outs.append(mean - uf[..., sl])
    pooled = jnp.stack(outs, axis=2).astype(u.dtype)
    mixed = jnp.einsum('bsgc,gcd->bsgd', pooled, w_group).reshape(b, s, BRANCH_WIDTH)
    return mixed * scale


def causal_depthwise_conv(x, w):
    k, c = w.shape
    return lax.conv_general_dilated(
        x, w[:, None, :].astype(x.dtype), window_strides=(1,), padding=((k - 1, 0),),
        dimension_numbers=('NWC', 'WIO', 'NWC'), feature_group_count=c)


def gated_deltanet(qkv_in, z, a, b_logit, conv_w, A_log, dt_bias, out_gain):
    f32 = jnp.float32
    bsz, s, _ = qkv_in.shape
    h, d, c = DN_HEADS, DN_HEAD_DIM, DN_CHUNK
    n = s // c
    qkv = jax.nn.silu(causal_depthwise_conv(qkv_in, conv_w)).astype(f32)
    q, k, v = jnp.split(qkv, 3, axis=-1)

    def to_chunks(t):
        return t.reshape(bsz, n, c, h, d).transpose(0, 3, 1, 2, 4)

    q, k, v = to_chunks(q), to_chunks(k), to_chunks(v)
    q = q * lax.rsqrt(jnp.sum(q * q, -1, keepdims=True) + EPS) * (d ** -0.5)
    k = k * lax.rsqrt(jnp.sum(k * k, -1, keepdims=True) + EPS)
    beta = jax.nn.sigmoid(b_logit.astype(f32)).reshape(bsz, n, c, h).transpose(0, 3, 1, 2)
    g = -jnp.exp(A_log.astype(f32)) * jax.nn.softplus(a.astype(f32) + dt_bias.astype(f32))
    g = g.reshape(bsz, n, c, h).transpose(0, 3, 1, 2)
    gc = jnp.cumsum(g, axis=-1)

    idx = jnp.arange(c)
    lower_incl = idx[:, None] >= idx[None, :]
    strict = idx[:, None] > idx[None, :]
    diff = gc[..., :, None] - gc[..., None, :]
    decay = jnp.where(lower_incl, jnp.exp(jnp.where(lower_incl, diff, 0.0)), 0.0)

    kb = k * beta[..., None]
    lmat = jnp.einsum('bhnid,bhnjd->bhnij', kb, k) * jnp.where(strict, decay, 0.0)
    eye = jnp.eye(c, dtype=f32)
    rhs = jnp.concatenate([v * beta[..., None], kb * jnp.exp(gc)[..., None]], axis=-1)
    sol = lax.linalg.triangular_solve(lmat + eye, rhs, left_side=True, lower=True, unit_diagonal=True)
    u, w = sol[..., :d], sol[..., d:]

    attn_qk = jnp.einsum('bhnid,bhnjd->bhnij', q, k) * decay
    q_dec = q * jnp.exp(gc)[..., None]
    k_dec = k * jnp.exp(gc[..., -1:] - gc)[..., None]
    chunk_decay = jnp.exp(gc[..., -1])

    def step(state, xs):
        u_n, w_n, qd_n, kd_n, a_n, cd_n = xs
        v_new = u_n - jnp.einsum('bhcd,bhde->bhce', w_n, state)
        o_n = (jnp.einsum('bhcd,bhde->bhce', qd_n, state)
               + jnp.einsum('bhij,bhje->bhie', a_n, v_new))
        state = state * cd_n[..., None, None] + jnp.einsum('bhcd,bhce->bhde', kd_n, v_new)
        return state, o_n

    xs = tuple(jnp.moveaxis(t, 2, 0) for t in (u, w, q_dec, k_dec, attn_qk, chunk_decay))
    state0 = jnp.zeros((bsz, h, d, d), f32)
    _, o = lax.scan(step, state0, xs)
    o = o.transpose(1, 0, 3, 2, 4).reshape(bsz, s, h, d)
    o = o * lax.rsqrt(jnp.mean(o * o, -1, keepdims=True) + EPS) * out_gain.astype(f32)
    o = o * jax.nn.silu(z.astype(f32)).reshape(bsz, s, h, d)
    return o.reshape(bsz, s, BRANCH_WIDTH).astype(qkv_in.dtype)


def stick_breaking_attention(qkv):
    f32 = jnp.float32
    bsz, s, _ = qkv.shape
    h, d, blk = SB_HEADS, SB_HEAD_DIM, SB_BLOCK
    q, k, v = [t.reshape(bsz, s, h, d).transpose(0, 2, 1, 3) for t in jnp.split(qkv, 3, axis=-1)]
    scale = d ** -0.5
    outs = []
    for i in range(s // blk):
        q0, kl = i * blk, (i + 1) * blk
        qb = q[:, :, q0:kl]
        kb, vb = k[:, :, :kl], v[:, :, :kl]
        logits = jnp.einsum('bhqd,bhkd->bhqk', qb, kb).astype(f32) * scale
        causal = jnp.arange(kl)[None, :] < jnp.arange(q0, kl)[:, None]
        log_not = jnp.where(causal, jax.nn.log_sigmoid(-logits), 0.0)
        tail = lax.cumsum(log_not, axis=3, reverse=True) - log_not
        weights = jnp.where(causal, jnp.exp(jax.nn.log_sigmoid(logits) + tail), 0.0)
        outs.append(jnp.einsum('bhqk,bhkd->bhqd', weights.astype(vb.dtype), vb))
    o = jnp.concatenate(outs, axis=2)
    return o.transpose(0, 2, 1, 3).reshape(bsz, s, BRANCH_WIDTH)


def _fwd_setup_inputs(seed: int = 0) -> dict:
    key = jax.random.key(seed)
    ks = jax.random.split(key, 17)
    L, D, F = DEPTH, D_MODEL, D_FF
    f32 = jnp.float32

    def dense(k, shape, fan_in):
        return jax.random.normal(k, shape, f32) * (fan_in ** -0.5)

    def gain(k, shape):
        return 1.0 + 0.02 * jax.random.normal(k, shape, f32)

    dt = jnp.exp(jax.random.uniform(ks[12], (L, DN_HEADS), f32,
                                    minval=math.log(1e-3), maxval=math.log(1e-1)))
    return {
        "x": jax.random.normal(ks[0], (BATCH, SEQ, D), f32),
        "ffn_norm": gain(ks[1], (L, 2, D)),
        "ffn_w_gate": dense(ks[2], (L, 2, D, F), D),
        "ffn_w_up": dense(ks[3], (L, 2, D, F), D),
        "ffn_w_down": dense(ks[4], (L, 2, F, D), F),
        "mix_norm": gain(ks[5], (L, D)),
        "w_in": dense(ks[6], (L, D, P_IN), D),
        "b_gate": 0.01 * jax.random.normal(ks[7], (L, N_BRANCH * D), f32),
        "pool_w": dense(ks[8], (L, POOL_GROUPS, POOL_GROUP_DIM, POOL_GROUP_DIM), POOL_GROUP_DIM),
        "pool_scale": gain(ks[9], (L, BRANCH_WIDTH)),
        "dn_conv": dense(ks[10], (L, DN_CONV, 3 * BRANCH_WIDTH), DN_CONV),
        "dn_A_log": jnp.log(jax.random.uniform(ks[11], (L, DN_HEADS), f32, minval=1.0, maxval=16.0)),
        "dn_dt_bias": dt + jnp.log(-jnp.expm1(-dt)),
        "dn_out_norm": gain(ks[13], (L, DN_HEAD_DIM)),
        "w_branch": dense(ks[14], (L, N_BRANCH, BRANCH_WIDTH, D), BRANCH_WIDTH),
        "w_out": dense(ks[15], (L, D, D), D),
        "final_norm": gain(ks[16], (D,)),
    }


def _fwd_reference(x, ffn_norm, ffn_w_gate, ffn_w_up, ffn_w_down, mix_norm, w_in, b_gate,
              pool_w, pool_scale, dn_conv, dn_A_log, dn_dt_bias, dn_out_norm,
              w_branch, w_out, final_norm):
    bsz, s, d_model = x.shape
    split_points = [int(p) for p in np.cumsum(IN_SPLITS)[:-1]]
    for l in range(DEPTH):
        hf = rms_norm(x, ffn_norm[l, 0])
        x = x + 0.5 * swiglu(hf, ffn_w_gate[l, 0], ffn_w_up[l, 0], ffn_w_down[l, 0])

        h = rms_norm(x, mix_norm[l])
        proj = h @ w_in[l]
        u_pool, dn_qkv, dn_z, dn_a, dn_b, sb_qkv, gate_logits = jnp.split(proj, split_points, axis=-1)
        y_pool = pool_mixer(u_pool, pool_w[l], pool_scale[l])
        y_dn = gated_deltanet(dn_qkv, dn_z, dn_a, dn_b, dn_conv[l], dn_A_log[l], dn_dt_bias[l], dn_out_norm[l])
        y_sb = stick_breaking_attention(sb_qkv)

        branches = jnp.stack([y_pool, y_dn, y_sb], axis=2)
        branch_d = jnp.einsum('bsnw,nwd->bsnd', branches, w_branch[l])
        gates = jax.nn.sigmoid((gate_logits + b_gate[l]).astype(jnp.float32)).astype(x.dtype)
        gates = gates.reshape(bsz, s, N_BRANCH, d_model)
        merged = jnp.sum(gates * branch_d, axis=2)
        x = x + merged @ w_out[l]

        hf = rms_norm(x, ffn_norm[l, 1])
        x = x + 0.5 * swiglu(hf, ffn_w_gate[l, 1], ffn_w_up[l, 1], ffn_w_down[l, 1])
    return rms_norm(x, final_norm)


import jax as _jax
import jax.numpy as _jnp

TWIN_FORMAT = 'train_step'
FWD_PARAMS = ['x', 'ffn_norm', 'ffn_w_gate', 'ffn_w_up', 'ffn_w_down', 'mix_norm', 'w_in', 'b_gate', 'pool_w', 'pool_scale', 'dn_conv', 'dn_A_log', 'dn_dt_bias', 'dn_out_norm', 'w_branch', 'w_out', 'final_norm']
TWIN_WEIGHTS = ['ffn_norm', 'ffn_w_gate', 'ffn_w_up', 'ffn_w_down', 'mix_norm', 'w_in', 'b_gate', 'pool_w', 'pool_scale', 'dn_conv', 'dn_A_log', 'dn_dt_bias', 'dn_out_norm', 'w_branch', 'w_out', 'final_norm']
TWIN_DIFF_INPUT = 'x'
TWIN_INPUTS = ['x', 'ffn_norm', 'ffn_w_gate', 'ffn_w_up', 'ffn_w_down', 'mix_norm', 'w_in', 'b_gate', 'pool_w', 'pool_scale', 'dn_conv', 'dn_A_log', 'dn_dt_bias', 'dn_out_norm', 'w_branch', 'w_out', 'final_norm', 'loss_target', 'm_ffn_norm', 'm_ffn_w_gate', 'm_ffn_w_up', 'm_ffn_w_down', 'm_mix_norm', 'm_w_in', 'm_b_gate', 'm_pool_w', 'm_pool_scale', 'm_dn_conv', 'm_dn_A_log', 'm_dn_dt_bias', 'm_dn_out_norm', 'm_w_branch', 'm_w_out', 'm_final_norm', 'v_ffn_norm', 'v_ffn_w_gate', 'v_ffn_w_up', 'v_ffn_w_down', 'v_mix_norm', 'v_w_in', 'v_b_gate', 'v_pool_w', 'v_pool_scale', 'v_dn_conv', 'v_dn_A_log', 'v_dn_dt_bias', 'v_dn_out_norm', 'v_w_branch', 'v_w_out', 'v_final_norm']
TWIN_OUTPUTS = ['loss', 'grad_x', 'grad_ffn_norm', 'grad_ffn_w_gate', 'grad_ffn_w_up', 'grad_ffn_w_down', 'grad_mix_norm', 'grad_w_in', 'grad_b_gate', 'grad_pool_w', 'grad_pool_scale', 'grad_dn_conv', 'grad_dn_A_log', 'grad_dn_dt_bias', 'grad_dn_out_norm', 'grad_w_branch', 'grad_w_out', 'grad_final_norm', 'delta_ffn_norm', 'delta_ffn_w_gate', 'delta_ffn_w_up', 'delta_ffn_w_down', 'delta_mix_norm', 'delta_w_in', 'delta_b_gate', 'delta_pool_w', 'delta_pool_scale', 'delta_dn_conv', 'delta_dn_A_log', 'delta_dn_dt_bias', 'delta_dn_out_norm', 'delta_w_branch', 'delta_w_out', 'delta_final_norm', 'new_m_ffn_norm', 'new_m_ffn_w_gate', 'new_m_ffn_w_up', 'new_m_ffn_w_down', 'new_m_mix_norm', 'new_m_w_in', 'new_m_b_gate', 'new_m_pool_w', 'new_m_pool_scale', 'new_m_dn_conv', 'new_m_dn_A_log', 'new_m_dn_dt_bias', 'new_m_dn_out_norm', 'new_m_w_branch', 'new_m_w_out', 'new_m_final_norm', 'new_v_ffn_norm', 'new_v_ffn_w_gate', 'new_v_ffn_w_up', 'new_v_ffn_w_down', 'new_v_mix_norm', 'new_v_w_in', 'new_v_b_gate', 'new_v_pool_w', 'new_v_pool_scale', 'new_v_dn_conv', 'new_v_dn_A_log', 'new_v_dn_dt_bias', 'new_v_dn_out_norm', 'new_v_w_branch', 'new_v_w_out', 'new_v_final_norm']
TWIN_LEAF_KINDS = {'loss': 'loss', 'grad_x': 'grad_x', 'grad_ffn_norm': 'grad_w', 'grad_ffn_w_gate': 'grad_w', 'grad_ffn_w_up': 'grad_w', 'grad_ffn_w_down': 'grad_w', 'grad_mix_norm': 'grad_w', 'grad_w_in': 'grad_w', 'grad_b_gate': 'grad_w', 'grad_pool_w': 'grad_w', 'grad_pool_scale': 'grad_w', 'grad_dn_conv': 'grad_w', 'grad_dn_A_log': 'grad_w', 'grad_dn_dt_bias': 'grad_w', 'grad_dn_out_norm': 'grad_w', 'grad_w_branch': 'grad_w', 'grad_w_out': 'grad_w', 'grad_final_norm': 'grad_w', 'delta_ffn_norm': 'delta_w', 'delta_ffn_w_gate': 'delta_w', 'delta_ffn_w_up': 'delta_w', 'delta_ffn_w_down': 'delta_w', 'delta_mix_norm': 'delta_w', 'delta_w_in': 'delta_w', 'delta_b_gate': 'delta_w', 'delta_pool_w': 'delta_w', 'delta_pool_scale': 'delta_w', 'delta_dn_conv': 'delta_w', 'delta_dn_A_log': 'delta_w', 'delta_dn_dt_bias': 'delta_w', 'delta_dn_out_norm': 'delta_w', 'delta_w_branch': 'delta_w', 'delta_w_out': 'delta_w', 'delta_final_norm': 'delta_w', 'new_m_ffn_norm': 'new_m', 'new_m_ffn_w_gate': 'new_m', 'new_m_ffn_w_up': 'new_m', 'new_m_ffn_w_down': 'new_m', 'new_m_mix_norm': 'new_m', 'new_m_w_in': 'new_m', 'new_m_b_gate': 'new_m', 'new_m_pool_w': 'new_m', 'new_m_pool_scale': 'new_m', 'new_m_dn_conv': 'new_m', 'new_m_dn_A_log': 'new_m', 'new_m_dn_dt_bias': 'new_m', 'new_m_dn_out_norm': 'new_m', 'new_m_w_branch': 'new_m', 'new_m_w_out': 'new_m', 'new_m_final_norm': 'new_m', 'new_v_ffn_norm': 'new_v', 'new_v_ffn_w_gate': 'new_v', 'new_v_ffn_w_up': 'new_v', 'new_v_ffn_w_down': 'new_v', 'new_v_mix_norm': 'new_v', 'new_v_w_in': 'new_v', 'new_v_b_gate': 'new_v', 'new_v_pool_w': 'new_v', 'new_v_pool_scale': 'new_v', 'new_v_dn_conv': 'new_v', 'new_v_dn_A_log': 'new_v', 'new_v_dn_dt_bias': 'new_v', 'new_v_dn_out_norm': 'new_v', 'new_v_w_branch': 'new_v', 'new_v_w_out': 'new_v', 'new_v_final_norm': 'new_v'}


def _forward(args):
    return _fwd_reference(*[args[k] for k in FWD_PARAMS])


def _output_shape():
    out = _jax.eval_shape(lambda: _forward(_fwd_setup_inputs(0)))
    return out.shape, out.dtype

N_MICROBATCH = 1
ADAM_LR = 0.001
ADAM_B1 = 0.9
ADAM_B2 = 0.999
ADAM_EPS = 1e-08
ADAM_WD = 0.01
ADAM_STEP = 10
PER_EXAMPLE_BATCH_AXIS = {'x': 0, 'loss_target': 0}
SHARED_INPUTS = []
_WEIGHT_DTYPES = {'ffn_norm': _jnp.float32, 'ffn_w_gate': _jnp.float32, 'ffn_w_up': _jnp.float32, 'ffn_w_down': _jnp.float32, 'mix_norm': _jnp.float32, 'w_in': _jnp.float32, 'b_gate': _jnp.float32, 'pool_w': _jnp.float32, 'pool_scale': _jnp.float32, 'dn_conv': _jnp.float32, 'dn_A_log': _jnp.float32, 'dn_dt_bias': _jnp.float32, 'dn_out_norm': _jnp.float32, 'w_branch': _jnp.float32, 'w_out': _jnp.float32, 'final_norm': _jnp.float32}
MOMENT_SCALE = {'ffn_norm': 6.794959e-02, 'ffn_w_gate': 2.921335e-02, 'ffn_w_up': 2.828033e-02, 'ffn_w_down': 4.690673e-02, 'mix_norm': 1.230439e-01, 'w_in': 4.658679e-02, 'b_gate': 2.198906e-02, 'pool_w': 9.625070e-02, 'pool_scale': 9.558816e-02, 'dn_conv': 4.947642e-02, 'dn_A_log': 2.667423e-01, 'dn_dt_bias': 2.622655e-01, 'dn_out_norm': 1.465962e-01, 'w_branch': 5.551338e-02, 'w_out': 9.589919e-02, 'final_norm': 3.203257e+01}


def _to_microbatches(a, axis):
    t = _jnp.moveaxis(a, axis, 0)
    t = t.reshape((N_MICROBATCH, t.shape[0] // N_MICROBATCH) + t.shape[1:])
    return _jnp.moveaxis(t, 1, axis + 1)


def setup_inputs(seed: int = 0) -> dict:
    inp = _fwd_setup_inputs(seed)
    key = _jax.random.fold_in(_jax.random.key(seed), 7919)
    shape, _ = _output_shape()
    out = dict(inp)
    out["loss_target"] = _jax.random.normal(_jax.random.fold_in(key, 0), shape, _jnp.float32)
    for i, name in enumerate(TWIN_WEIGHTS):
        w = inp[name].astype(_jnp.float32)
        if MOMENT_SCALE is None:
            s = _jnp.sqrt(_jnp.mean(_jnp.square(w)) + 1e-30)
        else:
            s = MOMENT_SCALE[name]
        km, kv = _jax.random.split(_jax.random.fold_in(key, i + 1))
        out[name] = w
        out["m_" + name] = s * _jax.random.normal(km, w.shape, _jnp.float32)
        out["v_" + name] = (s * s) * _jax.random.uniform(kv, w.shape, _jnp.float32, 0.5, 1.5)
    if N_MICROBATCH > 1:
        for name, axis in PER_EXAMPLE_BATCH_AXIS.items():
            out[name] = _to_microbatches(out[name], axis)
    return {'x': out['x'], 'ffn_norm': out['ffn_norm'], 'ffn_w_gate': out['ffn_w_gate'], 'ffn_w_up': out['ffn_w_up'], 'ffn_w_down': out['ffn_w_down'], 'mix_norm': out['mix_norm'], 'w_in': out['w_in'], 'b_gate': out['b_gate'], 'pool_w': out['pool_w'], 'pool_scale': out['pool_scale'], 'dn_conv': out['dn_conv'], 'dn_A_log': out['dn_A_log'], 'dn_dt_bias': out['dn_dt_bias'], 'dn_out_norm': out['dn_out_norm'], 'w_branch': out['w_branch'], 'w_out': out['w_out'], 'final_norm': out['final_norm'], 'loss_target': out['loss_target'], 'm_ffn_norm': out['m_ffn_norm'], 'm_ffn_w_gate': out['m_ffn_w_gate'], 'm_ffn_w_up': out['m_ffn_w_up'], 'm_ffn_w_down': out['m_ffn_w_down'], 'm_mix_norm': out['m_mix_norm'], 'm_w_in': out['m_w_in'], 'm_b_gate': out['m_b_gate'], 'm_pool_w': out['m_pool_w'], 'm_pool_scale': out['m_pool_scale'], 'm_dn_conv': out['m_dn_conv'], 'm_dn_A_log': out['m_dn_A_log'], 'm_dn_dt_bias': out['m_dn_dt_bias'], 'm_dn_out_norm': out['m_dn_out_norm'], 'm_w_branch': out['m_w_branch'], 'm_w_out': out['m_w_out'], 'm_final_norm': out['m_final_norm'], 'v_ffn_norm': out['v_ffn_norm'], 'v_ffn_w_gate': out['v_ffn_w_gate'], 'v_ffn_w_up': out['v_ffn_w_up'], 'v_ffn_w_down': out['v_ffn_w_down'], 'v_mix_norm': out['v_mix_norm'], 'v_w_in': out['v_w_in'], 'v_b_gate': out['v_b_gate'], 'v_pool_w': out['v_pool_w'], 'v_pool_scale': out['v_pool_scale'], 'v_dn_conv': out['v_dn_conv'], 'v_dn_A_log': out['v_dn_A_log'], 'v_dn_dt_bias': out['v_dn_dt_bias'], 'v_dn_out_norm': out['v_dn_out_norm'], 'v_w_branch': out['v_w_branch'], 'v_w_out': out['v_w_out'], 'v_final_norm': out['v_final_norm']}


def _loss(weights, diff, rest, loss_target):
    with _jax.named_scope("forward"):
        args = {**rest, TWIN_DIFF_INPUT: diff, **{k: w.astype(_WEIGHT_DTYPES[k]) for k, w in weights.items()}}
        y = _forward(args)
    with _jax.named_scope("loss_head"):
        err = _jnp.square(y.astype(_jnp.float32) - loss_target)
        return 0.5 * _jnp.sum(_jnp.mean(err, axis=-1)) if err.ndim else 0.5 * err


def _adamw(w, g, m, v):
    m = ADAM_B1 * m + (1.0 - ADAM_B1) * g
    v = ADAM_B2 * v + (1.0 - ADAM_B2) * _jnp.square(g)
    m_hat = m / (1.0 - ADAM_B1 ** ADAM_STEP)
    v_hat = v / (1.0 - ADAM_B2 ** ADAM_STEP)
    delta = -ADAM_LR * (m_hat / (_jnp.sqrt(v_hat) + ADAM_EPS) + ADAM_WD * w)
    return delta, m, v


def reference(x, ffn_norm, ffn_w_gate, ffn_w_up, ffn_w_down, mix_norm, w_in, b_gate, pool_w, pool_scale, dn_conv, dn_A_log, dn_dt_bias, dn_out_norm, w_branch, w_out, final_norm, loss_target, m_ffn_norm, m_ffn_w_gate, m_ffn_w_up, m_ffn_w_down, m_mix_norm, m_w_in, m_b_gate, m_pool_w, m_pool_scale, m_dn_conv, m_dn_A_log, m_dn_dt_bias, m_dn_out_norm, m_w_branch, m_w_out, m_final_norm, v_ffn_norm, v_ffn_w_gate, v_ffn_w_up, v_ffn_w_down, v_mix_norm, v_w_in, v_b_gate, v_pool_w, v_pool_scale, v_dn_conv, v_dn_A_log, v_dn_dt_bias, v_dn_out_norm, v_w_branch, v_w_out, v_final_norm):
    given = dict(x=x, ffn_norm=ffn_norm, ffn_w_gate=ffn_w_gate, ffn_w_up=ffn_w_up, ffn_w_down=ffn_w_down, mix_norm=mix_norm, w_in=w_in, b_gate=b_gate, pool_w=pool_w, pool_scale=pool_scale, dn_conv=dn_conv, dn_A_log=dn_A_log, dn_dt_bias=dn_dt_bias, dn_out_norm=dn_out_norm, w_branch=w_branch, w_out=w_out, final_norm=final_norm, loss_target=loss_target, m_ffn_norm=m_ffn_norm, m_ffn_w_gate=m_ffn_w_gate, m_ffn_w_up=m_ffn_w_up, m_ffn_w_down=m_ffn_w_down, m_mix_norm=m_mix_norm, m_w_in=m_w_in, m_b_gate=m_b_gate, m_pool_w=m_pool_w, m_pool_scale=m_pool_scale, m_dn_conv=m_dn_conv, m_dn_A_log=m_dn_A_log, m_dn_dt_bias=m_dn_dt_bias, m_dn_out_norm=m_dn_out_norm, m_w_branch=m_w_branch, m_w_out=m_w_out, m_final_norm=m_final_norm, v_ffn_norm=v_ffn_norm, v_ffn_w_gate=v_ffn_w_gate, v_ffn_w_up=v_ffn_w_up, v_ffn_w_down=v_ffn_w_down, v_mix_norm=v_mix_norm, v_w_in=v_w_in, v_b_gate=v_b_gate, v_pool_w=v_pool_w, v_pool_scale=v_pool_scale, v_dn_conv=v_dn_conv, v_dn_A_log=v_dn_A_log, v_dn_dt_bias=v_dn_dt_bias, v_dn_out_norm=v_dn_out_norm, v_w_branch=v_w_branch, v_w_out=v_w_out, v_final_norm=v_final_norm)
    weights = {n: given[n] for n in TWIN_WEIGHTS}
    shared = {n: given[n] for n in SHARED_INPUTS}
    per_example = {n: given[n] for n in ['x']}
    grad_fn = _jax.value_and_grad(_loss, argnums=(0, 1))

    def one_microbatch(ex, loss_target):
        ex = dict(ex)
        diff = ex.pop(TWIN_DIFF_INPUT)
        return grad_fn(weights, diff, {**shared, **ex}, loss_target)

    if N_MICROBATCH == 1:
        loss, (grad_w, grad_x) = one_microbatch(per_example, given["loss_target"])
    else:
        def body(carry, xs):
            loss_sum, grad_sum = carry
            l_k, (gw_k, gx_k) = one_microbatch(xs[0], xs[1])
            with _jax.named_scope("update"):
                return (loss_sum + l_k, _jax.tree.map(_jnp.add, grad_sum, gw_k)), gx_k

        init = (_jnp.zeros((), _jnp.float32), _jax.tree.map(_jnp.zeros_like, weights))
        (loss, grad_w), grad_x = _jax.lax.scan(body, init, (per_example, given["loss_target"]))
    with _jax.named_scope("update"):
        delta_w, new_m, new_v = {}, {}, {}
        for n in TWIN_WEIGHTS:
            delta_w[n], new_m[n], new_v[n] = _adamw(weights[n], grad_w[n], given["m_" + n], given["v_" + n])
    return (loss, grad_x, *[grad_w[n] for n in TWIN_WEIGHTS], *[delta_w[n] for n in TWIN_WEIGHTS],
            *[new_m[n] for n in TWIN_WEIGHTS], *[new_v[n] for n in TWIN_WEIGHTS])
```

```python
import functools
import math

import jax
import jax.numpy as jnp
from jax import lax
from jax.experimental import pallas as pl
from jax.experimental.pallas import tpu as pltpu

F32, BF16 = jnp.float32, jnp.bfloat16
D_MODEL, D_FF, DEPTH = 1024, 2816, 4
BW = 512
HD = 128
NH = 4
DN_CHUNK = 64
EPS = 1e-6
N_DEV = 8
LANE = 128
CB_POOL, CB_DNQ, CB_DNK, CB_DNV, CB_DNZ, CB_SBQ, CB_SBK, CB_SBV = 0, 4, 8, 12, 16, 20, 24, 28
CB_GATE = 4
P_MAIN = 7168
AB_LO, AB_HI = 2560, 2568
ADAM_LR, ADAM_B1, ADAM_B2, ADAM_EPS, ADAM_WD, ADAM_STEP = 0.001, 0.9, 0.999, 1e-08, 0.01, 10
VMEM_LIMIT = 56 * 1024 * 1024
HIGHEST = lax.Precision.HIGHEST
NT_DIMS = (((1,), (1,)), ((), ()))
TN_DIMS = (((0,), (0,)), ((), ()))
NN_DIMS = (((1,), (0,)), ((), ()))


def _cp(dims=None):
    return pltpu.CompilerParams(dimension_semantics=dims, vmem_limit_bytes=VMEM_LIMIT)


def _pick(n, cands):
    for c in cands:
        if n % c == 0:
            return c
    return n


def _bdot(a, b, dims=NN_DIMS):
    return lax.dot_general(a.astype(BF16), b.astype(BF16), dims, preferred_element_type=F32)


def _hdot(a, b, dims=NN_DIMS):
    return lax.dot_general(a, b, dims, precision=HIGHEST, preferred_element_type=F32)


def _split_dot(x, m01):
    hi = x.astype(BF16)
    lo = (x - hi.astype(F32)).astype(BF16)
    return (lax.dot_general(hi, m01, NN_DIMS, preferred_element_type=F32)
            + lax.dot_general(lo, m01, NN_DIMS, preferred_element_type=F32))


def _sigmoid(x):
    return 1.0 / (1.0 + jnp.exp(-x))


def _log_sigmoid(x):
    return jnp.minimum(x, 0.0) - jnp.log1p(jnp.exp(-jnp.abs(x)))


def _softplus(x):
    return jnp.maximum(x, 0.0) + jnp.log1p(jnp.exp(-jnp.abs(x)))


def _shift_down(x, k):
    r = lax.broadcasted_iota(jnp.int32, x.shape, 0)
    return jnp.where(r >= k, pltpu.roll(x, k, 0), 0.0)


def _shift_up(x, k):
    n = x.shape[0]
    r = lax.broadcasted_iota(jnp.int32, x.shape, 0)
    return jnp.where(r < n - k, pltpu.roll(x, n - k, 0), 0.0)


def _mm(a, b, *, ta=False, tb=False, out_dtype=F32, name):
    (kk, m) = a.shape if ta else a.shape[::-1]
    (k2, n) = b.shape[::-1] if tb else b.shape
    assert kk == k2, (a.shape, b.shape, ta, tb)
    bm = _pick(m, (1024, 512, 256, 128))
    bn = _pick(n, (1024, 1408, 512, 256, 128))
    bk = _pick(kk, (512, 256, 128))
    nk = kk // bk
    dims = (((0 if ta else 1,), (1 if tb else 0,)), ((), ()))

    def body(a_ref, b_ref, o_ref, acc_ref):
        k = pl.program_id(2)

        @pl.when(k == 0)
        def _():
            acc_ref[...] = jnp.zeros_like(acc_ref)

        acc_ref[...] += lax.dot_general(a_ref[...].astype(BF16), b_ref[...].astype(BF16), dims,
                                        preferred_element_type=F32)

        @pl.when(k == nk - 1)
        def _():
            o_ref[...] = acc_ref[...].astype(out_dtype)

    a_spec = (pl.BlockSpec((bk, bm), lambda i, j, k: (k, i)) if ta else pl.BlockSpec((bm, bk), lambda i, j, k: (i, k)))
    b_spec = (pl.BlockSpec((bn, bk), lambda i, j, k: (j, k)) if tb else pl.BlockSpec((bk, bn), lambda i, j, k: (k, j)))
    return pl.pallas_call(
        body, grid=(m // bm, n // bn, nk), in_specs=[a_spec, b_spec],
        out_specs=pl.BlockSpec((bm, bn), lambda i, j, k: (i, j)),
        out_shape=jax.ShapeDtypeStruct((m, n), out_dtype),
        scratch_shapes=[pltpu.VMEM((bm, bn), F32)],
        compiler_params=_cp(("parallel", "parallel", "arbitrary")), name=name)(a, b)


def _rms_stats(x):
    rstd = lax.rsqrt(jnp.mean(x * x, axis=-1, keepdims=True) + EPS)
    return x * rstd, rstd


def _rms_bwd_vals(dh, xhat, rstd, g):
    dxh = dh * g
    dx = rstd * (dxh - xhat * jnp.mean(dxh * xhat, axis=-1, keepdims=True))
    return dx, jnp.sum(dh * xhat, axis=0, keepdims=True)


def _rms_fwd(x, g, *, name):
    t, d = x.shape
    tm = _pick(t, (512, 256, 128))

    def body(x_ref, g_ref, h_ref):
        xhat, _ = _rms_stats(x_ref[...])
        h_ref[...] = (xhat * g_ref[...]).astype(BF16)

    return pl.pallas_call(
        body, grid=(t // tm,),
        in_specs=[pl.BlockSpec((tm, d), lambda i: (i, 0)), pl.BlockSpec((1, d), lambda i: (0, 0))],
        out_specs=pl.BlockSpec((tm, d), lambda i: (i, 0)), out_shape=jax.ShapeDtypeStruct((t, d), BF16),
        compiler_params=_cp(("parallel",)), name=name)(x, g.reshape(1, d))


def _rms_bwd(x, g, dh_a, dh_b, dres, *, name):
    t, d = x.shape
    tm = _pick(t, (512, 256, 128))

    def body(x_ref, g_ref, dha_ref, dhb_ref, dres_ref, dx_ref, dg_ref):
        xhat, rstd = _rms_stats(x_ref[...])
        dx, dg = _rms_bwd_vals(dha_ref[...] + dhb_ref[...], xhat, rstd, g_ref[...])
        dx_ref[...] = dres_ref[...] + dx

        @pl.when(pl.program_id(0) == 0)
        def _():
            dg_ref[...] = jnp.zeros_like(dg_ref)

        dg_ref[...] += dg

    row = pl.BlockSpec((tm, d), lambda i: (i, 0))
    vec = pl.BlockSpec((1, d), lambda i: (0, 0))
    return pl.pallas_call(
        body, grid=(t // tm,), in_specs=[row, vec, row, row, row], out_specs=[row, vec],
        out_shape=[jax.ShapeDtypeStruct((t, d), F32), jax.ShapeDtypeStruct((1, d), F32)],
        compiler_params=_cp(("arbitrary",)), name=name)(x, g.reshape(1, d), dh_a, dh_b, dres)


FFN_TM, FFN_FC = 512, 256


def _ffn_fwd(x, g, wg, wu, wd, *, name):
    t, d = x.shape
    f = wg.shape[1]
    tm, fc = _pick(t, (FFN_TM, 256, 128)), _pick(f, (FFN_FC, 128))
    nf = f // fc

    def body(x_ref, g_ref, wg_ref, wu_ref, wd_ref, o_ref, h_ref, acc_ref):
        j = pl.program_id(1)

        @pl.when(j == 0)
        def _():
            xhat, _ = _rms_stats(x_ref[...])
            h_ref[...] = (xhat * g_ref[...]).astype(BF16)
            acc_ref[...] = jnp.zeros_like(acc_ref)

        h = h_ref[...]
        a = _bdot(h, wg_ref[...])
        b = _bdot(h, wu_ref[...])
        s = a * _sigmoid(a) * b
        acc_ref[...] += _bdot(s, wd_ref[...])

        @pl.when(j == nf - 1)
        def _():
            o_ref[...] = x_ref[...] + 0.5 * acc_ref[...]

    row = pl.BlockSpec((tm, d), lambda i, j: (i, 0))
    return pl.pallas_call(
        body, grid=(t // tm, nf),
        in_specs=[row, pl.BlockSpec((1, d), lambda i, j: (0, 0)),
                  pl.BlockSpec((d, fc), lambda i, j: (0, j)), pl.BlockSpec((d, fc), lambda i, j: (0, j)),
                  pl.BlockSpec((fc, d), lambda i, j: (j, 0))],
        out_specs=row, out_shape=jax.ShapeDtypeStruct((t, d), F32),
        scratch_shapes=[pltpu.VMEM((tm, d), BF16), pltpu.VMEM((tm, d), F32)],
        compiler_params=_cp(("parallel", "arbitrary")), name=name)(x, g.reshape(1, d), wg, wu, wd)


def _ffn_bwd(x, g, wg, wu, wd, dy, *, name):
    t, d = x.shape
    f = wg.shape[1]
    tm, fc = _pick(t, (FFN_TM, 256, 128)), _pick(f, (FFN_FC, 128))
    nf = f // fc

    def body(x_ref, g_ref, wg_ref, wu_ref, wd_ref, dy_ref,
             dx_ref, dg_ref, h_ref, dyh_ref, da_ref, db_ref, s_ref, acc_ref):
        i, j = pl.program_id(0), pl.program_id(1)

        @pl.when(j == 0)
        def _():
            xhat, _ = _rms_stats(x_ref[...])
            h_ref[...] = (xhat * g_ref[...]).astype(BF16)
            dyh_ref[...] = (0.5 * dy_ref[...]).astype(BF16)
            acc_ref[...] = jnp.zeros_like(acc_ref)

        h = h_ref[...]
        a = _bdot(h, wg_ref[...])
        b = _bdot(h, wu_ref[...])
        sg = _sigmoid(a)
        silu = a * sg
        s_ref[...] = (silu * b).astype(BF16)
        ds = _bdot(dyh_ref[...], wd_ref[...], NT_DIMS)
        da = (ds * b * (sg * (1.0 + a * (1.0 - sg)))).astype(BF16)
        db = (ds * silu).astype(BF16)
        da_ref[...] = da
        db_ref[...] = db
        acc_ref[...] += _bdot(da, wg_ref[...], NT_DIMS) + _bdot(db, wu_ref[...], NT_DIMS)

        @pl.when((i == 0) & (j == 0))
        def _():
            dg_ref[...] = jnp.zeros_like(dg_ref)

        @pl.when(j == nf - 1)
        def _():
            xhat, rstd = _rms_stats(x_ref[...])
            dx, dg = _rms_bwd_vals(acc_ref[...], xhat, rstd, g_ref[...])
            dx_ref[...] = dy_ref[...] + dx
            dg_ref[...] += dg

    row = pl.BlockSpec((tm, d), lambda i, j: (i, 0))
    vec = pl.BlockSpec((1, d), lambda i, j: (0, 0))
    fblk = pl.BlockSpec((tm, fc), lambda i, j: (i, j))
    return pl.pallas_call(
        body, grid=(t // tm, nf),
        in_specs=[row, vec, pl.BlockSpec((d, fc), lambda i, j: (0, j)), pl.BlockSpec((d, fc), lambda i, j: (0, j)),
                  pl.BlockSpec((fc, d), lambda i, j: (j, 0)), row],
        out_specs=[row, vec, row, row, fblk, fblk, fblk],
        out_shape=[jax.ShapeDtypeStruct((t, d), F32), jax.ShapeDtypeStruct((1, d), F32),
                   jax.ShapeDtypeStruct((t, d), BF16), jax.ShapeDtypeStruct((t, d), BF16),
                   jax.ShapeDtypeStruct((t, f), BF16), jax.ShapeDtypeStruct((t, f), BF16),
                   jax.ShapeDtypeStruct((t, f), BF16)],
        scratch_shapes=[pltpu.VMEM((tm, d), F32)],
        compiler_params=_cp(("arbitrary", "arbitrary")), name=name)(x, g.reshape(1, d), wg, wu, wd, dy)


def _pool_core(u, grp):
    s = u.shape[0]
    w2 = u + _shift_down(u, 1)
    w4 = w2 + _shift_down(w2, 2)
    w8 = w4 + _shift_down(w4, 4)
    w16 = w8 + _shift_down(w8, 8)
    wsum = jnp.where(grp == 0, w2, jnp.where(grp == 1, w4, jnp.where(grp == 2, w8, w16)))
    win = jnp.left_shift(2, grp).astype(F32)
    t1 = (lax.broadcasted_iota(jnp.int32, (s, 1), 0) + 1).astype(F32)
    inv = 1.0 / jnp.minimum(t1, win)
    return wsum * inv - u, inv


def _pool_fwd(proj, pool_w, pool_scale, nb, s, *, name):
    def body(u_ref, w_ref, sc_ref, y_ref):
        pooled, _ = _pool_core(u_ref[...], pl.program_id(0))
        y_ref[...] = _bdot(pooled, w_ref[...]) * sc_ref[...]

    return pl.pallas_call(
        body, grid=(NH, nb),
        in_specs=[pl.BlockSpec((s, HD), lambda g, b: (b, CB_POOL + g)),
                  pl.BlockSpec((None, HD, HD), lambda g, b: (g, 0, 0)), pl.BlockSpec((1, HD), lambda g, b: (0, g))],
        out_specs=pl.BlockSpec((s, HD), lambda g, b: (b, g)),
        out_shape=jax.ShapeDtypeStruct((nb * s, BW), F32),
        compiler_params=_cp(("parallel", "parallel")), name=name)(proj, pool_w, pool_scale)


def _pool_bwd(proj, pool_w, pool_scale, dy, nb, s, *, name):
    def body(u_ref, w_ref, sc_ref, dy_ref, du_ref, dw_ref, dsc_ref):
        grp, b = pl.program_id(0), pl.program_id(1)
        pooled, inv = _pool_core(u_ref[...], grp)
        mixed = _bdot(pooled, w_ref[...])
        dy = dy_ref[...]
        dmixed = dy * sc_ref[...]
        dpooled = _bdot(dmixed, w_ref[...], NT_DIMS)
        r = dpooled * inv
        v2 = r + _shift_up(r, 1)
        v4 = v2 + _shift_up(v2, 2)
        v8 = v4 + _shift_up(v4, 4)
        v16 = v8 + _shift_up(v8, 8)
        vsum = jnp.where(grp == 0, v2, jnp.where(grp == 1, v4, jnp.where(grp == 2, v8, v16)))
        du_ref[...] = vsum - dpooled

        @pl.when(b == 0)
        def _():
            dw_ref[...] = jnp.zeros_like(dw_ref)
            dsc_ref[...] = jnp.zeros_like(dsc_ref)

        dw_ref[...] += _bdot(pooled, dmixed, TN_DIMS)
        dsc_ref[...] += jnp.sum(dy * mixed, axis=0, keepdims=True)

    return pl.pallas_call(
        body, grid=(NH, nb),
        in_specs=[pl.BlockSpec((s, HD), lambda g, b: (b, CB_POOL + g)),
                  pl.BlockSpec((None, HD, HD), lambda g, b: (g, 0, 0)), pl.BlockSpec((1, HD), lambda g, b: (0, g)),
                  pl.BlockSpec((s, HD), lambda g, b: (b, g))],
        out_specs=[pl.BlockSpec((s, HD), lambda g, b: (b, g)), pl.BlockSpec((None, HD, HD), lambda g, b: (g, 0, 0)),
                   pl.BlockSpec((1, HD), lambda g, b: (0, g))],
        out_shape=[jax.ShapeDtypeStruct((nb * s, BW), F32), jax.ShapeDtypeStruct((NH, HD, HD), F32),
                   jax.ShapeDtypeStruct((1, BW), F32)],
        compiler_params=_cp(("arbitrary", "arbitrary")), name=name)(proj, pool_w, pool_scale, dy)


SB_BLK = 128


def _sb_block(qb, kb, q0, k0):
    z = _bdot(qb, kb, NT_DIMS) * (HD ** -0.5)
    row = lax.broadcasted_iota(jnp.int32, (SB_BLK, SB_BLK), 0) + q0
    col = lax.broadcasted_iota(jnp.int32, (SB_BLK, SB_BLK), 1) + k0
    causal = col < row
    lsz = _log_sigmoid(z)
    lnm = jnp.where(causal, lsz - z, 0.0)
    return lsz, lnm, causal


def _tri01(lower):
    r = lax.broadcasted_iota(jnp.int32, (SB_BLK, SB_BLK), 0)
    c = lax.broadcasted_iota(jnp.int32, (SB_BLK, SB_BLK), 1)
    return jnp.where((r < c) if lower else (r > c), 1.0, 0.0).astype(BF16)


def _sb_fwd(proj, nb, s, *, name):
    nq = s // SB_BLK

    def body(q_ref, k_ref, v_ref, o_ref):
        after = _tri01(False)

        def qblock(i, _):
            q0 = pl.multiple_of(i * SB_BLK, SB_BLK)
            qb = q_ref[pl.ds(q0, SB_BLK), :]

            def kblock(jj, carry):
                acc, ct = carry
                k0 = pl.multiple_of((i - jj) * SB_BLK, SB_BLK)
                lsz, lnm, causal = _sb_block(qb, k_ref[pl.ds(k0, SB_BLK), :], q0, k0)
                tail = _split_dot(lnm, after) + ct
                w = jnp.where(causal, jnp.exp(lsz + tail), 0.0)
                acc = acc + _bdot(w, v_ref[pl.ds(k0, SB_BLK), :])
                return acc, ct + jnp.sum(lnm, axis=1, keepdims=True)

            acc, _ = lax.fori_loop(0, i + 1, kblock,
                                   (jnp.zeros((SB_BLK, HD), F32), jnp.zeros((SB_BLK, 1), F32)))
            o_ref[pl.ds(q0, SB_BLK), :] = acc
            return 0

        lax.fori_loop(0, nq, qblock, 0)

    def col(cb):
        return pl.BlockSpec((s, HD), lambda b, h: (b, cb + h))

    return pl.pallas_call(
        body, grid=(nb, NH), in_specs=[col(CB_SBQ), col(CB_SBK), col(CB_SBV)],
        out_specs=pl.BlockSpec((s, HD), lambda b, h: (b, h)),
        out_shape=jax.ShapeDtypeStruct((nb * s, BW), F32),
        compiler_params=_cp(("parallel", "parallel")), name=name)(proj, proj, proj)


def _sb_bwd(proj, dy, nb, s, *, name):
    nq = s // SB_BLK
    scale = HD ** -0.5

    def body(q_ref, k_ref, v_ref, do_ref, dq_ref, dk_ref, dv_ref, ct_ref):
        after = _tri01(False)
        before = _tri01(True)
        dk_ref[...] = jnp.zeros_like(dk_ref)
        dv_ref[...] = jnp.zeros_like(dv_ref)

        def qblock(i, _):
            q0 = pl.multiple_of(i * SB_BLK, SB_BLK)
            qb = q_ref[pl.ds(q0, SB_BLK), :]
            dob = do_ref[pl.ds(q0, SB_BLK), :]

            def tails(jj, ct):
                j = i - jj
                k0 = pl.multiple_of(j * SB_BLK, SB_BLK)
                _, lnm, _ = _sb_block(qb, k_ref[pl.ds(k0, SB_BLK), :], q0, k0)
                ct_ref[j] = jnp.broadcast_to(ct, (SB_BLK, LANE))
                return ct + jnp.sum(lnm, axis=1, keepdims=True)

            lax.fori_loop(0, i + 1, tails, jnp.zeros((SB_BLK, 1), F32))

            def kblock(j, carry):
                dq, ce = carry
                k0 = pl.multiple_of(j * SB_BLK, SB_BLK)
                kb = k_ref[pl.ds(k0, SB_BLK), :]
                vb = v_ref[pl.ds(k0, SB_BLK), :]
                lsz, lnm, causal = _sb_block(qb, kb, q0, k0)
                tail = _split_dot(lnm, after) + ct_ref[j]
                w = jnp.where(causal, jnp.exp(lsz + tail), 0.0)
                e = _bdot(dob, vb, NT_DIMS) * w
                epre = _split_dot(e, before) + ce
                sig = jnp.exp(lsz)
                dz = jnp.where(causal, e * (1.0 - sig) - epre * sig, 0.0) * scale
                dq = dq + _bdot(dz, kb)
                dk_ref[pl.ds(k0, SB_BLK), :] += _bdot(dz, qb, TN_DIMS)
                dv_ref[pl.ds(k0, SB_BLK), :] += _bdot(w, dob, TN_DIMS)
                return dq, ce + jnp.sum(e, axis=1, keepdims=True)

            dq, _ = lax.fori_loop(0, i + 1, kblock,
                                  (jnp.zeros((SB_BLK, HD), F32), jnp.zeros((SB_BLK, 1), F32)))
            dq_ref[pl.ds(q0, SB_BLK), :] = dq
            return 0

        lax.fori_loop(0, nq, qblock, 0)

    def col(cb):
        return pl.BlockSpec((s, HD), lambda b, h: (b, cb + h))

    out = pl.BlockSpec((s, HD), lambda b, h: (b, h))
    sds = jax.ShapeDtypeStruct((nb * s, BW), F32)
    return pl.pallas_call(
        body, grid=(nb, NH), in_specs=[col(CB_SBQ), col(CB_SBK), col(CB_SBV), out],
        out_specs=[out, out, out], out_shape=[sds, sds, sds],
        scratch_shapes=[pltpu.VMEM((nq, SB_BLK, LANE), F32)],
        compiler_params=_cp(("parallel", "parallel")), name=name)(proj, proj, proj, dy)


def _make_cdot(dims, dims_da, dims_db, swap_a=False, swap_b=False):
    @jax.custom_vjp
    def f(a, b):
        return _bdot(a, b, dims)

    def fwd(a, b):
        return _bdot(a, b, dims), (a, b)

    def bwd(res, g):
        a, b = res
        da = _bdot(b, g, dims_da) if swap_a else _bdot(g, b, dims_da)
        db = _bdot(g, a, dims_db) if swap_b else _bdot(a, g, dims_db)
        return da, db

    f.defvjp(fwd, bwd)
    return f


_cdot = _make_cdot(NN_DIMS, NT_DIMS, TN_DIMS)
_cdot_nt = _make_cdot(NT_DIMS, NN_DIMS, TN_DIMS, swap_b=True)
_cdot_tn = _make_cdot(TN_DIMS, NT_DIMS, NN_DIMS, swap_a=True)


def _dn_chunk(q, k, v, bb, gb, state):
    c = q.shape[0]
    r = lax.broadcasted_iota(jnp.int32, (c, c), 0)
    cc = lax.broadcasted_iota(jnp.int32, (c, c), 1)
    incl, strict = r >= cc, r > cc
    gc = _hdot(incl.astype(F32), gb)
    gc_row = _hdot(jnp.full((c, HD), 1.0 / HD, F32), gc, NT_DIMS)
    diff = gc[:, :c] - gc_row
    decay = jnp.where(incl, jnp.exp(jnp.where(incl, diff, 0.0)), 0.0)
    kb = k * bb
    lmat = _cdot_nt(kb, k) * jnp.where(strict, decay, 0.0)
    egc = jnp.exp(gc)
    inv = jnp.where(r == cc, 1.0, 0.0) - lmat
    pw = _hdot(lmat, lmat)
    for it in range(int(math.log2(c)) - 1):
        inv = inv + _hdot(inv, pw)
        if it < int(math.log2(c)) - 2:
            pw = _hdot(pw, pw)
    u = _hdot(inv, v * bb)
    w = _hdot(inv, kb * egc)
    attn = _cdot_nt(q, k) * decay
    gl = jnp.sum(gb, axis=0, keepdims=True)
    k_dec = k * jnp.exp(gl - gc)
    v_new = u - _cdot(w, state)
    o = _cdot(q * egc, state) + _cdot(attn, v_new)
    new_state = state * jnp.exp(gl) + _cdot_tn(k_dec, v_new)
    return o, new_state


def _lane_pick(row, idx):
    lane = lax.broadcasted_iota(jnp.int32, row.shape, 1)
    return jnp.sum(jnp.where(lane == idx, row, 0.0), axis=1, keepdims=True)


def _col_pick(x, idx):
    lane = lax.broadcasted_iota(jnp.int32, x.shape, 1)
    return jnp.sum(jnp.where(lane == idx, x, 0.0), axis=1, keepdims=True)


def _conv_silu(x, w):
    xc = (w[3:4, :] * x + w[2:3, :] * _shift_down(x, 1) + w[1:2, :] * _shift_down(x, 2)
          + w[0:1, :] * _shift_down(x, 3))
    return xc * _sigmoid(xc), xc


def _conv_silu_bwd(x, w, xc, dxs, dw_ref):
    sg = _sigmoid(xc)
    dxc = dxs * (sg * (1.0 + xc * (1.0 - sg)))
    dx = (w[3:4, :] * dxc + w[2:3, :] * _shift_up(dxc, 1) + w[1:2, :] * _shift_up(dxc, 2)
          + w[0:1, :] * _shift_up(dxc, 3))
    dw_ref[3:4, :] += jnp.sum(dxc * x, axis=0, keepdims=True)
    dw_ref[2:3, :] += jnp.sum(dxc * _shift_down(x, 1), axis=0, keepdims=True)
    dw_ref[1:2, :] += jnp.sum(dxc * _shift_down(x, 2), axis=0, keepdims=True)
    dw_ref[0:1, :] += jnp.sum(dxc * _shift_down(x, 3), axis=0, keepdims=True)
    return dx


def _dn_prep(qr_ref, kr_ref, vr_ref, ab_ref, cq_ref, ck_ref, cv_ref, par_ref, head):
    qs, qc = _conv_silu(qr_ref[...], cq_ref[...])
    ks, kc = _conv_silu(kr_ref[...], ck_ref[...])
    vs, vc = _conv_silu(vr_ref[...], cv_ref[...])
    rq = lax.rsqrt(jnp.sum(qs * qs, axis=1, keepdims=True) + EPS)
    rk = lax.rsqrt(jnp.sum(ks * ks, axis=1, keepdims=True) + EPS)
    ab = ab_ref[...]
    a_in = _col_pick(ab, head) + _lane_pick(par_ref[1:2, :], head)
    beta = _sigmoid(_col_pick(ab, NH + head))
    neg_ea = -jnp.exp(_lane_pick(par_ref[0:1, :], head))
    g = neg_ea * _softplus(a_in)
    return dict(q=qs * rq * (HD ** -0.5), k=ks * rk, v=vs, beta=beta, g=g, qs=qs, ks=ks, qc=qc, kc=kc, vc=vc,
                rq=rq, rk=rk, a_in=a_in, neg_ea=neg_ea)


def _dn_specs(nb, s):
    def col(cb):
        return pl.BlockSpec((s, HD), lambda h, b: (b, cb + h))

    def conv(cb):
        return pl.BlockSpec((DN_CONV_W, HD), lambda h, b: (0, cb + h))

    return col, conv


DN_CONV_W = 4


def _dn_fwd(proj, ab, conv_w, par, gain, nb, s, *, name):
    nc = s // DN_CHUNK
    col, conv = _dn_specs(nb, s)

    def body(qr_ref, kr_ref, vr_ref, z_ref, ab_ref, cq_ref, ck_ref, cv_ref, par_ref, gain_ref,
             y_ref, o_ref, st_ref, q_s, k_s, v_s, bb_s, gb_s):
        p = _dn_prep(qr_ref, kr_ref, vr_ref, ab_ref, cq_ref, ck_ref, cv_ref, par_ref, pl.program_id(0))
        q_s[...], k_s[...], v_s[...] = p["q"], p["k"], p["v"]
        bb_s[...] = jnp.broadcast_to(p["beta"], (s, HD))
        gb_s[...] = jnp.broadcast_to(p["g"], (s, HD))

        def chunk(ci, state):
            sl = pl.ds(pl.multiple_of(ci * DN_CHUNK, DN_CHUNK), DN_CHUNK)
            st_ref[ci] = state
            o, state = _dn_chunk(q_s[sl, :], k_s[sl, :], v_s[sl, :], bb_s[sl, :], gb_s[sl, :], state)
            o_ref[sl, :] = o
            return state

        lax.fori_loop(0, nc, chunk, jnp.zeros((HD, HD), F32))
        o = o_ref[...]
        z = z_ref[...]
        on = o * lax.rsqrt(jnp.mean(o * o, axis=1, keepdims=True) + EPS) * gain_ref[...]
        y_ref[...] = on * (z * _sigmoid(z))

    out = pl.BlockSpec((s, HD), lambda h, b: (b, h))
    sds = jax.ShapeDtypeStruct((nb * s, BW), F32)
    return pl.pallas_call(
        body, grid=(NH, nb),
        in_specs=[col(CB_DNQ), col(CB_DNK), col(CB_DNV), col(CB_DNZ), pl.BlockSpec((s, LANE), lambda h, b: (b, 0)),
                  conv(0), conv(NH), conv(2 * NH), pl.BlockSpec((8, LANE), lambda h, b: (0, 0)),
                  pl.BlockSpec((1, HD), lambda h, b: (0, 0))],
        out_specs=[out, out, pl.BlockSpec((None, None, nc, HD, HD), lambda h, b: (b, h, 0, 0, 0))],
        out_shape=[sds, sds, jax.ShapeDtypeStruct((nb, NH, nc, HD, HD), F32)],
        scratch_shapes=[pltpu.VMEM((s, HD), F32)] * 5,
        compiler_params=_cp(("parallel", "parallel")), name=name)(
            proj, proj, proj, proj, ab, conv_w, conv_w, conv_w, par, gain)


def _dn_bwd(proj, ab, conv_w, par, gain, o_pre, states, dy, nb, s, *, name):
    nc = s // DN_CHUNK
    col, conv = _dn_specs(nb, s)

    def body(qr_ref, kr_ref, vr_ref, z_ref, ab_ref, cq_ref, ck_ref, cv_ref, par_ref, gain_ref, o_ref, st_ref, dy_ref,
             dqr_ref, dkr_ref, dvr_ref, dz_ref, dab_ref, dcq_ref, dck_ref, dcv_ref, dpar_ref, dgain_ref,
             q_s, k_s, v_s, bb_s, gb_s, do_s):
        head, b = pl.program_id(0), pl.program_id(1)
        p = _dn_prep(qr_ref, kr_ref, vr_ref, ab_ref, cq_ref, ck_ref, cv_ref, par_ref, head)
        q_s[...], k_s[...], v_s[...] = p["q"], p["k"], p["v"]
        bb_s[...] = jnp.broadcast_to(p["beta"], (s, HD))
        gb_s[...] = jnp.broadcast_to(p["g"], (s, HD))

        @pl.when(b == 0)
        def _():
            for ref in (dcq_ref, dck_ref, dcv_ref, dpar_ref):
                ref[...] = jnp.zeros_like(ref)

        @pl.when((b == 0) & (head == 0))
        def _():
            dgain_ref[...] = jnp.zeros_like(dgain_ref)

        o, z, dy = o_ref[...], z_ref[...], dy_ref[...]
        rstd = lax.rsqrt(jnp.mean(o * o, axis=1, keepdims=True) + EPS)
        ohat = o * rstd
        sgz = _sigmoid(z)
        dz_ref[...] = dy * (ohat * gain_ref[...]) * (sgz * (1.0 + z * (1.0 - sgz)))
        don = dy * (z * sgz)
        dgain_ref[...] += jnp.sum(don * ohat, axis=0, keepdims=True)
        dxh = don * gain_ref[...]
        do_s[...] = rstd * (dxh - ohat * jnp.mean(dxh * ohat, axis=1, keepdims=True))

        def chunk(cr, dstate):
            ci = nc - 1 - cr
            sl = pl.ds(pl.multiple_of(ci * DN_CHUNK, DN_CHUNK), DN_CHUNK)
            _, vjp = jax.vjp(_dn_chunk, q_s[sl, :], k_s[sl, :], v_s[sl, :], bb_s[sl, :], gb_s[sl, :], st_ref[ci])
            dq, dk, dv, dbb, dgb, dstate = vjp((do_s[sl, :], dstate))
            q_s[sl, :], k_s[sl, :], v_s[sl, :] = dq, dk, dv
            bb_s[sl, :] = jnp.broadcast_to(jnp.sum(dbb, axis=1, keepdims=True), (DN_CHUNK, HD))
            gb_s[sl, :] = jnp.broadcast_to(jnp.sum(dgb, axis=1, keepdims=True), (DN_CHUNK, HD))
            return dstate

        lax.fori_loop(0, nc, chunk, jnp.zeros((HD, HD), F32))

        dq, dk, dv = q_s[...], k_s[...], v_s[...]
        qs, ks, rq, rk = p["qs"], p["ks"], p["rq"], p["rk"]
        dqs = (HD ** -0.5) * (rq * dq - qs * (rq * rq * rq) * jnp.sum(dq * qs, axis=1, keepdims=True))
        dks = rk * dk - ks * (rk * rk * rk) * jnp.sum(dk * ks, axis=1, keepdims=True)
        dqr_ref[...] = _conv_silu_bwd(qr_ref[...], cq_ref[...], p["qc"], dqs, dcq_ref)
        dkr_ref[...] = _conv_silu_bwd(kr_ref[...], ck_ref[...], p["kc"], dks, dck_ref)
        dvr_ref[...] = _conv_silu_bwd(vr_ref[...], cv_ref[...], p["vc"], dv, dcv_ref)

        dbeta, dg = bb_s[:, 0:1], gb_s[:, 0:1]
        beta = p["beta"]
        db_logit = dbeta * beta * (1.0 - beta)
        da = dg * p["neg_ea"] * _sigmoid(p["a_in"])
        lane = lax.broadcasted_iota(jnp.int32, (s, LANE), 1)
        dab_ref[...] = jnp.where(lane == head, da, 0.0) + jnp.where(lane == NH + head, db_logit, 0.0)
        dpar_ref[0:1, :] += jnp.broadcast_to(jnp.sum(dg * p["g"], axis=0, keepdims=True), (1, LANE))
        dpar_ref[1:2, :] += jnp.broadcast_to(jnp.sum(da, axis=0, keepdims=True), (1, LANE))

    out = pl.BlockSpec((s, HD), lambda h, b: (b, h))
    cblk = pl.BlockSpec((DN_CONV_W, HD), lambda h, b: (0, h))
    sds = jax.ShapeDtypeStruct((nb * s, BW), F32)
    csds = jax.ShapeDtypeStruct((DN_CONV_W, BW), F32)
    return pl.pallas_call(
        body, grid=(NH, nb),
        in_specs=[col(CB_DNQ), col(CB_DNK), col(CB_DNV), col(CB_DNZ), pl.BlockSpec((s, LANE), lambda h, b: (b, 0)),
                  conv(0), conv(NH), conv(2 * NH), pl.BlockSpec((8, LANE), lambda h, b: (0, 0)),
                  pl.BlockSpec((1, HD), lambda h, b: (0, 0)), out,
                  pl.BlockSpec((None, None, nc, HD, HD), lambda h, b: (b, h, 0, 0, 0)), out],
        out_specs=[out, out, out, out, pl.BlockSpec((None, s, LANE), lambda h, b: (h, b, 0)), cblk, cblk, cblk,
                   pl.BlockSpec((None, 8, LANE), lambda h, b: (h, 0, 0)), pl.BlockSpec((1, HD), lambda h, b: (0, 0))],
        out_shape=[sds, sds, sds, sds, jax.ShapeDtypeStruct((NH, nb * s, LANE), F32), csds, csds, csds,
                   jax.ShapeDtypeStruct((NH, 8, LANE), F32), jax.ShapeDtypeStruct((1, HD), F32)],
        scratch_shapes=[pltpu.VMEM((s, HD), F32)] * 6,
        compiler_params=_cp(("arbitrary", "arbitrary")), name=name)(
            proj, proj, proj, proj, ab, conv_w, conv_w, conv_w, par, gain, o_pre, states, dy)


def _sum_heads(x, *, name):
    nh, t, c = x.shape
    tm = _pick(t, (1024, 512, 256, 128))

    def body(x_ref, o_ref):
        o_ref[...] = (x_ref[0] + x_ref[1] + x_ref[2] + x_ref[3]).astype(BF16)

    return pl.pallas_call(
        body, grid=(t // tm,), in_specs=[pl.BlockSpec((nh, tm, c), lambda i: (0, i, 0))],
        out_specs=pl.BlockSpec((tm, c), lambda i: (i, 0)), out_shape=jax.ShapeDtypeStruct((t, c), BF16),
        compiler_params=_cp(("parallel",)), name=name)(x)


MERGE_TM = 256


def _merge_fwd(x, proj, yp, yd, ys, b_gate, wb, wo, *, name):
    t, d = x.shape
    tm = _pick(t, (MERGE_TM, 128))

    def body(x_ref, g0_ref, g1_ref, g2_ref, yp_ref, yd_ref, ys_ref, bg_ref, wb_ref, wo_ref, o_ref):
        merged = jnp.zeros((tm, d), F32)
        for n, (g_ref, y_ref) in enumerate(((g0_ref, yp_ref), (g1_ref, yd_ref), (g2_ref, ys_ref))):
            gate = _sigmoid(g_ref[...] + bg_ref[:, n * d:(n + 1) * d])
            merged = merged + gate * _bdot(y_ref[...], wb_ref[n])
        o_ref[...] = x_ref[...] + _bdot(merged, wo_ref[...])

    row = pl.BlockSpec((tm, d), lambda i: (i, 0))
    yblk = pl.BlockSpec((tm, BW), lambda i: (i, 0))

    def gl(n):
        return pl.BlockSpec((tm, d), lambda i: (i, CB_GATE + n))

    return pl.pallas_call(
        body, grid=(t // tm,),
        in_specs=[row, gl(0), gl(1), gl(2), yblk, yblk, yblk, pl.BlockSpec((1, 3 * d), lambda i: (0, 0)),
                  pl.BlockSpec((3, BW, d), lambda i: (0, 0, 0)), pl.BlockSpec((d, d), lambda i: (0, 0))],
        out_specs=row, out_shape=jax.ShapeDtypeStruct((t, d), F32),
        compiler_params=_cp(("parallel",)), name=name)(x, proj, proj, proj, yp, yd, ys, b_gate, wb, wo)


def _merge_bwd(proj, yp, yd, ys, b_gate, wb, wo, dx, *, name):
    t, d = dx.shape
    tm = _pick(t, (MERGE_TM, 128))

    def body(g0_ref, g1_ref, g2_ref, yp_ref, yd_ref, ys_ref, bg_ref, wb_ref, wo_ref, dx_ref,
             dyp_ref, dyd_ref, dys_ref, dgl_ref, mg_ref, dxh_ref, dbd_ref, dbg_ref):
        dxh = dx_ref[...].astype(BF16)
        dxh_ref[...] = dxh
        dmerged = _bdot(dxh, wo_ref[...], NT_DIMS)
        merged = jnp.zeros((tm, d), F32)

        @pl.when(pl.program_id(0) == 0)
        def _():
            dbg_ref[...] = jnp.zeros_like(dbg_ref)

        for n, (g_ref, y_ref, dy_ref) in enumerate(((g0_ref, yp_ref, dyp_ref), (g1_ref, yd_ref, dyd_ref),
                                                    (g2_ref, ys_ref, dys_ref))):
            gate = _sigmoid(g_ref[...] + bg_ref[:, n * d:(n + 1) * d])
            bd = _bdot(y_ref[...], wb_ref[n])
            merged = merged + gate * bd
            dgl = dmerged * bd * gate * (1.0 - gate)
            dgl_ref[:, n * d:(n + 1) * d] = dgl.astype(BF16)
            dbg_ref[:, n * d:(n + 1) * d] += jnp.sum(dgl, axis=0, keepdims=True)
            dbd = (dmerged * gate).astype(BF16)
            dbd_ref[n] = dbd
            dy_ref[...] = _bdot(dbd, wb_ref[n], NT_DIMS)
        mg_ref[...] = merged.astype(BF16)

    row = pl.BlockSpec((tm, d), lambda i: (i, 0))
    yblk = pl.BlockSpec((tm, BW), lambda i: (i, 0))
    bgv = pl.BlockSpec((1, 3 * d), lambda i: (0, 0))

    def gl(n):
        return pl.BlockSpec((tm, d), lambda i: (i, CB_GATE + n))

    ysds = jax.ShapeDtypeStruct((t, BW), F32)
    return pl.pallas_call(
        body, grid=(t // tm,),
        in_specs=[gl(0), gl(1), gl(2), yblk, yblk, yblk, bgv,
                  pl.BlockSpec((3, BW, d), lambda i: (0, 0, 0)), pl.BlockSpec((d, d), lambda i: (0, 0)), row],
        out_specs=[yblk, yblk, yblk, pl.BlockSpec((tm, 3 * d), lambda i: (i, 0)), row, row,
                   pl.BlockSpec((3, tm, d), lambda i: (0, i, 0)), bgv],
        out_shape=[ysds, ysds, ysds, jax.ShapeDtypeStruct((t, 3 * d), BF16), jax.ShapeDtypeStruct((t, d), BF16),
                   jax.ShapeDtypeStruct((t, d), BF16), jax.ShapeDtypeStruct((3, t, d), BF16),
                   jax.ShapeDtypeStruct((1, 3 * d), F32)],
        compiler_params=_cp(("arbitrary",)), name=name)(proj, proj, proj, yp, yd, ys, b_gate, wb, wo, dx)


def _loss_head(x, g, target, *, name):
    t, d = x.shape
    tm = _pick(t, (512, 256, 128))

    def body(x_ref, g_ref, t_ref, dx_ref, dg_ref, loss_ref):
        xhat, rstd = _rms_stats(x_ref[...])
        err = xhat * g_ref[...] - t_ref[...]
        dx, dg = _rms_bwd_vals(err * (1.0 / d), xhat, rstd, g_ref[...])
        dx_ref[...] = dx

        @pl.when(pl.program_id(0) == 0)
        def _():
            dg_ref[...] = jnp.zeros_like(dg_ref)
            loss_ref[...] = jnp.zeros_like(loss_ref)

        dg_ref[...] += dg
        part = jnp.sum(jnp.sum(err * err, axis=1, keepdims=True), axis=0, keepdims=True) * (0.5 / d)
        loss_ref[...] += jnp.broadcast_to(part, (1, LANE))

    row = pl.BlockSpec((tm, d), lambda i: (i, 0))
    vec = pl.BlockSpec((1, d), lambda i: (0, 0))
    return pl.pallas_call(
        body, grid=(t // tm,), in_specs=[row, vec, row],
        out_specs=[row, vec, pl.BlockSpec((1, LANE), lambda i: (0, 0))],
        out_shape=[jax.ShapeDtypeStruct((t, d), F32), jax.ShapeDtypeStruct((1, d), F32),
                   jax.ShapeDtypeStruct((1, LANE), F32)],
        compiler_params=_cp(("arbitrary",)), name=name)(x, g.reshape(1, d), target)


def _adamw(w, g, m, v, *, name):
    rows, cols = w.shape
    fits = [c for c in (1024, 704, 512, 352, 256, 128, 64, 32, 16, 8) if c * cols * 4 * 14 <= VMEM_LIMIT // 2]
    tr = _pick(rows, fits)
    c1 = 1.0 / (1.0 - ADAM_B1 ** ADAM_STEP)
    c2 = 1.0 / (1.0 - ADAM_B2 ** ADAM_STEP)

    def body(w_ref, g_ref, m_ref, v_ref, d_ref, nm_ref, nv_ref):
        g = g_ref[...]
        nm = ADAM_B1 * m_ref[...] + (1.0 - ADAM_B1) * g
        nv = ADAM_B2 * v_ref[...] + (1.0 - ADAM_B2) * (g * g)
        nm_ref[...] = nm
        nv_ref[...] = nv
        d_ref[...] = -ADAM_LR * ((nm * c1) / (jnp.sqrt(nv * c2) + ADAM_EPS) + ADAM_WD * w_ref[...])

    blk = pl.BlockSpec((tr, cols), lambda i: (i, 0))
    sds = jax.ShapeDtypeStruct((rows, cols), F32)
    return pl.pallas_call(
        body, grid=(rows // tr,), in_specs=[blk] * 4, out_specs=[blk] * 3, out_shape=[sds] * 3,
        compiler_params=_cp(("parallel",)), name=name)(w, g, m, v)


def _peers():
    x, y, c = lax.axis_index("x"), lax.axis_index("y"), lax.axis_index("c")
    flips = [(0, 0, 1), (0, 1, 0), (0, 1, 1), (1, 0, 0), (1, 0, 1), (1, 1, 0), (1, 1, 1)]
    others = [(jnp.where(fx, 1 - x, x), jnp.where(fy, 1 - y, y), jnp.where(fc, 1 - c, c)) for fx, fy, fc in flips]
    return (x, y, c), others


def _slot(p):
    return 4 * p[0] + 2 * p[1] + p[2]


def _exchange(x, *, gather, name):
    blk = x.shape if gather else x.shape[1:]

    def body(x_ref, o_ref, send_sems, recv_sems, local_sem):
        me, others = _peers()
        mine = pltpu.make_async_copy(x_ref if gather else x_ref.at[_slot(me)], o_ref.at[_slot(me)], local_sem)
        mine.start()
        copies = []
        for k, p in enumerate(others):
            cp = pltpu.make_async_remote_copy(
                src_ref=x_ref if gather else x_ref.at[_slot(p)], dst_ref=o_ref.at[_slot(me)],
                send_sem=send_sems.at[k], recv_sem=recv_sems.at[k], device_id=p, device_id_type=pl.DeviceIdType.MESH)
            cp.start()
            copies.append(cp)
        for cp in copies:
            cp.wait_recv()
        for cp in copies:
            cp.wait_send()
        mine.wait()

    return pl.pallas_call(
        body, out_shape=jax.ShapeDtypeStruct((N_DEV,) + tuple(blk), x.dtype),
        in_specs=[pl.BlockSpec(memory_space=pl.ANY)], out_specs=pl.BlockSpec(memory_space=pl.ANY),
        scratch_shapes=[pltpu.SemaphoreType.DMA((N_DEV - 1,)), pltpu.SemaphoreType.DMA((N_DEV - 1,)),
                        pltpu.SemaphoreType.DMA],
        compiler_params=pltpu.CompilerParams(has_side_effects=True), name=name)(x)


def _sum_slots(x, *, name):
    nd, rows, cols = x.shape
    tr = _pick(rows, (512, 256, 128, 64, 32, 16, 8))

    def body(x_ref, o_ref):
        acc = x_ref[0].astype(F32)
        for j in range(1, nd):
            acc = acc + x_ref[j].astype(F32)
        o_ref[...] = acc

    return pl.pallas_call(
        body, grid=(rows // tr,), in_specs=[pl.BlockSpec((nd, tr, cols), lambda i: (0, i, 0))],
        out_specs=pl.BlockSpec((tr, cols), lambda i: (i, 0)), out_shape=jax.ShapeDtypeStruct((rows, cols), F32),
        compiler_params=_cp(("parallel",)), name=name)(x)


def _rows(a, width=1024):
    return a.reshape(-1, width)


def _pad_rows(a, mult=8):
    r = (-a.shape[0]) % mult
    return jnp.pad(a, ((0, r), (0, 0))) if r else a


def _flat128(a):
    f = a.reshape(-1)
    return jnp.pad(f, (0, (-f.shape[0]) % LANE)).reshape(-1, LANE)


def _unshard(gathered, shape, axis):
    g = gathered.reshape((N_DEV,) + tuple(shape))
    g = jnp.moveaxis(g, 0, axis)
    full = list(shape)
    full[axis] *= N_DEV
    return g.reshape(full)


def _to_shards(full, axis):
    shape = list(full.shape)
    shape[axis:axis + 1] = [N_DEV, shape[axis] // N_DEV]
    g = jnp.moveaxis(full.reshape(shape), axis, 0)
    return g.reshape(N_DEV, -1, 1024)


PACK_ROWS = 512
BIG = (("ffn_w_gate", 3), ("ffn_w_up", 3), ("ffn_w_down", 2), ("w_in", 2), ("w_branch", 3), ("w_out", 1))


def kernel(x, ffn_norm, ffn_w_gate, ffn_w_up, ffn_w_down, mix_norm, w_in, b_gate, pool_w, pool_scale, dn_conv, dn_A_log, dn_dt_bias, dn_out_norm, w_branch, w_out, final_norm, loss_target, m_ffn_norm, m_ffn_w_gate, m_ffn_w_up, m_ffn_w_down, m_mix_norm, m_w_in, m_b_gate, m_pool_w, m_pool_scale, m_dn_conv, m_dn_A_log, m_dn_dt_bias, m_dn_out_norm, m_w_branch, m_w_out, m_final_norm, v_ffn_norm, v_ffn_w_gate, v_ffn_w_up, v_ffn_w_down, v_mix_norm, v_w_in, v_b_gate, v_pool_w, v_pool_scale, v_dn_conv, v_dn_A_log, v_dn_dt_bias, v_dn_out_norm, v_w_branch, v_w_out, v_final_norm):
    wts = dict(ffn_norm=ffn_norm, ffn_w_gate=ffn_w_gate, ffn_w_up=ffn_w_up, ffn_w_down=ffn_w_down, mix_norm=mix_norm,
               w_in=w_in, b_gate=b_gate, pool_w=pool_w, pool_scale=pool_scale, dn_conv=dn_conv, dn_A_log=dn_A_log,
               dn_dt_bias=dn_dt_bias, dn_out_norm=dn_out_norm, w_branch=w_branch, w_out=w_out, final_norm=final_norm)
    mom = dict(ffn_norm=m_ffn_norm, ffn_w_gate=m_ffn_w_gate, ffn_w_up=m_ffn_w_up, ffn_w_down=m_ffn_w_down,
               mix_norm=m_mix_norm, w_in=m_w_in, b_gate=m_b_gate, pool_w=m_pool_w, pool_scale=m_pool_scale,
               dn_conv=m_dn_conv, dn_A_log=m_dn_A_log, dn_dt_bias=m_dn_dt_bias, dn_out_norm=m_dn_out_norm,
               w_branch=m_w_branch, w_out=m_w_out, final_norm=m_final_norm)
    var = dict(ffn_norm=v_ffn_norm, ffn_w_gate=v_ffn_w_gate, ffn_w_up=v_ffn_w_up, ffn_w_down=v_ffn_w_down,
               mix_norm=v_mix_norm, w_in=v_w_in, b_gate=v_b_gate, pool_w=v_pool_w, pool_scale=v_pool_scale,
               dn_conv=v_dn_conv, dn_A_log=v_dn_A_log, dn_dt_bias=v_dn_dt_bias, dn_out_norm=v_dn_out_norm,
               w_branch=v_w_branch, w_out=v_w_out, final_norm=v_final_norm)
    nb, s, d = x.shape
    t = nb * s
    me = 4 * lax.axis_index("x") + 2 * lax.axis_index("y") + lax.axis_index("c")

    packed = _pad_rows(jnp.concatenate([_rows(wts[n].astype(BF16)) for n, _ in BIG], axis=0), PACK_ROWS)
    gathered = _exchange(packed, gather=True, name="gather_weights")
    full, off = {}, 0
    for n, axis in BIG:
        r = wts[n].size // 1024
        full[n] = _unshard(gathered[:, off:off + r, :], wts[n].shape, axis)
        off += r
    w_main = jnp.concatenate([full["w_in"][:, :, :AB_LO], full["w_in"][:, :, AB_HI:]], axis=2)
    w_ab = jnp.pad(full["w_in"][:, :, AB_LO:AB_HI], ((0, 0), (0, 0), (0, LANE - (AB_HI - AB_LO))))
    small_sh = _exchange(jnp.concatenate([_flat128(ffn_norm), _flat128(dn_conv)], axis=0), gather=True,
                         name="gather_small")
    nfr = ffn_norm.size // LANE
    ffn_norm_full = _unshard(small_sh[:, :nfr], ffn_norm.shape, 2)
    dn_conv_full = _unshard(small_sh[:, nfr:], dn_conv.shape, 2)
    pool_w_h = pool_w.astype(BF16)

    xs = x.reshape(t, d)
    saved = []
    for l in range(DEPTH):
        sv = dict(x0=xs)
        xs = _ffn_fwd(xs, ffn_norm_full[l, 0], full["ffn_w_gate"][l, 0], full["ffn_w_up"][l, 0],
                      full["ffn_w_down"][l, 0], name="ffn_fwd")
        sv["x1"] = xs
        h = _rms_fwd(xs, mix_norm[l], name="mix_rms")
        proj = _mm(h, w_main[l], name="proj")
        ab = _mm(h, w_ab[l], name="proj_ab")
        par = jnp.pad(jnp.stack([dn_A_log[l], dn_dt_bias[l]]), ((0, 6), (0, LANE - NH)))
        gain = dn_out_norm[l].reshape(1, HD)
        psc = pool_scale[l].reshape(1, BW)
        yp = _pool_fwd(proj, pool_w_h[l], psc, nb, s, name="pool_fwd")
        yd, o_pre, states = _dn_fwd(proj, ab, dn_conv_full[l], par, gain, nb, s, name="dn_fwd")
        ys = _sb_fwd(proj, nb, s, name="sb_fwd")
        bg = b_gate[l].reshape(1, 3 * d)
        xs = _merge_fwd(xs, proj, yp, yd, ys, bg, full["w_branch"][l], full["w_out"][l], name="merge_fwd")
        sv.update(x2=xs, h=h, proj=proj, ab=ab, par=par, gain=gain, psc=psc, yp=yp, yd=yd, ys=ys, o_pre=o_pre,
                  states=states, bg=bg)
        xs = _ffn_fwd(xs, ffn_norm_full[l, 1], full["ffn_w_gate"][l, 1], full["ffn_w_up"][l, 1],
                      full["ffn_w_down"][l, 1], name="ffn_fwd")
        saved.append(sv)

    dx, g_final, loss_row = _loss_head(xs, final_norm, loss_target.reshape(t, d), name="loss_head")
    loss = lax.psum(loss_row[0, 0], ("x", "y", "c"))

    gw = {n: [None] * DEPTH for n in ("ffn_norm", "ffn_w_gate", "ffn_w_up", "ffn_w_down", "mix_norm", "w_in", "b_gate",
                                      "pool_w", "pool_scale", "dn_conv", "dn_A_log", "dn_dt_bias", "dn_out_norm",
                                      "w_branch", "w_out")}

    def ffn_back(l, i, x_in, dy):
        dxi, dg, hb, dyh, da, db, sact = _ffn_bwd(x_in, ffn_norm_full[l, i], full["ffn_w_gate"][l, i],
                                                  full["ffn_w_up"][l, i], full["ffn_w_down"][l, i], dy, name="ffn_bwd")
        return dxi, dg, (_mm(hb, da, ta=True, out_dtype=BF16, name="dw_gate_up"),
                         _mm(hb, db, ta=True, out_dtype=BF16, name="dw_gate_up"),
                         _mm(sact, dyh, ta=True, out_dtype=BF16, name="dw_down"))

    for l in reversed(range(DEPTH)):
        sv = saved[l]
        dx, dg1, (dwg1, dwu1, dwd1) = ffn_back(l, 1, sv["x2"], dx)
        dyp, dyd, dys, dgl, merged, dxh, dbd, dbg = _merge_bwd(sv["proj"], sv["yp"], sv["yd"], sv["ys"], sv["bg"],
                                                               full["w_branch"][l], full["w_out"][l], dx,
                                                               name="merge_bwd")
        gw["w_out"][l] = _mm(merged, dxh, ta=True, out_dtype=BF16, name="dw_out")
        gw["w_branch"][l] = jnp.stack([_mm(y, dbd[n], ta=True, out_dtype=BF16, name="dw_branch")
                                       for n, y in enumerate((sv["yp"], sv["yd"], sv["ys"]))])
        gw["b_gate"][l] = dbg.reshape(3 * d)
        du, dpw, dps = _pool_bwd(sv["proj"], pool_w_h[l], sv["psc"], dyp, nb, s, name="pool_bwd")
        gw["pool_w"][l], gw["pool_scale"][l] = dpw, dps.reshape(BW)
        dqr, dkr, dvr, dz, dab4, dcq, dck, dcv, dpar, dgain = _dn_bwd(
            sv["proj"], sv["ab"], dn_conv_full[l], sv["par"], sv["gain"], sv["o_pre"], sv["states"], dyd, nb, s,
            name="dn_bwd")
        gw["dn_conv"][l] = jnp.concatenate([dcq, dck, dcv], axis=1)
        gw["dn_A_log"][l], gw["dn_dt_bias"][l], gw["dn_out_norm"][l] = dpar[:, 0, 0], dpar[:, 1, 0], dgain.reshape(HD)
        dsq, dsk, dsv = _sb_bwd(sv["proj"], dys, nb, s, name="sb_bwd")
        dab = _sum_heads(dab4, name="sum_heads")
        dproj = jnp.concatenate([du.astype(BF16), dqr.astype(BF16), dkr.astype(BF16), dvr.astype(BF16),
                                 dz.astype(BF16), dsq.astype(BF16), dsk.astype(BF16), dsv.astype(BF16), dgl], axis=1)
        dw_main = _mm(sv["h"], dproj, ta=True, out_dtype=BF16, name="dw_in")
        dw_ab = _mm(sv["h"], dab, ta=True, out_dtype=BF16, name="dw_ab")
        gw["w_in"][l] = jnp.concatenate([dw_main[:, :AB_LO], dw_ab[:, :AB_HI - AB_LO], dw_main[:, AB_LO:]], axis=1)
        dh_main = _mm(dproj, w_main[l], tb=True, name="dh_mix")
        dh_ab = _mm(dab, w_ab[l], tb=True, name="dh_mix_ab")
        dx, dgm = _rms_bwd(sv["x1"], mix_norm[l], dh_main, dh_ab, dx, name="mix_rms_bwd")
        gw["mix_norm"][l] = dgm.reshape(d)
        dx, dg0, (dwg0, dwu0, dwd0) = ffn_back(l, 0, sv["x0"], dx)
        gw["ffn_norm"][l] = jnp.stack([dg0.reshape(d), dg1.reshape(d)])
        gw["ffn_w_gate"][l] = jnp.stack([dwg0, dwg1])
        gw["ffn_w_up"][l] = jnp.stack([dwu0, dwu1])
        gw["ffn_w_down"][l] = jnp.stack([dwd0, dwd1])
    grad_x = dx.reshape(nb, s, d)
    gw = {n: jnp.stack(v) for n, v in gw.items()}
    gw["final_norm"] = g_final.reshape(d)

    g_packed = jnp.concatenate([_to_shards(gw[n], axis) for n, axis in BIG], axis=1)
    g_packed = jnp.pad(g_packed, ((0, 0), (0, (-g_packed.shape[1]) % PACK_ROWS), (0, 0)))
    g_recv = _exchange(g_packed, gather=False, name="scatter_grads")
    g_sum = _sum_slots(g_recv, name="sum_grads")
    grads, off = {}, 0
    for n, _ in BIG:
        r = wts[n].size // 1024
        grads[n] = g_sum[off:off + r].reshape(wts[n].shape)
        off += r

    small = ("ffn_norm", "mix_norm", "b_gate", "pool_w", "pool_scale", "dn_conv", "dn_A_log", "dn_dt_bias",
             "dn_out_norm", "final_norm")
    sp = _pad_rows(jnp.concatenate([_flat128(gw[n]) for n in small], axis=0))
    ssum = _sum_slots(_exchange(sp, gather=True, name="gather_small_grads"), name="sum_small_grads")
    off = 0
    for n in small:
        r = -(-gw[n].size // LANE)
        g = ssum[off:off + r].reshape(-1)[:gw[n].size].reshape(gw[n].shape)
        off += r
        if n in ("ffn_norm", "dn_conv"):
            w = wts[n].shape[2]
            g = lax.dynamic_slice_in_dim(g, me * w, w, axis=2)
        grads[n] = g

    delta, new_m, new_v = {}, {}, {}
    for n, _ in BIG:
        c = wts[n].shape[-1]
        dl, nm, nv = _adamw(wts[n].reshape(-1, c), grads[n].reshape(-1, c), mom[n].reshape(-1, c),
                            var[n].reshape(-1, c), name="adamw_" + n)
        delta[n], new_m[n], new_v[n] = (a.reshape(wts[n].shape) for a in (dl, nm, nv))
    pk = lambda src: _pad_rows(jnp.concatenate([_flat128(src[n]) for n in small], axis=0))
    dl, nm, nv = _adamw(pk(wts), pk(grads), pk(mom), pk(var), name="adamw_small")
    off = 0
    for n in small:
        r = -(-wts[n].size // LANE)
        for dst, src in ((delta, dl), (new_m, nm), (new_v, nv)):
            dst[n] = src[off:off + r].reshape(-1)[:wts[n].size].reshape(wts[n].shape)
        off += r

    order = ("ffn_norm", "ffn_w_gate", "ffn_w_up", "ffn_w_down", "mix_norm", "w_in", "b_gate", "pool_w", "pool_scale",
             "dn_conv", "dn_A_log", "dn_dt_bias", "dn_out_norm", "w_branch", "w_out", "final_norm")
    return (loss, grad_x, *[grads[n] for n in order], *[delta[n] for n in order], *[new_m[n] for n in order],
            *[new_v[n] for n in order])
```

```python
import functools
import math

import jax
import jax.numpy as jnp
from jax import lax
from jax.experimental import pallas as pl
from jax.experimental.pallas import tpu as pltpu

F32, BF16 = jnp.float32, jnp.bfloat16
D_MODEL, D_FF, DEPTH = 1024, 2816, 4
BW = 512
HD = 128
NH = 4
DN_CHUNK = 64
EPS = 1e-6
N_DEV = 8
LANE = 128
CB_POOL, CB_DNQ, CB_DNK, CB_DNV, CB_DNZ, CB_SBQ, CB_SBK, CB_SBV = 0, 4, 8, 12, 16, 20, 24, 28
CB_GATE = 4
P_MAIN = 7168
AB_LO, AB_HI = 2560, 2568
ADAM_LR, ADAM_B1, ADAM_B2, ADAM_EPS, ADAM_WD, ADAM_STEP = 0.001, 0.9, 0.999, 1e-08, 0.01, 10
VMEM_LIMIT = 56 * 1024 * 1024
HIGHEST = lax.Precision.HIGHEST
NT_DIMS = (((1,), (1,)), ((), ()))
TN_DIMS = (((0,), (0,)), ((), ()))
NN_DIMS = (((1,), (0,)), ((), ()))


def _cp(dims=None):
    return pltpu.CompilerParams(dimension_semantics=dims, vmem_limit_bytes=VMEM_LIMIT)


def _pick(n, cands):
    for c in cands:
        if n % c == 0:
            return c
    return n


def _bdot(a, b, dims=NN_DIMS):
    return lax.dot_general(a.astype(BF16), b.astype(BF16), dims, preferred_element_type=F32)


def _hdot(a, b, dims=NN_DIMS):
    return lax.dot_general(a, b, dims, precision=HIGHEST, preferred_element_type=F32)


def _split_dot(x, m01):
    hi = x.astype(BF16)
    lo = (x - hi.astype(F32)).astype(BF16)
    return (lax.dot_general(hi, m01, NN_DIMS, preferred_element_type=F32)
            + lax.dot_general(lo, m01, NN_DIMS, preferred_element_type=F32))


def _sigmoid(x):
    return 1.0 / (1.0 + jnp.exp(-x))


def _log_sigmoid(x):
    return jnp.minimum(x, 0.0) - jnp.log1p(jnp.exp(-jnp.abs(x)))


def _softplus(x):
    return jnp.maximum(x, 0.0) + jnp.log1p(jnp.exp(-jnp.abs(x)))


def _shift_down(x, k):
    r = lax.broadcasted_iota(jnp.int32, x.shape, 0)
    return jnp.where(r >= k, pltpu.roll(x, k, 0), 0.0)


def _shift_up(x, k):
    n = x.shape[0]
    r = lax.broadcasted_iota(jnp.int32, x.shape, 0)
    return jnp.where(r < n - k, pltpu.roll(x, n - k, 0), 0.0)


def _mm(a, b, *, ta=False, tb=False, out_dtype=F32, name):
    (kk, m) = a.shape if ta else a.shape[::-1]
    (k2, n) = b.shape[::-1] if tb else b.shape
    assert kk == k2, (a.shape, b.shape, ta, tb)
    bm = _pick(m, (1024, 512, 256, 128))
    bn = _pick(n, (1024, 1408, 512, 256, 128))
    bk = _pick(kk, (512, 256, 128))
    nk = kk // bk
    dims = (((0 if ta else 1,), (1 if tb else 0,)), ((), ()))

    def body(a_ref, b_ref, o_ref, acc_ref):
        k = pl.program_id(2)

        @pl.when(k == 0)
        def _():
            acc_ref[...] = jnp.zeros_like(acc_ref)

        acc_ref[...] += lax.dot_general(a_ref[...].astype(BF16), b_ref[...].astype(BF16), dims,
                                        preferred_element_type=F32)

        @pl.when(k == nk - 1)
        def _():
            o_ref[...] = acc_ref[...].astype(out_dtype)

    a_spec = (pl.BlockSpec((bk, bm), lambda i, j, k: (k, i)) if ta else pl.BlockSpec((bm, bk), lambda i, j, k: (i, k)))
    b_spec = (pl.BlockSpec((bn, bk), lambda i, j, k: (j, k)) if tb else pl.BlockSpec((bk, bn), lambda i, j, k: (k, j)))
    return pl.pallas_call(
        body, grid=(m // bm, n // bn, nk), in_specs=[a_spec, b_spec],
        out_specs=pl.BlockSpec((bm, bn), lambda i, j, k: (i, j)),
        out_shape=jax.ShapeDtypeStruct((m, n), out_dtype),
        scratch_shapes=[pltpu.VMEM((bm, bn), F32)],
        compiler_params=_cp(("parallel", "parallel", "arbitrary")), name=name)(a, b)


def _mm_tn_slots(a, b, *, name):
    a3, b3 = a.ndim == 3, b.ndim == 3
    ns = a.shape[0] if a3 else b.shape[0]
    t, m = a.shape[-2:]
    n = b.shape[-1]
    bm, bn, bk = _pick(m, (1024, 512, 256, 128)), _pick(n, (1024, 512, 256, 128)), _pick(t, (512, 256, 128))
    nk = t // bk

    def body(a_ref, b_ref, o_ref, acc_ref):
        k = pl.program_id(3)

        @pl.when(k == 0)
        def _():
            acc_ref[...] = jnp.zeros_like(acc_ref)

        acc_ref[...] += _bdot(a_ref[...], b_ref[...], TN_DIMS)

        @pl.when(k == nk - 1)
        def _():
            o_ref[...] = acc_ref[...].astype(BF16)

    a_spec = (pl.BlockSpec((None, bk, bm), lambda s, i, j, k: (s, k, i)) if a3
              else pl.BlockSpec((bk, bm), lambda s, i, j, k: (k, i)))
    b_spec = (pl.BlockSpec((None, bk, bn), lambda s, i, j, k: (s, k, j)) if b3
              else pl.BlockSpec((bk, bn), lambda s, i, j, k: (k, j)))
    return pl.pallas_call(
        body, grid=(ns, m // bm, n // bn, nk), in_specs=[a_spec, b_spec],
        out_specs=pl.BlockSpec((None, bm, bn), lambda s, i, j, k: (s, i, j)),
        out_shape=jax.ShapeDtypeStruct((ns, m, n), BF16), scratch_shapes=[pltpu.VMEM((bm, bn), F32)],
        compiler_params=_cp(("parallel", "parallel", "parallel", "arbitrary")), name=name)(a, b)


def _rms_stats(x):
    rstd = lax.rsqrt(jnp.mean(x * x, axis=-1, keepdims=True) + EPS)
    return x * rstd, rstd


def _rms_bwd_vals(dh, xhat, rstd, g):
    dxh = dh * g
    dx = rstd * (dxh - xhat * jnp.mean(dxh * xhat, axis=-1, keepdims=True))
    return dx, jnp.sum(dh * xhat, axis=0, keepdims=True)


def _rms_fwd(x, g, *, name):
    t, d = x.shape
    tm = _pick(t, (512, 256, 128))

    def body(x_ref, g_ref, h_ref):
        xhat, _ = _rms_stats(x_ref[...])
        h_ref[...] = (xhat * g_ref[...]).astype(BF16)

    return pl.pallas_call(
        body, grid=(t // tm,),
        in_specs=[pl.BlockSpec((tm, d), lambda i: (i, 0)), pl.BlockSpec((1, d), lambda i: (0, 0))],
        out_specs=pl.BlockSpec((tm, d), lambda i: (i, 0)), out_shape=jax.ShapeDtypeStruct((t, d), BF16),
        compiler_params=_cp(("parallel",)), name=name)(x, g.reshape(1, d))


def _rms_bwd(x, g, dh_a, dh_b, dres, *, name):
    t, d = x.shape
    tm = _pick(t, (512, 256, 128))

    def body(x_ref, g_ref, dha_ref, dhb_ref, dres_ref, dx_ref, dg_ref):
        xhat, rstd = _rms_stats(x_ref[...])
        dx, dg = _rms_bwd_vals(dha_ref[...] + dhb_ref[...], xhat, rstd, g_ref[...])
        dx_ref[...] = dres_ref[...] + dx

        @pl.when(pl.program_id(0) == 0)
        def _():
            dg_ref[...] = jnp.zeros_like(dg_ref)

        dg_ref[...] += dg

    row = pl.BlockSpec((tm, d), lambda i: (i, 0))
    vec = pl.BlockSpec((1, d), lambda i: (0, 0))
    return pl.pallas_call(
        body, grid=(t // tm,), in_specs=[row, vec, row, row, row], out_specs=[row, vec],
        out_shape=[jax.ShapeDtypeStruct((t, d), F32), jax.ShapeDtypeStruct((1, d), F32)],
        compiler_params=_cp(("arbitrary",)), name=name)(x, g.reshape(1, d), dh_a, dh_b, dres)


FFN_TM = 512


def _ffn_fwd(x, g, wg, wu, wd, *, name):
    t, d = x.shape
    nf, _, fc = wg.shape
    tm = _pick(t, (FFN_TM, 256, 128))

    def body(x_ref, g_ref, wg_ref, wu_ref, wd_ref, o_ref, h_ref, acc_ref):
        j = pl.program_id(1)

        @pl.when(j == 0)
        def _():
            xhat, _ = _rms_stats(x_ref[...])
            h_ref[...] = (xhat * g_ref[...]).astype(BF16)
            acc_ref[...] = jnp.zeros_like(acc_ref)

        h = h_ref[...]
        a = _bdot(h, wg_ref[...])
        b = _bdot(h, wu_ref[...])
        s = a * _sigmoid(a) * b
        acc_ref[...] += _bdot(s, wd_ref[...])

        @pl.when(j == nf - 1)
        def _():
            o_ref[...] = x_ref[...] + 0.5 * acc_ref[...]

    row = pl.BlockSpec((tm, d), lambda i, j: (i, 0))
    return pl.pallas_call(
        body, grid=(t // tm, nf),
        in_specs=[row, pl.BlockSpec((1, d), lambda i, j: (0, 0)),
                  pl.BlockSpec((None, d, fc), lambda i, j: (j, 0, 0)), pl.BlockSpec((None, d, fc), lambda i, j: (j, 0, 0)),
                  pl.BlockSpec((None, fc, d), lambda i, j: (j, 0, 0))],
        out_specs=row, out_shape=jax.ShapeDtypeStruct((t, d), F32),
        scratch_shapes=[pltpu.VMEM((tm, d), BF16), pltpu.VMEM((tm, d), F32)],
        compiler_params=_cp(("parallel", "arbitrary")), name=name)(x, g.reshape(1, d), wg, wu, wd)


def _ffn_bwd(x, g, wg, wu, wd, dy, *, name):
    t, d = x.shape
    nf, _, fc = wg.shape
    tm = _pick(t, (FFN_TM, 256, 128))

    def body(x_ref, g_ref, wg_ref, wu_ref, wd_ref, dy_ref,
             dx_ref, dg_ref, h_ref, dyh_ref, da_ref, db_ref, s_ref, acc_ref):
        i, j = pl.program_id(0), pl.program_id(1)

        @pl.when(j == 0)
        def _():
            xhat, _ = _rms_stats(x_ref[...])
            h_ref[...] = (xhat * g_ref[...]).astype(BF16)
            dyh_ref[...] = (0.5 * dy_ref[...]).astype(BF16)
            acc_ref[...] = jnp.zeros_like(acc_ref)

        h = h_ref[...]
        a = _bdot(h, wg_ref[...])
        b = _bdot(h, wu_ref[...])
        sg = _sigmoid(a)
        silu = a * sg
        s_ref[...] = (silu * b).astype(BF16)
        ds = _bdot(dyh_ref[...], wd_ref[...], NT_DIMS)
        da = (ds * b * (sg * (1.0 + a * (1.0 - sg)))).astype(BF16)
        db = (ds * silu).astype(BF16)
        da_ref[...] = da
        db_ref[...] = db
        acc_ref[...] += _bdot(da, wg_ref[...], NT_DIMS) + _bdot(db, wu_ref[...], NT_DIMS)

        @pl.when((i == 0) & (j == 0))
        def _():
            dg_ref[...] = jnp.zeros_like(dg_ref)

        @pl.when(j == nf - 1)
        def _():
            xhat, rstd = _rms_stats(x_ref[...])
            dx, dg = _rms_bwd_vals(acc_ref[...], xhat, rstd, g_ref[...])
            dx_ref[...] = dy_ref[...] + dx
            dg_ref[...] += dg

    row = pl.BlockSpec((tm, d), lambda i, j: (i, 0))
    vec = pl.BlockSpec((1, d), lambda i, j: (0, 0))
    fblk = pl.BlockSpec((None, tm, fc), lambda i, j: (j, i, 0))
    return pl.pallas_call(
        body, grid=(t // tm, nf),
        in_specs=[row, vec, pl.BlockSpec((None, d, fc), lambda i, j: (j, 0, 0)),
                  pl.BlockSpec((None, d, fc), lambda i, j: (j, 0, 0)), pl.BlockSpec((None, fc, d), lambda i, j: (j, 0, 0)),
                  row],
        out_specs=[row, vec, row, row, fblk, fblk, fblk],
        out_shape=[jax.ShapeDtypeStruct((t, d), F32), jax.ShapeDtypeStruct((1, d), F32),
                   jax.ShapeDtypeStruct((t, d), BF16), jax.ShapeDtypeStruct((t, d), BF16),
                   jax.ShapeDtypeStruct((nf, t, fc), BF16), jax.ShapeDtypeStruct((nf, t, fc), BF16),
                   jax.ShapeDtypeStruct((nf, t, fc), BF16)],
        scratch_shapes=[pltpu.VMEM((tm, d), F32)],
        compiler_params=_cp(("arbitrary", "arbitrary")), name=name)(x, g.reshape(1, d), wg, wu, wd, dy)


def _pool_core(u, grp):
    s = u.shape[0]
    w2 = u + _shift_down(u, 1)
    w4 = w2 + _shift_down(w2, 2)
    w8 = w4 + _shift_down(w4, 4)
    w16 = w8 + _shift_down(w8, 8)
    wsum = jnp.where(grp == 0, w2, jnp.where(grp == 1, w4, jnp.where(grp == 2, w8, w16)))
    win = jnp.left_shift(2, grp).astype(F32)
    t1 = (lax.broadcasted_iota(jnp.int32, (s, 1), 0) + 1).astype(F32)
    inv = 1.0 / jnp.minimum(t1, win)
    return wsum * inv - u, inv


def _pool_fwd(proj, pool_w, pool_scale, nb, s, *, name):
    def body(u_ref, w_ref, sc_ref, y_ref):
        pooled, _ = _pool_core(u_ref[...], pl.program_id(0))
        y_ref[...] = _bdot(pooled, w_ref[...]) * sc_ref[...]

    return pl.pallas_call(
        body, grid=(NH, nb),
        in_specs=[pl.BlockSpec((s, HD), lambda g, b: (b, CB_POOL + g)),
                  pl.BlockSpec((None, HD, HD), lambda g, b: (g, 0, 0)), pl.BlockSpec((1, HD), lambda g, b: (0, g))],
        out_specs=pl.BlockSpec((s, HD), lambda g, b: (b, g)),
        out_shape=jax.ShapeDtypeStruct((nb * s, BW), F32),
        compiler_params=_cp(("parallel", "parallel")), name=name)(proj, pool_w, pool_scale)


def _pool_bwd(proj, pool_w, pool_scale, dy, nb, s, *, name):
    def body(u_ref, w_ref, sc_ref, dy_ref, du_ref, dw_ref, dsc_ref):
        grp, b = pl.program_id(0), pl.program_id(1)
        pooled, inv = _pool_core(u_ref[...], grp)
        mixed = _bdot(pooled, w_ref[...])
        dy = dy_ref[...]
        dmixed = dy * sc_ref[...]
        dpooled = _bdot(dmixed, w_ref[...], NT_DIMS)
        r = dpooled * inv
        v2 = r + _shift_up(r, 1)
        v4 = v2 + _shift_up(v2, 2)
        v8 = v4 + _shift_up(v4, 4)
        v16 = v8 + _shift_up(v8, 8)
        vsum = jnp.where(grp == 0, v2, jnp.where(grp == 1, v4, jnp.where(grp == 2, v8, v16)))
        du_ref[...] = vsum - dpooled

        @pl.when(b == 0)
        def _():
            dw_ref[...] = jnp.zeros_like(dw_ref)
            dsc_ref[...] = jnp.zeros_like(dsc_ref)

        dw_ref[...] += _bdot(pooled, dmixed, TN_DIMS)
        dsc_ref[...] += jnp.sum(dy * mixed, axis=0, keepdims=True)

    return pl.pallas_call(
        body, grid=(NH, nb),
        in_specs=[pl.BlockSpec((s, HD), lambda g, b: (b, CB_POOL + g)),
                  pl.BlockSpec((None, HD, HD), lambda g, b: (g, 0, 0)), pl.BlockSpec((1, HD), lambda g, b: (0, g)),
                  pl.BlockSpec((s, HD), lambda g, b: (b, g))],
        out_specs=[pl.BlockSpec((s, HD), lambda g, b: (b, g)), pl.BlockSpec((None, HD, HD), lambda g, b: (g, 0, 0)),
                   pl.BlockSpec((1, HD), lambda g, b: (0, g))],
        out_shape=[jax.ShapeDtypeStruct((nb * s, BW), F32), jax.ShapeDtypeStruct((NH, HD, HD), F32),
                   jax.ShapeDtypeStruct((1, BW), F32)],
        compiler_params=_cp(("arbitrary", "arbitrary")), name=name)(proj, pool_w, pool_scale, dy)


SB_BLK = 128


SB_G = 4
SB_KG = SB_G * SB_BLK


def _sb_block(qb, kg, q0, k0):
    z = _bdot(qb, kg, NT_DIMS) * (HD ** -0.5)
    row = lax.broadcasted_iota(jnp.int32, (SB_BLK, SB_KG), 0) + q0
    col = lax.broadcasted_iota(jnp.int32, (SB_BLK, SB_KG), 1) + k0
    causal = col < row
    lsz = _log_sigmoid(z)
    lnm = jnp.where(causal, lsz - z, 0.0)
    return lsz, lnm, causal


def _sub(x, m):
    return x[:, m * SB_BLK:(m + 1) * SB_BLK]


def _sb_tails(lnm, after, ct):
    tails, cts = [None] * SB_G, [None] * SB_G
    for m in reversed(range(SB_G)):
        cts[m] = ct
        tails[m] = _split_dot(_sub(lnm, m), after) + ct
        ct = ct + jnp.sum(_sub(lnm, m), axis=1, keepdims=True)
    return jnp.concatenate(tails, axis=1), cts, ct


def _tri01(lower):
    r = lax.broadcasted_iota(jnp.int32, (SB_BLK, SB_BLK), 0)
    c = lax.broadcasted_iota(jnp.int32, (SB_BLK, SB_BLK), 1)
    return jnp.where((r < c) if lower else (r > c), 1.0, 0.0).astype(BF16)


def _sb_fwd(proj, nb, s, *, name):
    nq = s // SB_BLK

    def body(q_ref, k_ref, v_ref, o_ref):
        after = _tri01(False)

        def qblock(i, _):
            q0 = pl.multiple_of(i * SB_BLK, SB_BLK)
            qb = q_ref[pl.ds(q0, SB_BLK), :]

            def kgroup(jj, carry):
                acc, ct = carry
                k0 = pl.multiple_of((i // SB_G - jj) * SB_KG, SB_KG)
                lsz, lnm, causal = _sb_block(qb, k_ref[pl.ds(k0, SB_KG), :], q0, k0)
                tail, _, ct = _sb_tails(lnm, after, ct)
                w = jnp.where(causal, jnp.exp(lsz + tail), 0.0)
                return acc + _bdot(w, v_ref[pl.ds(k0, SB_KG), :]), ct

            acc, _ = lax.fori_loop(0, i // SB_G + 1, kgroup,
                                   (jnp.zeros((SB_BLK, HD), F32), jnp.zeros((SB_BLK, 1), F32)))
            o_ref[pl.ds(q0, SB_BLK), :] = acc
            return 0

        lax.fori_loop(0, nq, qblock, 0)

    def col(cb):
        return pl.BlockSpec((s, HD), lambda b, h: (b, cb + h))

    return pl.pallas_call(
        body, grid=(nb, NH), in_specs=[col(CB_SBQ), col(CB_SBK), col(CB_SBV)],
        out_specs=pl.BlockSpec((s, HD), lambda b, h: (b, h)),
        out_shape=jax.ShapeDtypeStruct((nb * s, BW), F32),
        compiler_params=_cp(("parallel", "parallel")), name=name)(proj, proj, proj)


def _sb_bwd(proj, dy, nb, s, *, name):
    nq = s // SB_BLK
    scale = HD ** -0.5

    def body(q_ref, k_ref, v_ref, do_ref, dq_ref, dk_ref, dv_ref, ct_ref):
        after = _tri01(False)
        before = _tri01(True)
        dk_ref[...] = jnp.zeros_like(dk_ref)
        dv_ref[...] = jnp.zeros_like(dv_ref)

        def qblock(i, _):
            q0 = pl.multiple_of(i * SB_BLK, SB_BLK)
            qb = q_ref[pl.ds(q0, SB_BLK), :]
            dob = do_ref[pl.ds(q0, SB_BLK), :]

            def tails(jj, ct):
                g = i // SB_G - jj
                k0 = pl.multiple_of(g * SB_KG, SB_KG)
                _, lnm, _ = _sb_block(qb, k_ref[pl.ds(k0, SB_KG), :], q0, k0)
                ct_ref[g] = jnp.broadcast_to(ct, (SB_BLK, LANE))
                return ct + jnp.sum(lnm, axis=1, keepdims=True)

            lax.fori_loop(0, i // SB_G + 1, tails, jnp.zeros((SB_BLK, 1), F32))

            def kgroup(g, carry):
                dq, ce = carry
                k0 = pl.multiple_of(g * SB_KG, SB_KG)
                kg = k_ref[pl.ds(k0, SB_KG), :]
                vg = v_ref[pl.ds(k0, SB_KG), :]
                lsz, lnm, causal = _sb_block(qb, kg, q0, k0)
                tail, _, _ = _sb_tails(lnm, after, ct_ref[g][:, 0:1])
                w = jnp.where(causal, jnp.exp(lsz + tail), 0.0)
                e = _bdot(dob, vg, NT_DIMS) * w
                pres = []
                for m in range(SB_G):
                    pres.append(_split_dot(_sub(e, m), before) + ce)
                    ce = ce + jnp.sum(_sub(e, m), axis=1, keepdims=True)
                sig = jnp.exp(lsz)
                dz = jnp.where(causal, e * (1.0 - sig) - jnp.concatenate(pres, axis=1) * sig, 0.0) * scale
                dk_ref[pl.ds(k0, SB_KG), :] += _bdot(dz, qb, TN_DIMS)
                dv_ref[pl.ds(k0, SB_KG), :] += _bdot(w, dob, TN_DIMS)
                return dq + _bdot(dz, kg), ce

            dq, _ = lax.fori_loop(0, i // SB_G + 1, kgroup,
                                  (jnp.zeros((SB_BLK, HD), F32), jnp.zeros((SB_BLK, 1), F32)))
            dq_ref[pl.ds(q0, SB_BLK), :] = dq
            return 0

        lax.fori_loop(0, nq, qblock, 0)

    def col(cb):
        return pl.BlockSpec((s, HD), lambda b, h: (b, cb + h))

    out = pl.BlockSpec((s, HD), lambda b, h: (b, h))
    sds = jax.ShapeDtypeStruct((nb * s, BW), F32)
    return pl.pallas_call(
        body, grid=(nb, NH), in_specs=[col(CB_SBQ), col(CB_SBK), col(CB_SBV), out],
        out_specs=[out, out, out], out_shape=[sds, sds, sds],
        scratch_shapes=[pltpu.VMEM((nq, SB_BLK, LANE), F32)],
        compiler_params=_cp(("parallel", "parallel")), name=name)(proj, proj, proj, dy)


def _make_cdot(dims, dims_da, dims_db, swap_a=False, swap_b=False):
    @jax.custom_vjp
    def f(a, b):
        return _bdot(a, b, dims)

    def fwd(a, b):
        return _bdot(a, b, dims), (a, b)

    def bwd(res, g):
        a, b = res
        da = _bdot(b, g, dims_da) if swap_a else _bdot(g, b, dims_da)
        db = _bdot(g, a, dims_db) if swap_b else _bdot(a, g, dims_db)
        return da, db

    f.defvjp(fwd, bwd)
    return f


_cdot = _make_cdot(NN_DIMS, NT_DIMS, TN_DIMS)
_cdot_nt = _make_cdot(NT_DIMS, NN_DIMS, TN_DIMS, swap_b=True)
_cdot_tn = _make_cdot(TN_DIMS, NT_DIMS, NN_DIMS, swap_a=True)


def _dn_local(q, k, v, bb, gb):
    c = q.shape[0]
    r = lax.broadcasted_iota(jnp.int32, (c, c), 0)
    cc = lax.broadcasted_iota(jnp.int32, (c, c), 1)
    incl, strict = r >= cc, r > cc
    gc = _hdot(incl.astype(F32), gb)
    gc_row = _hdot(jnp.full((c, HD), 1.0 / HD, F32), gc, NT_DIMS)
    diff = gc[:, :c] - gc_row
    decay = jnp.where(incl, jnp.exp(jnp.where(incl, diff, 0.0)), 0.0)
    kb = k * bb
    lmat = _cdot_nt(kb, k) * jnp.where(strict, decay, 0.0)
    egc = jnp.exp(gc)
    inv = jnp.where(r == cc, 1.0, 0.0) - lmat
    pw = _hdot(lmat, lmat)
    for it in range(int(math.log2(c)) - 1):
        inv = inv + _hdot(inv, pw)
        if it < int(math.log2(c)) - 2:
            pw = _hdot(pw, pw)
    u = _hdot(inv, v * bb)
    w = _hdot(inv, kb * egc)
    attn = _cdot_nt(q, k) * decay
    gl = jnp.sum(gb, axis=0, keepdims=True)
    return u, w, attn, q * egc, k * jnp.exp(gl - gc), jnp.exp(gl)


def _dn_step(u, w, attn, qd, kd, cd, state):
    v_new = u - _cdot(w, state)
    o = _cdot(qd, state) + _cdot(attn, v_new)
    return o, state * cd + _cdot_tn(kd, v_new)


DN_UNROLL = 4
DN_UNROLL_BWD = 2


def _rowrefs(*refs):
    return [(r, "rows") for r in refs]


def _dn_local_pass(fn, nc, unroll, ins, outs):
    def get(ref, kind, ci):
        return ref[pl.ds(pl.multiple_of(ci * DN_CHUNK, DN_CHUNK), DN_CHUNK), :] if kind == "rows" else ref[ci]

    def put(ref, kind, ci, val):
        if kind == "rows":
            ref[pl.ds(pl.multiple_of(ci * DN_CHUNK, DN_CHUNK), DN_CHUNK), :] = val
        else:
            ref[ci] = val

    def step(it, _):
        cis = [it * unroll + j for j in range(unroll)]
        loaded = [[get(ref, kind, ci) for ref, kind in ins] for ci in cis]
        results = [fn(*args) for args in loaded]
        for ci, res in zip(cis, results):
            for (ref, kind), val in zip(outs, res):
                put(ref, kind, ci, val)
        return 0

    lax.fori_loop(0, nc // unroll, step, 0)


def _lane_pick(row, idx):
    lane = lax.broadcasted_iota(jnp.int32, row.shape, 1)
    return jnp.sum(jnp.where(lane == idx, row, 0.0), axis=1, keepdims=True)


def _col_pick(x, idx):
    lane = lax.broadcasted_iota(jnp.int32, x.shape, 1)
    return jnp.sum(jnp.where(lane == idx, x, 0.0), axis=1, keepdims=True)


def _conv_silu(x, w):
    xc = (w[3:4, :] * x + w[2:3, :] * _shift_down(x, 1) + w[1:2, :] * _shift_down(x, 2)
          + w[0:1, :] * _shift_down(x, 3))
    return xc * _sigmoid(xc), xc


def _conv_silu_bwd(x, w, xc, dxs, dw_ref):
    sg = _sigmoid(xc)
    dxc = dxs * (sg * (1.0 + xc * (1.0 - sg)))
    dx = (w[3:4, :] * dxc + w[2:3, :] * _shift_up(dxc, 1) + w[1:2, :] * _shift_up(dxc, 2)
          + w[0:1, :] * _shift_up(dxc, 3))
    dw_ref[3:4, :] += jnp.sum(dxc * x, axis=0, keepdims=True)
    dw_ref[2:3, :] += jnp.sum(dxc * _shift_down(x, 1), axis=0, keepdims=True)
    dw_ref[1:2, :] += jnp.sum(dxc * _shift_down(x, 2), axis=0, keepdims=True)
    dw_ref[0:1, :] += jnp.sum(dxc * _shift_down(x, 3), axis=0, keepdims=True)
    return dx


def _dn_prep(qr_ref, kr_ref, vr_ref, ab_ref, cq_ref, ck_ref, cv_ref, par_ref, head):
    qs, qc = _conv_silu(qr_ref[...], cq_ref[...])
    ks, kc = _conv_silu(kr_ref[...], ck_ref[...])
    vs, vc = _conv_silu(vr_ref[...], cv_ref[...])
    rq = lax.rsqrt(jnp.sum(qs * qs, axis=1, keepdims=True) + EPS)
    rk = lax.rsqrt(jnp.sum(ks * ks, axis=1, keepdims=True) + EPS)
    ab = ab_ref[...]
    a_in = _col_pick(ab, head) + _lane_pick(par_ref[1:2, :], head)
    beta = _sigmoid(_col_pick(ab, NH + head))
    neg_ea = -jnp.exp(_lane_pick(par_ref[0:1, :], head))
    g = neg_ea * _softplus(a_in)
    return dict(q=qs * rq * (HD ** -0.5), k=ks * rk, v=vs, beta=beta, g=g, qs=qs, ks=ks, qc=qc, kc=kc, vc=vc,
                rq=rq, rk=rk, a_in=a_in, neg_ea=neg_ea)


def _dn_specs(nb, s):
    def col(cb):
        return pl.BlockSpec((s, HD), lambda h, b: (b, cb + h))

    def conv(cb):
        return pl.BlockSpec((DN_CONV_W, HD), lambda h, b: (0, cb + h))

    return col, conv


DN_CONV_W = 4


def _dn_fwd(proj, ab, conv_w, par, gain, nb, s, *, name):
    nc = s // DN_CHUNK
    col, conv = _dn_specs(nb, s)

    def body(qr_ref, kr_ref, vr_ref, z_ref, ab_ref, cq_ref, ck_ref, cv_ref, par_ref, gain_ref,
             y_ref, o_ref, st_ref, q_s, k_s, v_s, bb_s, gb_s, at_s, cd_s):
        p = _dn_prep(qr_ref, kr_ref, vr_ref, ab_ref, cq_ref, ck_ref, cv_ref, par_ref, pl.program_id(0))
        q_s[...], k_s[...], v_s[...] = p["q"], p["k"], p["v"]
        bb_s[...] = jnp.broadcast_to(p["beta"], (s, HD))
        gb_s[...] = jnp.broadcast_to(p["g"], (s, HD))
        _dn_local_pass(_dn_local, nc, DN_UNROLL, _rowrefs(q_s, k_s, v_s, bb_s, gb_s),
                       _rowrefs(v_s, bb_s) + [(at_s, "chunk")] + _rowrefs(q_s, k_s) + [(cd_s, "chunk")])

        def chunk(ci, state):
            sl = pl.ds(pl.multiple_of(ci * DN_CHUNK, DN_CHUNK), DN_CHUNK)
            st_ref[ci] = state
            o, state = _dn_step(v_s[sl, :], bb_s[sl, :], at_s[ci], q_s[sl, :], k_s[sl, :], cd_s[ci], state)
            o_ref[sl, :] = o
            return state

        lax.fori_loop(0, nc, chunk, jnp.zeros((HD, HD), F32))
        o = o_ref[...]
        z = z_ref[...]
        on = o * lax.rsqrt(jnp.mean(o * o, axis=1, keepdims=True) + EPS) * gain_ref[...]
        y_ref[...] = on * (z * _sigmoid(z))

    out = pl.BlockSpec((s, HD), lambda h, b: (b, h))
    sds = jax.ShapeDtypeStruct((nb * s, BW), F32)
    return pl.pallas_call(
        body, grid=(NH, nb),
        in_specs=[col(CB_DNQ), col(CB_DNK), col(CB_DNV), col(CB_DNZ), pl.BlockSpec((s, LANE), lambda h, b: (b, 0)),
                  conv(0), conv(NH), conv(2 * NH), pl.BlockSpec((8, LANE), lambda h, b: (0, 0)),
                  pl.BlockSpec((1, HD), lambda h, b: (0, 0))],
        out_specs=[out, out, pl.BlockSpec((None, None, nc, HD, HD), lambda h, b: (b, h, 0, 0, 0))],
        out_shape=[sds, sds, jax.ShapeDtypeStruct((nb, NH, nc, HD, HD), F32)],
        scratch_shapes=[pltpu.VMEM((s, HD), F32)] * 5 + [pltpu.VMEM((nc, DN_CHUNK, DN_CHUNK), F32),
                                                          pltpu.VMEM((nc, 1, HD), F32)],
        compiler_params=_cp(("parallel", "parallel")), name=name)(
            proj, proj, proj, proj, ab, conv_w, conv_w, conv_w, par, gain)


def _dn_bwd(proj, ab, conv_w, par, gain, o_pre, states, dy, nb, s, *, name):
    nc = s // DN_CHUNK
    col, conv = _dn_specs(nb, s)

    def body(qr_ref, kr_ref, vr_ref, z_ref, ab_ref, cq_ref, ck_ref, cv_ref, par_ref, gain_ref, o_ref, st_ref, dy_ref,
             dqr_ref, dkr_ref, dvr_ref, dz_ref, dab_ref, dcq_ref, dck_ref, dcv_ref, dpar_ref, dgain_ref,
             q_s, k_s, v_s, bb_s, gb_s, do_s, u_s, w_s, qd_s, kd_s, at_s, cd_s):
        head, b = pl.program_id(0), pl.program_id(1)
        p = _dn_prep(qr_ref, kr_ref, vr_ref, ab_ref, cq_ref, ck_ref, cv_ref, par_ref, head)
        q_s[...], k_s[...], v_s[...] = p["q"], p["k"], p["v"]
        bb_s[...] = jnp.broadcast_to(p["beta"], (s, HD))
        gb_s[...] = jnp.broadcast_to(p["g"], (s, HD))

        @pl.when(b == 0)
        def _():
            for ref in (dcq_ref, dck_ref, dcv_ref, dpar_ref):
                ref[...] = jnp.zeros_like(ref)

        @pl.when((b == 0) & (head == 0))
        def _():
            dgain_ref[...] = jnp.zeros_like(dgain_ref)

        o, z, dy = o_ref[...], z_ref[...], dy_ref[...]
        rstd = lax.rsqrt(jnp.mean(o * o, axis=1, keepdims=True) + EPS)
        ohat = o * rstd
        sgz = _sigmoid(z)
        dz_ref[...] = dy * (ohat * gain_ref[...]) * (sgz * (1.0 + z * (1.0 - sgz)))
        don = dy * (z * sgz)
        dgain_ref[...] += jnp.sum(don * ohat, axis=0, keepdims=True)
        dxh = don * gain_ref[...]
        do_s[...] = rstd * (dxh - ohat * jnp.mean(dxh * ohat, axis=1, keepdims=True))

        local_refs = _rowrefs(u_s, w_s) + [(at_s, "chunk")] + _rowrefs(qd_s, kd_s) + [(cd_s, "chunk")]
        _dn_local_pass(_dn_local, nc, DN_UNROLL, _rowrefs(q_s, k_s, v_s, bb_s, gb_s), local_refs)

        def chunk(cr, dstate):
            ci = nc - 1 - cr
            sl = pl.ds(pl.multiple_of(ci * DN_CHUNK, DN_CHUNK), DN_CHUNK)
            _, vjp = jax.vjp(_dn_step, u_s[sl, :], w_s[sl, :], at_s[ci], qd_s[sl, :], kd_s[sl, :], cd_s[ci], st_ref[ci])
            du, dw, dat, dqd, dkd, dcd, dstate = vjp((do_s[sl, :], dstate))
            u_s[sl, :], w_s[sl, :], qd_s[sl, :], kd_s[sl, :] = du, dw, dqd, dkd
            at_s[ci] = dat
            cd_s[ci] = dcd
            return dstate

        lax.fori_loop(0, nc, chunk, jnp.zeros((HD, HD), F32))

        def local_bwd(q, k, v, bb, gb, *cts):
            _, vjp = jax.vjp(_dn_local, q, k, v, bb, gb)
            dq, dk, dv, dbb, dgb = vjp(cts)
            return (dq, dk, dv, jnp.broadcast_to(jnp.sum(dbb, axis=1, keepdims=True), (DN_CHUNK, HD)),
                    jnp.broadcast_to(jnp.sum(dgb, axis=1, keepdims=True), (DN_CHUNK, HD)))

        _dn_local_pass(local_bwd, nc, DN_UNROLL_BWD, _rowrefs(q_s, k_s, v_s, bb_s, gb_s) + local_refs,
                       _rowrefs(q_s, k_s, v_s, bb_s, gb_s))

        dq, dk, dv = q_s[...], k_s[...], v_s[...]
        qs, ks, rq, rk = p["qs"], p["ks"], p["rq"], p["rk"]
        dqs = (HD ** -0.5) * (rq * dq - qs * (rq * rq * rq) * jnp.sum(dq * qs, axis=1, keepdims=True))
        dks = rk * dk - ks * (rk * rk * rk) * jnp.sum(dk * ks, axis=1, keepdims=True)
        dqr_ref[...] = _conv_silu_bwd(qr_ref[...], cq_ref[...], p["qc"], dqs, dcq_ref)
        dkr_ref[...] = _conv_silu_bwd(kr_ref[...], ck_ref[...], p["kc"], dks, dck_ref)
        dvr_ref[...] = _conv_silu_bwd(vr_ref[...], cv_ref[...], p["vc"], dv, dcv_ref)

        dbeta, dg = bb_s[:, 0:1], gb_s[:, 0:1]
        beta = p["beta"]
        db_logit = dbeta * beta * (1.0 - beta)
        da = dg * p["neg_ea"] * _sigmoid(p["a_in"])
        lane = lax.broadcasted_iota(jnp.int32, (s, LANE), 1)
        dab_ref[...] = jnp.where(lane == head, da, 0.0) + jnp.where(lane == NH + head, db_logit, 0.0)
        dpar_ref[0:1, :] += jnp.broadcast_to(jnp.sum(dg * p["g"], axis=0, keepdims=True), (1, LANE))
        dpar_ref[1:2, :] += jnp.broadcast_to(jnp.sum(da, axis=0, keepdims=True), (1, LANE))

    out = pl.BlockSpec((s, HD), lambda h, b: (b, h))
    cblk = pl.BlockSpec((DN_CONV_W, HD), lambda h, b: (0, h))
    sds = jax.ShapeDtypeStruct((nb * s, BW), F32)
    csds = jax.ShapeDtypeStruct((DN_CONV_W, BW), F32)
    return pl.pallas_call(
        body, grid=(NH, nb),
        in_specs=[col(CB_DNQ), col(CB_DNK), col(CB_DNV), col(CB_DNZ), pl.BlockSpec((s, LANE), lambda h, b: (b, 0)),
                  conv(0), conv(NH), conv(2 * NH), pl.BlockSpec((8, LANE), lambda h, b: (0, 0)),
                  pl.BlockSpec((1, HD), lambda h, b: (0, 0)), out,
                  pl.BlockSpec((None, None, nc, HD, HD), lambda h, b: (b, h, 0, 0, 0)), out],
        out_specs=[out, out, out, out, pl.BlockSpec((None, s, LANE), lambda h, b: (h, b, 0)), cblk, cblk, cblk,
                   pl.BlockSpec((None, 8, LANE), lambda h, b: (h, 0, 0)), pl.BlockSpec((1, HD), lambda h, b: (0, 0))],
        out_shape=[sds, sds, sds, sds, jax.ShapeDtypeStruct((NH, nb * s, LANE), F32), csds, csds, csds,
                   jax.ShapeDtypeStruct((NH, 8, LANE), F32), jax.ShapeDtypeStruct((1, HD), F32)],
        scratch_shapes=[pltpu.VMEM((s, HD), F32)] * 10 + [pltpu.VMEM((nc, DN_CHUNK, DN_CHUNK), F32),
                                                           pltpu.VMEM((nc, 1, HD), F32)],
        compiler_params=_cp(("arbitrary", "arbitrary")), name=name)(
            proj, proj, proj, proj, ab, conv_w, conv_w, conv_w, par, gain, o_pre, states, dy)


def _sum_heads(x, *, name):
    nh, t, c = x.shape
    tm = _pick(t, (1024, 512, 256, 128))

    def body(x_ref, o_ref):
        o_ref[...] = (x_ref[0] + x_ref[1] + x_ref[2] + x_ref[3]).astype(BF16)

    return pl.pallas_call(
        body, grid=(t // tm,), in_specs=[pl.BlockSpec((nh, tm, c), lambda i: (0, i, 0))],
        out_specs=pl.BlockSpec((tm, c), lambda i: (i, 0)), out_shape=jax.ShapeDtypeStruct((t, c), BF16),
        compiler_params=_cp(("parallel",)), name=name)(x)


MERGE_TM = 256


def _merge_fwd(x, proj, yp, yd, ys, b_gate, wb, wo, *, name):
    t, d = x.shape
    tm = _pick(t, (MERGE_TM, 128))

    def body(x_ref, g0_ref, g1_ref, g2_ref, yp_ref, yd_ref, ys_ref, bg_ref, wb_ref, wo_ref, o_ref):
        merged = jnp.zeros((tm, d), F32)
        for n, (g_ref, y_ref) in enumerate(((g0_ref, yp_ref), (g1_ref, yd_ref), (g2_ref, ys_ref))):
            gate = _sigmoid(g_ref[...] + bg_ref[:, n * d:(n + 1) * d])
            merged = merged + gate * _bdot(y_ref[...], wb_ref[n])
        o_ref[...] = x_ref[...] + _bdot(merged, wo_ref[...])

    row = pl.BlockSpec((tm, d), lambda i: (i, 0))
    yblk = pl.BlockSpec((tm, BW), lambda i: (i, 0))

    def gl(n):
        return pl.BlockSpec((tm, d), lambda i: (i, CB_GATE + n))

    return pl.pallas_call(
        body, grid=(t // tm,),
        in_specs=[row, gl(0), gl(1), gl(2), yblk, yblk, yblk, pl.BlockSpec((1, 3 * d), lambda i: (0, 0)),
                  pl.BlockSpec((3, BW, d), lambda i: (0, 0, 0)), pl.BlockSpec((d, d), lambda i: (0, 0))],
        out_specs=row, out_shape=jax.ShapeDtypeStruct((t, d), F32),
        compiler_params=_cp(("parallel",)), name=name)(x, proj, proj, proj, yp, yd, ys, b_gate, wb, wo)


def _merge_bwd(proj, yp, yd, ys, b_gate, wb, wo, dx, *, name):
    t, d = dx.shape
    tm = _pick(t, (MERGE_TM, 128))

    def body(g0_ref, g1_ref, g2_ref, yp_ref, yd_ref, ys_ref, bg_ref, wb_ref, wo_ref, dx_ref,
             dyp_ref, dyd_ref, dys_ref, dgl_ref, mg_ref, dxh_ref, dbd_ref, dbg_ref):
        dxh = dx_ref[...].astype(BF16)
        dxh_ref[...] = dxh
        dmerged = _bdot(dxh, wo_ref[...], NT_DIMS)
        merged = jnp.zeros((tm, d), F32)

        @pl.when(pl.program_id(0) == 0)
        def _():
            dbg_ref[...] = jnp.zeros_like(dbg_ref)

        for n, (g_ref, y_ref, dy_ref) in enumerate(((g0_ref, yp_ref, dyp_ref), (g1_ref, yd_ref, dyd_ref),
                                                    (g2_ref, ys_ref, dys_ref))):
            gate = _sigmoid(g_ref[...] + bg_ref[:, n * d:(n + 1) * d])
            bd = _bdot(y_ref[...], wb_ref[n])
            merged = merged + gate * bd
            dgl = dmerged * bd * gate * (1.0 - gate)
            dgl_ref[:, n * d:(n + 1) * d] = dgl.astype(BF16)
            dbg_ref[:, n * d:(n + 1) * d] += jnp.sum(dgl, axis=0, keepdims=True)
            dbd = (dmerged * gate).astype(BF16)
            dbd_ref[n] = dbd
            dy_ref[...] = _bdot(dbd, wb_ref[n], NT_DIMS)
        mg_ref[...] = merged.astype(BF16)

    row = pl.BlockSpec((tm, d), lambda i: (i, 0))
    yblk = pl.BlockSpec((tm, BW), lambda i: (i, 0))
    bgv = pl.BlockSpec((1, 3 * d), lambda i: (0, 0))

    def gl(n):
        return pl.BlockSpec((tm, d), lambda i: (i, CB_GATE + n))

    ysds = jax.ShapeDtypeStruct((t, BW), F32)
    return pl.pallas_call(
        body, grid=(t // tm,),
        in_specs=[gl(0), gl(1), gl(2), yblk, yblk, yblk, bgv,
                  pl.BlockSpec((3, BW, d), lambda i: (0, 0, 0)), pl.BlockSpec((d, d), lambda i: (0, 0)), row],
        out_specs=[yblk, yblk, yblk, pl.BlockSpec((tm, 3 * d), lambda i: (i, 0)), row, row,
                   pl.BlockSpec((3, tm, d), lambda i: (0, i, 0)), bgv],
        out_shape=[ysds, ysds, ysds, jax.ShapeDtypeStruct((t, 3 * d), BF16), jax.ShapeDtypeStruct((t, d), BF16),
                   jax.ShapeDtypeStruct((t, d), BF16), jax.ShapeDtypeStruct((3, t, d), BF16),
                   jax.ShapeDtypeStruct((1, 3 * d), F32)],
        compiler_params=_cp(("arbitrary",)), name=name)(proj, proj, proj, yp, yd, ys, b_gate, wb, wo, dx)


def _loss_head(x, g, target, *, name):
    t, d = x.shape
    tm = _pick(t, (512, 256, 128))

    def body(x_ref, g_ref, t_ref, dx_ref, dg_ref, loss_ref):
        xhat, rstd = _rms_stats(x_ref[...])
        err = xhat * g_ref[...] - t_ref[...]
        dx, dg = _rms_bwd_vals(err * (1.0 / d), xhat, rstd, g_ref[...])
        dx_ref[...] = dx

        @pl.when(pl.program_id(0) == 0)
        def _():
            dg_ref[...] = jnp.zeros_like(dg_ref)
            loss_ref[...] = jnp.zeros_like(loss_ref)

        dg_ref[...] += dg
        part = jnp.sum(jnp.sum(err * err, axis=1, keepdims=True), axis=0, keepdims=True) * (0.5 / d)
        loss_ref[...] += jnp.broadcast_to(part, (1, LANE))

    row = pl.BlockSpec((tm, d), lambda i: (i, 0))
    vec = pl.BlockSpec((1, d), lambda i: (0, 0))
    return pl.pallas_call(
        body, grid=(t // tm,), in_specs=[row, vec, row],
        out_specs=[row, vec, pl.BlockSpec((1, LANE), lambda i: (0, 0))],
        out_shape=[jax.ShapeDtypeStruct((t, d), F32), jax.ShapeDtypeStruct((1, d), F32),
                   jax.ShapeDtypeStruct((1, LANE), F32)],
        compiler_params=_cp(("arbitrary",)), name=name)(x, g.reshape(1, d), target)


def _adamw(w, g, m, v, *, name):
    rows, cols = w.shape
    fits = [c for c in (1024, 704, 512, 352, 256, 128, 64, 32, 16, 8) if c * cols * 4 * 14 <= VMEM_LIMIT // 2]
    tr = _pick(rows, fits)
    c1 = 1.0 / (1.0 - ADAM_B1 ** ADAM_STEP)
    c2 = 1.0 / (1.0 - ADAM_B2 ** ADAM_STEP)

    def body(w_ref, g_ref, m_ref, v_ref, d_ref, nm_ref, nv_ref):
        g = g_ref[...]
        nm = ADAM_B1 * m_ref[...] + (1.0 - ADAM_B1) * g
        nv = ADAM_B2 * v_ref[...] + (1.0 - ADAM_B2) * (g * g)
        nm_ref[...] = nm
        nv_ref[...] = nv
        d_ref[...] = -ADAM_LR * ((nm * c1) / (jnp.sqrt(nv * c2) + ADAM_EPS) + ADAM_WD * w_ref[...])

    blk = pl.BlockSpec((tr, cols), lambda i: (i, 0))
    sds = jax.ShapeDtypeStruct((rows, cols), F32)
    return pl.pallas_call(
        body, grid=(rows // tr,), in_specs=[blk] * 4, out_specs=[blk] * 3, out_shape=[sds] * 3,
        compiler_params=_cp(("parallel",)), name=name)(w, g, m, v)


MESH_ID = pl.DeviceIdType.MESH
HBM_SPEC = pl.BlockSpec(memory_space=pl.ANY)
OTHER_CHIPS = ((1, 0), (0, 1), (1, 1))


def _at_slot(ref, nl, slot):
    return ref.at[(slice(None),) * nl + (slot,)]


def _slotted(shape, nl, slots):
    return tuple(shape[:nl]) + (slots,) + tuple(shape[nl:])


def _flip(v, f):
    return 1 - v if f else v


def _comm_call(body, n, out_shapes, n_remote, args, name):
    return pl.pallas_call(
        body, out_shape=out_shapes, in_specs=[HBM_SPEC] * len(args), out_specs=[HBM_SPEC] * len(out_shapes),
        scratch_shapes=[pltpu.SemaphoreType.DMA((n * n_remote,)), pltpu.SemaphoreType.DMA((n * n_remote,)),
                        pltpu.SemaphoreType.DMA((n * 4,))],
        compiler_params=pltpu.CompilerParams(has_side_effects=True), name=name)(*args)


def _gather(xs, nls, *, name):
    n = len(xs)

    def body(*refs):
        x_refs, o_refs, (send_sems, recv_sems, local_sems) = refs[:n], refs[n:2 * n], refs[2 * n:]
        x, y, c = lax.axis_index("x"), lax.axis_index("y"), lax.axis_index("c")
        me, sibling = (x, y, c), (x, y, 1 - c)
        chips = [(_flip(x, fx), _flip(y, fy)) for fx, fy in OTHER_CHIPS]

        def copy(a, k, block, to, src=None):
            dst = _at_slot(o_refs[a], nls[a], 4 * block[0] + 2 * block[1] + block[2])
            return pltpu.make_async_remote_copy(
                src_ref=dst if src is None else src, dst_ref=dst, send_sem=send_sems.at[a * 7 + k],
                recv_sem=recv_sems.at[a * 7 + k], device_id=to, device_id_type=MESH_ID)

        mine = [pltpu.make_async_copy(x_refs[a], _at_slot(o_refs[a], nls[a], 4 * x + 2 * y + c), local_sems.at[a])
                for a in range(n)]
        sent = []
        for a in range(n):
            mine[a].start()
            sent.append(copy(a, 0, me, sibling, src=x_refs[a]))
            sent += [copy(a, 1 + j, me, (*chip, c), src=x_refs[a]) for j, chip in enumerate(chips)]
        for cp in sent:
            cp.start()
        for j, chip in enumerate(chips):
            for a in range(n):
                copy(a, 1 + j, (*chip, c), me).wait_recv()
                passed = copy(a, 4 + j, (*chip, c), sibling)
                passed.start()
                sent.append(passed)
        for a in range(n):
            copy(a, 0, sibling, me).wait_recv()
            for j, chip in enumerate(chips):
                copy(a, 4 + j, (*chip, 1 - c), me).wait_recv()
        for cp in sent:
            cp.wait_send()
        for cp in mine:
            cp.wait()

    outs = [jax.ShapeDtypeStruct(_slotted(v.shape, nl, N_DEV), v.dtype) for v, nl in zip(xs, nls)]
    return _comm_call(body, n, outs, 7, xs, name)


def _scatter_pair(gs, nls, *, name):
    n = len(gs)

    def body(*refs):
        g_refs, own_refs, got_refs = refs[:n], refs[n:2 * n], refs[2 * n:3 * n]
        send_sems, recv_sems, local_sems = refs[3 * n:]
        x, y, c = lax.axis_index("x"), lax.axis_index("y"), lax.axis_index("c")
        local, remote = [], []
        for a in range(n):
            for q in range(4):
                cp = pltpu.make_async_copy(_at_slot(g_refs[a], nls[a], 2 * q + c), _at_slot(own_refs[a], nls[a], q),
                                           local_sems.at[a * 4 + q])
                cp.start()
                local.append(cp)
                rc = pltpu.make_async_remote_copy(
                    src_ref=_at_slot(g_refs[a], nls[a], 2 * q + 1 - c), dst_ref=_at_slot(got_refs[a], nls[a], q),
                    send_sem=send_sems.at[a * 4 + q], recv_sem=recv_sems.at[a * 4 + q], device_id=(x, y, 1 - c),
                    device_id_type=MESH_ID)
                rc.start()
                remote.append(rc)
        for rc in remote:
            rc.wait_recv()
        for rc in remote:
            rc.wait_send()
        for cp in local:
            cp.wait()

    outs = [jax.ShapeDtypeStruct(_slotted(v.shape[:nl] + v.shape[nl + 1:], nl, 4), v.dtype) for v, nl in zip(gs, nls)]
    res = _comm_call(body, n, outs + outs, 4, gs, name)
    return res[:n], res[n:]


def _scatter_chips(ps, nls, *, name):
    n = len(ps)

    def body(*refs):
        p_refs, r_refs, (send_sems, recv_sems, local_sems) = refs[:n], refs[n:2 * n], refs[2 * n:]
        x, y, c = lax.axis_index("x"), lax.axis_index("y"), lax.axis_index("c")
        my_chip = 2 * x + y
        local, remote = [], []
        for a in range(n):
            cp = pltpu.make_async_copy(_at_slot(p_refs[a], nls[a], my_chip), _at_slot(r_refs[a], nls[a], my_chip),
                                       local_sems.at[a])
            cp.start()
            local.append(cp)
            for k, (fx, fy) in enumerate(OTHER_CHIPS):
                tx, ty = _flip(x, fx), _flip(y, fy)
                rc = pltpu.make_async_remote_copy(
                    src_ref=_at_slot(p_refs[a], nls[a], 2 * tx + ty), dst_ref=_at_slot(r_refs[a], nls[a], my_chip),
                    send_sem=send_sems.at[a * 3 + k], recv_sem=recv_sems.at[a * 3 + k], device_id=(tx, ty, c),
                    device_id_type=MESH_ID)
                rc.start()
                remote.append(rc)
        for rc in remote:
            rc.wait_recv()
        for rc in remote:
            rc.wait_send()
        for cp in local:
            cp.wait()

    outs = [jax.ShapeDtypeStruct(v.shape, v.dtype) for v in ps]
    return _comm_call(body, n, outs, 3, ps, name)


def _add_bf16(a, b, *, name):
    shape = a.shape
    a2, b2 = a.reshape(-1, shape[-1]), b.reshape(-1, shape[-1])
    rows, cols = a2.shape
    tr = _pick(rows, (1024, 704, 512, 352, 256, 128, 64, 32, 16))

    def body(a_ref, b_ref, o_ref):
        o_ref[...] = (a_ref[...].astype(F32) + b_ref[...].astype(F32)).astype(BF16)

    blk = pl.BlockSpec((tr, cols), lambda i: (i, 0))
    out = pl.pallas_call(body, grid=(rows // tr,), in_specs=[blk, blk], out_specs=blk,
                         out_shape=jax.ShapeDtypeStruct((rows, cols), BF16), compiler_params=_cp(("parallel",)),
                         name=name)(a2, b2)
    return out.reshape(shape)


def _sum_adamw(r, w, m, v, nl, *, name):
    shape = w.shape
    rows, cols = shape[-2:]
    lf = math.prod(shape[:-2])
    r4 = r.reshape(lf, 4, rows, cols)
    fits = [c for c in (1024, 512, 352, 256, 128, 64, 32, 16) if c * cols * (7 * 4 + 4 * 2) * 2 <= VMEM_LIMIT // 2]
    tr = _pick(rows, fits)
    c1 = 1.0 / (1.0 - ADAM_B1 ** ADAM_STEP)
    c2 = 1.0 / (1.0 - ADAM_B2 ** ADAM_STEP)

    def body(r_ref, w_ref, m_ref, v_ref, g_ref, d_ref, nm_ref, nv_ref):
        g = r_ref[0].astype(F32)
        for q in range(1, 4):
            g = g + r_ref[q].astype(F32)
        g_ref[...] = g
        nm = ADAM_B1 * m_ref[...] + (1.0 - ADAM_B1) * g
        nv = ADAM_B2 * v_ref[...] + (1.0 - ADAM_B2) * (g * g)
        nm_ref[...] = nm
        nv_ref[...] = nv
        d_ref[...] = -ADAM_LR * ((nm * c1) / (jnp.sqrt(nv * c2) + ADAM_EPS) + ADAM_WD * w_ref[...])

    blk = pl.BlockSpec((None, tr, cols), lambda i, j: (i, j, 0))
    sds = jax.ShapeDtypeStruct((lf, rows, cols), F32)
    outs = pl.pallas_call(
        body, grid=(lf, rows // tr), in_specs=[pl.BlockSpec((None, 4, tr, cols), lambda i, j: (i, 0, j, 0))] + [blk] * 3,
        out_specs=[blk] * 4, out_shape=[sds] * 4, compiler_params=_cp(("parallel", "parallel")), name=name)(
            r4, w.reshape(lf, rows, cols), m.reshape(lf, rows, cols), v.reshape(lf, rows, cols))
    return [o.reshape(shape) for o in outs]


def _sum_slots(x, *, name):
    nd, rows, cols = x.shape
    tr = _pick(rows, (512, 256, 128, 64, 32, 16, 8))

    def body(x_ref, o_ref):
        acc = x_ref[0].astype(F32)
        for j in range(1, nd):
            acc = acc + x_ref[j].astype(F32)
        o_ref[...] = acc

    return pl.pallas_call(
        body, grid=(rows // tr,), in_specs=[pl.BlockSpec((nd, tr, cols), lambda i: (0, i, 0))],
        out_specs=pl.BlockSpec((tr, cols), lambda i: (i, 0)), out_shape=jax.ShapeDtypeStruct((rows, cols), F32),
        compiler_params=_cp(("parallel",)), name=name)(x)


def _pad_rows(a, mult=8):
    r = (-a.shape[0]) % mult
    return jnp.pad(a, ((0, r), (0, 0))) if r else a


def _flat128(a):
    f = a.reshape(-1)
    return jnp.pad(f, (0, (-f.shape[0]) % LANE)).reshape(-1, LANE)


def _unshard(gathered, shape, axis):
    g = gathered.reshape((N_DEV,) + tuple(shape))
    g = jnp.moveaxis(g, 0, axis)
    full = list(shape)
    full[axis] *= N_DEV
    return g.reshape(full)


def _col_shards(full):
    rows, cols = full.shape
    return jnp.moveaxis(full.reshape(rows, N_DEV, cols // N_DEV), 1, 0)


BIG = (("ffn_w_gate", 2), ("ffn_w_up", 2), ("ffn_w_down", 2), ("w_in", 1), ("w_branch", 2), ("w_out", 1))


def kernel(x, ffn_norm, ffn_w_gate, ffn_w_up, ffn_w_down, mix_norm, w_in, b_gate, pool_w, pool_scale, dn_conv, dn_A_log, dn_dt_bias, dn_out_norm, w_branch, w_out, final_norm, loss_target, m_ffn_norm, m_ffn_w_gate, m_ffn_w_up, m_ffn_w_down, m_mix_norm, m_w_in, m_b_gate, m_pool_w, m_pool_scale, m_dn_conv, m_dn_A_log, m_dn_dt_bias, m_dn_out_norm, m_w_branch, m_w_out, m_final_norm, v_ffn_norm, v_ffn_w_gate, v_ffn_w_up, v_ffn_w_down, v_mix_norm, v_w_in, v_b_gate, v_pool_w, v_pool_scale, v_dn_conv, v_dn_A_log, v_dn_dt_bias, v_dn_out_norm, v_w_branch, v_w_out, v_final_norm):
    wts = dict(ffn_norm=ffn_norm, ffn_w_gate=ffn_w_gate, ffn_w_up=ffn_w_up, ffn_w_down=ffn_w_down, mix_norm=mix_norm,
               w_in=w_in, b_gate=b_gate, pool_w=pool_w, pool_scale=pool_scale, dn_conv=dn_conv, dn_A_log=dn_A_log,
               dn_dt_bias=dn_dt_bias, dn_out_norm=dn_out_norm, w_branch=w_branch, w_out=w_out, final_norm=final_norm)
    mom = dict(ffn_norm=m_ffn_norm, ffn_w_gate=m_ffn_w_gate, ffn_w_up=m_ffn_w_up, ffn_w_down=m_ffn_w_down,
               mix_norm=m_mix_norm, w_in=m_w_in, b_gate=m_b_gate, pool_w=m_pool_w, pool_scale=m_pool_scale,
               dn_conv=m_dn_conv, dn_A_log=m_dn_A_log, dn_dt_bias=m_dn_dt_bias, dn_out_norm=m_dn_out_norm,
               w_branch=m_w_branch, w_out=m_w_out, final_norm=m_final_norm)
    var = dict(ffn_norm=v_ffn_norm, ffn_w_gate=v_ffn_w_gate, ffn_w_up=v_ffn_w_up, ffn_w_down=v_ffn_w_down,
               mix_norm=v_mix_norm, w_in=v_w_in, b_gate=v_b_gate, pool_w=v_pool_w, pool_scale=v_pool_scale,
               dn_conv=v_dn_conv, dn_A_log=v_dn_A_log, dn_dt_bias=v_dn_dt_bias, dn_out_norm=v_dn_out_norm,
               w_branch=v_w_branch, w_out=v_w_out, final_norm=v_final_norm)
    nb, s, d = x.shape
    t = nb * s
    me = 4 * lax.axis_index("x") + 2 * lax.axis_index("y") + lax.axis_index("c")

    nls = [nl for _, nl in BIG]
    small_sh = jnp.concatenate([_flat128(ffn_norm), _flat128(dn_conv)], axis=0)
    *gat, small_g = _gather([wts[n].astype(BF16) for n, _ in BIG] + [small_sh], nls + [0], name="gather_weights")
    full = dict(zip([n for n, _ in BIG], gat))
    w_in_full = jnp.moveaxis(full["w_in"], 1, 2).reshape(DEPTH, d, -1)
    w_main = jnp.concatenate([w_in_full[:, :, :AB_LO], w_in_full[:, :, AB_HI:]], axis=2)
    w_ab = jnp.pad(w_in_full[:, :, AB_LO:AB_HI], ((0, 0), (0, 0), (0, LANE - (AB_HI - AB_LO))))
    wb_full = jnp.moveaxis(full["w_branch"], 2, 3).reshape(DEPTH, 3, BW, d)
    wo_full = full["w_out"].reshape(DEPTH, d, d)
    nfr = ffn_norm.size // LANE
    ffn_norm_full = _unshard(small_g[:, :nfr], ffn_norm.shape, 2)
    dn_conv_full = _unshard(small_g[:, nfr:], dn_conv.shape, 2)
    pool_w_h = pool_w.astype(BF16)

    xs = x.reshape(t, d)
    saved = []
    for l in range(DEPTH):
        sv = dict(x0=xs)
        xs = _ffn_fwd(xs, ffn_norm_full[l, 0], full["ffn_w_gate"][l, 0], full["ffn_w_up"][l, 0],
                      full["ffn_w_down"][l, 0], name="ffn_fwd")
        sv["x1"] = xs
        h = _rms_fwd(xs, mix_norm[l], name="mix_rms")
        proj = _mm(h, w_main[l], name="proj")
        ab = _mm(h, w_ab[l], name="proj_ab")
        par = jnp.pad(jnp.stack([dn_A_log[l], dn_dt_bias[l]]), ((0, 6), (0, LANE - NH)))
        gain = dn_out_norm[l].reshape(1, HD)
        psc = pool_scale[l].reshape(1, BW)
        yp = _pool_fwd(proj, pool_w_h[l], psc, nb, s, name="pool_fwd")
        yd, o_pre, states = _dn_fwd(proj, ab, dn_conv_full[l], par, gain, nb, s, name="dn_fwd")
        ys = _sb_fwd(proj, nb, s, name="sb_fwd")
        bg = b_gate[l].reshape(1, 3 * d)
        xs = _merge_fwd(xs, proj, yp, yd, ys, bg, wb_full[l], wo_full[l], name="merge_fwd")
        sv.update(x2=xs, h=h, proj=proj, ab=ab, par=par, gain=gain, psc=psc, yp=yp, yd=yd, ys=ys, o_pre=o_pre,
                  states=states, bg=bg)
        xs = _ffn_fwd(xs, ffn_norm_full[l, 1], full["ffn_w_gate"][l, 1], full["ffn_w_up"][l, 1],
                      full["ffn_w_down"][l, 1], name="ffn_fwd")
        saved.append(sv)

    dx, g_final, loss_row = _loss_head(xs, final_norm, loss_target.reshape(t, d), name="loss_head")
    loss = lax.psum(loss_row[0, 0], ("x", "y", "c"))

    gw = {n: [None] * DEPTH for n in ("ffn_norm", "ffn_w_gate", "ffn_w_up", "ffn_w_down", "mix_norm", "w_in", "b_gate",
                                      "pool_w", "pool_scale", "dn_conv", "dn_A_log", "dn_dt_bias", "dn_out_norm",
                                      "w_branch", "w_out")}

    def ffn_back(l, i, x_in, dy):
        dxi, dg, hb, dyh, da, db, sact = _ffn_bwd(x_in, ffn_norm_full[l, i], full["ffn_w_gate"][l, i],
                                                  full["ffn_w_up"][l, i], full["ffn_w_down"][l, i], dy, name="ffn_bwd")
        return dxi, dg, (_mm_tn_slots(hb, da, name="dw_gate_up"), _mm_tn_slots(hb, db, name="dw_gate_up"),
                         _mm_tn_slots(sact, dyh, name="dw_down"))

    for l in reversed(range(DEPTH)):
        sv = saved[l]
        dx, dg1, (dwg1, dwu1, dwd1) = ffn_back(l, 1, sv["x2"], dx)
        dyp, dyd, dys, dgl, merged, dxh, dbd, dbg = _merge_bwd(sv["proj"], sv["yp"], sv["yd"], sv["ys"], sv["bg"],
                                                               wb_full[l], wo_full[l], dx,
                                                               name="merge_bwd")
        gw["w_out"][l] = _mm(merged, dxh, ta=True, out_dtype=BF16, name="dw_out").reshape(N_DEV, d // N_DEV, d)
        gw["w_branch"][l] = jnp.stack([_col_shards(_mm(y, dbd[n], ta=True, out_dtype=BF16, name="dw_branch"))
                                       for n, y in enumerate((sv["yp"], sv["yd"], sv["ys"]))])
        gw["b_gate"][l] = dbg.reshape(3 * d)
        du, dpw, dps = _pool_bwd(sv["proj"], pool_w_h[l], sv["psc"], dyp, nb, s, name="pool_bwd")
        gw["pool_w"][l], gw["pool_scale"][l] = dpw, dps.reshape(BW)
        dqr, dkr, dvr, dz, dab4, dcq, dck, dcv, dpar, dgain = _dn_bwd(
            sv["proj"], sv["ab"], dn_conv_full[l], sv["par"], sv["gain"], sv["o_pre"], sv["states"], dyd, nb, s,
            name="dn_bwd")
        gw["dn_conv"][l] = jnp.concatenate([dcq, dck, dcv], axis=1)
        gw["dn_A_log"][l], gw["dn_dt_bias"][l], gw["dn_out_norm"][l] = dpar[:, 0, 0], dpar[:, 1, 0], dgain.reshape(HD)
        dsq, dsk, dsv = _sb_bwd(sv["proj"], dys, nb, s, name="sb_bwd")
        dab = _sum_heads(dab4, name="sum_heads")
        dproj = jnp.concatenate([du.astype(BF16), dqr.astype(BF16), dkr.astype(BF16), dvr.astype(BF16),
                                 dz.astype(BF16), dsq.astype(BF16), dsk.astype(BF16), dsv.astype(BF16), dgl], axis=1)
        dw_main = _mm(sv["h"], dproj, ta=True, out_dtype=BF16, name="dw_in")
        dw_ab = _mm(sv["h"], dab, ta=True, out_dtype=BF16, name="dw_ab")
        gw["w_in"][l] = _col_shards(jnp.concatenate([dw_main[:, :AB_LO], dw_ab[:, :AB_HI - AB_LO],
                                                     dw_main[:, AB_LO:]], axis=1))
        dh_main = _mm(dproj, w_main[l], tb=True, name="dh_mix")
        dh_ab = _mm(dab, w_ab[l], tb=True, name="dh_mix_ab")
        dx, dgm = _rms_bwd(sv["x1"], mix_norm[l], dh_main, dh_ab, dx, name="mix_rms_bwd")
        gw["mix_norm"][l] = dgm.reshape(d)
        dx, dg0, (dwg0, dwu0, dwd0) = ffn_back(l, 0, sv["x0"], dx)
        gw["ffn_norm"][l] = jnp.stack([dg0.reshape(d), dg1.reshape(d)])
        gw["ffn_w_gate"][l] = jnp.stack([dwg0, dwg1])
        gw["ffn_w_up"][l] = jnp.stack([dwu0, dwu1])
        gw["ffn_w_down"][l] = jnp.stack([dwd0, dwd1])
    grad_x = dx.reshape(nb, s, d)
    gw = {n: jnp.stack(v) for n, v in gw.items()}
    gw["final_norm"] = g_final.reshape(d)

    big = [n for n, _ in BIG]
    own, got = _scatter_pair([gw[n] for n in big], nls, name="scatter_grads_pair")
    chip_sums = [_add_bf16(a, b, name="add_pair_" + n) for n, a, b in zip(big, own, got)]
    recv = _scatter_chips(chip_sums, nls, name="scatter_grads_chips")
    grads, delta, new_m, new_v = {}, {}, {}, {}
    for n, nl, r in zip(big, nls, recv):
        grads[n], delta[n], new_m[n], new_v[n] = _sum_adamw(r, wts[n], mom[n], var[n], nl, name="adamw_" + n)

    small = ("ffn_norm", "mix_norm", "b_gate", "pool_w", "pool_scale", "dn_conv", "dn_A_log", "dn_dt_bias",
             "dn_out_norm", "final_norm")
    sp = _pad_rows(jnp.concatenate([_flat128(gw[n]) for n in small], axis=0))
    ssum = _sum_slots(_gather([sp], [0], name="gather_small_grads")[0], name="sum_small_grads")
    off = 0
    for n in small:
        r = -(-gw[n].size // LANE)
        g = ssum[off:off + r].reshape(-1)[:gw[n].size].reshape(gw[n].shape)
        off += r
        if n in ("ffn_norm", "dn_conv"):
            w = wts[n].shape[2]
            g = lax.dynamic_slice_in_dim(g, me * w, w, axis=2)
        grads[n] = g

    pk = lambda src: _pad_rows(jnp.concatenate([_flat128(src[n]) for n in small], axis=0))
    dl, nm, nv = _adamw(pk(wts), pk(grads), pk(mom), pk(var), name="adamw_small")
    off = 0
    for n in small:
        r = -(-wts[n].size // LANE)
        for dst, src in ((delta, dl), (new_m, nm), (new_v, nv)):
            dst[n] = src[off:off + r].reshape(-1)[:wts[n].size].reshape(wts[n].shape)
        off += r

    order = ("ffn_norm", "ffn_w_gate", "ffn_w_up", "ffn_w_down", "mix_norm", "w_in", "b_gate", "pool_w", "pool_scale",
             "dn_conv", "dn_A_log", "dn_dt_bias", "dn_out_norm", "w_branch", "w_out", "final_norm")
    return (loss, grad_x, *[grads[n] for n in order], *[delta[n] for n in order], *[new_m[n] for n in order],
            *[new_v[n] for n in order])
```

```python
import functools
import math

import jax
import jax.numpy as jnp
from jax import lax
from jax.experimental import pallas as pl
from jax.experimental.pallas import tpu as pltpu

F32, BF16 = jnp.float32, jnp.bfloat16
D_MODEL, D_FF, DEPTH = 1024, 2816, 4
BW = 512
HD = 128
NH = 4
DN_CHUNK = 64
EPS = 1e-6
N_DEV = 8
LANE = 128
CB_POOL, CB_DNQ, CB_DNK, CB_DNV, CB_DNZ, CB_SBQ, CB_SBK, CB_SBV = 0, 4, 8, 12, 16, 20, 24, 28
CB_GATE = 4
P_MAIN = 7168
AB_LO, AB_HI = 2560, 2568
ADAM_LR, ADAM_B1, ADAM_B2, ADAM_EPS, ADAM_WD, ADAM_STEP = 0.001, 0.9, 0.999, 1e-08, 0.01, 10
VMEM_LIMIT = 56 * 1024 * 1024
HIGHEST = lax.Precision.HIGHEST
NT_DIMS = (((1,), (1,)), ((), ()))
TN_DIMS = (((0,), (0,)), ((), ()))
NN_DIMS = (((1,), (0,)), ((), ()))


def _cp(dims=None, vmem=VMEM_LIMIT):
    return pltpu.CompilerParams(dimension_semantics=dims, vmem_limit_bytes=vmem)


def _pick(n, cands):
    for c in cands:
        if n % c == 0:
            return c
    return n


def _bdot(a, b, dims=NN_DIMS):
    return lax.dot_general(a.astype(BF16), b.astype(BF16), dims, preferred_element_type=F32)


def _hdot(a, b, dims=NN_DIMS):
    return lax.dot_general(a, b, dims, precision=HIGHEST, preferred_element_type=F32)


def _split_dot(x, m01):
    hi = x.astype(BF16)
    lo = (x - hi.astype(F32)).astype(BF16)
    return (lax.dot_general(hi, m01, NN_DIMS, preferred_element_type=F32)
            + lax.dot_general(lo, m01, NN_DIMS, preferred_element_type=F32))


def _sigmoid(x):
    return 1.0 / (1.0 + jnp.exp(-x))


def _log_sigmoid(x):
    return jnp.minimum(x, 0.0) - jnp.log1p(jnp.exp(-jnp.abs(x)))


def _softplus(x):
    return jnp.maximum(x, 0.0) + jnp.log1p(jnp.exp(-jnp.abs(x)))


def _shift_down(x, k):
    r = lax.broadcasted_iota(jnp.int32, x.shape, 0)
    return jnp.where(r >= k, pltpu.roll(x, k, 0), 0.0)


def _shift_up(x, k):
    n = x.shape[0]
    r = lax.broadcasted_iota(jnp.int32, x.shape, 0)
    return jnp.where(r < n - k, pltpu.roll(x, n - k, 0), 0.0)


def _mm(a, b, *, ta=False, tb=False, out_dtype=F32, name):
    (kk, m) = a.shape if ta else a.shape[::-1]
    (k2, n) = b.shape[::-1] if tb else b.shape
    assert kk == k2, (a.shape, b.shape, ta, tb)
    bm = _pick(m, (1024, 512, 256, 128))
    bn = _pick(n, (1024, 1408, 512, 256, 128))
    bk = _pick(kk, (512, 256, 128))
    nk = kk // bk
    dims = (((0 if ta else 1,), (1 if tb else 0,)), ((), ()))

    def body(a_ref, b_ref, o_ref, acc_ref):
        k = pl.program_id(2)

        @pl.when(k == 0)
        def _():
            acc_ref[...] = jnp.zeros_like(acc_ref)

        acc_ref[...] += lax.dot_general(a_ref[...].astype(BF16), b_ref[...].astype(BF16), dims,
                                        preferred_element_type=F32)

        @pl.when(k == nk - 1)
        def _():
            o_ref[...] = acc_ref[...].astype(out_dtype)

    a_spec = (pl.BlockSpec((bk, bm), lambda i, j, k: (k, i)) if ta else pl.BlockSpec((bm, bk), lambda i, j, k: (i, k)))
    b_spec = (pl.BlockSpec((bn, bk), lambda i, j, k: (j, k)) if tb else pl.BlockSpec((bk, bn), lambda i, j, k: (k, j)))
    return pl.pallas_call(
        body, grid=(m // bm, n // bn, nk), in_specs=[a_spec, b_spec],
        out_specs=pl.BlockSpec((bm, bn), lambda i, j, k: (i, j)),
        out_shape=jax.ShapeDtypeStruct((m, n), out_dtype),
        scratch_shapes=[pltpu.VMEM((bm, bn), F32)],
        compiler_params=_cp(("parallel", "parallel", "arbitrary")), name=name)(a, b)


def _mm_tn_slots(a, b, *, name):
    a3, b3 = a.ndim == 3, b.ndim == 3
    ns = a.shape[0] if a3 else b.shape[0]
    t, m = a.shape[-2:]
    n = b.shape[-1]
    bm, bn, bk = _pick(m, (1024, 512, 256, 128)), _pick(n, (1024, 512, 256, 128)), _pick(t, (512, 256, 128))
    nk = t // bk

    def body(a_ref, b_ref, o_ref, acc_ref):
        k = pl.program_id(3)

        @pl.when(k == 0)
        def _():
            acc_ref[...] = jnp.zeros_like(acc_ref)

        acc_ref[...] += _bdot(a_ref[...], b_ref[...], TN_DIMS)

        @pl.when(k == nk - 1)
        def _():
            o_ref[...] = acc_ref[...].astype(BF16)

    a_spec = (pl.BlockSpec((None, bk, bm), lambda s, i, j, k: (s, k, i)) if a3
              else pl.BlockSpec((bk, bm), lambda s, i, j, k: (k, i)))
    b_spec = (pl.BlockSpec((None, bk, bn), lambda s, i, j, k: (s, k, j)) if b3
              else pl.BlockSpec((bk, bn), lambda s, i, j, k: (k, j)))
    return pl.pallas_call(
        body, grid=(ns, m // bm, n // bn, nk), in_specs=[a_spec, b_spec],
        out_specs=pl.BlockSpec((None, bm, bn), lambda s, i, j, k: (s, i, j)),
        out_shape=jax.ShapeDtypeStruct((ns, m, n), BF16), scratch_shapes=[pltpu.VMEM((bm, bn), F32)],
        compiler_params=_cp(("parallel", "parallel", "parallel", "arbitrary")), name=name)(a, b)


def _rms_stats(x):
    rstd = lax.rsqrt(jnp.mean(x * x, axis=-1, keepdims=True) + EPS)
    return x * rstd, rstd


def _rms_bwd_vals(dh, xhat, rstd, g):
    dxh = dh * g
    dx = rstd * (dxh - xhat * jnp.mean(dxh * xhat, axis=-1, keepdims=True))
    return dx, jnp.sum(dh * xhat, axis=0, keepdims=True)


def _rms_fwd(x, g, *, name):
    t, d = x.shape
    tm = _pick(t, (512, 256, 128))

    def body(x_ref, g_ref, h_ref):
        xhat, _ = _rms_stats(x_ref[...])
        h_ref[...] = (xhat * g_ref[...]).astype(BF16)

    return pl.pallas_call(
        body, grid=(t // tm,),
        in_specs=[pl.BlockSpec((tm, d), lambda i: (i, 0)), pl.BlockSpec((1, d), lambda i: (0, 0))],
        out_specs=pl.BlockSpec((tm, d), lambda i: (i, 0)), out_shape=jax.ShapeDtypeStruct((t, d), BF16),
        compiler_params=_cp(("parallel",)), name=name)(x, g.reshape(1, d))


def _rms_bwd(x, g, dh_a, dh_b, dres, *, name):
    t, d = x.shape
    tm = _pick(t, (512, 256, 128))

    def body(x_ref, g_ref, dha_ref, dhb_ref, dres_ref, dx_ref, dg_ref):
        xhat, rstd = _rms_stats(x_ref[...])
        dx, dg = _rms_bwd_vals(dha_ref[...] + dhb_ref[...], xhat, rstd, g_ref[...])
        dx_ref[...] = dres_ref[...] + dx

        @pl.when(pl.program_id(0) == 0)
        def _():
            dg_ref[...] = jnp.zeros_like(dg_ref)

        dg_ref[...] += dg

    row = pl.BlockSpec((tm, d), lambda i: (i, 0))
    vec = pl.BlockSpec((1, d), lambda i: (0, 0))
    return pl.pallas_call(
        body, grid=(t // tm,), in_specs=[row, vec, row, row, row], out_specs=[row, vec],
        out_shape=[jax.ShapeDtypeStruct((t, d), F32), jax.ShapeDtypeStruct((1, d), F32)],
        compiler_params=_cp(("arbitrary",)), name=name)(x, g.reshape(1, d), dh_a, dh_b, dres)


FFN_TM = 512


def _ffn_fwd(x, g, wg, wu, wd, *, name):
    t, d = x.shape
    nf, _, fc = wg.shape
    tm = _pick(t, (FFN_TM, 256, 128))

    def body(x_ref, g_ref, wg_ref, wu_ref, wd_ref, o_ref, h_ref, acc_ref):
        j = pl.program_id(1)

        @pl.when(j == 0)
        def _():
            xhat, _ = _rms_stats(x_ref[...])
            h_ref[...] = (xhat * g_ref[...]).astype(BF16)
            acc_ref[...] = jnp.zeros_like(acc_ref)

        h = h_ref[...]
        a = _bdot(h, wg_ref[...])
        b = _bdot(h, wu_ref[...])
        s = a * _sigmoid(a) * b
        acc_ref[...] += _bdot(s, wd_ref[...])

        @pl.when(j == nf - 1)
        def _():
            o_ref[...] = x_ref[...] + 0.5 * acc_ref[...]

    row = pl.BlockSpec((tm, d), lambda i, j: (i, 0))
    return pl.pallas_call(
        body, grid=(t // tm, nf),
        in_specs=[row, pl.BlockSpec((1, d), lambda i, j: (0, 0)),
                  pl.BlockSpec((None, d, fc), lambda i, j: (j, 0, 0)), pl.BlockSpec((None, d, fc), lambda i, j: (j, 0, 0)),
                  pl.BlockSpec((None, fc, d), lambda i, j: (j, 0, 0))],
        out_specs=row, out_shape=jax.ShapeDtypeStruct((t, d), F32),
        scratch_shapes=[pltpu.VMEM((tm, d), BF16), pltpu.VMEM((tm, d), F32)],
        compiler_params=_cp(("parallel", "arbitrary")), name=name)(x, g.reshape(1, d), wg, wu, wd)


def _ffn_bwd(x, g, wg, wu, wd, dy, *, name):
    t, d = x.shape
    nf, _, fc = wg.shape
    tm = _pick(t, (FFN_TM, 256, 128))

    def body(x_ref, g_ref, wg_ref, wu_ref, wd_ref, dy_ref,
             dx_ref, dg_ref, h_ref, dyh_ref, da_ref, db_ref, s_ref, acc_ref):
        i, j = pl.program_id(0), pl.program_id(1)

        @pl.when(j == 0)
        def _():
            xhat, _ = _rms_stats(x_ref[...])
            h_ref[...] = (xhat * g_ref[...]).astype(BF16)
            dyh_ref[...] = (0.5 * dy_ref[...]).astype(BF16)
            acc_ref[...] = jnp.zeros_like(acc_ref)

        h = h_ref[...]
        a = _bdot(h, wg_ref[...])
        b = _bdot(h, wu_ref[...])
        sg = _sigmoid(a)
        silu = a * sg
        s_ref[...] = (silu * b).astype(BF16)
        ds = _bdot(dyh_ref[...], wd_ref[...], NT_DIMS)
        da = (ds * b * (sg * (1.0 + a * (1.0 - sg)))).astype(BF16)
        db = (ds * silu).astype(BF16)
        da_ref[...] = da
        db_ref[...] = db
        acc_ref[...] += _bdot(da, wg_ref[...], NT_DIMS) + _bdot(db, wu_ref[...], NT_DIMS)

        @pl.when((i == 0) & (j == 0))
        def _():
            dg_ref[...] = jnp.zeros_like(dg_ref)

        @pl.when(j == nf - 1)
        def _():
            xhat, rstd = _rms_stats(x_ref[...])
            dx, dg = _rms_bwd_vals(acc_ref[...], xhat, rstd, g_ref[...])
            dx_ref[...] = dy_ref[...] + dx
            dg_ref[...] += dg

    row = pl.BlockSpec((tm, d), lambda i, j: (i, 0))
    vec = pl.BlockSpec((1, d), lambda i, j: (0, 0))
    fblk = pl.BlockSpec((None, tm, fc), lambda i, j: (j, i, 0))
    return pl.pallas_call(
        body, grid=(t // tm, nf),
        in_specs=[row, vec, pl.BlockSpec((None, d, fc), lambda i, j: (j, 0, 0)),
                  pl.BlockSpec((None, d, fc), lambda i, j: (j, 0, 0)), pl.BlockSpec((None, fc, d), lambda i, j: (j, 0, 0)),
                  row],
        out_specs=[row, vec, row, row, fblk, fblk, fblk],
        out_shape=[jax.ShapeDtypeStruct((t, d), F32), jax.ShapeDtypeStruct((1, d), F32),
                   jax.ShapeDtypeStruct((t, d), BF16), jax.ShapeDtypeStruct((t, d), BF16),
                   jax.ShapeDtypeStruct((nf, t, fc), BF16), jax.ShapeDtypeStruct((nf, t, fc), BF16),
                   jax.ShapeDtypeStruct((nf, t, fc), BF16)],
        scratch_shapes=[pltpu.VMEM((tm, d), F32)],
        compiler_params=_cp(("arbitrary", "arbitrary")), name=name)(x, g.reshape(1, d), wg, wu, wd, dy)


def _pool_core(u, grp):
    s = u.shape[0]
    w2 = u + _shift_down(u, 1)
    w4 = w2 + _shift_down(w2, 2)
    w8 = w4 + _shift_down(w4, 4)
    w16 = w8 + _shift_down(w8, 8)
    wsum = jnp.where(grp == 0, w2, jnp.where(grp == 1, w4, jnp.where(grp == 2, w8, w16)))
    win = jnp.left_shift(2, grp).astype(F32)
    t1 = (lax.broadcasted_iota(jnp.int32, (s, 1), 0) + 1).astype(F32)
    inv = 1.0 / jnp.minimum(t1, win)
    return wsum * inv - u, inv


def _pool_fwd(proj, pool_w, pool_scale, nb, s, *, name):
    def body(u_ref, w_ref, sc_ref, y_ref):
        pooled, _ = _pool_core(u_ref[...], pl.program_id(0))
        y_ref[...] = _bdot(pooled, w_ref[...]) * sc_ref[...]

    return pl.pallas_call(
        body, grid=(NH, nb),
        in_specs=[pl.BlockSpec((s, HD), lambda g, b: (b, CB_POOL + g)),
                  pl.BlockSpec((None, HD, HD), lambda g, b: (g, 0, 0)), pl.BlockSpec((1, HD), lambda g, b: (0, g))],
        out_specs=pl.BlockSpec((s, HD), lambda g, b: (b, g)),
        out_shape=jax.ShapeDtypeStruct((nb * s, BW), F32),
        compiler_params=_cp(("parallel", "parallel")), name=name)(proj, pool_w, pool_scale)


def _pool_bwd(proj, pool_w, pool_scale, dy, nb, s, *, name):
    def body(u_ref, w_ref, sc_ref, dy_ref, du_ref, dw_ref, dsc_ref):
        grp, b = pl.program_id(0), pl.program_id(1)
        pooled, inv = _pool_core(u_ref[...], grp)
        mixed = _bdot(pooled, w_ref[...])
        dy = dy_ref[...]
        dmixed = dy * sc_ref[...]
        dpooled = _bdot(dmixed, w_ref[...], NT_DIMS)
        r = dpooled * inv
        v2 = r + _shift_up(r, 1)
        v4 = v2 + _shift_up(v2, 2)
        v8 = v4 + _shift_up(v4, 4)
        v16 = v8 + _shift_up(v8, 8)
        vsum = jnp.where(grp == 0, v2, jnp.where(grp == 1, v4, jnp.where(grp == 2, v8, v16)))
        du_ref[...] = vsum - dpooled

        @pl.when(b == 0)
        def _():
            dw_ref[...] = jnp.zeros_like(dw_ref)
            dsc_ref[...] = jnp.zeros_like(dsc_ref)

        dw_ref[...] += _bdot(pooled, dmixed, TN_DIMS)
        dsc_ref[...] += jnp.sum(dy * mixed, axis=0, keepdims=True)

    return pl.pallas_call(
        body, grid=(NH, nb),
        in_specs=[pl.BlockSpec((s, HD), lambda g, b: (b, CB_POOL + g)),
                  pl.BlockSpec((None, HD, HD), lambda g, b: (g, 0, 0)), pl.BlockSpec((1, HD), lambda g, b: (0, g)),
                  pl.BlockSpec((s, HD), lambda g, b: (b, g))],
        out_specs=[pl.BlockSpec((s, HD), lambda g, b: (b, g)), pl.BlockSpec((None, HD, HD), lambda g, b: (g, 0, 0)),
                   pl.BlockSpec((1, HD), lambda g, b: (0, g))],
        out_shape=[jax.ShapeDtypeStruct((nb * s, BW), F32), jax.ShapeDtypeStruct((NH, HD, HD), F32),
                   jax.ShapeDtypeStruct((1, BW), F32)],
        compiler_params=_cp(("arbitrary", "arbitrary")), name=name)(proj, pool_w, pool_scale, dy)


SB_BLK = 128


SB_G = 4
SB_KG = SB_G * SB_BLK


def _sb_block(qb, kg, q0, k0):
    z = _bdot(qb, kg, NT_DIMS) * (HD ** -0.5)
    row = lax.broadcasted_iota(jnp.int32, (SB_BLK, SB_KG), 0) + q0
    col = lax.broadcasted_iota(jnp.int32, (SB_BLK, SB_KG), 1) + k0
    causal = col < row
    lsz = _log_sigmoid(z)
    lnm = jnp.where(causal, lsz - z, 0.0)
    return lsz, lnm, causal


def _sub(x, m):
    return x[:, m * SB_BLK:(m + 1) * SB_BLK]


def _sb_tails(lnm, after, ct):
    tails, cts = [None] * SB_G, [None] * SB_G
    for m in reversed(range(SB_G)):
        cts[m] = ct
        tails[m] = _split_dot(_sub(lnm, m), after) + ct
        ct = ct + jnp.sum(_sub(lnm, m), axis=1, keepdims=True)
    return jnp.concatenate(tails, axis=1), cts, ct


def _tri01(lower):
    r = lax.broadcasted_iota(jnp.int32, (SB_BLK, SB_BLK), 0)
    c = lax.broadcasted_iota(jnp.int32, (SB_BLK, SB_BLK), 1)
    return jnp.where((r < c) if lower else (r > c), 1.0, 0.0).astype(BF16)


def _sb_fwd(proj, nb, s, *, name):
    nq = s // SB_BLK

    def body(q_ref, k_ref, v_ref, o_ref):
        after = _tri01(False)

        def qblock(i, _):
            q0 = pl.multiple_of(i * SB_BLK, SB_BLK)
            qb = q_ref[pl.ds(q0, SB_BLK), :]

            def kgroup(jj, carry):
                acc, ct = carry
                k0 = pl.multiple_of((i // SB_G - jj) * SB_KG, SB_KG)
                lsz, lnm, causal = _sb_block(qb, k_ref[pl.ds(k0, SB_KG), :], q0, k0)
                tail, _, ct = _sb_tails(lnm, after, ct)
                w = jnp.where(causal, jnp.exp(lsz + tail), 0.0)
                return acc + _bdot(w, v_ref[pl.ds(k0, SB_KG), :]), ct

            acc, _ = lax.fori_loop(0, i // SB_G + 1, kgroup,
                                   (jnp.zeros((SB_BLK, HD), F32), jnp.zeros((SB_BLK, 1), F32)))
            o_ref[pl.ds(q0, SB_BLK), :] = acc
            return 0

        lax.fori_loop(0, nq, qblock, 0)

    def col(cb):
        return pl.BlockSpec((s, HD), lambda b, h: (b, cb + h))

    return pl.pallas_call(
        body, grid=(nb, NH), in_specs=[col(CB_SBQ), col(CB_SBK), col(CB_SBV)],
        out_specs=pl.BlockSpec((s, HD), lambda b, h: (b, h)),
        out_shape=jax.ShapeDtypeStruct((nb * s, BW), F32),
        compiler_params=_cp(("parallel", "parallel")), name=name)(proj, proj, proj)


def _sb_bwd(proj, dy, nb, s, *, name):
    nq = s // SB_BLK
    scale = HD ** -0.5

    def body(q_ref, k_ref, v_ref, do_ref, dq_ref, dk_ref, dv_ref, ct_ref):
        after = _tri01(False)
        before = _tri01(True)
        dk_ref[...] = jnp.zeros_like(dk_ref)
        dv_ref[...] = jnp.zeros_like(dv_ref)

        def qblock(i, _):
            q0 = pl.multiple_of(i * SB_BLK, SB_BLK)
            qb = q_ref[pl.ds(q0, SB_BLK), :]
            dob = do_ref[pl.ds(q0, SB_BLK), :]

            def tails(jj, ct):
                g = i // SB_G - jj
                k0 = pl.multiple_of(g * SB_KG, SB_KG)
                _, lnm, _ = _sb_block(qb, k_ref[pl.ds(k0, SB_KG), :], q0, k0)
                ct_ref[g] = jnp.broadcast_to(ct, (SB_BLK, LANE))
                return ct + jnp.sum(lnm, axis=1, keepdims=True)

            lax.fori_loop(0, i // SB_G + 1, tails, jnp.zeros((SB_BLK, 1), F32))

            def kgroup(g, carry):
                dq, ce = carry
                k0 = pl.multiple_of(g * SB_KG, SB_KG)
                kg = k_ref[pl.ds(k0, SB_KG), :]
                vg = v_ref[pl.ds(k0, SB_KG), :]
                lsz, lnm, causal = _sb_block(qb, kg, q0, k0)
                tail, _, _ = _sb_tails(lnm, after, ct_ref[g][:, 0:1])
                w = jnp.where(causal, jnp.exp(lsz + tail), 0.0)
                e = _bdot(dob, vg, NT_DIMS) * w
                pres = []
                for m in range(SB_G):
                    pres.append(_split_dot(_sub(e, m), before) + ce)
                    ce = ce + jnp.sum(_sub(e, m), axis=1, keepdims=True)
                sig = jnp.exp(lsz)
                dz = jnp.where(causal, e * (1.0 - sig) - jnp.concatenate(pres, axis=1) * sig, 0.0) * scale
                dk_ref[pl.ds(k0, SB_KG), :] += _bdot(dz, qb, TN_DIMS)
                dv_ref[pl.ds(k0, SB_KG), :] += _bdot(w, dob, TN_DIMS)
                return dq + _bdot(dz, kg), ce

            dq, _ = lax.fori_loop(0, i // SB_G + 1, kgroup,
                                  (jnp.zeros((SB_BLK, HD), F32), jnp.zeros((SB_BLK, 1), F32)))
            dq_ref[pl.ds(q0, SB_BLK), :] = dq
            return 0

        lax.fori_loop(0, nq, qblock, 0)

    def col(cb):
        return pl.BlockSpec((s, HD), lambda b, h: (b, cb + h))

    out = pl.BlockSpec((s, HD), lambda b, h: (b, h))
    sds = jax.ShapeDtypeStruct((nb * s, BW), F32)
    return pl.pallas_call(
        body, grid=(nb, NH), in_specs=[col(CB_SBQ), col(CB_SBK), col(CB_SBV), out],
        out_specs=[out, out, out], out_shape=[sds, sds, sds],
        scratch_shapes=[pltpu.VMEM((nq, SB_BLK, LANE), F32)],
        compiler_params=_cp(("parallel", "parallel")), name=name)(proj, proj, proj, dy)


def _make_cdot(dims, dims_da, dims_db, swap_a=False, swap_b=False):
    @jax.custom_vjp
    def f(a, b):
        return _bdot(a, b, dims)

    def fwd(a, b):
        return _bdot(a, b, dims), (a, b)

    def bwd(res, g):
        a, b = res
        da = _bdot(b, g, dims_da) if swap_a else _bdot(g, b, dims_da)
        db = _bdot(g, a, dims_db) if swap_b else _bdot(a, g, dims_db)
        return da, db

    f.defvjp(fwd, bwd)
    return f


_cdot = _make_cdot(NN_DIMS, NT_DIMS, TN_DIMS)
_cdot_nt = _make_cdot(NT_DIMS, NN_DIMS, TN_DIMS, swap_b=True)
_cdot_tn = _make_cdot(TN_DIMS, NT_DIMS, NN_DIMS, swap_a=True)


DN_SUPER = 4 * DN_CHUNK


def _dn_local(q, k, v, bb, gb):
    n = q.shape[0]
    r = lax.broadcasted_iota(jnp.int32, (n, n), 0)
    cc = lax.broadcasted_iota(jnp.int32, (n, n), 1)
    shift = int(math.log2(DN_CHUNK))
    same = lax.shift_right_logical(r, shift) == lax.shift_right_logical(cc, shift)
    incl = jnp.where(same, jnp.where(r >= cc, 1.0, 0.0), 0.0)
    strict = jnp.where(same, jnp.where(r > cc, 1.0, 0.0), 0.0)
    gc = _hdot(incl, gb)
    gc_row = _hdot(jnp.full((n, HD), 1.0 / HD, F32), gc, NT_DIMS)
    diff = jnp.concatenate([gc] * (n // HD), axis=1) - gc_row
    decay = incl * jnp.exp(diff * incl)
    kb = k * bb
    lmat = _cdot_nt(kb, k) * (strict * decay)
    egc = jnp.exp(gc)
    inv = jnp.where(r == cc, 1.0, 0.0) - lmat
    pw = _hdot(lmat, lmat)
    for it in range(shift - 1):
        inv = inv + _hdot(inv, pw)
        if it < shift - 2:
            pw = _hdot(pw, pw)
    u = _hdot(inv, v * bb)
    w = _hdot(inv, kb * egc)
    attn = _cdot_nt(q, k) * decay
    gl = _hdot(jnp.where(same, 1.0, 0.0), gb)
    return u, w, attn, q * egc, k * jnp.exp(gl - gc), jnp.exp(gl)


def _attn_pairs(attn):
    return jnp.concatenate([attn[:HD, :HD], attn[HD:, HD:]], axis=0)


def _attn_unpairs(a):
    z = jnp.zeros((HD, HD), F32)
    return jnp.concatenate([jnp.concatenate([a[:HD], z], axis=1), jnp.concatenate([z, a[HD:]], axis=1)], axis=0)


def _dn_step(u, w, a, qd, kd, cdrows, state, odd):
    v_new = u - _cdot(w, state)
    z = jnp.zeros_like(v_new)
    o = _cdot(qd, state) + _cdot(a, jnp.concatenate([z, v_new] if odd else [v_new, z], axis=0))
    return o, state * jnp.mean(cdrows, axis=0, keepdims=True) + _cdot_tn(kd, v_new)


def _dn_local_pass(fn, s, ins, outs):
    def step(it, _):
        sl = pl.ds(pl.multiple_of(it * DN_SUPER, DN_SUPER), DN_SUPER)
        res = fn(*[ref[sl, :] for ref in ins])
        for ref, val in zip(outs, res):
            ref[sl, :] = val
        return 0

    lax.fori_loop(0, s // DN_SUPER, step, 0)


def _lane_pick(row, idx):
    lane = lax.broadcasted_iota(jnp.int32, row.shape, 1)
    return jnp.sum(jnp.where(lane == idx, row, 0.0), axis=1, keepdims=True)


def _col_pick(x, idx):
    lane = lax.broadcasted_iota(jnp.int32, x.shape, 1)
    return jnp.sum(jnp.where(lane == idx, x, 0.0), axis=1, keepdims=True)


def _conv_silu(x, w):
    xc = (w[3:4, :] * x + w[2:3, :] * _shift_down(x, 1) + w[1:2, :] * _shift_down(x, 2)
          + w[0:1, :] * _shift_down(x, 3))
    return xc * _sigmoid(xc), xc


def _conv_silu_bwd(x, w, xc, dxs, dw_ref):
    sg = _sigmoid(xc)
    dxc = dxs * (sg * (1.0 + xc * (1.0 - sg)))
    dx = (w[3:4, :] * dxc + w[2:3, :] * _shift_up(dxc, 1) + w[1:2, :] * _shift_up(dxc, 2)
          + w[0:1, :] * _shift_up(dxc, 3))
    dw_ref[3:4, :] += jnp.sum(dxc * x, axis=0, keepdims=True)
    dw_ref[2:3, :] += jnp.sum(dxc * _shift_down(x, 1), axis=0, keepdims=True)
    dw_ref[1:2, :] += jnp.sum(dxc * _shift_down(x, 2), axis=0, keepdims=True)
    dw_ref[0:1, :] += jnp.sum(dxc * _shift_down(x, 3), axis=0, keepdims=True)
    return dx


def _dn_prep(qr_ref, kr_ref, vr_ref, ab_ref, cq_ref, ck_ref, cv_ref, par_ref, head):
    qs, qc = _conv_silu(qr_ref[...], cq_ref[...])
    ks, kc = _conv_silu(kr_ref[...], ck_ref[...])
    vs, vc = _conv_silu(vr_ref[...], cv_ref[...])
    rq = lax.rsqrt(jnp.sum(qs * qs, axis=1, keepdims=True) + EPS)
    rk = lax.rsqrt(jnp.sum(ks * ks, axis=1, keepdims=True) + EPS)
    ab = ab_ref[...]
    a_in = _col_pick(ab, head) + _lane_pick(par_ref[1:2, :], head)
    beta = _sigmoid(_col_pick(ab, NH + head))
    neg_ea = -jnp.exp(_lane_pick(par_ref[0:1, :], head))
    g = neg_ea * _softplus(a_in)
    return dict(q=qs * rq * (HD ** -0.5), k=ks * rk, v=vs, beta=beta, g=g, qs=qs, ks=ks, qc=qc, kc=kc, vc=vc,
                rq=rq, rk=rk, a_in=a_in, neg_ea=neg_ea)


ONE_BUF = pl.Buffered(1)
DN_BWD_VMEM = 62 * 1024 * 1024


def _dn_specs(nb, s):
    def col(cb):
        return pl.BlockSpec((s, HD), lambda h, b: (b, cb + h), pipeline_mode=ONE_BUF)

    def conv(cb):
        return pl.BlockSpec((DN_CONV_W, HD), lambda h, b: (0, cb + h))

    return col, conv


DN_CONV_W = 4


def _dn_fwd(proj, ab, conv_w, par, gain, nb, s, *, name):
    nc = s // DN_CHUNK
    col, conv = _dn_specs(nb, s)

    def body(qr_ref, kr_ref, vr_ref, z_ref, ab_ref, cq_ref, ck_ref, cv_ref, par_ref, gain_ref,
             y_ref, o_ref, st_ref, q_s, k_s, v_s, bb_s, gb_s, at_s, cd_s):
        p = _dn_prep(qr_ref, kr_ref, vr_ref, ab_ref, cq_ref, ck_ref, cv_ref, par_ref, pl.program_id(0))
        q_s[...], k_s[...], v_s[...] = p["q"], p["k"], p["v"]
        bb_s[...] = jnp.broadcast_to(p["beta"], (s, HD))
        gb_s[...] = jnp.broadcast_to(p["g"], (s, HD))
        def local(*args):
            u, w, attn, qd, kd, cd = _dn_local(*args)
            return u, w, _attn_pairs(attn), qd, kd, cd

        _dn_local_pass(local, s, [q_s, k_s, v_s, bb_s, gb_s], [v_s, bb_s, at_s, q_s, k_s, cd_s])

        def chunk_pair(pi, state):
            for odd in (0, 1):
                ci = 2 * pi + odd
                sl = pl.ds(pl.multiple_of(ci * DN_CHUNK, DN_CHUNK), DN_CHUNK)
                st_ref[ci] = state
                o, state = _dn_step(v_s[sl, :], bb_s[sl, :], at_s[sl, :], q_s[sl, :], k_s[sl, :], cd_s[sl, :],
                                    state, odd)
                o_ref[sl, :] = o
            return state

        lax.fori_loop(0, nc // 2, chunk_pair, jnp.zeros((HD, HD), F32))
        o = o_ref[...]
        z = z_ref[...]
        on = o * lax.rsqrt(jnp.mean(o * o, axis=1, keepdims=True) + EPS) * gain_ref[...]
        y_ref[...] = on * (z * _sigmoid(z))

    out = pl.BlockSpec((s, HD), lambda h, b: (b, h))
    sds = jax.ShapeDtypeStruct((nb * s, BW), F32)
    return pl.pallas_call(
        body, grid=(NH, nb),
        in_specs=[col(CB_DNQ), col(CB_DNK), col(CB_DNV), col(CB_DNZ), pl.BlockSpec((s, LANE), lambda h, b: (b, 0)),
                  conv(0), conv(NH), conv(2 * NH), pl.BlockSpec((8, LANE), lambda h, b: (0, 0)),
                  pl.BlockSpec((1, HD), lambda h, b: (0, 0))],
        out_specs=[out, out, pl.BlockSpec((None, None, nc, HD, HD), lambda h, b: (b, h, 0, 0, 0))],
        out_shape=[sds, sds, jax.ShapeDtypeStruct((nb, NH, nc, HD, HD), F32)],
        scratch_shapes=[pltpu.VMEM((s, HD), F32)] * 7,
        compiler_params=_cp(("parallel", "parallel")), name=name)(
            proj, proj, proj, proj, ab, conv_w, conv_w, conv_w, par, gain)


def _dn_bwd(proj, ab, conv_w, par, gain, o_pre, states, dy, nb, s, *, name):
    nc = s // DN_CHUNK
    col, conv = _dn_specs(nb, s)

    def body(qr_ref, kr_ref, vr_ref, z_ref, ab_ref, cq_ref, ck_ref, cv_ref, par_ref, gain_ref, o_ref, st_ref, dy_ref,
             dqr_ref, dkr_ref, dvr_ref, dz_ref, dab_ref, dcq_ref, dck_ref, dcv_ref, dpar_ref, dgain_ref,
             q_s, k_s, v_s, bb_s, gb_s, do_s, u_s, w_s, qd_s, kd_s, at_s, cd_s):
        head, b = pl.program_id(0), pl.program_id(1)
        p = _dn_prep(qr_ref, kr_ref, vr_ref, ab_ref, cq_ref, ck_ref, cv_ref, par_ref, head)
        q_s[...], k_s[...], v_s[...] = p["q"], p["k"], p["v"]
        bb_s[...] = jnp.broadcast_to(p["beta"], (s, HD))
        gb_s[...] = jnp.broadcast_to(p["g"], (s, HD))

        @pl.when(b == 0)
        def _():
            for ref in (dcq_ref, dck_ref, dcv_ref, dpar_ref):
                ref[...] = jnp.zeros_like(ref)

        @pl.when((b == 0) & (head == 0))
        def _():
            dgain_ref[...] = jnp.zeros_like(dgain_ref)

        o, z, dy = o_ref[...], z_ref[...], dy_ref[...]
        rstd = lax.rsqrt(jnp.mean(o * o, axis=1, keepdims=True) + EPS)
        ohat = o * rstd
        sgz = _sigmoid(z)
        dz_ref[...] = dy * (ohat * gain_ref[...]) * (sgz * (1.0 + z * (1.0 - sgz)))
        don = dy * (z * sgz)
        dgain_ref[...] += jnp.sum(don * ohat, axis=0, keepdims=True)
        dxh = don * gain_ref[...]
        do_s[...] = rstd * (dxh - ohat * jnp.mean(dxh * ohat, axis=1, keepdims=True))

        def local(*args):
            u, w, attn, qd, kd, cd = _dn_local(*args)
            return u, w, _attn_pairs(attn), qd, kd, cd

        local_refs = [u_s, w_s, at_s, qd_s, kd_s, cd_s]
        _dn_local_pass(local, s, [q_s, k_s, v_s, bb_s, gb_s], local_refs)

        def chunk_pair(pr, dstate):
            for odd in (1, 0):
                ci = nc - 1 - 2 * pr - (1 - odd)
                sl = pl.ds(pl.multiple_of(ci * DN_CHUNK, DN_CHUNK), DN_CHUNK)
                _, vjp = jax.vjp(functools.partial(_dn_step, odd=odd), u_s[sl, :], w_s[sl, :], at_s[sl, :],
                                 qd_s[sl, :], kd_s[sl, :], cd_s[sl, :], st_ref[ci])
                du, dw, dat, dqd, dkd, dcd, dstate = vjp((do_s[sl, :], dstate))
                u_s[sl, :], w_s[sl, :], at_s[sl, :], qd_s[sl, :], kd_s[sl, :], cd_s[sl, :] = du, dw, dat, dqd, dkd, dcd
            return dstate

        lax.fori_loop(0, nc // 2, chunk_pair, jnp.zeros((HD, HD), F32))

        def local_bwd(q, k, v, bb, gb, du, dw, dat, dqd, dkd, dcd):
            _, vjp = jax.vjp(_dn_local, q, k, v, bb, gb)
            dq, dk, dv, dbb, dgb = vjp((du, dw, _attn_unpairs(dat), dqd, dkd, dcd))
            return (dq, dk, dv, jnp.broadcast_to(jnp.sum(dbb, axis=1, keepdims=True), (DN_SUPER, HD)),
                    jnp.broadcast_to(jnp.sum(dgb, axis=1, keepdims=True), (DN_SUPER, HD)))

        _dn_local_pass(local_bwd, s, [q_s, k_s, v_s, bb_s, gb_s] + local_refs, [q_s, k_s, v_s, bb_s, gb_s])

        dq, dk, dv = q_s[...], k_s[...], v_s[...]
        qs, ks, rq, rk = p["qs"], p["ks"], p["rq"], p["rk"]
        dqs = (HD ** -0.5) * (rq * dq - qs * (rq * rq * rq) * jnp.sum(dq * qs, axis=1, keepdims=True))
        dks = rk * dk - ks * (rk * rk * rk) * jnp.sum(dk * ks, axis=1, keepdims=True)
        dqr_ref[...] = _conv_silu_bwd(qr_ref[...], cq_ref[...], p["qc"], dqs, dcq_ref)
        dkr_ref[...] = _conv_silu_bwd(kr_ref[...], ck_ref[...], p["kc"], dks, dck_ref)
        dvr_ref[...] = _conv_silu_bwd(vr_ref[...], cv_ref[...], p["vc"], dv, dcv_ref)

        dbeta, dg = bb_s[:, 0:1], gb_s[:, 0:1]
        beta = p["beta"]
        db_logit = dbeta * beta * (1.0 - beta)
        da = dg * p["neg_ea"] * _sigmoid(p["a_in"])
        lane = lax.broadcasted_iota(jnp.int32, (s, LANE), 1)
        dab_ref[...] = jnp.where(lane == head, da, 0.0) + jnp.where(lane == NH + head, db_logit, 0.0)
        dpar_ref[0:1, :] += jnp.broadcast_to(jnp.sum(dg * p["g"], axis=0, keepdims=True), (1, LANE))
        dpar_ref[1:2, :] += jnp.broadcast_to(jnp.sum(da, axis=0, keepdims=True), (1, LANE))

    out = pl.BlockSpec((s, HD), lambda h, b: (b, h))
    in_blk = pl.BlockSpec((s, HD), lambda h, b: (b, h), pipeline_mode=ONE_BUF)
    cblk = pl.BlockSpec((DN_CONV_W, HD), lambda h, b: (0, h))
    sds = jax.ShapeDtypeStruct((nb * s, BW), F32)
    csds = jax.ShapeDtypeStruct((DN_CONV_W, BW), F32)
    return pl.pallas_call(
        body, grid=(NH, nb),
        in_specs=[col(CB_DNQ), col(CB_DNK), col(CB_DNV), col(CB_DNZ),
                  pl.BlockSpec((s, LANE), lambda h, b: (b, 0), pipeline_mode=ONE_BUF),
                  conv(0), conv(NH), conv(2 * NH), pl.BlockSpec((8, LANE), lambda h, b: (0, 0)),
                  pl.BlockSpec((1, HD), lambda h, b: (0, 0)), in_blk,
                  pl.BlockSpec((None, None, nc, HD, HD), lambda h, b: (b, h, 0, 0, 0), pipeline_mode=ONE_BUF), in_blk],
        out_specs=[out, out, out, out, pl.BlockSpec((None, s, LANE), lambda h, b: (h, b, 0)), cblk, cblk, cblk,
                   pl.BlockSpec((None, 8, LANE), lambda h, b: (h, 0, 0)), pl.BlockSpec((1, HD), lambda h, b: (0, 0))],
        out_shape=[sds, sds, sds, sds, jax.ShapeDtypeStruct((NH, nb * s, LANE), F32), csds, csds, csds,
                   jax.ShapeDtypeStruct((NH, 8, LANE), F32), jax.ShapeDtypeStruct((1, HD), F32)],
        scratch_shapes=[pltpu.VMEM((s, HD), F32)] * 12,
        compiler_params=_cp(("arbitrary", "arbitrary"), DN_BWD_VMEM), name=name)(
            proj, proj, proj, proj, ab, conv_w, conv_w, conv_w, par, gain, o_pre, states, dy)


def _sum_heads(x, *, name):
    nh, t, c = x.shape
    tm = _pick(t, (1024, 512, 256, 128))

    def body(x_ref, o_ref):
        o_ref[...] = (x_ref[0] + x_ref[1] + x_ref[2] + x_ref[3]).astype(BF16)

    return pl.pallas_call(
        body, grid=(t // tm,), in_specs=[pl.BlockSpec((nh, tm, c), lambda i: (0, i, 0))],
        out_specs=pl.BlockSpec((tm, c), lambda i: (i, 0)), out_shape=jax.ShapeDtypeStruct((t, c), BF16),
        compiler_params=_cp(("parallel",)), name=name)(x)


MERGE_TM = 256


def _merge_fwd(x, proj, yp, yd, ys, b_gate, wb, wo, *, name):
    t, d = x.shape
    tm = _pick(t, (MERGE_TM, 128))

    def body(x_ref, g0_ref, g1_ref, g2_ref, yp_ref, yd_ref, ys_ref, bg_ref, wb_ref, wo_ref, o_ref):
        merged = jnp.zeros((tm, d), F32)
        for n, (g_ref, y_ref) in enumerate(((g0_ref, yp_ref), (g1_ref, yd_ref), (g2_ref, ys_ref))):
            gate = _sigmoid(g_ref[...] + bg_ref[:, n * d:(n + 1) * d])
            merged = merged + gate * _bdot(y_ref[...], wb_ref[n])
        o_ref[...] = x_ref[...] + _bdot(merged, wo_ref[...])

    row = pl.BlockSpec((tm, d), lambda i: (i, 0))
    yblk = pl.BlockSpec((tm, BW), lambda i: (i, 0))

    def gl(n):
        return pl.BlockSpec((tm, d), lambda i: (i, CB_GATE + n))

    return pl.pallas_call(
        body, grid=(t // tm,),
        in_specs=[row, gl(0), gl(1), gl(2), yblk, yblk, yblk, pl.BlockSpec((1, 3 * d), lambda i: (0, 0)),
                  pl.BlockSpec((3, BW, d), lambda i: (0, 0, 0)), pl.BlockSpec((d, d), lambda i: (0, 0))],
        out_specs=row, out_shape=jax.ShapeDtypeStruct((t, d), F32),
        compiler_params=_cp(("parallel",)), name=name)(x, proj, proj, proj, yp, yd, ys, b_gate, wb, wo)


def _merge_bwd(proj, yp, yd, ys, b_gate, wb, wo, dx, *, name):
    t, d = dx.shape
    tm = _pick(t, (MERGE_TM, 128))

    def body(g0_ref, g1_ref, g2_ref, yp_ref, yd_ref, ys_ref, bg_ref, wb_ref, wo_ref, dx_ref,
             dyp_ref, dyd_ref, dys_ref, dgl_ref, mg_ref, dxh_ref, dbd_ref, dbg_ref):
        dxh = dx_ref[...].astype(BF16)
        dxh_ref[...] = dxh
        dmerged = _bdot(dxh, wo_ref[...], NT_DIMS)
        merged = jnp.zeros((tm, d), F32)

        @pl.when(pl.program_id(0) == 0)
        def _():
            dbg_ref[...] = jnp.zeros_like(dbg_ref)

        for n, (g_ref, y_ref, dy_ref) in enumerate(((g0_ref, yp_ref, dyp_ref), (g1_ref, yd_ref, dyd_ref),
                                                    (g2_ref, ys_ref, dys_ref))):
            gate = _sigmoid(g_ref[...] + bg_ref[:, n * d:(n + 1) * d])
            bd = _bdot(y_ref[...], wb_ref[n])
            merged = merged + gate * bd
            dgl = dmerged * bd * gate * (1.0 - gate)
            dgl_ref[:, n * d:(n + 1) * d] = dgl.astype(BF16)
            dbg_ref[:, n * d:(n + 1) * d] += jnp.sum(dgl, axis=0, keepdims=True)
            dbd = (dmerged * gate).astype(BF16)
            dbd_ref[n] = dbd
            dy_ref[...] = _bdot(dbd, wb_ref[n], NT_DIMS)
        mg_ref[...] = merged.astype(BF16)

    row = pl.BlockSpec((tm, d), lambda i: (i, 0))
    yblk = pl.BlockSpec((tm, BW), lambda i: (i, 0))
    bgv = pl.BlockSpec((1, 3 * d), lambda i: (0, 0))

    def gl(n):
        return pl.BlockSpec((tm, d), lambda i: (i, CB_GATE + n))

    ysds = jax.ShapeDtypeStruct((t, BW), F32)
    return pl.pallas_call(
        body, grid=(t // tm,),
        in_specs=[gl(0), gl(1), gl(2), yblk, yblk, yblk, bgv,
                  pl.BlockSpec((3, BW, d), lambda i: (0, 0, 0)), pl.BlockSpec((d, d), lambda i: (0, 0)), row],
        out_specs=[yblk, yblk, yblk, pl.BlockSpec((tm, 3 * d), lambda i: (i, 0)), row, row,
                   pl.BlockSpec((3, tm, d), lambda i: (0, i, 0)), bgv],
        out_shape=[ysds, ysds, ysds, jax.ShapeDtypeStruct((t, 3 * d), BF16), jax.ShapeDtypeStruct((t, d), BF16),
                   jax.ShapeDtypeStruct((t, d), BF16), jax.ShapeDtypeStruct((3, t, d), BF16),
                   jax.ShapeDtypeStruct((1, 3 * d), F32)],
        compiler_params=_cp(("arbitrary",)), name=name)(proj, proj, proj, yp, yd, ys, b_gate, wb, wo, dx)


def _loss_head(x, g, target, *, name):
    t, d = x.shape
    tm = _pick(t, (512, 256, 128))

    def body(x_ref, g_ref, t_ref, dx_ref, dg_ref, loss_ref):
        xhat, rstd = _rms_stats(x_ref[...])
        err = xhat * g_ref[...] - t_ref[...]
        dx, dg = _rms_bwd_vals(err * (1.0 / d), xhat, rstd, g_ref[...])
        dx_ref[...] = dx

        @pl.when(pl.program_id(0) == 0)
        def _():
            dg_ref[...] = jnp.zeros_like(dg_ref)
            loss_ref[...] = jnp.zeros_like(loss_ref)

        dg_ref[...] += dg
        part = jnp.sum(jnp.sum(err * err, axis=1, keepdims=True), axis=0, keepdims=True) * (0.5 / d)
        loss_ref[...] += jnp.broadcast_to(part, (1, LANE))

    row = pl.BlockSpec((tm, d), lambda i: (i, 0))
    vec = pl.BlockSpec((1, d), lambda i: (0, 0))
    return pl.pallas_call(
        body, grid=(t // tm,), in_specs=[row, vec, row],
        out_specs=[row, vec, pl.BlockSpec((1, LANE), lambda i: (0, 0))],
        out_shape=[jax.ShapeDtypeStruct((t, d), F32), jax.ShapeDtypeStruct((1, d), F32),
                   jax.ShapeDtypeStruct((1, LANE), F32)],
        compiler_params=_cp(("arbitrary",)), name=name)(x, g.reshape(1, d), target)


def _adamw(w, g, m, v, *, name):
    rows, cols = w.shape
    fits = [c for c in (1024, 704, 512, 352, 256, 128, 64, 32, 16, 8) if c * cols * 4 * 14 <= VMEM_LIMIT // 2]
    tr = _pick(rows, fits)
    c1 = 1.0 / (1.0 - ADAM_B1 ** ADAM_STEP)
    c2 = 1.0 / (1.0 - ADAM_B2 ** ADAM_STEP)

    def body(w_ref, g_ref, m_ref, v_ref, d_ref, nm_ref, nv_ref):
        g = g_ref[...]
        nm = ADAM_B1 * m_ref[...] + (1.0 - ADAM_B1) * g
        nv = ADAM_B2 * v_ref[...] + (1.0 - ADAM_B2) * (g * g)
        nm_ref[...] = nm
        nv_ref[...] = nv
        d_ref[...] = -ADAM_LR * ((nm * c1) / (jnp.sqrt(nv * c2) + ADAM_EPS) + ADAM_WD * w_ref[...])

    blk = pl.BlockSpec((tr, cols), lambda i: (i, 0))
    sds = jax.ShapeDtypeStruct((rows, cols), F32)
    return pl.pallas_call(
        body, grid=(rows // tr,), in_specs=[blk] * 4, out_specs=[blk] * 3, out_shape=[sds] * 3,
        compiler_params=_cp(("parallel",)), name=name)(w, g, m, v)


MESH_ID = pl.DeviceIdType.MESH
HBM_SPEC = pl.BlockSpec(memory_space=pl.ANY)
OTHER_CHIPS = ((1, 0), (0, 1), (1, 1))


def _at_slot(ref, nl, slot):
    return ref.at[(slice(None),) * nl + (slot,)]


def _slotted(shape, nl, slots):
    return tuple(shape[:nl]) + (slots,) + tuple(shape[nl:])


def _flip(v, f):
    return 1 - v if f else v


def _comm_call(body, n, out_shapes, n_remote, args, name):
    return pl.pallas_call(
        body, out_shape=out_shapes, in_specs=[HBM_SPEC] * len(args), out_specs=[HBM_SPEC] * len(out_shapes),
        scratch_shapes=[pltpu.SemaphoreType.DMA((n * n_remote,)), pltpu.SemaphoreType.DMA((n * n_remote,)),
                        pltpu.SemaphoreType.DMA((n * 4,))],
        compiler_params=pltpu.CompilerParams(has_side_effects=True), name=name)(*args)


def _gather(xs, nls, *, name):
    n = len(xs)

    def body(*refs):
        x_refs, o_refs, (send_sems, recv_sems, local_sems) = refs[:n], refs[n:2 * n], refs[2 * n:]
        x, y, c = lax.axis_index("x"), lax.axis_index("y"), lax.axis_index("c")
        me, sibling = (x, y, c), (x, y, 1 - c)
        chips = [(_flip(x, fx), _flip(y, fy)) for fx, fy in OTHER_CHIPS]

        def copy(a, k, block, to, src=None):
            dst = _at_slot(o_refs[a], nls[a], 4 * block[0] + 2 * block[1] + block[2])
            return pltpu.make_async_remote_copy(
                src_ref=dst if src is None else src, dst_ref=dst, send_sem=send_sems.at[a * 7 + k],
                recv_sem=recv_sems.at[a * 7 + k], device_id=to, device_id_type=MESH_ID)

        mine = [pltpu.make_async_copy(x_refs[a], _at_slot(o_refs[a], nls[a], 4 * x + 2 * y + c), local_sems.at[a])
                for a in range(n)]
        sent = []
        for a in range(n):
            mine[a].start()
            sent.append(copy(a, 0, me, sibling, src=x_refs[a]))
            sent += [copy(a, 1 + j, me, (*chip, c), src=x_refs[a]) for j, chip in enumerate(chips)]
        for cp in sent:
            cp.start()
        for j, chip in enumerate(chips):
            for a in range(n):
                copy(a, 1 + j, (*chip, c), me).wait_recv()
                passed = copy(a, 4 + j, (*chip, c), sibling)
                passed.start()
                sent.append(passed)
        for a in range(n):
            copy(a, 0, sibling, me).wait_recv()
            for j, chip in enumerate(chips):
                copy(a, 4 + j, (*chip, 1 - c), me).wait_recv()
        for cp in sent:
            cp.wait_send()
        for cp in mine:
            cp.wait()

    outs = [jax.ShapeDtypeStruct(_slotted(v.shape, nl, N_DEV), v.dtype) for v, nl in zip(xs, nls)]
    return _comm_call(body, n, outs, 7, xs, name)


def _scatter_pair(gs, nls, *, name):
    n = len(gs)

    def body(*refs):
        g_refs, got_refs, (send_sems, recv_sems, _) = refs[:n], refs[n:2 * n], refs[2 * n:]
        x, y, c = lax.axis_index("x"), lax.axis_index("y"), lax.axis_index("c")
        remote = []
        for a in range(n):
            for q in range(4):
                rc = pltpu.make_async_remote_copy(
                    src_ref=_at_slot(g_refs[a], nls[a], 2 * q + 1 - c), dst_ref=_at_slot(got_refs[a], nls[a], q),
                    send_sem=send_sems.at[a * 4 + q], recv_sem=recv_sems.at[a * 4 + q], device_id=(x, y, 1 - c),
                    device_id_type=MESH_ID)
                rc.start()
                remote.append(rc)
        for rc in remote:
            rc.wait_recv()
        for rc in remote:
            rc.wait_send()

    outs = [jax.ShapeDtypeStruct(_slotted(v.shape[:nl] + v.shape[nl + 1:], nl, 4), v.dtype) for v, nl in zip(gs, nls)]
    return _comm_call(body, n, outs, 4, gs, name)


def _scatter_chips(ps, nls, *, name):
    n = len(ps)

    def body(*refs):
        p_refs, r_refs, (send_sems, recv_sems, _) = refs[:n], refs[n:2 * n], refs[2 * n:]
        x, y, c = lax.axis_index("x"), lax.axis_index("y"), lax.axis_index("c")
        remote = []
        for a in range(n):
            for k, (fx, fy) in enumerate(OTHER_CHIPS):
                tx, ty = _flip(x, fx), _flip(y, fy)
                rc = pltpu.make_async_remote_copy(
                    src_ref=_at_slot(p_refs[a], nls[a], 2 * tx + ty), dst_ref=_at_slot(r_refs[a], nls[a], k),
                    send_sem=send_sems.at[a * 3 + k], recv_sem=recv_sems.at[a * 3 + k], device_id=(tx, ty, c),
                    device_id_type=MESH_ID)
                rc.start()
                remote.append(rc)
        for rc in remote:
            rc.wait_recv()
        for rc in remote:
            rc.wait_send()

    outs = [jax.ShapeDtypeStruct(_slotted(v.shape[:nl] + v.shape[nl + 1:], nl, 3), v.dtype) for v, nl in zip(ps, nls)]
    return _comm_call(body, n, outs, 3, ps, name)


def _pair_add(g, got, core, *, name):
    rows, cols = g.shape[-2:]
    lf = math.prod(got.shape[:-3])
    tr = _pick(rows, (1024, 512, 352, 256, 128))

    def body(core_ref, g_ref, got_ref, o_ref):
        o_ref[...] = (g_ref[...].astype(F32) + got_ref[...].astype(F32)).astype(BF16)

    blk = pl.BlockSpec((None, None, tr, cols), lambda i, q, j, core_ref: (i, q, j, 0))
    out = pl.pallas_call(
        body, grid_spec=pltpu.PrefetchScalarGridSpec(
            num_scalar_prefetch=1, grid=(lf, 4, rows // tr),
            in_specs=[pl.BlockSpec((None, None, None, tr, cols), lambda i, q, j, core_ref: (i, q, core_ref[0], j, 0)),
                      blk], out_specs=blk),
        out_shape=jax.ShapeDtypeStruct((lf, 4, rows, cols), BF16),
        compiler_params=_cp(("parallel", "parallel", "parallel")), name=name)(
            core, g.reshape(lf, 4, 2, rows, cols), got.reshape(lf, 4, rows, cols))
    return out.reshape(got.shape)


def _sum_adamw(p, r, chip, w, m, v, *, name):
    shape = w.shape
    rows, cols = shape[-2:]
    lf = math.prod(shape[:-2])
    fits = [c for c in (1024, 512, 352, 256, 128, 64, 32, 16) if c * cols * (7 * 4 + 4 * 2) * 2 <= VMEM_LIMIT // 2]
    tr = _pick(rows, fits)
    c1 = 1.0 / (1.0 - ADAM_B1 ** ADAM_STEP)
    c2 = 1.0 / (1.0 - ADAM_B2 ** ADAM_STEP)

    def body(chip_ref, p_ref, r_ref, w_ref, m_ref, v_ref, g_ref, d_ref, nm_ref, nv_ref):
        g = p_ref[...].astype(F32)
        for k in range(3):
            g = g + r_ref[k].astype(F32)
        g_ref[...] = g
        nm = ADAM_B1 * m_ref[...] + (1.0 - ADAM_B1) * g
        nv = ADAM_B2 * v_ref[...] + (1.0 - ADAM_B2) * (g * g)
        nm_ref[...] = nm
        nv_ref[...] = nv
        d_ref[...] = -ADAM_LR * ((nm * c1) / (jnp.sqrt(nv * c2) + ADAM_EPS) + ADAM_WD * w_ref[...])

    blk = pl.BlockSpec((None, tr, cols), lambda i, j, chip_ref: (i, j, 0))
    sds = jax.ShapeDtypeStruct((lf, rows, cols), F32)
    outs = pl.pallas_call(
        body, grid_spec=pltpu.PrefetchScalarGridSpec(
            num_scalar_prefetch=1, grid=(lf, rows // tr),
            in_specs=[pl.BlockSpec((None, None, tr, cols), lambda i, j, chip_ref: (i, chip_ref[0], j, 0)),
                      pl.BlockSpec((None, 3, tr, cols), lambda i, j, chip_ref: (i, 0, j, 0))] + [blk] * 3,
            out_specs=[blk] * 4),
        out_shape=[sds] * 4, compiler_params=_cp(("parallel", "parallel")), name=name)(
            chip, p.reshape(lf, 4, rows, cols), r.reshape(lf, 3, rows, cols), w.reshape(lf, rows, cols),
            m.reshape(lf, rows, cols), v.reshape(lf, rows, cols))
    return [o.reshape(shape) for o in outs]


def _sum_slots(x, *, name):
    nd, rows, cols = x.shape
    tr = _pick(rows, (512, 256, 128, 64, 32, 16, 8))

    def body(x_ref, o_ref):
        acc = x_ref[0].astype(F32)
        for j in range(1, nd):
            acc = acc + x_ref[j].astype(F32)
        o_ref[...] = acc

    return pl.pallas_call(
        body, grid=(rows // tr,), in_specs=[pl.BlockSpec((nd, tr, cols), lambda i: (0, i, 0))],
        out_specs=pl.BlockSpec((tr, cols), lambda i: (i, 0)), out_shape=jax.ShapeDtypeStruct((rows, cols), F32),
        compiler_params=_cp(("parallel",)), name=name)(x)


def _pad_rows(a, mult=8):
    r = (-a.shape[0]) % mult
    return jnp.pad(a, ((0, r), (0, 0))) if r else a


def _flat128(a):
    f = a.reshape(-1)
    return jnp.pad(f, (0, (-f.shape[0]) % LANE)).reshape(-1, LANE)


def _unshard(gathered, shape, axis):
    g = gathered.reshape((N_DEV,) + tuple(shape))
    g = jnp.moveaxis(g, 0, axis)
    full = list(shape)
    full[axis] *= N_DEV
    return g.reshape(full)


def _col_shards(full):
    rows, cols = full.shape
    return jnp.moveaxis(full.reshape(rows, N_DEV, cols // N_DEV), 1, 0)


BIG = (("ffn_w_gate", 2), ("ffn_w_up", 2), ("ffn_w_down", 2), ("w_in", 1), ("w_branch", 2), ("w_out", 1))


def kernel(x, ffn_norm, ffn_w_gate, ffn_w_up, ffn_w_down, mix_norm, w_in, b_gate, pool_w, pool_scale, dn_conv, dn_A_log, dn_dt_bias, dn_out_norm, w_branch, w_out, final_norm, loss_target, m_ffn_norm, m_ffn_w_gate, m_ffn_w_up, m_ffn_w_down, m_mix_norm, m_w_in, m_b_gate, m_pool_w, m_pool_scale, m_dn_conv, m_dn_A_log, m_dn_dt_bias, m_dn_out_norm, m_w_branch, m_w_out, m_final_norm, v_ffn_norm, v_ffn_w_gate, v_ffn_w_up, v_ffn_w_down, v_mix_norm, v_w_in, v_b_gate, v_pool_w, v_pool_scale, v_dn_conv, v_dn_A_log, v_dn_dt_bias, v_dn_out_norm, v_w_branch, v_w_out, v_final_norm):
    wts = dict(ffn_norm=ffn_norm, ffn_w_gate=ffn_w_gate, ffn_w_up=ffn_w_up, ffn_w_down=ffn_w_down, mix_norm=mix_norm,
               w_in=w_in, b_gate=b_gate, pool_w=pool_w, pool_scale=pool_scale, dn_conv=dn_conv, dn_A_log=dn_A_log,
               dn_dt_bias=dn_dt_bias, dn_out_norm=dn_out_norm, w_branch=w_branch, w_out=w_out, final_norm=final_norm)
    mom = dict(ffn_norm=m_ffn_norm, ffn_w_gate=m_ffn_w_gate, ffn_w_up=m_ffn_w_up, ffn_w_down=m_ffn_w_down,
               mix_norm=m_mix_norm, w_in=m_w_in, b_gate=m_b_gate, pool_w=m_pool_w, pool_scale=m_pool_scale,
               dn_conv=m_dn_conv, dn_A_log=m_dn_A_log, dn_dt_bias=m_dn_dt_bias, dn_out_norm=m_dn_out_norm,
               w_branch=m_w_branch, w_out=m_w_out, final_norm=m_final_norm)
    var = dict(ffn_norm=v_ffn_norm, ffn_w_gate=v_ffn_w_gate, ffn_w_up=v_ffn_w_up, ffn_w_down=v_ffn_w_down,
               mix_norm=v_mix_norm, w_in=v_w_in, b_gate=v_b_gate, pool_w=v_pool_w, pool_scale=v_pool_scale,
               dn_conv=v_dn_conv, dn_A_log=v_dn_A_log, dn_dt_bias=v_dn_dt_bias, dn_out_norm=v_dn_out_norm,
               w_branch=v_w_branch, w_out=v_w_out, final_norm=v_final_norm)
    nb, s, d = x.shape
    t = nb * s
    me = 4 * lax.axis_index("x") + 2 * lax.axis_index("y") + lax.axis_index("c")

    nls = [nl for _, nl in BIG]
    small_sh = jnp.concatenate([_flat128(ffn_norm), _flat128(dn_conv)], axis=0)
    *gat, small_g = _gather([wts[n].astype(BF16) for n, _ in BIG] + [small_sh], nls + [0], name="gather_weights")
    full = dict(zip([n for n, _ in BIG], gat))
    w_in_full = jnp.moveaxis(full["w_in"], 1, 2).reshape(DEPTH, d, -1)
    w_main = jnp.concatenate([w_in_full[:, :, :AB_LO], w_in_full[:, :, AB_HI:]], axis=2)
    w_ab = jnp.pad(w_in_full[:, :, AB_LO:AB_HI], ((0, 0), (0, 0), (0, LANE - (AB_HI - AB_LO))))
    wb_full = jnp.moveaxis(full["w_branch"], 2, 3).reshape(DEPTH, 3, BW, d)
    wo_full = full["w_out"].reshape(DEPTH, d, d)
    nfr = ffn_norm.size // LANE
    ffn_norm_full = _unshard(small_g[:, :nfr], ffn_norm.shape, 2)
    dn_conv_full = _unshard(small_g[:, nfr:], dn_conv.shape, 2)
    pool_w_h = pool_w.astype(BF16)

    xs = x.reshape(t, d)
    saved = []
    for l in range(DEPTH):
        sv = dict(x0=xs)
        xs = _ffn_fwd(xs, ffn_norm_full[l, 0], full["ffn_w_gate"][l, 0], full["ffn_w_up"][l, 0],
                      full["ffn_w_down"][l, 0], name="ffn_fwd")
        sv["x1"] = xs
        h = _rms_fwd(xs, mix_norm[l], name="mix_rms")
        proj = _mm(h, w_main[l], name="proj")
        ab = _mm(h, w_ab[l], name="proj_ab")
        par = jnp.pad(jnp.stack([dn_A_log[l], dn_dt_bias[l]]), ((0, 6), (0, LANE - NH)))
        gain = dn_out_norm[l].reshape(1, HD)
        psc = pool_scale[l].reshape(1, BW)
        yp = _pool_fwd(proj, pool_w_h[l], psc, nb, s, name="pool_fwd")
        yd, o_pre, states = _dn_fwd(proj, ab, dn_conv_full[l], par, gain, nb, s, name="dn_fwd")
        ys = _sb_fwd(proj, nb, s, name="sb_fwd")
        bg = b_gate[l].reshape(1, 3 * d)
        xs = _merge_fwd(xs, proj, yp, yd, ys, bg, wb_full[l], wo_full[l], name="merge_fwd")
        sv.update(x2=xs, h=h, proj=proj, ab=ab, par=par, gain=gain, psc=psc, yp=yp, yd=yd, ys=ys, o_pre=o_pre,
                  states=states, bg=bg)
        xs = _ffn_fwd(xs, ffn_norm_full[l, 1], full["ffn_w_gate"][l, 1], full["ffn_w_up"][l, 1],
                      full["ffn_w_down"][l, 1], name="ffn_fwd")
        saved.append(sv)

    dx, g_final, loss_row = _loss_head(xs, final_norm, loss_target.reshape(t, d), name="loss_head")
    loss = lax.psum(loss_row[0, 0], ("x", "y", "c"))

    gw = {n: [None] * DEPTH for n in ("ffn_norm", "ffn_w_gate", "ffn_w_up", "ffn_w_down", "mix_norm", "w_in", "b_gate",
                                      "pool_w", "pool_scale", "dn_conv", "dn_A_log", "dn_dt_bias", "dn_out_norm",
                                      "w_branch", "w_out")}

    def ffn_back(l, i, x_in, dy):
        dxi, dg, hb, dyh, da, db, sact = _ffn_bwd(x_in, ffn_norm_full[l, i], full["ffn_w_gate"][l, i],
                                                  full["ffn_w_up"][l, i], full["ffn_w_down"][l, i], dy, name="ffn_bwd")
        return dxi, dg, (_mm_tn_slots(hb, da, name="dw_gate_up"), _mm_tn_slots(hb, db, name="dw_gate_up"),
                         _mm_tn_slots(sact, dyh, name="dw_down"))

    for l in reversed(range(DEPTH)):
        sv = saved[l]
        dx, dg1, (dwg1, dwu1, dwd1) = ffn_back(l, 1, sv["x2"], dx)
        dyp, dyd, dys, dgl, merged, dxh, dbd, dbg = _merge_bwd(sv["proj"], sv["yp"], sv["yd"], sv["ys"], sv["bg"],
                                                               wb_full[l], wo_full[l], dx,
                                                               name="merge_bwd")
        gw["w_out"][l] = _mm(merged, dxh, ta=True, out_dtype=BF16, name="dw_out").reshape(N_DEV, d // N_DEV, d)
        gw["w_branch"][l] = jnp.stack([_col_shards(_mm(y, dbd[n], ta=True, out_dtype=BF16, name="dw_branch"))
                                       for n, y in enumerate((sv["yp"], sv["yd"], sv["ys"]))])
        gw["b_gate"][l] = dbg.reshape(3 * d)
        du, dpw, dps = _pool_bwd(sv["proj"], pool_w_h[l], sv["psc"], dyp, nb, s, name="pool_bwd")
        gw["pool_w"][l], gw["pool_scale"][l] = dpw, dps.reshape(BW)
        dqr, dkr, dvr, dz, dab4, dcq, dck, dcv, dpar, dgain = _dn_bwd(
            sv["proj"], sv["ab"], dn_conv_full[l], sv["par"], sv["gain"], sv["o_pre"], sv["states"], dyd, nb, s,
            name="dn_bwd")
        gw["dn_conv"][l] = jnp.concatenate([dcq, dck, dcv], axis=1)
        gw["dn_A_log"][l], gw["dn_dt_bias"][l], gw["dn_out_norm"][l] = dpar[:, 0, 0], dpar[:, 1, 0], dgain.reshape(HD)
        dsq, dsk, dsv = _sb_bwd(sv["proj"], dys, nb, s, name="sb_bwd")
        dab = _sum_heads(dab4, name="sum_heads")
        dproj = jnp.concatenate([du.astype(BF16), dqr.astype(BF16), dkr.astype(BF16), dvr.astype(BF16),
                                 dz.astype(BF16), dsq.astype(BF16), dsk.astype(BF16), dsv.astype(BF16), dgl], axis=1)
        dw_main = _mm(sv["h"], dproj, ta=True, out_dtype=BF16, name="dw_in")
        dw_ab = _mm(sv["h"], dab, ta=True, out_dtype=BF16, name="dw_ab")
        gw["w_in"][l] = _col_shards(jnp.concatenate([dw_main[:, :AB_LO], dw_ab[:, :AB_HI - AB_LO],
                                                     dw_main[:, AB_LO:]], axis=1))
        dh_main = _mm(dproj, w_main[l], tb=True, name="dh_mix")
        dh_ab = _mm(dab, w_ab[l], tb=True, name="dh_mix_ab")
        dx, dgm = _rms_bwd(sv["x1"], mix_norm[l], dh_main, dh_ab, dx, name="mix_rms_bwd")
        gw["mix_norm"][l] = dgm.reshape(d)
        dx, dg0, (dwg0, dwu0, dwd0) = ffn_back(l, 0, sv["x0"], dx)
        gw["ffn_norm"][l] = jnp.stack([dg0.reshape(d), dg1.reshape(d)])
        gw["ffn_w_gate"][l] = jnp.stack([dwg0, dwg1])
        gw["ffn_w_up"][l] = jnp.stack([dwu0, dwu1])
        gw["ffn_w_down"][l] = jnp.stack([dwd0, dwd1])
    grad_x = dx.reshape(nb, s, d)
    gw = {n: jnp.stack(v) for n, v in gw.items()}
    gw["final_norm"] = g_final.reshape(d)

    big = [n for n, _ in BIG]
    core = lax.axis_index("c").astype(jnp.int32).reshape(1)
    chip = (2 * lax.axis_index("x") + lax.axis_index("y")).astype(jnp.int32).reshape(1)
    got = _scatter_pair([gw[n] for n in big], nls, name="scatter_grads_pair")
    chip_sums = [_pair_add(gw[n], b, core, name="add_pair_" + n) for n, b in zip(big, got)]
    recv = _scatter_chips(chip_sums, nls, name="scatter_grads_chips")
    grads, delta, new_m, new_v = {}, {}, {}, {}
    for n, p, r in zip(big, chip_sums, recv):
        grads[n], delta[n], new_m[n], new_v[n] = _sum_adamw(p, r, chip, wts[n], mom[n], var[n], name="adamw_" + n)

    small = ("ffn_norm", "mix_norm", "b_gate", "pool_w", "pool_scale", "dn_conv", "dn_A_log", "dn_dt_bias",
             "dn_out_norm", "final_norm")
    sp = _pad_rows(jnp.concatenate([_flat128(gw[n]) for n in small], axis=0))
    ssum = _sum_slots(_gather([sp], [0], name="gather_small_grads")[0], name="sum_small_grads")
    off = 0
    for n in small:
        r = -(-gw[n].size // LANE)
        g = ssum[off:off + r].reshape(-1)[:gw[n].size].reshape(gw[n].shape)
        off += r
        if n in ("ffn_norm", "dn_conv"):
            w = wts[n].shape[2]
            g = lax.dynamic_slice_in_dim(g, me * w, w, axis=2)
        grads[n] = g

    pk = lambda src: _pad_rows(jnp.concatenate([_flat128(src[n]) for n in small], axis=0))
    dl, nm, nv = _adamw(pk(wts), pk(grads), pk(mom), pk(var), name="adamw_small")
    off = 0
    for n in small:
        r = -(-wts[n].size // LANE)
        for dst, src in ((delta, dl), (new_m, nm), (new_v, nv)):
            dst[n] = src[off:off + r].reshape(-1)[:wts[n].size].reshape(wts[n].shape)
        off += r

    order = ("ffn_norm", "ffn_w_gate", "ffn_w_up", "ffn_w_down", "mix_norm", "w_in", "b_gate", "pool_w", "pool_scale",
             "dn_conv", "dn_A_log", "dn_dt_bias", "dn_out_norm", "w_branch", "w_out", "final_norm")
    return (loss, grad_x, *[grads[n] for n in order], *[delta[n] for n in order], *[new_m[n] for n in order],
            *[new_v[n] for n in order])
```

```python
import functools
import math

import jax
import jax.numpy as jnp
from jax import lax
from jax.experimental import pallas as pl
from jax.experimental.pallas import tpu as pltpu

F32, BF16 = jnp.float32, jnp.bfloat16
D_MODEL, D_FF, DEPTH = 1024, 2816, 4
BW = 512
HD = 128
NH = 4
DN_CHUNK = 64
EPS = 1e-6
N_DEV = 8
LANE = 128
CB_POOL, CB_DNQ, CB_DNK, CB_DNV, CB_DNZ, CB_SBQ, CB_SBK, CB_SBV = 0, 4, 8, 12, 16, 20, 24, 28
CB_GATE = 4
P_MAIN = 7168
AB_LO, AB_HI = 2560, 2568
ADAM_LR, ADAM_B1, ADAM_B2, ADAM_EPS, ADAM_WD, ADAM_STEP = 0.001, 0.9, 0.999, 1e-08, 0.01, 10
VMEM_LIMIT = 56 * 1024 * 1024
HIGHEST = lax.Precision.HIGHEST
NT_DIMS = (((1,), (1,)), ((), ()))
TN_DIMS = (((0,), (0,)), ((), ()))
NN_DIMS = (((1,), (0,)), ((), ()))


def _cp(dims=None, vmem=VMEM_LIMIT):
    return pltpu.CompilerParams(dimension_semantics=dims, vmem_limit_bytes=vmem)


def _pick(n, cands):
    for c in cands:
        if n % c == 0:
            return c
    return n


def _bdot(a, b, dims=NN_DIMS):
    return lax.dot_general(a.astype(BF16), b.astype(BF16), dims, preferred_element_type=F32)


def _hdot(a, b, dims=NN_DIMS):
    return lax.dot_general(a, b, dims, precision=lax.Precision.HIGH, preferred_element_type=F32)


def _split_dot(x, m01):
    hi = x.astype(BF16)
    lo = (x - hi.astype(F32)).astype(BF16)
    return (lax.dot_general(hi, m01, NN_DIMS, preferred_element_type=F32)
            + lax.dot_general(lo, m01, NN_DIMS, preferred_element_type=F32))


def _sigmoid(x):
    return 1.0 / (1.0 + jnp.exp(-x))


def _log_sigmoid(x):
    return jnp.minimum(x, 0.0) - jnp.log1p(jnp.exp(-jnp.abs(x)))


def _softplus(x):
    return jnp.maximum(x, 0.0) + jnp.log1p(jnp.exp(-jnp.abs(x)))


def _shift_down(x, k):
    r = lax.broadcasted_iota(jnp.int32, x.shape, 0)
    return jnp.where(r >= k, pltpu.roll(x, k, 0), 0.0)


def _shift_up(x, k):
    n = x.shape[0]
    r = lax.broadcasted_iota(jnp.int32, x.shape, 0)
    return jnp.where(r < n - k, pltpu.roll(x, n - k, 0), 0.0)


def _mm(a, b, *, ta=False, tb=False, out_dtype=F32, name):
    (kk, m) = a.shape if ta else a.shape[::-1]
    (k2, n) = b.shape[::-1] if tb else b.shape
    assert kk == k2, (a.shape, b.shape, ta, tb)
    bm = _pick(m, (1024, 512, 256, 128))
    bn = _pick(n, (1024, 1408, 512, 256, 128))
    bk = _pick(kk, (512, 256, 128))
    nk = kk // bk
    dims = (((0 if ta else 1,), (1 if tb else 0,)), ((), ()))

    def body(a_ref, b_ref, o_ref, acc_ref):
        k = pl.program_id(2)

        @pl.when(k == 0)
        def _():
            acc_ref[...] = jnp.zeros_like(acc_ref)

        acc_ref[...] += lax.dot_general(a_ref[...].astype(BF16), b_ref[...].astype(BF16), dims,
                                        preferred_element_type=F32)

        @pl.when(k == nk - 1)
        def _():
            o_ref[...] = acc_ref[...].astype(out_dtype)

    a_spec = (pl.BlockSpec((bk, bm), lambda i, j, k: (k, i)) if ta else pl.BlockSpec((bm, bk), lambda i, j, k: (i, k)))
    b_spec = (pl.BlockSpec((bn, bk), lambda i, j, k: (j, k)) if tb else pl.BlockSpec((bk, bn), lambda i, j, k: (k, j)))
    return pl.pallas_call(
        body, grid=(m // bm, n // bn, nk), in_specs=[a_spec, b_spec],
        out_specs=pl.BlockSpec((bm, bn), lambda i, j, k: (i, j)),
        out_shape=jax.ShapeDtypeStruct((m, n), out_dtype),
        scratch_shapes=[pltpu.VMEM((bm, bn), F32)],
        compiler_params=_cp(("parallel", "parallel", "arbitrary")), name=name)(a, b)


def _mm_slots(a, b, *, name):
    a3, b3 = a.ndim == 3, b.ndim == 3
    ns = a.shape[0] if a3 else b.shape[0]
    m, t = a.shape[-2:]
    n = b.shape[-1]
    bm, bn, bk = _pick(m, (1024, 512, 256, 128)), _pick(n, (1024, 512, 256, 128)), _pick(t, (1024, 512, 256, 128))
    nk = t // bk

    def body(a_ref, b_ref, o_ref, acc_ref):
        k = pl.program_id(3)

        @pl.when(k == 0)
        def _():
            acc_ref[...] = jnp.zeros_like(acc_ref)

        acc_ref[...] += _bdot(a_ref[...], b_ref[...])

        @pl.when(k == nk - 1)
        def _():
            o_ref[...] = acc_ref[...].astype(BF16)

    a_spec = (pl.BlockSpec((None, bm, bk), lambda s, i, j, k: (s, i, k)) if a3
              else pl.BlockSpec((bm, bk), lambda s, i, j, k: (i, k)))
    b_spec = (pl.BlockSpec((None, bk, bn), lambda s, i, j, k: (s, k, j)) if b3
              else pl.BlockSpec((bk, bn), lambda s, i, j, k: (k, j)))
    return pl.pallas_call(
        body, grid=(ns, m // bm, n // bn, nk), in_specs=[a_spec, b_spec],
        out_specs=pl.BlockSpec((None, bm, bn), lambda s, i, j, k: (s, i, j)),
        out_shape=jax.ShapeDtypeStruct((ns, m, n), BF16), scratch_shapes=[pltpu.VMEM((bm, bn), F32)],
        compiler_params=_cp(("parallel", "parallel", "parallel", "arbitrary")), name=name)(a, b)


def _rms_stats(x):
    rstd = lax.rsqrt(jnp.mean(x * x, axis=-1, keepdims=True) + EPS)
    return x * rstd, rstd


def _rms_bwd_vals(dh, xhat, rstd, g):
    dxh = dh * g
    dx = rstd * (dxh - xhat * jnp.mean(dxh * xhat, axis=-1, keepdims=True))
    return dx, jnp.sum(dh * xhat, axis=0, keepdims=True)


def _rms_fwd(x, g, *, name):
    t, d = x.shape
    tm = _pick(t, (512, 256, 128))

    def body(x_ref, g_ref, h_ref):
        xhat, _ = _rms_stats(x_ref[...])
        h_ref[...] = (xhat * g_ref[...]).astype(BF16)

    return pl.pallas_call(
        body, grid=(t // tm,),
        in_specs=[pl.BlockSpec((tm, d), lambda i: (i, 0)), pl.BlockSpec((1, d), lambda i: (0, 0))],
        out_specs=pl.BlockSpec((tm, d), lambda i: (i, 0)), out_shape=jax.ShapeDtypeStruct((t, d), BF16),
        compiler_params=_cp(("parallel",)), name=name)(x, g.reshape(1, d))


def _rms_bwd(x, g, dh_a, dh_b, dres, *, name):
    t, d = x.shape
    tm = _pick(t, (512, 256, 128))

    def body(x_ref, g_ref, dha_ref, dhb_ref, dres_ref, dx_ref, dg_ref):
        xhat, rstd = _rms_stats(x_ref[...])
        dx, dg = _rms_bwd_vals(dha_ref[...] + dhb_ref[...], xhat, rstd, g_ref[...])
        dx_ref[...] = dres_ref[...] + dx

        @pl.when(pl.program_id(0) == 0)
        def _():
            dg_ref[...] = jnp.zeros_like(dg_ref)

        dg_ref[...] += dg

    row = pl.BlockSpec((tm, d), lambda i: (i, 0))
    vec = pl.BlockSpec((1, d), lambda i: (0, 0))
    return pl.pallas_call(
        body, grid=(t // tm,), in_specs=[row, vec, row, row, row], out_specs=[row, vec],
        out_shape=[jax.ShapeDtypeStruct((t, d), F32), jax.ShapeDtypeStruct((1, d), F32)],
        compiler_params=_cp(("arbitrary",)), name=name)(x, g.reshape(1, d), dh_a, dh_b, dres)


FFN_TM = 512


def _ffn_fwd(x, g, wg, wu, wd, *, name):
    t, d = x.shape
    nf, _, fc = wg.shape
    tm = _pick(t, (FFN_TM, 256, 128))

    def body(x_ref, g_ref, wg_ref, wu_ref, wd_ref, o_ref, h_ref, acc_ref):
        j = pl.program_id(1)

        @pl.when(j == 0)
        def _():
            xhat, _ = _rms_stats(x_ref[...])
            h_ref[...] = (xhat * g_ref[...]).astype(BF16)
            acc_ref[...] = jnp.zeros_like(acc_ref)

        h = h_ref[...]
        a = _bdot(h, wg_ref[...])
        b = _bdot(h, wu_ref[...])
        s = a * _sigmoid(a) * b
        acc_ref[...] += _bdot(s, wd_ref[...])

        @pl.when(j == nf - 1)
        def _():
            o_ref[...] = x_ref[...] + 0.5 * acc_ref[...]

    row = pl.BlockSpec((tm, d), lambda i, j: (i, 0))
    return pl.pallas_call(
        body, grid=(t // tm, nf),
        in_specs=[row, pl.BlockSpec((1, d), lambda i, j: (0, 0)),
                  pl.BlockSpec((None, d, fc), lambda i, j: (j, 0, 0)), pl.BlockSpec((None, d, fc), lambda i, j: (j, 0, 0)),
                  pl.BlockSpec((None, fc, d), lambda i, j: (j, 0, 0))],
        out_specs=row, out_shape=jax.ShapeDtypeStruct((t, d), F32),
        scratch_shapes=[pltpu.VMEM((tm, d), BF16), pltpu.VMEM((tm, d), F32)],
        compiler_params=_cp(("parallel", "arbitrary")), name=name)(x, g.reshape(1, d), wg, wu, wd)


def _ffn_bwd(x, g, wg, wu, wd, dy, *, name):
    t, d = x.shape
    nf, _, fc = wg.shape
    tm = _pick(t, (FFN_TM, 256, 128))

    def body(x_ref, g_ref, wg_ref, wu_ref, wd_ref, dy_ref,
             dx_ref, dg_ref, ht_ref, dyh_ref, da_ref, db_ref, st_ref, acc_ref, h_ref):
        i, j = pl.program_id(0), pl.program_id(1)

        @pl.when(j == 0)
        def _():
            xhat, _ = _rms_stats(x_ref[...])
            hf = xhat * g_ref[...]
            h_ref[...] = hf.astype(BF16)
            ht_ref[...] = hf.T.astype(BF16)
            dyh_ref[...] = (0.5 * dy_ref[...]).astype(BF16)
            acc_ref[...] = jnp.zeros_like(acc_ref)

        h = h_ref[...]
        a = _bdot(h, wg_ref[...])
        b = _bdot(h, wu_ref[...])
        sg = _sigmoid(a)
        silu = a * sg
        st_ref[...] = (silu * b).T.astype(BF16)
        ds = _bdot(dyh_ref[...], wd_ref[...], NT_DIMS)
        da = (ds * b * (sg * (1.0 + a * (1.0 - sg)))).astype(BF16)
        db = (ds * silu).astype(BF16)
        da_ref[...] = da
        db_ref[...] = db
        acc_ref[...] += _bdot(da, wg_ref[...], NT_DIMS) + _bdot(db, wu_ref[...], NT_DIMS)

        @pl.when((i == 0) & (j == 0))
        def _():
            dg_ref[...] = jnp.zeros_like(dg_ref)

        @pl.when(j == nf - 1)
        def _():
            xhat, rstd = _rms_stats(x_ref[...])
            dx, dg = _rms_bwd_vals(acc_ref[...], xhat, rstd, g_ref[...])
            dx_ref[...] = dy_ref[...] + dx
            dg_ref[...] += dg

    row = pl.BlockSpec((tm, d), lambda i, j: (i, 0))
    vec = pl.BlockSpec((1, d), lambda i, j: (0, 0))
    fblk = pl.BlockSpec((None, tm, fc), lambda i, j: (j, i, 0))
    return pl.pallas_call(
        body, grid=(t // tm, nf),
        in_specs=[row, vec, pl.BlockSpec((None, d, fc), lambda i, j: (j, 0, 0)),
                  pl.BlockSpec((None, d, fc), lambda i, j: (j, 0, 0)), pl.BlockSpec((None, fc, d), lambda i, j: (j, 0, 0)),
                  row],
        out_specs=[row, vec, pl.BlockSpec((d, tm), lambda i, j: (0, i)), row, fblk, fblk,
                   pl.BlockSpec((None, fc, tm), lambda i, j: (j, 0, i))],
        out_shape=[jax.ShapeDtypeStruct((t, d), F32), jax.ShapeDtypeStruct((1, d), F32),
                   jax.ShapeDtypeStruct((d, t), BF16), jax.ShapeDtypeStruct((t, d), BF16),
                   jax.ShapeDtypeStruct((nf, t, fc), BF16), jax.ShapeDtypeStruct((nf, t, fc), BF16),
                   jax.ShapeDtypeStruct((nf, fc, t), BF16)],
        scratch_shapes=[pltpu.VMEM((tm, d), F32), pltpu.VMEM((tm, d), BF16)],
        compiler_params=_cp(("arbitrary", "arbitrary")), name=name)(x, g.reshape(1, d), wg, wu, wd, dy)


def _pool_core(u, grp):
    s = u.shape[0]
    w2 = u + _shift_down(u, 1)
    w4 = w2 + _shift_down(w2, 2)
    w8 = w4 + _shift_down(w4, 4)
    w16 = w8 + _shift_down(w8, 8)
    wsum = jnp.where(grp == 0, w2, jnp.where(grp == 1, w4, jnp.where(grp == 2, w8, w16)))
    win = jnp.left_shift(2, grp).astype(F32)
    t1 = (lax.broadcasted_iota(jnp.int32, (s, 1), 0) + 1).astype(F32)
    inv = 1.0 / jnp.minimum(t1, win)
    return wsum * inv - u, inv


def _pool_fwd(proj, pool_w, pool_scale, nb, s, *, name):
    def body(u_ref, w_ref, sc_ref, y_ref):
        pooled, _ = _pool_core(u_ref[...], pl.program_id(0))
        y_ref[...] = _bdot(pooled, w_ref[...]) * sc_ref[...]

    return pl.pallas_call(
        body, grid=(NH, nb),
        in_specs=[pl.BlockSpec((s, HD), lambda g, b: (b, CB_POOL + g)),
                  pl.BlockSpec((None, HD, HD), lambda g, b: (g, 0, 0)), pl.BlockSpec((1, HD), lambda g, b: (0, g))],
        out_specs=pl.BlockSpec((s, HD), lambda g, b: (b, g)),
        out_shape=jax.ShapeDtypeStruct((nb * s, BW), F32),
        compiler_params=_cp(("parallel", "parallel")), name=name)(proj, pool_w, pool_scale)


def _pool_bwd(proj, pool_w, pool_scale, dy, nb, s, *, name):
    def body(u_ref, w_ref, sc_ref, dy_ref, du_ref, dw_ref, dsc_ref):
        grp, b = pl.program_id(0), pl.program_id(1)
        pooled, inv = _pool_core(u_ref[...], grp)
        mixed = _bdot(pooled, w_ref[...])
        dy = dy_ref[...]
        dmixed = dy * sc_ref[...]
        dpooled = _bdot(dmixed, w_ref[...], NT_DIMS)
        r = dpooled * inv
        v2 = r + _shift_up(r, 1)
        v4 = v2 + _shift_up(v2, 2)
        v8 = v4 + _shift_up(v4, 4)
        v16 = v8 + _shift_up(v8, 8)
        vsum = jnp.where(grp == 0, v2, jnp.where(grp == 1, v4, jnp.where(grp == 2, v8, v16)))
        du_ref[...] = vsum - dpooled

        @pl.when(b == 0)
        def _():
            dw_ref[...] = jnp.zeros_like(dw_ref)
            dsc_ref[...] = jnp.zeros_like(dsc_ref)

        dw_ref[...] += _bdot(pooled, dmixed, TN_DIMS)
        dsc_ref[...] += jnp.sum(dy * mixed, axis=0, keepdims=True)

    return pl.pallas_call(
        body, grid=(NH, nb),
        in_specs=[pl.BlockSpec((s, HD), lambda g, b: (b, CB_POOL + g)),
                  pl.BlockSpec((None, HD, HD), lambda g, b: (g, 0, 0)), pl.BlockSpec((1, HD), lambda g, b: (0, g)),
                  pl.BlockSpec((s, HD), lambda g, b: (b, g))],
        out_specs=[pl.BlockSpec((s, HD), lambda g, b: (b, g)), pl.BlockSpec((None, HD, HD), lambda g, b: (g, 0, 0)),
                   pl.BlockSpec((1, HD), lambda g, b: (0, g))],
        out_shape=[jax.ShapeDtypeStruct((nb * s, BW), F32), jax.ShapeDtypeStruct((NH, HD, HD), F32),
                   jax.ShapeDtypeStruct((1, BW), F32)],
        compiler_params=_cp(("arbitrary", "arbitrary")), name=name)(proj, pool_w, pool_scale, dy)


SB_BLK = 128


SB_G = 4
SB_KG = SB_G * SB_BLK


def _sb_block(qb, kg, q0, k0):
    z = _bdot(qb, kg, NT_DIMS) * (HD ** -0.5)
    row = lax.broadcasted_iota(jnp.int32, (SB_BLK, SB_KG), 0) + q0
    col = lax.broadcasted_iota(jnp.int32, (SB_BLK, SB_KG), 1) + k0
    causal = col < row
    lsz = _log_sigmoid(z)
    lnm = jnp.where(causal, lsz - z, 0.0)
    return lsz, lnm, causal


def _sub(x, m):
    return x[:, m * SB_BLK:(m + 1) * SB_BLK]


def _sb_tails(lnm, after, ct):
    tails, cts = [None] * SB_G, [None] * SB_G
    for m in reversed(range(SB_G)):
        cts[m] = ct
        tails[m] = _split_dot(_sub(lnm, m), after) + ct
        ct = ct + jnp.sum(_sub(lnm, m), axis=1, keepdims=True)
    return jnp.concatenate(tails, axis=1), cts, ct


def _tri01(lower):
    r = lax.broadcasted_iota(jnp.int32, (SB_BLK, SB_BLK), 0)
    c = lax.broadcasted_iota(jnp.int32, (SB_BLK, SB_BLK), 1)
    return jnp.where((r < c) if lower else (r > c), 1.0, 0.0).astype(BF16)


def _split3(x):
    hi = x.astype(BF16)
    mid = (x - hi.astype(F32)).astype(BF16)
    lo = (x - hi.astype(F32) - mid.astype(F32)).astype(BF16)
    return hi, mid, lo


def _row_sums_as_rows(x):
    ones = jnp.ones((8, x.shape[1]), BF16)
    return sum(lax.dot_general(ones, p, NT_DIMS, preferred_element_type=F32) for p in _split3(x))


def _rows_to_cols(rows):
    eighth = jnp.full((8, LANE), 0.125, BF16)
    return sum(lax.dot_general(p, eighth, TN_DIMS, preferred_element_type=F32) for p in _split3(rows))


def _sb_fwd(proj, nb, s, *, name):
    nq = s // SB_BLK
    ng = nq // SB_G

    def body(q_ref, k_ref, v_ref, o_ref, ctr_ref):
        after = _tri01(False)

        def qblock(i, _):
            q0 = pl.multiple_of(i * SB_BLK, SB_BLK)
            qb = q_ref[pl.ds(q0, SB_BLK), :]

            def kgroup(jj, carry):
                acc, ct, ctr = carry
                g = i // SB_G - jj
                k0 = pl.multiple_of(g * SB_KG, SB_KG)
                lsz, lnm, causal = _sb_block(qb, k_ref[pl.ds(k0, SB_KG), :], q0, k0)
                ctr_ref[i * ng + g] = ctr
                tail, _, ct = _sb_tails(lnm, after, ct)
                w = jnp.where(causal, jnp.exp(lsz + tail), 0.0)
                return acc + _bdot(w, v_ref[pl.ds(k0, SB_KG), :]), ct, ctr + _row_sums_as_rows(lnm)

            acc, _, _ = lax.fori_loop(0, i // SB_G + 1, kgroup, (jnp.zeros((SB_BLK, HD), F32),
                                                                 jnp.zeros((SB_BLK, 1), F32), jnp.zeros((8, LANE), F32)))
            o_ref[pl.ds(q0, SB_BLK), :] = acc
            return 0

        lax.fori_loop(0, nq, qblock, 0)

    def col(cb):
        return pl.BlockSpec((s, HD), lambda b, h: (b, cb + h))

    return pl.pallas_call(
        body, grid=(nb, NH), in_specs=[col(CB_SBQ), col(CB_SBK), col(CB_SBV)],
        out_specs=[pl.BlockSpec((s, HD), lambda b, h: (b, h)),
                   pl.BlockSpec((None, None, nq * ng, 8, LANE), lambda b, h: (b, h, 0, 0, 0))],
        out_shape=[jax.ShapeDtypeStruct((nb * s, BW), F32), jax.ShapeDtypeStruct((nb, NH, nq * ng, 8, LANE), F32)],
        compiler_params=_cp(("parallel", "parallel")), name=name)(proj, proj, proj)


def _sb_bwd(proj, ctr, dy, nb, s, *, name):
    nq = s // SB_BLK
    ng = nq // SB_G
    scale = HD ** -0.5

    def body(q_ref, k_ref, v_ref, ctr_ref, do_ref, dq_ref, dk_ref, dv_ref):
        after = _tri01(False)
        before = _tri01(True)
        dk_ref[...] = jnp.zeros_like(dk_ref)
        dv_ref[...] = jnp.zeros_like(dv_ref)

        def qblock(i, _):
            q0 = pl.multiple_of(i * SB_BLK, SB_BLK)
            qb = q_ref[pl.ds(q0, SB_BLK), :]
            dob = do_ref[pl.ds(q0, SB_BLK), :]

            def kgroup(g, carry):
                dq, ce = carry
                k0 = pl.multiple_of(g * SB_KG, SB_KG)
                kg = k_ref[pl.ds(k0, SB_KG), :]
                vg = v_ref[pl.ds(k0, SB_KG), :]
                lsz, lnm, causal = _sb_block(qb, kg, q0, k0)
                tail, _, _ = _sb_tails(lnm, after, _rows_to_cols(ctr_ref[i * ng + g])[:, 0:1])
                w = jnp.where(causal, jnp.exp(lsz + tail), 0.0)
                e = _bdot(dob, vg, NT_DIMS) * w
                pres = []
                for m in range(SB_G):
                    pres.append(_split_dot(_sub(e, m), before) + ce)
                    ce = ce + jnp.sum(_sub(e, m), axis=1, keepdims=True)
                sig = jnp.exp(lsz)
                dz = jnp.where(causal, e * (1.0 - sig) - jnp.concatenate(pres, axis=1) * sig, 0.0) * scale
                dk_ref[pl.ds(k0, SB_KG), :] += _bdot(dz, qb, TN_DIMS)
                dv_ref[pl.ds(k0, SB_KG), :] += _bdot(w, dob, TN_DIMS)
                return dq + _bdot(dz, kg), ce

            dq, _ = lax.fori_loop(0, i // SB_G + 1, kgroup,
                                  (jnp.zeros((SB_BLK, HD), F32), jnp.zeros((SB_BLK, 1), F32)))
            dq_ref[pl.ds(q0, SB_BLK), :] = dq
            return 0

        lax.fori_loop(0, nq, qblock, 0)

    def col(cb):
        return pl.BlockSpec((s, HD), lambda b, h: (b, cb + h))

    out = pl.BlockSpec((s, HD), lambda b, h: (b, h))
    sds = jax.ShapeDtypeStruct((nb * s, BW), F32)
    return pl.pallas_call(
        body, grid=(nb, NH),
        in_specs=[col(CB_SBQ), col(CB_SBK), col(CB_SBV),
                  pl.BlockSpec((None, None, nq * ng, 8, LANE), lambda b, h: (b, h, 0, 0, 0)), out],
        out_specs=[out, out, out], out_shape=[sds, sds, sds],
        compiler_params=_cp(("parallel", "parallel")), name=name)(proj, proj, proj, ctr, dy)


def _make_cdot(dims, dims_da, dims_db, swap_a=False, swap_b=False):
    @jax.custom_vjp
    def f(a, b):
        return _bdot(a, b, dims)

    def fwd(a, b):
        return _bdot(a, b, dims), (a, b)

    def bwd(res, g):
        a, b = res
        da = _bdot(b, g, dims_da) if swap_a else _bdot(g, b, dims_da)
        db = _bdot(g, a, dims_db) if swap_b else _bdot(a, g, dims_db)
        return da, db

    f.defvjp(fwd, bwd)
    return f


_cdot = _make_cdot(NN_DIMS, NT_DIMS, TN_DIMS)
_cdot_nt = _make_cdot(NT_DIMS, NN_DIMS, TN_DIMS, swap_b=True)
_cdot_tn = _make_cdot(TN_DIMS, NT_DIMS, NN_DIMS, swap_a=True)


DN_SUPER = 4 * DN_CHUNK


def _dn_local(q, k, v, bb, gb):
    n = q.shape[0]
    r = lax.broadcasted_iota(jnp.int32, (n, n), 0)
    cc = lax.broadcasted_iota(jnp.int32, (n, n), 1)
    shift = int(math.log2(DN_CHUNK))
    same = lax.shift_right_logical(r, shift) == lax.shift_right_logical(cc, shift)
    incl = jnp.where(same, jnp.where(r >= cc, 1.0, 0.0), 0.0)
    strict = jnp.where(same, jnp.where(r > cc, 1.0, 0.0), 0.0)
    gc = _hdot(incl, gb)
    gc_row = _hdot(jnp.full((n, HD), 1.0 / HD, F32), gc, NT_DIMS)
    diff = jnp.concatenate([gc] * (n // HD), axis=1) - gc_row
    decay = incl * jnp.exp(diff * incl)
    kb = k * bb
    lmat = _cdot_nt(kb, k) * (strict * decay)
    egc = jnp.exp(gc)
    inv = jnp.where(r == cc, 1.0, 0.0) - lmat
    pw = _hdot(lmat, lmat)
    for it in range(shift - 1):
        inv = inv + _hdot(inv, pw)
        if it < shift - 2:
            pw = _hdot(pw, pw)
    u = _hdot(inv, v * bb)
    w = _hdot(inv, kb * egc)
    attn = _cdot_nt(q, k) * decay
    gl = _hdot(jnp.where(same, 1.0, 0.0), gb)
    return u, w, attn, q * egc, k * jnp.exp(gl - gc), jnp.exp(gl)


def _attn_pairs(attn):
    return jnp.concatenate([attn[:HD, :HD], attn[HD:, HD:]], axis=0)


def _attn_unpairs(a):
    z = jnp.zeros((HD, HD), F32)
    return jnp.concatenate([jnp.concatenate([a[:HD], z], axis=1), jnp.concatenate([z, a[HD:]], axis=1)], axis=0)


def _dn_step(u, w, a, qd, kd, cdrows, state, odd):
    v_new = u - _cdot(w, state)
    z = jnp.zeros_like(v_new)
    o = _cdot(qd, state) + _cdot(a, jnp.concatenate([z, v_new] if odd else [v_new, z], axis=0))
    return o, state * jnp.mean(cdrows, axis=0, keepdims=True) + _cdot_tn(kd, v_new)


def _dn_local_pass(fn, s, ins, outs):
    def step(it, _):
        sl = pl.ds(pl.multiple_of(it * DN_SUPER, DN_SUPER), DN_SUPER)
        res = fn(*[ref[sl, :] for ref in ins])
        for ref, val in zip(outs, res):
            ref[sl, :] = val
        return 0

    lax.fori_loop(0, s // DN_SUPER, step, 0)


def _lane_pick(row, idx):
    lane = lax.broadcasted_iota(jnp.int32, row.shape, 1)
    return jnp.sum(jnp.where(lane == idx, row, 0.0), axis=1, keepdims=True)


def _col_pick(x, idx):
    lane = lax.broadcasted_iota(jnp.int32, x.shape, 1)
    return jnp.sum(jnp.where(lane == idx, x, 0.0), axis=1, keepdims=True)


def _conv_silu(x, w):
    xc = (w[3:4, :] * x + w[2:3, :] * _shift_down(x, 1) + w[1:2, :] * _shift_down(x, 2)
          + w[0:1, :] * _shift_down(x, 3))
    return xc * _sigmoid(xc), xc


def _conv_silu_bwd(x, w, xc, dxs, dw_ref):
    sg = _sigmoid(xc)
    dxc = dxs * (sg * (1.0 + xc * (1.0 - sg)))
    dx = (w[3:4, :] * dxc + w[2:3, :] * _shift_up(dxc, 1) + w[1:2, :] * _shift_up(dxc, 2)
          + w[0:1, :] * _shift_up(dxc, 3))
    dw_ref[3:4, :] += jnp.sum(dxc * x, axis=0, keepdims=True)
    dw_ref[2:3, :] += jnp.sum(dxc * _shift_down(x, 1), axis=0, keepdims=True)
    dw_ref[1:2, :] += jnp.sum(dxc * _shift_down(x, 2), axis=0, keepdims=True)
    dw_ref[0:1, :] += jnp.sum(dxc * _shift_down(x, 3), axis=0, keepdims=True)
    return dx


def _dn_prep(qr_ref, kr_ref, vr_ref, ab_ref, cq_ref, ck_ref, cv_ref, par_ref, head):
    qs, qc = _conv_silu(qr_ref[...], cq_ref[...])
    ks, kc = _conv_silu(kr_ref[...], ck_ref[...])
    vs, vc = _conv_silu(vr_ref[...], cv_ref[...])
    rq = lax.rsqrt(jnp.sum(qs * qs, axis=1, keepdims=True) + EPS)
    rk = lax.rsqrt(jnp.sum(ks * ks, axis=1, keepdims=True) + EPS)
    ab = ab_ref[...]
    a_in = _col_pick(ab, head) + _lane_pick(par_ref[1:2, :], head)
    beta = _sigmoid(_col_pick(ab, NH + head))
    neg_ea = -jnp.exp(_lane_pick(par_ref[0:1, :], head))
    g = neg_ea * _softplus(a_in)
    return dict(q=qs * rq * (HD ** -0.5), k=ks * rk, v=vs, beta=beta, g=g, qs=qs, ks=ks, qc=qc, kc=kc, vc=vc,
                rq=rq, rk=rk, a_in=a_in, neg_ea=neg_ea)


ONE_BUF = pl.Buffered(1)
DN_BWD_VMEM = 62 * 1024 * 1024


def _dn_specs(nb, s):
    def col(cb):
        return pl.BlockSpec((s, HD), lambda h, b: (b, cb + h), pipeline_mode=ONE_BUF)

    def conv(cb):
        return pl.BlockSpec((DN_CONV_W, HD), lambda h, b: (0, cb + h))

    return col, conv


DN_CONV_W = 4


def _dn_fwd(proj, ab, conv_w, par, gain, nb, s, *, name):
    nc = s // DN_CHUNK
    col, conv = _dn_specs(nb, s)

    def body(qr_ref, kr_ref, vr_ref, z_ref, ab_ref, cq_ref, ck_ref, cv_ref, par_ref, gain_ref,
             y_ref, o_ref, st_ref, q_s, k_s, v_s, bb_s, gb_s, at_s, cd_s):
        p = _dn_prep(qr_ref, kr_ref, vr_ref, ab_ref, cq_ref, ck_ref, cv_ref, par_ref, pl.program_id(0))
        q_s[...], k_s[...], v_s[...] = p["q"], p["k"], p["v"]
        bb_s[...] = jnp.broadcast_to(p["beta"], (s, HD))
        gb_s[...] = jnp.broadcast_to(p["g"], (s, HD))
        def local(*args):
            u, w, attn, qd, kd, cd = _dn_local(*args)
            return u, w, _attn_pairs(attn), qd, kd, cd

        _dn_local_pass(local, s, [q_s, k_s, v_s, bb_s, gb_s], [v_s, bb_s, at_s, q_s, k_s, cd_s])

        def chunk_pair(pi, state):
            for odd in (0, 1):
                ci = 2 * pi + odd
                sl = pl.ds(pl.multiple_of(ci * DN_CHUNK, DN_CHUNK), DN_CHUNK)
                st_ref[ci] = state
                o, state = _dn_step(v_s[sl, :], bb_s[sl, :], at_s[sl, :], q_s[sl, :], k_s[sl, :], cd_s[sl, :],
                                    state, odd)
                o_ref[sl, :] = o
            return state

        lax.fori_loop(0, nc // 2, chunk_pair, jnp.zeros((HD, HD), F32))
        o = o_ref[...]
        z = z_ref[...]
        on = o * lax.rsqrt(jnp.mean(o * o, axis=1, keepdims=True) + EPS) * gain_ref[...]
        y_ref[...] = on * (z * _sigmoid(z))

    out = pl.BlockSpec((s, HD), lambda h, b: (b, h))
    sds = jax.ShapeDtypeStruct((nb * s, BW), F32)
    return pl.pallas_call(
        body, grid=(NH, nb),
        in_specs=[col(CB_DNQ), col(CB_DNK), col(CB_DNV), col(CB_DNZ), pl.BlockSpec((s, LANE), lambda h, b: (b, 0)),
                  conv(0), conv(NH), conv(2 * NH), pl.BlockSpec((8, LANE), lambda h, b: (0, 0)),
                  pl.BlockSpec((1, HD), lambda h, b: (0, 0))],
        out_specs=[out, out, pl.BlockSpec((None, None, nc, HD, HD), lambda h, b: (b, h, 0, 0, 0))],
        out_shape=[sds, sds, jax.ShapeDtypeStruct((nb, NH, nc, HD, HD), F32)],
        scratch_shapes=[pltpu.VMEM((s, HD), F32)] * 7,
        compiler_params=_cp(("parallel", "parallel")), name=name)(
            proj, proj, proj, proj, ab, conv_w, conv_w, conv_w, par, gain)


def _dn_bwd(proj, ab, conv_w, par, gain, o_pre, states, dy, nb, s, *, name):
    nc = s // DN_CHUNK
    col, conv = _dn_specs(nb, s)

    def body(qr_ref, kr_ref, vr_ref, z_ref, ab_ref, cq_ref, ck_ref, cv_ref, par_ref, gain_ref, o_ref, st_ref, dy_ref,
             dqr_ref, dkr_ref, dvr_ref, dz_ref, dab_ref, dcq_ref, dck_ref, dcv_ref, dpar_ref, dgain_ref,
             q_s, k_s, v_s, bb_s, gb_s, do_s, u_s, w_s, qd_s, kd_s, at_s, cd_s):
        head, b = pl.program_id(0), pl.program_id(1)
        p = _dn_prep(qr_ref, kr_ref, vr_ref, ab_ref, cq_ref, ck_ref, cv_ref, par_ref, head)
        q_s[...], k_s[...], v_s[...] = p["q"], p["k"], p["v"]
        bb_s[...] = jnp.broadcast_to(p["beta"], (s, HD))
        gb_s[...] = jnp.broadcast_to(p["g"], (s, HD))

        @pl.when(b == 0)
        def _():
            for ref in (dcq_ref, dck_ref, dcv_ref, dpar_ref):
                ref[...] = jnp.zeros_like(ref)

        @pl.when((b == 0) & (head == 0))
        def _():
            dgain_ref[...] = jnp.zeros_like(dgain_ref)

        o, z, dy = o_ref[...], z_ref[...], dy_ref[...]
        rstd = lax.rsqrt(jnp.mean(o * o, axis=1, keepdims=True) + EPS)
        ohat = o * rstd
        sgz = _sigmoid(z)
        dz_ref[...] = dy * (ohat * gain_ref[...]) * (sgz * (1.0 + z * (1.0 - sgz)))
        don = dy * (z * sgz)
        dgain_ref[...] += jnp.sum(don * ohat, axis=0, keepdims=True)
        dxh = don * gain_ref[...]
        do_s[...] = rstd * (dxh - ohat * jnp.mean(dxh * ohat, axis=1, keepdims=True))

        def local(*args):
            u, w, attn, qd, kd, cd = _dn_local(*args)
            return u, w, _attn_pairs(attn), qd, kd, cd

        local_refs = [u_s, w_s, at_s, qd_s, kd_s, cd_s]
        _dn_local_pass(local, s, [q_s, k_s, v_s, bb_s, gb_s], local_refs)

        def chunk_pair(pr, dstate):
            for odd in (1, 0):
                ci = nc - 1 - 2 * pr - (1 - odd)
                sl = pl.ds(pl.multiple_of(ci * DN_CHUNK, DN_CHUNK), DN_CHUNK)
                _, vjp = jax.vjp(functools.partial(_dn_step, odd=odd), u_s[sl, :], w_s[sl, :], at_s[sl, :],
                                 qd_s[sl, :], kd_s[sl, :], cd_s[sl, :], st_ref[ci])
                du, dw, dat, dqd, dkd, dcd, dstate = vjp((do_s[sl, :], dstate))
                u_s[sl, :], w_s[sl, :], at_s[sl, :], qd_s[sl, :], kd_s[sl, :], cd_s[sl, :] = du, dw, dat, dqd, dkd, dcd
            return dstate

        lax.fori_loop(0, nc // 2, chunk_pair, jnp.zeros((HD, HD), F32))

        def local_bwd(q, k, v, bb, gb, du, dw, dat, dqd, dkd, dcd):
            _, vjp = jax.vjp(_dn_local, q, k, v, bb, gb)
            dq, dk, dv, dbb, dgb = vjp((du, dw, _attn_unpairs(dat), dqd, dkd, dcd))
            return (dq, dk, dv, jnp.broadcast_to(jnp.sum(dbb, axis=1, keepdims=True), (DN_SUPER, HD)),
                    jnp.broadcast_to(jnp.sum(dgb, axis=1, keepdims=True), (DN_SUPER, HD)))

        _dn_local_pass(local_bwd, s, [q_s, k_s, v_s, bb_s, gb_s] + local_refs, [q_s, k_s, v_s, bb_s, gb_s])

        dq, dk, dv = q_s[...], k_s[...], v_s[...]
        qs, ks, rq, rk = p["qs"], p["ks"], p["rq"], p["rk"]
        dqs = (HD ** -0.5) * (rq * dq - qs * (rq * rq * rq) * jnp.sum(dq * qs, axis=1, keepdims=True))
        dks = rk * dk - ks * (rk * rk * rk) * jnp.sum(dk * ks, axis=1, keepdims=True)
        dqr_ref[...] = _conv_silu_bwd(qr_ref[...], cq_ref[...], p["qc"], dqs, dcq_ref)
        dkr_ref[...] = _conv_silu_bwd(kr_ref[...], ck_ref[...], p["kc"], dks, dck_ref)
        dvr_ref[...] = _conv_silu_bwd(vr_ref[...], cv_ref[...], p["vc"], dv, dcv_ref)

        dbeta, dg = bb_s[:, 0:1], gb_s[:, 0:1]
        beta = p["beta"]
        db_logit = dbeta * beta * (1.0 - beta)
        da = dg * p["neg_ea"] * _sigmoid(p["a_in"])
        lane = lax.broadcasted_iota(jnp.int32, (s, LANE), 1)
        dab_ref[...] = jnp.where(lane == head, da, 0.0) + jnp.where(lane == NH + head, db_logit, 0.0)
        dpar_ref[0:1, :] += jnp.broadcast_to(jnp.sum(dg * p["g"], axis=0, keepdims=True), (1, LANE))
        dpar_ref[1:2, :] += jnp.broadcast_to(jnp.sum(da, axis=0, keepdims=True), (1, LANE))

    out = pl.BlockSpec((s, HD), lambda h, b: (b, h))
    in_blk = pl.BlockSpec((s, HD), lambda h, b: (b, h), pipeline_mode=ONE_BUF)
    cblk = pl.BlockSpec((DN_CONV_W, HD), lambda h, b: (0, h))
    sds = jax.ShapeDtypeStruct((nb * s, BW), F32)
    csds = jax.ShapeDtypeStruct((DN_CONV_W, BW), F32)
    return pl.pallas_call(
        body, grid=(NH, nb),
        in_specs=[col(CB_DNQ), col(CB_DNK), col(CB_DNV), col(CB_DNZ),
                  pl.BlockSpec((s, LANE), lambda h, b: (b, 0), pipeline_mode=ONE_BUF),
                  conv(0), conv(NH), conv(2 * NH), pl.BlockSpec((8, LANE), lambda h, b: (0, 0)),
                  pl.BlockSpec((1, HD), lambda h, b: (0, 0)), in_blk,
                  pl.BlockSpec((None, None, nc, HD, HD), lambda h, b: (b, h, 0, 0, 0), pipeline_mode=ONE_BUF), in_blk],
        out_specs=[out, out, out, out, pl.BlockSpec((None, s, LANE), lambda h, b: (h, b, 0)), cblk, cblk, cblk,
                   pl.BlockSpec((None, 8, LANE), lambda h, b: (h, 0, 0)), pl.BlockSpec((1, HD), lambda h, b: (0, 0))],
        out_shape=[sds, sds, sds, sds, jax.ShapeDtypeStruct((NH, nb * s, LANE), F32), csds, csds, csds,
                   jax.ShapeDtypeStruct((NH, 8, LANE), F32), jax.ShapeDtypeStruct((1, HD), F32)],
        scratch_shapes=[pltpu.VMEM((s, HD), F32)] * 12,
        compiler_params=_cp(("arbitrary", "arbitrary"), DN_BWD_VMEM), name=name)(
            proj, proj, proj, proj, ab, conv_w, conv_w, conv_w, par, gain, o_pre, states, dy)


def _sum_heads(x, *, name):
    nh, t, c = x.shape
    tm = _pick(t, (1024, 512, 256, 128))

    def body(x_ref, o_ref):
        o_ref[...] = (x_ref[0] + x_ref[1] + x_ref[2] + x_ref[3]).astype(BF16)

    return pl.pallas_call(
        body, grid=(t // tm,), in_specs=[pl.BlockSpec((nh, tm, c), lambda i: (0, i, 0))],
        out_specs=pl.BlockSpec((tm, c), lambda i: (i, 0)), out_shape=jax.ShapeDtypeStruct((t, c), BF16),
        compiler_params=_cp(("parallel",)), name=name)(x)


MERGE_TM = 256


def _merge_fwd(x, proj, yp, yd, ys, b_gate, wb, wo, *, name):
    t, d = x.shape
    tm = _pick(t, (MERGE_TM, 128))

    def body(x_ref, g0_ref, g1_ref, g2_ref, yp_ref, yd_ref, ys_ref, bg_ref, wb_ref, wo_ref, o_ref):
        merged = jnp.zeros((tm, d), F32)
        for n, (g_ref, y_ref) in enumerate(((g0_ref, yp_ref), (g1_ref, yd_ref), (g2_ref, ys_ref))):
            gate = _sigmoid(g_ref[...] + bg_ref[:, n * d:(n + 1) * d])
            merged = merged + gate * _bdot(y_ref[...], wb_ref[n])
        o_ref[...] = x_ref[...] + _bdot(merged, wo_ref[...])

    row = pl.BlockSpec((tm, d), lambda i: (i, 0))
    yblk = pl.BlockSpec((tm, BW), lambda i: (i, 0))

    def gl(n):
        return pl.BlockSpec((tm, d), lambda i: (i, CB_GATE + n))

    return pl.pallas_call(
        body, grid=(t // tm,),
        in_specs=[row, gl(0), gl(1), gl(2), yblk, yblk, yblk, pl.BlockSpec((1, 3 * d), lambda i: (0, 0)),
                  pl.BlockSpec((3, BW, d), lambda i: (0, 0, 0)), pl.BlockSpec((d, d), lambda i: (0, 0))],
        out_specs=row, out_shape=jax.ShapeDtypeStruct((t, d), F32),
        compiler_params=_cp(("parallel",)), name=name)(x, proj, proj, proj, yp, yd, ys, b_gate, wb, wo)


def _merge_bwd(proj, yp, yd, ys, b_gate, wb, wo, dx, *, name):
    t, d = dx.shape
    tm = _pick(t, (MERGE_TM, 128))

    def body(g0_ref, g1_ref, g2_ref, yp_ref, yd_ref, ys_ref, bg_ref, wb_ref, wo_ref, dx_ref,
             dyp_ref, dyd_ref, dys_ref, dgl_ref, mg_ref, dxh_ref, dbd_ref, dbg_ref):
        dxh = dx_ref[...].astype(BF16)
        dxh_ref[...] = dxh
        dmerged = _bdot(dxh, wo_ref[...], NT_DIMS)
        merged = jnp.zeros((tm, d), F32)

        @pl.when(pl.program_id(0) == 0)
        def _():
            dbg_ref[...] = jnp.zeros_like(dbg_ref)

        for n, (g_ref, y_ref, dy_ref) in enumerate(((g0_ref, yp_ref, dyp_ref), (g1_ref, yd_ref, dyd_ref),
                                                    (g2_ref, ys_ref, dys_ref))):
            gate = _sigmoid(g_ref[...] + bg_ref[:, n * d:(n + 1) * d])
            bd = _bdot(y_ref[...], wb_ref[n])
            merged = merged + gate * bd
            dgl = dmerged * bd * gate * (1.0 - gate)
            dgl_ref[:, n * d:(n + 1) * d] = dgl.astype(BF16)
            dbg_ref[:, n * d:(n + 1) * d] += jnp.sum(dgl, axis=0, keepdims=True)
            dbd = (dmerged * gate).astype(BF16)
            dbd_ref[n] = dbd
            dy_ref[...] = _bdot(dbd, wb_ref[n], NT_DIMS)
        mg_ref[...] = merged.astype(BF16)

    row = pl.BlockSpec((tm, d), lambda i: (i, 0))
    yblk = pl.BlockSpec((tm, BW), lambda i: (i, 0))
    bgv = pl.BlockSpec((1, 3 * d), lambda i: (0, 0))

    def gl(n):
        return pl.BlockSpec((tm, d), lambda i: (i, CB_GATE + n))

    ysds = jax.ShapeDtypeStruct((t, BW), F32)
    return pl.pallas_call(
        body, grid=(t // tm,),
        in_specs=[gl(0), gl(1), gl(2), yblk, yblk, yblk, bgv,
                  pl.BlockSpec((3, BW, d), lambda i: (0, 0, 0)), pl.BlockSpec((d, d), lambda i: (0, 0)), row],
        out_specs=[yblk, yblk, yblk, pl.BlockSpec((tm, 3 * d), lambda i: (i, 0)), row, row,
                   pl.BlockSpec((3, tm, d), lambda i: (0, i, 0)), bgv],
        out_shape=[ysds, ysds, ysds, jax.ShapeDtypeStruct((t, 3 * d), BF16), jax.ShapeDtypeStruct((t, d), BF16),
                   jax.ShapeDtypeStruct((t, d), BF16), jax.ShapeDtypeStruct((3, t, d), BF16),
                   jax.ShapeDtypeStruct((1, 3 * d), F32)],
        compiler_params=_cp(("arbitrary",)), name=name)(proj, proj, proj, yp, yd, ys, b_gate, wb, wo, dx)


def _loss_head(x, g, target, *, name):
    t, d = x.shape
    tm = _pick(t, (512, 256, 128))

    def body(x_ref, g_ref, t_ref, dx_ref, dg_ref, loss_ref):
        xhat, rstd = _rms_stats(x_ref[...])
        err = xhat * g_ref[...] - t_ref[...]
        dx, dg = _rms_bwd_vals(err * (1.0 / d), xhat, rstd, g_ref[...])
        dx_ref[...] = dx

        @pl.when(pl.program_id(0) == 0)
        def _():
            dg_ref[...] = jnp.zeros_like(dg_ref)
            loss_ref[...] = jnp.zeros_like(loss_ref)

        dg_ref[...] += dg
        part = jnp.sum(jnp.sum(err * err, axis=1, keepdims=True), axis=0, keepdims=True) * (0.5 / d)
        loss_ref[...] += jnp.broadcast_to(part, (1, LANE))

    row = pl.BlockSpec((tm, d), lambda i: (i, 0))
    vec = pl.BlockSpec((1, d), lambda i: (0, 0))
    return pl.pallas_call(
        body, grid=(t // tm,), in_specs=[row, vec, row],
        out_specs=[row, vec, pl.BlockSpec((1, LANE), lambda i: (0, 0))],
        out_shape=[jax.ShapeDtypeStruct((t, d), F32), jax.ShapeDtypeStruct((1, d), F32),
                   jax.ShapeDtypeStruct((1, LANE), F32)],
        compiler_params=_cp(("arbitrary",)), name=name)(x, g.reshape(1, d), target)


def _adamw(w, g, m, v, *, name):
    rows, cols = w.shape
    fits = [c for c in (1024, 704, 512, 352, 256, 128, 64, 32, 16, 8) if c * cols * 4 * 14 <= VMEM_LIMIT // 2]
    tr = _pick(rows, fits)
    c1 = 1.0 / (1.0 - ADAM_B1 ** ADAM_STEP)
    c2 = 1.0 / (1.0 - ADAM_B2 ** ADAM_STEP)

    def body(w_ref, g_ref, m_ref, v_ref, d_ref, nm_ref, nv_ref):
        g = g_ref[...]
        nm = ADAM_B1 * m_ref[...] + (1.0 - ADAM_B1) * g
        nv = ADAM_B2 * v_ref[...] + (1.0 - ADAM_B2) * (g * g)
        nm_ref[...] = nm
        nv_ref[...] = nv
        d_ref[...] = -ADAM_LR * ((nm * c1) / (jnp.sqrt(nv * c2) + ADAM_EPS) + ADAM_WD * w_ref[...])

    blk = pl.BlockSpec((tr, cols), lambda i: (i, 0))
    sds = jax.ShapeDtypeStruct((rows, cols), F32)
    return pl.pallas_call(
        body, grid=(rows // tr,), in_specs=[blk] * 4, out_specs=[blk] * 3, out_shape=[sds] * 3,
        compiler_params=_cp(("parallel",)), name=name)(w, g, m, v)


MESH_ID = pl.DeviceIdType.MESH
HBM_SPEC = pl.BlockSpec(memory_space=pl.ANY)
OTHER_CHIPS = ((1, 0), (0, 1), (1, 1))


def _at_slot(ref, nl, slot):
    return ref.at[(slice(None),) * nl + (slot,)]


def _slotted(shape, nl, slots):
    return tuple(shape[:nl]) + (slots,) + tuple(shape[nl:])


def _flip(v, f):
    return 1 - v if f else v


def _comm_call(body, n, out_shapes, n_remote, args, name):
    return pl.pallas_call(
        body, out_shape=out_shapes, in_specs=[HBM_SPEC] * len(args), out_specs=[HBM_SPEC] * len(out_shapes),
        scratch_shapes=[pltpu.SemaphoreType.DMA((n * n_remote,)), pltpu.SemaphoreType.DMA((n * n_remote,)),
                        pltpu.SemaphoreType.DMA((n * 4,))],
        compiler_params=pltpu.CompilerParams(has_side_effects=True), name=name)(*args)


def _gather(xs, nls, *, name):
    n = len(xs)

    def body(*refs):
        x_refs, o_refs, (send_sems, recv_sems, local_sems) = refs[:n], refs[n:2 * n], refs[2 * n:]
        x, y, c = lax.axis_index("x"), lax.axis_index("y"), lax.axis_index("c")
        me, sibling = (x, y, c), (x, y, 1 - c)
        chips = [(_flip(x, fx), _flip(y, fy)) for fx, fy in OTHER_CHIPS]

        def copy(a, k, block, to, src=None):
            dst = _at_slot(o_refs[a], nls[a], 4 * block[0] + 2 * block[1] + block[2])
            return pltpu.make_async_remote_copy(
                src_ref=dst if src is None else src, dst_ref=dst, send_sem=send_sems.at[a * 7 + k],
                recv_sem=recv_sems.at[a * 7 + k], device_id=to, device_id_type=MESH_ID)

        mine = [pltpu.make_async_copy(x_refs[a], _at_slot(o_refs[a], nls[a], 4 * x + 2 * y + c), local_sems.at[a])
                for a in range(n)]
        sent = []
        for a in range(n):
            mine[a].start()
            sent.append(copy(a, 0, me, sibling, src=x_refs[a]))
            sent += [copy(a, 1 + j, me, (*chip, c), src=x_refs[a]) for j, chip in enumerate(chips)]
        for cp in sent:
            cp.start()
        for j, chip in enumerate(chips):
            for a in range(n):
                copy(a, 1 + j, (*chip, c), me).wait_recv()
                passed = copy(a, 4 + j, (*chip, c), sibling)
                passed.start()
                sent.append(passed)
        for a in range(n):
            copy(a, 0, sibling, me).wait_recv()
            for j, chip in enumerate(chips):
                copy(a, 4 + j, (*chip, 1 - c), me).wait_recv()
        for cp in sent:
            cp.wait_send()
        for cp in mine:
            cp.wait()

    outs = [jax.ShapeDtypeStruct(_slotted(v.shape, nl, N_DEV), v.dtype) for v, nl in zip(xs, nls)]
    return _comm_call(body, n, outs, 7, xs, name)


def _scatter_pair(gs, nls, *, name):
    n = len(gs)

    def body(*refs):
        g_refs, got_refs, (send_sems, recv_sems, _) = refs[:n], refs[n:2 * n], refs[2 * n:]
        x, y, c = lax.axis_index("x"), lax.axis_index("y"), lax.axis_index("c")
        remote = []
        for a in range(n):
            for q in range(4):
                rc = pltpu.make_async_remote_copy(
                    src_ref=_at_slot(g_refs[a], nls[a], 2 * q + 1 - c), dst_ref=_at_slot(got_refs[a], nls[a], q),
                    send_sem=send_sems.at[a * 4 + q], recv_sem=recv_sems.at[a * 4 + q], device_id=(x, y, 1 - c),
                    device_id_type=MESH_ID)
                rc.start()
                remote.append(rc)
        for rc in remote:
            rc.wait_recv()
        for rc in remote:
            rc.wait_send()

    outs = [jax.ShapeDtypeStruct(_slotted(v.shape[:nl] + v.shape[nl + 1:], nl, 4), v.dtype) for v, nl in zip(gs, nls)]
    return _comm_call(body, n, outs, 4, gs, name)


def _scatter_chips(ps, nls, *, name):
    n = len(ps)

    def body(*refs):
        p_refs, r_refs, (send_sems, recv_sems, _) = refs[:n], refs[n:2 * n], refs[2 * n:]
        x, y, c = lax.axis_index("x"), lax.axis_index("y"), lax.axis_index("c")
        remote = []
        for a in range(n):
            for k, (fx, fy) in enumerate(OTHER_CHIPS):
                tx, ty = _flip(x, fx), _flip(y, fy)
                rc = pltpu.make_async_remote_copy(
                    src_ref=_at_slot(p_refs[a], nls[a], 2 * tx + ty), dst_ref=_at_slot(r_refs[a], nls[a], k),
                    send_sem=send_sems.at[a * 3 + k], recv_sem=recv_sems.at[a * 3 + k], device_id=(tx, ty, c),
                    device_id_type=MESH_ID)
                rc.start()
                remote.append(rc)
        for rc in remote:
            rc.wait_recv()
        for rc in remote:
            rc.wait_send()

    outs = [jax.ShapeDtypeStruct(_slotted(v.shape[:nl] + v.shape[nl + 1:], nl, 3), v.dtype) for v, nl in zip(ps, nls)]
    return _comm_call(body, n, outs, 3, ps, name)


def _pair_add(g, got, core, *, name):
    rows, cols = g.shape[-2:]
    lf = math.prod(got.shape[:-3])
    tr = _pick(rows, (1024, 512, 352, 256, 128))

    def body(core_ref, g_ref, got_ref, o_ref):
        o_ref[...] = (g_ref[...].astype(F32) + got_ref[...].astype(F32)).astype(BF16)

    blk = pl.BlockSpec((None, None, tr, cols), lambda i, q, j, core_ref: (i, q, j, 0))
    out = pl.pallas_call(
        body, grid_spec=pltpu.PrefetchScalarGridSpec(
            num_scalar_prefetch=1, grid=(lf, 4, rows // tr),
            in_specs=[pl.BlockSpec((None, None, None, tr, cols), lambda i, q, j, core_ref: (i, q, core_ref[0], j, 0)),
                      blk], out_specs=blk),
        out_shape=jax.ShapeDtypeStruct((lf, 4, rows, cols), BF16),
        compiler_params=_cp(("parallel", "parallel", "parallel")), name=name)(
            core, g.reshape(lf, 4, 2, rows, cols), got.reshape(lf, 4, rows, cols))
    return out.reshape(got.shape)


def _sum_adamw(p, r, chip, w, m, v, *, name):
    shape = w.shape
    rows, cols = shape[-2:]
    lf = math.prod(shape[:-2])
    fits = [c for c in (1024, 512, 352, 256, 128, 64, 32, 16) if c * cols * (7 * 4 + 4 * 2) * 2 <= VMEM_LIMIT // 2]
    tr = _pick(rows, fits)
    c1 = 1.0 / (1.0 - ADAM_B1 ** ADAM_STEP)
    c2 = 1.0 / (1.0 - ADAM_B2 ** ADAM_STEP)

    def body(chip_ref, p_ref, r_ref, w_ref, m_ref, v_ref, g_ref, d_ref, nm_ref, nv_ref):
        g = p_ref[...].astype(F32)
        for k in range(3):
            g = g + r_ref[k].astype(F32)
        g_ref[...] = g
        nm = ADAM_B1 * m_ref[...] + (1.0 - ADAM_B1) * g
        nv = ADAM_B2 * v_ref[...] + (1.0 - ADAM_B2) * (g * g)
        nm_ref[...] = nm
        nv_ref[...] = nv
        d_ref[...] = -ADAM_LR * ((nm * c1) / (jnp.sqrt(nv * c2) + ADAM_EPS) + ADAM_WD * w_ref[...])

    blk = pl.BlockSpec((None, tr, cols), lambda i, j, chip_ref: (i, j, 0))
    sds = jax.ShapeDtypeStruct((lf, rows, cols), F32)
    outs = pl.pallas_call(
        body, grid_spec=pltpu.PrefetchScalarGridSpec(
            num_scalar_prefetch=1, grid=(lf, rows // tr),
            in_specs=[pl.BlockSpec((None, None, tr, cols), lambda i, j, chip_ref: (i, chip_ref[0], j, 0)),
                      pl.BlockSpec((None, 3, tr, cols), lambda i, j, chip_ref: (i, 0, j, 0))] + [blk] * 3,
            out_specs=[blk] * 4),
        out_shape=[sds] * 4, compiler_params=_cp(("parallel", "parallel")), name=name)(
            chip, p.reshape(lf, 4, rows, cols), r.reshape(lf, 3, rows, cols), w.reshape(lf, rows, cols),
            m.reshape(lf, rows, cols), v.reshape(lf, rows, cols))
    return [o.reshape(shape) for o in outs]


def _sum_slots(x, *, name):
    nd, rows, cols = x.shape
    tr = _pick(rows, (512, 256, 128, 64, 32, 16, 8))

    def body(x_ref, o_ref):
        acc = x_ref[0].astype(F32)
        for j in range(1, nd):
            acc = acc + x_ref[j].astype(F32)
        o_ref[...] = acc

    return pl.pallas_call(
        body, grid=(rows // tr,), in_specs=[pl.BlockSpec((nd, tr, cols), lambda i: (0, i, 0))],
        out_specs=pl.BlockSpec((tr, cols), lambda i: (i, 0)), out_shape=jax.ShapeDtypeStruct((rows, cols), F32),
        compiler_params=_cp(("parallel",)), name=name)(x)


def _pad_rows(a, mult=8):
    r = (-a.shape[0]) % mult
    return jnp.pad(a, ((0, r), (0, 0))) if r else a


def _flat128(a):
    f = a.reshape(-1)
    return jnp.pad(f, (0, (-f.shape[0]) % LANE)).reshape(-1, LANE)


def _unshard(gathered, shape, axis):
    g = gathered.reshape((N_DEV,) + tuple(shape))
    g = jnp.moveaxis(g, 0, axis)
    full = list(shape)
    full[axis] *= N_DEV
    return g.reshape(full)


def _col_shards(full):
    rows, cols = full.shape
    return jnp.moveaxis(full.reshape(rows, N_DEV, cols // N_DEV), 1, 0)


BIG = (("ffn_w_gate", 2), ("ffn_w_up", 2), ("ffn_w_down", 2), ("w_in", 1), ("w_branch", 2), ("w_out", 1))


def kernel(x, ffn_norm, ffn_w_gate, ffn_w_up, ffn_w_down, mix_norm, w_in, b_gate, pool_w, pool_scale, dn_conv, dn_A_log, dn_dt_bias, dn_out_norm, w_branch, w_out, final_norm, loss_target, m_ffn_norm, m_ffn_w_gate, m_ffn_w_up, m_ffn_w_down, m_mix_norm, m_w_in, m_b_gate, m_pool_w, m_pool_scale, m_dn_conv, m_dn_A_log, m_dn_dt_bias, m_dn_out_norm, m_w_branch, m_w_out, m_final_norm, v_ffn_norm, v_ffn_w_gate, v_ffn_w_up, v_ffn_w_down, v_mix_norm, v_w_in, v_b_gate, v_pool_w, v_pool_scale, v_dn_conv, v_dn_A_log, v_dn_dt_bias, v_dn_out_norm, v_w_branch, v_w_out, v_final_norm):
    wts = dict(ffn_norm=ffn_norm, ffn_w_gate=ffn_w_gate, ffn_w_up=ffn_w_up, ffn_w_down=ffn_w_down, mix_norm=mix_norm,
               w_in=w_in, b_gate=b_gate, pool_w=pool_w, pool_scale=pool_scale, dn_conv=dn_conv, dn_A_log=dn_A_log,
               dn_dt_bias=dn_dt_bias, dn_out_norm=dn_out_norm, w_branch=w_branch, w_out=w_out, final_norm=final_norm)
    mom = dict(ffn_norm=m_ffn_norm, ffn_w_gate=m_ffn_w_gate, ffn_w_up=m_ffn_w_up, ffn_w_down=m_ffn_w_down,
               mix_norm=m_mix_norm, w_in=m_w_in, b_gate=m_b_gate, pool_w=m_pool_w, pool_scale=m_pool_scale,
               dn_conv=m_dn_conv, dn_A_log=m_dn_A_log, dn_dt_bias=m_dn_dt_bias, dn_out_norm=m_dn_out_norm,
               w_branch=m_w_branch, w_out=m_w_out, final_norm=m_final_norm)
    var = dict(ffn_norm=v_ffn_norm, ffn_w_gate=v_ffn_w_gate, ffn_w_up=v_ffn_w_up, ffn_w_down=v_ffn_w_down,
               mix_norm=v_mix_norm, w_in=v_w_in, b_gate=v_b_gate, pool_w=v_pool_w, pool_scale=v_pool_scale,
               dn_conv=v_dn_conv, dn_A_log=v_dn_A_log, dn_dt_bias=v_dn_dt_bias, dn_out_norm=v_dn_out_norm,
               w_branch=v_w_branch, w_out=v_w_out, final_norm=v_final_norm)
    nb, s, d = x.shape
    t = nb * s
    me = 4 * lax.axis_index("x") + 2 * lax.axis_index("y") + lax.axis_index("c")

    nls = [nl for _, nl in BIG]
    small_sh = jnp.concatenate([_flat128(ffn_norm), _flat128(dn_conv)], axis=0)
    *gat, small_g = _gather([wts[n].astype(BF16) for n, _ in BIG] + [small_sh], nls + [0], name="gather_weights")
    full = dict(zip([n for n, _ in BIG], gat))
    w_in_full = jnp.moveaxis(full["w_in"], 1, 2).reshape(DEPTH, d, -1)
    w_main = jnp.concatenate([w_in_full[:, :, :AB_LO], w_in_full[:, :, AB_HI:]], axis=2)
    w_ab = jnp.pad(w_in_full[:, :, AB_LO:AB_HI], ((0, 0), (0, 0), (0, LANE - (AB_HI - AB_LO))))
    wb_full = jnp.moveaxis(full["w_branch"], 2, 3).reshape(DEPTH, 3, BW, d)
    wo_full = full["w_out"].reshape(DEPTH, d, d)
    nfr = ffn_norm.size // LANE
    ffn_norm_full = _unshard(small_g[:, :nfr], ffn_norm.shape, 2)
    dn_conv_full = _unshard(small_g[:, nfr:], dn_conv.shape, 2)
    pool_w_h = pool_w.astype(BF16)

    xs = x.reshape(t, d)
    saved = []
    for l in range(DEPTH):
        sv = dict(x0=xs)
        xs = _ffn_fwd(xs, ffn_norm_full[l, 0], full["ffn_w_gate"][l, 0], full["ffn_w_up"][l, 0],
                      full["ffn_w_down"][l, 0], name="ffn_fwd")
        sv["x1"] = xs
        h = _rms_fwd(xs, mix_norm[l], name="mix_rms")
        proj = _mm(h, w_main[l], name="proj")
        ab = _mm(h, w_ab[l], name="proj_ab")
        par = jnp.pad(jnp.stack([dn_A_log[l], dn_dt_bias[l]]), ((0, 6), (0, LANE - NH)))
        gain = dn_out_norm[l].reshape(1, HD)
        psc = pool_scale[l].reshape(1, BW)
        yp = _pool_fwd(proj, pool_w_h[l], psc, nb, s, name="pool_fwd")
        yd, o_pre, states = _dn_fwd(proj, ab, dn_conv_full[l], par, gain, nb, s, name="dn_fwd")
        ys, sb_ctr = _sb_fwd(proj, nb, s, name="sb_fwd")
        bg = b_gate[l].reshape(1, 3 * d)
        xs = _merge_fwd(xs, proj, yp, yd, ys, bg, wb_full[l], wo_full[l], name="merge_fwd")
        sv.update(x2=xs, h=h, proj=proj, ab=ab, par=par, gain=gain, psc=psc, yp=yp, yd=yd, ys=ys, sb_ctr=sb_ctr, o_pre=o_pre,
                  states=states, bg=bg)
        xs = _ffn_fwd(xs, ffn_norm_full[l, 1], full["ffn_w_gate"][l, 1], full["ffn_w_up"][l, 1],
                      full["ffn_w_down"][l, 1], name="ffn_fwd")
        saved.append(sv)

    dx, g_final, loss_row = _loss_head(xs, final_norm, loss_target.reshape(t, d), name="loss_head")
    loss = lax.psum(loss_row[0, 0], ("x", "y", "c"))

    gw = {n: [None] * DEPTH for n in ("ffn_norm", "ffn_w_gate", "ffn_w_up", "ffn_w_down", "mix_norm", "w_in", "b_gate",
                                      "pool_w", "pool_scale", "dn_conv", "dn_A_log", "dn_dt_bias", "dn_out_norm",
                                      "w_branch", "w_out")}

    def ffn_back(l, i, x_in, dy):
        dxi, dg, hb, dyh, da, db, sact = _ffn_bwd(x_in, ffn_norm_full[l, i], full["ffn_w_gate"][l, i],
                                                  full["ffn_w_up"][l, i], full["ffn_w_down"][l, i], dy, name="ffn_bwd")
        return dxi, dg, (_mm_slots(hb, da, name="dw_gate_up"), _mm_slots(hb, db, name="dw_gate_up"),
                         _mm_slots(sact, dyh, name="dw_down"))

    for l in reversed(range(DEPTH)):
        sv = saved[l]
        dx, dg1, (dwg1, dwu1, dwd1) = ffn_back(l, 1, sv["x2"], dx)
        dyp, dyd, dys, dgl, merged, dxh, dbd, dbg = _merge_bwd(sv["proj"], sv["yp"], sv["yd"], sv["ys"], sv["bg"],
                                                               wb_full[l], wo_full[l], dx,
                                                               name="merge_bwd")
        gw["w_out"][l] = _mm(merged, dxh, ta=True, out_dtype=BF16, name="dw_out").reshape(N_DEV, d // N_DEV, d)
        gw["w_branch"][l] = jnp.stack([_col_shards(_mm(y, dbd[n], ta=True, out_dtype=BF16, name="dw_branch"))
                                       for n, y in enumerate((sv["yp"], sv["yd"], sv["ys"]))])
        gw["b_gate"][l] = dbg.reshape(3 * d)
        du, dpw, dps = _pool_bwd(sv["proj"], pool_w_h[l], sv["psc"], dyp, nb, s, name="pool_bwd")
        gw["pool_w"][l], gw["pool_scale"][l] = dpw, dps.reshape(BW)
        dqr, dkr, dvr, dz, dab4, dcq, dck, dcv, dpar, dgain = _dn_bwd(
            sv["proj"], sv["ab"], dn_conv_full[l], sv["par"], sv["gain"], sv["o_pre"], sv["states"], dyd, nb, s,
            name="dn_bwd")
        gw["dn_conv"][l] = jnp.concatenate([dcq, dck, dcv], axis=1)
        gw["dn_A_log"][l], gw["dn_dt_bias"][l], gw["dn_out_norm"][l] = dpar[:, 0, 0], dpar[:, 1, 0], dgain.reshape(HD)
        dsq, dsk, dsv = _sb_bwd(sv["proj"], sv["sb_ctr"], dys, nb, s, name="sb_bwd")
        dab = _sum_heads(dab4, name="sum_heads")
        dproj = jnp.concatenate([du.astype(BF16), dqr.astype(BF16), dkr.astype(BF16), dvr.astype(BF16),
                                 dz.astype(BF16), dsq.astype(BF16), dsk.astype(BF16), dsv.astype(BF16), dgl], axis=1)
        dw_main = _mm(sv["h"], dproj, ta=True, out_dtype=BF16, name="dw_in")
        dw_ab = _mm(sv["h"], dab, ta=True, out_dtype=BF16, name="dw_ab")
        gw["w_in"][l] = _col_shards(jnp.concatenate([dw_main[:, :AB_LO], dw_ab[:, :AB_HI - AB_LO],
                                                     dw_main[:, AB_LO:]], axis=1))
        dh_main = _mm(dproj, w_main[l], tb=True, name="dh_mix")
        dh_ab = _mm(dab, w_ab[l], tb=True, name="dh_mix_ab")
        dx, dgm = _rms_bwd(sv["x1"], mix_norm[l], dh_main, dh_ab, dx, name="mix_rms_bwd")
        gw["mix_norm"][l] = dgm.reshape(d)
        dx, dg0, (dwg0, dwu0, dwd0) = ffn_back(l, 0, sv["x0"], dx)
        gw["ffn_norm"][l] = jnp.stack([dg0.reshape(d), dg1.reshape(d)])
        gw["ffn_w_gate"][l] = jnp.stack([dwg0, dwg1])
        gw["ffn_w_up"][l] = jnp.stack([dwu0, dwu1])
        gw["ffn_w_down"][l] = jnp.stack([dwd0, dwd1])
    grad_x = dx.reshape(nb, s, d)
    gw = {n: jnp.stack(v) for n, v in gw.items()}
    gw["final_norm"] = g_final.reshape(d)

    big = [n for n, _ in BIG]
    core = lax.axis_index("c").astype(jnp.int32).reshape(1)
    chip = (2 * lax.axis_index("x") + lax.axis_index("y")).astype(jnp.int32).reshape(1)
    got = _scatter_pair([gw[n] for n in big], nls, name="scatter_grads_pair")
    chip_sums = [_pair_add(gw[n], b, core, name="add_pair_" + n) for n, b in zip(big, got)]
    recv = _scatter_chips(chip_sums, nls, name="scatter_grads_chips")
    grads, delta, new_m, new_v = {}, {}, {}, {}
    for n, p, r in zip(big, chip_sums, recv):
        grads[n], delta[n], new_m[n], new_v[n] = _sum_adamw(p, r, chip, wts[n], mom[n], var[n], name="adamw_" + n)

    small = ("ffn_norm", "mix_norm", "b_gate", "pool_w", "pool_scale", "dn_conv", "dn_A_log", "dn_dt_bias",
             "dn_out_norm", "final_norm")
    sp = _pad_rows(jnp.concatenate([_flat128(gw[n]) for n in small], axis=0))
    ssum = _sum_slots(_gather([sp], [0], name="gather_small_grads")[0], name="sum_small_grads")
    off = 0
    for n in small:
        r = -(-gw[n].size // LANE)
        g = ssum[off:off + r].reshape(-1)[:gw[n].size].reshape(gw[n].shape)
        off += r
        if n in ("ffn_norm", "dn_conv"):
            w = wts[n].shape[2]
            g = lax.dynamic_slice_in_dim(g, me * w, w, axis=2)
        grads[n] = g

    pk = lambda src: _pad_rows(jnp.concatenate([_flat128(src[n]) for n in small], axis=0))
    dl, nm, nv = _adamw(pk(wts), pk(grads), pk(mom), pk(var), name="adamw_small")
    off = 0
    for n in small:
        r = -(-wts[n].size // LANE)
        for dst, src in ((delta, dl), (new_m, nm), (new_v, nv)):
            dst[n] = src[off:off + r].reshape(-1)[:wts[n].size].reshape(wts[n].shape)
        off += r

    order = ("ffn_norm", "ffn_w_gate", "ffn_w_up", "ffn_w_down", "mix_norm", "w_in", "b_gate", "pool_w", "pool_scale",
             "dn_conv", "dn_A_log", "dn_dt_bias", "dn_out_norm", "w_branch", "w_out", "final_norm")
    return (loss, grad_x, *[grads[n] for n in order], *[delta[n] for n in order], *[new_m[n] for n in order],
            *[new_v[n] for n in order])
```

```python
import functools
import math

import jax
import jax.numpy as jnp
from jax import lax
from jax.experimental import pallas as pl
from jax.experimental.pallas import tpu as pltpu

F32, BF16 = jnp.float32, jnp.bfloat16
D_MODEL, D_FF, DEPTH = 1024, 2816, 4
BW = 512
HD = 128
NH = 4
DN_CHUNK = 64
EPS = 1e-6
N_DEV = 8
LANE = 128
CB_POOL, CB_DNQ, CB_DNK, CB_DNV, CB_DNZ, CB_SBQ, CB_SBK, CB_SBV = 0, 4, 8, 12, 16, 20, 24, 28
CB_GATE = 4
P_MAIN = 7168
AB_LO, AB_HI = 2560, 2568
ADAM_LR, ADAM_B1, ADAM_B2, ADAM_EPS, ADAM_WD, ADAM_STEP = 0.001, 0.9, 0.999, 1e-08, 0.01, 10
VMEM_LIMIT = 56 * 1024 * 1024
HIGHEST = lax.Precision.HIGHEST
NT_DIMS = (((1,), (1,)), ((), ()))
TN_DIMS = (((0,), (0,)), ((), ()))
NN_DIMS = (((1,), (0,)), ((), ()))


def _cp(dims=None, vmem=VMEM_LIMIT):
    return pltpu.CompilerParams(dimension_semantics=dims, vmem_limit_bytes=vmem)


def _pick(n, cands):
    for c in cands:
        if n % c == 0:
            return c
    return n


def _bdot(a, b, dims=NN_DIMS):
    return lax.dot_general(a.astype(BF16), b.astype(BF16), dims, preferred_element_type=F32)


def _hdot(a, b, dims=NN_DIMS):
    return lax.dot_general(a, b, dims, precision=lax.Precision.HIGH, preferred_element_type=F32)


def _split_dot(x, m01):
    hi = x.astype(BF16)
    lo = (x - hi.astype(F32)).astype(BF16)
    return (lax.dot_general(hi, m01, NN_DIMS, preferred_element_type=F32)
            + lax.dot_general(lo, m01, NN_DIMS, preferred_element_type=F32))


def _sigmoid(x):
    return 1.0 / (1.0 + jnp.exp(-x))


def _log_sigmoid(x):
    return jnp.minimum(x, 0.0) - jnp.log1p(jnp.exp(-jnp.abs(x)))


def _softplus(x):
    return jnp.maximum(x, 0.0) + jnp.log1p(jnp.exp(-jnp.abs(x)))


def _shift_down(x, k):
    r = lax.broadcasted_iota(jnp.int32, x.shape, 0)
    return jnp.where(r >= k, pltpu.roll(x, k, 0), 0.0)


def _shift_up(x, k):
    n = x.shape[0]
    r = lax.broadcasted_iota(jnp.int32, x.shape, 0)
    return jnp.where(r < n - k, pltpu.roll(x, n - k, 0), 0.0)


def _mm(a, b, *, ta=False, tb=False, out_dtype=F32, name):
    (kk, m) = a.shape if ta else a.shape[::-1]
    (k2, n) = b.shape[::-1] if tb else b.shape
    assert kk == k2, (a.shape, b.shape, ta, tb)
    bm = _pick(m, (1024, 512, 256, 128))
    bn = _pick(n, (1024, 1408, 512, 256, 128))
    bk = _pick(kk, (512, 256, 128))
    nk = kk // bk
    dims = (((0 if ta else 1,), (1 if tb else 0,)), ((), ()))

    def body(a_ref, b_ref, o_ref, acc_ref):
        k = pl.program_id(2)

        @pl.when(k == 0)
        def _():
            acc_ref[...] = jnp.zeros_like(acc_ref)

        acc_ref[...] += lax.dot_general(a_ref[...].astype(BF16), b_ref[...].astype(BF16), dims,
                                        preferred_element_type=F32)

        @pl.when(k == nk - 1)
        def _():
            o_ref[...] = acc_ref[...].astype(out_dtype)

    a_spec = (pl.BlockSpec((bk, bm), lambda i, j, k: (k, i)) if ta else pl.BlockSpec((bm, bk), lambda i, j, k: (i, k)))
    b_spec = (pl.BlockSpec((bn, bk), lambda i, j, k: (j, k)) if tb else pl.BlockSpec((bk, bn), lambda i, j, k: (k, j)))
    return pl.pallas_call(
        body, grid=(m // bm, n // bn, nk), in_specs=[a_spec, b_spec],
        out_specs=pl.BlockSpec((bm, bn), lambda i, j, k: (i, j)),
        out_shape=jax.ShapeDtypeStruct((m, n), out_dtype),
        scratch_shapes=[pltpu.VMEM((bm, bn), F32)],
        compiler_params=_cp(("parallel", "parallel", "arbitrary")), name=name)(a, b)


def _mm_slots(a, b, *, name):
    a3, b3 = a.ndim == 3, b.ndim == 3
    ns = a.shape[0] if a3 else b.shape[0]
    m, t = a.shape[-2:]
    n = b.shape[-1]
    bm, bn, bk = _pick(m, (1024, 512, 256, 128)), _pick(n, (1024, 512, 256, 128)), _pick(t, (1024, 512, 256, 128))
    nk = t // bk

    def body(a_ref, b_ref, o_ref, acc_ref):
        k = pl.program_id(3)

        @pl.when(k == 0)
        def _():
            acc_ref[...] = jnp.zeros_like(acc_ref)

        acc_ref[...] += _bdot(a_ref[...], b_ref[...])

        @pl.when(k == nk - 1)
        def _():
            o_ref[...] = acc_ref[...].astype(BF16)

    a_spec = (pl.BlockSpec((None, bm, bk), lambda s, i, j, k: (s, i, k)) if a3
              else pl.BlockSpec((bm, bk), lambda s, i, j, k: (i, k)))
    b_spec = (pl.BlockSpec((None, bk, bn), lambda s, i, j, k: (s, k, j)) if b3
              else pl.BlockSpec((bk, bn), lambda s, i, j, k: (k, j)))
    return pl.pallas_call(
        body, grid=(ns, m // bm, n // bn, nk), in_specs=[a_spec, b_spec],
        out_specs=pl.BlockSpec((None, bm, bn), lambda s, i, j, k: (s, i, j)),
        out_shape=jax.ShapeDtypeStruct((ns, m, n), BF16), scratch_shapes=[pltpu.VMEM((bm, bn), F32)],
        compiler_params=_cp(("parallel", "parallel", "parallel", "arbitrary")), name=name)(a, b)


def _rms_stats(x):
    rstd = lax.rsqrt(jnp.mean(x * x, axis=-1, keepdims=True) + EPS)
    return x * rstd, rstd


def _rms_bwd_vals(dh, xhat, rstd, g):
    dxh = dh * g
    dx = rstd * (dxh - xhat * jnp.mean(dxh * xhat, axis=-1, keepdims=True))
    return dx, jnp.sum(dh * xhat, axis=0, keepdims=True)


def _rms_fwd(x, g, *, name):
    t, d = x.shape
    tm = _pick(t, (512, 256, 128))

    def body(x_ref, g_ref, h_ref):
        xhat, _ = _rms_stats(x_ref[...])
        h_ref[...] = (xhat * g_ref[...]).astype(BF16)

    return pl.pallas_call(
        body, grid=(t // tm,),
        in_specs=[pl.BlockSpec((tm, d), lambda i: (i, 0)), pl.BlockSpec((1, d), lambda i: (0, 0))],
        out_specs=pl.BlockSpec((tm, d), lambda i: (i, 0)), out_shape=jax.ShapeDtypeStruct((t, d), BF16),
        compiler_params=_cp(("parallel",)), name=name)(x, g.reshape(1, d))


def _rms_bwd(x, g, dh_a, dh_b, dres, *, name):
    t, d = x.shape
    tm = _pick(t, (512, 256, 128))

    def body(x_ref, g_ref, dha_ref, dhb_ref, dres_ref, dx_ref, dg_ref):
        xhat, rstd = _rms_stats(x_ref[...])
        dx, dg = _rms_bwd_vals(dha_ref[...] + dhb_ref[...], xhat, rstd, g_ref[...])
        dx_ref[...] = dres_ref[...] + dx

        @pl.when(pl.program_id(0) == 0)
        def _():
            dg_ref[...] = jnp.zeros_like(dg_ref)

        dg_ref[...] += dg

    row = pl.BlockSpec((tm, d), lambda i: (i, 0))
    vec = pl.BlockSpec((1, d), lambda i: (0, 0))
    return pl.pallas_call(
        body, grid=(t // tm,), in_specs=[row, vec, row, row, row], out_specs=[row, vec],
        out_shape=[jax.ShapeDtypeStruct((t, d), F32), jax.ShapeDtypeStruct((1, d), F32)],
        compiler_params=_cp(("arbitrary",)), name=name)(x, g.reshape(1, d), dh_a, dh_b, dres)


FFN_TM = 512


def _ffn_fwd(x, g, wg, wu, wd, *, name):
    t, d = x.shape
    nf, _, fc = wg.shape
    tm = _pick(t, (FFN_TM, 256, 128))

    def body(x_ref, g_ref, wg_ref, wu_ref, wd_ref, o_ref, h_ref, acc_ref):
        j = pl.program_id(1)

        @pl.when(j == 0)
        def _():
            xhat, _ = _rms_stats(x_ref[...])
            h_ref[...] = (xhat * g_ref[...]).astype(BF16)
            acc_ref[...] = jnp.zeros_like(acc_ref)

        h = h_ref[...]
        a = _bdot(h, wg_ref[...])
        b = _bdot(h, wu_ref[...])
        s = a * _sigmoid(a) * b
        acc_ref[...] += _bdot(s, wd_ref[...])

        @pl.when(j == nf - 1)
        def _():
            o_ref[...] = x_ref[...] + 0.5 * acc_ref[...]

    row = pl.BlockSpec((tm, d), lambda i, j: (i, 0))
    return pl.pallas_call(
        body, grid=(t // tm, nf),
        in_specs=[row, pl.BlockSpec((1, d), lambda i, j: (0, 0)),
                  pl.BlockSpec((None, d, fc), lambda i, j: (j, 0, 0)), pl.BlockSpec((None, d, fc), lambda i, j: (j, 0, 0)),
                  pl.BlockSpec((None, fc, d), lambda i, j: (j, 0, 0))],
        out_specs=row, out_shape=jax.ShapeDtypeStruct((t, d), F32),
        scratch_shapes=[pltpu.VMEM((tm, d), BF16), pltpu.VMEM((tm, d), F32)],
        compiler_params=_cp(("parallel", "arbitrary")), name=name)(x, g.reshape(1, d), wg, wu, wd)


def _ffn_bwd(x, g, wg, wu, wd, dy, *, name):
    t, d = x.shape
    nf, _, fc = wg.shape
    tm = _pick(t, (FFN_TM, 256, 128))

    def body(x_ref, g_ref, wg_ref, wu_ref, wd_ref, dy_ref,
             dx_ref, dg_ref, ht_ref, dyh_ref, da_ref, db_ref, st_ref, acc_ref, h_ref):
        i, j = pl.program_id(0), pl.program_id(1)

        @pl.when(j == 0)
        def _():
            xhat, _ = _rms_stats(x_ref[...])
            hf = xhat * g_ref[...]
            h_ref[...] = hf.astype(BF16)
            ht_ref[...] = hf.T.astype(BF16)
            dyh_ref[...] = (0.5 * dy_ref[...]).astype(BF16)
            acc_ref[...] = jnp.zeros_like(acc_ref)

        h = h_ref[...]
        a = _bdot(h, wg_ref[...])
        b = _bdot(h, wu_ref[...])
        sg = _sigmoid(a)
        silu = a * sg
        st_ref[...] = (silu * b).T.astype(BF16)
        ds = _bdot(dyh_ref[...], wd_ref[...], NT_DIMS)
        da = (ds * b * (sg * (1.0 + a * (1.0 - sg)))).astype(BF16)
        db = (ds * silu).astype(BF16)
        da_ref[...] = da
        db_ref[...] = db
        acc_ref[...] += _bdot(da, wg_ref[...], NT_DIMS) + _bdot(db, wu_ref[...], NT_DIMS)

        @pl.when((i == 0) & (j == 0))
        def _():
            dg_ref[...] = jnp.zeros_like(dg_ref)

        @pl.when(j == nf - 1)
        def _():
            xhat, rstd = _rms_stats(x_ref[...])
            dx, dg = _rms_bwd_vals(acc_ref[...], xhat, rstd, g_ref[...])
            dx_ref[...] = dy_ref[...] + dx
            dg_ref[...] += dg

    row = pl.BlockSpec((tm, d), lambda i, j: (i, 0))
    vec = pl.BlockSpec((1, d), lambda i, j: (0, 0))
    fblk = pl.BlockSpec((None, tm, fc), lambda i, j: (j, i, 0))
    return pl.pallas_call(
        body, grid=(t // tm, nf),
        in_specs=[row, vec, pl.BlockSpec((None, d, fc), lambda i, j: (j, 0, 0)),
                  pl.BlockSpec((None, d, fc), lambda i, j: (j, 0, 0)), pl.BlockSpec((None, fc, d), lambda i, j: (j, 0, 0)),
                  row],
        out_specs=[row, vec, pl.BlockSpec((d, tm), lambda i, j: (0, i)), row, fblk, fblk,
                   pl.BlockSpec((None, fc, tm), lambda i, j: (j, 0, i))],
        out_shape=[jax.ShapeDtypeStruct((t, d), F32), jax.ShapeDtypeStruct((1, d), F32),
                   jax.ShapeDtypeStruct((d, t), BF16), jax.ShapeDtypeStruct((t, d), BF16),
                   jax.ShapeDtypeStruct((nf, t, fc), BF16), jax.ShapeDtypeStruct((nf, t, fc), BF16),
                   jax.ShapeDtypeStruct((nf, fc, t), BF16)],
        scratch_shapes=[pltpu.VMEM((tm, d), F32), pltpu.VMEM((tm, d), BF16)],
        compiler_params=_cp(("arbitrary", "arbitrary")), name=name)(x, g.reshape(1, d), wg, wu, wd, dy)


def _pool_core(u, grp):
    s = u.shape[0]
    w2 = u + _shift_down(u, 1)
    w4 = w2 + _shift_down(w2, 2)
    w8 = w4 + _shift_down(w4, 4)
    w16 = w8 + _shift_down(w8, 8)
    wsum = jnp.where(grp == 0, w2, jnp.where(grp == 1, w4, jnp.where(grp == 2, w8, w16)))
    win = jnp.left_shift(2, grp).astype(F32)
    t1 = (lax.broadcasted_iota(jnp.int32, (s, 1), 0) + 1).astype(F32)
    inv = 1.0 / jnp.minimum(t1, win)
    return wsum * inv - u, inv


def _pool_fwd(proj, pool_w, pool_scale, nb, s, *, name):
    def body(u_ref, w_ref, sc_ref, y_ref):
        pooled, _ = _pool_core(u_ref[...], pl.program_id(0))
        y_ref[...] = _bdot(pooled, w_ref[...]) * sc_ref[...]

    return pl.pallas_call(
        body, grid=(NH, nb),
        in_specs=[pl.BlockSpec((s, HD), lambda g, b: (b, CB_POOL + g)),
                  pl.BlockSpec((None, HD, HD), lambda g, b: (g, 0, 0)), pl.BlockSpec((1, HD), lambda g, b: (0, g))],
        out_specs=pl.BlockSpec((s, HD), lambda g, b: (b, g)),
        out_shape=jax.ShapeDtypeStruct((nb * s, BW), F32),
        compiler_params=_cp(("parallel", "parallel")), name=name)(proj, pool_w, pool_scale)


def _pool_bwd(proj, pool_w, pool_scale, dy, nb, s, *, name):
    def body(u_ref, w_ref, sc_ref, dy_ref, du_ref, dw_ref, dsc_ref):
        grp, b = pl.program_id(0), pl.program_id(1)
        pooled, inv = _pool_core(u_ref[...], grp)
        mixed = _bdot(pooled, w_ref[...])
        dy = dy_ref[...]
        dmixed = dy * sc_ref[...]
        dpooled = _bdot(dmixed, w_ref[...], NT_DIMS)
        r = dpooled * inv
        v2 = r + _shift_up(r, 1)
        v4 = v2 + _shift_up(v2, 2)
        v8 = v4 + _shift_up(v4, 4)
        v16 = v8 + _shift_up(v8, 8)
        vsum = jnp.where(grp == 0, v2, jnp.where(grp == 1, v4, jnp.where(grp == 2, v8, v16)))
        du_ref[...] = vsum - dpooled

        @pl.when(b == 0)
        def _():
            dw_ref[...] = jnp.zeros_like(dw_ref)
            dsc_ref[...] = jnp.zeros_like(dsc_ref)

        dw_ref[...] += _bdot(pooled, dmixed, TN_DIMS)
        dsc_ref[...] += jnp.sum(dy * mixed, axis=0, keepdims=True)

    return pl.pallas_call(
        body, grid=(NH, nb),
        in_specs=[pl.BlockSpec((s, HD), lambda g, b: (b, CB_POOL + g)),
                  pl.BlockSpec((None, HD, HD), lambda g, b: (g, 0, 0)), pl.BlockSpec((1, HD), lambda g, b: (0, g)),
                  pl.BlockSpec((s, HD), lambda g, b: (b, g))],
        out_specs=[pl.BlockSpec((s, HD), lambda g, b: (b, g)), pl.BlockSpec((None, HD, HD), lambda g, b: (g, 0, 0)),
                   pl.BlockSpec((1, HD), lambda g, b: (0, g))],
        out_shape=[jax.ShapeDtypeStruct((nb * s, BW), F32), jax.ShapeDtypeStruct((NH, HD, HD), F32),
                   jax.ShapeDtypeStruct((1, BW), F32)],
        compiler_params=_cp(("arbitrary", "arbitrary")), name=name)(proj, pool_w, pool_scale, dy)


SB_BLK = 128


SB_G = 4
SB_KG = SB_G * SB_BLK


def _sb_block(qb, kg, q0, k0):
    z = _bdot(qb, kg, NT_DIMS) * (HD ** -0.5)
    row = lax.broadcasted_iota(jnp.int32, (SB_BLK, SB_KG), 0) + q0
    col = lax.broadcasted_iota(jnp.int32, (SB_BLK, SB_KG), 1) + k0
    causal = col < row
    lsz = _log_sigmoid(z)
    lnm = jnp.where(causal, lsz - z, 0.0)
    return lsz, lnm, causal


def _sub(x, m):
    return x[:, m * SB_BLK:(m + 1) * SB_BLK]


def _sb_tails(lnm, after, ct):
    tails, cts = [None] * SB_G, [None] * SB_G
    for m in reversed(range(SB_G)):
        cts[m] = ct
        tails[m] = _split_dot(_sub(lnm, m), after) + ct
        ct = ct + jnp.sum(_sub(lnm, m), axis=1, keepdims=True)
    return jnp.concatenate(tails, axis=1), cts, ct


def _tri01(lower):
    r = lax.broadcasted_iota(jnp.int32, (SB_BLK, SB_BLK), 0)
    c = lax.broadcasted_iota(jnp.int32, (SB_BLK, SB_BLK), 1)
    return jnp.where((r < c) if lower else (r > c), 1.0, 0.0).astype(BF16)


def _split3(x):
    hi = x.astype(BF16)
    mid = (x - hi.astype(F32)).astype(BF16)
    lo = (x - hi.astype(F32) - mid.astype(F32)).astype(BF16)
    return hi, mid, lo


def _row_sums_as_rows(x):
    ones = jnp.ones((8, x.shape[1]), BF16)
    return sum(lax.dot_general(ones, p, NT_DIMS, preferred_element_type=F32) for p in _split3(x))


def _rows_to_cols(rows):
    eighth = jnp.full((8, LANE), 0.125, BF16)
    return sum(lax.dot_general(p, eighth, TN_DIMS, preferred_element_type=F32) for p in _split3(rows))


def _sb_fwd(proj, nb, s, *, name):
    nq = s // SB_BLK
    ng = nq // SB_G

    def body(q_ref, k_ref, v_ref, o_ref, ctr_ref):
        after = _tri01(False)

        def qblock(i, _):
            q0 = pl.multiple_of(i * SB_BLK, SB_BLK)
            qb = q_ref[pl.ds(q0, SB_BLK), :]

            def kgroup(jj, carry):
                acc, ct, ctr = carry
                g = i // SB_G - jj
                k0 = pl.multiple_of(g * SB_KG, SB_KG)
                lsz, lnm, causal = _sb_block(qb, k_ref[pl.ds(k0, SB_KG), :], q0, k0)
                ctr_ref[i * ng + g] = ctr
                tail, _, ct = _sb_tails(lnm, after, ct)
                w = jnp.where(causal, jnp.exp(lsz + tail), 0.0)
                return acc + _bdot(w, v_ref[pl.ds(k0, SB_KG), :]), ct, ctr + _row_sums_as_rows(lnm)

            acc, _, _ = lax.fori_loop(0, i // SB_G + 1, kgroup, (jnp.zeros((SB_BLK, HD), F32),
                                                                 jnp.zeros((SB_BLK, 1), F32), jnp.zeros((8, LANE), F32)))
            o_ref[pl.ds(q0, SB_BLK), :] = acc
            return 0

        lax.fori_loop(0, nq, qblock, 0)

    def col(cb):
        return pl.BlockSpec((s, HD), lambda b, h: (b, cb + h))

    return pl.pallas_call(
        body, grid=(nb, NH), in_specs=[col(CB_SBQ), col(CB_SBK), col(CB_SBV)],
        out_specs=[pl.BlockSpec((s, HD), lambda b, h: (b, h)),
                   pl.BlockSpec((None, None, nq * ng, 8, LANE), lambda b, h: (b, h, 0, 0, 0))],
        out_shape=[jax.ShapeDtypeStruct((nb * s, BW), F32), jax.ShapeDtypeStruct((nb, NH, nq * ng, 8, LANE), F32)],
        compiler_params=_cp(("parallel", "parallel")), name=name)(proj, proj, proj)


def _sb_bwd(proj, ctr, dy, nb, s, *, name):
    nq = s // SB_BLK
    ng = nq // SB_G
    scale = HD ** -0.5

    def body(q_ref, k_ref, v_ref, ctr_ref, do_ref, dq_ref, dk_ref, dv_ref):
        after = _tri01(False)
        before = _tri01(True)
        dk_ref[...] = jnp.zeros_like(dk_ref)
        dv_ref[...] = jnp.zeros_like(dv_ref)

        def qblock(i, _):
            q0 = pl.multiple_of(i * SB_BLK, SB_BLK)
            qb = q_ref[pl.ds(q0, SB_BLK), :]
            dob = do_ref[pl.ds(q0, SB_BLK), :]

            def kgroup(g, carry):
                dq, ce = carry
                k0 = pl.multiple_of(g * SB_KG, SB_KG)
                kg = k_ref[pl.ds(k0, SB_KG), :]
                vg = v_ref[pl.ds(k0, SB_KG), :]
                lsz, lnm, causal = _sb_block(qb, kg, q0, k0)
                tail, _, _ = _sb_tails(lnm, after, _rows_to_cols(ctr_ref[i * ng + g])[:, 0:1])
                w = jnp.where(causal, jnp.exp(lsz + tail), 0.0)
                e = _bdot(dob, vg, NT_DIMS) * w
                pres = []
                for m in range(SB_G):
                    pres.append(_split_dot(_sub(e, m), before) + ce)
                    ce = ce + jnp.sum(_sub(e, m), axis=1, keepdims=True)
                sig = jnp.exp(lsz)
                dz = jnp.where(causal, e * (1.0 - sig) - jnp.concatenate(pres, axis=1) * sig, 0.0) * scale
                dk_ref[pl.ds(k0, SB_KG), :] += _bdot(dz, qb, TN_DIMS)
                dv_ref[pl.ds(k0, SB_KG), :] += _bdot(w, dob, TN_DIMS)
                return dq + _bdot(dz, kg), ce

            dq, _ = lax.fori_loop(0, i // SB_G + 1, kgroup,
                                  (jnp.zeros((SB_BLK, HD), F32), jnp.zeros((SB_BLK, 1), F32)))
            dq_ref[pl.ds(q0, SB_BLK), :] = dq
            return 0

        lax.fori_loop(0, nq, qblock, 0)

    def col(cb):
        return pl.BlockSpec((s, HD), lambda b, h: (b, cb + h))

    out = pl.BlockSpec((s, HD), lambda b, h: (b, h))
    sds = jax.ShapeDtypeStruct((nb * s, BW), F32)
    return pl.pallas_call(
        body, grid=(nb, NH),
        in_specs=[col(CB_SBQ), col(CB_SBK), col(CB_SBV),
                  pl.BlockSpec((None, None, nq * ng, 8, LANE), lambda b, h: (b, h, 0, 0, 0)), out],
        out_specs=[out, out, out], out_shape=[sds, sds, sds],
        compiler_params=_cp(("parallel", "parallel")), name=name)(proj, proj, proj, ctr, dy)


def _make_cdot(dims, dims_da, dims_db, swap_a=False, swap_b=False):
    @jax.custom_vjp
    def f(a, b):
        return _bdot(a, b, dims)

    def fwd(a, b):
        return _bdot(a, b, dims), (a, b)

    def bwd(res, g):
        a, b = res
        da = _bdot(b, g, dims_da) if swap_a else _bdot(g, b, dims_da)
        db = _bdot(g, a, dims_db) if swap_b else _bdot(a, g, dims_db)
        return da, db

    f.defvjp(fwd, bwd)
    return f


_cdot = _make_cdot(NN_DIMS, NT_DIMS, TN_DIMS)
_cdot_nt = _make_cdot(NT_DIMS, NN_DIMS, TN_DIMS, swap_b=True)
_cdot_tn = _make_cdot(TN_DIMS, NT_DIMS, NN_DIMS, swap_a=True)


DN_SUPER = 4 * DN_CHUNK


@jax.custom_vjp
def _unit_lower_inverse(lmat):
    n = lmat.shape[0]
    steps = int(math.log2(DN_CHUNK))
    eye = jnp.where(lax.broadcasted_iota(jnp.int32, (n, n), 0) == lax.broadcasted_iota(jnp.int32, (n, n), 1), 1.0, 0.0)
    inv = eye - lmat
    pw = _hdot(lmat, lmat)
    for it in range(steps - 1):
        inv = inv + _hdot(inv, pw)
        if it < steps - 2:
            pw = _hdot(pw, pw)
    return inv


def _unit_lower_inverse_fwd(lmat):
    inv = _unit_lower_inverse(lmat)
    return inv, inv


def _unit_lower_inverse_bwd(inv, g):
    return (-_hdot(_hdot(inv, g, TN_DIMS), inv, NT_DIMS),)


_unit_lower_inverse.defvjp(_unit_lower_inverse_fwd, _unit_lower_inverse_bwd)


def _dn_local(q, k, v, bb, gb):
    n = q.shape[0]
    r = lax.broadcasted_iota(jnp.int32, (n, n), 0)
    cc = lax.broadcasted_iota(jnp.int32, (n, n), 1)
    shift = int(math.log2(DN_CHUNK))
    same = lax.shift_right_logical(r, shift) == lax.shift_right_logical(cc, shift)
    incl = jnp.where(same, jnp.where(r >= cc, 1.0, 0.0), 0.0)
    strict = jnp.where(same, jnp.where(r > cc, 1.0, 0.0), 0.0)
    gc = _hdot(incl, gb)
    gc_row = _hdot(jnp.full((n, HD), 1.0 / HD, F32), gc, NT_DIMS)
    diff = jnp.concatenate([gc] * (n // HD), axis=1) - gc_row
    decay = incl * jnp.exp(diff * incl)
    kb = k * bb
    lmat = _cdot_nt(kb, k) * (strict * decay)
    egc = jnp.exp(gc)
    inv = _unit_lower_inverse(lmat)
    u = _hdot(inv, v * bb)
    w = _hdot(inv, kb * egc)
    attn = _cdot_nt(q, k) * decay
    gl = _hdot(jnp.where(same, 1.0, 0.0), gb)
    return u, w, attn, q * egc, k * jnp.exp(gl - gc), jnp.exp(gl)


def _attn_pairs(attn):
    return jnp.concatenate([attn[:HD, :HD], attn[HD:, HD:]], axis=0)


def _attn_unpairs(a):
    z = jnp.zeros((HD, HD), F32)
    return jnp.concatenate([jnp.concatenate([a[:HD], z], axis=1), jnp.concatenate([z, a[HD:]], axis=1)], axis=0)


def _dn_step(u, w, a, qd, kd, cdrows, state, odd):
    v_new = u - _cdot(w, state)
    z = jnp.zeros_like(v_new)
    o = _cdot(qd, state) + _cdot(a, jnp.concatenate([z, v_new] if odd else [v_new, z], axis=0))
    return o, state * jnp.mean(cdrows, axis=0, keepdims=True) + _cdot_tn(kd, v_new)


def _dn_local_pass(fn, s, ins, outs):
    def step(it, _):
        sl = pl.ds(pl.multiple_of(it * DN_SUPER, DN_SUPER), DN_SUPER)
        res = fn(*[ref[sl, :] for ref in ins])
        for ref, val in zip(outs, res):
            ref[sl, :] = val
        return 0

    lax.fori_loop(0, s // DN_SUPER, step, 0)


def _lane_pick(row, idx):
    lane = lax.broadcasted_iota(jnp.int32, row.shape, 1)
    return jnp.sum(jnp.where(lane == idx, row, 0.0), axis=1, keepdims=True)


def _col_pick(x, idx):
    lane = lax.broadcasted_iota(jnp.int32, x.shape, 1)
    return jnp.sum(jnp.where(lane == idx, x, 0.0), axis=1, keepdims=True)


def _conv_silu(x, w):
    xc = (w[3:4, :] * x + w[2:3, :] * _shift_down(x, 1) + w[1:2, :] * _shift_down(x, 2)
          + w[0:1, :] * _shift_down(x, 3))
    return xc * _sigmoid(xc), xc


def _conv_silu_bwd(x, w, xc, dxs, dw_ref):
    sg = _sigmoid(xc)
    dxc = dxs * (sg * (1.0 + xc * (1.0 - sg)))
    dx = (w[3:4, :] * dxc + w[2:3, :] * _shift_up(dxc, 1) + w[1:2, :] * _shift_up(dxc, 2)
          + w[0:1, :] * _shift_up(dxc, 3))
    dw_ref[3:4, :] += jnp.sum(dxc * x, axis=0, keepdims=True)
    dw_ref[2:3, :] += jnp.sum(dxc * _shift_down(x, 1), axis=0, keepdims=True)
    dw_ref[1:2, :] += jnp.sum(dxc * _shift_down(x, 2), axis=0, keepdims=True)
    dw_ref[0:1, :] += jnp.sum(dxc * _shift_down(x, 3), axis=0, keepdims=True)
    return dx


def _dn_prep(qr_ref, kr_ref, vr_ref, ab_ref, cq_ref, ck_ref, cv_ref, par_ref, head):
    qs, qc = _conv_silu(qr_ref[...], cq_ref[...])
    ks, kc = _conv_silu(kr_ref[...], ck_ref[...])
    vs, vc = _conv_silu(vr_ref[...], cv_ref[...])
    rq = lax.rsqrt(jnp.sum(qs * qs, axis=1, keepdims=True) + EPS)
    rk = lax.rsqrt(jnp.sum(ks * ks, axis=1, keepdims=True) + EPS)
    ab = ab_ref[...]
    a_in = _col_pick(ab, head) + _lane_pick(par_ref[1:2, :], head)
    beta = _sigmoid(_col_pick(ab, NH + head))
    neg_ea = -jnp.exp(_lane_pick(par_ref[0:1, :], head))
    g = neg_ea * _softplus(a_in)
    return dict(q=qs * rq * (HD ** -0.5), k=ks * rk, v=vs, beta=beta, g=g, qs=qs, ks=ks, qc=qc, kc=kc, vc=vc,
                rq=rq, rk=rk, a_in=a_in, neg_ea=neg_ea)


ONE_BUF = pl.Buffered(1)
DN_BWD_VMEM = 62 * 1024 * 1024


def _dn_specs(nb, s):
    def col(cb):
        return pl.BlockSpec((s, HD), lambda h, b: (b, cb + h), pipeline_mode=ONE_BUF)

    def conv(cb):
        return pl.BlockSpec((DN_CONV_W, HD), lambda h, b: (0, cb + h))

    return col, conv


DN_CONV_W = 4


def _with_exchange(body, n_in, n_out, n_scratch, exchange, grid):
    if exchange is None:
        return body
    parts_fn, n, nls = exchange

    def wrapped(*refs):
        ins, xs = refs[:n_in], refs[n_in:n_in + n]
        outs, os = refs[n_in + n:n_in + n + n_out], refs[n_in + n + n_out:n_in + 2 * n + n_out]
        rest = refs[n_in + 2 * n + n_out:]
        scratch, sems = rest[:n_scratch], rest[n_scratch:]
        pos = [pl.program_id(k) for k in range(len(grid))]
        first = functools.reduce(jnp.logical_and, [p == 0 for p in pos])
        last = functools.reduce(jnp.logical_and, [p == g - 1 for p, g in zip(pos, grid)])
        start, finish = parts_fn(xs, os, nls, *sems)
        pl.when(first)(start)
        body(*ins, *outs, *scratch)
        pl.when(last)(finish)

    return wrapped


def _dn_fwd(proj, ab, conv_w, par, gain, nb, s, *, name, gather=None):
    nc = s // DN_CHUNK
    col, conv = _dn_specs(nb, s)
    gx, gnl = gather if gather else ([], [])

    def body(qr_ref, kr_ref, vr_ref, z_ref, ab_ref, cq_ref, ck_ref, cv_ref, par_ref, gain_ref,
             y_ref, o_ref, st_ref, q_s, k_s, v_s, bb_s, gb_s, at_s, cd_s):
        p = _dn_prep(qr_ref, kr_ref, vr_ref, ab_ref, cq_ref, ck_ref, cv_ref, par_ref, pl.program_id(0))
        q_s[...], k_s[...], v_s[...] = p["q"], p["k"], p["v"]
        bb_s[...] = jnp.broadcast_to(p["beta"], (s, HD))
        gb_s[...] = jnp.broadcast_to(p["g"], (s, HD))
        def local(*args):
            u, w, attn, qd, kd, cd = _dn_local(*args)
            return u, w, _attn_pairs(attn), qd, kd, cd

        _dn_local_pass(local, s, [q_s, k_s, v_s, bb_s, gb_s], [v_s, bb_s, at_s, q_s, k_s, cd_s])

        def chunk_pair(pi, state):
            for odd in (0, 1):
                ci = 2 * pi + odd
                sl = pl.ds(pl.multiple_of(ci * DN_CHUNK, DN_CHUNK), DN_CHUNK)
                st_ref[ci] = state
                o, state = _dn_step(v_s[sl, :], bb_s[sl, :], at_s[sl, :], q_s[sl, :], k_s[sl, :], cd_s[sl, :],
                                    state, odd)
                o_ref[sl, :] = o
            return state

        lax.fori_loop(0, nc // 2, chunk_pair, jnp.zeros((HD, HD), F32))
        o = o_ref[...]
        z = z_ref[...]
        on = o * lax.rsqrt(jnp.mean(o * o, axis=1, keepdims=True) + EPS) * gain_ref[...]
        y_ref[...] = on * (z * _sigmoid(z))

    out = pl.BlockSpec((s, HD), lambda h, b: (b, h))
    sds = jax.ShapeDtypeStruct((nb * s, BW), F32)
    exchange = (_gather_parts, len(gx), gnl) if gx else None
    res = pl.pallas_call(
        _with_exchange(body, 10, 3, 7, exchange, (NH, nb)), grid=(NH, nb),
        in_specs=[col(CB_DNQ), col(CB_DNK), col(CB_DNV), col(CB_DNZ), pl.BlockSpec((s, LANE), lambda h, b: (b, 0)),
                  conv(0), conv(NH), conv(2 * NH), pl.BlockSpec((8, LANE), lambda h, b: (0, 0)),
                  pl.BlockSpec((1, HD), lambda h, b: (0, 0))] + [HBM_SPEC] * len(gx),
        out_specs=[out, out, pl.BlockSpec((None, None, nc, HD, HD), lambda h, b: (b, h, 0, 0, 0))]
        + [HBM_SPEC] * len(gx),
        out_shape=[sds, sds, jax.ShapeDtypeStruct((nb, NH, nc, HD, HD), F32)] + _gather_shapes(gx, gnl),
        scratch_shapes=[pltpu.VMEM((s, HD), F32)] * 7 + (_comm_sems(len(gx), 7) if gx else []),
        compiler_params=_cp(("arbitrary", "arbitrary")), name=name)(
            proj, proj, proj, proj, ab, conv_w, conv_w, conv_w, par, gain, *gx)
    return res[0], res[1], res[2], list(res[3:])


def _dn_bwd(proj, ab, conv_w, par, gain, o_pre, states, dy, nb, s, *, name, scatter=None):
    nc = s // DN_CHUNK
    col, conv = _dn_specs(nb, s)
    gx, gnl = scatter if scatter else ([], [])

    def body(qr_ref, kr_ref, vr_ref, z_ref, ab_ref, cq_ref, ck_ref, cv_ref, par_ref, gain_ref, o_ref, st_ref, dy_ref,
             dqr_ref, dkr_ref, dvr_ref, dz_ref, dab_ref, dcq_ref, dck_ref, dcv_ref, dpar_ref, dgain_ref,
             q_s, k_s, v_s, bb_s, gb_s, do_s, u_s, w_s, qd_s, kd_s, at_s, cd_s):
        head, b = pl.program_id(0), pl.program_id(1)
        p = _dn_prep(qr_ref, kr_ref, vr_ref, ab_ref, cq_ref, ck_ref, cv_ref, par_ref, head)
        q_s[...], k_s[...], v_s[...] = p["q"], p["k"], p["v"]
        bb_s[...] = jnp.broadcast_to(p["beta"], (s, HD))
        gb_s[...] = jnp.broadcast_to(p["g"], (s, HD))

        @pl.when(b == 0)
        def _():
            for ref in (dcq_ref, dck_ref, dcv_ref, dpar_ref):
                ref[...] = jnp.zeros_like(ref)

        @pl.when((b == 0) & (head == 0))
        def _():
            dgain_ref[...] = jnp.zeros_like(dgain_ref)

        o, z, dy = o_ref[...], z_ref[...], dy_ref[...]
        rstd = lax.rsqrt(jnp.mean(o * o, axis=1, keepdims=True) + EPS)
        ohat = o * rstd
        sgz = _sigmoid(z)
        dz_ref[...] = dy * (ohat * gain_ref[...]) * (sgz * (1.0 + z * (1.0 - sgz)))
        don = dy * (z * sgz)
        dgain_ref[...] += jnp.sum(don * ohat, axis=0, keepdims=True)
        dxh = don * gain_ref[...]
        do_s[...] = rstd * (dxh - ohat * jnp.mean(dxh * ohat, axis=1, keepdims=True))

        def local(*args):
            u, w, attn, qd, kd, cd = _dn_local(*args)
            return u, w, _attn_pairs(attn), qd, kd, cd

        local_refs = [u_s, w_s, at_s, qd_s, kd_s, cd_s]
        _dn_local_pass(local, s, [q_s, k_s, v_s, bb_s, gb_s], local_refs)

        def chunk_pair(pr, dstate):
            for odd in (1, 0):
                ci = nc - 1 - 2 * pr - (1 - odd)
                sl = pl.ds(pl.multiple_of(ci * DN_CHUNK, DN_CHUNK), DN_CHUNK)
                _, vjp = jax.vjp(functools.partial(_dn_step, odd=odd), u_s[sl, :], w_s[sl, :], at_s[sl, :],
                                 qd_s[sl, :], kd_s[sl, :], cd_s[sl, :], st_ref[ci])
                du, dw, dat, dqd, dkd, dcd, dstate = vjp((do_s[sl, :], dstate))
                u_s[sl, :], w_s[sl, :], at_s[sl, :], qd_s[sl, :], kd_s[sl, :], cd_s[sl, :] = du, dw, dat, dqd, dkd, dcd
            return dstate

        lax.fori_loop(0, nc // 2, chunk_pair, jnp.zeros((HD, HD), F32))

        def local_bwd(q, k, v, bb, gb, du, dw, dat, dqd, dkd, dcd):
            _, vjp = jax.vjp(_dn_local, q, k, v, bb, gb)
            dq, dk, dv, dbb, dgb = vjp((du, dw, _attn_unpairs(dat), dqd, dkd, dcd))
            return (dq, dk, dv, jnp.broadcast_to(jnp.sum(dbb, axis=1, keepdims=True), (DN_SUPER, HD)),
                    jnp.broadcast_to(jnp.sum(dgb, axis=1, keepdims=True), (DN_SUPER, HD)))

        _dn_local_pass(local_bwd, s, [q_s, k_s, v_s, bb_s, gb_s] + local_refs, [q_s, k_s, v_s, bb_s, gb_s])

        dq, dk, dv = q_s[...], k_s[...], v_s[...]
        qs, ks, rq, rk = p["qs"], p["ks"], p["rq"], p["rk"]
        dqs = (HD ** -0.5) * (rq * dq - qs * (rq * rq * rq) * jnp.sum(dq * qs, axis=1, keepdims=True))
        dks = rk * dk - ks * (rk * rk * rk) * jnp.sum(dk * ks, axis=1, keepdims=True)
        dqr_ref[...] = _conv_silu_bwd(qr_ref[...], cq_ref[...], p["qc"], dqs, dcq_ref)
        dkr_ref[...] = _conv_silu_bwd(kr_ref[...], ck_ref[...], p["kc"], dks, dck_ref)
        dvr_ref[...] = _conv_silu_bwd(vr_ref[...], cv_ref[...], p["vc"], dv, dcv_ref)

        dbeta, dg = bb_s[:, 0:1], gb_s[:, 0:1]
        beta = p["beta"]
        db_logit = dbeta * beta * (1.0 - beta)
        da = dg * p["neg_ea"] * _sigmoid(p["a_in"])
        lane = lax.broadcasted_iota(jnp.int32, (s, LANE), 1)
        dab_ref[...] = jnp.where(lane == head, da, 0.0) + jnp.where(lane == NH + head, db_logit, 0.0)
        dpar_ref[0:1, :] += jnp.broadcast_to(jnp.sum(dg * p["g"], axis=0, keepdims=True), (1, LANE))
        dpar_ref[1:2, :] += jnp.broadcast_to(jnp.sum(da, axis=0, keepdims=True), (1, LANE))

    out = pl.BlockSpec((s, HD), lambda h, b: (b, h))
    in_blk = pl.BlockSpec((s, HD), lambda h, b: (b, h), pipeline_mode=ONE_BUF)
    cblk = pl.BlockSpec((DN_CONV_W, HD), lambda h, b: (0, h))
    sds = jax.ShapeDtypeStruct((nb * s, BW), F32)
    csds = jax.ShapeDtypeStruct((DN_CONV_W, BW), F32)
    exchange = (_all_to_all_parts, len(gx), gnl) if gx else None
    res = pl.pallas_call(
        _with_exchange(body, 13, 10, 12, exchange, (NH, nb)), grid=(NH, nb),
        in_specs=[col(CB_DNQ), col(CB_DNK), col(CB_DNV), col(CB_DNZ),
                  pl.BlockSpec((s, LANE), lambda h, b: (b, 0), pipeline_mode=ONE_BUF),
                  conv(0), conv(NH), conv(2 * NH), pl.BlockSpec((8, LANE), lambda h, b: (0, 0)),
                  pl.BlockSpec((1, HD), lambda h, b: (0, 0)), in_blk,
                  pl.BlockSpec((None, None, nc, HD, HD), lambda h, b: (b, h, 0, 0, 0), pipeline_mode=ONE_BUF), in_blk]
        + [HBM_SPEC] * len(gx),
        out_specs=[out, out, out, out, pl.BlockSpec((None, s, LANE), lambda h, b: (h, b, 0)), cblk, cblk, cblk,
                   pl.BlockSpec((None, 8, LANE), lambda h, b: (h, 0, 0)), pl.BlockSpec((1, HD), lambda h, b: (0, 0))]
        + [HBM_SPEC] * len(gx),
        out_shape=[sds, sds, sds, sds, jax.ShapeDtypeStruct((NH, nb * s, LANE), F32), csds, csds, csds,
                   jax.ShapeDtypeStruct((NH, 8, LANE), F32), jax.ShapeDtypeStruct((1, HD), F32)]
        + _all_to_all_shapes(gx, gnl),
        scratch_shapes=[pltpu.VMEM((s, HD), F32)] * 12 + (_comm_sems(len(gx), 7) if gx else []),
        compiler_params=_cp(("arbitrary", "arbitrary"), DN_BWD_VMEM), name=name)(
            proj, proj, proj, proj, ab, conv_w, conv_w, conv_w, par, gain, o_pre, states, dy, *gx)
    return tuple(res[:10]) + (list(res[10:]),)


def _sum_heads(x, *, name):
    nh, t, c = x.shape
    tm = _pick(t, (1024, 512, 256, 128))

    def body(x_ref, o_ref):
        o_ref[...] = (x_ref[0] + x_ref[1] + x_ref[2] + x_ref[3]).astype(BF16)

    return pl.pallas_call(
        body, grid=(t // tm,), in_specs=[pl.BlockSpec((nh, tm, c), lambda i: (0, i, 0))],
        out_specs=pl.BlockSpec((tm, c), lambda i: (i, 0)), out_shape=jax.ShapeDtypeStruct((t, c), BF16),
        compiler_params=_cp(("parallel",)), name=name)(x)


MERGE_TM = 256


def _merge_fwd(x, proj, yp, yd, ys, b_gate, wb, wo, *, name):
    t, d = x.shape
    tm = _pick(t, (MERGE_TM, 128))

    def body(x_ref, g0_ref, g1_ref, g2_ref, yp_ref, yd_ref, ys_ref, bg_ref, wb_ref, wo_ref, o_ref):
        merged = jnp.zeros((tm, d), F32)
        for n, (g_ref, y_ref) in enumerate(((g0_ref, yp_ref), (g1_ref, yd_ref), (g2_ref, ys_ref))):
            gate = _sigmoid(g_ref[...] + bg_ref[:, n * d:(n + 1) * d])
            merged = merged + gate * _bdot(y_ref[...], wb_ref[n])
        o_ref[...] = x_ref[...] + _bdot(merged, wo_ref[...])

    row = pl.BlockSpec((tm, d), lambda i: (i, 0))
    yblk = pl.BlockSpec((tm, BW), lambda i: (i, 0))

    def gl(n):
        return pl.BlockSpec((tm, d), lambda i: (i, CB_GATE + n))

    return pl.pallas_call(
        body, grid=(t // tm,),
        in_specs=[row, gl(0), gl(1), gl(2), yblk, yblk, yblk, pl.BlockSpec((1, 3 * d), lambda i: (0, 0)),
                  pl.BlockSpec((3, BW, d), lambda i: (0, 0, 0)), pl.BlockSpec((d, d), lambda i: (0, 0))],
        out_specs=row, out_shape=jax.ShapeDtypeStruct((t, d), F32),
        compiler_params=_cp(("parallel",)), name=name)(x, proj, proj, proj, yp, yd, ys, b_gate, wb, wo)


def _merge_bwd(proj, yp, yd, ys, b_gate, wb, wo, dx, *, name):
    t, d = dx.shape
    tm = _pick(t, (MERGE_TM, 128))

    def body(g0_ref, g1_ref, g2_ref, yp_ref, yd_ref, ys_ref, bg_ref, wb_ref, wo_ref, dx_ref,
             dyp_ref, dyd_ref, dys_ref, dgl_ref, mg_ref, dxh_ref, dbd_ref, dbg_ref):
        dxh = dx_ref[...].astype(BF16)
        dxh_ref[...] = dxh
        dmerged = _bdot(dxh, wo_ref[...], NT_DIMS)
        merged = jnp.zeros((tm, d), F32)

        @pl.when(pl.program_id(0) == 0)
        def _():
            dbg_ref[...] = jnp.zeros_like(dbg_ref)

        for n, (g_ref, y_ref, dy_ref) in enumerate(((g0_ref, yp_ref, dyp_ref), (g1_ref, yd_ref, dyd_ref),
                                                    (g2_ref, ys_ref, dys_ref))):
            gate = _sigmoid(g_ref[...] + bg_ref[:, n * d:(n + 1) * d])
            bd = _bdot(y_ref[...], wb_ref[n])
            merged = merged + gate * bd
            dgl = dmerged * bd * gate * (1.0 - gate)
            dgl_ref[:, n * d:(n + 1) * d] = dgl.astype(BF16)
            dbg_ref[:, n * d:(n + 1) * d] += jnp.sum(dgl, axis=0, keepdims=True)
            dbd = (dmerged * gate).astype(BF16)
            dbd_ref[n] = dbd
            dy_ref[...] = _bdot(dbd, wb_ref[n], NT_DIMS)
        mg_ref[...] = merged.astype(BF16)

    row = pl.BlockSpec((tm, d), lambda i: (i, 0))
    yblk = pl.BlockSpec((tm, BW), lambda i: (i, 0))
    bgv = pl.BlockSpec((1, 3 * d), lambda i: (0, 0))

    def gl(n):
        return pl.BlockSpec((tm, d), lambda i: (i, CB_GATE + n))

    ysds = jax.ShapeDtypeStruct((t, BW), F32)
    return pl.pallas_call(
        body, grid=(t // tm,),
        in_specs=[gl(0), gl(1), gl(2), yblk, yblk, yblk, bgv,
                  pl.BlockSpec((3, BW, d), lambda i: (0, 0, 0)), pl.BlockSpec((d, d), lambda i: (0, 0)), row],
        out_specs=[yblk, yblk, yblk, pl.BlockSpec((tm, 3 * d), lambda i: (i, 0)), row, row,
                   pl.BlockSpec((3, tm, d), lambda i: (0, i, 0)), bgv],
        out_shape=[ysds, ysds, ysds, jax.ShapeDtypeStruct((t, 3 * d), BF16), jax.ShapeDtypeStruct((t, d), BF16),
                   jax.ShapeDtypeStruct((t, d), BF16), jax.ShapeDtypeStruct((3, t, d), BF16),
                   jax.ShapeDtypeStruct((1, 3 * d), F32)],
        compiler_params=_cp(("arbitrary",)), name=name)(proj, proj, proj, yp, yd, ys, b_gate, wb, wo, dx)


def _loss_head(x, g, target, *, name):
    t, d = x.shape
    tm = _pick(t, (512, 256, 128))

    def body(x_ref, g_ref, t_ref, dx_ref, dg_ref, loss_ref):
        xhat, rstd = _rms_stats(x_ref[...])
        err = xhat * g_ref[...] - t_ref[...]
        dx, dg = _rms_bwd_vals(err * (1.0 / d), xhat, rstd, g_ref[...])
        dx_ref[...] = dx

        @pl.when(pl.program_id(0) == 0)
        def _():
            dg_ref[...] = jnp.zeros_like(dg_ref)
            loss_ref[...] = jnp.zeros_like(loss_ref)

        dg_ref[...] += dg
        part = jnp.sum(jnp.sum(err * err, axis=1, keepdims=True), axis=0, keepdims=True) * (0.5 / d)
        loss_ref[...] += jnp.broadcast_to(part, (1, LANE))

    row = pl.BlockSpec((tm, d), lambda i: (i, 0))
    vec = pl.BlockSpec((1, d), lambda i: (0, 0))
    return pl.pallas_call(
        body, grid=(t // tm,), in_specs=[row, vec, row],
        out_specs=[row, vec, pl.BlockSpec((1, LANE), lambda i: (0, 0))],
        out_shape=[jax.ShapeDtypeStruct((t, d), F32), jax.ShapeDtypeStruct((1, d), F32),
                   jax.ShapeDtypeStruct((1, LANE), F32)],
        compiler_params=_cp(("arbitrary",)), name=name)(x, g.reshape(1, d), target)


def _adamw(w, g, m, v, *, name):
    rows, cols = w.shape
    fits = [c for c in (1024, 704, 512, 352, 256, 128, 64, 32, 16, 8) if c * cols * 4 * 14 <= VMEM_LIMIT // 2]
    tr = _pick(rows, fits)
    c1 = 1.0 / (1.0 - ADAM_B1 ** ADAM_STEP)
    c2 = 1.0 / (1.0 - ADAM_B2 ** ADAM_STEP)

    def body(w_ref, g_ref, m_ref, v_ref, d_ref, nm_ref, nv_ref):
        g = g_ref[...]
        nm = ADAM_B1 * m_ref[...] + (1.0 - ADAM_B1) * g
        nv = ADAM_B2 * v_ref[...] + (1.0 - ADAM_B2) * (g * g)
        nm_ref[...] = nm
        nv_ref[...] = nv
        d_ref[...] = -ADAM_LR * ((nm * c1) / (jnp.sqrt(nv * c2) + ADAM_EPS) + ADAM_WD * w_ref[...])

    blk = pl.BlockSpec((tr, cols), lambda i: (i, 0))
    sds = jax.ShapeDtypeStruct((rows, cols), F32)
    return pl.pallas_call(
        body, grid=(rows // tr,), in_specs=[blk] * 4, out_specs=[blk] * 3, out_shape=[sds] * 3,
        compiler_params=_cp(("parallel",)), name=name)(w, g, m, v)


MESH_ID = pl.DeviceIdType.MESH
HBM_SPEC = pl.BlockSpec(memory_space=pl.ANY)
OTHER_CHIPS = ((1, 0), (0, 1), (1, 1))


def _at_slot(ref, nl, slot):
    return ref.at[(slice(None),) * nl + (slot,)]


def _slotted(shape, nl, slots):
    return tuple(shape[:nl]) + (slots,) + tuple(shape[nl:])


def _flip(v, f):
    return 1 - v if f else v


def _comm_call(body, n, out_shapes, n_remote, args, name):
    return pl.pallas_call(
        body, out_shape=out_shapes, in_specs=[HBM_SPEC] * len(args), out_specs=[HBM_SPEC] * len(out_shapes),
        scratch_shapes=[pltpu.SemaphoreType.DMA((n * n_remote,)), pltpu.SemaphoreType.DMA((n * n_remote,)),
                        pltpu.SemaphoreType.DMA((n * 4,))],
        compiler_params=pltpu.CompilerParams(has_side_effects=True), name=name)(*args)


def _gather(xs, nls, *, name):
    n = len(xs)

    def body(*refs):
        start, finish = _gather_parts(refs[:n], refs[n:2 * n], nls, *refs[2 * n:])
        start()
        finish()

    return _comm_call(body, n, _gather_shapes(xs, nls), 7, xs, name)


def _gather_shapes(xs, nls):
    return [jax.ShapeDtypeStruct(_slotted(v.shape, nl, N_DEV), v.dtype) for v, nl in zip(xs, nls)]


def _comm_sems(n, n_remote):
    return [pltpu.SemaphoreType.DMA((n * n_remote,)), pltpu.SemaphoreType.DMA((n * n_remote,)),
            pltpu.SemaphoreType.DMA((n * 4,))]


def _gather_parts(x_refs, o_refs, nls, send_sems, recv_sems, local_sems):
    n = len(x_refs)
    x, y, c = lax.axis_index("x"), lax.axis_index("y"), lax.axis_index("c")
    me, sibling = (x, y, c), (x, y, 1 - c)
    chips = [(_flip(x, fx), _flip(y, fy)) for fx, fy in OTHER_CHIPS]

    def copy(a, k, block, to, src=None):
        dst = _at_slot(o_refs[a], nls[a], 4 * block[0] + 2 * block[1] + block[2])
        return pltpu.make_async_remote_copy(
            src_ref=dst if src is None else src, dst_ref=dst, send_sem=send_sems.at[a * 7 + k],
            recv_sem=recv_sems.at[a * 7 + k], device_id=to, device_id_type=MESH_ID)

    def mine(a):
        return pltpu.make_async_copy(x_refs[a], _at_slot(o_refs[a], nls[a], 4 * x + 2 * y + c), local_sems.at[a])

    def first(a):
        return ([copy(a, 0, me, sibling, src=x_refs[a])]
                + [copy(a, 1 + j, me, (*chip, c), src=x_refs[a]) for j, chip in enumerate(chips)])

    def start():
        for a in range(n):
            mine(a).start()
            for cp in first(a):
                cp.start()

    def finish():
        passed = []
        for j, chip in enumerate(chips):
            for a in range(n):
                copy(a, 1 + j, (*chip, c), me).wait_recv()
                passed.append(copy(a, 4 + j, (*chip, c), sibling))
                passed[-1].start()
        for a in range(n):
            copy(a, 0, sibling, me).wait_recv()
            for j, chip in enumerate(chips):
                copy(a, 4 + j, (*chip, 1 - c), me).wait_recv()
        for a in range(n):
            for cp in first(a):
                cp.wait_send()
        for cp in passed:
            cp.wait_send()
        for a in range(n):
            mine(a).wait()

    return start, finish


ALL_FLIPS = ((0, 0, 1), (0, 1, 0), (0, 1, 1), (1, 0, 0), (1, 0, 1), (1, 1, 0), (1, 1, 1))


def _all_to_all_parts(g_refs, r_refs, nls, send_sems, recv_sems, local_sems):
    del local_sems
    n = len(g_refs)
    x, y, c = lax.axis_index("x"), lax.axis_index("y"), lax.axis_index("c")

    def copies():
        out = []
        for a in range(n):
            for k, (fx, fy, fc) in enumerate(ALL_FLIPS):
                p = (_flip(x, fx), _flip(y, fy), _flip(c, fc))
                out.append(pltpu.make_async_remote_copy(
                    src_ref=_at_slot(g_refs[a], nls[a], 4 * p[0] + 2 * p[1] + p[2]), dst_ref=_at_slot(r_refs[a], nls[a], k),
                    send_sem=send_sems.at[a * 7 + k], recv_sem=recv_sems.at[a * 7 + k], device_id=p,
                    device_id_type=MESH_ID))
        return out

    def start():
        for cp in copies():
            cp.start()

    def finish():
        cps = copies()
        for cp in cps:
            cp.wait_recv()
        for cp in cps:
            cp.wait_send()

    return start, finish


def _all_to_all_shapes(gs, nls):
    return [jax.ShapeDtypeStruct(_slotted(v.shape[:nl] + v.shape[nl + 1:], nl, 7), v.dtype) for v, nl in zip(gs, nls)]


def _scatter_pair(gs, nls, *, name):
    n = len(gs)

    def body(*refs):
        g_refs, got_refs, (send_sems, recv_sems, _) = refs[:n], refs[n:2 * n], refs[2 * n:]
        x, y, c = lax.axis_index("x"), lax.axis_index("y"), lax.axis_index("c")
        remote = []
        for a in range(n):
            for q in range(4):
                rc = pltpu.make_async_remote_copy(
                    src_ref=_at_slot(g_refs[a], nls[a], 2 * q + 1 - c), dst_ref=_at_slot(got_refs[a], nls[a], q),
                    send_sem=send_sems.at[a * 4 + q], recv_sem=recv_sems.at[a * 4 + q], device_id=(x, y, 1 - c),
                    device_id_type=MESH_ID)
                rc.start()
                remote.append(rc)
        for rc in remote:
            rc.wait_recv()
        for rc in remote:
            rc.wait_send()

    outs = [jax.ShapeDtypeStruct(_slotted(v.shape[:nl] + v.shape[nl + 1:], nl, 4), v.dtype) for v, nl in zip(gs, nls)]
    return _comm_call(body, n, outs, 4, gs, name)


def _scatter_chips(ps, nls, *, name):
    n = len(ps)

    def body(*refs):
        p_refs, r_refs, (send_sems, recv_sems, _) = refs[:n], refs[n:2 * n], refs[2 * n:]
        x, y, c = lax.axis_index("x"), lax.axis_index("y"), lax.axis_index("c")
        remote = []
        for a in range(n):
            for k, (fx, fy) in enumerate(OTHER_CHIPS):
                tx, ty = _flip(x, fx), _flip(y, fy)
                rc = pltpu.make_async_remote_copy(
                    src_ref=_at_slot(p_refs[a], nls[a], 2 * tx + ty), dst_ref=_at_slot(r_refs[a], nls[a], k),
                    send_sem=send_sems.at[a * 3 + k], recv_sem=recv_sems.at[a * 3 + k], device_id=(tx, ty, c),
                    device_id_type=MESH_ID)
                rc.start()
                remote.append(rc)
        for rc in remote:
            rc.wait_recv()
        for rc in remote:
            rc.wait_send()

    outs = [jax.ShapeDtypeStruct(_slotted(v.shape[:nl] + v.shape[nl + 1:], nl, 3), v.dtype) for v, nl in zip(ps, nls)]
    return _comm_call(body, n, outs, 3, ps, name)


def _pair_add(g, got, core, *, name):
    rows, cols = g.shape[-2:]
    lf = math.prod(got.shape[:-3])
    tr = _pick(rows, (1024, 512, 352, 256, 128))

    def body(core_ref, g_ref, got_ref, o_ref):
        o_ref[...] = (g_ref[...].astype(F32) + got_ref[...].astype(F32)).astype(BF16)

    blk = pl.BlockSpec((None, None, tr, cols), lambda i, q, j, core_ref: (i, q, j, 0))
    out = pl.pallas_call(
        body, grid_spec=pltpu.PrefetchScalarGridSpec(
            num_scalar_prefetch=1, grid=(lf, 4, rows // tr),
            in_specs=[pl.BlockSpec((None, None, None, tr, cols), lambda i, q, j, core_ref: (i, q, core_ref[0], j, 0)),
                      blk], out_specs=blk),
        out_shape=jax.ShapeDtypeStruct((lf, 4, rows, cols), BF16),
        compiler_params=_cp(("parallel", "parallel", "parallel")), name=name)(
            core, g.reshape(lf, 4, 2, rows, cols), got.reshape(lf, 4, rows, cols))
    return out.reshape(got.shape)


def _sum_adamw(p, r, own, w, m, v, layer, *, name):
    shape = w.shape[1:]
    rows, cols = shape[-2:]
    lf = math.prod(shape[:-2])
    np_, nk = p.shape[-3], r.shape[-3]
    fits = [c for c in (1024, 512, 352, 256, 128, 64, 32, 16) if c * cols * (7 * 4 + (nk + 1) * 2) * 2 <= VMEM_LIMIT // 2]
    tr = _pick(rows, fits)
    c1 = 1.0 / (1.0 - ADAM_B1 ** ADAM_STEP)
    c2 = 1.0 / (1.0 - ADAM_B2 ** ADAM_STEP)

    def body(own_ref, p_ref, r_ref, w_ref, m_ref, v_ref, g_ref, d_ref, nm_ref, nv_ref):
        g = p_ref[...].astype(F32)
        for k in range(nk):
            g = g + r_ref[k].astype(F32)
        g_ref[...] = g
        nm = ADAM_B1 * m_ref[...] + (1.0 - ADAM_B1) * g
        nv = ADAM_B2 * v_ref[...] + (1.0 - ADAM_B2) * (g * g)
        nm_ref[...] = nm
        nv_ref[...] = nv
        d_ref[...] = -ADAM_LR * ((nm * c1) / (jnp.sqrt(nv * c2) + ADAM_EPS) + ADAM_WD * w_ref[...])

    blk = pl.BlockSpec((None, tr, cols), lambda i, j, own_ref: (i, j, 0))
    wblk = pl.BlockSpec((None, None, tr, cols), lambda i, j, own_ref: (layer, i, j, 0))
    sds = jax.ShapeDtypeStruct((lf, rows, cols), F32)
    full = (w.shape[0], lf, rows, cols)
    outs = pl.pallas_call(
        body, grid_spec=pltpu.PrefetchScalarGridSpec(
            num_scalar_prefetch=1, grid=(lf, rows // tr),
            in_specs=[pl.BlockSpec((None, None, tr, cols), lambda i, j, own_ref: (i, own_ref[0], j, 0)),
                      pl.BlockSpec((None, nk, tr, cols), lambda i, j, own_ref: (i, 0, j, 0))] + [wblk] * 3,
            out_specs=[blk] * 4),
        out_shape=[sds] * 4, compiler_params=_cp(("parallel", "parallel")), name=name)(
            own, p.reshape(lf, np_, rows, cols), r.reshape(lf, nk, rows, cols), w.reshape(full), m.reshape(full),
            v.reshape(full))
    return [o.reshape(shape) for o in outs]


def _sum_slots(x, *, name):
    nd, rows, cols = x.shape
    tr = _pick(rows, (512, 256, 128, 64, 32, 16, 8))

    def body(x_ref, o_ref):
        acc = x_ref[0].astype(F32)
        for j in range(1, nd):
            acc = acc + x_ref[j].astype(F32)
        o_ref[...] = acc

    return pl.pallas_call(
        body, grid=(rows // tr,), in_specs=[pl.BlockSpec((nd, tr, cols), lambda i: (0, i, 0))],
        out_specs=pl.BlockSpec((tr, cols), lambda i: (i, 0)), out_shape=jax.ShapeDtypeStruct((rows, cols), F32),
        compiler_params=_cp(("parallel",)), name=name)(x)


def _pad_rows(a, mult=8):
    r = (-a.shape[0]) % mult
    return jnp.pad(a, ((0, r), (0, 0))) if r else a


def _flat128(a):
    f = a.reshape(-1)
    return jnp.pad(f, (0, (-f.shape[0]) % LANE)).reshape(-1, LANE)


def _unshard(gathered, shape, axis):
    g = gathered.reshape((N_DEV,) + tuple(shape))
    g = jnp.moveaxis(g, 0, axis)
    full = list(shape)
    full[axis] *= N_DEV
    return g.reshape(full)


def _col_shards(full):
    rows, cols = full.shape
    return jnp.moveaxis(full.reshape(rows, N_DEV, cols // N_DEV), 1, 0)


BIG = (("ffn_w_gate", 2), ("ffn_w_up", 2), ("ffn_w_down", 2), ("w_in", 1), ("w_branch", 2), ("w_out", 1))


def kernel(x, ffn_norm, ffn_w_gate, ffn_w_up, ffn_w_down, mix_norm, w_in, b_gate, pool_w, pool_scale, dn_conv, dn_A_log, dn_dt_bias, dn_out_norm, w_branch, w_out, final_norm, loss_target, m_ffn_norm, m_ffn_w_gate, m_ffn_w_up, m_ffn_w_down, m_mix_norm, m_w_in, m_b_gate, m_pool_w, m_pool_scale, m_dn_conv, m_dn_A_log, m_dn_dt_bias, m_dn_out_norm, m_w_branch, m_w_out, m_final_norm, v_ffn_norm, v_ffn_w_gate, v_ffn_w_up, v_ffn_w_down, v_mix_norm, v_w_in, v_b_gate, v_pool_w, v_pool_scale, v_dn_conv, v_dn_A_log, v_dn_dt_bias, v_dn_out_norm, v_w_branch, v_w_out, v_final_norm):
    wts = dict(ffn_norm=ffn_norm, ffn_w_gate=ffn_w_gate, ffn_w_up=ffn_w_up, ffn_w_down=ffn_w_down, mix_norm=mix_norm,
               w_in=w_in, b_gate=b_gate, pool_w=pool_w, pool_scale=pool_scale, dn_conv=dn_conv, dn_A_log=dn_A_log,
               dn_dt_bias=dn_dt_bias, dn_out_norm=dn_out_norm, w_branch=w_branch, w_out=w_out, final_norm=final_norm)
    mom = dict(ffn_norm=m_ffn_norm, ffn_w_gate=m_ffn_w_gate, ffn_w_up=m_ffn_w_up, ffn_w_down=m_ffn_w_down,
               mix_norm=m_mix_norm, w_in=m_w_in, b_gate=m_b_gate, pool_w=m_pool_w, pool_scale=m_pool_scale,
               dn_conv=m_dn_conv, dn_A_log=m_dn_A_log, dn_dt_bias=m_dn_dt_bias, dn_out_norm=m_dn_out_norm,
               w_branch=m_w_branch, w_out=m_w_out, final_norm=m_final_norm)
    var = dict(ffn_norm=v_ffn_norm, ffn_w_gate=v_ffn_w_gate, ffn_w_up=v_ffn_w_up, ffn_w_down=v_ffn_w_down,
               mix_norm=v_mix_norm, w_in=v_w_in, b_gate=v_b_gate, pool_w=v_pool_w, pool_scale=v_pool_scale,
               dn_conv=v_dn_conv, dn_A_log=v_dn_A_log, dn_dt_bias=v_dn_dt_bias, dn_out_norm=v_dn_out_norm,
               w_branch=v_w_branch, w_out=v_w_out, final_norm=v_final_norm)
    nb, s, d = x.shape
    t = nb * s
    me = 4 * lax.axis_index("x") + 2 * lax.axis_index("y") + lax.axis_index("c")

    big = [n for n, _ in BIG]
    nls = [nl - 1 for _, nl in BIG]
    shards = lambda l: [wts[n][l].astype(BF16) for n in big]
    small_sh = jnp.concatenate([_flat128(ffn_norm), _flat128(dn_conv)], axis=0)
    *gat0, small_g = _gather(shards(0) + [small_sh], nls + [0], name="gather_weights")
    full = [dict(zip(big, gat0))] + [None] * (DEPTH - 1)

    def mixer_weights(l):
        w_in_full = jnp.moveaxis(full[l]["w_in"], 0, 1).reshape(d, -1)
        w_main = jnp.concatenate([w_in_full[:, :AB_LO], w_in_full[:, AB_HI:]], axis=1)
        w_ab = jnp.pad(w_in_full[:, AB_LO:AB_HI], ((0, 0), (0, LANE - (AB_HI - AB_LO))))
        wb = jnp.moveaxis(full[l]["w_branch"], 1, 2).reshape(3, BW, d)
        return w_main, w_ab, wb, full[l]["w_out"].reshape(d, d)

    nfr = ffn_norm.size // LANE
    ffn_norm_full = _unshard(small_g[:, :nfr], ffn_norm.shape, 2)
    dn_conv_full = _unshard(small_g[:, nfr:], dn_conv.shape, 2)
    pool_w_h = pool_w.astype(BF16)

    xs = x.reshape(t, d)
    saved = []
    for l in range(DEPTH):
        sv = dict(x0=xs)
        xs = _ffn_fwd(xs, ffn_norm_full[l, 0], full[l]["ffn_w_gate"][0], full[l]["ffn_w_up"][0],
                      full[l]["ffn_w_down"][0], name="ffn_fwd")
        sv["x1"] = xs
        w_main, w_ab, wb, wo = mixer_weights(l)
        h = _rms_fwd(xs, mix_norm[l], name="mix_rms")
        proj = _mm(h, w_main, name="proj")
        ab = _mm(h, w_ab, name="proj_ab")
        par = jnp.pad(jnp.stack([dn_A_log[l], dn_dt_bias[l]]), ((0, 6), (0, LANE - NH)))
        gain = dn_out_norm[l].reshape(1, HD)
        psc = pool_scale[l].reshape(1, BW)
        yp = _pool_fwd(proj, pool_w_h[l], psc, nb, s, name="pool_fwd")
        yd, o_pre, states, gat = _dn_fwd(proj, ab, dn_conv_full[l], par, gain, nb, s,
                                         name="dn_fwd" if l == DEPTH - 1 else "dn_fwd_gather",
                                         gather=(shards(l + 1), nls) if l < DEPTH - 1 else None)
        if l < DEPTH - 1:
            full[l + 1] = dict(zip(big, gat))
        ys, sb_ctr = _sb_fwd(proj, nb, s, name="sb_fwd")
        bg = b_gate[l].reshape(1, 3 * d)
        xs = _merge_fwd(xs, proj, yp, yd, ys, bg, wb, wo, name="merge_fwd")
        sv.update(x2=xs, h=h, proj=proj, ab=ab, par=par, gain=gain, psc=psc, yp=yp, yd=yd, ys=ys, sb_ctr=sb_ctr, o_pre=o_pre,
                  states=states, bg=bg, w_main=w_main, w_ab=w_ab, wb=wb, wo=wo)
        xs = _ffn_fwd(xs, ffn_norm_full[l, 1], full[l]["ffn_w_gate"][1], full[l]["ffn_w_up"][1],
                      full[l]["ffn_w_down"][1], name="ffn_fwd")
        saved.append(sv)

    dx, g_final, loss_row = _loss_head(xs, final_norm, loss_target.reshape(t, d), name="loss_head")
    loss = lax.psum(loss_row[0, 0], ("x", "y", "c"))

    gw = {n: [None] * DEPTH for n in ("ffn_norm", "ffn_w_gate", "ffn_w_up", "ffn_w_down", "mix_norm", "w_in", "b_gate",
                                      "pool_w", "pool_scale", "dn_conv", "dn_A_log", "dn_dt_bias", "dn_out_norm",
                                      "w_branch", "w_out")}

    me_i = me.astype(jnp.int32).reshape(1)
    grads, delta, new_m, new_v = ({n: [None] * DEPTH for n in big} for _ in range(4))
    pending = None

    def finish_layer(l, own_blocks, arrived, own_slot):
        for n, p, r in zip(big, own_blocks, arrived):
            grads[n][l], delta[n][l], new_m[n][l], new_v[n][l] = _sum_adamw(
                p, r, own_slot, wts[n], mom[n], var[n], l, name=f"adamw_{n}_{l}")

    def ffn_back(l, i, x_in, dy):
        dxi, dg, hb, dyh, da, db, sact = _ffn_bwd(x_in, ffn_norm_full[l, i], full[l]["ffn_w_gate"][i],
                                                  full[l]["ffn_w_up"][i], full[l]["ffn_w_down"][i], dy, name="ffn_bwd")
        return dxi, dg, (_mm_slots(hb, da, name="dw_gate_up"), _mm_slots(hb, db, name="dw_gate_up"),
                         _mm_slots(sact, dyh, name="dw_down"))

    for l in reversed(range(DEPTH)):
        sv = saved[l]
        dx, dg1, (dwg1, dwu1, dwd1) = ffn_back(l, 1, sv["x2"], dx)
        dyp, dyd, dys, dgl, merged, dxh, dbd, dbg = _merge_bwd(sv["proj"], sv["yp"], sv["yd"], sv["ys"], sv["bg"],
                                                               sv["wb"], sv["wo"], dx, name="merge_bwd")
        gw["w_out"][l] = _mm(merged, dxh, ta=True, out_dtype=BF16, name="dw_out").reshape(N_DEV, d // N_DEV, d)
        gw["w_branch"][l] = jnp.stack([_col_shards(_mm(y, dbd[n], ta=True, out_dtype=BF16, name="dw_branch"))
                                       for n, y in enumerate((sv["yp"], sv["yd"], sv["ys"]))])
        gw["b_gate"][l] = dbg.reshape(3 * d)
        du, dpw, dps = _pool_bwd(sv["proj"], pool_w_h[l], sv["psc"], dyp, nb, s, name="pool_bwd")
        gw["pool_w"][l], gw["pool_scale"][l] = dpw, dps.reshape(BW)
        dqr, dkr, dvr, dz, dab4, dcq, dck, dcv, dpar, dgain, arrived = _dn_bwd(
            sv["proj"], sv["ab"], dn_conv_full[l], sv["par"], sv["gain"], sv["o_pre"], sv["states"], dyd, nb, s,
            name="dn_bwd_scatter" if pending is not None else "dn_bwd",
            scatter=([gw[n][pending] for n in big], nls) if pending is not None else None)
        if pending is not None:
            finish_layer(pending, [gw[n][pending] for n in big], arrived, me_i)
        gw["dn_conv"][l] = jnp.concatenate([dcq, dck, dcv], axis=1)
        gw["dn_A_log"][l], gw["dn_dt_bias"][l], gw["dn_out_norm"][l] = dpar[:, 0, 0], dpar[:, 1, 0], dgain.reshape(HD)
        dsq, dsk, dsv = _sb_bwd(sv["proj"], sv["sb_ctr"], dys, nb, s, name="sb_bwd")
        dab = _sum_heads(dab4, name="sum_heads")
        dproj = jnp.concatenate([du.astype(BF16), dqr.astype(BF16), dkr.astype(BF16), dvr.astype(BF16),
                                 dz.astype(BF16), dsq.astype(BF16), dsk.astype(BF16), dsv.astype(BF16), dgl], axis=1)
        dw_main = _mm(sv["h"], dproj, ta=True, out_dtype=BF16, name="dw_in")
        dw_ab = _mm(sv["h"], dab, ta=True, out_dtype=BF16, name="dw_ab")
        gw["w_in"][l] = _col_shards(jnp.concatenate([dw_main[:, :AB_LO], dw_ab[:, :AB_HI - AB_LO],
                                                     dw_main[:, AB_LO:]], axis=1))
        dh_main = _mm(dproj, sv["w_main"], tb=True, name="dh_mix")
        dh_ab = _mm(dab, sv["w_ab"], tb=True, name="dh_mix_ab")
        dx, dgm = _rms_bwd(sv["x1"], mix_norm[l], dh_main, dh_ab, dx, name="mix_rms_bwd")
        gw["mix_norm"][l] = dgm.reshape(d)
        dx, dg0, (dwg0, dwu0, dwd0) = ffn_back(l, 0, sv["x0"], dx)
        gw["ffn_norm"][l] = jnp.stack([dg0.reshape(d), dg1.reshape(d)])
        gw["ffn_w_gate"][l] = jnp.stack([dwg0, dwg1])
        gw["ffn_w_up"][l] = jnp.stack([dwu0, dwu1])
        gw["ffn_w_down"][l] = jnp.stack([dwd0, dwd1])
        pending = l
    grad_x = dx.reshape(nb, s, d)

    core = lax.axis_index("c").astype(jnp.int32).reshape(1)
    chip = (2 * lax.axis_index("x") + lax.axis_index("y")).astype(jnp.int32).reshape(1)
    last = [gw[n][0] for n in big]
    got = _scatter_pair(last, nls, name="scatter_grads_pair")
    chip_sums = [_pair_add(g, b, core, name="add_pair_" + n) for n, g, b in zip(big, last, got)]
    finish_layer(0, chip_sums, _scatter_chips(chip_sums, nls, name="scatter_grads_chips"), chip)
    grads, delta, new_m, new_v = ({n: jnp.stack(v[n]) for n in big} for v in (grads, delta, new_m, new_v))
    gw = {n: jnp.stack(v) for n, v in gw.items() if n not in big}
    gw["final_norm"] = g_final.reshape(d)

    small = ("ffn_norm", "mix_norm", "b_gate", "pool_w", "pool_scale", "dn_conv", "dn_A_log", "dn_dt_bias",
             "dn_out_norm", "final_norm")
    sp = _pad_rows(jnp.concatenate([_flat128(gw[n]) for n in small], axis=0))
    ssum = _sum_slots(_gather([sp], [0], name="gather_small_grads")[0], name="sum_small_grads")
    off = 0
    for n in small:
        r = -(-gw[n].size // LANE)
        g = ssum[off:off + r].reshape(-1)[:gw[n].size].reshape(gw[n].shape)
        off += r
        if n in ("ffn_norm", "dn_conv"):
            w = wts[n].shape[2]
            g = lax.dynamic_slice_in_dim(g, me * w, w, axis=2)
        grads[n] = g

    pk = lambda src: _pad_rows(jnp.concatenate([_flat128(src[n]) for n in small], axis=0))
    dl, nm, nv = _adamw(pk(wts), pk(grads), pk(mom), pk(var), name="adamw_small")
    off = 0
    for n in small:
        r = -(-wts[n].size // LANE)
        for dst, src in ((delta, dl), (new_m, nm), (new_v, nv)):
            dst[n] = src[off:off + r].reshape(-1)[:wts[n].size].reshape(wts[n].shape)
        off += r

    order = ("ffn_norm", "ffn_w_gate", "ffn_w_up", "ffn_w_down", "mix_norm", "w_in", "b_gate", "pool_w", "pool_scale",
             "dn_conv", "dn_A_log", "dn_dt_bias", "dn_out_norm", "w_branch", "w_out", "final_norm")
    return (loss, grad_x, *[grads[n] for n in order], *[delta[n] for n in order], *[new_m[n] for n in order],
            *[new_v[n] for n in order])
```

```python
import functools
import math

import jax
import jax.numpy as jnp
from jax import lax
from jax.experimental import pallas as pl
from jax.experimental.pallas import tpu as pltpu

F32, BF16 = jnp.float32, jnp.bfloat16
D_MODEL, D_FF, DEPTH = 1024, 2816, 4
BW = 512
HD = 128
NH = 4
DN_CHUNK = 64
EPS = 1e-6
N_DEV = 8
LANE = 128
CB_POOL, CB_DNQ, CB_DNK, CB_DNV, CB_DNZ, CB_SBQ, CB_SBK, CB_SBV = 0, 4, 8, 12, 16, 20, 24, 28
CB_GATE = 4
P_MAIN = 7168
AB_LO, AB_HI = 2560, 2568
ADAM_LR, ADAM_B1, ADAM_B2, ADAM_EPS, ADAM_WD, ADAM_STEP = 0.001, 0.9, 0.999, 1e-08, 0.01, 10
VMEM_LIMIT = 56 * 1024 * 1024
HIGHEST = lax.Precision.HIGHEST
NT_DIMS = (((1,), (1,)), ((), ()))
TN_DIMS = (((0,), (0,)), ((), ()))
NN_DIMS = (((1,), (0,)), ((), ()))


def _cp(dims=None, vmem=VMEM_LIMIT):
    return pltpu.CompilerParams(dimension_semantics=dims, vmem_limit_bytes=vmem)


def _pick(n, cands):
    for c in cands:
        if n % c == 0:
            return c
    return n


def _bdot(a, b, dims=NN_DIMS):
    return lax.dot_general(a.astype(BF16), b.astype(BF16), dims, preferred_element_type=F32)


def _hdot(a, b, dims=NN_DIMS):
    return lax.dot_general(a, b, dims, precision=lax.Precision.HIGH, preferred_element_type=F32)


def _split_dot(x, m01):
    hi = x.astype(BF16)
    lo = (x - hi.astype(F32)).astype(BF16)
    return (lax.dot_general(hi, m01, NN_DIMS, preferred_element_type=F32)
            + lax.dot_general(lo, m01, NN_DIMS, preferred_element_type=F32))


def _sigmoid(x):
    return 1.0 / (1.0 + jnp.exp(-x))


def _log_sigmoid(x):
    return jnp.minimum(x, 0.0) - jnp.log1p(jnp.exp(-jnp.abs(x)))


def _softplus(x):
    return jnp.maximum(x, 0.0) + jnp.log1p(jnp.exp(-jnp.abs(x)))


def _shift_down(x, k):
    r = lax.broadcasted_iota(jnp.int32, x.shape, 0)
    return jnp.where(r >= k, pltpu.roll(x, k, 0), 0.0)


def _shift_up(x, k):
    n = x.shape[0]
    r = lax.broadcasted_iota(jnp.int32, x.shape, 0)
    return jnp.where(r < n - k, pltpu.roll(x, n - k, 0), 0.0)


def _mm(a, b, *, ta=False, tb=False, out_dtype=F32, name):
    (kk, m) = a.shape if ta else a.shape[::-1]
    (k2, n) = b.shape[::-1] if tb else b.shape
    assert kk == k2, (a.shape, b.shape, ta, tb)
    bm = _pick(m, (1024, 512, 256, 128))
    bn = _pick(n, (1024, 1408, 512, 256, 128))
    bk = _pick(kk, (512, 256, 128))
    nk = kk // bk
    dims = (((0 if ta else 1,), (1 if tb else 0,)), ((), ()))

    def body(a_ref, b_ref, o_ref, acc_ref):
        k = pl.program_id(2)

        @pl.when(k == 0)
        def _():
            acc_ref[...] = jnp.zeros_like(acc_ref)

        acc_ref[...] += lax.dot_general(a_ref[...].astype(BF16), b_ref[...].astype(BF16), dims,
                                        preferred_element_type=F32)

        @pl.when(k == nk - 1)
        def _():
            o_ref[...] = acc_ref[...].astype(out_dtype)

    a_spec = (pl.BlockSpec((bk, bm), lambda i, j, k: (k, i)) if ta else pl.BlockSpec((bm, bk), lambda i, j, k: (i, k)))
    b_spec = (pl.BlockSpec((bn, bk), lambda i, j, k: (j, k)) if tb else pl.BlockSpec((bk, bn), lambda i, j, k: (k, j)))
    return pl.pallas_call(
        body, grid=(m // bm, n // bn, nk), in_specs=[a_spec, b_spec],
        out_specs=pl.BlockSpec((bm, bn), lambda i, j, k: (i, j)),
        out_shape=jax.ShapeDtypeStruct((m, n), out_dtype),
        scratch_shapes=[pltpu.VMEM((bm, bn), F32)],
        compiler_params=_cp(("parallel", "parallel", "arbitrary")), name=name)(a, b)


def _mm_slots(a, b, *, name):
    a3, b3 = a.ndim == 3, b.ndim == 3
    ns = a.shape[0] if a3 else b.shape[0]
    m, t = a.shape[-2:]
    n = b.shape[-1]
    bm, bn, bk = _pick(m, (1024, 512, 256, 128)), _pick(n, (1024, 512, 256, 128)), _pick(t, (1024, 512, 256, 128))
    nk = t // bk

    def body(a_ref, b_ref, o_ref, acc_ref):
        k = pl.program_id(3)

        @pl.when(k == 0)
        def _():
            acc_ref[...] = jnp.zeros_like(acc_ref)

        acc_ref[...] += _bdot(a_ref[...], b_ref[...])

        @pl.when(k == nk - 1)
        def _():
            o_ref[...] = acc_ref[...].astype(BF16)

    a_spec = (pl.BlockSpec((None, bm, bk), lambda s, i, j, k: (s, i, k)) if a3
              else pl.BlockSpec((bm, bk), lambda s, i, j, k: (i, k)))
    b_spec = (pl.BlockSpec((None, bk, bn), lambda s, i, j, k: (s, k, j)) if b3
              else pl.BlockSpec((bk, bn), lambda s, i, j, k: (k, j)))
    return pl.pallas_call(
        body, grid=(ns, m // bm, n // bn, nk), in_specs=[a_spec, b_spec],
        out_specs=pl.BlockSpec((None, bm, bn), lambda s, i, j, k: (s, i, j)),
        out_shape=jax.ShapeDtypeStruct((ns, m, n), BF16), scratch_shapes=[pltpu.VMEM((bm, bn), F32)],
        compiler_params=_cp(("parallel", "parallel", "parallel", "arbitrary")), name=name)(a, b)


def _rms_stats(x):
    rstd = lax.rsqrt(jnp.mean(x * x, axis=-1, keepdims=True) + EPS)
    return x * rstd, rstd


def _rms_bwd_vals(dh, xhat, rstd, g):
    dxh = dh * g
    dx = rstd * (dxh - xhat * jnp.mean(dxh * xhat, axis=-1, keepdims=True))
    return dx, jnp.sum(dh * xhat, axis=0, keepdims=True)


def _rms_fwd(x, g, *, name):
    t, d = x.shape
    tm = _pick(t, (512, 256, 128))

    def body(x_ref, g_ref, h_ref):
        xhat, _ = _rms_stats(x_ref[...])
        h_ref[...] = (xhat * g_ref[...]).astype(BF16)

    return pl.pallas_call(
        body, grid=(t // tm,),
        in_specs=[pl.BlockSpec((tm, d), lambda i: (i, 0)), pl.BlockSpec((1, d), lambda i: (0, 0))],
        out_specs=pl.BlockSpec((tm, d), lambda i: (i, 0)), out_shape=jax.ShapeDtypeStruct((t, d), BF16),
        compiler_params=_cp(("parallel",)), name=name)(x, g.reshape(1, d))


def _rms_bwd(x, g, dh_a, dh_b, dres, *, name):
    t, d = x.shape
    tm = _pick(t, (512, 256, 128))

    def body(x_ref, g_ref, dha_ref, dhb_ref, dres_ref, dx_ref, dg_ref):
        xhat, rstd = _rms_stats(x_ref[...])
        dx, dg = _rms_bwd_vals(dha_ref[...] + dhb_ref[...], xhat, rstd, g_ref[...])
        dx_ref[...] = dres_ref[...] + dx

        @pl.when(pl.program_id(0) == 0)
        def _():
            dg_ref[...] = jnp.zeros_like(dg_ref)

        dg_ref[...] += dg

    row = pl.BlockSpec((tm, d), lambda i: (i, 0))
    vec = pl.BlockSpec((1, d), lambda i: (0, 0))
    return pl.pallas_call(
        body, grid=(t // tm,), in_specs=[row, vec, row, row, row], out_specs=[row, vec],
        out_shape=[jax.ShapeDtypeStruct((t, d), F32), jax.ShapeDtypeStruct((1, d), F32)],
        compiler_params=_cp(("arbitrary",)), name=name)(x, g.reshape(1, d), dh_a, dh_b, dres)


FFN_TM = 512


def _ffn_fwd(x, g, wg, wu, wd, *, name):
    t, d = x.shape
    nf, _, fc = wg.shape
    tm = _pick(t, (FFN_TM, 256, 128))

    def body(x_ref, g_ref, wg_ref, wu_ref, wd_ref, o_ref, h_ref, acc_ref):
        j = pl.program_id(1)

        @pl.when(j == 0)
        def _():
            xhat, _ = _rms_stats(x_ref[...])
            h_ref[...] = (xhat * g_ref[...]).astype(BF16)
            acc_ref[...] = jnp.zeros_like(acc_ref)

        h = h_ref[...]
        a = _bdot(h, wg_ref[...])
        b = _bdot(h, wu_ref[...])
        s = a * _sigmoid(a) * b
        acc_ref[...] += _bdot(s, wd_ref[...])

        @pl.when(j == nf - 1)
        def _():
            o_ref[...] = x_ref[...] + 0.5 * acc_ref[...]

    row = pl.BlockSpec((tm, d), lambda i, j: (i, 0))
    return pl.pallas_call(
        body, grid=(t // tm, nf),
        in_specs=[row, pl.BlockSpec((1, d), lambda i, j: (0, 0)),
                  pl.BlockSpec((None, d, fc), lambda i, j: (j, 0, 0)), pl.BlockSpec((None, d, fc), lambda i, j: (j, 0, 0)),
                  pl.BlockSpec((None, fc, d), lambda i, j: (j, 0, 0))],
        out_specs=row, out_shape=jax.ShapeDtypeStruct((t, d), F32),
        scratch_shapes=[pltpu.VMEM((tm, d), BF16), pltpu.VMEM((tm, d), F32)],
        compiler_params=_cp(("parallel", "arbitrary")), name=name)(x, g.reshape(1, d), wg, wu, wd)


def _ffn_bwd(x, g, wg, wu, wd, dy, *, name):
    t, d = x.shape
    nf, _, fc = wg.shape
    tm = _pick(t, (FFN_TM, 256, 128))

    def body(x_ref, g_ref, wg_ref, wu_ref, wd_ref, dy_ref,
             dx_ref, dg_ref, ht_ref, dyh_ref, da_ref, db_ref, st_ref, acc_ref, h_ref):
        i, j = pl.program_id(0), pl.program_id(1)

        @pl.when(j == 0)
        def _():
            xhat, _ = _rms_stats(x_ref[...])
            hf = xhat * g_ref[...]
            h_ref[...] = hf.astype(BF16)
            ht_ref[...] = hf.T.astype(BF16)
            dyh_ref[...] = (0.5 * dy_ref[...]).astype(BF16)
            acc_ref[...] = jnp.zeros_like(acc_ref)

        h = h_ref[...]
        a = _bdot(h, wg_ref[...])
        b = _bdot(h, wu_ref[...])
        sg = _sigmoid(a)
        silu = a * sg
        st_ref[...] = (silu * b).T.astype(BF16)
        ds = _bdot(dyh_ref[...], wd_ref[...], NT_DIMS)
        da = (ds * b * (sg * (1.0 + a * (1.0 - sg)))).astype(BF16)
        db = (ds * silu).astype(BF16)
        da_ref[...] = da
        db_ref[...] = db
        acc_ref[...] += _bdot(da, wg_ref[...], NT_DIMS) + _bdot(db, wu_ref[...], NT_DIMS)

        @pl.when((i == 0) & (j == 0))
        def _():
            dg_ref[...] = jnp.zeros_like(dg_ref)

        @pl.when(j == nf - 1)
        def _():
            xhat, rstd = _rms_stats(x_ref[...])
            dx, dg = _rms_bwd_vals(acc_ref[...], xhat, rstd, g_ref[...])
            dx_ref[...] = dy_ref[...] + dx
            dg_ref[...] += dg

    row = pl.BlockSpec((tm, d), lambda i, j: (i, 0))
    vec = pl.BlockSpec((1, d), lambda i, j: (0, 0))
    fblk = pl.BlockSpec((None, tm, fc), lambda i, j: (j, i, 0))
    return pl.pallas_call(
        body, grid=(t // tm, nf),
        in_specs=[row, vec, pl.BlockSpec((None, d, fc), lambda i, j: (j, 0, 0)),
                  pl.BlockSpec((None, d, fc), lambda i, j: (j, 0, 0)), pl.BlockSpec((None, fc, d), lambda i, j: (j, 0, 0)),
                  row],
        out_specs=[row, vec, pl.BlockSpec((d, tm), lambda i, j: (0, i)), row, fblk, fblk,
                   pl.BlockSpec((None, fc, tm), lambda i, j: (j, 0, i))],
        out_shape=[jax.ShapeDtypeStruct((t, d), F32), jax.ShapeDtypeStruct((1, d), F32),
                   jax.ShapeDtypeStruct((d, t), BF16), jax.ShapeDtypeStruct((t, d), BF16),
                   jax.ShapeDtypeStruct((nf, t, fc), BF16), jax.ShapeDtypeStruct((nf, t, fc), BF16),
                   jax.ShapeDtypeStruct((nf, fc, t), BF16)],
        scratch_shapes=[pltpu.VMEM((tm, d), F32), pltpu.VMEM((tm, d), BF16)],
        compiler_params=_cp(("arbitrary", "arbitrary")), name=name)(x, g.reshape(1, d), wg, wu, wd, dy)


def _pool_core(u, grp):
    s = u.shape[0]
    w2 = u + _shift_down(u, 1)
    w4 = w2 + _shift_down(w2, 2)
    w8 = w4 + _shift_down(w4, 4)
    w16 = w8 + _shift_down(w8, 8)
    wsum = jnp.where(grp == 0, w2, jnp.where(grp == 1, w4, jnp.where(grp == 2, w8, w16)))
    win = jnp.left_shift(2, grp).astype(F32)
    t1 = (lax.broadcasted_iota(jnp.int32, (s, 1), 0) + 1).astype(F32)
    inv = 1.0 / jnp.minimum(t1, win)
    return wsum * inv - u, inv


def _pool_fwd(proj, pool_w, pool_scale, nb, s, *, name):
    def body(u_ref, w_ref, sc_ref, y_ref):
        pooled, _ = _pool_core(u_ref[...], pl.program_id(0))
        y_ref[...] = _bdot(pooled, w_ref[...]) * sc_ref[...]

    return pl.pallas_call(
        body, grid=(NH, nb),
        in_specs=[pl.BlockSpec((s, HD), lambda g, b: (b, CB_POOL + g)),
                  pl.BlockSpec((None, HD, HD), lambda g, b: (g, 0, 0)), pl.BlockSpec((1, HD), lambda g, b: (0, g))],
        out_specs=pl.BlockSpec((s, HD), lambda g, b: (b, g)),
        out_shape=jax.ShapeDtypeStruct((nb * s, BW), F32),
        compiler_params=_cp(("parallel", "parallel")), name=name)(proj, pool_w, pool_scale)


def _pool_bwd(proj, pool_w, pool_scale, dy, nb, s, *, name):
    def body(u_ref, w_ref, sc_ref, dy_ref, du_ref, dw_ref, dsc_ref):
        grp, b = pl.program_id(0), pl.program_id(1)
        pooled, inv = _pool_core(u_ref[...], grp)
        mixed = _bdot(pooled, w_ref[...])
        dy = dy_ref[...]
        dmixed = dy * sc_ref[...]
        dpooled = _bdot(dmixed, w_ref[...], NT_DIMS)
        r = dpooled * inv
        v2 = r + _shift_up(r, 1)
        v4 = v2 + _shift_up(v2, 2)
        v8 = v4 + _shift_up(v4, 4)
        v16 = v8 + _shift_up(v8, 8)
        vsum = jnp.where(grp == 0, v2, jnp.where(grp == 1, v4, jnp.where(grp == 2, v8, v16)))
        du_ref[...] = vsum - dpooled

        @pl.when(b == 0)
        def _():
            dw_ref[...] = jnp.zeros_like(dw_ref)
            dsc_ref[...] = jnp.zeros_like(dsc_ref)

        dw_ref[...] += _bdot(pooled, dmixed, TN_DIMS)
        dsc_ref[...] += jnp.sum(dy * mixed, axis=0, keepdims=True)

    return pl.pallas_call(
        body, grid=(NH, nb),
        in_specs=[pl.BlockSpec((s, HD), lambda g, b: (b, CB_POOL + g)),
                  pl.BlockSpec((None, HD, HD), lambda g, b: (g, 0, 0)), pl.BlockSpec((1, HD), lambda g, b: (0, g)),
                  pl.BlockSpec((s, HD), lambda g, b: (b, g))],
        out_specs=[pl.BlockSpec((s, HD), lambda g, b: (b, g)), pl.BlockSpec((None, HD, HD), lambda g, b: (g, 0, 0)),
                   pl.BlockSpec((1, HD), lambda g, b: (0, g))],
        out_shape=[jax.ShapeDtypeStruct((nb * s, BW), F32), jax.ShapeDtypeStruct((NH, HD, HD), F32),
                   jax.ShapeDtypeStruct((1, BW), F32)],
        compiler_params=_cp(("arbitrary", "arbitrary")), name=name)(proj, pool_w, pool_scale, dy)


SB_BLK = 128


SB_G = 4
SB_KG = SB_G * SB_BLK


def _sb_block(qb, kg, q0, k0, diagonal):
    z = _bdot(qb, kg, NT_DIMS) * (HD ** -0.5)
    lsz = _log_sigmoid(z)
    if not diagonal:
        return lsz, lsz - z, None
    row = lax.broadcasted_iota(jnp.int32, (SB_BLK, SB_KG), 0) + q0
    col = lax.broadcasted_iota(jnp.int32, (SB_BLK, SB_KG), 1) + k0
    causal = col < row
    return lsz, jnp.where(causal, lsz - z, 0.0), causal


def _keep(causal, x):
    return x if causal is None else jnp.where(causal, x, 0.0)


def _sub(x, m):
    return x[:, m * SB_BLK:(m + 1) * SB_BLK]


def _sb_tails(lnm, after, ct):
    tails, cts = [None] * SB_G, [None] * SB_G
    for m in reversed(range(SB_G)):
        cts[m] = ct
        tails[m] = _split_dot(_sub(lnm, m), after) + ct
        ct = ct + jnp.sum(_sub(lnm, m), axis=1, keepdims=True)
    return jnp.concatenate(tails, axis=1), cts, ct


def _tri01(lower):
    r = lax.broadcasted_iota(jnp.int32, (SB_BLK, SB_BLK), 0)
    c = lax.broadcasted_iota(jnp.int32, (SB_BLK, SB_BLK), 1)
    return jnp.where((r < c) if lower else (r > c), 1.0, 0.0).astype(BF16)


def _split3(x):
    hi = x.astype(BF16)
    mid = (x - hi.astype(F32)).astype(BF16)
    lo = (x - hi.astype(F32) - mid.astype(F32)).astype(BF16)
    return hi, mid, lo


def _row_sums_as_rows(x):
    ones = jnp.ones((8, x.shape[1]), BF16)
    return sum(lax.dot_general(ones, p, NT_DIMS, preferred_element_type=F32) for p in _split3(x))


def _rows_to_cols(rows):
    eighth = jnp.full((8, LANE), 0.125, BF16)
    return sum(lax.dot_general(p, eighth, TN_DIMS, preferred_element_type=F32) for p in _split3(rows))


def _sb_fwd(proj, nb, s, *, name):
    nq = s // SB_BLK
    ng = nq // SB_G

    def body(q_ref, k_ref, v_ref, o_ref, ctr_ref):
        after = _tri01(False)

        def qblock(i, _):
            q0 = pl.multiple_of(i * SB_BLK, SB_BLK)
            qb = q_ref[pl.ds(q0, SB_BLK), :]

            def kgroup(g, carry, diagonal):
                acc, ct, ctr = carry
                k0 = pl.multiple_of(g * SB_KG, SB_KG)
                lsz, lnm, causal = _sb_block(qb, k_ref[pl.ds(k0, SB_KG), :], q0, k0, diagonal)
                ctr_ref[i * ng + g] = ctr
                tail, _, ct = _sb_tails(lnm, after, ct)
                w = _keep(causal, jnp.exp(lsz + tail))
                return acc + _bdot(w, v_ref[pl.ds(k0, SB_KG), :]), ct, ctr + _row_sums_as_rows(lnm)

            gd = i // SB_G
            carry = kgroup(gd, (jnp.zeros((SB_BLK, HD), F32), jnp.zeros((SB_BLK, 1), F32), jnp.zeros((8, LANE), F32)),
                           True)
            acc, _, _ = lax.fori_loop(0, gd, lambda jj, c: kgroup(gd - 1 - jj, c, False), carry)
            o_ref[pl.ds(q0, SB_BLK), :] = acc
            return 0

        lax.fori_loop(0, nq, qblock, 0)

    def col(cb):
        return pl.BlockSpec((s, HD), lambda b, h: (b, cb + h))

    return pl.pallas_call(
        body, grid=(nb, NH), in_specs=[col(CB_SBQ), col(CB_SBK), col(CB_SBV)],
        out_specs=[pl.BlockSpec((s, HD), lambda b, h: (b, h)),
                   pl.BlockSpec((None, None, nq * ng, 8, LANE), lambda b, h: (b, h, 0, 0, 0))],
        out_shape=[jax.ShapeDtypeStruct((nb * s, BW), F32), jax.ShapeDtypeStruct((nb, NH, nq * ng, 8, LANE), F32)],
        compiler_params=_cp(("parallel", "parallel")), name=name)(proj, proj, proj)


def _sb_bwd(proj, ctr, dy, nb, s, *, name):
    nq = s // SB_BLK
    ng = nq // SB_G
    scale = HD ** -0.5

    def body(q_ref, k_ref, v_ref, ctr_ref, do_ref, dq_ref, dk_ref, dv_ref):
        after = _tri01(False)
        before = _tri01(True)
        dk_ref[...] = jnp.zeros_like(dk_ref)
        dv_ref[...] = jnp.zeros_like(dv_ref)

        def qblock(i, _):
            q0 = pl.multiple_of(i * SB_BLK, SB_BLK)
            qb = q_ref[pl.ds(q0, SB_BLK), :]
            dob = do_ref[pl.ds(q0, SB_BLK), :]

            def kgroup(g, carry, diagonal):
                dq, ce = carry
                k0 = pl.multiple_of(g * SB_KG, SB_KG)
                kg = k_ref[pl.ds(k0, SB_KG), :]
                vg = v_ref[pl.ds(k0, SB_KG), :]
                lsz, lnm, causal = _sb_block(qb, kg, q0, k0, diagonal)
                tail, _, _ = _sb_tails(lnm, after, _rows_to_cols(ctr_ref[i * ng + g])[:, 0:1])
                w = _keep(causal, jnp.exp(lsz + tail))
                e = _bdot(dob, vg, NT_DIMS) * w
                pres = []
                for m in range(SB_G):
                    pres.append(_split_dot(_sub(e, m), before) + ce)
                    ce = ce + jnp.sum(_sub(e, m), axis=1, keepdims=True)
                sig = jnp.exp(lsz)
                dz = _keep(causal, e * (1.0 - sig) - jnp.concatenate(pres, axis=1) * sig) * scale
                dk_ref[pl.ds(k0, SB_KG), :] += _bdot(dz, qb, TN_DIMS)
                dv_ref[pl.ds(k0, SB_KG), :] += _bdot(w, dob, TN_DIMS)
                return dq + _bdot(dz, kg), ce

            gd = i // SB_G
            carry = lax.fori_loop(0, gd, lambda g, c: kgroup(g, c, False),
                                  (jnp.zeros((SB_BLK, HD), F32), jnp.zeros((SB_BLK, 1), F32)))
            dq, _ = kgroup(gd, carry, True)
            dq_ref[pl.ds(q0, SB_BLK), :] = dq
            return 0

        lax.fori_loop(0, nq, qblock, 0)

    def col(cb):
        return pl.BlockSpec((s, HD), lambda b, h: (b, cb + h))

    out = pl.BlockSpec((s, HD), lambda b, h: (b, h))
    sds = jax.ShapeDtypeStruct((nb * s, BW), F32)
    return pl.pallas_call(
        body, grid=(nb, NH),
        in_specs=[col(CB_SBQ), col(CB_SBK), col(CB_SBV),
                  pl.BlockSpec((None, None, nq * ng, 8, LANE), lambda b, h: (b, h, 0, 0, 0)), out],
        out_specs=[out, out, out], out_shape=[sds, sds, sds],
        compiler_params=_cp(("parallel", "parallel")), name=name)(proj, proj, proj, ctr, dy)


def _make_cdot(dims, dims_da, dims_db, swap_a=False, swap_b=False):
    @jax.custom_vjp
    def f(a, b):
        return _bdot(a, b, dims)

    def fwd(a, b):
        return _bdot(a, b, dims), (a, b)

    def bwd(res, g):
        a, b = res
        da = _bdot(b, g, dims_da) if swap_a else _bdot(g, b, dims_da)
        db = _bdot(g, a, dims_db) if swap_b else _bdot(a, g, dims_db)
        return da, db

    f.defvjp(fwd, bwd)
    return f


_cdot = _make_cdot(NN_DIMS, NT_DIMS, TN_DIMS)
_cdot_nt = _make_cdot(NT_DIMS, NN_DIMS, TN_DIMS, swap_b=True)
_cdot_tn = _make_cdot(TN_DIMS, NT_DIMS, NN_DIMS, swap_a=True)


DN_SUPER = 4 * DN_CHUNK


@jax.custom_vjp
def _unit_lower_inverse(lmat):
    n = lmat.shape[0]
    steps = int(math.log2(DN_CHUNK))
    eye = jnp.where(lax.broadcasted_iota(jnp.int32, (n, n), 0) == lax.broadcasted_iota(jnp.int32, (n, n), 1), 1.0, 0.0)
    inv = eye - lmat
    pw = _hdot(lmat, lmat)
    for it in range(steps - 1):
        inv = inv + _hdot(inv, pw)
        if it < steps - 2:
            pw = _hdot(pw, pw)
    return inv


def _unit_lower_inverse_fwd(lmat):
    inv = _unit_lower_inverse(lmat)
    return inv, inv


def _unit_lower_inverse_bwd(inv, g):
    return (-_hdot(_hdot(inv, g, TN_DIMS), inv, NT_DIMS),)


_unit_lower_inverse.defvjp(_unit_lower_inverse_fwd, _unit_lower_inverse_bwd)


def _dn_local(q, k, v, bb, gb):
    n = q.shape[0]
    r = lax.broadcasted_iota(jnp.int32, (n, n), 0)
    cc = lax.broadcasted_iota(jnp.int32, (n, n), 1)
    shift = int(math.log2(DN_CHUNK))
    same = lax.shift_right_logical(r, shift) == lax.shift_right_logical(cc, shift)
    incl = jnp.where(same, jnp.where(r >= cc, 1.0, 0.0), 0.0)
    strict = jnp.where(same, jnp.where(r > cc, 1.0, 0.0), 0.0)
    gc = _hdot(incl, gb)
    gc_row = _hdot(jnp.full((n, HD), 1.0 / HD, F32), gc, NT_DIMS)
    diff = jnp.concatenate([gc] * (n // HD), axis=1) - gc_row
    decay = incl * jnp.exp(diff * incl)
    kb = k * bb
    lmat = _cdot_nt(kb, k) * (strict * decay)
    egc = jnp.exp(gc)
    inv = _unit_lower_inverse(lmat)
    u = _hdot(inv, v * bb)
    w = _hdot(inv, kb * egc)
    attn = _cdot_nt(q, k) * decay
    gl = _hdot(jnp.where(same, 1.0, 0.0), gb)
    return u, w, attn, q * egc, k * jnp.exp(gl - gc), jnp.exp(gl)


def _attn_pairs(attn):
    return jnp.concatenate([attn[:HD, :HD], attn[HD:, HD:]], axis=0)


def _attn_unpairs(a):
    z = jnp.zeros((HD, HD), F32)
    return jnp.concatenate([jnp.concatenate([a[:HD], z], axis=1), jnp.concatenate([z, a[HD:]], axis=1)], axis=0)


def _dn_step(u, w, a, qd, kd, cdrows, state, odd):
    v_new = u - _cdot(w, state)
    z = jnp.zeros_like(v_new)
    o = _cdot(qd, state) + _cdot(a, jnp.concatenate([z, v_new] if odd else [v_new, z], axis=0))
    return o, state * jnp.mean(cdrows, axis=0, keepdims=True) + _cdot_tn(kd, v_new)


def _dn_local_pass(fn, s, ins, outs):
    def step(it, _):
        sl = pl.ds(pl.multiple_of(it * DN_SUPER, DN_SUPER), DN_SUPER)
        res = fn(*[ref[sl, :] for ref in ins])
        for ref, val in zip(outs, res):
            ref[sl, :] = val
        return 0

    lax.fori_loop(0, s // DN_SUPER, step, 0)


def _lane_pick(row, idx):
    lane = lax.broadcasted_iota(jnp.int32, row.shape, 1)
    return jnp.sum(jnp.where(lane == idx, row, 0.0), axis=1, keepdims=True)


def _col_pick(x, idx):
    lane = lax.broadcasted_iota(jnp.int32, x.shape, 1)
    return jnp.sum(jnp.where(lane == idx, x, 0.0), axis=1, keepdims=True)


def _conv_silu(x, w):
    xc = (w[3:4, :] * x + w[2:3, :] * _shift_down(x, 1) + w[1:2, :] * _shift_down(x, 2)
          + w[0:1, :] * _shift_down(x, 3))
    return xc * _sigmoid(xc), xc


def _conv_silu_bwd(x, w, xc, dxs, dw_ref):
    sg = _sigmoid(xc)
    dxc = dxs * (sg * (1.0 + xc * (1.0 - sg)))
    dx = (w[3:4, :] * dxc + w[2:3, :] * _shift_up(dxc, 1) + w[1:2, :] * _shift_up(dxc, 2)
          + w[0:1, :] * _shift_up(dxc, 3))
    dw_ref[3:4, :] += jnp.sum(dxc * x, axis=0, keepdims=True)
    dw_ref[2:3, :] += jnp.sum(dxc * _shift_down(x, 1), axis=0, keepdims=True)
    dw_ref[1:2, :] += jnp.sum(dxc * _shift_down(x, 2), axis=0, keepdims=True)
    dw_ref[0:1, :] += jnp.sum(dxc * _shift_down(x, 3), axis=0, keepdims=True)
    return dx


def _dn_prep(qr_ref, kr_ref, vr_ref, ab_ref, cq_ref, ck_ref, cv_ref, par_ref, head):
    qs, qc = _conv_silu(qr_ref[...], cq_ref[...])
    ks, kc = _conv_silu(kr_ref[...], ck_ref[...])
    vs, vc = _conv_silu(vr_ref[...], cv_ref[...])
    rq = lax.rsqrt(jnp.sum(qs * qs, axis=1, keepdims=True) + EPS)
    rk = lax.rsqrt(jnp.sum(ks * ks, axis=1, keepdims=True) + EPS)
    ab = ab_ref[...]
    a_in = _col_pick(ab, head) + _lane_pick(par_ref[1:2, :], head)
    beta = _sigmoid(_col_pick(ab, NH + head))
    neg_ea = -jnp.exp(_lane_pick(par_ref[0:1, :], head))
    g = neg_ea * _softplus(a_in)
    return dict(q=qs * rq * (HD ** -0.5), k=ks * rk, v=vs, beta=beta, g=g, qs=qs, ks=ks, qc=qc, kc=kc, vc=vc,
                rq=rq, rk=rk, a_in=a_in, neg_ea=neg_ea)


ONE_BUF = pl.Buffered(1)
DN_BWD_VMEM = 62 * 1024 * 1024


def _dn_specs(nb, s):
    def col(cb):
        return pl.BlockSpec((s, HD), lambda h, b: (b, cb + h), pipeline_mode=ONE_BUF)

    def conv(cb):
        return pl.BlockSpec((DN_CONV_W, HD), lambda h, b: (0, cb + h))

    return col, conv


DN_CONV_W = 4


def _with_exchange(body, n_in, n_out, n_scratch, exchange, grid):
    if exchange is None:
        return body
    parts_fn, n, nls = exchange

    def wrapped(*refs):
        ins, xs = refs[:n_in], refs[n_in:n_in + n]
        outs, os = refs[n_in + n:n_in + n + n_out], refs[n_in + n + n_out:n_in + 2 * n + n_out]
        rest = refs[n_in + 2 * n + n_out:]
        scratch, sems = rest[:n_scratch], rest[n_scratch:]
        pos = [pl.program_id(k) for k in range(len(grid))]
        first = functools.reduce(jnp.logical_and, [p == 0 for p in pos])
        last = functools.reduce(jnp.logical_and, [p == g - 1 for p, g in zip(pos, grid)])
        start, finish = parts_fn(xs, os, nls, *sems)
        pl.when(first)(start)
        body(*ins, *outs, *scratch)
        pl.when(last)(finish)

    return wrapped


def _dn_fwd(proj, ab, conv_w, par, gain, nb, s, *, name, gather=None):
    nc = s // DN_CHUNK
    col, conv = _dn_specs(nb, s)
    gx, gnl = gather if gather else ([], [])

    def body(qr_ref, kr_ref, vr_ref, z_ref, ab_ref, cq_ref, ck_ref, cv_ref, par_ref, gain_ref,
             y_ref, o_ref, st_ref, u_ref, w_ref, at_ref, qd_ref, kd_ref, cd_ref, q_s, k_s, v_s, bb_s, gb_s):
        p = _dn_prep(qr_ref, kr_ref, vr_ref, ab_ref, cq_ref, ck_ref, cv_ref, par_ref, pl.program_id(0))
        q_s[...], k_s[...], v_s[...] = p["q"], p["k"], p["v"]
        bb_s[...] = jnp.broadcast_to(p["beta"], (s, HD))
        gb_s[...] = jnp.broadcast_to(p["g"], (s, HD))
        def local(*args):
            u, w, attn, qd, kd, cd = _dn_local(*args)
            return u, w, _attn_pairs(attn), qd, kd, cd

        _dn_local_pass(local, s, [q_s, k_s, v_s, bb_s, gb_s], [u_ref, w_ref, at_ref, qd_ref, kd_ref, cd_ref])

        def chunk_pair(pi, state):
            for odd in (0, 1):
                ci = 2 * pi + odd
                sl = pl.ds(pl.multiple_of(ci * DN_CHUNK, DN_CHUNK), DN_CHUNK)
                st_ref[ci] = state
                o, state = _dn_step(u_ref[sl, :], w_ref[sl, :], at_ref[sl, :], qd_ref[sl, :], kd_ref[sl, :],
                                    cd_ref[sl, :], state, odd)
                o_ref[sl, :] = o
            return state

        lax.fori_loop(0, nc // 2, chunk_pair, jnp.zeros((HD, HD), F32))
        o = o_ref[...]
        z = z_ref[...]
        on = o * lax.rsqrt(jnp.mean(o * o, axis=1, keepdims=True) + EPS) * gain_ref[...]
        y_ref[...] = on * (z * _sigmoid(z))

    out = pl.BlockSpec((s, HD), lambda h, b: (b, h))
    sds = jax.ShapeDtypeStruct((nb * s, BW), F32)
    exchange = (_gather_parts, len(gx), gnl) if gx else None
    res = pl.pallas_call(
        _with_exchange(body, 10, 9, 5, exchange, (NH, nb)), grid=(NH, nb),
        in_specs=[col(CB_DNQ), col(CB_DNK), col(CB_DNV), col(CB_DNZ), pl.BlockSpec((s, LANE), lambda h, b: (b, 0)),
                  conv(0), conv(NH), conv(2 * NH), pl.BlockSpec((8, LANE), lambda h, b: (0, 0)),
                  pl.BlockSpec((1, HD), lambda h, b: (0, 0))] + [HBM_SPEC] * len(gx),
        out_specs=[out, out, pl.BlockSpec((None, None, nc, HD, HD), lambda h, b: (b, h, 0, 0, 0))] + [out] * 6
        + [HBM_SPEC] * len(gx),
        out_shape=[sds, sds, jax.ShapeDtypeStruct((nb, NH, nc, HD, HD), F32)] + [sds] * 6 + _gather_shapes(gx, gnl),
        scratch_shapes=[pltpu.VMEM((s, HD), F32)] * 5 + (_comm_sems(len(gx), 7) if gx else []),
        compiler_params=_cp(("arbitrary", "arbitrary")), name=name)(
            proj, proj, proj, proj, ab, conv_w, conv_w, conv_w, par, gain, *gx)
    return res[0], res[1], res[2], list(res[3:9]), list(res[9:])


def _dn_bwd(proj, ab, conv_w, par, gain, o_pre, states, local, dy, nb, s, *, name, scatter=None):
    nc = s // DN_CHUNK
    col, conv = _dn_specs(nb, s)
    gx, gnl = scatter if scatter else ([], [])

    def body(qr_ref, kr_ref, vr_ref, z_ref, ab_ref, cq_ref, ck_ref, cv_ref, par_ref, gain_ref, o_ref, st_ref, dy_ref,
             u_hbm, w_hbm, at_hbm, qd_hbm, kd_hbm, cd_hbm,
             dqr_ref, dkr_ref, dvr_ref, dz_ref, dab_ref, dcq_ref, dck_ref, dcv_ref, dpar_ref, dgain_ref,
             q_s, k_s, v_s, bb_s, gb_s, do_s, u_s, w_s, qd_s, kd_s, at_s, cd_s, load_sems):
        head, b = pl.program_id(0), pl.program_id(1)
        local_refs = [u_s, w_s, at_s, qd_s, kd_s, cd_s]
        loads = [pltpu.make_async_copy(src.at[pl.ds(pl.multiple_of(b * s, s), s), pl.ds(pl.multiple_of(head * HD, HD), HD)],
                                       dst, load_sems.at[i])
                 for i, (src, dst) in enumerate(zip((u_hbm, w_hbm, at_hbm, qd_hbm, kd_hbm, cd_hbm), local_refs))]
        for cp in loads:
            cp.start()
        p = _dn_prep(qr_ref, kr_ref, vr_ref, ab_ref, cq_ref, ck_ref, cv_ref, par_ref, head)
        q_s[...], k_s[...], v_s[...] = p["q"], p["k"], p["v"]
        bb_s[...] = jnp.broadcast_to(p["beta"], (s, HD))
        gb_s[...] = jnp.broadcast_to(p["g"], (s, HD))

        @pl.when(b == 0)
        def _():
            for ref in (dcq_ref, dck_ref, dcv_ref, dpar_ref):
                ref[...] = jnp.zeros_like(ref)

        @pl.when((b == 0) & (head == 0))
        def _():
            dgain_ref[...] = jnp.zeros_like(dgain_ref)

        o, z, dy = o_ref[...], z_ref[...], dy_ref[...]
        rstd = lax.rsqrt(jnp.mean(o * o, axis=1, keepdims=True) + EPS)
        ohat = o * rstd
        sgz = _sigmoid(z)
        dz_ref[...] = dy * (ohat * gain_ref[...]) * (sgz * (1.0 + z * (1.0 - sgz)))
        don = dy * (z * sgz)
        dgain_ref[...] += jnp.sum(don * ohat, axis=0, keepdims=True)
        dxh = don * gain_ref[...]
        do_s[...] = rstd * (dxh - ohat * jnp.mean(dxh * ohat, axis=1, keepdims=True))

        for cp in loads:
            cp.wait()

        def chunk_pair(pr, dstate):
            for odd in (1, 0):
                ci = nc - 1 - 2 * pr - (1 - odd)
                sl = pl.ds(pl.multiple_of(ci * DN_CHUNK, DN_CHUNK), DN_CHUNK)
                _, vjp = jax.vjp(functools.partial(_dn_step, odd=odd), u_s[sl, :], w_s[sl, :], at_s[sl, :],
                                 qd_s[sl, :], kd_s[sl, :], cd_s[sl, :], st_ref[ci])
                du, dw, dat, dqd, dkd, dcd, dstate = vjp((do_s[sl, :], dstate))
                u_s[sl, :], w_s[sl, :], at_s[sl, :], qd_s[sl, :], kd_s[sl, :], cd_s[sl, :] = du, dw, dat, dqd, dkd, dcd
            return dstate

        lax.fori_loop(0, nc // 2, chunk_pair, jnp.zeros((HD, HD), F32))

        def local_bwd(q, k, v, bb, gb, du, dw, dat, dqd, dkd, dcd):
            _, vjp = jax.vjp(_dn_local, q, k, v, bb, gb)
            dq, dk, dv, dbb, dgb = vjp((du, dw, _attn_unpairs(dat), dqd, dkd, dcd))
            return (dq, dk, dv, jnp.broadcast_to(jnp.sum(dbb, axis=1, keepdims=True), (DN_SUPER, HD)),
                    jnp.broadcast_to(jnp.sum(dgb, axis=1, keepdims=True), (DN_SUPER, HD)))

        _dn_local_pass(local_bwd, s, [q_s, k_s, v_s, bb_s, gb_s] + local_refs, [q_s, k_s, v_s, bb_s, gb_s])

        dq, dk, dv = q_s[...], k_s[...], v_s[...]
        qs, ks, rq, rk = p["qs"], p["ks"], p["rq"], p["rk"]
        dqs = (HD ** -0.5) * (rq * dq - qs * (rq * rq * rq) * jnp.sum(dq * qs, axis=1, keepdims=True))
        dks = rk * dk - ks * (rk * rk * rk) * jnp.sum(dk * ks, axis=1, keepdims=True)
        dqr_ref[...] = _conv_silu_bwd(qr_ref[...], cq_ref[...], p["qc"], dqs, dcq_ref)
        dkr_ref[...] = _conv_silu_bwd(kr_ref[...], ck_ref[...], p["kc"], dks, dck_ref)
        dvr_ref[...] = _conv_silu_bwd(vr_ref[...], cv_ref[...], p["vc"], dv, dcv_ref)

        dbeta, dg = bb_s[:, 0:1], gb_s[:, 0:1]
        beta = p["beta"]
        db_logit = dbeta * beta * (1.0 - beta)
        da = dg * p["neg_ea"] * _sigmoid(p["a_in"])
        lane = lax.broadcasted_iota(jnp.int32, (s, LANE), 1)
        dab_ref[...] = jnp.where(lane == head, da, 0.0) + jnp.where(lane == NH + head, db_logit, 0.0)
        dpar_ref[0:1, :] += jnp.broadcast_to(jnp.sum(dg * p["g"], axis=0, keepdims=True), (1, LANE))
        dpar_ref[1:2, :] += jnp.broadcast_to(jnp.sum(da, axis=0, keepdims=True), (1, LANE))

    out = pl.BlockSpec((s, HD), lambda h, b: (b, h))
    in_blk = pl.BlockSpec((s, HD), lambda h, b: (b, h), pipeline_mode=ONE_BUF)
    cblk = pl.BlockSpec((DN_CONV_W, HD), lambda h, b: (0, h))
    sds = jax.ShapeDtypeStruct((nb * s, BW), F32)
    csds = jax.ShapeDtypeStruct((DN_CONV_W, BW), F32)
    exchange = (_all_to_all_parts, len(gx), gnl) if gx else None
    res = pl.pallas_call(
        _with_exchange(body, 19, 10, 13, exchange, (NH, nb)), grid=(NH, nb),
        in_specs=[col(CB_DNQ), col(CB_DNK), col(CB_DNV), col(CB_DNZ),
                  pl.BlockSpec((s, LANE), lambda h, b: (b, 0), pipeline_mode=ONE_BUF),
                  conv(0), conv(NH), conv(2 * NH), pl.BlockSpec((8, LANE), lambda h, b: (0, 0)),
                  pl.BlockSpec((1, HD), lambda h, b: (0, 0)), in_blk,
                  pl.BlockSpec((None, None, nc, HD, HD), lambda h, b: (b, h, 0, 0, 0), pipeline_mode=ONE_BUF), in_blk]
        + [HBM_SPEC] * (6 + len(gx)),
        out_specs=[out, out, out, out, pl.BlockSpec((None, s, LANE), lambda h, b: (h, b, 0)), cblk, cblk, cblk,
                   pl.BlockSpec((None, 8, LANE), lambda h, b: (h, 0, 0)), pl.BlockSpec((1, HD), lambda h, b: (0, 0))]
        + [HBM_SPEC] * len(gx),
        out_shape=[sds, sds, sds, sds, jax.ShapeDtypeStruct((NH, nb * s, LANE), F32), csds, csds, csds,
                   jax.ShapeDtypeStruct((NH, 8, LANE), F32), jax.ShapeDtypeStruct((1, HD), F32)]
        + _all_to_all_shapes(gx, gnl),
        scratch_shapes=[pltpu.VMEM((s, HD), F32)] * 12 + [pltpu.SemaphoreType.DMA((6,))]
        + (_comm_sems(len(gx), 7) if gx else []),
        compiler_params=_cp(("arbitrary", "arbitrary"), DN_BWD_VMEM), name=name)(
            proj, proj, proj, proj, ab, conv_w, conv_w, conv_w, par, gain, o_pre, states, dy, *local, *gx)
    return tuple(res[:10]) + (list(res[10:]),)


def _sum_heads(x, *, name):
    nh, t, c = x.shape
    tm = _pick(t, (1024, 512, 256, 128))

    def body(x_ref, o_ref):
        o_ref[...] = (x_ref[0] + x_ref[1] + x_ref[2] + x_ref[3]).astype(BF16)

    return pl.pallas_call(
        body, grid=(t // tm,), in_specs=[pl.BlockSpec((nh, tm, c), lambda i: (0, i, 0))],
        out_specs=pl.BlockSpec((tm, c), lambda i: (i, 0)), out_shape=jax.ShapeDtypeStruct((t, c), BF16),
        compiler_params=_cp(("parallel",)), name=name)(x)


MERGE_TM = 256


def _merge_fwd(x, proj, yp, yd, ys, b_gate, wb, wo, *, name):
    t, d = x.shape
    tm = _pick(t, (MERGE_TM, 128))

    def body(x_ref, g0_ref, g1_ref, g2_ref, yp_ref, yd_ref, ys_ref, bg_ref, wb_ref, wo_ref, o_ref):
        merged = jnp.zeros((tm, d), F32)
        for n, (g_ref, y_ref) in enumerate(((g0_ref, yp_ref), (g1_ref, yd_ref), (g2_ref, ys_ref))):
            gate = _sigmoid(g_ref[...] + bg_ref[:, n * d:(n + 1) * d])
            merged = merged + gate * _bdot(y_ref[...], wb_ref[n])
        o_ref[...] = x_ref[...] + _bdot(merged, wo_ref[...])

    row = pl.BlockSpec((tm, d), lambda i: (i, 0))
    yblk = pl.BlockSpec((tm, BW), lambda i: (i, 0))

    def gl(n):
        return pl.BlockSpec((tm, d), lambda i: (i, CB_GATE + n))

    return pl.pallas_call(
        body, grid=(t // tm,),
        in_specs=[row, gl(0), gl(1), gl(2), yblk, yblk, yblk, pl.BlockSpec((1, 3 * d), lambda i: (0, 0)),
                  pl.BlockSpec((3, BW, d), lambda i: (0, 0, 0)), pl.BlockSpec((d, d), lambda i: (0, 0))],
        out_specs=row, out_shape=jax.ShapeDtypeStruct((t, d), F32),
        compiler_params=_cp(("parallel",)), name=name)(x, proj, proj, proj, yp, yd, ys, b_gate, wb, wo)


def _merge_bwd(proj, yp, yd, ys, b_gate, wb, wo, dx, *, name):
    t, d = dx.shape
    tm = _pick(t, (MERGE_TM, 128))

    def body(g0_ref, g1_ref, g2_ref, yp_ref, yd_ref, ys_ref, bg_ref, wb_ref, wo_ref, dx_ref,
             dyp_ref, dyd_ref, dys_ref, dgl_ref, mg_ref, dxh_ref, dbd_ref, dbg_ref):
        dxh = dx_ref[...].astype(BF16)
        dxh_ref[...] = dxh
        dmerged = _bdot(dxh, wo_ref[...], NT_DIMS)
        merged = jnp.zeros((tm, d), F32)

        @pl.when(pl.program_id(0) == 0)
        def _():
            dbg_ref[...] = jnp.zeros_like(dbg_ref)

        for n, (g_ref, y_ref, dy_ref) in enumerate(((g0_ref, yp_ref, dyp_ref), (g1_ref, yd_ref, dyd_ref),
                                                    (g2_ref, ys_ref, dys_ref))):
            gate = _sigmoid(g_ref[...] + bg_ref[:, n * d:(n + 1) * d])
            bd = _bdot(y_ref[...], wb_ref[n])
            merged = merged + gate * bd
            dgl = dmerged * bd * gate * (1.0 - gate)
            dgl_ref[:, n * d:(n + 1) * d] = dgl.astype(BF16)
            dbg_ref[:, n * d:(n + 1) * d] += jnp.sum(dgl, axis=0, keepdims=True)
            dbd = (dmerged * gate).astype(BF16)
            dbd_ref[n] = dbd
            dy_ref[...] = _bdot(dbd, wb_ref[n], NT_DIMS)
        mg_ref[...] = merged.astype(BF16)

    row = pl.BlockSpec((tm, d), lambda i: (i, 0))
    yblk = pl.BlockSpec((tm, BW), lambda i: (i, 0))
    bgv = pl.BlockSpec((1, 3 * d), lambda i: (0, 0))

    def gl(n):
        return pl.BlockSpec((tm, d), lambda i: (i, CB_GATE + n))

    ysds = jax.ShapeDtypeStruct((t, BW), F32)
    return pl.pallas_call(
        body, grid=(t // tm,),
        in_specs=[gl(0), gl(1), gl(2), yblk, yblk, yblk, bgv,
                  pl.BlockSpec((3, BW, d), lambda i: (0, 0, 0)), pl.BlockSpec((d, d), lambda i: (0, 0)), row],
        out_specs=[yblk, yblk, yblk, pl.BlockSpec((tm, 3 * d), lambda i: (i, 0)), row, row,
                   pl.BlockSpec((3, tm, d), lambda i: (0, i, 0)), bgv],
        out_shape=[ysds, ysds, ysds, jax.ShapeDtypeStruct((t, 3 * d), BF16), jax.ShapeDtypeStruct((t, d), BF16),
                   jax.ShapeDtypeStruct((t, d), BF16), jax.ShapeDtypeStruct((3, t, d), BF16),
                   jax.ShapeDtypeStruct((1, 3 * d), F32)],
        compiler_params=_cp(("arbitrary",)), name=name)(proj, proj, proj, yp, yd, ys, b_gate, wb, wo, dx)


def _loss_head(x, g, target, *, name):
    t, d = x.shape
    tm = _pick(t, (512, 256, 128))

    def body(x_ref, g_ref, t_ref, dx_ref, dg_ref, loss_ref):
        xhat, rstd = _rms_stats(x_ref[...])
        err = xhat * g_ref[...] - t_ref[...]
        dx, dg = _rms_bwd_vals(err * (1.0 / d), xhat, rstd, g_ref[...])
        dx_ref[...] = dx

        @pl.when(pl.program_id(0) == 0)
        def _():
            dg_ref[...] = jnp.zeros_like(dg_ref)
            loss_ref[...] = jnp.zeros_like(loss_ref)

        dg_ref[...] += dg
        part = jnp.sum(jnp.sum(err * err, axis=1, keepdims=True), axis=0, keepdims=True) * (0.5 / d)
        loss_ref[...] += jnp.broadcast_to(part, (1, LANE))

    row = pl.BlockSpec((tm, d), lambda i: (i, 0))
    vec = pl.BlockSpec((1, d), lambda i: (0, 0))
    return pl.pallas_call(
        body, grid=(t // tm,), in_specs=[row, vec, row],
        out_specs=[row, vec, pl.BlockSpec((1, LANE), lambda i: (0, 0))],
        out_shape=[jax.ShapeDtypeStruct((t, d), F32), jax.ShapeDtypeStruct((1, d), F32),
                   jax.ShapeDtypeStruct((1, LANE), F32)],
        compiler_params=_cp(("arbitrary",)), name=name)(x, g.reshape(1, d), target)


def _adamw(w, g, m, v, *, name):
    rows, cols = w.shape
    fits = [c for c in (1024, 704, 512, 352, 256, 128, 64, 32, 16, 8) if c * cols * 4 * 14 <= VMEM_LIMIT // 2]
    tr = _pick(rows, fits)
    c1 = 1.0 / (1.0 - ADAM_B1 ** ADAM_STEP)
    c2 = 1.0 / (1.0 - ADAM_B2 ** ADAM_STEP)

    def body(w_ref, g_ref, m_ref, v_ref, d_ref, nm_ref, nv_ref):
        g = g_ref[...]
        nm = ADAM_B1 * m_ref[...] + (1.0 - ADAM_B1) * g
        nv = ADAM_B2 * v_ref[...] + (1.0 - ADAM_B2) * (g * g)
        nm_ref[...] = nm
        nv_ref[...] = nv
        d_ref[...] = -ADAM_LR * ((nm * c1) / (jnp.sqrt(nv * c2) + ADAM_EPS) + ADAM_WD * w_ref[...])

    blk = pl.BlockSpec((tr, cols), lambda i: (i, 0))
    sds = jax.ShapeDtypeStruct((rows, cols), F32)
    return pl.pallas_call(
        body, grid=(rows // tr,), in_specs=[blk] * 4, out_specs=[blk] * 3, out_shape=[sds] * 3,
        compiler_params=_cp(("parallel",)), name=name)(w, g, m, v)


MESH_ID = pl.DeviceIdType.MESH
HBM_SPEC = pl.BlockSpec(memory_space=pl.ANY)
OTHER_CHIPS = ((1, 0), (0, 1), (1, 1))


def _at_slot(ref, nl, slot):
    return ref.at[(slice(None),) * nl + (slot,)]


def _slotted(shape, nl, slots):
    return tuple(shape[:nl]) + (slots,) + tuple(shape[nl:])


def _flip(v, f):
    return 1 - v if f else v


def _comm_call(body, n, out_shapes, n_remote, args, name):
    return pl.pallas_call(
        body, out_shape=out_shapes, in_specs=[HBM_SPEC] * len(args), out_specs=[HBM_SPEC] * len(out_shapes),
        scratch_shapes=[pltpu.SemaphoreType.DMA((n * n_remote,)), pltpu.SemaphoreType.DMA((n * n_remote,)),
                        pltpu.SemaphoreType.DMA((n * 4,))],
        compiler_params=pltpu.CompilerParams(has_side_effects=True), name=name)(*args)


def _gather(xs, nls, *, name):
    n = len(xs)

    def body(*refs):
        start, finish = _gather_parts(refs[:n], refs[n:2 * n], nls, *refs[2 * n:])
        start()
        finish()

    return _comm_call(body, n, _gather_shapes(xs, nls), 7, xs, name)


def _gather_shapes(xs, nls):
    return [jax.ShapeDtypeStruct(_slotted(v.shape, nl, N_DEV), v.dtype) for v, nl in zip(xs, nls)]


def _comm_sems(n, n_remote):
    return [pltpu.SemaphoreType.DMA((n * n_remote,)), pltpu.SemaphoreType.DMA((n * n_remote,)),
            pltpu.SemaphoreType.DMA((n * 4,))]


def _gather_parts(x_refs, o_refs, nls, send_sems, recv_sems, local_sems):
    n = len(x_refs)
    x, y, c = lax.axis_index("x"), lax.axis_index("y"), lax.axis_index("c")
    me, sibling = (x, y, c), (x, y, 1 - c)
    chips = [(_flip(x, fx), _flip(y, fy)) for fx, fy in OTHER_CHIPS]

    def copy(a, k, block, to, src=None):
        dst = _at_slot(o_refs[a], nls[a], 4 * block[0] + 2 * block[1] + block[2])
        return pltpu.make_async_remote_copy(
            src_ref=dst if src is None else src, dst_ref=dst, send_sem=send_sems.at[a * 7 + k],
            recv_sem=recv_sems.at[a * 7 + k], device_id=to, device_id_type=MESH_ID)

    def mine(a):
        return pltpu.make_async_copy(x_refs[a], _at_slot(o_refs[a], nls[a], 4 * x + 2 * y + c), local_sems.at[a])

    def first(a):
        return ([copy(a, 0, me, sibling, src=x_refs[a])]
                + [copy(a, 1 + j, me, (*chip, c), src=x_refs[a]) for j, chip in enumerate(chips)])

    def start():
        for a in range(n):
            mine(a).start()
            for cp in first(a):
                cp.start()

    def finish():
        passed = []
        for j, chip in enumerate(chips):
            for a in range(n):
                copy(a, 1 + j, (*chip, c), me).wait_recv()
                passed.append(copy(a, 4 + j, (*chip, c), sibling))
                passed[-1].start()
        for a in range(n):
            copy(a, 0, sibling, me).wait_recv()
            for j, chip in enumerate(chips):
                copy(a, 4 + j, (*chip, 1 - c), me).wait_recv()
        for a in range(n):
            for cp in first(a):
                cp.wait_send()
        for cp in passed:
            cp.wait_send()
        for a in range(n):
            mine(a).wait()

    return start, finish


ALL_FLIPS = ((0, 0, 1), (0, 1, 0), (0, 1, 1), (1, 0, 0), (1, 0, 1), (1, 1, 0), (1, 1, 1))


def _all_to_all_parts(g_refs, r_refs, nls, send_sems, recv_sems, local_sems):
    del local_sems
    n = len(g_refs)
    x, y, c = lax.axis_index("x"), lax.axis_index("y"), lax.axis_index("c")

    def copies():
        out = []
        for a in range(n):
            for k, (fx, fy, fc) in enumerate(ALL_FLIPS):
                p = (_flip(x, fx), _flip(y, fy), _flip(c, fc))
                out.append(pltpu.make_async_remote_copy(
                    src_ref=_at_slot(g_refs[a], nls[a], 4 * p[0] + 2 * p[1] + p[2]), dst_ref=_at_slot(r_refs[a], nls[a], k),
                    send_sem=send_sems.at[a * 7 + k], recv_sem=recv_sems.at[a * 7 + k], device_id=p,
                    device_id_type=MESH_ID))
        return out

    def start():
        for cp in copies():
            cp.start()

    def finish():
        cps = copies()
        for cp in cps:
            cp.wait_recv()
        for cp in cps:
            cp.wait_send()

    return start, finish


def _all_to_all_shapes(gs, nls):
    return [jax.ShapeDtypeStruct(_slotted(v.shape[:nl] + v.shape[nl + 1:], nl, 7), v.dtype) for v, nl in zip(gs, nls)]


def _scatter_pair(gs, nls, *, name):
    n = len(gs)

    def body(*refs):
        g_refs, got_refs, (send_sems, recv_sems, _) = refs[:n], refs[n:2 * n], refs[2 * n:]
        x, y, c = lax.axis_index("x"), lax.axis_index("y"), lax.axis_index("c")
        remote = []
        for a in range(n):
            for q in range(4):
                rc = pltpu.make_async_remote_copy(
                    src_ref=_at_slot(g_refs[a], nls[a], 2 * q + 1 - c), dst_ref=_at_slot(got_refs[a], nls[a], q),
                    send_sem=send_sems.at[a * 4 + q], recv_sem=recv_sems.at[a * 4 + q], device_id=(x, y, 1 - c),
                    device_id_type=MESH_ID)
                rc.start()
                remote.append(rc)
        for rc in remote:
            rc.wait_recv()
        for rc in remote:
            rc.wait_send()

    outs = [jax.ShapeDtypeStruct(_slotted(v.shape[:nl] + v.shape[nl + 1:], nl, 4), v.dtype) for v, nl in zip(gs, nls)]
    return _comm_call(body, n, outs, 4, gs, name)


def _scatter_chips(ps, nls, *, name):
    n = len(ps)

    def body(*refs):
        p_refs, r_refs, (send_sems, recv_sems, _) = refs[:n], refs[n:2 * n], refs[2 * n:]
        x, y, c = lax.axis_index("x"), lax.axis_index("y"), lax.axis_index("c")
        remote = []
        for a in range(n):
            for k, (fx, fy) in enumerate(OTHER_CHIPS):
                tx, ty = _flip(x, fx), _flip(y, fy)
                rc = pltpu.make_async_remote_copy(
                    src_ref=_at_slot(p_refs[a], nls[a], 2 * tx + ty), dst_ref=_at_slot(r_refs[a], nls[a], k),
                    send_sem=send_sems.at[a * 3 + k], recv_sem=recv_sems.at[a * 3 + k], device_id=(tx, ty, c),
                    device_id_type=MESH_ID)
                rc.start()
                remote.append(rc)
        for rc in remote:
            rc.wait_recv()
        for rc in remote:
            rc.wait_send()

    outs = [jax.ShapeDtypeStruct(_slotted(v.shape[:nl] + v.shape[nl + 1:], nl, 3), v.dtype) for v, nl in zip(ps, nls)]
    return _comm_call(body, n, outs, 3, ps, name)


def _pair_add(g, got, core, *, name):
    rows, cols = g.shape[-2:]
    lf = math.prod(got.shape[:-3])
    tr = _pick(rows, (1024, 512, 352, 256, 128))

    def body(core_ref, g_ref, got_ref, o_ref):
        o_ref[...] = (g_ref[...].astype(F32) + got_ref[...].astype(F32)).astype(BF16)

    blk = pl.BlockSpec((None, None, tr, cols), lambda i, q, j, core_ref: (i, q, j, 0))
    out = pl.pallas_call(
        body, grid_spec=pltpu.PrefetchScalarGridSpec(
            num_scalar_prefetch=1, grid=(lf, 4, rows // tr),
            in_specs=[pl.BlockSpec((None, None, None, tr, cols), lambda i, q, j, core_ref: (i, q, core_ref[0], j, 0)),
                      blk], out_specs=blk),
        out_shape=jax.ShapeDtypeStruct((lf, 4, rows, cols), BF16),
        compiler_params=_cp(("parallel", "parallel", "parallel")), name=name)(
            core, g.reshape(lf, 4, 2, rows, cols), got.reshape(lf, 4, rows, cols))
    return out.reshape(got.shape)


def _sum_adamw(p, r, own, w, m, v, layer, prev, *, name):
    shape = w.shape[1:]
    rows, cols = shape[-2:]
    lf = math.prod(shape[:-2])
    np_, nk = p.shape[-3], r.shape[-3]
    fits = [c for c in (1024, 512, 352, 256, 128, 64, 32, 16) if c * cols * (7 * 4 + (nk + 1) * 2) * 2 <= VMEM_LIMIT // 2]
    tr = _pick(rows, fits)
    c1 = 1.0 / (1.0 - ADAM_B1 ** ADAM_STEP)
    c2 = 1.0 / (1.0 - ADAM_B2 ** ADAM_STEP)

    def body(own_ref, p_ref, r_ref, w_ref, m_ref, v_ref, *rest):
        g_ref, d_ref, nm_ref, nv_ref = rest[-4:]
        g = p_ref[...].astype(F32)
        for k in range(nk):
            g = g + r_ref[k].astype(F32)
        g_ref[...] = g
        nm = ADAM_B1 * m_ref[...] + (1.0 - ADAM_B1) * g
        nv = ADAM_B2 * v_ref[...] + (1.0 - ADAM_B2) * (g * g)
        nm_ref[...] = nm
        nv_ref[...] = nv
        d_ref[...] = -ADAM_LR * ((nm * c1) / (jnp.sqrt(nv * c2) + ADAM_EPS) + ADAM_WD * w_ref[...])

    wblk = pl.BlockSpec((None, None, tr, cols), lambda i, j, own_ref: (layer, i, j, 0))
    full = (w.shape[0], lf, rows, cols)
    sds = jax.ShapeDtypeStruct(full, F32)
    prev = [] if prev is None else [a.reshape(full) for a in prev]
    outs = pl.pallas_call(
        body, grid_spec=pltpu.PrefetchScalarGridSpec(
            num_scalar_prefetch=1, grid=(lf, rows // tr),
            in_specs=[pl.BlockSpec((None, None, tr, cols), lambda i, j, own_ref: (i, own_ref[0], j, 0)),
                      pl.BlockSpec((None, nk, tr, cols), lambda i, j, own_ref: (i, 0, j, 0))] + [wblk] * 3
            + [HBM_SPEC] * len(prev),
            out_specs=[wblk] * 4),
        out_shape=[sds] * 4, input_output_aliases={6 + i: i for i in range(len(prev))},
        compiler_params=_cp(("parallel", "parallel")), name=name)(
            own, p.reshape(lf, np_, rows, cols), r.reshape(lf, nk, rows, cols), w.reshape(full), m.reshape(full),
            v.reshape(full), *prev)
    return [o.reshape(w.shape) for o in outs]


def _sum_slots(x, *, name):
    nd, rows, cols = x.shape
    tr = _pick(rows, (512, 256, 128, 64, 32, 16, 8))

    def body(x_ref, o_ref):
        acc = x_ref[0].astype(F32)
        for j in range(1, nd):
            acc = acc + x_ref[j].astype(F32)
        o_ref[...] = acc

    return pl.pallas_call(
        body, grid=(rows // tr,), in_specs=[pl.BlockSpec((nd, tr, cols), lambda i: (0, i, 0))],
        out_specs=pl.BlockSpec((tr, cols), lambda i: (i, 0)), out_shape=jax.ShapeDtypeStruct((rows, cols), F32),
        compiler_params=_cp(("parallel",)), name=name)(x)


def _pad_rows(a, mult=8):
    r = (-a.shape[0]) % mult
    return jnp.pad(a, ((0, r), (0, 0))) if r else a


def _flat128(a):
    f = a.reshape(-1)
    return jnp.pad(f, (0, (-f.shape[0]) % LANE)).reshape(-1, LANE)


def _unshard(gathered, shape, axis):
    g = gathered.reshape((N_DEV,) + tuple(shape))
    g = jnp.moveaxis(g, 0, axis)
    full = list(shape)
    full[axis] *= N_DEV
    return g.reshape(full)


def _col_shards(full):
    rows, cols = full.shape
    return jnp.moveaxis(full.reshape(rows, N_DEV, cols // N_DEV), 1, 0)


BIG = (("ffn_w_gate", 2), ("ffn_w_up", 2), ("ffn_w_down", 2), ("w_in", 1), ("w_branch", 2), ("w_out", 1))


def kernel(x, ffn_norm, ffn_w_gate, ffn_w_up, ffn_w_down, mix_norm, w_in, b_gate, pool_w, pool_scale, dn_conv, dn_A_log, dn_dt_bias, dn_out_norm, w_branch, w_out, final_norm, loss_target, m_ffn_norm, m_ffn_w_gate, m_ffn_w_up, m_ffn_w_down, m_mix_norm, m_w_in, m_b_gate, m_pool_w, m_pool_scale, m_dn_conv, m_dn_A_log, m_dn_dt_bias, m_dn_out_norm, m_w_branch, m_w_out, m_final_norm, v_ffn_norm, v_ffn_w_gate, v_ffn_w_up, v_ffn_w_down, v_mix_norm, v_w_in, v_b_gate, v_pool_w, v_pool_scale, v_dn_conv, v_dn_A_log, v_dn_dt_bias, v_dn_out_norm, v_w_branch, v_w_out, v_final_norm):
    wts = dict(ffn_norm=ffn_norm, ffn_w_gate=ffn_w_gate, ffn_w_up=ffn_w_up, ffn_w_down=ffn_w_down, mix_norm=mix_norm,
               w_in=w_in, b_gate=b_gate, pool_w=pool_w, pool_scale=pool_scale, dn_conv=dn_conv, dn_A_log=dn_A_log,
               dn_dt_bias=dn_dt_bias, dn_out_norm=dn_out_norm, w_branch=w_branch, w_out=w_out, final_norm=final_norm)
    mom = dict(ffn_norm=m_ffn_norm, ffn_w_gate=m_ffn_w_gate, ffn_w_up=m_ffn_w_up, ffn_w_down=m_ffn_w_down,
               mix_norm=m_mix_norm, w_in=m_w_in, b_gate=m_b_gate, pool_w=m_pool_w, pool_scale=m_pool_scale,
               dn_conv=m_dn_conv, dn_A_log=m_dn_A_log, dn_dt_bias=m_dn_dt_bias, dn_out_norm=m_dn_out_norm,
               w_branch=m_w_branch, w_out=m_w_out, final_norm=m_final_norm)
    var = dict(ffn_norm=v_ffn_norm, ffn_w_gate=v_ffn_w_gate, ffn_w_up=v_ffn_w_up, ffn_w_down=v_ffn_w_down,
               mix_norm=v_mix_norm, w_in=v_w_in, b_gate=v_b_gate, pool_w=v_pool_w, pool_scale=v_pool_scale,
               dn_conv=v_dn_conv, dn_A_log=v_dn_A_log, dn_dt_bias=v_dn_dt_bias, dn_out_norm=v_dn_out_norm,
               w_branch=v_w_branch, w_out=v_w_out, final_norm=v_final_norm)
    nb, s, d = x.shape
    t = nb * s
    me = 4 * lax.axis_index("x") + 2 * lax.axis_index("y") + lax.axis_index("c")

    big = [n for n, _ in BIG]
    nls = [nl - 1 for _, nl in BIG]
    shards = lambda l: [wts[n][l].astype(BF16) for n in big]
    small_sh = jnp.concatenate([_flat128(ffn_norm), _flat128(dn_conv)], axis=0)
    *gat0, small_g = _gather(shards(0) + [small_sh], nls + [0], name="gather_weights")
    full = [dict(zip(big, gat0))] + [None] * (DEPTH - 1)

    def mixer_weights(l):
        w_in_full = jnp.moveaxis(full[l]["w_in"], 0, 1).reshape(d, -1)
        w_main = jnp.concatenate([w_in_full[:, :AB_LO], w_in_full[:, AB_HI:]], axis=1)
        w_ab = jnp.pad(w_in_full[:, AB_LO:AB_HI], ((0, 0), (0, LANE - (AB_HI - AB_LO))))
        wb = jnp.moveaxis(full[l]["w_branch"], 1, 2).reshape(3, BW, d)
        return w_main, w_ab, wb, full[l]["w_out"].reshape(d, d)

    nfr = ffn_norm.size // LANE
    ffn_norm_full = _unshard(small_g[:, :nfr], ffn_norm.shape, 2)
    dn_conv_full = _unshard(small_g[:, nfr:], dn_conv.shape, 2)
    pool_w_h = pool_w.astype(BF16)

    xs = x.reshape(t, d)
    saved = []
    for l in range(DEPTH):
        sv = dict(x0=xs)
        xs = _ffn_fwd(xs, ffn_norm_full[l, 0], full[l]["ffn_w_gate"][0], full[l]["ffn_w_up"][0],
                      full[l]["ffn_w_down"][0], name="ffn_fwd")
        sv["x1"] = xs
        w_main, w_ab, wb, wo = mixer_weights(l)
        h = _rms_fwd(xs, mix_norm[l], name="mix_rms")
        proj = _mm(h, w_main, name="proj")
        ab = _mm(h, w_ab, name="proj_ab")
        par = jnp.pad(jnp.stack([dn_A_log[l], dn_dt_bias[l]]), ((0, 6), (0, LANE - NH)))
        gain = dn_out_norm[l].reshape(1, HD)
        psc = pool_scale[l].reshape(1, BW)
        yp = _pool_fwd(proj, pool_w_h[l], psc, nb, s, name="pool_fwd")
        yd, o_pre, states, dn_local, gat = _dn_fwd(proj, ab, dn_conv_full[l], par, gain, nb, s,
                                         name="dn_fwd" if l == DEPTH - 1 else "dn_fwd_gather",
                                         gather=(shards(l + 1), nls) if l < DEPTH - 1 else None)
        if l < DEPTH - 1:
            full[l + 1] = dict(zip(big, gat))
        ys, sb_ctr = _sb_fwd(proj, nb, s, name="sb_fwd")
        bg = b_gate[l].reshape(1, 3 * d)
        xs = _merge_fwd(xs, proj, yp, yd, ys, bg, wb, wo, name="merge_fwd")
        sv.update(x2=xs, h=h, proj=proj, ab=ab, par=par, gain=gain, psc=psc, yp=yp, yd=yd, ys=ys, sb_ctr=sb_ctr, o_pre=o_pre,
                  states=states, dn_local=dn_local, bg=bg, w_main=w_main, w_ab=w_ab, wb=wb, wo=wo)
        xs = _ffn_fwd(xs, ffn_norm_full[l, 1], full[l]["ffn_w_gate"][1], full[l]["ffn_w_up"][1],
                      full[l]["ffn_w_down"][1], name="ffn_fwd")
        saved.append(sv)

    dx, g_final, loss_row = _loss_head(xs, final_norm, loss_target.reshape(t, d), name="loss_head")
    loss = lax.psum(loss_row[0, 0], ("x", "y", "c"))

    gw = {n: [None] * DEPTH for n in ("ffn_norm", "ffn_w_gate", "ffn_w_up", "ffn_w_down", "mix_norm", "w_in", "b_gate",
                                      "pool_w", "pool_scale", "dn_conv", "dn_A_log", "dn_dt_bias", "dn_out_norm",
                                      "w_branch", "w_out")}

    me_i = me.astype(jnp.int32).reshape(1)
    updated = {n: None for n in big}
    pending = None

    def finish_layer(l, own_blocks, arrived, own_slot):
        for n, p, r in zip(big, own_blocks, arrived):
            updated[n] = _sum_adamw(p, r, own_slot, wts[n], mom[n], var[n], l, updated[n], name=f"adamw_{n}_{l}")

    def ffn_back(l, i, x_in, dy):
        dxi, dg, hb, dyh, da, db, sact = _ffn_bwd(x_in, ffn_norm_full[l, i], full[l]["ffn_w_gate"][i],
                                                  full[l]["ffn_w_up"][i], full[l]["ffn_w_down"][i], dy, name="ffn_bwd")
        return dxi, dg, (_mm_slots(hb, da, name="dw_gate_up"), _mm_slots(hb, db, name="dw_gate_up"),
                         _mm_slots(sact, dyh, name="dw_down"))

    for l in reversed(range(DEPTH)):
        sv = saved[l]
        dx, dg1, (dwg1, dwu1, dwd1) = ffn_back(l, 1, sv["x2"], dx)
        dyp, dyd, dys, dgl, merged, dxh, dbd, dbg = _merge_bwd(sv["proj"], sv["yp"], sv["yd"], sv["ys"], sv["bg"],
                                                               sv["wb"], sv["wo"], dx, name="merge_bwd")
        gw["w_out"][l] = _mm(merged, dxh, ta=True, out_dtype=BF16, name="dw_out").reshape(N_DEV, d // N_DEV, d)
        gw["w_branch"][l] = jnp.stack([_col_shards(_mm(y, dbd[n], ta=True, out_dtype=BF16, name="dw_branch"))
                                       for n, y in enumerate((sv["yp"], sv["yd"], sv["ys"]))])
        gw["b_gate"][l] = dbg.reshape(3 * d)
        du, dpw, dps = _pool_bwd(sv["proj"], pool_w_h[l], sv["psc"], dyp, nb, s, name="pool_bwd")
        gw["pool_w"][l], gw["pool_scale"][l] = dpw, dps.reshape(BW)
        dqr, dkr, dvr, dz, dab4, dcq, dck, dcv, dpar, dgain, arrived = _dn_bwd(
            sv["proj"], sv["ab"], dn_conv_full[l], sv["par"], sv["gain"], sv["o_pre"], sv["states"], sv["dn_local"],
            dyd, nb, s, name="dn_bwd_scatter" if pending is not None else "dn_bwd",
            scatter=([gw[n][pending] for n in big], nls) if pending is not None else None)
        if pending is not None:
            finish_layer(pending, [gw[n][pending] for n in big], arrived, me_i)
        gw["dn_conv"][l] = jnp.concatenate([dcq, dck, dcv], axis=1)
        gw["dn_A_log"][l], gw["dn_dt_bias"][l], gw["dn_out_norm"][l] = dpar[:, 0, 0], dpar[:, 1, 0], dgain.reshape(HD)
        dsq, dsk, dsv = _sb_bwd(sv["proj"], sv["sb_ctr"], dys, nb, s, name="sb_bwd")
        dab = _sum_heads(dab4, name="sum_heads")
        dproj = jnp.concatenate([du.astype(BF16), dqr.astype(BF16), dkr.astype(BF16), dvr.astype(BF16),
                                 dz.astype(BF16), dsq.astype(BF16), dsk.astype(BF16), dsv.astype(BF16), dgl], axis=1)
        dw_main = _mm(sv["h"], dproj, ta=True, out_dtype=BF16, name="dw_in")
        dw_ab = _mm(sv["h"], dab, ta=True, out_dtype=BF16, name="dw_ab")
        gw["w_in"][l] = _col_shards(jnp.concatenate([dw_main[:, :AB_LO], dw_ab[:, :AB_HI - AB_LO],
                                                     dw_main[:, AB_LO:]], axis=1))
        dh_main = _mm(dproj, sv["w_main"], tb=True, name="dh_mix")
        dh_ab = _mm(dab, sv["w_ab"], tb=True, name="dh_mix_ab")
        dx, dgm = _rms_bwd(sv["x1"], mix_norm[l], dh_main, dh_ab, dx, name="mix_rms_bwd")
        gw["mix_norm"][l] = dgm.reshape(d)
        dx, dg0, (dwg0, dwu0, dwd0) = ffn_back(l, 0, sv["x0"], dx)
        gw["ffn_norm"][l] = jnp.stack([dg0.reshape(d), dg1.reshape(d)])
        gw["ffn_w_gate"][l] = jnp.stack([dwg0, dwg1])
        gw["ffn_w_up"][l] = jnp.stack([dwu0, dwu1])
        gw["ffn_w_down"][l] = jnp.stack([dwd0, dwd1])
        pending = l
    grad_x = dx.reshape(nb, s, d)

    core = lax.axis_index("c").astype(jnp.int32).reshape(1)
    chip = (2 * lax.axis_index("x") + lax.axis_index("y")).astype(jnp.int32).reshape(1)
    last = [gw[n][0] for n in big]
    got = _scatter_pair(last, nls, name="scatter_grads_pair")
    chip_sums = [_pair_add(g, b, core, name="add_pair_" + n) for n, g, b in zip(big, last, got)]
    finish_layer(0, chip_sums, _scatter_chips(chip_sums, nls, name="scatter_grads_chips"), chip)
    grads, delta, new_m, new_v = ({n: updated[n][i] for n in big} for i in range(4))
    gw = {n: jnp.stack(v) for n, v in gw.items() if n not in big}
    gw["final_norm"] = g_final.reshape(d)

    small = ("ffn_norm", "mix_norm", "b_gate", "pool_w", "pool_scale", "dn_conv", "dn_A_log", "dn_dt_bias",
             "dn_out_norm", "final_norm")
    sp = _pad_rows(jnp.concatenate([_flat128(gw[n]) for n in small], axis=0))
    ssum = _sum_slots(_gather([sp], [0], name="gather_small_grads")[0], name="sum_small_grads")
    off = 0
    for n in small:
        r = -(-gw[n].size // LANE)
        g = ssum[off:off + r].reshape(-1)[:gw[n].size].reshape(gw[n].shape)
        off += r
        if n in ("ffn_norm", "dn_conv"):
            w = wts[n].shape[2]
            g = lax.dynamic_slice_in_dim(g, me * w, w, axis=2)
        grads[n] = g

    pk = lambda src: _pad_rows(jnp.concatenate([_flat128(src[n]) for n in small], axis=0))
    dl, nm, nv = _adamw(pk(wts), pk(grads), pk(mom), pk(var), name="adamw_small")
    off = 0
    for n in small:
        r = -(-wts[n].size // LANE)
        for dst, src in ((delta, dl), (new_m, nm), (new_v, nv)):
            dst[n] = src[off:off + r].reshape(-1)[:wts[n].size].reshape(wts[n].shape)
        off += r

    order = ("ffn_norm", "ffn_w_gate", "ffn_w_up", "ffn_w_down", "mix_norm", "w_in", "b_gate", "pool_w", "pool_scale",
             "dn_conv", "dn_A_log", "dn_dt_bias", "dn_out_norm", "w_branch", "w_out", "final_norm")
    return (loss, grad_x, *[grads[n] for n in order], *[delta[n] for n in order], *[new_m[n] for n in order],
            *[new_v[n] for n in order])
```

```python
import functools
import math

import jax
import jax.numpy as jnp
from jax import lax
from jax.experimental import pallas as pl
from jax.experimental.pallas import tpu as pltpu

F32, BF16 = jnp.float32, jnp.bfloat16
D_MODEL, D_FF, DEPTH = 1024, 2816, 4
BW = 512
HD = 128
NH = 4
DN_CHUNK = 64
EPS = 1e-6
N_DEV = 8
LANE = 128
CB_POOL, CB_DNQ, CB_DNK, CB_DNV, CB_DNZ, CB_SBQ, CB_SBK, CB_SBV = 0, 4, 8, 12, 16, 20, 24, 28
CB_GATE = 4
P_MAIN = 7168
AB_LO, AB_HI = 2560, 2568
ADAM_LR, ADAM_B1, ADAM_B2, ADAM_EPS, ADAM_WD, ADAM_STEP = 0.001, 0.9, 0.999, 1e-08, 0.01, 10
VMEM_LIMIT = 56 * 1024 * 1024
HIGHEST = lax.Precision.HIGHEST
NT_DIMS = (((1,), (1,)), ((), ()))
TN_DIMS = (((0,), (0,)), ((), ()))
NN_DIMS = (((1,), (0,)), ((), ()))


def _cp(dims=None, vmem=VMEM_LIMIT):
    return pltpu.CompilerParams(dimension_semantics=dims, vmem_limit_bytes=vmem)


def _pick(n, cands):
    for c in cands:
        if n % c == 0:
            return c
    return n


def _bdot(a, b, dims=NN_DIMS):
    return lax.dot_general(a.astype(BF16), b.astype(BF16), dims, preferred_element_type=F32)


def _hdot(a, b, dims=NN_DIMS):
    return lax.dot_general(a, b, dims, precision=lax.Precision.HIGH, preferred_element_type=F32)


def _split_dot(x, m01):
    hi = x.astype(BF16)
    lo = (x - hi.astype(F32)).astype(BF16)
    return (lax.dot_general(hi, m01, NN_DIMS, preferred_element_type=F32)
            + lax.dot_general(lo, m01, NN_DIMS, preferred_element_type=F32))


def _sigmoid(x):
    return 1.0 / (1.0 + jnp.exp(-x))


def _log_sigmoid(x):
    return jnp.minimum(x, 0.0) - jnp.log1p(jnp.exp(-jnp.abs(x)))


def _softplus(x):
    return jnp.maximum(x, 0.0) + jnp.log1p(jnp.exp(-jnp.abs(x)))


def _shift_down(x, k):
    r = lax.broadcasted_iota(jnp.int32, x.shape, 0)
    return jnp.where(r >= k, pltpu.roll(x, k, 0), 0.0)


def _shift_up(x, k):
    n = x.shape[0]
    r = lax.broadcasted_iota(jnp.int32, x.shape, 0)
    return jnp.where(r < n - k, pltpu.roll(x, n - k, 0), 0.0)


def _mm(a, b, *, ta=False, tb=False, out_dtype=F32, name):
    (kk, m) = a.shape if ta else a.shape[::-1]
    (k2, n) = b.shape[::-1] if tb else b.shape
    assert kk == k2, (a.shape, b.shape, ta, tb)
    bm = _pick(m, (1024, 512, 256, 128))
    bn = _pick(n, (1024, 1408, 512, 256, 128))
    bk = _pick(kk, (512, 256, 128))
    nk = kk // bk
    dims = (((0 if ta else 1,), (1 if tb else 0,)), ((), ()))

    def body(a_ref, b_ref, o_ref, acc_ref):
        k = pl.program_id(2)

        @pl.when(k == 0)
        def _():
            acc_ref[...] = jnp.zeros_like(acc_ref)

        acc_ref[...] += lax.dot_general(a_ref[...].astype(BF16), b_ref[...].astype(BF16), dims,
                                        preferred_element_type=F32)

        @pl.when(k == nk - 1)
        def _():
            o_ref[...] = acc_ref[...].astype(out_dtype)

    a_spec = (pl.BlockSpec((bk, bm), lambda i, j, k: (k, i)) if ta else pl.BlockSpec((bm, bk), lambda i, j, k: (i, k)))
    b_spec = (pl.BlockSpec((bn, bk), lambda i, j, k: (j, k)) if tb else pl.BlockSpec((bk, bn), lambda i, j, k: (k, j)))
    return pl.pallas_call(
        body, grid=(m // bm, n // bn, nk), in_specs=[a_spec, b_spec],
        out_specs=pl.BlockSpec((bm, bn), lambda i, j, k: (i, j)),
        out_shape=jax.ShapeDtypeStruct((m, n), out_dtype),
        scratch_shapes=[pltpu.VMEM((bm, bn), F32)],
        compiler_params=_cp(("parallel", "parallel", "arbitrary")), name=name)(a, b)


def _mm_slots(a, b, *, name):
    a3, b3 = a.ndim == 3, b.ndim == 3
    ns = a.shape[0] if a3 else b.shape[0]
    m, t = a.shape[-2:]
    n = b.shape[-1]
    bm, bn, bk = _pick(m, (1024, 512, 256, 128)), _pick(n, (1024, 512, 256, 128)), _pick(t, (1024, 512, 256, 128))
    nk = t // bk

    def body(a_ref, b_ref, o_ref, acc_ref):
        k = pl.program_id(3)

        @pl.when(k == 0)
        def _():
            acc_ref[...] = jnp.zeros_like(acc_ref)

        acc_ref[...] += _bdot(a_ref[...], b_ref[...])

        @pl.when(k == nk - 1)
        def _():
            o_ref[...] = acc_ref[...].astype(BF16)

    a_spec = (pl.BlockSpec((None, bm, bk), lambda s, i, j, k: (s, i, k)) if a3
              else pl.BlockSpec((bm, bk), lambda s, i, j, k: (i, k)))
    b_spec = (pl.BlockSpec((None, bk, bn), lambda s, i, j, k: (s, k, j)) if b3
              else pl.BlockSpec((bk, bn), lambda s, i, j, k: (k, j)))
    return pl.pallas_call(
        body, grid=(ns, m // bm, n // bn, nk), in_specs=[a_spec, b_spec],
        out_specs=pl.BlockSpec((None, bm, bn), lambda s, i, j, k: (s, i, j)),
        out_shape=jax.ShapeDtypeStruct((ns, m, n), BF16), scratch_shapes=[pltpu.VMEM((bm, bn), F32)],
        compiler_params=_cp(("parallel", "parallel", "parallel", "arbitrary")), name=name)(a, b)


def _rms_stats(x):
    rstd = lax.rsqrt(jnp.mean(x * x, axis=-1, keepdims=True) + EPS)
    return x * rstd, rstd


def _rms_bwd_vals(dh, xhat, rstd, g):
    dxh = dh * g
    dx = rstd * (dxh - xhat * jnp.mean(dxh * xhat, axis=-1, keepdims=True))
    return dx, jnp.sum(dh * xhat, axis=0, keepdims=True)


def _rms_fwd(x, g, *, name):
    t, d = x.shape
    tm = _pick(t, (512, 256, 128))

    def body(x_ref, g_ref, h_ref):
        xhat, _ = _rms_stats(x_ref[...])
        h_ref[...] = (xhat * g_ref[...]).astype(BF16)

    return pl.pallas_call(
        body, grid=(t // tm,),
        in_specs=[pl.BlockSpec((tm, d), lambda i: (i, 0)), pl.BlockSpec((1, d), lambda i: (0, 0))],
        out_specs=pl.BlockSpec((tm, d), lambda i: (i, 0)), out_shape=jax.ShapeDtypeStruct((t, d), BF16),
        compiler_params=_cp(("parallel",)), name=name)(x, g.reshape(1, d))


def _rms_bwd(x, g, dh_a, dh_b, dres, *, name):
    t, d = x.shape
    tm = _pick(t, (512, 256, 128))

    def body(x_ref, g_ref, dha_ref, dhb_ref, dres_ref, dx_ref, dg_ref):
        xhat, rstd = _rms_stats(x_ref[...])
        dx, dg = _rms_bwd_vals(dha_ref[...] + dhb_ref[...], xhat, rstd, g_ref[...])
        dx_ref[...] = dres_ref[...] + dx

        @pl.when(pl.program_id(0) == 0)
        def _():
            dg_ref[...] = jnp.zeros_like(dg_ref)

        dg_ref[...] += dg

    row = pl.BlockSpec((tm, d), lambda i: (i, 0))
    vec = pl.BlockSpec((1, d), lambda i: (0, 0))
    return pl.pallas_call(
        body, grid=(t // tm,), in_specs=[row, vec, row, row, row], out_specs=[row, vec],
        out_shape=[jax.ShapeDtypeStruct((t, d), F32), jax.ShapeDtypeStruct((1, d), F32)],
        compiler_params=_cp(("arbitrary",)), name=name)(x, g.reshape(1, d), dh_a, dh_b, dres)


FFN_TM = 512


def _ffn_fwd(x, g, wg, wu, wd, *, name):
    t, d = x.shape
    nf, _, fc = wg.shape
    tm = _pick(t, (FFN_TM, 256, 128))

    def body(x_ref, g_ref, wg_ref, wu_ref, wd_ref, o_ref, a_ref, b_ref, h_ref, acc_ref):
        j = pl.program_id(1)

        @pl.when(j == 0)
        def _():
            xhat, _ = _rms_stats(x_ref[...])
            h_ref[...] = (xhat * g_ref[...]).astype(BF16)
            acc_ref[...] = jnp.zeros_like(acc_ref)

        h = h_ref[...]
        a = _bdot(h, wg_ref[...])
        b = _bdot(h, wu_ref[...])
        a_ref[...] = a.astype(BF16)
        b_ref[...] = b.astype(BF16)
        s = a * _sigmoid(a) * b
        acc_ref[...] += _bdot(s, wd_ref[...])

        @pl.when(j == nf - 1)
        def _():
            o_ref[...] = x_ref[...] + 0.5 * acc_ref[...]

    row = pl.BlockSpec((tm, d), lambda i, j: (i, 0))
    return pl.pallas_call(
        body, grid=(t // tm, nf),
        in_specs=[row, pl.BlockSpec((1, d), lambda i, j: (0, 0)),
                  pl.BlockSpec((None, d, fc), lambda i, j: (j, 0, 0)), pl.BlockSpec((None, d, fc), lambda i, j: (j, 0, 0)),
                  pl.BlockSpec((None, fc, d), lambda i, j: (j, 0, 0))],
        out_specs=[row, pl.BlockSpec((None, tm, fc), lambda i, j: (j, i, 0)),
                   pl.BlockSpec((None, tm, fc), lambda i, j: (j, i, 0))],
        out_shape=[jax.ShapeDtypeStruct((t, d), F32), jax.ShapeDtypeStruct((nf, t, fc), BF16),
                   jax.ShapeDtypeStruct((nf, t, fc), BF16)],
        scratch_shapes=[pltpu.VMEM((tm, d), BF16), pltpu.VMEM((tm, d), F32)],
        compiler_params=_cp(("parallel", "arbitrary")), name=name)(x, g.reshape(1, d), wg, wu, wd)


def _ffn_bwd(x, g, wg, wu, wd, a_pre, b_pre, dy, *, name):
    t, d = x.shape
    nf, _, fc = wg.shape
    tm = _pick(t, (FFN_TM, 256, 128))

    def body(x_ref, g_ref, wg_ref, wu_ref, wd_ref, a_ref, b_ref, dy_ref,
             dx_ref, dg_ref, ht_ref, dyh_ref, da_ref, db_ref, st_ref, acc_ref):
        i, j = pl.program_id(0), pl.program_id(1)

        @pl.when(j == 0)
        def _():
            xhat, _ = _rms_stats(x_ref[...])
            ht_ref[...] = (xhat * g_ref[...]).T.astype(BF16)
            dyh_ref[...] = (0.5 * dy_ref[...]).astype(BF16)
            acc_ref[...] = jnp.zeros_like(acc_ref)

        a = a_ref[...].astype(F32)
        b = b_ref[...].astype(F32)
        sg = _sigmoid(a)
        silu = a * sg
        st_ref[...] = (silu * b).T.astype(BF16)
        ds = _bdot(dyh_ref[...], wd_ref[...], NT_DIMS)
        da = (ds * b * (sg * (1.0 + a * (1.0 - sg)))).astype(BF16)
        db = (ds * silu).astype(BF16)
        da_ref[...] = da
        db_ref[...] = db
        acc_ref[...] += _bdot(da, wg_ref[...], NT_DIMS) + _bdot(db, wu_ref[...], NT_DIMS)

        @pl.when((i == 0) & (j == 0))
        def _():
            dg_ref[...] = jnp.zeros_like(dg_ref)

        @pl.when(j == nf - 1)
        def _():
            xhat, rstd = _rms_stats(x_ref[...])
            dx, dg = _rms_bwd_vals(acc_ref[...], xhat, rstd, g_ref[...])
            dx_ref[...] = dy_ref[...] + dx
            dg_ref[...] += dg

    row = pl.BlockSpec((tm, d), lambda i, j: (i, 0))
    vec = pl.BlockSpec((1, d), lambda i, j: (0, 0))
    fblk = pl.BlockSpec((None, tm, fc), lambda i, j: (j, i, 0))
    return pl.pallas_call(
        body, grid=(t // tm, nf),
        in_specs=[row, vec, pl.BlockSpec((None, d, fc), lambda i, j: (j, 0, 0)),
                  pl.BlockSpec((None, d, fc), lambda i, j: (j, 0, 0)), pl.BlockSpec((None, fc, d), lambda i, j: (j, 0, 0)),
                  fblk, fblk, row],
        out_specs=[row, vec, pl.BlockSpec((d, tm), lambda i, j: (0, i)), row, fblk, fblk,
                   pl.BlockSpec((None, fc, tm), lambda i, j: (j, 0, i))],
        out_shape=[jax.ShapeDtypeStruct((t, d), F32), jax.ShapeDtypeStruct((1, d), F32),
                   jax.ShapeDtypeStruct((d, t), BF16), jax.ShapeDtypeStruct((t, d), BF16),
                   jax.ShapeDtypeStruct((nf, t, fc), BF16), jax.ShapeDtypeStruct((nf, t, fc), BF16),
                   jax.ShapeDtypeStruct((nf, fc, t), BF16)],
        scratch_shapes=[pltpu.VMEM((tm, d), F32)],
        compiler_params=_cp(("arbitrary", "arbitrary")), name=name)(x, g.reshape(1, d), wg, wu, wd, a_pre, b_pre, dy)


def _pool_core(u, grp):
    s = u.shape[0]
    w2 = u + _shift_down(u, 1)
    w4 = w2 + _shift_down(w2, 2)
    w8 = w4 + _shift_down(w4, 4)
    w16 = w8 + _shift_down(w8, 8)
    wsum = jnp.where(grp == 0, w2, jnp.where(grp == 1, w4, jnp.where(grp == 2, w8, w16)))
    win = jnp.left_shift(2, grp).astype(F32)
    t1 = (lax.broadcasted_iota(jnp.int32, (s, 1), 0) + 1).astype(F32)
    inv = 1.0 / jnp.minimum(t1, win)
    return wsum * inv - u, inv


def _pool_fwd(proj, pool_w, pool_scale, nb, s, *, name):
    def body(u_ref, w_ref, sc_ref, y_ref):
        pooled, _ = _pool_core(u_ref[...], pl.program_id(0))
        y_ref[...] = _bdot(pooled, w_ref[...]) * sc_ref[...]

    return pl.pallas_call(
        body, grid=(NH, nb),
        in_specs=[pl.BlockSpec((s, HD), lambda g, b: (b, CB_POOL + g)),
                  pl.BlockSpec((None, HD, HD), lambda g, b: (g, 0, 0)), pl.BlockSpec((1, HD), lambda g, b: (0, g))],
        out_specs=pl.BlockSpec((s, HD), lambda g, b: (b, g)),
        out_shape=jax.ShapeDtypeStruct((nb * s, BW), F32),
        compiler_params=_cp(("parallel", "parallel")), name=name)(proj, pool_w, pool_scale)


def _pool_bwd(proj, pool_w, pool_scale, dy, nb, s, *, name):
    def body(u_ref, w_ref, sc_ref, dy_ref, du_ref, dw_ref, dsc_ref):
        grp, b = pl.program_id(0), pl.program_id(1)
        pooled, inv = _pool_core(u_ref[...], grp)
        mixed = _bdot(pooled, w_ref[...])
        dy = dy_ref[...]
        dmixed = dy * sc_ref[...]
        dpooled = _bdot(dmixed, w_ref[...], NT_DIMS)
        r = dpooled * inv
        v2 = r + _shift_up(r, 1)
        v4 = v2 + _shift_up(v2, 2)
        v8 = v4 + _shift_up(v4, 4)
        v16 = v8 + _shift_up(v8, 8)
        vsum = jnp.where(grp == 0, v2, jnp.where(grp == 1, v4, jnp.where(grp == 2, v8, v16)))
        du_ref[...] = vsum - dpooled

        @pl.when(b == 0)
        def _():
            dw_ref[...] = jnp.zeros_like(dw_ref)
            dsc_ref[...] = jnp.zeros_like(dsc_ref)

        dw_ref[...] += _bdot(pooled, dmixed, TN_DIMS)
        dsc_ref[...] += jnp.sum(dy * mixed, axis=0, keepdims=True)

    return pl.pallas_call(
        body, grid=(NH, nb),
        in_specs=[pl.BlockSpec((s, HD), lambda g, b: (b, CB_POOL + g)),
                  pl.BlockSpec((None, HD, HD), lambda g, b: (g, 0, 0)), pl.BlockSpec((1, HD), lambda g, b: (0, g)),
                  pl.BlockSpec((s, HD), lambda g, b: (b, g))],
        out_specs=[pl.BlockSpec((s, HD), lambda g, b: (b, g)), pl.BlockSpec((None, HD, HD), lambda g, b: (g, 0, 0)),
                   pl.BlockSpec((1, HD), lambda g, b: (0, g))],
        out_shape=[jax.ShapeDtypeStruct((nb * s, BW), F32), jax.ShapeDtypeStruct((NH, HD, HD), F32),
                   jax.ShapeDtypeStruct((1, BW), F32)],
        compiler_params=_cp(("arbitrary", "arbitrary")), name=name)(proj, pool_w, pool_scale, dy)


SB_BLK = 128


SB_G = 4
SB_KG = SB_G * SB_BLK
SB_Q = 2 * SB_BLK


def _sb_block(qb, kg, q0, k0, diagonal):
    z = _bdot(qb, kg, NT_DIMS) * (HD ** -0.5)
    lsz = _log_sigmoid(z)
    if not diagonal:
        return lsz, lsz - z, None
    row = lax.broadcasted_iota(jnp.int32, z.shape, 0) + q0
    col = lax.broadcasted_iota(jnp.int32, z.shape, 1) + k0
    causal = col < row
    return lsz, jnp.where(causal, lsz - z, 0.0), causal


def _keep(causal, x):
    return x if causal is None else jnp.where(causal, x, 0.0)


def _sub(x, m):
    return x[:, m * SB_BLK:(m + 1) * SB_BLK]


def _sb_tails(lnm, after, ct):
    hi = lnm.astype(BF16)
    lo = (lnm - hi.astype(F32)).astype(BF16)
    tails = [None] * SB_G
    for m in reversed(range(SB_G)):
        tails[m] = (lax.dot_general(_sub(hi, m), after, NN_DIMS, preferred_element_type=F32)
                    + lax.dot_general(_sub(lo, m), after, NN_DIMS, preferred_element_type=F32)) + ct
        ct = ct + jnp.sum(_sub(lnm, m), axis=1, keepdims=True)
    ones = jnp.ones((8, lnm.shape[1]), BF16)
    rows = (lax.dot_general(ones, hi, NT_DIMS, preferred_element_type=F32)
            + lax.dot_general(ones, lo, NT_DIMS, preferred_element_type=F32))
    return jnp.concatenate(tails, axis=1), rows, ct


def _tri01(lower):
    r = lax.broadcasted_iota(jnp.int32, (SB_BLK, SB_BLK), 0)
    c = lax.broadcasted_iota(jnp.int32, (SB_BLK, SB_BLK), 1)
    return jnp.where((r < c) if lower else (r > c), 1.0, 0.0).astype(BF16)


def _split3(x):
    hi = x.astype(BF16)
    mid = (x - hi.astype(F32)).astype(BF16)
    lo = (x - hi.astype(F32) - mid.astype(F32)).astype(BF16)
    return hi, mid, lo


def _rows_to_cols(rows):
    eighth = jnp.full((8, LANE), 0.125, BF16)
    return sum(lax.dot_general(p, eighth, TN_DIMS, preferred_element_type=F32) for p in _split3(rows))


def _sb_fwd(proj, nb, s, *, name):
    nq = s // SB_Q
    ng = s // SB_KG

    def body(q_ref, k_ref, v_ref, o_ref, ctr_ref):
        after = _tri01(False)

        def qblock(i, _):
            q0 = pl.multiple_of(i * SB_Q, SB_Q)
            qb = q_ref[pl.ds(q0, SB_Q), :]

            def kgroup(g, carry, diagonal):
                acc, ct, ctr = carry
                k0 = pl.multiple_of(g * SB_KG, SB_KG)
                lsz, lnm, causal = _sb_block(qb, k_ref[pl.ds(k0, SB_KG), :], q0, k0, diagonal)
                ctr_ref[i * ng + g] = ctr
                tail, rows, ct = _sb_tails(lnm, after, ct)
                w = _keep(causal, jnp.exp(lsz + tail))
                return acc + _bdot(w, v_ref[pl.ds(k0, SB_KG), :]), ct, ctr + rows

            gd = (i * SB_Q) // SB_KG
            carry = kgroup(gd, (jnp.zeros((SB_Q, HD), F32), jnp.zeros((SB_Q, 1), F32), jnp.zeros((8, SB_Q), F32)), True)
            acc, _, _ = lax.fori_loop(0, gd, lambda jj, c: kgroup(gd - 1 - jj, c, False), carry)
            o_ref[pl.ds(q0, SB_Q), :] = acc
            return 0

        lax.fori_loop(0, nq, qblock, 0)

    def col(cb):
        return pl.BlockSpec((s, HD), lambda b, h: (b, cb + h))

    return pl.pallas_call(
        body, grid=(nb, NH), in_specs=[col(CB_SBQ), col(CB_SBK), col(CB_SBV)],
        out_specs=[pl.BlockSpec((s, HD), lambda b, h: (b, h)),
                   pl.BlockSpec((None, None, nq * ng, 8, SB_Q), lambda b, h: (b, h, 0, 0, 0))],
        out_shape=[jax.ShapeDtypeStruct((nb * s, BW), F32), jax.ShapeDtypeStruct((nb, NH, nq * ng, 8, SB_Q), F32)],
        compiler_params=_cp(("parallel", "parallel")), name=name)(proj, proj, proj)


def _sb_bwd(proj, ctr, dy, nb, s, *, name):
    nq = s // SB_Q
    ng = s // SB_KG
    scale = HD ** -0.5

    def body(q_ref, k_ref, v_ref, ctr_ref, do_ref, dq_ref, dk_ref, dv_ref):
        after = _tri01(False)
        before = _tri01(True)
        dk_ref[...] = jnp.zeros_like(dk_ref)
        dv_ref[...] = jnp.zeros_like(dv_ref)

        def qblock(i, _):
            q0 = pl.multiple_of(i * SB_Q, SB_Q)
            qb = q_ref[pl.ds(q0, SB_Q), :]
            dob = do_ref[pl.ds(q0, SB_Q), :]

            def kgroup(g, carry, diagonal):
                dq, ce = carry
                k0 = pl.multiple_of(g * SB_KG, SB_KG)
                kg = k_ref[pl.ds(k0, SB_KG), :]
                vg = v_ref[pl.ds(k0, SB_KG), :]
                lsz, lnm, causal = _sb_block(qb, kg, q0, k0, diagonal)
                tail, _, _ = _sb_tails(lnm, after, _rows_to_cols(ctr_ref[i * ng + g])[:, 0:1])
                w = _keep(causal, jnp.exp(lsz + tail))
                e = _bdot(dob, vg, NT_DIMS) * w
                pres = []
                for m in range(SB_G):
                    pres.append(_split_dot(_sub(e, m), before) + ce)
                    ce = ce + jnp.sum(_sub(e, m), axis=1, keepdims=True)
                sig = jnp.exp(lsz)
                dz = _keep(causal, e * (1.0 - sig) - jnp.concatenate(pres, axis=1) * sig) * scale
                dk_ref[pl.ds(k0, SB_KG), :] += _bdot(dz, qb, TN_DIMS)
                dv_ref[pl.ds(k0, SB_KG), :] += _bdot(w, dob, TN_DIMS)
                return dq + _bdot(dz, kg), ce

            gd = (i * SB_Q) // SB_KG
            carry = lax.fori_loop(0, gd, lambda g, c: kgroup(g, c, False),
                                  (jnp.zeros((SB_Q, HD), F32), jnp.zeros((SB_Q, 1), F32)))
            dq, _ = kgroup(gd, carry, True)
            dq_ref[pl.ds(q0, SB_Q), :] = dq
            return 0

        lax.fori_loop(0, nq, qblock, 0)

    def col(cb):
        return pl.BlockSpec((s, HD), lambda b, h: (b, cb + h))

    out = pl.BlockSpec((s, HD), lambda b, h: (b, h))
    sds = jax.ShapeDtypeStruct((nb * s, BW), F32)
    return pl.pallas_call(
        body, grid=(nb, NH),
        in_specs=[col(CB_SBQ), col(CB_SBK), col(CB_SBV),
                  pl.BlockSpec((None, None, nq * ng, 8, SB_Q), lambda b, h: (b, h, 0, 0, 0)), out],
        out_specs=[out, out, out], out_shape=[sds, sds, sds],
        compiler_params=_cp(("parallel", "parallel")), name=name)(proj, proj, proj, ctr, dy)


def _make_cdot(dims, dims_da, dims_db, swap_a=False, swap_b=False):
    @jax.custom_vjp
    def f(a, b):
        return _bdot(a, b, dims)

    def fwd(a, b):
        return _bdot(a, b, dims), (a, b)

    def bwd(res, g):
        a, b = res
        da = _bdot(b, g, dims_da) if swap_a else _bdot(g, b, dims_da)
        db = _bdot(g, a, dims_db) if swap_b else _bdot(a, g, dims_db)
        return da, db

    f.defvjp(fwd, bwd)
    return f


_cdot = _make_cdot(NN_DIMS, NT_DIMS, TN_DIMS)
_cdot_nt = _make_cdot(NT_DIMS, NN_DIMS, TN_DIMS, swap_b=True)
_cdot_tn = _make_cdot(TN_DIMS, NT_DIMS, NN_DIMS, swap_a=True)


DN_SUPER = 4 * DN_CHUNK


@jax.custom_vjp
def _unit_lower_inverse(lmat):
    n = lmat.shape[0]
    steps = int(math.log2(DN_CHUNK))
    eye = jnp.where(lax.broadcasted_iota(jnp.int32, (n, n), 0) == lax.broadcasted_iota(jnp.int32, (n, n), 1), 1.0, 0.0)
    inv = eye - lmat
    pw = _hdot(lmat, lmat)
    for it in range(steps - 1):
        inv = inv + _hdot(inv, pw)
        if it < steps - 2:
            pw = _hdot(pw, pw)
    return inv


def _unit_lower_inverse_fwd(lmat):
    inv = _unit_lower_inverse(lmat)
    return inv, inv


def _unit_lower_inverse_bwd(inv, g):
    return (-_hdot(_hdot(inv, g, TN_DIMS), inv, NT_DIMS),)


_unit_lower_inverse.defvjp(_unit_lower_inverse_fwd, _unit_lower_inverse_bwd)


def _dn_local(q, k, v, bb, gb):
    n = q.shape[0]
    r = lax.broadcasted_iota(jnp.int32, (n, n), 0)
    cc = lax.broadcasted_iota(jnp.int32, (n, n), 1)
    shift = int(math.log2(DN_CHUNK))
    same = lax.shift_right_logical(r, shift) == lax.shift_right_logical(cc, shift)
    incl = jnp.where(same, jnp.where(r >= cc, 1.0, 0.0), 0.0)
    strict = jnp.where(same, jnp.where(r > cc, 1.0, 0.0), 0.0)
    gc = _hdot(incl, gb)
    gc_row = _hdot(jnp.full((n, HD), 1.0 / HD, F32), gc, NT_DIMS)
    diff = jnp.concatenate([gc] * (n // HD), axis=1) - gc_row
    decay = incl * jnp.exp(diff * incl)
    kb = k * bb
    lmat = _cdot_nt(kb, k) * (strict * decay)
    egc = jnp.exp(gc)
    inv = _unit_lower_inverse(lmat)
    u = _hdot(inv, v * bb)
    w = _hdot(inv, kb * egc)
    attn = _cdot_nt(q, k) * decay
    gl = _hdot(jnp.where(same, 1.0, 0.0), gb)
    return u, w, attn, q * egc, k * jnp.exp(gl - gc), jnp.exp(gl)


def _attn_pairs(attn):
    return jnp.concatenate([attn[:HD, :HD], attn[HD:, HD:]], axis=0)


def _attn_unpairs(a):
    z = jnp.zeros((HD, HD), F32)
    return jnp.concatenate([jnp.concatenate([a[:HD], z], axis=1), jnp.concatenate([z, a[HD:]], axis=1)], axis=0)


def _dn_step(u, w, a, qd, kd, cdrows, state, odd):
    v_new = u - _cdot(w, state)
    z = jnp.zeros_like(v_new)
    o = _cdot(qd, state) + _cdot(a, jnp.concatenate([z, v_new] if odd else [v_new, z], axis=0))
    return o, state * jnp.mean(cdrows, axis=0, keepdims=True) + _cdot_tn(kd, v_new)


def _dn_local_pass(fn, s, ins, outs):
    def step(it, _):
        sl = pl.ds(pl.multiple_of(it * DN_SUPER, DN_SUPER), DN_SUPER)
        res = fn(*[ref[sl, :] for ref in ins])
        for ref, val in zip(outs, res):
            ref[sl, :] = val
        return 0

    lax.fori_loop(0, s // DN_SUPER, step, 0)


def _lane_pick(row, idx):
    lane = lax.broadcasted_iota(jnp.int32, row.shape, 1)
    return jnp.sum(jnp.where(lane == idx, row, 0.0), axis=1, keepdims=True)


def _col_pick(x, idx):
    lane = lax.broadcasted_iota(jnp.int32, x.shape, 1)
    return jnp.sum(jnp.where(lane == idx, x, 0.0), axis=1, keepdims=True)


def _conv_silu(x, w):
    xc = (w[3:4, :] * x + w[2:3, :] * _shift_down(x, 1) + w[1:2, :] * _shift_down(x, 2)
          + w[0:1, :] * _shift_down(x, 3))
    return xc * _sigmoid(xc), xc


def _conv_silu_bwd(x, w, xc, dxs, dw_ref):
    sg = _sigmoid(xc)
    dxc = dxs * (sg * (1.0 + xc * (1.0 - sg)))
    dx = (w[3:4, :] * dxc + w[2:3, :] * _shift_up(dxc, 1) + w[1:2, :] * _shift_up(dxc, 2)
          + w[0:1, :] * _shift_up(dxc, 3))
    dw_ref[3:4, :] += jnp.sum(dxc * x, axis=0, keepdims=True)
    dw_ref[2:3, :] += jnp.sum(dxc * _shift_down(x, 1), axis=0, keepdims=True)
    dw_ref[1:2, :] += jnp.sum(dxc * _shift_down(x, 2), axis=0, keepdims=True)
    dw_ref[0:1, :] += jnp.sum(dxc * _shift_down(x, 3), axis=0, keepdims=True)
    return dx


def _dn_prep(qr_ref, kr_ref, vr_ref, ab_ref, cq_ref, ck_ref, cv_ref, par_ref, head):
    qs, qc = _conv_silu(qr_ref[...], cq_ref[...])
    ks, kc = _conv_silu(kr_ref[...], ck_ref[...])
    vs, vc = _conv_silu(vr_ref[...], cv_ref[...])
    rq = lax.rsqrt(jnp.sum(qs * qs, axis=1, keepdims=True) + EPS)
    rk = lax.rsqrt(jnp.sum(ks * ks, axis=1, keepdims=True) + EPS)
    ab = ab_ref[...]
    a_in = _col_pick(ab, head) + _lane_pick(par_ref[1:2, :], head)
    beta = _sigmoid(_col_pick(ab, NH + head))
    neg_ea = -jnp.exp(_lane_pick(par_ref[0:1, :], head))
    g = neg_ea * _softplus(a_in)
    return dict(q=qs * rq * (HD ** -0.5), k=ks * rk, v=vs, beta=beta, g=g, qs=qs, ks=ks, qc=qc, kc=kc, vc=vc,
                rq=rq, rk=rk, a_in=a_in, neg_ea=neg_ea)


ONE_BUF = pl.Buffered(1)
DN_BWD_VMEM = 62 * 1024 * 1024


def _dn_specs(nb, s):
    def col(cb):
        return pl.BlockSpec((s, HD), lambda h, b: (b, cb + h), pipeline_mode=ONE_BUF)

    def conv(cb):
        return pl.BlockSpec((DN_CONV_W, HD), lambda h, b: (0, cb + h))

    return col, conv


DN_CONV_W = 4


def _with_exchange(body, n_in, n_out, n_scratch, exchange, grid):
    if exchange is None:
        return body
    parts_fn, n, nls = exchange

    def wrapped(*refs):
        ins, xs = refs[:n_in], refs[n_in:n_in + n]
        outs, os = refs[n_in + n:n_in + n + n_out], refs[n_in + n + n_out:n_in + 2 * n + n_out]
        rest = refs[n_in + 2 * n + n_out:]
        scratch, sems = rest[:n_scratch], rest[n_scratch:]
        pos = [pl.program_id(k) for k in range(len(grid))]
        first = functools.reduce(jnp.logical_and, [p == 0 for p in pos])
        last = functools.reduce(jnp.logical_and, [p == g - 1 for p, g in zip(pos, grid)])
        start, finish = parts_fn(xs, os, nls, *sems)
        pl.when(first)(start)
        body(*ins, *outs, *scratch)
        pl.when(last)(finish)

    return wrapped


def _dn_fwd(proj, ab, conv_w, par, gain, nb, s, *, name, gather=None):
    nc = s // DN_CHUNK
    col, conv = _dn_specs(nb, s)
    gx, gnl = gather if gather else ([], [])

    def body(qr_ref, kr_ref, vr_ref, z_ref, ab_ref, cq_ref, ck_ref, cv_ref, par_ref, gain_ref,
             y_ref, o_ref, st_ref, u_ref, w_ref, at_ref, qd_ref, kd_ref, cd_ref, q_s, k_s, v_s, bb_s, gb_s):
        p = _dn_prep(qr_ref, kr_ref, vr_ref, ab_ref, cq_ref, ck_ref, cv_ref, par_ref, pl.program_id(0))
        q_s[...], k_s[...], v_s[...] = p["q"], p["k"], p["v"]
        bb_s[...] = jnp.broadcast_to(p["beta"], (s, HD))
        gb_s[...] = jnp.broadcast_to(p["g"], (s, HD))
        def local(*args):
            u, w, attn, qd, kd, cd = _dn_local(*args)
            return u, w, _attn_pairs(attn), qd, kd, cd

        _dn_local_pass(local, s, [q_s, k_s, v_s, bb_s, gb_s], [u_ref, w_ref, at_ref, qd_ref, kd_ref, cd_ref])

        def chunk_pair(pi, state):
            for odd in (0, 1):
                ci = 2 * pi + odd
                sl = pl.ds(pl.multiple_of(ci * DN_CHUNK, DN_CHUNK), DN_CHUNK)
                st_ref[ci] = state
                o, state = _dn_step(u_ref[sl, :], w_ref[sl, :], at_ref[sl, :], qd_ref[sl, :], kd_ref[sl, :],
                                    cd_ref[sl, :], state, odd)
                o_ref[sl, :] = o
            return state

        lax.fori_loop(0, nc // 2, chunk_pair, jnp.zeros((HD, HD), F32))
        o = o_ref[...]
        z = z_ref[...]
        on = o * lax.rsqrt(jnp.mean(o * o, axis=1, keepdims=True) + EPS) * gain_ref[...]
        y_ref[...] = on * (z * _sigmoid(z))

    out = pl.BlockSpec((s, HD), lambda h, b: (b, h))
    sds = jax.ShapeDtypeStruct((nb * s, BW), F32)
    exchange = (_gather_parts, len(gx), gnl) if gx else None
    res = pl.pallas_call(
        _with_exchange(body, 10, 9, 5, exchange, (NH, nb)), grid=(NH, nb),
        in_specs=[col(CB_DNQ), col(CB_DNK), col(CB_DNV), col(CB_DNZ), pl.BlockSpec((s, LANE), lambda h, b: (b, 0)),
                  conv(0), conv(NH), conv(2 * NH), pl.BlockSpec((8, LANE), lambda h, b: (0, 0)),
                  pl.BlockSpec((1, HD), lambda h, b: (0, 0))] + [HBM_SPEC] * len(gx),
        out_specs=[out, out, pl.BlockSpec((None, None, nc, HD, HD), lambda h, b: (b, h, 0, 0, 0))] + [out] * 6
        + [HBM_SPEC] * len(gx),
        out_shape=[sds, sds, jax.ShapeDtypeStruct((nb, NH, nc, HD, HD), F32)] + [sds] * 6 + _gather_shapes(gx, gnl),
        scratch_shapes=[pltpu.VMEM((s, HD), F32)] * 5 + (_comm_sems(len(gx), 7) if gx else []),
        compiler_params=_cp(("arbitrary", "arbitrary")), name=name)(
            proj, proj, proj, proj, ab, conv_w, conv_w, conv_w, par, gain, *gx)
    return res[0], res[1], res[2], list(res[3:9]), list(res[9:])


def _dn_bwd(proj, ab, conv_w, par, gain, o_pre, states, local, dy, nb, s, *, name, scatter=None):
    nc = s // DN_CHUNK
    col, conv = _dn_specs(nb, s)
    gx, gnl = scatter if scatter else ([], [])

    def body(qr_ref, kr_ref, vr_ref, z_ref, ab_ref, cq_ref, ck_ref, cv_ref, par_ref, gain_ref, o_ref, st_ref, dy_ref,
             u_hbm, w_hbm, at_hbm, qd_hbm, kd_hbm, cd_hbm,
             dqr_ref, dkr_ref, dvr_ref, dz_ref, dab_ref, dcq_ref, dck_ref, dcv_ref, dpar_ref, dgain_ref,
             q_s, k_s, v_s, bb_s, gb_s, do_s, u_s, w_s, qd_s, kd_s, at_s, cd_s, load_sems):
        head, b = pl.program_id(0), pl.program_id(1)
        local_refs = [u_s, w_s, at_s, qd_s, kd_s, cd_s]
        loads = [pltpu.make_async_copy(src.at[pl.ds(pl.multiple_of(b * s, s), s), pl.ds(pl.multiple_of(head * HD, HD), HD)],
                                       dst, load_sems.at[i])
                 for i, (src, dst) in enumerate(zip((u_hbm, w_hbm, at_hbm, qd_hbm, kd_hbm, cd_hbm), local_refs))]
        for cp in loads:
            cp.start()
        p = _dn_prep(qr_ref, kr_ref, vr_ref, ab_ref, cq_ref, ck_ref, cv_ref, par_ref, head)
        q_s[...], k_s[...], v_s[...] = p["q"], p["k"], p["v"]
        bb_s[...] = jnp.broadcast_to(p["beta"], (s, HD))
        gb_s[...] = jnp.broadcast_to(p["g"], (s, HD))

        @pl.when(b == 0)
        def _():
            for ref in (dcq_ref, dck_ref, dcv_ref, dpar_ref):
                ref[...] = jnp.zeros_like(ref)

        @pl.when((b == 0) & (head == 0))
        def _():
            dgain_ref[...] = jnp.zeros_like(dgain_ref)

        o, z, dy = o_ref[...], z_ref[...], dy_ref[...]
        rstd = lax.rsqrt(jnp.mean(o * o, axis=1, keepdims=True) + EPS)
        ohat = o * rstd
        sgz = _sigmoid(z)
        dz_ref[...] = dy * (ohat * gain_ref[...]) * (sgz * (1.0 + z * (1.0 - sgz)))
        don = dy * (z * sgz)
        dgain_ref[...] += jnp.sum(don * ohat, axis=0, keepdims=True)
        dxh = don * gain_ref[...]
        do_s[...] = rstd * (dxh - ohat * jnp.mean(dxh * ohat, axis=1, keepdims=True))

        for cp in loads:
            cp.wait()

        def chunk_pair(pr, dstate):
            for odd in (1, 0):
                ci = nc - 1 - 2 * pr - (1 - odd)
                sl = pl.ds(pl.multiple_of(ci * DN_CHUNK, DN_CHUNK), DN_CHUNK)
                _, vjp = jax.vjp(functools.partial(_dn_step, odd=odd), u_s[sl, :], w_s[sl, :], at_s[sl, :],
                                 qd_s[sl, :], kd_s[sl, :], cd_s[sl, :], st_ref[ci])
                du, dw, dat, dqd, dkd, dcd, dstate = vjp((do_s[sl, :], dstate))
                u_s[sl, :], w_s[sl, :], at_s[sl, :], qd_s[sl, :], kd_s[sl, :], cd_s[sl, :] = du, dw, dat, dqd, dkd, dcd
            return dstate

        lax.fori_loop(0, nc // 2, chunk_pair, jnp.zeros((HD, HD), F32))

        def local_bwd(q, k, v, bb, gb, du, dw, dat, dqd, dkd, dcd):
            _, vjp = jax.vjp(_dn_local, q, k, v, bb, gb)
            dq, dk, dv, dbb, dgb = vjp((du, dw, _attn_unpairs(dat), dqd, dkd, dcd))
            return (dq, dk, dv, jnp.broadcast_to(jnp.sum(dbb, axis=1, keepdims=True), (DN_SUPER, HD)),
                    jnp.broadcast_to(jnp.sum(dgb, axis=1, keepdims=True), (DN_SUPER, HD)))

        _dn_local_pass(local_bwd, s, [q_s, k_s, v_s, bb_s, gb_s] + local_refs, [q_s, k_s, v_s, bb_s, gb_s])

        dq, dk, dv = q_s[...], k_s[...], v_s[...]
        qs, ks, rq, rk = p["qs"], p["ks"], p["rq"], p["rk"]
        dqs = (HD ** -0.5) * (rq * dq - qs * (rq * rq * rq) * jnp.sum(dq * qs, axis=1, keepdims=True))
        dks = rk * dk - ks * (rk * rk * rk) * jnp.sum(dk * ks, axis=1, keepdims=True)
        dqr_ref[...] = _conv_silu_bwd(qr_ref[...], cq_ref[...], p["qc"], dqs, dcq_ref)
        dkr_ref[...] = _conv_silu_bwd(kr_ref[...], ck_ref[...], p["kc"], dks, dck_ref)
        dvr_ref[...] = _conv_silu_bwd(vr_ref[...], cv_ref[...], p["vc"], dv, dcv_ref)

        dbeta, dg = bb_s[:, 0:1], gb_s[:, 0:1]
        beta = p["beta"]
        db_logit = dbeta * beta * (1.0 - beta)
        da = dg * p["neg_ea"] * _sigmoid(p["a_in"])
        lane = lax.broadcasted_iota(jnp.int32, (s, LANE), 1)
        dab_ref[...] = jnp.where(lane == head, da, 0.0) + jnp.where(lane == NH + head, db_logit, 0.0)
        dpar_ref[0:1, :] += jnp.broadcast_to(jnp.sum(dg * p["g"], axis=0, keepdims=True), (1, LANE))
        dpar_ref[1:2, :] += jnp.broadcast_to(jnp.sum(da, axis=0, keepdims=True), (1, LANE))

    out = pl.BlockSpec((s, HD), lambda h, b: (b, h))
    in_blk = pl.BlockSpec((s, HD), lambda h, b: (b, h), pipeline_mode=ONE_BUF)
    cblk = pl.BlockSpec((DN_CONV_W, HD), lambda h, b: (0, h))
    sds = jax.ShapeDtypeStruct((nb * s, BW), F32)
    csds = jax.ShapeDtypeStruct((DN_CONV_W, BW), F32)
    exchange = (_all_to_all_parts, len(gx), gnl) if gx else None
    res = pl.pallas_call(
        _with_exchange(body, 19, 10, 13, exchange, (NH, nb)), grid=(NH, nb),
        in_specs=[col(CB_DNQ), col(CB_DNK), col(CB_DNV), col(CB_DNZ),
                  pl.BlockSpec((s, LANE), lambda h, b: (b, 0), pipeline_mode=ONE_BUF),
                  conv(0), conv(NH), conv(2 * NH), pl.BlockSpec((8, LANE), lambda h, b: (0, 0)),
                  pl.BlockSpec((1, HD), lambda h, b: (0, 0)), in_blk,
                  pl.BlockSpec((None, None, nc, HD, HD), lambda h, b: (b, h, 0, 0, 0), pipeline_mode=ONE_BUF), in_blk]
        + [HBM_SPEC] * (6 + len(gx)),
        out_specs=[out, out, out, out, pl.BlockSpec((None, s, LANE), lambda h, b: (h, b, 0)), cblk, cblk, cblk,
                   pl.BlockSpec((None, 8, LANE), lambda h, b: (h, 0, 0)), pl.BlockSpec((1, HD), lambda h, b: (0, 0))]
        + [HBM_SPEC] * len(gx),
        out_shape=[sds, sds, sds, sds, jax.ShapeDtypeStruct((NH, nb * s, LANE), F32), csds, csds, csds,
                   jax.ShapeDtypeStruct((NH, 8, LANE), F32), jax.ShapeDtypeStruct((1, HD), F32)]
        + _all_to_all_shapes(gx, gnl),
        scratch_shapes=[pltpu.VMEM((s, HD), F32)] * 12 + [pltpu.SemaphoreType.DMA((6,))]
        + (_comm_sems(len(gx), 7) if gx else []),
        compiler_params=_cp(("arbitrary", "arbitrary"), DN_BWD_VMEM), name=name)(
            proj, proj, proj, proj, ab, conv_w, conv_w, conv_w, par, gain, o_pre, states, dy, *local, *gx)
    return tuple(res[:10]) + (list(res[10:]),)


def _sum_heads(x, *, name):
    nh, t, c = x.shape
    tm = _pick(t, (1024, 512, 256, 128))

    def body(x_ref, o_ref):
        o_ref[...] = (x_ref[0] + x_ref[1] + x_ref[2] + x_ref[3]).astype(BF16)

    return pl.pallas_call(
        body, grid=(t // tm,), in_specs=[pl.BlockSpec((nh, tm, c), lambda i: (0, i, 0))],
        out_specs=pl.BlockSpec((tm, c), lambda i: (i, 0)), out_shape=jax.ShapeDtypeStruct((t, c), BF16),
        compiler_params=_cp(("parallel",)), name=name)(x)


MERGE_TM = 256


def _merge_fwd(x, proj, yp, yd, ys, b_gate, wb, wo, *, name):
    t, d = x.shape
    tm = _pick(t, (MERGE_TM, 128))

    def body(x_ref, g0_ref, g1_ref, g2_ref, yp_ref, yd_ref, ys_ref, bg_ref, wb_ref, wo_ref, o_ref):
        merged = jnp.zeros((tm, d), F32)
        for n, (g_ref, y_ref) in enumerate(((g0_ref, yp_ref), (g1_ref, yd_ref), (g2_ref, ys_ref))):
            gate = _sigmoid(g_ref[...] + bg_ref[:, n * d:(n + 1) * d])
            merged = merged + gate * _bdot(y_ref[...], wb_ref[n])
        o_ref[...] = x_ref[...] + _bdot(merged, wo_ref[...])

    row = pl.BlockSpec((tm, d), lambda i: (i, 0))
    yblk = pl.BlockSpec((tm, BW), lambda i: (i, 0))

    def gl(n):
        return pl.BlockSpec((tm, d), lambda i: (i, CB_GATE + n))

    return pl.pallas_call(
        body, grid=(t // tm,),
        in_specs=[row, gl(0), gl(1), gl(2), yblk, yblk, yblk, pl.BlockSpec((1, 3 * d), lambda i: (0, 0)),
                  pl.BlockSpec((3, BW, d), lambda i: (0, 0, 0)), pl.BlockSpec((d, d), lambda i: (0, 0))],
        out_specs=row, out_shape=jax.ShapeDtypeStruct((t, d), F32),
        compiler_params=_cp(("parallel",)), name=name)(x, proj, proj, proj, yp, yd, ys, b_gate, wb, wo)


def _merge_bwd(proj, yp, yd, ys, b_gate, wb, wo, dx, *, name):
    t, d = dx.shape
    tm = _pick(t, (MERGE_TM, 128))

    def body(g0_ref, g1_ref, g2_ref, yp_ref, yd_ref, ys_ref, bg_ref, wb_ref, wo_ref, dx_ref,
             dyp_ref, dyd_ref, dys_ref, dgl_ref, mg_ref, dxh_ref, dbd_ref, dbg_ref):
        dxh = dx_ref[...].astype(BF16)
        dxh_ref[...] = dxh
        dmerged = _bdot(dxh, wo_ref[...], NT_DIMS)
        merged = jnp.zeros((tm, d), F32)

        @pl.when(pl.program_id(0) == 0)
        def _():
            dbg_ref[...] = jnp.zeros_like(dbg_ref)

        for n, (g_ref, y_ref, dy_ref) in enumerate(((g0_ref, yp_ref, dyp_ref), (g1_ref, yd_ref, dyd_ref),
                                                    (g2_ref, ys_ref, dys_ref))):
            gate = _sigmoid(g_ref[...] + bg_ref[:, n * d:(n + 1) * d])
            bd = _bdot(y_ref[...], wb_ref[n])
            merged = merged + gate * bd
            dgl = dmerged * bd * gate * (1.0 - gate)
            dgl_ref[:, n * d:(n + 1) * d] = dgl.astype(BF16)
            dbg_ref[:, n * d:(n + 1) * d] += jnp.sum(dgl, axis=0, keepdims=True)
            dbd = (dmerged * gate).astype(BF16)
            dbd_ref[n] = dbd
            dy_ref[...] = _bdot(dbd, wb_ref[n], NT_DIMS)
        mg_ref[...] = merged.astype(BF16)

    row = pl.BlockSpec((tm, d), lambda i: (i, 0))
    yblk = pl.BlockSpec((tm, BW), lambda i: (i, 0))
    bgv = pl.BlockSpec((1, 3 * d), lambda i: (0, 0))

    def gl(n):
        return pl.BlockSpec((tm, d), lambda i: (i, CB_GATE + n))

    ysds = jax.ShapeDtypeStruct((t, BW), F32)
    return pl.pallas_call(
        body, grid=(t // tm,),
        in_specs=[gl(0), gl(1), gl(2), yblk, yblk, yblk, bgv,
                  pl.BlockSpec((3, BW, d), lambda i: (0, 0, 0)), pl.BlockSpec((d, d), lambda i: (0, 0)), row],
        out_specs=[yblk, yblk, yblk, pl.BlockSpec((tm, 3 * d), lambda i: (i, 0)), row, row,
                   pl.BlockSpec((3, tm, d), lambda i: (0, i, 0)), bgv],
        out_shape=[ysds, ysds, ysds, jax.ShapeDtypeStruct((t, 3 * d), BF16), jax.ShapeDtypeStruct((t, d), BF16),
                   jax.ShapeDtypeStruct((t, d), BF16), jax.ShapeDtypeStruct((3, t, d), BF16),
                   jax.ShapeDtypeStruct((1, 3 * d), F32)],
        compiler_params=_cp(("arbitrary",)), name=name)(proj, proj, proj, yp, yd, ys, b_gate, wb, wo, dx)


def _loss_head(x, g, target, *, name):
    t, d = x.shape
    tm = _pick(t, (512, 256, 128))

    def body(x_ref, g_ref, t_ref, dx_ref, dg_ref, loss_ref):
        xhat, rstd = _rms_stats(x_ref[...])
        err = xhat * g_ref[...] - t_ref[...]
        dx, dg = _rms_bwd_vals(err * (1.0 / d), xhat, rstd, g_ref[...])
        dx_ref[...] = dx

        @pl.when(pl.program_id(0) == 0)
        def _():
            dg_ref[...] = jnp.zeros_like(dg_ref)
            loss_ref[...] = jnp.zeros_like(loss_ref)

        dg_ref[...] += dg
        part = jnp.sum(jnp.sum(err * err, axis=1, keepdims=True), axis=0, keepdims=True) * (0.5 / d)
        loss_ref[...] += jnp.broadcast_to(part, (1, LANE))

    row = pl.BlockSpec((tm, d), lambda i: (i, 0))
    vec = pl.BlockSpec((1, d), lambda i: (0, 0))
    return pl.pallas_call(
        body, grid=(t // tm,), in_specs=[row, vec, row],
        out_specs=[row, vec, pl.BlockSpec((1, LANE), lambda i: (0, 0))],
        out_shape=[jax.ShapeDtypeStruct((t, d), F32), jax.ShapeDtypeStruct((1, d), F32),
                   jax.ShapeDtypeStruct((1, LANE), F32)],
        compiler_params=_cp(("arbitrary",)), name=name)(x, g.reshape(1, d), target)


def _adamw(w, g, m, v, *, name):
    rows, cols = w.shape
    fits = [c for c in (1024, 704, 512, 352, 256, 128, 64, 32, 16, 8) if c * cols * 4 * 14 <= VMEM_LIMIT // 2]
    tr = _pick(rows, fits)
    c1 = 1.0 / (1.0 - ADAM_B1 ** ADAM_STEP)
    c2 = 1.0 / (1.0 - ADAM_B2 ** ADAM_STEP)

    def body(w_ref, g_ref, m_ref, v_ref, d_ref, nm_ref, nv_ref):
        g = g_ref[...]
        nm = ADAM_B1 * m_ref[...] + (1.0 - ADAM_B1) * g
        nv = ADAM_B2 * v_ref[...] + (1.0 - ADAM_B2) * (g * g)
        nm_ref[...] = nm
        nv_ref[...] = nv
        d_ref[...] = -ADAM_LR * ((nm * c1) / (jnp.sqrt(nv * c2) + ADAM_EPS) + ADAM_WD * w_ref[...])

    blk = pl.BlockSpec((tr, cols), lambda i: (i, 0))
    sds = jax.ShapeDtypeStruct((rows, cols), F32)
    return pl.pallas_call(
        body, grid=(rows // tr,), in_specs=[blk] * 4, out_specs=[blk] * 3, out_shape=[sds] * 3,
        compiler_params=_cp(("parallel",)), name=name)(w, g, m, v)


MESH_ID = pl.DeviceIdType.MESH
HBM_SPEC = pl.BlockSpec(memory_space=pl.ANY)
OTHER_CHIPS = ((1, 0), (0, 1), (1, 1))


def _at_slot(ref, nl, slot):
    return ref.at[(slice(None),) * nl + (slot,)]


def _slotted(shape, nl, slots):
    return tuple(shape[:nl]) + (slots,) + tuple(shape[nl:])


def _flip(v, f):
    return 1 - v if f else v


def _comm_call(body, n, out_shapes, n_remote, args, name):
    return pl.pallas_call(
        body, out_shape=out_shapes, in_specs=[HBM_SPEC] * len(args), out_specs=[HBM_SPEC] * len(out_shapes),
        scratch_shapes=[pltpu.SemaphoreType.DMA((n * n_remote,)), pltpu.SemaphoreType.DMA((n * n_remote,)),
                        pltpu.SemaphoreType.DMA((n * 4,))],
        compiler_params=pltpu.CompilerParams(has_side_effects=True), name=name)(*args)


def _gather(xs, nls, *, name):
    n = len(xs)

    def body(*refs):
        start, finish = _gather_parts(refs[:n], refs[n:2 * n], nls, *refs[2 * n:])
        start()
        finish()

    return _comm_call(body, n, _gather_shapes(xs, nls), 7, xs, name)


def _gather_shapes(xs, nls):
    return [jax.ShapeDtypeStruct(_slotted(v.shape, nl, N_DEV), v.dtype) for v, nl in zip(xs, nls)]


def _comm_sems(n, n_remote):
    return [pltpu.SemaphoreType.DMA((n * n_remote,)), pltpu.SemaphoreType.DMA((n * n_remote,)),
            pltpu.SemaphoreType.DMA((n * 4,))]


def _gather_parts(x_refs, o_refs, nls, send_sems, recv_sems, local_sems):
    n = len(x_refs)
    x, y, c = lax.axis_index("x"), lax.axis_index("y"), lax.axis_index("c")
    me, sibling = (x, y, c), (x, y, 1 - c)
    chips = [(_flip(x, fx), _flip(y, fy)) for fx, fy in OTHER_CHIPS]

    def copy(a, k, block, to, src=None):
        dst = _at_slot(o_refs[a], nls[a], 4 * block[0] + 2 * block[1] + block[2])
        return pltpu.make_async_remote_copy(
            src_ref=dst if src is None else src, dst_ref=dst, send_sem=send_sems.at[a * 7 + k],
            recv_sem=recv_sems.at[a * 7 + k], device_id=to, device_id_type=MESH_ID)

    def mine(a):
        return pltpu.make_async_copy(x_refs[a], _at_slot(o_refs[a], nls[a], 4 * x + 2 * y + c), local_sems.at[a])

    def first(a):
        return ([copy(a, 0, me, sibling, src=x_refs[a])]
                + [copy(a, 1 + j, me, (*chip, c), src=x_refs[a]) for j, chip in enumerate(chips)])

    def start():
        for a in range(n):
            mine(a).start()
            for cp in first(a):
                cp.start()

    def finish():
        passed = []
        for j, chip in enumerate(chips):
            for a in range(n):
                copy(a, 1 + j, (*chip, c), me).wait_recv()
                passed.append(copy(a, 4 + j, (*chip, c), sibling))
                passed[-1].start()
        for a in range(n):
            copy(a, 0, sibling, me).wait_recv()
            for j, chip in enumerate(chips):
                copy(a, 4 + j, (*chip, 1 - c), me).wait_recv()
        for a in range(n):
            for cp in first(a):
                cp.wait_send()
        for cp in passed:
            cp.wait_send()
        for a in range(n):
            mine(a).wait()

    return start, finish


ALL_FLIPS = ((0, 0, 1), (0, 1, 0), (0, 1, 1), (1, 0, 0), (1, 0, 1), (1, 1, 0), (1, 1, 1))


def _all_to_all_parts(g_refs, r_refs, nls, send_sems, recv_sems, local_sems):
    del local_sems
    n = len(g_refs)
    x, y, c = lax.axis_index("x"), lax.axis_index("y"), lax.axis_index("c")

    def copies():
        out = []
        for a in range(n):
            for k, (fx, fy, fc) in enumerate(ALL_FLIPS):
                p = (_flip(x, fx), _flip(y, fy), _flip(c, fc))
                out.append(pltpu.make_async_remote_copy(
                    src_ref=_at_slot(g_refs[a], nls[a], 4 * p[0] + 2 * p[1] + p[2]), dst_ref=_at_slot(r_refs[a], nls[a], k),
                    send_sem=send_sems.at[a * 7 + k], recv_sem=recv_sems.at[a * 7 + k], device_id=p,
                    device_id_type=MESH_ID))
        return out

    def start():
        for cp in copies():
            cp.start()

    def finish():
        cps = copies()
        for cp in cps:
            cp.wait_recv()
        for cp in cps:
            cp.wait_send()

    return start, finish


def _all_to_all_shapes(gs, nls):
    return [jax.ShapeDtypeStruct(_slotted(v.shape[:nl] + v.shape[nl + 1:], nl, 7), v.dtype) for v, nl in zip(gs, nls)]


def _scatter_pair(gs, nls, *, name):
    n = len(gs)

    def body(*refs):
        g_refs, got_refs, (send_sems, recv_sems, _) = refs[:n], refs[n:2 * n], refs[2 * n:]
        x, y, c = lax.axis_index("x"), lax.axis_index("y"), lax.axis_index("c")
        remote = []
        for a in range(n):
            for q in range(4):
                rc = pltpu.make_async_remote_copy(
                    src_ref=_at_slot(g_refs[a], nls[a], 2 * q + 1 - c), dst_ref=_at_slot(got_refs[a], nls[a], q),
                    send_sem=send_sems.at[a * 4 + q], recv_sem=recv_sems.at[a * 4 + q], device_id=(x, y, 1 - c),
                    device_id_type=MESH_ID)
                rc.start()
                remote.append(rc)
        for rc in remote:
            rc.wait_recv()
        for rc in remote:
            rc.wait_send()

    outs = [jax.ShapeDtypeStruct(_slotted(v.shape[:nl] + v.shape[nl + 1:], nl, 4), v.dtype) for v, nl in zip(gs, nls)]
    return _comm_call(body, n, outs, 4, gs, name)


def _scatter_chips(ps, nls, *, name):
    n = len(ps)

    def body(*refs):
        p_refs, r_refs, (send_sems, recv_sems, _) = refs[:n], refs[n:2 * n], refs[2 * n:]
        x, y, c = lax.axis_index("x"), lax.axis_index("y"), lax.axis_index("c")
        remote = []
        for a in range(n):
            for k, (fx, fy) in enumerate(OTHER_CHIPS):
                tx, ty = _flip(x, fx), _flip(y, fy)
                rc = pltpu.make_async_remote_copy(
                    src_ref=_at_slot(p_refs[a], nls[a], 2 * tx + ty), dst_ref=_at_slot(r_refs[a], nls[a], k),
                    send_sem=send_sems.at[a * 3 + k], recv_sem=recv_sems.at[a * 3 + k], device_id=(tx, ty, c),
                    device_id_type=MESH_ID)
                rc.start()
                remote.append(rc)
        for rc in remote:
            rc.wait_recv()
        for rc in remote:
            rc.wait_send()

    outs = [jax.ShapeDtypeStruct(_slotted(v.shape[:nl] + v.shape[nl + 1:], nl, 3), v.dtype) for v, nl in zip(ps, nls)]
    return _comm_call(body, n, outs, 3, ps, name)


def _pair_add(g, got, core, *, name):
    rows, cols = g.shape[-2:]
    lf = math.prod(got.shape[:-3])
    tr = _pick(rows, (1024, 512, 352, 256, 128))

    def body(core_ref, g_ref, got_ref, o_ref):
        o_ref[...] = (g_ref[...].astype(F32) + got_ref[...].astype(F32)).astype(BF16)

    blk = pl.BlockSpec((None, None, tr, cols), lambda i, q, j, core_ref: (i, q, j, 0))
    out = pl.pallas_call(
        body, grid_spec=pltpu.PrefetchScalarGridSpec(
            num_scalar_prefetch=1, grid=(lf, 4, rows // tr),
            in_specs=[pl.BlockSpec((None, None, None, tr, cols), lambda i, q, j, core_ref: (i, q, core_ref[0], j, 0)),
                      blk], out_specs=blk),
        out_shape=jax.ShapeDtypeStruct((lf, 4, rows, cols), BF16),
        compiler_params=_cp(("parallel", "parallel", "parallel")), name=name)(
            core, g.reshape(lf, 4, 2, rows, cols), got.reshape(lf, 4, rows, cols))
    return out.reshape(got.shape)


def _sum_adamw(p, r, own, w, m, v, layer, prev, *, name):
    shape = w.shape[1:]
    rows, cols = shape[-2:]
    lf = math.prod(shape[:-2])
    np_, nk = p.shape[-3], r.shape[-3]
    fits = [c for c in (1024, 512, 352, 256, 128, 64, 32, 16) if c * cols * (7 * 4 + (nk + 1) * 2) * 2 <= VMEM_LIMIT // 2]
    tr = _pick(rows, fits)
    c1 = 1.0 / (1.0 - ADAM_B1 ** ADAM_STEP)
    c2 = 1.0 / (1.0 - ADAM_B2 ** ADAM_STEP)

    def body(own_ref, p_ref, r_ref, w_ref, m_ref, v_ref, *rest):
        g_ref, d_ref, nm_ref, nv_ref = rest[-4:]
        g = p_ref[...].astype(F32)
        for k in range(nk):
            g = g + r_ref[k].astype(F32)
        g_ref[...] = g
        nm = ADAM_B1 * m_ref[...] + (1.0 - ADAM_B1) * g
        nv = ADAM_B2 * v_ref[...] + (1.0 - ADAM_B2) * (g * g)
        nm_ref[...] = nm
        nv_ref[...] = nv
        d_ref[...] = -ADAM_LR * ((nm * c1) / (jnp.sqrt(nv * c2) + ADAM_EPS) + ADAM_WD * w_ref[...])

    wblk = pl.BlockSpec((None, None, tr, cols), lambda i, j, own_ref: (layer, i, j, 0))
    full = (w.shape[0], lf, rows, cols)
    sds = jax.ShapeDtypeStruct(full, F32)
    prev = [] if prev is None else [a.reshape(full) for a in prev]
    outs = pl.pallas_call(
        body, grid_spec=pltpu.PrefetchScalarGridSpec(
            num_scalar_prefetch=1, grid=(lf, rows // tr),
            in_specs=[pl.BlockSpec((None, None, tr, cols), lambda i, j, own_ref: (i, own_ref[0], j, 0)),
                      pl.BlockSpec((None, nk, tr, cols), lambda i, j, own_ref: (i, 0, j, 0))] + [wblk] * 3
            + [HBM_SPEC] * len(prev),
            out_specs=[wblk] * 4),
        out_shape=[sds] * 4, input_output_aliases={6 + i: i for i in range(len(prev))},
        compiler_params=_cp(("parallel", "parallel")), name=name)(
            own, p.reshape(lf, np_, rows, cols), r.reshape(lf, nk, rows, cols), w.reshape(full), m.reshape(full),
            v.reshape(full), *prev)
    return [o.reshape(w.shape) for o in outs]


def _sum_slots(x, *, name):
    nd, rows, cols = x.shape
    tr = _pick(rows, (512, 256, 128, 64, 32, 16, 8))

    def body(x_ref, o_ref):
        acc = x_ref[0].astype(F32)
        for j in range(1, nd):
            acc = acc + x_ref[j].astype(F32)
        o_ref[...] = acc

    return pl.pallas_call(
        body, grid=(rows // tr,), in_specs=[pl.BlockSpec((nd, tr, cols), lambda i: (0, i, 0))],
        out_specs=pl.BlockSpec((tr, cols), lambda i: (i, 0)), out_shape=jax.ShapeDtypeStruct((rows, cols), F32),
        compiler_params=_cp(("parallel",)), name=name)(x)


def _pad_rows(a, mult=8):
    r = (-a.shape[0]) % mult
    return jnp.pad(a, ((0, r), (0, 0))) if r else a


def _flat128(a):
    f = a.reshape(-1)
    return jnp.pad(f, (0, (-f.shape[0]) % LANE)).reshape(-1, LANE)


def _unshard(gathered, shape, axis):
    g = gathered.reshape((N_DEV,) + tuple(shape))
    g = jnp.moveaxis(g, 0, axis)
    full = list(shape)
    full[axis] *= N_DEV
    return g.reshape(full)


def _col_shards(full):
    rows, cols = full.shape
    return jnp.moveaxis(full.reshape(rows, N_DEV, cols // N_DEV), 1, 0)


BIG = (("ffn_w_gate", 2), ("ffn_w_up", 2), ("ffn_w_down", 2), ("w_in", 1), ("w_branch", 2), ("w_out", 1))


def kernel(x, ffn_norm, ffn_w_gate, ffn_w_up, ffn_w_down, mix_norm, w_in, b_gate, pool_w, pool_scale, dn_conv, dn_A_log, dn_dt_bias, dn_out_norm, w_branch, w_out, final_norm, loss_target, m_ffn_norm, m_ffn_w_gate, m_ffn_w_up, m_ffn_w_down, m_mix_norm, m_w_in, m_b_gate, m_pool_w, m_pool_scale, m_dn_conv, m_dn_A_log, m_dn_dt_bias, m_dn_out_norm, m_w_branch, m_w_out, m_final_norm, v_ffn_norm, v_ffn_w_gate, v_ffn_w_up, v_ffn_w_down, v_mix_norm, v_w_in, v_b_gate, v_pool_w, v_pool_scale, v_dn_conv, v_dn_A_log, v_dn_dt_bias, v_dn_out_norm, v_w_branch, v_w_out, v_final_norm):
    wts = dict(ffn_norm=ffn_norm, ffn_w_gate=ffn_w_gate, ffn_w_up=ffn_w_up, ffn_w_down=ffn_w_down, mix_norm=mix_norm,
               w_in=w_in, b_gate=b_gate, pool_w=pool_w, pool_scale=pool_scale, dn_conv=dn_conv, dn_A_log=dn_A_log,
               dn_dt_bias=dn_dt_bias, dn_out_norm=dn_out_norm, w_branch=w_branch, w_out=w_out, final_norm=final_norm)
    mom = dict(ffn_norm=m_ffn_norm, ffn_w_gate=m_ffn_w_gate, ffn_w_up=m_ffn_w_up, ffn_w_down=m_ffn_w_down,
               mix_norm=m_mix_norm, w_in=m_w_in, b_gate=m_b_gate, pool_w=m_pool_w, pool_scale=m_pool_scale,
               dn_conv=m_dn_conv, dn_A_log=m_dn_A_log, dn_dt_bias=m_dn_dt_bias, dn_out_norm=m_dn_out_norm,
               w_branch=m_w_branch, w_out=m_w_out, final_norm=m_final_norm)
    var = dict(ffn_norm=v_ffn_norm, ffn_w_gate=v_ffn_w_gate, ffn_w_up=v_ffn_w_up, ffn_w_down=v_ffn_w_down,
               mix_norm=v_mix_norm, w_in=v_w_in, b_gate=v_b_gate, pool_w=v_pool_w, pool_scale=v_pool_scale,
               dn_conv=v_dn_conv, dn_A_log=v_dn_A_log, dn_dt_bias=v_dn_dt_bias, dn_out_norm=v_dn_out_norm,
               w_branch=v_w_branch, w_out=v_w_out, final_norm=v_final_norm)
    nb, s, d = x.shape
    t = nb * s
    me = 4 * lax.axis_index("x") + 2 * lax.axis_index("y") + lax.axis_index("c")

    big = [n for n, _ in BIG]
    nls = [nl - 1 for _, nl in BIG]
    shards = lambda l: [wts[n][l].astype(BF16) for n in big]
    small_sh = jnp.concatenate([_flat128(ffn_norm), _flat128(dn_conv)], axis=0)
    *gat0, small_g = _gather(shards(0) + [small_sh], nls + [0], name="gather_weights")
    full = [dict(zip(big, gat0))] + [None] * (DEPTH - 1)

    def mixer_weights(l):
        w_in_full = jnp.moveaxis(full[l]["w_in"], 0, 1).reshape(d, -1)
        w_main = jnp.concatenate([w_in_full[:, :AB_LO], w_in_full[:, AB_HI:]], axis=1)
        w_ab = jnp.pad(w_in_full[:, AB_LO:AB_HI], ((0, 0), (0, LANE - (AB_HI - AB_LO))))
        wb = jnp.moveaxis(full[l]["w_branch"], 1, 2).reshape(3, BW, d)
        return w_main, w_ab, wb, full[l]["w_out"].reshape(d, d)

    nfr = ffn_norm.size // LANE
    ffn_norm_full = _unshard(small_g[:, :nfr], ffn_norm.shape, 2)
    dn_conv_full = _unshard(small_g[:, nfr:], dn_conv.shape, 2)
    pool_w_h = pool_w.astype(BF16)

    xs = x.reshape(t, d)
    saved = []
    for l in range(DEPTH):
        sv = dict(x0=xs)
        xs, *sv["ab0"] = _ffn_fwd(xs, ffn_norm_full[l, 0], full[l]["ffn_w_gate"][0], full[l]["ffn_w_up"][0],
                                  full[l]["ffn_w_down"][0], name="ffn_fwd")
        sv["x1"] = xs
        w_main, w_ab, wb, wo = mixer_weights(l)
        h = _rms_fwd(xs, mix_norm[l], name="mix_rms")
        proj = _mm(h, w_main, name="proj")
        ab = _mm(h, w_ab, name="proj_ab")
        par = jnp.pad(jnp.stack([dn_A_log[l], dn_dt_bias[l]]), ((0, 6), (0, LANE - NH)))
        gain = dn_out_norm[l].reshape(1, HD)
        psc = pool_scale[l].reshape(1, BW)
        yp = _pool_fwd(proj, pool_w_h[l], psc, nb, s, name="pool_fwd")
        yd, o_pre, states, dn_local, gat = _dn_fwd(proj, ab, dn_conv_full[l], par, gain, nb, s,
                                         name="dn_fwd" if l == DEPTH - 1 else "dn_fwd_gather",
                                         gather=(shards(l + 1), nls) if l < DEPTH - 1 else None)
        if l < DEPTH - 1:
            full[l + 1] = dict(zip(big, gat))
        ys, sb_ctr = _sb_fwd(proj, nb, s, name="sb_fwd")
        bg = b_gate[l].reshape(1, 3 * d)
        xs = _merge_fwd(xs, proj, yp, yd, ys, bg, wb, wo, name="merge_fwd")
        sv.update(x2=xs, h=h, proj=proj, ab=ab, par=par, gain=gain, psc=psc, yp=yp, yd=yd, ys=ys, sb_ctr=sb_ctr, o_pre=o_pre,
                  states=states, dn_local=dn_local, bg=bg, w_main=w_main, w_ab=w_ab, wb=wb, wo=wo)
        xs, *sv["ab1"] = _ffn_fwd(xs, ffn_norm_full[l, 1], full[l]["ffn_w_gate"][1], full[l]["ffn_w_up"][1],
                                  full[l]["ffn_w_down"][1], name="ffn_fwd")
        saved.append(sv)

    dx, g_final, loss_row = _loss_head(xs, final_norm, loss_target.reshape(t, d), name="loss_head")
    loss = lax.psum(loss_row[0, 0], ("x", "y", "c"))

    gw = {n: [None] * DEPTH for n in ("ffn_norm", "ffn_w_gate", "ffn_w_up", "ffn_w_down", "mix_norm", "w_in", "b_gate",
                                      "pool_w", "pool_scale", "dn_conv", "dn_A_log", "dn_dt_bias", "dn_out_norm",
                                      "w_branch", "w_out")}

    me_i = me.astype(jnp.int32).reshape(1)
    updated = {n: None for n in big}
    pending = None

    def finish_layer(l, own_blocks, arrived, own_slot):
        for n, p, r in zip(big, own_blocks, arrived):
            updated[n] = _sum_adamw(p, r, own_slot, wts[n], mom[n], var[n], l, updated[n], name=f"adamw_{n}_{l}")

    def ffn_back(l, i, x_in, dy):
        dxi, dg, hb, dyh, da, db, sact = _ffn_bwd(x_in, ffn_norm_full[l, i], full[l]["ffn_w_gate"][i],
                                                  full[l]["ffn_w_up"][i], full[l]["ffn_w_down"][i],
                                                  *saved[l][f"ab{i}"], dy, name="ffn_bwd")
        return dxi, dg, (_mm_slots(hb, da, name="dw_gate_up"), _mm_slots(hb, db, name="dw_gate_up"),
                         _mm_slots(sact, dyh, name="dw_down"))

    for l in reversed(range(DEPTH)):
        sv = saved[l]
        dx, dg1, (dwg1, dwu1, dwd1) = ffn_back(l, 1, sv["x2"], dx)
        dyp, dyd, dys, dgl, merged, dxh, dbd, dbg = _merge_bwd(sv["proj"], sv["yp"], sv["yd"], sv["ys"], sv["bg"],
                                                               sv["wb"], sv["wo"], dx, name="merge_bwd")
        gw["w_out"][l] = _mm(merged, dxh, ta=True, out_dtype=BF16, name="dw_out").reshape(N_DEV, d // N_DEV, d)
        gw["w_branch"][l] = jnp.stack([_col_shards(_mm(y, dbd[n], ta=True, out_dtype=BF16, name="dw_branch"))
                                       for n, y in enumerate((sv["yp"], sv["yd"], sv["ys"]))])
        gw["b_gate"][l] = dbg.reshape(3 * d)
        du, dpw, dps = _pool_bwd(sv["proj"], pool_w_h[l], sv["psc"], dyp, nb, s, name="pool_bwd")
        gw["pool_w"][l], gw["pool_scale"][l] = dpw, dps.reshape(BW)
        dqr, dkr, dvr, dz, dab4, dcq, dck, dcv, dpar, dgain, arrived = _dn_bwd(
            sv["proj"], sv["ab"], dn_conv_full[l], sv["par"], sv["gain"], sv["o_pre"], sv["states"], sv["dn_local"],
            dyd, nb, s, name="dn_bwd_scatter" if pending is not None else "dn_bwd",
            scatter=([gw[n][pending] for n in big], nls) if pending is not None else None)
        if pending is not None:
            finish_layer(pending, [gw[n][pending] for n in big], arrived, me_i)
        gw["dn_conv"][l] = jnp.concatenate([dcq, dck, dcv], axis=1)
        gw["dn_A_log"][l], gw["dn_dt_bias"][l], gw["dn_out_norm"][l] = dpar[:, 0, 0], dpar[:, 1, 0], dgain.reshape(HD)
        dsq, dsk, dsv = _sb_bwd(sv["proj"], sv["sb_ctr"], dys, nb, s, name="sb_bwd")
        dab = _sum_heads(dab4, name="sum_heads")
        dproj = jnp.concatenate([du.astype(BF16), dqr.astype(BF16), dkr.astype(BF16), dvr.astype(BF16),
                                 dz.astype(BF16), dsq.astype(BF16), dsk.astype(BF16), dsv.astype(BF16), dgl], axis=1)
        dw_main = _mm(sv["h"], dproj, ta=True, out_dtype=BF16, name="dw_in")
        dw_ab = _mm(sv["h"], dab, ta=True, out_dtype=BF16, name="dw_ab")
        gw["w_in"][l] = _col_shards(jnp.concatenate([dw_main[:, :AB_LO], dw_ab[:, :AB_HI - AB_LO],
                                                     dw_main[:, AB_LO:]], axis=1))
        dh_main = _mm(dproj, sv["w_main"], tb=True, name="dh_mix")
        dh_ab = _mm(dab, sv["w_ab"], tb=True, name="dh_mix_ab")
        dx, dgm = _rms_bwd(sv["x1"], mix_norm[l], dh_main, dh_ab, dx, name="mix_rms_bwd")
        gw["mix_norm"][l] = dgm.reshape(d)
        dx, dg0, (dwg0, dwu0, dwd0) = ffn_back(l, 0, sv["x0"], dx)
        gw["ffn_norm"][l] = jnp.stack([dg0.reshape(d), dg1.reshape(d)])
        gw["ffn_w_gate"][l] = jnp.stack([dwg0, dwg1])
        gw["ffn_w_up"][l] = jnp.stack([dwu0, dwu1])
        gw["ffn_w_down"][l] = jnp.stack([dwd0, dwd1])
        pending = l
    grad_x = dx.reshape(nb, s, d)

    core = lax.axis_index("c").astype(jnp.int32).reshape(1)
    chip = (2 * lax.axis_index("x") + lax.axis_index("y")).astype(jnp.int32).reshape(1)
    last = [gw[n][0] for n in big]
    got = _scatter_pair(last, nls, name="scatter_grads_pair")
    chip_sums = [_pair_add(g, b, core, name="add_pair_" + n) for n, g, b in zip(big, last, got)]
    finish_layer(0, chip_sums, _scatter_chips(chip_sums, nls, name="scatter_grads_chips"), chip)
    grads, delta, new_m, new_v = ({n: updated[n][i] for n in big} for i in range(4))
    gw = {n: jnp.stack(v) for n, v in gw.items() if n not in big}
    gw["final_norm"] = g_final.reshape(d)

    small = ("ffn_norm", "mix_norm", "b_gate", "pool_w", "pool_scale", "dn_conv", "dn_A_log", "dn_dt_bias",
             "dn_out_norm", "final_norm")
    sp = _pad_rows(jnp.concatenate([_flat128(gw[n]) for n in small], axis=0))
    ssum = _sum_slots(_gather([sp], [0], name="gather_small_grads")[0], name="sum_small_grads")
    off = 0
    for n in small:
        r = -(-gw[n].size // LANE)
        g = ssum[off:off + r].reshape(-1)[:gw[n].size].reshape(gw[n].shape)
        off += r
        if n in ("ffn_norm", "dn_conv"):
            w = wts[n].shape[2]
            g = lax.dynamic_slice_in_dim(g, me * w, w, axis=2)
        grads[n] = g

    pk = lambda src: _pad_rows(jnp.concatenate([_flat128(src[n]) for n in small], axis=0))
    dl, nm, nv = _adamw(pk(wts), pk(grads), pk(mom), pk(var), name="adamw_small")
    off = 0
    for n in small:
        r = -(-wts[n].size // LANE)
        for dst, src in ((delta, dl), (new_m, nm), (new_v, nv)):
            dst[n] = src[off:off + r].reshape(-1)[:wts[n].size].reshape(wts[n].shape)
        off += r

    order = ("ffn_norm", "ffn_w_gate", "ffn_w_up", "ffn_w_down", "mix_norm", "w_in", "b_gate", "pool_w", "pool_scale",
             "dn_conv", "dn_A_log", "dn_dt_bias", "dn_out_norm", "w_branch", "w_out", "final_norm")
    return (loss, grad_x, *[grads[n] for n in order], *[delta[n] for n in order], *[new_m[n] for n in order],
            *[new_v[n] for n in order])
```

```python
import functools
import math

import jax
import jax.numpy as jnp
from jax import lax
from jax.experimental import pallas as pl
from jax.experimental.pallas import tpu as pltpu

F32, BF16 = jnp.float32, jnp.bfloat16
D_MODEL, D_FF, DEPTH = 1024, 2816, 4
BW = 512
HD = 128
NH = 4
DN_CHUNK = 64
EPS = 1e-6
N_DEV = 8
LANE = 128
CB_POOL, CB_DNQ, CB_DNK, CB_DNV, CB_DNZ, CB_SBQ, CB_SBK, CB_SBV = 0, 4, 8, 12, 16, 20, 24, 28
CB_GATE = 4
P_MAIN = 7168
AB_LO, AB_HI = 2560, 2568
ADAM_LR, ADAM_B1, ADAM_B2, ADAM_EPS, ADAM_WD, ADAM_STEP = 0.001, 0.9, 0.999, 1e-08, 0.01, 10
VMEM_LIMIT = 56 * 1024 * 1024
HIGHEST = lax.Precision.HIGHEST
NT_DIMS = (((1,), (1,)), ((), ()))
TN_DIMS = (((0,), (0,)), ((), ()))
NN_DIMS = (((1,), (0,)), ((), ()))


def _cp(dims=None, vmem=VMEM_LIMIT):
    return pltpu.CompilerParams(dimension_semantics=dims, vmem_limit_bytes=vmem)


def _pick(n, cands):
    for c in cands:
        if n % c == 0:
            return c
    return n


def _bdot(a, b, dims=NN_DIMS):
    return lax.dot_general(a.astype(BF16), b.astype(BF16), dims, preferred_element_type=F32)


def _hdot(a, b, dims=NN_DIMS):
    return lax.dot_general(a, b, dims, precision=lax.Precision.HIGH, preferred_element_type=F32)


def _split_dot(x, m01):
    hi = x.astype(BF16)
    lo = (x - hi.astype(F32)).astype(BF16)
    return (lax.dot_general(hi, m01, NN_DIMS, preferred_element_type=F32)
            + lax.dot_general(lo, m01, NN_DIMS, preferred_element_type=F32))


def _sigmoid(x):
    return 1.0 / (1.0 + jnp.exp(-x))


def _log_sigmoid(x):
    return jnp.minimum(x, 0.0) - jnp.log1p(jnp.exp(-jnp.abs(x)))


def _softplus(x):
    return jnp.maximum(x, 0.0) + jnp.log1p(jnp.exp(-jnp.abs(x)))


def _shift_down(x, k):
    r = lax.broadcasted_iota(jnp.int32, x.shape, 0)
    return jnp.where(r >= k, pltpu.roll(x, k, 0), 0.0)


def _shift_up(x, k):
    n = x.shape[0]
    r = lax.broadcasted_iota(jnp.int32, x.shape, 0)
    return jnp.where(r < n - k, pltpu.roll(x, n - k, 0), 0.0)


def _mm(a, b, *, ta=False, tb=False, out_dtype=F32, name):
    (kk, m) = a.shape if ta else a.shape[::-1]
    (k2, n) = b.shape[::-1] if tb else b.shape
    assert kk == k2, (a.shape, b.shape, ta, tb)
    bm = _pick(m, (1024, 512, 256, 128))
    bn = _pick(n, (1024, 1408, 512, 256, 128))
    bk = _pick(kk, (512, 256, 128))
    nk = kk // bk
    dims = (((0 if ta else 1,), (1 if tb else 0,)), ((), ()))

    def body(a_ref, b_ref, o_ref, acc_ref):
        k = pl.program_id(2)

        @pl.when(k == 0)
        def _():
            acc_ref[...] = jnp.zeros_like(acc_ref)

        acc_ref[...] += lax.dot_general(a_ref[...].astype(BF16), b_ref[...].astype(BF16), dims,
                                        preferred_element_type=F32)

        @pl.when(k == nk - 1)
        def _():
            o_ref[...] = acc_ref[...].astype(out_dtype)

    a_spec = (pl.BlockSpec((bk, bm), lambda i, j, k: (k, i)) if ta else pl.BlockSpec((bm, bk), lambda i, j, k: (i, k)))
    b_spec = (pl.BlockSpec((bn, bk), lambda i, j, k: (j, k)) if tb else pl.BlockSpec((bk, bn), lambda i, j, k: (k, j)))
    return pl.pallas_call(
        body, grid=(m // bm, n // bn, nk), in_specs=[a_spec, b_spec],
        out_specs=pl.BlockSpec((bm, bn), lambda i, j, k: (i, j)),
        out_shape=jax.ShapeDtypeStruct((m, n), out_dtype),
        scratch_shapes=[pltpu.VMEM((bm, bn), F32)],
        compiler_params=_cp(("parallel", "parallel", "arbitrary")), name=name)(a, b)


def _mm_slots(a, b, *, name):
    a3, b3 = a.ndim == 3, b.ndim == 3
    ns = a.shape[0] if a3 else b.shape[0]
    m, t = a.shape[-2:]
    n = b.shape[-1]
    bk = _pick(t, (512, 256, 128))
    nk = t // bk

    def body(a_ref, b_ref, o_ref, acc_ref):
        k = pl.program_id(0)

        @pl.when(k == 0)
        def _():
            acc_ref[...] = jnp.zeros_like(acc_ref)

        for s in range(ns):
            acc_ref[s] += _bdot(a_ref[s] if a3 else a_ref[...], b_ref[s] if b3 else b_ref[...])

        @pl.when(k == nk - 1)
        def _():
            o_ref[...] = acc_ref[...].astype(BF16)

    a_spec = pl.BlockSpec((ns, m, bk), lambda k: (0, 0, k)) if a3 else pl.BlockSpec((m, bk), lambda k: (0, k))
    b_spec = pl.BlockSpec((ns, bk, n), lambda k: (0, k, 0)) if b3 else pl.BlockSpec((bk, n), lambda k: (k, 0))
    return pl.pallas_call(
        body, grid=(nk,), in_specs=[a_spec, b_spec], out_specs=pl.BlockSpec((ns, m, n), lambda k: (0, 0, 0)),
        out_shape=jax.ShapeDtypeStruct((ns, m, n), BF16), scratch_shapes=[pltpu.VMEM((ns, m, n), F32)],
        compiler_params=_cp(("arbitrary",)), name=name)(a, b)


def _rms_stats(x):
    rstd = lax.rsqrt(jnp.mean(x * x, axis=-1, keepdims=True) + EPS)
    return x * rstd, rstd


def _rms_bwd_vals(dh, xhat, rstd, g):
    dxh = dh * g
    dx = rstd * (dxh - xhat * jnp.mean(dxh * xhat, axis=-1, keepdims=True))
    return dx, jnp.sum(dh * xhat, axis=0, keepdims=True)


def _rms_fwd(x, g, *, name):
    t, d = x.shape
    tm = _pick(t, (512, 256, 128))

    def body(x_ref, g_ref, h_ref):
        xhat, _ = _rms_stats(x_ref[...])
        h_ref[...] = (xhat * g_ref[...]).astype(BF16)

    return pl.pallas_call(
        body, grid=(t // tm,),
        in_specs=[pl.BlockSpec((tm, d), lambda i: (i, 0)), pl.BlockSpec((1, d), lambda i: (0, 0))],
        out_specs=pl.BlockSpec((tm, d), lambda i: (i, 0)), out_shape=jax.ShapeDtypeStruct((t, d), BF16),
        compiler_params=_cp(("parallel",)), name=name)(x, g.reshape(1, d))


def _rms_bwd(x, g, dh_a, dh_b, dres, *, name):
    t, d = x.shape
    tm = _pick(t, (512, 256, 128))

    def body(x_ref, g_ref, dha_ref, dhb_ref, dres_ref, dx_ref, dg_ref):
        xhat, rstd = _rms_stats(x_ref[...])
        dx, dg = _rms_bwd_vals(dha_ref[...] + dhb_ref[...], xhat, rstd, g_ref[...])
        dx_ref[...] = dres_ref[...] + dx

        @pl.when(pl.program_id(0) == 0)
        def _():
            dg_ref[...] = jnp.zeros_like(dg_ref)

        dg_ref[...] += dg

    row = pl.BlockSpec((tm, d), lambda i: (i, 0))
    vec = pl.BlockSpec((1, d), lambda i: (0, 0))
    return pl.pallas_call(
        body, grid=(t // tm,), in_specs=[row, vec, row, row, row], out_specs=[row, vec],
        out_shape=[jax.ShapeDtypeStruct((t, d), F32), jax.ShapeDtypeStruct((1, d), F32)],
        compiler_params=_cp(("arbitrary",)), name=name)(x, g.reshape(1, d), dh_a, dh_b, dres)


FFN_TM = 512


def _ffn_fwd(x, g, wg, wu, wd, *, name, gather=None):
    t, d = x.shape
    nf, _, fc = wg.shape
    tm = _pick(t, (FFN_TM, 256, 128))
    gx, gnl = gather if gather else ([], [])

    def body(x_ref, g_ref, wg_ref, wu_ref, wd_ref, o_ref, a_ref, b_ref, h_ref, acc_ref):
        j = pl.program_id(1)

        @pl.when(j == 0)
        def _():
            xhat, _ = _rms_stats(x_ref[...])
            h_ref[...] = (xhat * g_ref[...]).astype(BF16)
            acc_ref[...] = jnp.zeros_like(acc_ref)

        h = h_ref[...]
        a = _bdot(h, wg_ref[...])
        b = _bdot(h, wu_ref[...])
        a_ref[...] = a.astype(BF16)
        b_ref[...] = b.astype(BF16)
        s = a * _sigmoid(a) * b
        acc_ref[...] += _bdot(s, wd_ref[...])

        @pl.when(j == nf - 1)
        def _():
            o_ref[...] = x_ref[...] + 0.5 * acc_ref[...]

    row = pl.BlockSpec((tm, d), lambda i, j: (i, 0))
    grid = (t // tm, nf)
    exchange = (_gather_parts, len(gx), gnl) if gx else None
    res = pl.pallas_call(
        _with_exchange(body, 5, 3, 2, exchange, grid), grid=grid,
        in_specs=[row, pl.BlockSpec((1, d), lambda i, j: (0, 0)),
                  pl.BlockSpec((None, d, fc), lambda i, j: (j, 0, 0)), pl.BlockSpec((None, d, fc), lambda i, j: (j, 0, 0)),
                  pl.BlockSpec((None, fc, d), lambda i, j: (j, 0, 0))] + [HBM_SPEC] * len(gx),
        out_specs=[row, pl.BlockSpec((None, tm, fc), lambda i, j: (j, i, 0)),
                   pl.BlockSpec((None, tm, fc), lambda i, j: (j, i, 0))] + [HBM_SPEC] * len(gx),
        out_shape=[jax.ShapeDtypeStruct((t, d), F32), jax.ShapeDtypeStruct((nf, t, fc), BF16),
                   jax.ShapeDtypeStruct((nf, t, fc), BF16)] + _gather_shapes(gx, gnl),
        scratch_shapes=[pltpu.VMEM((tm, d), BF16), pltpu.VMEM((tm, d), F32)] + (_comm_sems(len(gx), 7) if gx else []),
        compiler_params=_cp(("arbitrary", "arbitrary")), name=name)(x, g.reshape(1, d), wg, wu, wd, *gx)
    return res[0], res[1], res[2], list(res[3:])


def _ffn_bwd(x, g, wg, wu, wd, a_pre, b_pre, dy, *, name):
    t, d = x.shape
    nf, _, fc = wg.shape
    tm = _pick(t, (FFN_TM, 256, 128))

    def body(x_ref, g_ref, wg_ref, wu_ref, wd_ref, a_ref, b_ref, dy_ref,
             dx_ref, dg_ref, ht_ref, dyh_ref, da_ref, db_ref, st_ref, acc_ref):
        i, j = pl.program_id(0), pl.program_id(1)

        @pl.when(j == 0)
        def _():
            xhat, _ = _rms_stats(x_ref[...])
            ht_ref[...] = (xhat * g_ref[...]).T.astype(BF16)
            dyh_ref[...] = (0.5 * dy_ref[...]).astype(BF16)
            acc_ref[...] = jnp.zeros_like(acc_ref)

        a = a_ref[...].astype(F32)
        b = b_ref[...].astype(F32)
        sg = _sigmoid(a)
        silu = a * sg
        st_ref[...] = (silu * b).T.astype(BF16)
        ds = _bdot(dyh_ref[...], wd_ref[...], NT_DIMS)
        da = (ds * b * (sg * (1.0 + a * (1.0 - sg)))).astype(BF16)
        db = (ds * silu).astype(BF16)
        da_ref[...] = da
        db_ref[...] = db
        acc_ref[...] += _bdot(da, wg_ref[...], NT_DIMS) + _bdot(db, wu_ref[...], NT_DIMS)

        @pl.when((i == 0) & (j == 0))
        def _():
            dg_ref[...] = jnp.zeros_like(dg_ref)

        @pl.when(j == nf - 1)
        def _():
            xhat, rstd = _rms_stats(x_ref[...])
            dx, dg = _rms_bwd_vals(acc_ref[...], xhat, rstd, g_ref[...])
            dx_ref[...] = dy_ref[...] + dx
            dg_ref[...] += dg

    row = pl.BlockSpec((tm, d), lambda i, j: (i, 0))
    vec = pl.BlockSpec((1, d), lambda i, j: (0, 0))
    fblk = pl.BlockSpec((None, tm, fc), lambda i, j: (j, i, 0))
    return pl.pallas_call(
        body, grid=(t // tm, nf),
        in_specs=[row, vec, pl.BlockSpec((None, d, fc), lambda i, j: (j, 0, 0)),
                  pl.BlockSpec((None, d, fc), lambda i, j: (j, 0, 0)), pl.BlockSpec((None, fc, d), lambda i, j: (j, 0, 0)),
                  fblk, fblk, row],
        out_specs=[row, vec, pl.BlockSpec((d, tm), lambda i, j: (0, i)), row, fblk, fblk,
                   pl.BlockSpec((None, fc, tm), lambda i, j: (j, 0, i))],
        out_shape=[jax.ShapeDtypeStruct((t, d), F32), jax.ShapeDtypeStruct((1, d), F32),
                   jax.ShapeDtypeStruct((d, t), BF16), jax.ShapeDtypeStruct((t, d), BF16),
                   jax.ShapeDtypeStruct((nf, t, fc), BF16), jax.ShapeDtypeStruct((nf, t, fc), BF16),
                   jax.ShapeDtypeStruct((nf, fc, t), BF16)],
        scratch_shapes=[pltpu.VMEM((tm, d), F32)],
        compiler_params=_cp(("arbitrary", "arbitrary")), name=name)(x, g.reshape(1, d), wg, wu, wd, a_pre, b_pre, dy)


def _pool_core(u, grp):
    s = u.shape[0]
    w2 = u + _shift_down(u, 1)
    w4 = w2 + _shift_down(w2, 2)
    w8 = w4 + _shift_down(w4, 4)
    w16 = w8 + _shift_down(w8, 8)
    wsum = jnp.where(grp == 0, w2, jnp.where(grp == 1, w4, jnp.where(grp == 2, w8, w16)))
    win = jnp.left_shift(2, grp).astype(F32)
    t1 = (lax.broadcasted_iota(jnp.int32, (s, 1), 0) + 1).astype(F32)
    inv = 1.0 / jnp.minimum(t1, win)
    return wsum * inv - u, inv


def _pool_fwd(proj, pool_w, pool_scale, nb, s, *, name):
    def body(u_ref, w_ref, sc_ref, y_ref):
        pooled, _ = _pool_core(u_ref[...], pl.program_id(0))
        y_ref[...] = _bdot(pooled, w_ref[...]) * sc_ref[...]

    return pl.pallas_call(
        body, grid=(NH, nb),
        in_specs=[pl.BlockSpec((s, HD), lambda g, b: (b, CB_POOL + g)),
                  pl.BlockSpec((None, HD, HD), lambda g, b: (g, 0, 0)), pl.BlockSpec((1, HD), lambda g, b: (0, g))],
        out_specs=pl.BlockSpec((s, HD), lambda g, b: (b, g)),
        out_shape=jax.ShapeDtypeStruct((nb * s, BW), F32),
        compiler_params=_cp(("parallel", "parallel")), name=name)(proj, pool_w, pool_scale)


def _pool_bwd(proj, pool_w, pool_scale, dy, nb, s, *, name):
    def body(u_ref, w_ref, sc_ref, dy_ref, du_ref, dw_ref, dsc_ref):
        grp, b = pl.program_id(0), pl.program_id(1)
        pooled, inv = _pool_core(u_ref[...], grp)
        mixed = _bdot(pooled, w_ref[...])
        dy = dy_ref[...]
        dmixed = dy * sc_ref[...]
        dpooled = _bdot(dmixed, w_ref[...], NT_DIMS)
        r = dpooled * inv
        v2 = r + _shift_up(r, 1)
        v4 = v2 + _shift_up(v2, 2)
        v8 = v4 + _shift_up(v4, 4)
        v16 = v8 + _shift_up(v8, 8)
        vsum = jnp.where(grp == 0, v2, jnp.where(grp == 1, v4, jnp.where(grp == 2, v8, v16)))
        du_ref[...] = vsum - dpooled

        @pl.when(b == 0)
        def _():
            dw_ref[...] = jnp.zeros_like(dw_ref)
            dsc_ref[...] = jnp.zeros_like(dsc_ref)

        dw_ref[...] += _bdot(pooled, dmixed, TN_DIMS)
        dsc_ref[...] += jnp.sum(dy * mixed, axis=0, keepdims=True)

    return pl.pallas_call(
        body, grid=(NH, nb),
        in_specs=[pl.BlockSpec((s, HD), lambda g, b: (b, CB_POOL + g)),
                  pl.BlockSpec((None, HD, HD), lambda g, b: (g, 0, 0)), pl.BlockSpec((1, HD), lambda g, b: (0, g)),
                  pl.BlockSpec((s, HD), lambda g, b: (b, g))],
        out_specs=[pl.BlockSpec((s, HD), lambda g, b: (b, g)), pl.BlockSpec((None, HD, HD), lambda g, b: (g, 0, 0)),
                   pl.BlockSpec((1, HD), lambda g, b: (0, g))],
        out_shape=[jax.ShapeDtypeStruct((nb * s, BW), F32), jax.ShapeDtypeStruct((NH, HD, HD), F32),
                   jax.ShapeDtypeStruct((1, BW), F32)],
        compiler_params=_cp(("arbitrary", "arbitrary")), name=name)(proj, pool_w, pool_scale, dy)


SB_BLK = 128


SB_G = 4
SB_KG = SB_G * SB_BLK
SB_Q = 2 * SB_BLK


def _sb_block(qb, kg, q0, k0, diagonal):
    z = _bdot(qb, kg, NT_DIMS) * (HD ** -0.5)
    lsz = _log_sigmoid(z)
    if not diagonal:
        return lsz, lsz - z, None
    row = lax.broadcasted_iota(jnp.int32, z.shape, 0) + q0
    col = lax.broadcasted_iota(jnp.int32, z.shape, 1) + k0
    causal = col < row
    return lsz, jnp.where(causal, lsz - z, 0.0), causal


def _keep(causal, x):
    return x if causal is None else jnp.where(causal, x, 0.0)


def _sub(x, m):
    return x[:, m * SB_BLK:(m + 1) * SB_BLK]


def _sb_tails(lnm, after, ct):
    hi = lnm.astype(BF16)
    lo = (lnm - hi.astype(F32)).astype(BF16)
    tails = [None] * SB_G
    for m in reversed(range(SB_G)):
        tails[m] = (lax.dot_general(_sub(hi, m), after, NN_DIMS, preferred_element_type=F32)
                    + lax.dot_general(_sub(lo, m), after, NN_DIMS, preferred_element_type=F32)) + ct
        ct = ct + jnp.sum(_sub(lnm, m), axis=1, keepdims=True)
    ones = jnp.ones((8, lnm.shape[1]), BF16)
    rows = (lax.dot_general(ones, hi, NT_DIMS, preferred_element_type=F32)
            + lax.dot_general(ones, lo, NT_DIMS, preferred_element_type=F32))
    return jnp.concatenate(tails, axis=1), rows, ct


def _tri01(lower):
    r = lax.broadcasted_iota(jnp.int32, (SB_BLK, SB_BLK), 0)
    c = lax.broadcasted_iota(jnp.int32, (SB_BLK, SB_BLK), 1)
    return jnp.where((r < c) if lower else (r > c), 1.0, 0.0).astype(BF16)


def _split3(x):
    hi = x.astype(BF16)
    mid = (x - hi.astype(F32)).astype(BF16)
    lo = (x - hi.astype(F32) - mid.astype(F32)).astype(BF16)
    return hi, mid, lo


def _rows_to_cols(rows):
    eighth = jnp.full((8, LANE), 0.125, BF16)
    return sum(lax.dot_general(p, eighth, TN_DIMS, preferred_element_type=F32) for p in _split3(rows))


def _sb_fwd(proj, nb, s, *, name):
    nq = s // SB_Q
    ng = s // SB_KG

    def body(q_ref, k_ref, v_ref, o_ref, ctr_ref):
        after = _tri01(False)

        def qblock(i, _):
            q0 = pl.multiple_of(i * SB_Q, SB_Q)
            qb = q_ref[pl.ds(q0, SB_Q), :]

            def kgroup(g, carry, diagonal):
                acc, ct, ctr = carry
                k0 = pl.multiple_of(g * SB_KG, SB_KG)
                lsz, lnm, causal = _sb_block(qb, k_ref[pl.ds(k0, SB_KG), :], q0, k0, diagonal)
                ctr_ref[i * ng + g] = ctr
                tail, rows, ct = _sb_tails(lnm, after, ct)
                w = _keep(causal, jnp.exp(lsz + tail))
                return acc + _bdot(w, v_ref[pl.ds(k0, SB_KG), :]), ct, ctr + rows

            gd = (i * SB_Q) // SB_KG
            carry = kgroup(gd, (jnp.zeros((SB_Q, HD), F32), jnp.zeros((SB_Q, 1), F32), jnp.zeros((8, SB_Q), F32)), True)
            acc, _, _ = lax.fori_loop(0, gd, lambda jj, c: kgroup(gd - 1 - jj, c, False), carry)
            o_ref[pl.ds(q0, SB_Q), :] = acc
            return 0

        lax.fori_loop(0, nq, qblock, 0)

    def col(cb):
        return pl.BlockSpec((s, HD), lambda b, h: (b, cb + h))

    return pl.pallas_call(
        body, grid=(nb, NH), in_specs=[col(CB_SBQ), col(CB_SBK), col(CB_SBV)],
        out_specs=[pl.BlockSpec((s, HD), lambda b, h: (b, h)),
                   pl.BlockSpec((None, None, nq * ng, 8, SB_Q), lambda b, h: (b, h, 0, 0, 0))],
        out_shape=[jax.ShapeDtypeStruct((nb * s, BW), F32), jax.ShapeDtypeStruct((nb, NH, nq * ng, 8, SB_Q), F32)],
        compiler_params=_cp(("parallel", "parallel")), name=name)(proj, proj, proj)


def _sb_bwd(proj, ctr, dy, nb, s, *, name):
    nq = s // SB_Q
    ng = s // SB_KG
    scale = HD ** -0.5

    def body(q_ref, k_ref, v_ref, ctr_ref, do_ref, dq_ref, dk_ref, dv_ref):
        after = _tri01(False)
        before = _tri01(True)
        dk_ref[...] = jnp.zeros_like(dk_ref)
        dv_ref[...] = jnp.zeros_like(dv_ref)

        def qblock(i, _):
            q0 = pl.multiple_of(i * SB_Q, SB_Q)
            qb = q_ref[pl.ds(q0, SB_Q), :]
            dob = do_ref[pl.ds(q0, SB_Q), :]

            def kgroup(g, carry, diagonal):
                dq, ce = carry
                k0 = pl.multiple_of(g * SB_KG, SB_KG)
                kg = k_ref[pl.ds(k0, SB_KG), :]
                vg = v_ref[pl.ds(k0, SB_KG), :]
                lsz, lnm, causal = _sb_block(qb, kg, q0, k0, diagonal)
                tail, _, _ = _sb_tails(lnm, after, _rows_to_cols(ctr_ref[i * ng + g])[:, 0:1])
                w = _keep(causal, jnp.exp(lsz + tail))
                e = _bdot(dob, vg, NT_DIMS) * w
                pres = []
                for m in range(SB_G):
                    pres.append(_split_dot(_sub(e, m), before) + ce)
                    ce = ce + jnp.sum(_sub(e, m), axis=1, keepdims=True)
                sig = jnp.exp(lsz)
                dz = _keep(causal, e * (1.0 - sig) - jnp.concatenate(pres, axis=1) * sig) * scale
                dk_ref[pl.ds(k0, SB_KG), :] += _bdot(dz, qb, TN_DIMS)
                dv_ref[pl.ds(k0, SB_KG), :] += _bdot(w, dob, TN_DIMS)
                return dq + _bdot(dz, kg), ce

            gd = (i * SB_Q) // SB_KG
            carry = lax.fori_loop(0, gd, lambda g, c: kgroup(g, c, False),
                                  (jnp.zeros((SB_Q, HD), F32), jnp.zeros((SB_Q, 1), F32)))
            dq, _ = kgroup(gd, carry, True)
            dq_ref[pl.ds(q0, SB_Q), :] = dq
            return 0

        lax.fori_loop(0, nq, qblock, 0)

    def col(cb):
        return pl.BlockSpec((s, HD), lambda b, h: (b, cb + h))

    out = pl.BlockSpec((s, HD), lambda b, h: (b, h))
    sds = jax.ShapeDtypeStruct((nb * s, BW), F32)
    return pl.pallas_call(
        body, grid=(nb, NH),
        in_specs=[col(CB_SBQ), col(CB_SBK), col(CB_SBV),
                  pl.BlockSpec((None, None, nq * ng, 8, SB_Q), lambda b, h: (b, h, 0, 0, 0)), out],
        out_specs=[out, out, out], out_shape=[sds, sds, sds],
        compiler_params=_cp(("parallel", "parallel")), name=name)(proj, proj, proj, ctr, dy)


def _make_cdot(dims, dims_da, dims_db, swap_a=False, swap_b=False):
    @jax.custom_vjp
    def f(a, b):
        return _bdot(a, b, dims)

    def fwd(a, b):
        return _bdot(a, b, dims), (a, b)

    def bwd(res, g):
        a, b = res
        da = _bdot(b, g, dims_da) if swap_a else _bdot(g, b, dims_da)
        db = _bdot(g, a, dims_db) if swap_b else _bdot(a, g, dims_db)
        return da, db

    f.defvjp(fwd, bwd)
    return f


_cdot = _make_cdot(NN_DIMS, NT_DIMS, TN_DIMS)
_cdot_nt = _make_cdot(NT_DIMS, NN_DIMS, TN_DIMS, swap_b=True)
_cdot_tn = _make_cdot(TN_DIMS, NT_DIMS, NN_DIMS, swap_a=True)


DN_SUPER = 4 * DN_CHUNK


@jax.custom_vjp
def _unit_lower_inverse(lmat):
    n = lmat.shape[0]
    steps = int(math.log2(DN_CHUNK))
    eye = jnp.where(lax.broadcasted_iota(jnp.int32, (n, n), 0) == lax.broadcasted_iota(jnp.int32, (n, n), 1), 1.0, 0.0)
    inv = eye - lmat
    pw = _hdot(lmat, lmat)
    for it in range(steps - 1):
        inv = inv + _hdot(inv, pw)
        if it < steps - 2:
            pw = _hdot(pw, pw)
    return inv


def _unit_lower_inverse_fwd(lmat):
    inv = _unit_lower_inverse(lmat)
    return inv, inv


def _unit_lower_inverse_bwd(inv, g):
    return (-_hdot(_hdot(inv, g, TN_DIMS), inv, NT_DIMS),)


_unit_lower_inverse.defvjp(_unit_lower_inverse_fwd, _unit_lower_inverse_bwd)


def _dn_local(q, k, v, bb, gb):
    n = q.shape[0]
    r = lax.broadcasted_iota(jnp.int32, (n, n), 0)
    cc = lax.broadcasted_iota(jnp.int32, (n, n), 1)
    shift = int(math.log2(DN_CHUNK))
    same = lax.shift_right_logical(r, shift) == lax.shift_right_logical(cc, shift)
    incl = jnp.where(same, jnp.where(r >= cc, 1.0, 0.0), 0.0)
    strict = jnp.where(same, jnp.where(r > cc, 1.0, 0.0), 0.0)
    gc = _hdot(incl, gb)
    gc_row = _hdot(jnp.full((n, HD), 1.0 / HD, F32), gc, NT_DIMS)
    diff = jnp.concatenate([gc] * (n // HD), axis=1) - gc_row
    decay = incl * jnp.exp(diff * incl)
    kb = k * bb
    lmat = _cdot_nt(kb, k) * (strict * decay)
    egc = jnp.exp(gc)
    inv = _unit_lower_inverse(lmat)
    u = _hdot(inv, v * bb)
    w = _hdot(inv, kb * egc)
    attn = _cdot_nt(q, k) * decay
    gl = _hdot(jnp.where(same, 1.0, 0.0), gb)
    return u, w, attn, q * egc, k * jnp.exp(gl - gc), jnp.exp(gl)


def _attn_pairs(attn):
    return jnp.concatenate([attn[:HD, :HD], attn[HD:, HD:]], axis=0)


def _attn_unpairs(a):
    z = jnp.zeros((HD, HD), F32)
    return jnp.concatenate([jnp.concatenate([a[:HD], z], axis=1), jnp.concatenate([z, a[HD:]], axis=1)], axis=0)


def _dn_step(u, w, a, qd, kd, cdrows, state, odd):
    v_new = u - _cdot(w, state)
    z = jnp.zeros_like(v_new)
    o = _cdot(qd, state) + _cdot(a, jnp.concatenate([z, v_new] if odd else [v_new, z], axis=0))
    return o, state * jnp.mean(cdrows, axis=0, keepdims=True) + _cdot_tn(kd, v_new)


def _dn_local_pass(fn, s, ins, outs):
    def step(it, _):
        sl = pl.ds(pl.multiple_of(it * DN_SUPER, DN_SUPER), DN_SUPER)
        res = fn(*[ref[sl, :] for ref in ins])
        for ref, val in zip(outs, res):
            ref[sl, :] = val
        return 0

    lax.fori_loop(0, s // DN_SUPER, step, 0)


def _lane_pick(row, idx):
    lane = lax.broadcasted_iota(jnp.int32, row.shape, 1)
    return jnp.sum(jnp.where(lane == idx, row, 0.0), axis=1, keepdims=True)


def _col_pick(x, idx):
    lane = lax.broadcasted_iota(jnp.int32, x.shape, 1)
    return jnp.sum(jnp.where(lane == idx, x, 0.0), axis=1, keepdims=True)


def _conv_silu(x, w):
    xc = (w[3:4, :] * x + w[2:3, :] * _shift_down(x, 1) + w[1:2, :] * _shift_down(x, 2)
          + w[0:1, :] * _shift_down(x, 3))
    return xc * _sigmoid(xc), xc


def _conv_silu_bwd(x, w, xc, dxs, dw_ref):
    sg = _sigmoid(xc)
    dxc = dxs * (sg * (1.0 + xc * (1.0 - sg)))
    dx = (w[3:4, :] * dxc + w[2:3, :] * _shift_up(dxc, 1) + w[1:2, :] * _shift_up(dxc, 2)
          + w[0:1, :] * _shift_up(dxc, 3))
    dw_ref[3:4, :] += jnp.sum(dxc * x, axis=0, keepdims=True)
    dw_ref[2:3, :] += jnp.sum(dxc * _shift_down(x, 1), axis=0, keepdims=True)
    dw_ref[1:2, :] += jnp.sum(dxc * _shift_down(x, 2), axis=0, keepdims=True)
    dw_ref[0:1, :] += jnp.sum(dxc * _shift_down(x, 3), axis=0, keepdims=True)
    return dx


def _dn_prep(qr_ref, kr_ref, vr_ref, ab_ref, cq_ref, ck_ref, cv_ref, par_ref, head):
    qs, qc = _conv_silu(qr_ref[...], cq_ref[...])
    ks, kc = _conv_silu(kr_ref[...], ck_ref[...])
    vs, vc = _conv_silu(vr_ref[...], cv_ref[...])
    rq = lax.rsqrt(jnp.sum(qs * qs, axis=1, keepdims=True) + EPS)
    rk = lax.rsqrt(jnp.sum(ks * ks, axis=1, keepdims=True) + EPS)
    ab = ab_ref[...]
    a_in = _col_pick(ab, head) + _lane_pick(par_ref[1:2, :], head)
    beta = _sigmoid(_col_pick(ab, NH + head))
    neg_ea = -jnp.exp(_lane_pick(par_ref[0:1, :], head))
    g = neg_ea * _softplus(a_in)
    return dict(q=qs * rq * (HD ** -0.5), k=ks * rk, v=vs, beta=beta, g=g, qs=qs, ks=ks, qc=qc, kc=kc, vc=vc,
                rq=rq, rk=rk, a_in=a_in, neg_ea=neg_ea)


ONE_BUF = pl.Buffered(1)
DN_BWD_VMEM = 62 * 1024 * 1024


def _dn_specs(nb, s):
    def col(cb):
        return pl.BlockSpec((s, HD), lambda h, b: (b, cb + h), pipeline_mode=ONE_BUF)

    def conv(cb):
        return pl.BlockSpec((DN_CONV_W, HD), lambda h, b: (0, cb + h))

    return col, conv


DN_CONV_W = 4


def _with_exchange(body, n_in, n_out, n_scratch, exchange, grid):
    if exchange is None:
        return body
    parts_fn, n, nls = exchange

    def wrapped(*refs):
        ins, xs = refs[:n_in], refs[n_in:n_in + n]
        outs, os = refs[n_in + n:n_in + n + n_out], refs[n_in + n + n_out:n_in + 2 * n + n_out]
        rest = refs[n_in + 2 * n + n_out:]
        scratch, sems = rest[:n_scratch], rest[n_scratch:]
        pos = [pl.program_id(k) for k in range(len(grid))]
        first = functools.reduce(jnp.logical_and, [p == 0 for p in pos])
        last = functools.reduce(jnp.logical_and, [p == g - 1 for p, g in zip(pos, grid)])
        start, finish = parts_fn(xs, os, nls, *sems)
        pl.when(first)(start)
        body(*ins, *outs, *scratch)
        pl.when(last)(finish)

    return wrapped


def _dn_fwd(proj, ab, conv_w, par, gain, nb, s, *, name, gather=None):
    nc = s // DN_CHUNK
    col, conv = _dn_specs(nb, s)
    gx, gnl = gather if gather else ([], [])

    def body(qr_ref, kr_ref, vr_ref, z_ref, ab_ref, cq_ref, ck_ref, cv_ref, par_ref, gain_ref,
             y_ref, o_ref, st_ref, u_ref, w_ref, at_ref, qd_ref, kd_ref, cd_ref, q_s, k_s, v_s, bb_s, gb_s):
        p = _dn_prep(qr_ref, kr_ref, vr_ref, ab_ref, cq_ref, ck_ref, cv_ref, par_ref, pl.program_id(0))
        q_s[...], k_s[...], v_s[...] = p["q"], p["k"], p["v"]
        bb_s[...] = jnp.broadcast_to(p["beta"], (s, HD))
        gb_s[...] = jnp.broadcast_to(p["g"], (s, HD))
        def local(*args):
            u, w, attn, qd, kd, cd = _dn_local(*args)
            return u, w, _attn_pairs(attn), qd, kd, cd

        _dn_local_pass(local, s, [q_s, k_s, v_s, bb_s, gb_s], [u_ref, w_ref, at_ref, qd_ref, kd_ref, cd_ref])

        def chunk_pair(pi, state):
            for odd in (0, 1):
                ci = 2 * pi + odd
                sl = pl.ds(pl.multiple_of(ci * DN_CHUNK, DN_CHUNK), DN_CHUNK)
                st_ref[ci] = state
                o, state = _dn_step(u_ref[sl, :], w_ref[sl, :], at_ref[sl, :], qd_ref[sl, :], kd_ref[sl, :],
                                    cd_ref[sl, :], state, odd)
                o_ref[sl, :] = o
            return state

        lax.fori_loop(0, nc // 2, chunk_pair, jnp.zeros((HD, HD), F32))
        o = o_ref[...]
        z = z_ref[...]
        on = o * lax.rsqrt(jnp.mean(o * o, axis=1, keepdims=True) + EPS) * gain_ref[...]
        y_ref[...] = on * (z * _sigmoid(z))

    out = pl.BlockSpec((s, HD), lambda h, b: (b, h))
    sds = jax.ShapeDtypeStruct((nb * s, BW), F32)
    exchange = (_gather_parts, len(gx), gnl) if gx else None
    res = pl.pallas_call(
        _with_exchange(body, 10, 9, 5, exchange, (NH, nb)), grid=(NH, nb),
        in_specs=[col(CB_DNQ), col(CB_DNK), col(CB_DNV), col(CB_DNZ), pl.BlockSpec((s, LANE), lambda h, b: (b, 0)),
                  conv(0), conv(NH), conv(2 * NH), pl.BlockSpec((8, LANE), lambda h, b: (0, 0)),
                  pl.BlockSpec((1, HD), lambda h, b: (0, 0))] + [HBM_SPEC] * len(gx),
        out_specs=[out, out, pl.BlockSpec((None, None, nc, HD, HD), lambda h, b: (b, h, 0, 0, 0))] + [out] * 6
        + [HBM_SPEC] * len(gx),
        out_shape=[sds, sds, jax.ShapeDtypeStruct((nb, NH, nc, HD, HD), F32)] + [sds] * 6 + _gather_shapes(gx, gnl),
        scratch_shapes=[pltpu.VMEM((s, HD), F32)] * 5 + (_comm_sems(len(gx), 7) if gx else []),
        compiler_params=_cp(("arbitrary", "arbitrary")), name=name)(
            proj, proj, proj, proj, ab, conv_w, conv_w, conv_w, par, gain, *gx)
    return res[0], res[1], res[2], list(res[3:9]), list(res[9:])


def _dn_bwd(proj, ab, conv_w, par, gain, o_pre, states, local, dy, nb, s, *, name, scatter=None):
    nc = s // DN_CHUNK
    col, conv = _dn_specs(nb, s)
    gx, gnl = scatter if scatter else ([], [])

    def body(qr_ref, kr_ref, vr_ref, z_ref, ab_ref, cq_ref, ck_ref, cv_ref, par_ref, gain_ref, o_ref, st_ref, dy_ref,
             u_hbm, w_hbm, at_hbm, qd_hbm, kd_hbm, cd_hbm,
             dqr_ref, dkr_ref, dvr_ref, dz_ref, dab_ref, dcq_ref, dck_ref, dcv_ref, dpar_ref, dgain_ref,
             q_s, k_s, v_s, bb_s, gb_s, do_s, u_s, w_s, qd_s, kd_s, at_s, cd_s, load_sems):
        head, b = pl.program_id(0), pl.program_id(1)
        local_refs = [u_s, w_s, at_s, qd_s, kd_s, cd_s]
        loads = [pltpu.make_async_copy(src.at[pl.ds(pl.multiple_of(b * s, s), s), pl.ds(pl.multiple_of(head * HD, HD), HD)],
                                       dst, load_sems.at[i])
                 for i, (src, dst) in enumerate(zip((u_hbm, w_hbm, at_hbm, qd_hbm, kd_hbm, cd_hbm), local_refs))]
        for cp in loads:
            cp.start()
        p = _dn_prep(qr_ref, kr_ref, vr_ref, ab_ref, cq_ref, ck_ref, cv_ref, par_ref, head)
        q_s[...], k_s[...], v_s[...] = p["q"], p["k"], p["v"]
        bb_s[...] = jnp.broadcast_to(p["beta"], (s, HD))
        gb_s[...] = jnp.broadcast_to(p["g"], (s, HD))

        @pl.when(b == 0)
        def _():
            for ref in (dcq_ref, dck_ref, dcv_ref, dpar_ref):
                ref[...] = jnp.zeros_like(ref)

        @pl.when((b == 0) & (head == 0))
        def _():
            dgain_ref[...] = jnp.zeros_like(dgain_ref)

        o, z, dy = o_ref[...], z_ref[...], dy_ref[...]
        rstd = lax.rsqrt(jnp.mean(o * o, axis=1, keepdims=True) + EPS)
        ohat = o * rstd
        sgz = _sigmoid(z)
        dz_ref[...] = dy * (ohat * gain_ref[...]) * (sgz * (1.0 + z * (1.0 - sgz)))
        don = dy * (z * sgz)
        dgain_ref[...] += jnp.sum(don * ohat, axis=0, keepdims=True)
        dxh = don * gain_ref[...]
        do_s[...] = rstd * (dxh - ohat * jnp.mean(dxh * ohat, axis=1, keepdims=True))

        for cp in loads:
            cp.wait()

        def chunk_pair(pr, dstate):
            for odd in (1, 0):
                ci = nc - 1 - 2 * pr - (1 - odd)
                sl = pl.ds(pl.multiple_of(ci * DN_CHUNK, DN_CHUNK), DN_CHUNK)
                _, vjp = jax.vjp(functools.partial(_dn_step, odd=odd), u_s[sl, :], w_s[sl, :], at_s[sl, :],
                                 qd_s[sl, :], kd_s[sl, :], cd_s[sl, :], st_ref[ci])
                du, dw, dat, dqd, dkd, dcd, dstate = vjp((do_s[sl, :], dstate))
                u_s[sl, :], w_s[sl, :], at_s[sl, :], qd_s[sl, :], kd_s[sl, :], cd_s[sl, :] = du, dw, dat, dqd, dkd, dcd
            return dstate

        lax.fori_loop(0, nc // 2, chunk_pair, jnp.zeros((HD, HD), F32))

        def local_bwd(q, k, v, bb, gb, du, dw, dat, dqd, dkd, dcd):
            _, vjp = jax.vjp(_dn_local, q, k, v, bb, gb)
            dq, dk, dv, dbb, dgb = vjp((du, dw, _attn_unpairs(dat), dqd, dkd, dcd))
            return (dq, dk, dv, jnp.broadcast_to(jnp.sum(dbb, axis=1, keepdims=True), (DN_SUPER, HD)),
                    jnp.broadcast_to(jnp.sum(dgb, axis=1, keepdims=True), (DN_SUPER, HD)))

        _dn_local_pass(local_bwd, s, [q_s, k_s, v_s, bb_s, gb_s] + local_refs, [q_s, k_s, v_s, bb_s, gb_s])

        dq, dk, dv = q_s[...], k_s[...], v_s[...]
        qs, ks, rq, rk = p["qs"], p["ks"], p["rq"], p["rk"]
        dqs = (HD ** -0.5) * (rq * dq - qs * (rq * rq * rq) * jnp.sum(dq * qs, axis=1, keepdims=True))
        dks = rk * dk - ks * (rk * rk * rk) * jnp.sum(dk * ks, axis=1, keepdims=True)
        dqr_ref[...] = _conv_silu_bwd(qr_ref[...], cq_ref[...], p["qc"], dqs, dcq_ref)
        dkr_ref[...] = _conv_silu_bwd(kr_ref[...], ck_ref[...], p["kc"], dks, dck_ref)
        dvr_ref[...] = _conv_silu_bwd(vr_ref[...], cv_ref[...], p["vc"], dv, dcv_ref)

        dbeta, dg = bb_s[:, 0:1], gb_s[:, 0:1]
        beta = p["beta"]
        db_logit = dbeta * beta * (1.0 - beta)
        da = dg * p["neg_ea"] * _sigmoid(p["a_in"])
        lane = lax.broadcasted_iota(jnp.int32, (s, LANE), 1)
        dab_ref[...] = jnp.where(lane == head, da, 0.0) + jnp.where(lane == NH + head, db_logit, 0.0)
        dpar_ref[0:1, :] += jnp.broadcast_to(jnp.sum(dg * p["g"], axis=0, keepdims=True), (1, LANE))
        dpar_ref[1:2, :] += jnp.broadcast_to(jnp.sum(da, axis=0, keepdims=True), (1, LANE))

    out = pl.BlockSpec((s, HD), lambda h, b: (b, h))
    in_blk = pl.BlockSpec((s, HD), lambda h, b: (b, h), pipeline_mode=ONE_BUF)
    cblk = pl.BlockSpec((DN_CONV_W, HD), lambda h, b: (0, h))
    sds = jax.ShapeDtypeStruct((nb * s, BW), F32)
    csds = jax.ShapeDtypeStruct((DN_CONV_W, BW), F32)
    exchange = (_all_to_all_parts, len(gx), gnl) if gx else None
    res = pl.pallas_call(
        _with_exchange(body, 19, 10, 13, exchange, (NH, nb)), grid=(NH, nb),
        in_specs=[col(CB_DNQ), col(CB_DNK), col(CB_DNV), col(CB_DNZ),
                  pl.BlockSpec((s, LANE), lambda h, b: (b, 0), pipeline_mode=ONE_BUF),
                  conv(0), conv(NH), conv(2 * NH), pl.BlockSpec((8, LANE), lambda h, b: (0, 0)),
                  pl.BlockSpec((1, HD), lambda h, b: (0, 0)), in_blk,
                  pl.BlockSpec((None, None, nc, HD, HD), lambda h, b: (b, h, 0, 0, 0), pipeline_mode=ONE_BUF), in_blk]
        + [HBM_SPEC] * (6 + len(gx)),
        out_specs=[out, out, out, out, pl.BlockSpec((None, s, LANE), lambda h, b: (h, b, 0)), cblk, cblk, cblk,
                   pl.BlockSpec((None, 8, LANE), lambda h, b: (h, 0, 0)), pl.BlockSpec((1, HD), lambda h, b: (0, 0))]
        + [HBM_SPEC] * len(gx),
        out_shape=[sds, sds, sds, sds, jax.ShapeDtypeStruct((NH, nb * s, LANE), F32), csds, csds, csds,
                   jax.ShapeDtypeStruct((NH, 8, LANE), F32), jax.ShapeDtypeStruct((1, HD), F32)]
        + _all_to_all_shapes(gx, gnl),
        scratch_shapes=[pltpu.VMEM((s, HD), F32)] * 12 + [pltpu.SemaphoreType.DMA((6,))]
        + (_comm_sems(len(gx), 7) if gx else []),
        compiler_params=_cp(("arbitrary", "arbitrary"), DN_BWD_VMEM), name=name)(
            proj, proj, proj, proj, ab, conv_w, conv_w, conv_w, par, gain, o_pre, states, dy, *local, *gx)
    return tuple(res[:10]) + (list(res[10:]),)


def _sum_heads(x, *, name):
    nh, t, c = x.shape
    tm = _pick(t, (1024, 512, 256, 128))

    def body(x_ref, o_ref):
        o_ref[...] = (x_ref[0] + x_ref[1] + x_ref[2] + x_ref[3]).astype(BF16)

    return pl.pallas_call(
        body, grid=(t // tm,), in_specs=[pl.BlockSpec((nh, tm, c), lambda i: (0, i, 0))],
        out_specs=pl.BlockSpec((tm, c), lambda i: (i, 0)), out_shape=jax.ShapeDtypeStruct((t, c), BF16),
        compiler_params=_cp(("parallel",)), name=name)(x)


MERGE_TM = 256


def _merge_fwd(x, proj, yp, yd, ys, b_gate, wb, wo, *, name):
    t, d = x.shape
    tm = _pick(t, (MERGE_TM, 128))

    def body(x_ref, g0_ref, g1_ref, g2_ref, yp_ref, yd_ref, ys_ref, bg_ref, wb_ref, wo_ref, o_ref):
        merged = jnp.zeros((tm, d), F32)
        for n, (g_ref, y_ref) in enumerate(((g0_ref, yp_ref), (g1_ref, yd_ref), (g2_ref, ys_ref))):
            gate = _sigmoid(g_ref[...] + bg_ref[:, n * d:(n + 1) * d])
            merged = merged + gate * _bdot(y_ref[...], wb_ref[n])
        o_ref[...] = x_ref[...] + _bdot(merged, wo_ref[...])

    row = pl.BlockSpec((tm, d), lambda i: (i, 0))
    yblk = pl.BlockSpec((tm, BW), lambda i: (i, 0))

    def gl(n):
        return pl.BlockSpec((tm, d), lambda i: (i, CB_GATE + n))

    return pl.pallas_call(
        body, grid=(t // tm,),
        in_specs=[row, gl(0), gl(1), gl(2), yblk, yblk, yblk, pl.BlockSpec((1, 3 * d), lambda i: (0, 0)),
                  pl.BlockSpec((3, BW, d), lambda i: (0, 0, 0)), pl.BlockSpec((d, d), lambda i: (0, 0))],
        out_specs=row, out_shape=jax.ShapeDtypeStruct((t, d), F32),
        compiler_params=_cp(("parallel",)), name=name)(x, proj, proj, proj, yp, yd, ys, b_gate, wb, wo)


def _merge_bwd(proj, yp, yd, ys, b_gate, wb, wo, dx, *, name):
    t, d = dx.shape
    tm = _pick(t, (MERGE_TM, 128))

    def body(g0_ref, g1_ref, g2_ref, yp_ref, yd_ref, ys_ref, bg_ref, wb_ref, wo_ref, dx_ref,
             dyp_ref, dyd_ref, dys_ref, dgl_ref, mg_ref, dxh_ref, dbd_ref, dbg_ref):
        dxh = dx_ref[...].astype(BF16)
        dxh_ref[...] = dxh
        dmerged = _bdot(dxh, wo_ref[...], NT_DIMS)
        merged = jnp.zeros((tm, d), F32)

        @pl.when(pl.program_id(0) == 0)
        def _():
            dbg_ref[...] = jnp.zeros_like(dbg_ref)

        for n, (g_ref, y_ref, dy_ref) in enumerate(((g0_ref, yp_ref, dyp_ref), (g1_ref, yd_ref, dyd_ref),
                                                    (g2_ref, ys_ref, dys_ref))):
            gate = _sigmoid(g_ref[...] + bg_ref[:, n * d:(n + 1) * d])
            bd = _bdot(y_ref[...], wb_ref[n])
            merged = merged + gate * bd
            dgl = dmerged * bd * gate * (1.0 - gate)
            dgl_ref[:, n * d:(n + 1) * d] = dgl.astype(BF16)
            dbg_ref[:, n * d:(n + 1) * d] += jnp.sum(dgl, axis=0, keepdims=True)
            dbd = (dmerged * gate).astype(BF16)
            dbd_ref[n] = dbd
            dy_ref[...] = _bdot(dbd, wb_ref[n], NT_DIMS)
        mg_ref[...] = merged.astype(BF16)

    row = pl.BlockSpec((tm, d), lambda i: (i, 0))
    yblk = pl.BlockSpec((tm, BW), lambda i: (i, 0))
    bgv = pl.BlockSpec((1, 3 * d), lambda i: (0, 0))

    def gl(n):
        return pl.BlockSpec((tm, d), lambda i: (i, CB_GATE + n))

    ysds = jax.ShapeDtypeStruct((t, BW), F32)
    return pl.pallas_call(
        body, grid=(t // tm,),
        in_specs=[gl(0), gl(1), gl(2), yblk, yblk, yblk, bgv,
                  pl.BlockSpec((3, BW, d), lambda i: (0, 0, 0)), pl.BlockSpec((d, d), lambda i: (0, 0)), row],
        out_specs=[yblk, yblk, yblk, pl.BlockSpec((tm, 3 * d), lambda i: (i, 0)), row, row,
                   pl.BlockSpec((3, tm, d), lambda i: (0, i, 0)), bgv],
        out_shape=[ysds, ysds, ysds, jax.ShapeDtypeStruct((t, 3 * d), BF16), jax.ShapeDtypeStruct((t, d), BF16),
                   jax.ShapeDtypeStruct((t, d), BF16), jax.ShapeDtypeStruct((3, t, d), BF16),
                   jax.ShapeDtypeStruct((1, 3 * d), F32)],
        compiler_params=_cp(("arbitrary",)), name=name)(proj, proj, proj, yp, yd, ys, b_gate, wb, wo, dx)


def _loss_head(x, g, target, *, name):
    t, d = x.shape
    tm = _pick(t, (512, 256, 128))

    def body(x_ref, g_ref, t_ref, dx_ref, dg_ref, loss_ref):
        xhat, rstd = _rms_stats(x_ref[...])
        err = xhat * g_ref[...] - t_ref[...]
        dx, dg = _rms_bwd_vals(err * (1.0 / d), xhat, rstd, g_ref[...])
        dx_ref[...] = dx

        @pl.when(pl.program_id(0) == 0)
        def _():
            dg_ref[...] = jnp.zeros_like(dg_ref)
            loss_ref[...] = jnp.zeros_like(loss_ref)

        dg_ref[...] += dg
        part = jnp.sum(jnp.sum(err * err, axis=1, keepdims=True), axis=0, keepdims=True) * (0.5 / d)
        loss_ref[...] += jnp.broadcast_to(part, (1, LANE))

    row = pl.BlockSpec((tm, d), lambda i: (i, 0))
    vec = pl.BlockSpec((1, d), lambda i: (0, 0))
    return pl.pallas_call(
        body, grid=(t // tm,), in_specs=[row, vec, row],
        out_specs=[row, vec, pl.BlockSpec((1, LANE), lambda i: (0, 0))],
        out_shape=[jax.ShapeDtypeStruct((t, d), F32), jax.ShapeDtypeStruct((1, d), F32),
                   jax.ShapeDtypeStruct((1, LANE), F32)],
        compiler_params=_cp(("arbitrary",)), name=name)(x, g.reshape(1, d), target)


def _adamw(w, g, m, v, *, name):
    rows, cols = w.shape
    fits = [c for c in (1024, 704, 512, 352, 256, 128, 64, 32, 16, 8) if c * cols * 4 * 14 <= VMEM_LIMIT // 2]
    tr = _pick(rows, fits)
    c1 = 1.0 / (1.0 - ADAM_B1 ** ADAM_STEP)
    c2 = 1.0 / (1.0 - ADAM_B2 ** ADAM_STEP)

    def body(w_ref, g_ref, m_ref, v_ref, d_ref, nm_ref, nv_ref):
        g = g_ref[...]
        nm = ADAM_B1 * m_ref[...] + (1.0 - ADAM_B1) * g
        nv = ADAM_B2 * v_ref[...] + (1.0 - ADAM_B2) * (g * g)
        nm_ref[...] = nm
        nv_ref[...] = nv
        d_ref[...] = -ADAM_LR * ((nm * c1) / (jnp.sqrt(nv * c2) + ADAM_EPS) + ADAM_WD * w_ref[...])

    blk = pl.BlockSpec((tr, cols), lambda i: (i, 0))
    sds = jax.ShapeDtypeStruct((rows, cols), F32)
    return pl.pallas_call(
        body, grid=(rows // tr,), in_specs=[blk] * 4, out_specs=[blk] * 3, out_shape=[sds] * 3,
        compiler_params=_cp(("parallel",)), name=name)(w, g, m, v)


MESH_ID = pl.DeviceIdType.MESH
HBM_SPEC = pl.BlockSpec(memory_space=pl.ANY)
OTHER_CHIPS = ((1, 0), (0, 1), (1, 1))


def _at_slot(ref, nl, slot):
    return ref.at[(slice(None),) * nl + (slot,)]


def _slotted(shape, nl, slots):
    return tuple(shape[:nl]) + (slots,) + tuple(shape[nl:])


def _flip(v, f):
    return 1 - v if f else v


def _comm_call(body, n, out_shapes, n_remote, args, name):
    return pl.pallas_call(
        body, out_shape=out_shapes, in_specs=[HBM_SPEC] * len(args), out_specs=[HBM_SPEC] * len(out_shapes),
        scratch_shapes=[pltpu.SemaphoreType.DMA((n * n_remote,)), pltpu.SemaphoreType.DMA((n * n_remote,)),
                        pltpu.SemaphoreType.DMA((n * 4,))],
        compiler_params=pltpu.CompilerParams(has_side_effects=True), name=name)(*args)


def _gather(xs, nls, *, name):
    n = len(xs)

    def body(*refs):
        start, finish = _gather_parts(refs[:n], refs[n:2 * n], nls, *refs[2 * n:])
        start()
        finish()

    return _comm_call(body, n, _gather_shapes(xs, nls), 7, xs, name)


def _gather_shapes(xs, nls):
    return [jax.ShapeDtypeStruct(_slotted(v.shape, nl, N_DEV), v.dtype) for v, nl in zip(xs, nls)]


def _comm_sems(n, n_remote):
    return [pltpu.SemaphoreType.DMA((n * n_remote,)), pltpu.SemaphoreType.DMA((n * n_remote,)),
            pltpu.SemaphoreType.DMA((n * 4,))]


def _gather_parts(x_refs, o_refs, nls, send_sems, recv_sems, local_sems):
    n = len(x_refs)
    x, y, c = lax.axis_index("x"), lax.axis_index("y"), lax.axis_index("c")
    me, sibling = (x, y, c), (x, y, 1 - c)
    chips = [(_flip(x, fx), _flip(y, fy)) for fx, fy in OTHER_CHIPS]

    def copy(a, k, block, to, src=None):
        dst = _at_slot(o_refs[a], nls[a], 4 * block[0] + 2 * block[1] + block[2])
        return pltpu.make_async_remote_copy(
            src_ref=dst if src is None else src, dst_ref=dst, send_sem=send_sems.at[a * 7 + k],
            recv_sem=recv_sems.at[a * 7 + k], device_id=to, device_id_type=MESH_ID)

    def mine(a):
        return pltpu.make_async_copy(x_refs[a], _at_slot(o_refs[a], nls[a], 4 * x + 2 * y + c), local_sems.at[a])

    def first(a):
        return ([copy(a, 0, me, sibling, src=x_refs[a])]
                + [copy(a, 1 + j, me, (*chip, c), src=x_refs[a]) for j, chip in enumerate(chips)])

    def start():
        for a in range(n):
            mine(a).start()
            for cp in first(a):
                cp.start()

    def finish():
        passed = []
        for j, chip in enumerate(chips):
            for a in range(n):
                copy(a, 1 + j, (*chip, c), me).wait_recv()
                passed.append(copy(a, 4 + j, (*chip, c), sibling))
                passed[-1].start()
        for a in range(n):
            copy(a, 0, sibling, me).wait_recv()
            for j, chip in enumerate(chips):
                copy(a, 4 + j, (*chip, 1 - c), me).wait_recv()
        for a in range(n):
            for cp in first(a):
                cp.wait_send()
        for cp in passed:
            cp.wait_send()
        for a in range(n):
            mine(a).wait()

    return start, finish


ALL_FLIPS = ((0, 0, 1), (0, 1, 0), (0, 1, 1), (1, 0, 0), (1, 0, 1), (1, 1, 0), (1, 1, 1))


def _all_to_all_parts(g_refs, r_refs, nls, send_sems, recv_sems, local_sems):
    del local_sems
    n = len(g_refs)
    x, y, c = lax.axis_index("x"), lax.axis_index("y"), lax.axis_index("c")

    def copies():
        out = []
        for a in range(n):
            for k, (fx, fy, fc) in enumerate(ALL_FLIPS):
                p = (_flip(x, fx), _flip(y, fy), _flip(c, fc))
                out.append(pltpu.make_async_remote_copy(
                    src_ref=_at_slot(g_refs[a], nls[a], 4 * p[0] + 2 * p[1] + p[2]), dst_ref=_at_slot(r_refs[a], nls[a], k),
                    send_sem=send_sems.at[a * 7 + k], recv_sem=recv_sems.at[a * 7 + k], device_id=p,
                    device_id_type=MESH_ID))
        return out

    def start():
        for cp in copies():
            cp.start()

    def finish():
        cps = copies()
        for cp in cps:
            cp.wait_recv()
        for cp in cps:
            cp.wait_send()

    return start, finish


def _all_to_all_shapes(gs, nls):
    return [jax.ShapeDtypeStruct(_slotted(v.shape[:nl] + v.shape[nl + 1:], nl, 7), v.dtype) for v, nl in zip(gs, nls)]


def _scatter_pair(gs, nls, *, name):
    n = len(gs)

    def body(*refs):
        g_refs, got_refs, (send_sems, recv_sems, _) = refs[:n], refs[n:2 * n], refs[2 * n:]
        x, y, c = lax.axis_index("x"), lax.axis_index("y"), lax.axis_index("c")
        remote = []
        for a in range(n):
            for q in range(4):
                rc = pltpu.make_async_remote_copy(
                    src_ref=_at_slot(g_refs[a], nls[a], 2 * q + 1 - c), dst_ref=_at_slot(got_refs[a], nls[a], q),
                    send_sem=send_sems.at[a * 4 + q], recv_sem=recv_sems.at[a * 4 + q], device_id=(x, y, 1 - c),
                    device_id_type=MESH_ID)
                rc.start()
                remote.append(rc)
        for rc in remote:
            rc.wait_recv()
        for rc in remote:
            rc.wait_send()

    outs = [jax.ShapeDtypeStruct(_slotted(v.shape[:nl] + v.shape[nl + 1:], nl, 4), v.dtype) for v, nl in zip(gs, nls)]
    return _comm_call(body, n, outs, 4, gs, name)


def _scatter_chips(ps, nls, *, name):
    n = len(ps)

    def body(*refs):
        p_refs, r_refs, (send_sems, recv_sems, _) = refs[:n], refs[n:2 * n], refs[2 * n:]
        x, y, c = lax.axis_index("x"), lax.axis_index("y"), lax.axis_index("c")
        remote = []
        for a in range(n):
            for k, (fx, fy) in enumerate(OTHER_CHIPS):
                tx, ty = _flip(x, fx), _flip(y, fy)
                rc = pltpu.make_async_remote_copy(
                    src_ref=_at_slot(p_refs[a], nls[a], 2 * tx + ty), dst_ref=_at_slot(r_refs[a], nls[a], k),
                    send_sem=send_sems.at[a * 3 + k], recv_sem=recv_sems.at[a * 3 + k], device_id=(tx, ty, c),
                    device_id_type=MESH_ID)
                rc.start()
                remote.append(rc)
        for rc in remote:
            rc.wait_recv()
        for rc in remote:
            rc.wait_send()

    outs = [jax.ShapeDtypeStruct(_slotted(v.shape[:nl] + v.shape[nl + 1:], nl, 3), v.dtype) for v, nl in zip(ps, nls)]
    return _comm_call(body, n, outs, 3, ps, name)


def _pair_add(g, got, core, *, name):
    rows, cols = g.shape[-2:]
    lf = math.prod(got.shape[:-3])
    tr = _pick(rows, (1024, 512, 352, 256, 128))

    def body(core_ref, g_ref, got_ref, o_ref):
        o_ref[...] = (g_ref[...].astype(F32) + got_ref[...].astype(F32)).astype(BF16)

    blk = pl.BlockSpec((None, None, tr, cols), lambda i, q, j, core_ref: (i, q, j, 0))
    out = pl.pallas_call(
        body, grid_spec=pltpu.PrefetchScalarGridSpec(
            num_scalar_prefetch=1, grid=(lf, 4, rows // tr),
            in_specs=[pl.BlockSpec((None, None, None, tr, cols), lambda i, q, j, core_ref: (i, q, core_ref[0], j, 0)),
                      blk], out_specs=blk),
        out_shape=jax.ShapeDtypeStruct((lf, 4, rows, cols), BF16),
        compiler_params=_cp(("parallel", "parallel", "parallel")), name=name)(
            core, g.reshape(lf, 4, 2, rows, cols), got.reshape(lf, 4, rows, cols))
    return out.reshape(got.shape)


def _sum_adamw(p, r, own, w, m, v, layer, prev, *, name):
    shape = w.shape[1:]
    rows, cols = shape[-2:]
    lf = math.prod(shape[:-2])
    np_, nk = p.shape[-3], r.shape[-3]
    fits = [c for c in (1024, 512, 352, 256, 128, 64, 32, 16) if c * cols * (7 * 4 + (nk + 1) * 2) * 2 <= VMEM_LIMIT // 2]
    tr = _pick(rows, fits)
    c1 = 1.0 / (1.0 - ADAM_B1 ** ADAM_STEP)
    c2 = 1.0 / (1.0 - ADAM_B2 ** ADAM_STEP)

    def body(own_ref, p_ref, r_ref, w_ref, m_ref, v_ref, *rest):
        g_ref, d_ref, nm_ref, nv_ref = rest[-4:]
        g = p_ref[...].astype(F32)
        for k in range(nk):
            g = g + r_ref[k].astype(F32)
        g_ref[...] = g
        nm = ADAM_B1 * m_ref[...] + (1.0 - ADAM_B1) * g
        nv = ADAM_B2 * v_ref[...] + (1.0 - ADAM_B2) * (g * g)
        nm_ref[...] = nm
        nv_ref[...] = nv
        d_ref[...] = -ADAM_LR * ((nm * c1) / (jnp.sqrt(nv * c2) + ADAM_EPS) + ADAM_WD * w_ref[...])

    wblk = pl.BlockSpec((None, None, tr, cols), lambda i, j, own_ref: (layer, i, j, 0))
    full = (w.shape[0], lf, rows, cols)
    sds = jax.ShapeDtypeStruct(full, F32)
    prev = [] if prev is None else [a.reshape(full) for a in prev]
    outs = pl.pallas_call(
        body, grid_spec=pltpu.PrefetchScalarGridSpec(
            num_scalar_prefetch=1, grid=(lf, rows // tr),
            in_specs=[pl.BlockSpec((None, None, tr, cols), lambda i, j, own_ref: (i, own_ref[0], j, 0)),
                      pl.BlockSpec((None, nk, tr, cols), lambda i, j, own_ref: (i, 0, j, 0))] + [wblk] * 3
            + [HBM_SPEC] * len(prev),
            out_specs=[wblk] * 4),
        out_shape=[sds] * 4, input_output_aliases={6 + i: i for i in range(len(prev))},
        compiler_params=_cp(("parallel", "parallel")), name=name)(
            own, p.reshape(lf, np_, rows, cols), r.reshape(lf, nk, rows, cols), w.reshape(full), m.reshape(full),
            v.reshape(full), *prev)
    return [o.reshape(w.shape) for o in outs]


def _sum_slots(x, *, name):
    nd, rows, cols = x.shape
    tr = _pick(rows, (512, 256, 128, 64, 32, 16, 8))

    def body(x_ref, o_ref):
        acc = x_ref[0].astype(F32)
        for j in range(1, nd):
            acc = acc + x_ref[j].astype(F32)
        o_ref[...] = acc

    return pl.pallas_call(
        body, grid=(rows // tr,), in_specs=[pl.BlockSpec((nd, tr, cols), lambda i: (0, i, 0))],
        out_specs=pl.BlockSpec((tr, cols), lambda i: (i, 0)), out_shape=jax.ShapeDtypeStruct((rows, cols), F32),
        compiler_params=_cp(("parallel",)), name=name)(x)


def _pad_rows(a, mult=8):
    r = (-a.shape[0]) % mult
    return jnp.pad(a, ((0, r), (0, 0))) if r else a


def _flat128(a):
    f = a.reshape(-1)
    return jnp.pad(f, (0, (-f.shape[0]) % LANE)).reshape(-1, LANE)


def _unshard(gathered, shape, axis):
    g = gathered.reshape((N_DEV,) + tuple(shape))
    g = jnp.moveaxis(g, 0, axis)
    full = list(shape)
    full[axis] *= N_DEV
    return g.reshape(full)


def _col_shards(full):
    rows, cols = full.shape
    return jnp.moveaxis(full.reshape(rows, N_DEV, cols // N_DEV), 1, 0)


BIG = (("ffn_w_gate", 2), ("ffn_w_up", 2), ("ffn_w_down", 2), ("w_in", 1), ("w_branch", 2), ("w_out", 1))


def kernel(x, ffn_norm, ffn_w_gate, ffn_w_up, ffn_w_down, mix_norm, w_in, b_gate, pool_w, pool_scale, dn_conv, dn_A_log, dn_dt_bias, dn_out_norm, w_branch, w_out, final_norm, loss_target, m_ffn_norm, m_ffn_w_gate, m_ffn_w_up, m_ffn_w_down, m_mix_norm, m_w_in, m_b_gate, m_pool_w, m_pool_scale, m_dn_conv, m_dn_A_log, m_dn_dt_bias, m_dn_out_norm, m_w_branch, m_w_out, m_final_norm, v_ffn_norm, v_ffn_w_gate, v_ffn_w_up, v_ffn_w_down, v_mix_norm, v_w_in, v_b_gate, v_pool_w, v_pool_scale, v_dn_conv, v_dn_A_log, v_dn_dt_bias, v_dn_out_norm, v_w_branch, v_w_out, v_final_norm):
    wts = dict(ffn_norm=ffn_norm, ffn_w_gate=ffn_w_gate, ffn_w_up=ffn_w_up, ffn_w_down=ffn_w_down, mix_norm=mix_norm,
               w_in=w_in, b_gate=b_gate, pool_w=pool_w, pool_scale=pool_scale, dn_conv=dn_conv, dn_A_log=dn_A_log,
               dn_dt_bias=dn_dt_bias, dn_out_norm=dn_out_norm, w_branch=w_branch, w_out=w_out, final_norm=final_norm)
    mom = dict(ffn_norm=m_ffn_norm, ffn_w_gate=m_ffn_w_gate, ffn_w_up=m_ffn_w_up, ffn_w_down=m_ffn_w_down,
               mix_norm=m_mix_norm, w_in=m_w_in, b_gate=m_b_gate, pool_w=m_pool_w, pool_scale=m_pool_scale,
               dn_conv=m_dn_conv, dn_A_log=m_dn_A_log, dn_dt_bias=m_dn_dt_bias, dn_out_norm=m_dn_out_norm,
               w_branch=m_w_branch, w_out=m_w_out, final_norm=m_final_norm)
    var = dict(ffn_norm=v_ffn_norm, ffn_w_gate=v_ffn_w_gate, ffn_w_up=v_ffn_w_up, ffn_w_down=v_ffn_w_down,
               mix_norm=v_mix_norm, w_in=v_w_in, b_gate=v_b_gate, pool_w=v_pool_w, pool_scale=v_pool_scale,
               dn_conv=v_dn_conv, dn_A_log=v_dn_A_log, dn_dt_bias=v_dn_dt_bias, dn_out_norm=v_dn_out_norm,
               w_branch=v_w_branch, w_out=v_w_out, final_norm=v_final_norm)
    nb, s, d = x.shape
    t = nb * s
    me = 4 * lax.axis_index("x") + 2 * lax.axis_index("y") + lax.axis_index("c")

    big = [n for n, _ in BIG]
    nls = [nl - 1 for _, nl in BIG]
    shards = lambda l: [wts[n][l].astype(BF16) for n in big]
    small_sh = jnp.concatenate([_flat128(ffn_norm), _flat128(dn_conv)], axis=0)
    ffn3 = big[:3]
    *pre0, small_g = _gather([wts[n][0, 0].astype(BF16) for n in ffn3] + [small_sh], [0] * 4, name="gather_weights")
    rest0 = [wts[n][0, 1].astype(BF16) for n in ffn3] + [wts[n][0].astype(BF16) for n in big[3:]]
    rest0_nls = [0] * 3 + nls[3:]
    full = [None] * DEPTH

    def mixer_weights(l):
        w_in_full = jnp.moveaxis(full[l]["w_in"], 0, 1).reshape(d, -1)
        w_main = jnp.concatenate([w_in_full[:, :AB_LO], w_in_full[:, AB_HI:]], axis=1)
        w_ab = jnp.pad(w_in_full[:, AB_LO:AB_HI], ((0, 0), (0, LANE - (AB_HI - AB_LO))))
        wb = jnp.moveaxis(full[l]["w_branch"], 1, 2).reshape(3, BW, d)
        return w_main, w_ab, wb, full[l]["w_out"].reshape(d, d)

    nfr = ffn_norm.size // LANE
    ffn_norm_full = _unshard(small_g[:, :nfr], ffn_norm.shape, 2)
    dn_conv_full = _unshard(small_g[:, nfr:], dn_conv.shape, 2)
    pool_w_h = pool_w.astype(BF16)

    xs = x.reshape(t, d)
    saved = []
    for l in range(DEPTH):
        sv = dict(x0=xs)
        if l == 0:
            xs, a0, b0, got = _ffn_fwd(xs, ffn_norm_full[0, 0], *pre0, name="ffn_fwd_gather", gather=(rest0, rest0_nls))
            full[0] = dict(zip(ffn3, zip(pre0, got[:3])), **dict(zip(big[3:], got[3:])))
        else:
            xs, a0, b0, _ = _ffn_fwd(xs, ffn_norm_full[l, 0], full[l]["ffn_w_gate"][0], full[l]["ffn_w_up"][0],
                                     full[l]["ffn_w_down"][0], name="ffn_fwd")
        sv["ab0"] = (a0, b0)
        sv["x1"] = xs
        w_main, w_ab, wb, wo = mixer_weights(l)
        h = _rms_fwd(xs, mix_norm[l], name="mix_rms")
        proj = _mm(h, w_main, name="proj")
        ab = _mm(h, w_ab, name="proj_ab")
        par = jnp.pad(jnp.stack([dn_A_log[l], dn_dt_bias[l]]), ((0, 6), (0, LANE - NH)))
        gain = dn_out_norm[l].reshape(1, HD)
        psc = pool_scale[l].reshape(1, BW)
        yp = _pool_fwd(proj, pool_w_h[l], psc, nb, s, name="pool_fwd")
        yd, o_pre, states, dn_local, gat = _dn_fwd(proj, ab, dn_conv_full[l], par, gain, nb, s,
                                         name="dn_fwd" if l == DEPTH - 1 else "dn_fwd_gather",
                                         gather=(shards(l + 1), nls) if l < DEPTH - 1 else None)
        if l < DEPTH - 1:
            full[l + 1] = dict(zip(big, gat))
        ys, sb_ctr = _sb_fwd(proj, nb, s, name="sb_fwd")
        bg = b_gate[l].reshape(1, 3 * d)
        xs = _merge_fwd(xs, proj, yp, yd, ys, bg, wb, wo, name="merge_fwd")
        sv.update(x2=xs, h=h, proj=proj, ab=ab, par=par, gain=gain, psc=psc, yp=yp, yd=yd, ys=ys, sb_ctr=sb_ctr, o_pre=o_pre,
                  states=states, dn_local=dn_local, bg=bg, w_main=w_main, w_ab=w_ab, wb=wb, wo=wo)
        xs, a1, b1, _ = _ffn_fwd(xs, ffn_norm_full[l, 1], full[l]["ffn_w_gate"][1], full[l]["ffn_w_up"][1],
                                 full[l]["ffn_w_down"][1], name="ffn_fwd")
        sv["ab1"] = (a1, b1)
        saved.append(sv)

    dx, g_final, loss_row = _loss_head(xs, final_norm, loss_target.reshape(t, d), name="loss_head")
    loss = lax.psum(loss_row[0, 0], ("x", "y", "c"))

    gw = {n: [None] * DEPTH for n in ("ffn_norm", "ffn_w_gate", "ffn_w_up", "ffn_w_down", "mix_norm", "w_in", "b_gate",
                                      "pool_w", "pool_scale", "dn_conv", "dn_A_log", "dn_dt_bias", "dn_out_norm",
                                      "w_branch", "w_out")}

    me_i = me.astype(jnp.int32).reshape(1)
    updated = {n: None for n in big}
    pending = None

    def finish_layer(l, own_blocks, arrived, own_slot):
        for n, p, r in zip(big, own_blocks, arrived):
            updated[n] = _sum_adamw(p, r, own_slot, wts[n], mom[n], var[n], l, updated[n], name=f"adamw_{n}_{l}")

    def ffn_back(l, i, x_in, dy):
        dxi, dg, hb, dyh, da, db, sact = _ffn_bwd(x_in, ffn_norm_full[l, i], full[l]["ffn_w_gate"][i],
                                                  full[l]["ffn_w_up"][i], full[l]["ffn_w_down"][i],
                                                  *saved[l][f"ab{i}"], dy, name="ffn_bwd")
        return dxi, dg, (_mm_slots(hb, da, name="dw_gate_up"), _mm_slots(hb, db, name="dw_gate_up"),
                         _mm_slots(sact, dyh, name="dw_down"))

    for l in reversed(range(DEPTH)):
        sv = saved[l]
        dx, dg1, (dwg1, dwu1, dwd1) = ffn_back(l, 1, sv["x2"], dx)
        dyp, dyd, dys, dgl, merged, dxh, dbd, dbg = _merge_bwd(sv["proj"], sv["yp"], sv["yd"], sv["ys"], sv["bg"],
                                                               sv["wb"], sv["wo"], dx, name="merge_bwd")
        gw["w_out"][l] = _mm(merged, dxh, ta=True, out_dtype=BF16, name="dw_out").reshape(N_DEV, d // N_DEV, d)
        gw["w_branch"][l] = jnp.stack([_col_shards(_mm(y, dbd[n], ta=True, out_dtype=BF16, name="dw_branch"))
                                       for n, y in enumerate((sv["yp"], sv["yd"], sv["ys"]))])
        gw["b_gate"][l] = dbg.reshape(3 * d)
        du, dpw, dps = _pool_bwd(sv["proj"], pool_w_h[l], sv["psc"], dyp, nb, s, name="pool_bwd")
        gw["pool_w"][l], gw["pool_scale"][l] = dpw, dps.reshape(BW)
        dqr, dkr, dvr, dz, dab4, dcq, dck, dcv, dpar, dgain, arrived = _dn_bwd(
            sv["proj"], sv["ab"], dn_conv_full[l], sv["par"], sv["gain"], sv["o_pre"], sv["states"], sv["dn_local"],
            dyd, nb, s, name="dn_bwd_scatter" if pending is not None else "dn_bwd",
            scatter=([gw[n][pending] for n in big], nls) if pending is not None else None)
        if pending is not None:
            finish_layer(pending, [gw[n][pending] for n in big], arrived, me_i)
        gw["dn_conv"][l] = jnp.concatenate([dcq, dck, dcv], axis=1)
        gw["dn_A_log"][l], gw["dn_dt_bias"][l], gw["dn_out_norm"][l] = dpar[:, 0, 0], dpar[:, 1, 0], dgain.reshape(HD)
        dsq, dsk, dsv = _sb_bwd(sv["proj"], sv["sb_ctr"], dys, nb, s, name="sb_bwd")
        dab = _sum_heads(dab4, name="sum_heads")
        dproj = jnp.concatenate([du.astype(BF16), dqr.astype(BF16), dkr.astype(BF16), dvr.astype(BF16),
                                 dz.astype(BF16), dsq.astype(BF16), dsk.astype(BF16), dsv.astype(BF16), dgl], axis=1)
        dw_main = _mm(sv["h"], dproj, ta=True, out_dtype=BF16, name="dw_in")
        dw_ab = _mm(sv["h"], dab, ta=True, out_dtype=BF16, name="dw_ab")
        gw["w_in"][l] = _col_shards(jnp.concatenate([dw_main[:, :AB_LO], dw_ab[:, :AB_HI - AB_LO],
                                                     dw_main[:, AB_LO:]], axis=1))
        dh_main = _mm(dproj, sv["w_main"], tb=True, name="dh_mix")
        dh_ab = _mm(dab, sv["w_ab"], tb=True, name="dh_mix_ab")
        dx, dgm = _rms_bwd(sv["x1"], mix_norm[l], dh_main, dh_ab, dx, name="mix_rms_bwd")
        gw["mix_norm"][l] = dgm.reshape(d)
        dx, dg0, (dwg0, dwu0, dwd0) = ffn_back(l, 0, sv["x0"], dx)
        gw["ffn_norm"][l] = jnp.stack([dg0.reshape(d), dg1.reshape(d)])
        gw["ffn_w_gate"][l] = jnp.stack([dwg0, dwg1])
        gw["ffn_w_up"][l] = jnp.stack([dwu0, dwu1])
        gw["ffn_w_down"][l] = jnp.stack([dwd0, dwd1])
        pending = l
    grad_x = dx.reshape(nb, s, d)

    core = lax.axis_index("c").astype(jnp.int32).reshape(1)
    chip = (2 * lax.axis_index("x") + lax.axis_index("y")).astype(jnp.int32).reshape(1)
    last = [gw[n][0] for n in big]
    got = _scatter_pair(last, nls, name="scatter_grads_pair")
    chip_sums = [_pair_add(g, b, core, name="add_pair_" + n) for n, g, b in zip(big, last, got)]
    finish_layer(0, chip_sums, _scatter_chips(chip_sums, nls, name="scatter_grads_chips"), chip)
    grads, delta, new_m, new_v = ({n: updated[n][i] for n in big} for i in range(4))
    gw = {n: jnp.stack(v) for n, v in gw.items() if n not in big}
    gw["final_norm"] = g_final.reshape(d)

    small = ("ffn_norm", "mix_norm", "b_gate", "pool_w", "pool_scale", "dn_conv", "dn_A_log", "dn_dt_bias",
             "dn_out_norm", "final_norm")
    sp = _pad_rows(jnp.concatenate([_flat128(gw[n]) for n in small], axis=0))
    ssum = _sum_slots(_gather([sp], [0], name="gather_small_grads")[0], name="sum_small_grads")
    off = 0
    for n in small:
        r = -(-gw[n].size // LANE)
        g = ssum[off:off + r].reshape(-1)[:gw[n].size].reshape(gw[n].shape)
        off += r
        if n in ("ffn_norm", "dn_conv"):
            w = wts[n].shape[2]
            g = lax.dynamic_slice_in_dim(g, me * w, w, axis=2)
        grads[n] = g

    pk = lambda src: _pad_rows(jnp.concatenate([_flat128(src[n]) for n in small], axis=0))
    dl, nm, nv = _adamw(pk(wts), pk(grads), pk(mom), pk(var), name="adamw_small")
    off = 0
    for n in small:
        r = -(-wts[n].size // LANE)
        for dst, src in ((delta, dl), (new_m, nm), (new_v, nv)):
            dst[n] = src[off:off + r].reshape(-1)[:wts[n].size].reshape(wts[n].shape)
        off += r

    order = ("ffn_norm", "ffn_w_gate", "ffn_w_up", "ffn_w_down", "mix_norm", "w_in", "b_gate", "pool_w", "pool_scale",
             "dn_conv", "dn_A_log", "dn_dt_bias", "dn_out_norm", "w_branch", "w_out", "final_norm")
    return (loss, grad_x, *[grads[n] for n in order], *[delta[n] for n in order], *[new_m[n] for n in order],
            *[new_v[n] for n in order])
```

```python
import functools
import math

import jax
import jax.numpy as jnp
from jax import lax
from jax.experimental import pallas as pl
from jax.experimental.pallas import tpu as pltpu

F32, BF16 = jnp.float32, jnp.bfloat16
D_MODEL, D_FF, DEPTH = 1024, 2816, 4
BW = 512
HD = 128
NH = 4
DN_CHUNK = 64
EPS = 1e-6
N_DEV = 8
LANE = 128
CB_POOL, CB_DNQ, CB_DNK, CB_DNV, CB_DNZ, CB_SBQ, CB_SBK, CB_SBV = 0, 4, 8, 12, 16, 20, 24, 28
CB_GATE = 4
P_MAIN = 7168
AB_LO, AB_HI = 2560, 2568
ADAM_LR, ADAM_B1, ADAM_B2, ADAM_EPS, ADAM_WD, ADAM_STEP = 0.001, 0.9, 0.999, 1e-08, 0.01, 10
VMEM_LIMIT = 56 * 1024 * 1024
HIGHEST = lax.Precision.HIGHEST
NT_DIMS = (((1,), (1,)), ((), ()))
TN_DIMS = (((0,), (0,)), ((), ()))
NN_DIMS = (((1,), (0,)), ((), ()))


def _cp(dims=None, vmem=VMEM_LIMIT):
    return pltpu.CompilerParams(dimension_semantics=dims, vmem_limit_bytes=vmem)


def _pick(n, cands):
    for c in cands:
        if n % c == 0:
            return c
    return n


def _bdot(a, b, dims=NN_DIMS):
    return lax.dot_general(a.astype(BF16), b.astype(BF16), dims, preferred_element_type=F32)


def _hdot(a, b, dims=NN_DIMS):
    return lax.dot_general(a, b, dims, precision=lax.Precision.HIGH, preferred_element_type=F32)


def _split_dot(x, m01):
    hi = x.astype(BF16)
    lo = (x - hi.astype(F32)).astype(BF16)
    return (lax.dot_general(hi, m01, NN_DIMS, preferred_element_type=F32)
            + lax.dot_general(lo, m01, NN_DIMS, preferred_element_type=F32))


def _sigmoid(x):
    return 1.0 / (1.0 + jnp.exp(-x))


def _log_sigmoid(x):
    return jnp.minimum(x, 0.0) - jnp.log1p(jnp.exp(-jnp.abs(x)))


def _softplus(x):
    return jnp.maximum(x, 0.0) + jnp.log1p(jnp.exp(-jnp.abs(x)))


def _shift_down(x, k):
    r = lax.broadcasted_iota(jnp.int32, x.shape, 0)
    return jnp.where(r >= k, pltpu.roll(x, k, 0), 0.0)


def _shift_up(x, k):
    n = x.shape[0]
    r = lax.broadcasted_iota(jnp.int32, x.shape, 0)
    return jnp.where(r < n - k, pltpu.roll(x, n - k, 0), 0.0)


def _mm(a, b, *, ta=False, tb=False, out_dtype=F32, name):
    (kk, m) = a.shape if ta else a.shape[::-1]
    (k2, n) = b.shape[::-1] if tb else b.shape
    assert kk == k2, (a.shape, b.shape, ta, tb)
    bm = _pick(m, (1024, 512, 256, 128))
    bn = _pick(n, (1024, 1408, 512, 256, 128))
    bk = _pick(kk, (512, 256, 128))
    nk = kk // bk
    dims = (((0 if ta else 1,), (1 if tb else 0,)), ((), ()))

    def body(a_ref, b_ref, o_ref, acc_ref):
        k = pl.program_id(2)

        @pl.when(k == 0)
        def _():
            acc_ref[...] = jnp.zeros_like(acc_ref)

        acc_ref[...] += lax.dot_general(a_ref[...].astype(BF16), b_ref[...].astype(BF16), dims,
                                        preferred_element_type=F32)

        @pl.when(k == nk - 1)
        def _():
            o_ref[...] = acc_ref[...].astype(out_dtype)

    a_spec = (pl.BlockSpec((bk, bm), lambda i, j, k: (k, i)) if ta else pl.BlockSpec((bm, bk), lambda i, j, k: (i, k)))
    b_spec = (pl.BlockSpec((bn, bk), lambda i, j, k: (j, k)) if tb else pl.BlockSpec((bk, bn), lambda i, j, k: (k, j)))
    return pl.pallas_call(
        body, grid=(m // bm, n // bn, nk), in_specs=[a_spec, b_spec],
        out_specs=pl.BlockSpec((bm, bn), lambda i, j, k: (i, j)),
        out_shape=jax.ShapeDtypeStruct((m, n), out_dtype),
        scratch_shapes=[pltpu.VMEM((bm, bn), F32)],
        compiler_params=_cp(("parallel", "parallel", "arbitrary")), name=name)(a, b)


def _mm_slots(a, b, *, name):
    a3, b3 = a.ndim == 3, b.ndim == 3
    ns = a.shape[0] if a3 else b.shape[0]
    m, t = a.shape[-2:]
    n = b.shape[-1]
    bk = _pick(t, (512, 256, 128))
    nk = t // bk

    def body(a_ref, b_ref, o_ref, acc_ref):
        k = pl.program_id(0)

        @pl.when(k == 0)
        def _():
            acc_ref[...] = jnp.zeros_like(acc_ref)

        for s in range(ns):
            acc_ref[s] += _bdot(a_ref[s] if a3 else a_ref[...], b_ref[s] if b3 else b_ref[...])

        @pl.when(k == nk - 1)
        def _():
            o_ref[...] = acc_ref[...].astype(BF16)

    a_spec = pl.BlockSpec((ns, m, bk), lambda k: (0, 0, k)) if a3 else pl.BlockSpec((m, bk), lambda k: (0, k))
    b_spec = pl.BlockSpec((ns, bk, n), lambda k: (0, k, 0)) if b3 else pl.BlockSpec((bk, n), lambda k: (k, 0))
    return pl.pallas_call(
        body, grid=(nk,), in_specs=[a_spec, b_spec], out_specs=pl.BlockSpec((ns, m, n), lambda k: (0, 0, 0)),
        out_shape=jax.ShapeDtypeStruct((ns, m, n), BF16), scratch_shapes=[pltpu.VMEM((ns, m, n), F32)],
        compiler_params=_cp(("arbitrary",)), name=name)(a, b)


def _rms_stats(x):
    rstd = lax.rsqrt(jnp.mean(x * x, axis=-1, keepdims=True) + EPS)
    return x * rstd, rstd


def _rms_bwd_vals(dh, xhat, rstd, g):
    dxh = dh * g
    dx = rstd * (dxh - xhat * jnp.mean(dxh * xhat, axis=-1, keepdims=True))
    return dx, jnp.sum(dh * xhat, axis=0, keepdims=True)


def _rms_fwd(x, g, *, name):
    t, d = x.shape
    tm = _pick(t, (512, 256, 128))

    def body(x_ref, g_ref, h_ref):
        xhat, _ = _rms_stats(x_ref[...])
        h_ref[...] = (xhat * g_ref[...]).astype(BF16)

    return pl.pallas_call(
        body, grid=(t // tm,),
        in_specs=[pl.BlockSpec((tm, d), lambda i: (i, 0)), pl.BlockSpec((1, d), lambda i: (0, 0))],
        out_specs=pl.BlockSpec((tm, d), lambda i: (i, 0)), out_shape=jax.ShapeDtypeStruct((t, d), BF16),
        compiler_params=_cp(("parallel",)), name=name)(x, g.reshape(1, d))


def _rms_bwd(x, g, dh_a, dh_b, dres, *, name):
    t, d = x.shape
    tm = _pick(t, (512, 256, 128))

    def body(x_ref, g_ref, dha_ref, dhb_ref, dres_ref, dx_ref, dg_ref):
        xhat, rstd = _rms_stats(x_ref[...])
        dx, dg = _rms_bwd_vals(dha_ref[...] + dhb_ref[...], xhat, rstd, g_ref[...])
        dx_ref[...] = dres_ref[...] + dx

        @pl.when(pl.program_id(0) == 0)
        def _():
            dg_ref[...] = jnp.zeros_like(dg_ref)

        dg_ref[...] += dg

    row = pl.BlockSpec((tm, d), lambda i: (i, 0))
    vec = pl.BlockSpec((1, d), lambda i: (0, 0))
    return pl.pallas_call(
        body, grid=(t // tm,), in_specs=[row, vec, row, row, row], out_specs=[row, vec],
        out_shape=[jax.ShapeDtypeStruct((t, d), F32), jax.ShapeDtypeStruct((1, d), F32)],
        compiler_params=_cp(("arbitrary",)), name=name)(x, g.reshape(1, d), dh_a, dh_b, dres)


FFN_TM = 512


def _ffn_fwd(x, g, wg, wu, wd, *, name, gather=None):
    t, d = x.shape
    nf, _, fc = wg.shape
    tm = _pick(t, (FFN_TM, 256, 128))
    gx, gnl = gather if gather else ([], [])

    def body(x_ref, g_ref, wg_ref, wu_ref, wd_ref, o_ref, a_ref, b_ref, h_ref, acc_ref):
        j = pl.program_id(1)

        @pl.when(j == 0)
        def _():
            xhat, _ = _rms_stats(x_ref[...])
            h_ref[...] = (xhat * g_ref[...]).astype(BF16)
            acc_ref[...] = jnp.zeros_like(acc_ref)

        h = h_ref[...]
        a = _bdot(h, wg_ref[...])
        b = _bdot(h, wu_ref[...])
        a_ref[...] = a.astype(BF16)
        b_ref[...] = b.astype(BF16)
        s = a * _sigmoid(a) * b
        acc_ref[...] += _bdot(s, wd_ref[...])

        @pl.when(j == nf - 1)
        def _():
            o_ref[...] = x_ref[...] + 0.5 * acc_ref[...]

    row = pl.BlockSpec((tm, d), lambda i, j: (i, 0))
    grid = (t // tm, nf)
    exchange = (_gather_parts, len(gx), gnl) if gx else None
    res = pl.pallas_call(
        _with_exchange(body, 5, 3, 2, exchange, grid), grid=grid,
        in_specs=[row, pl.BlockSpec((1, d), lambda i, j: (0, 0)),
                  pl.BlockSpec((None, d, fc), lambda i, j: (j, 0, 0)), pl.BlockSpec((None, d, fc), lambda i, j: (j, 0, 0)),
                  pl.BlockSpec((None, fc, d), lambda i, j: (j, 0, 0))] + [HBM_SPEC] * len(gx),
        out_specs=[row, pl.BlockSpec((None, tm, fc), lambda i, j: (j, i, 0)),
                   pl.BlockSpec((None, tm, fc), lambda i, j: (j, i, 0))] + [HBM_SPEC] * len(gx),
        out_shape=[jax.ShapeDtypeStruct((t, d), F32), jax.ShapeDtypeStruct((nf, t, fc), BF16),
                   jax.ShapeDtypeStruct((nf, t, fc), BF16)] + _gather_shapes(gx, gnl),
        scratch_shapes=[pltpu.VMEM((tm, d), BF16), pltpu.VMEM((tm, d), F32)] + (_comm_sems(len(gx), 7) if gx else []),
        compiler_params=_cp(("arbitrary", "arbitrary")), name=name)(x, g.reshape(1, d), wg, wu, wd, *gx)
    return res[0], res[1], res[2], list(res[3:])


def _ffn_bwd(x, g, wg, wu, wd, a_pre, b_pre, dy, *, name):
    t, d = x.shape
    nf, _, fc = wg.shape
    tm = _pick(t, (FFN_TM, 256, 128))

    def body(x_ref, g_ref, wg_ref, wu_ref, wd_ref, a_ref, b_ref, dy_ref,
             dx_ref, dg_ref, ht_ref, dyh_ref, da_ref, db_ref, st_ref, acc_ref):
        i, j = pl.program_id(0), pl.program_id(1)

        @pl.when(j == 0)
        def _():
            xhat, _ = _rms_stats(x_ref[...])
            ht_ref[...] = (xhat * g_ref[...]).T.astype(BF16)
            dyh_ref[...] = (0.5 * dy_ref[...]).astype(BF16)
            acc_ref[...] = jnp.zeros_like(acc_ref)

        a = a_ref[...].astype(F32)
        b = b_ref[...].astype(F32)
        sg = _sigmoid(a)
        silu = a * sg
        st_ref[...] = (silu * b).T.astype(BF16)
        ds = _bdot(dyh_ref[...], wd_ref[...], NT_DIMS)
        da = (ds * b * (sg * (1.0 + a * (1.0 - sg)))).astype(BF16)
        db = (ds * silu).astype(BF16)
        da_ref[...] = da
        db_ref[...] = db
        acc_ref[...] += _bdot(da, wg_ref[...], NT_DIMS) + _bdot(db, wu_ref[...], NT_DIMS)

        @pl.when((i == 0) & (j == 0))
        def _():
            dg_ref[...] = jnp.zeros_like(dg_ref)

        @pl.when(j == nf - 1)
        def _():
            xhat, rstd = _rms_stats(x_ref[...])
            dx, dg = _rms_bwd_vals(acc_ref[...], xhat, rstd, g_ref[...])
            dx_ref[...] = dy_ref[...] + dx
            dg_ref[...] += dg

    row = pl.BlockSpec((tm, d), lambda i, j: (i, 0))
    vec = pl.BlockSpec((1, d), lambda i, j: (0, 0))
    fblk = pl.BlockSpec((None, tm, fc), lambda i, j: (j, i, 0))
    return pl.pallas_call(
        body, grid=(t // tm, nf),
        in_specs=[row, vec, pl.BlockSpec((None, d, fc), lambda i, j: (j, 0, 0)),
                  pl.BlockSpec((None, d, fc), lambda i, j: (j, 0, 0)), pl.BlockSpec((None, fc, d), lambda i, j: (j, 0, 0)),
                  fblk, fblk, row],
        out_specs=[row, vec, pl.BlockSpec((d, tm), lambda i, j: (0, i)), row, fblk, fblk,
                   pl.BlockSpec((None, fc, tm), lambda i, j: (j, 0, i))],
        out_shape=[jax.ShapeDtypeStruct((t, d), F32), jax.ShapeDtypeStruct((1, d), F32),
                   jax.ShapeDtypeStruct((d, t), BF16), jax.ShapeDtypeStruct((t, d), BF16),
                   jax.ShapeDtypeStruct((nf, t, fc), BF16), jax.ShapeDtypeStruct((nf, t, fc), BF16),
                   jax.ShapeDtypeStruct((nf, fc, t), BF16)],
        scratch_shapes=[pltpu.VMEM((tm, d), F32)],
        compiler_params=_cp(("arbitrary", "arbitrary")), name=name)(x, g.reshape(1, d), wg, wu, wd, a_pre, b_pre, dy)


def _pool_core(u, grp):
    s = u.shape[0]
    w2 = u + _shift_down(u, 1)
    w4 = w2 + _shift_down(w2, 2)
    w8 = w4 + _shift_down(w4, 4)
    w16 = w8 + _shift_down(w8, 8)
    wsum = jnp.where(grp == 0, w2, jnp.where(grp == 1, w4, jnp.where(grp == 2, w8, w16)))
    win = jnp.left_shift(2, grp).astype(F32)
    t1 = (lax.broadcasted_iota(jnp.int32, (s, 1), 0) + 1).astype(F32)
    inv = 1.0 / jnp.minimum(t1, win)
    return wsum * inv - u, inv


def _pool_fwd(proj, pool_w, pool_scale, nb, s, *, name):
    def body(u_ref, w_ref, sc_ref, y_ref):
        pooled, _ = _pool_core(u_ref[...], pl.program_id(0))
        y_ref[...] = _bdot(pooled, w_ref[...]) * sc_ref[...]

    return pl.pallas_call(
        body, grid=(NH, nb),
        in_specs=[pl.BlockSpec((s, HD), lambda g, b: (b, CB_POOL + g)),
                  pl.BlockSpec((None, HD, HD), lambda g, b: (g, 0, 0)), pl.BlockSpec((1, HD), lambda g, b: (0, g))],
        out_specs=pl.BlockSpec((s, HD), lambda g, b: (b, g)),
        out_shape=jax.ShapeDtypeStruct((nb * s, BW), F32),
        compiler_params=_cp(("parallel", "parallel")), name=name)(proj, pool_w, pool_scale)


def _pool_bwd(proj, pool_w, pool_scale, dy, nb, s, *, name):
    def body(u_ref, w_ref, sc_ref, dy_ref, du_ref, dw_ref, dsc_ref):
        grp, b = pl.program_id(0), pl.program_id(1)
        pooled, inv = _pool_core(u_ref[...], grp)
        mixed = _bdot(pooled, w_ref[...])
        dy = dy_ref[...]
        dmixed = dy * sc_ref[...]
        dpooled = _bdot(dmixed, w_ref[...], NT_DIMS)
        r = dpooled * inv
        v2 = r + _shift_up(r, 1)
        v4 = v2 + _shift_up(v2, 2)
        v8 = v4 + _shift_up(v4, 4)
        v16 = v8 + _shift_up(v8, 8)
        vsum = jnp.where(grp == 0, v2, jnp.where(grp == 1, v4, jnp.where(grp == 2, v8, v16)))
        du_ref[...] = vsum - dpooled

        @pl.when(b == 0)
        def _():
            dw_ref[...] = jnp.zeros_like(dw_ref)
            dsc_ref[...] = jnp.zeros_like(dsc_ref)

        dw_ref[...] += _bdot(pooled, dmixed, TN_DIMS)
        dsc_ref[...] += jnp.sum(dy * mixed, axis=0, keepdims=True)

    return pl.pallas_call(
        body, grid=(NH, nb),
        in_specs=[pl.BlockSpec((s, HD), lambda g, b: (b, CB_POOL + g)),
                  pl.BlockSpec((None, HD, HD), lambda g, b: (g, 0, 0)), pl.BlockSpec((1, HD), lambda g, b: (0, g)),
                  pl.BlockSpec((s, HD), lambda g, b: (b, g))],
        out_specs=[pl.BlockSpec((s, HD), lambda g, b: (b, g)), pl.BlockSpec((None, HD, HD), lambda g, b: (g, 0, 0)),
                   pl.BlockSpec((1, HD), lambda g, b: (0, g))],
        out_shape=[jax.ShapeDtypeStruct((nb * s, BW), F32), jax.ShapeDtypeStruct((NH, HD, HD), F32),
                   jax.ShapeDtypeStruct((1, BW), F32)],
        compiler_params=_cp(("arbitrary", "arbitrary")), name=name)(proj, pool_w, pool_scale, dy)


SB_BLK = 128


SB_G = 4
SB_KG = SB_G * SB_BLK
SB_Q = 2 * SB_BLK


def _sb_block(qb, kg, q0, k0, diagonal):
    z = _bdot(qb, kg, NT_DIMS) * (HD ** -0.5)
    lsz = _log_sigmoid(z)
    if not diagonal:
        return lsz, lsz - z, None
    row = lax.broadcasted_iota(jnp.int32, z.shape, 0) + q0
    col = lax.broadcasted_iota(jnp.int32, z.shape, 1) + k0
    causal = col < row
    return lsz, jnp.where(causal, lsz - z, 0.0), causal


def _keep(causal, x):
    return x if causal is None else jnp.where(causal, x, 0.0)


def _sub(x, m):
    return x[:, m * SB_BLK:(m + 1) * SB_BLK]


def _sb_tails(lnm, after, ct):
    hi = lnm.astype(BF16)
    lo = (lnm - hi.astype(F32)).astype(BF16)
    tails = [None] * SB_G
    for m in reversed(range(SB_G)):
        tails[m] = (lax.dot_general(_sub(hi, m), after, NN_DIMS, preferred_element_type=F32)
                    + lax.dot_general(_sub(lo, m), after, NN_DIMS, preferred_element_type=F32)) + ct
        ct = ct + jnp.sum(_sub(lnm, m), axis=1, keepdims=True)
    ones = jnp.ones((8, lnm.shape[1]), BF16)
    rows = (lax.dot_general(ones, hi, NT_DIMS, preferred_element_type=F32)
            + lax.dot_general(ones, lo, NT_DIMS, preferred_element_type=F32))
    return jnp.concatenate(tails, axis=1), rows, ct


def _tri01(lower):
    r = lax.broadcasted_iota(jnp.int32, (SB_BLK, SB_BLK), 0)
    c = lax.broadcasted_iota(jnp.int32, (SB_BLK, SB_BLK), 1)
    return jnp.where((r < c) if lower else (r > c), 1.0, 0.0).astype(BF16)


def _split3(x):
    hi = x.astype(BF16)
    mid = (x - hi.astype(F32)).astype(BF16)
    lo = (x - hi.astype(F32) - mid.astype(F32)).astype(BF16)
    return hi, mid, lo


def _rows_to_cols(rows):
    eighth = jnp.full((8, LANE), 0.125, BF16)
    return sum(lax.dot_general(p, eighth, TN_DIMS, preferred_element_type=F32) for p in _split3(rows))


def _sb_fwd(proj, nb, s, *, name):
    nq = s // SB_Q
    ng = s // SB_KG

    def body(q_ref, k_ref, v_ref, o_ref, ctr_ref):
        after = _tri01(False)

        def qblock(i, _):
            q0 = pl.multiple_of(i * SB_Q, SB_Q)
            qb = q_ref[pl.ds(q0, SB_Q), :]

            def kgroup(g, carry, diagonal):
                acc, ct, ctr = carry
                k0 = pl.multiple_of(g * SB_KG, SB_KG)
                lsz, lnm, causal = _sb_block(qb, k_ref[pl.ds(k0, SB_KG), :], q0, k0, diagonal)
                ctr_ref[i * ng + g] = ctr
                tail, rows, ct = _sb_tails(lnm, after, ct)
                w = _keep(causal, jnp.exp(lsz + tail))
                return acc + _bdot(w, v_ref[pl.ds(k0, SB_KG), :]), ct, ctr + rows

            gd = (i * SB_Q) // SB_KG
            carry = kgroup(gd, (jnp.zeros((SB_Q, HD), F32), jnp.zeros((SB_Q, 1), F32), jnp.zeros((8, SB_Q), F32)), True)
            acc, _, _ = lax.fori_loop(0, gd, lambda jj, c: kgroup(gd - 1 - jj, c, False), carry)
            o_ref[pl.ds(q0, SB_Q), :] = acc
            return 0

        lax.fori_loop(0, nq, qblock, 0)

    def col(cb):
        return pl.BlockSpec((s, HD), lambda b, h: (b, cb + h))

    return pl.pallas_call(
        body, grid=(nb, NH), in_specs=[col(CB_SBQ), col(CB_SBK), col(CB_SBV)],
        out_specs=[pl.BlockSpec((s, HD), lambda b, h: (b, h)),
                   pl.BlockSpec((None, None, nq * ng, 8, SB_Q), lambda b, h: (b, h, 0, 0, 0))],
        out_shape=[jax.ShapeDtypeStruct((nb * s, BW), F32), jax.ShapeDtypeStruct((nb, NH, nq * ng, 8, SB_Q), F32)],
        compiler_params=_cp(("parallel", "parallel")), name=name)(proj, proj, proj)


def _sb_bwd(proj, ctr, dy, nb, s, *, name):
    nq = s // SB_Q
    ng = s // SB_KG
    scale = HD ** -0.5

    def body(q_ref, k_ref, v_ref, ctr_ref, do_ref, dq_ref, dk_ref, dv_ref):
        after = _tri01(False)
        before = _tri01(True)
        dk_ref[...] = jnp.zeros_like(dk_ref)
        dv_ref[...] = jnp.zeros_like(dv_ref)

        def qblock(i, _):
            q0 = pl.multiple_of(i * SB_Q, SB_Q)
            qb = q_ref[pl.ds(q0, SB_Q), :]
            dob = do_ref[pl.ds(q0, SB_Q), :]

            def kgroup(g, carry, diagonal):
                dq, ce = carry
                k0 = pl.multiple_of(g * SB_KG, SB_KG)
                kg = k_ref[pl.ds(k0, SB_KG), :]
                vg = v_ref[pl.ds(k0, SB_KG), :]
                lsz, lnm, causal = _sb_block(qb, kg, q0, k0, diagonal)
                tail, _, _ = _sb_tails(lnm, after, _rows_to_cols(ctr_ref[i * ng + g])[:, 0:1])
                w = _keep(causal, jnp.exp(lsz + tail))
                e = _bdot(dob, vg, NT_DIMS) * w
                pres = []
                for m in range(SB_G):
                    pres.append(_split_dot(_sub(e, m), before) + ce)
                    ce = ce + jnp.sum(_sub(e, m), axis=1, keepdims=True)
                sig = jnp.exp(lsz)
                dz = _keep(causal, e * (1.0 - sig) - jnp.concatenate(pres, axis=1) * sig) * scale
                dk_ref[pl.ds(k0, SB_KG), :] += _bdot(dz, qb, TN_DIMS)
                dv_ref[pl.ds(k0, SB_KG), :] += _bdot(w, dob, TN_DIMS)
                return dq + _bdot(dz, kg), ce

            gd = (i * SB_Q) // SB_KG
            carry = lax.fori_loop(0, gd, lambda g, c: kgroup(g, c, False),
                                  (jnp.zeros((SB_Q, HD), F32), jnp.zeros((SB_Q, 1), F32)))
            dq, _ = kgroup(gd, carry, True)
            dq_ref[pl.ds(q0, SB_Q), :] = dq
            return 0

        lax.fori_loop(0, nq, qblock, 0)

    def col(cb):
        return pl.BlockSpec((s, HD), lambda b, h: (b, cb + h))

    out = pl.BlockSpec((s, HD), lambda b, h: (b, h))
    sds = jax.ShapeDtypeStruct((nb * s, BW), F32)
    return pl.pallas_call(
        body, grid=(nb, NH),
        in_specs=[col(CB_SBQ), col(CB_SBK), col(CB_SBV),
                  pl.BlockSpec((None, None, nq * ng, 8, SB_Q), lambda b, h: (b, h, 0, 0, 0)), out],
        out_specs=[out, out, out], out_shape=[sds, sds, sds],
        compiler_params=_cp(("parallel", "parallel")), name=name)(proj, proj, proj, ctr, dy)


def _make_cdot(dims, dims_da, dims_db, swap_a=False, swap_b=False):
    @jax.custom_vjp
    def f(a, b):
        return _bdot(a, b, dims)

    def fwd(a, b):
        return _bdot(a, b, dims), (a, b)

    def bwd(res, g):
        a, b = res
        da = _bdot(b, g, dims_da) if swap_a else _bdot(g, b, dims_da)
        db = _bdot(g, a, dims_db) if swap_b else _bdot(a, g, dims_db)
        return da, db

    f.defvjp(fwd, bwd)
    return f


_cdot = _make_cdot(NN_DIMS, NT_DIMS, TN_DIMS)
_cdot_nt = _make_cdot(NT_DIMS, NN_DIMS, TN_DIMS, swap_b=True)
_cdot_tn = _make_cdot(TN_DIMS, NT_DIMS, NN_DIMS, swap_a=True)


DN_SUPER = 4 * DN_CHUNK


@jax.custom_vjp
def _unit_lower_inverse(lmat):
    n = lmat.shape[0]
    steps = int(math.log2(DN_CHUNK))
    eye = jnp.where(lax.broadcasted_iota(jnp.int32, (n, n), 0) == lax.broadcasted_iota(jnp.int32, (n, n), 1), 1.0, 0.0)
    inv = eye - lmat
    pw = _hdot(lmat, lmat)
    for it in range(steps - 1):
        inv = inv + _hdot(inv, pw)
        if it < steps - 2:
            pw = _hdot(pw, pw)
    return inv


def _unit_lower_inverse_fwd(lmat):
    inv = _unit_lower_inverse(lmat)
    return inv, inv


def _unit_lower_inverse_bwd(inv, g):
    return (-_hdot(_hdot(inv, g, TN_DIMS), inv, NT_DIMS),)


_unit_lower_inverse.defvjp(_unit_lower_inverse_fwd, _unit_lower_inverse_bwd)


@jax.custom_vjp
def _known_inverse(lmat, inv):
    return inv


def _known_inverse_fwd(lmat, inv):
    return inv, inv


def _known_inverse_bwd(inv, g):
    return -_hdot(_hdot(inv, g, TN_DIMS), inv, NT_DIMS), jnp.zeros_like(inv)


_known_inverse.defvjp(_known_inverse_fwd, _known_inverse_bwd)


def _dn_local(q, k, v, bb, gb, known_inv=None):
    n = q.shape[0]
    r = lax.broadcasted_iota(jnp.int32, (n, n), 0)
    cc = lax.broadcasted_iota(jnp.int32, (n, n), 1)
    shift = int(math.log2(DN_CHUNK))
    same = lax.shift_right_logical(r, shift) == lax.shift_right_logical(cc, shift)
    incl = jnp.where(same, jnp.where(r >= cc, 1.0, 0.0), 0.0)
    strict = jnp.where(same, jnp.where(r > cc, 1.0, 0.0), 0.0)
    gc = _hdot(incl, gb)
    gc_row = _hdot(jnp.full((n, HD), 1.0 / HD, F32), gc, NT_DIMS)
    diff = jnp.concatenate([gc] * (n // HD), axis=1) - gc_row
    decay = incl * jnp.exp(diff * incl)
    kb = k * bb
    lmat = _cdot_nt(kb, k) * (strict * decay)
    egc = jnp.exp(gc)
    inv = _unit_lower_inverse(lmat) if known_inv is None else _known_inverse(lmat, known_inv)
    u = _hdot(inv, v * bb)
    w = _hdot(inv, kb * egc)
    attn = _cdot_nt(q, k) * decay
    gl = _hdot(jnp.where(same, 1.0, 0.0), gb)
    return u, w, attn, q * egc, k * jnp.exp(gl - gc), jnp.exp(gl), inv


def _attn_pairs(attn):
    return jnp.concatenate([attn[:HD, :HD], attn[HD:, HD:]], axis=0)


def _attn_unpairs(a):
    z = jnp.zeros((HD, HD), F32)
    return jnp.concatenate([jnp.concatenate([a[:HD], z], axis=1), jnp.concatenate([z, a[HD:]], axis=1)], axis=0)


def _dn_step(u, w, a, qd, kd, cdrows, state, odd):
    v_new = u - _cdot(w, state)
    z = jnp.zeros_like(v_new)
    o = _cdot(qd, state) + _cdot(a, jnp.concatenate([z, v_new] if odd else [v_new, z], axis=0))
    return o, state * jnp.mean(cdrows, axis=0, keepdims=True) + _cdot_tn(kd, v_new)


def _dn_local_pass(fn, s, ins, outs):
    def step(it, _):
        sl = pl.ds(pl.multiple_of(it * DN_SUPER, DN_SUPER), DN_SUPER)
        res = fn(*[ref[sl, :] for ref in ins])
        for ref, val in zip(outs, res):
            ref[sl, :] = val
        return 0

    lax.fori_loop(0, s // DN_SUPER, step, 0)


def _lane_pick(row, idx):
    lane = lax.broadcasted_iota(jnp.int32, row.shape, 1)
    return jnp.sum(jnp.where(lane == idx, row, 0.0), axis=1, keepdims=True)


def _col_pick(x, idx):
    lane = lax.broadcasted_iota(jnp.int32, x.shape, 1)
    return jnp.sum(jnp.where(lane == idx, x, 0.0), axis=1, keepdims=True)


def _conv_silu(x, w):
    xc = (w[3:4, :] * x + w[2:3, :] * _shift_down(x, 1) + w[1:2, :] * _shift_down(x, 2)
          + w[0:1, :] * _shift_down(x, 3))
    return xc * _sigmoid(xc), xc


def _conv_silu_bwd(x, w, xc, dxs, dw_ref):
    sg = _sigmoid(xc)
    dxc = dxs * (sg * (1.0 + xc * (1.0 - sg)))
    dx = (w[3:4, :] * dxc + w[2:3, :] * _shift_up(dxc, 1) + w[1:2, :] * _shift_up(dxc, 2)
          + w[0:1, :] * _shift_up(dxc, 3))
    dw_ref[3:4, :] += jnp.sum(dxc * x, axis=0, keepdims=True)
    dw_ref[2:3, :] += jnp.sum(dxc * _shift_down(x, 1), axis=0, keepdims=True)
    dw_ref[1:2, :] += jnp.sum(dxc * _shift_down(x, 2), axis=0, keepdims=True)
    dw_ref[0:1, :] += jnp.sum(dxc * _shift_down(x, 3), axis=0, keepdims=True)
    return dx


def _dn_prep(qr_ref, kr_ref, vr_ref, ab_ref, cq_ref, ck_ref, cv_ref, par_ref, head):
    qs, qc = _conv_silu(qr_ref[...], cq_ref[...])
    ks, kc = _conv_silu(kr_ref[...], ck_ref[...])
    vs, vc = _conv_silu(vr_ref[...], cv_ref[...])
    rq = lax.rsqrt(jnp.sum(qs * qs, axis=1, keepdims=True) + EPS)
    rk = lax.rsqrt(jnp.sum(ks * ks, axis=1, keepdims=True) + EPS)
    ab = ab_ref[...]
    a_in = _col_pick(ab, head) + _lane_pick(par_ref[1:2, :], head)
    beta = _sigmoid(_col_pick(ab, NH + head))
    neg_ea = -jnp.exp(_lane_pick(par_ref[0:1, :], head))
    g = neg_ea * _softplus(a_in)
    return dict(q=qs * rq * (HD ** -0.5), k=ks * rk, v=vs, beta=beta, g=g, qs=qs, ks=ks, qc=qc, kc=kc, vc=vc,
                rq=rq, rk=rk, a_in=a_in, neg_ea=neg_ea)


ONE_BUF = pl.Buffered(1)
DN_BWD_VMEM = 62 * 1024 * 1024


def _dn_specs(nb, s):
    def col(cb):
        return pl.BlockSpec((s, HD), lambda h, b: (b, cb + h), pipeline_mode=ONE_BUF)

    def conv(cb):
        return pl.BlockSpec((DN_CONV_W, HD), lambda h, b: (0, cb + h))

    return col, conv


DN_CONV_W = 4


def _with_exchange(body, n_in, n_out, n_scratch, exchange, grid):
    if exchange is None:
        return body
    parts_fn, n, nls = exchange

    def wrapped(*refs):
        ins, xs = refs[:n_in], refs[n_in:n_in + n]
        outs, os = refs[n_in + n:n_in + n + n_out], refs[n_in + n + n_out:n_in + 2 * n + n_out]
        rest = refs[n_in + 2 * n + n_out:]
        scratch, sems = rest[:n_scratch], rest[n_scratch:]
        pos = [pl.program_id(k) for k in range(len(grid))]
        first = functools.reduce(jnp.logical_and, [p == 0 for p in pos])
        last = functools.reduce(jnp.logical_and, [p == g - 1 for p, g in zip(pos, grid)])
        start, finish = parts_fn(xs, os, nls, *sems)
        pl.when(first)(start)
        body(*ins, *outs, *scratch)
        pl.when(last)(finish)

    return wrapped


def _dn_fwd(proj, ab, conv_w, par, gain, nb, s, *, name, gather=None):
    nc = s // DN_CHUNK
    col, conv = _dn_specs(nb, s)
    gx, gnl = gather if gather else ([], [])

    def body(qr_ref, kr_ref, vr_ref, z_ref, ab_ref, cq_ref, ck_ref, cv_ref, par_ref, gain_ref,
             y_ref, o_ref, st_ref, u_ref, w_ref, at_ref, qd_ref, kd_ref, cd_ref, inv_ref, q_s, k_s, v_s, bb_s, gb_s):
        p = _dn_prep(qr_ref, kr_ref, vr_ref, ab_ref, cq_ref, ck_ref, cv_ref, par_ref, pl.program_id(0))
        q_s[...], k_s[...], v_s[...] = p["q"], p["k"], p["v"]
        bb_s[...] = jnp.broadcast_to(p["beta"], (s, HD))
        gb_s[...] = jnp.broadcast_to(p["g"], (s, HD))
        def local(*args):
            u, w, attn, qd, kd, cd, inv = _dn_local(*args)
            return u, w, _attn_pairs(attn), qd, kd, cd, inv

        _dn_local_pass(local, s, [q_s, k_s, v_s, bb_s, gb_s], [u_ref, w_ref, at_ref, qd_ref, kd_ref, cd_ref, inv_ref])

        def chunk_pair(pi, state):
            for odd in (0, 1):
                ci = 2 * pi + odd
                sl = pl.ds(pl.multiple_of(ci * DN_CHUNK, DN_CHUNK), DN_CHUNK)
                st_ref[ci] = state
                o, state = _dn_step(u_ref[sl, :], w_ref[sl, :], at_ref[sl, :], qd_ref[sl, :], kd_ref[sl, :],
                                    cd_ref[sl, :], state, odd)
                o_ref[sl, :] = o
            return state

        lax.fori_loop(0, nc // 2, chunk_pair, jnp.zeros((HD, HD), F32))
        o = o_ref[...]
        z = z_ref[...]
        on = o * lax.rsqrt(jnp.mean(o * o, axis=1, keepdims=True) + EPS) * gain_ref[...]
        y_ref[...] = on * (z * _sigmoid(z))

    out = pl.BlockSpec((s, HD), lambda h, b: (b, h))
    sds = jax.ShapeDtypeStruct((nb * s, BW), F32)
    exchange = (_gather_parts, len(gx), gnl) if gx else None
    res = pl.pallas_call(
        _with_exchange(body, 10, 10, 5, exchange, (NH, nb)), grid=(NH, nb),
        in_specs=[col(CB_DNQ), col(CB_DNK), col(CB_DNV), col(CB_DNZ), pl.BlockSpec((s, LANE), lambda h, b: (b, 0)),
                  conv(0), conv(NH), conv(2 * NH), pl.BlockSpec((8, LANE), lambda h, b: (0, 0)),
                  pl.BlockSpec((1, HD), lambda h, b: (0, 0))] + [HBM_SPEC] * len(gx),
        out_specs=[out, out, pl.BlockSpec((None, None, nc, HD, HD), lambda h, b: (b, h, 0, 0, 0))] + [out] * 6
        + [pl.BlockSpec((s, DN_SUPER), lambda h, b: (b, h))] + [HBM_SPEC] * len(gx),
        out_shape=[sds, sds, jax.ShapeDtypeStruct((nb, NH, nc, HD, HD), F32)] + [sds] * 6
        + [jax.ShapeDtypeStruct((nb * s, NH * DN_SUPER), F32)] + _gather_shapes(gx, gnl),
        scratch_shapes=[pltpu.VMEM((s, HD), F32)] * 5 + (_comm_sems(len(gx), 7) if gx else []),
        compiler_params=_cp(("arbitrary", "arbitrary")), name=name)(
            proj, proj, proj, proj, ab, conv_w, conv_w, conv_w, par, gain, *gx)
    return res[0], res[1], res[2], list(res[3:10]), list(res[10:])


def _dn_bwd(proj, ab, conv_w, par, gain, o_pre, states, local, dy, nb, s, *, name, scatter=None):
    nc = s // DN_CHUNK
    col, conv = _dn_specs(nb, s)
    gx, gnl = scatter if scatter else ([], [])

    def body(qr_ref, kr_ref, vr_ref, z_ref, ab_ref, cq_ref, ck_ref, cv_ref, par_ref, gain_ref, o_ref, st_ref, dy_ref,
             u_hbm, w_hbm, at_hbm, qd_hbm, kd_hbm, cd_hbm, inv_hbm,
             dqr_ref, dkr_ref, dvr_ref, dz_ref, dab_ref, dcq_ref, dck_ref, dcv_ref, dpar_ref, dgain_ref,
             q_s, k_s, v_s, bb_s, gb_s, do_s, u_s, w_s, qd_s, kd_s, at_s, cd_s, load_sems, inv_buf, inv_sems):
        head, b = pl.program_id(0), pl.program_id(1)
        local_refs = [u_s, w_s, at_s, qd_s, kd_s, cd_s]
        loads = [pltpu.make_async_copy(src.at[pl.ds(pl.multiple_of(b * s, s), s), pl.ds(pl.multiple_of(head * HD, HD), HD)],
                                       dst, load_sems.at[i])
                 for i, (src, dst) in enumerate(zip((u_hbm, w_hbm, at_hbm, qd_hbm, kd_hbm, cd_hbm), local_refs))]
        for cp in loads:
            cp.start()
        p = _dn_prep(qr_ref, kr_ref, vr_ref, ab_ref, cq_ref, ck_ref, cv_ref, par_ref, head)
        q_s[...], k_s[...], v_s[...] = p["q"], p["k"], p["v"]
        bb_s[...] = jnp.broadcast_to(p["beta"], (s, HD))
        gb_s[...] = jnp.broadcast_to(p["g"], (s, HD))

        @pl.when(b == 0)
        def _():
            for ref in (dcq_ref, dck_ref, dcv_ref, dpar_ref):
                ref[...] = jnp.zeros_like(ref)

        @pl.when((b == 0) & (head == 0))
        def _():
            dgain_ref[...] = jnp.zeros_like(dgain_ref)

        o, z, dy = o_ref[...], z_ref[...], dy_ref[...]
        rstd = lax.rsqrt(jnp.mean(o * o, axis=1, keepdims=True) + EPS)
        ohat = o * rstd
        sgz = _sigmoid(z)
        dz_ref[...] = dy * (ohat * gain_ref[...]) * (sgz * (1.0 + z * (1.0 - sgz)))
        don = dy * (z * sgz)
        dgain_ref[...] += jnp.sum(don * ohat, axis=0, keepdims=True)
        dxh = don * gain_ref[...]
        do_s[...] = rstd * (dxh - ohat * jnp.mean(dxh * ohat, axis=1, keepdims=True))

        for cp in loads:
            cp.wait()

        def chunk_pair(pr, dstate):
            for odd in (1, 0):
                ci = nc - 1 - 2 * pr - (1 - odd)
                sl = pl.ds(pl.multiple_of(ci * DN_CHUNK, DN_CHUNK), DN_CHUNK)
                _, vjp = jax.vjp(functools.partial(_dn_step, odd=odd), u_s[sl, :], w_s[sl, :], at_s[sl, :],
                                 qd_s[sl, :], kd_s[sl, :], cd_s[sl, :], st_ref[ci])
                du, dw, dat, dqd, dkd, dcd, dstate = vjp((do_s[sl, :], dstate))
                u_s[sl, :], w_s[sl, :], at_s[sl, :], qd_s[sl, :], kd_s[sl, :], cd_s[sl, :] = du, dw, dat, dqd, dkd, dcd
            return dstate

        lax.fori_loop(0, nc // 2, chunk_pair, jnp.zeros((HD, HD), F32))

        def inv_load(it, slot):
            rows = pl.ds(pl.multiple_of(b * s + it * DN_SUPER, DN_SUPER), DN_SUPER)
            cols = pl.ds(pl.multiple_of(head * DN_SUPER, DN_SUPER), DN_SUPER)
            return pltpu.make_async_copy(inv_hbm.at[rows, cols], inv_buf.at[slot], inv_sems.at[slot])

        def local_bwd(it, _):
            slot = lax.rem(it, 2)
            inv_load(it, slot).wait()

            @pl.when(it + 1 < s // DN_SUPER)
            def _():
                inv_load(it + 1, 1 - slot).start()

            sl = pl.ds(pl.multiple_of(it * DN_SUPER, DN_SUPER), DN_SUPER)
            ins = [ref[sl, :] for ref in (q_s, k_s, v_s, bb_s, gb_s)]
            du, dw, dat, dqd, dkd, dcd = [ref[sl, :] for ref in local_refs]
            _, vjp = jax.vjp(lambda *a: _dn_local(*a, known_inv=inv_buf[slot])[:6], *ins)
            dq, dk, dv, dbb, dgb = vjp((du, dw, _attn_unpairs(dat), dqd, dkd, dcd))
            q_s[sl, :], k_s[sl, :], v_s[sl, :] = dq, dk, dv
            bb_s[sl, :] = jnp.broadcast_to(jnp.sum(dbb, axis=1, keepdims=True), (DN_SUPER, HD))
            gb_s[sl, :] = jnp.broadcast_to(jnp.sum(dgb, axis=1, keepdims=True), (DN_SUPER, HD))
            return 0

        inv_load(0, 0).start()
        lax.fori_loop(0, s // DN_SUPER, local_bwd, 0)

        dq, dk, dv = q_s[...], k_s[...], v_s[...]
        qs, ks, rq, rk = p["qs"], p["ks"], p["rq"], p["rk"]
        dqs = (HD ** -0.5) * (rq * dq - qs * (rq * rq * rq) * jnp.sum(dq * qs, axis=1, keepdims=True))
        dks = rk * dk - ks * (rk * rk * rk) * jnp.sum(dk * ks, axis=1, keepdims=True)
        dqr_ref[...] = _conv_silu_bwd(qr_ref[...], cq_ref[...], p["qc"], dqs, dcq_ref)
        dkr_ref[...] = _conv_silu_bwd(kr_ref[...], ck_ref[...], p["kc"], dks, dck_ref)
        dvr_ref[...] = _conv_silu_bwd(vr_ref[...], cv_ref[...], p["vc"], dv, dcv_ref)

        dbeta, dg = bb_s[:, 0:1], gb_s[:, 0:1]
        beta = p["beta"]
        db_logit = dbeta * beta * (1.0 - beta)
        da = dg * p["neg_ea"] * _sigmoid(p["a_in"])
        lane = lax.broadcasted_iota(jnp.int32, (s, LANE), 1)
        dab_ref[...] = jnp.where(lane == head, da, 0.0) + jnp.where(lane == NH + head, db_logit, 0.0)
        dpar_ref[0:1, :] += jnp.broadcast_to(jnp.sum(dg * p["g"], axis=0, keepdims=True), (1, LANE))
        dpar_ref[1:2, :] += jnp.broadcast_to(jnp.sum(da, axis=0, keepdims=True), (1, LANE))

    out = pl.BlockSpec((s, HD), lambda h, b: (b, h))
    in_blk = pl.BlockSpec((s, HD), lambda h, b: (b, h), pipeline_mode=ONE_BUF)
    cblk = pl.BlockSpec((DN_CONV_W, HD), lambda h, b: (0, h))
    sds = jax.ShapeDtypeStruct((nb * s, BW), F32)
    csds = jax.ShapeDtypeStruct((DN_CONV_W, BW), F32)
    exchange = (_all_to_all_parts, len(gx), gnl) if gx else None
    res = pl.pallas_call(
        _with_exchange(body, 20, 10, 15, exchange, (NH, nb)), grid=(NH, nb),
        in_specs=[col(CB_DNQ), col(CB_DNK), col(CB_DNV), col(CB_DNZ),
                  pl.BlockSpec((s, LANE), lambda h, b: (b, 0), pipeline_mode=ONE_BUF),
                  conv(0), conv(NH), conv(2 * NH), pl.BlockSpec((8, LANE), lambda h, b: (0, 0)),
                  pl.BlockSpec((1, HD), lambda h, b: (0, 0)), in_blk,
                  pl.BlockSpec((None, None, nc, HD, HD), lambda h, b: (b, h, 0, 0, 0), pipeline_mode=ONE_BUF), in_blk]
        + [HBM_SPEC] * (7 + len(gx)),
        out_specs=[out, out, out, out, pl.BlockSpec((None, s, LANE), lambda h, b: (h, b, 0)), cblk, cblk, cblk,
                   pl.BlockSpec((None, 8, LANE), lambda h, b: (h, 0, 0)), pl.BlockSpec((1, HD), lambda h, b: (0, 0))]
        + [HBM_SPEC] * len(gx),
        out_shape=[sds, sds, sds, sds, jax.ShapeDtypeStruct((NH, nb * s, LANE), F32), csds, csds, csds,
                   jax.ShapeDtypeStruct((NH, 8, LANE), F32), jax.ShapeDtypeStruct((1, HD), F32)]
        + _all_to_all_shapes(gx, gnl),
        scratch_shapes=[pltpu.VMEM((s, HD), F32)] * 12 + [pltpu.SemaphoreType.DMA((6,)),
                                                           pltpu.VMEM((2, DN_SUPER, DN_SUPER), F32),
                                                           pltpu.SemaphoreType.DMA((2,))]
        + (_comm_sems(len(gx), 7) if gx else []),
        compiler_params=_cp(("arbitrary", "arbitrary"), DN_BWD_VMEM), name=name)(
            proj, proj, proj, proj, ab, conv_w, conv_w, conv_w, par, gain, o_pre, states, dy, *local, *gx)
    return tuple(res[:10]) + (list(res[10:]),)


def _sum_heads(x, *, name):
    nh, t, c = x.shape
    tm = _pick(t, (1024, 512, 256, 128))

    def body(x_ref, o_ref):
        o_ref[...] = (x_ref[0] + x_ref[1] + x_ref[2] + x_ref[3]).astype(BF16)

    return pl.pallas_call(
        body, grid=(t // tm,), in_specs=[pl.BlockSpec((nh, tm, c), lambda i: (0, i, 0))],
        out_specs=pl.BlockSpec((tm, c), lambda i: (i, 0)), out_shape=jax.ShapeDtypeStruct((t, c), BF16),
        compiler_params=_cp(("parallel",)), name=name)(x)


MERGE_TM = 256


def _merge_fwd(x, proj, yp, yd, ys, b_gate, wb, wo, *, name):
    t, d = x.shape
    tm = _pick(t, (MERGE_TM, 128))

    def body(x_ref, g0_ref, g1_ref, g2_ref, yp_ref, yd_ref, ys_ref, bg_ref, wb_ref, wo_ref, o_ref):
        merged = jnp.zeros((tm, d), F32)
        for n, (g_ref, y_ref) in enumerate(((g0_ref, yp_ref), (g1_ref, yd_ref), (g2_ref, ys_ref))):
            gate = _sigmoid(g_ref[...] + bg_ref[:, n * d:(n + 1) * d])
            merged = merged + gate * _bdot(y_ref[...], wb_ref[n])
        o_ref[...] = x_ref[...] + _bdot(merged, wo_ref[...])

    row = pl.BlockSpec((tm, d), lambda i: (i, 0))
    yblk = pl.BlockSpec((tm, BW), lambda i: (i, 0))

    def gl(n):
        return pl.BlockSpec((tm, d), lambda i: (i, CB_GATE + n))

    return pl.pallas_call(
        body, grid=(t // tm,),
        in_specs=[row, gl(0), gl(1), gl(2), yblk, yblk, yblk, pl.BlockSpec((1, 3 * d), lambda i: (0, 0)),
                  pl.BlockSpec((3, BW, d), lambda i: (0, 0, 0)), pl.BlockSpec((d, d), lambda i: (0, 0))],
        out_specs=row, out_shape=jax.ShapeDtypeStruct((t, d), F32),
        compiler_params=_cp(("parallel",)), name=name)(x, proj, proj, proj, yp, yd, ys, b_gate, wb, wo)


def _merge_bwd(proj, yp, yd, ys, b_gate, wb, wo, dx, *, name):
    t, d = dx.shape
    tm = _pick(t, (MERGE_TM, 128))

    def body(g0_ref, g1_ref, g2_ref, yp_ref, yd_ref, ys_ref, bg_ref, wb_ref, wo_ref, dx_ref,
             dyp_ref, dyd_ref, dys_ref, dgl_ref, mg_ref, dxh_ref, dbd_ref, dbg_ref):
        dxh = dx_ref[...].astype(BF16)
        dxh_ref[...] = dxh
        dmerged = _bdot(dxh, wo_ref[...], NT_DIMS)
        merged = jnp.zeros((tm, d), F32)

        @pl.when(pl.program_id(0) == 0)
        def _():
            dbg_ref[...] = jnp.zeros_like(dbg_ref)

        for n, (g_ref, y_ref, dy_ref) in enumerate(((g0_ref, yp_ref, dyp_ref), (g1_ref, yd_ref, dyd_ref),
                                                    (g2_ref, ys_ref, dys_ref))):
            gate = _sigmoid(g_ref[...] + bg_ref[:, n * d:(n + 1) * d])
            bd = _bdot(y_ref[...], wb_ref[n])
            merged = merged + gate * bd
            dgl = dmerged * bd * gate * (1.0 - gate)
            dgl_ref[:, n * d:(n + 1) * d] = dgl.astype(BF16)
            dbg_ref[:, n * d:(n + 1) * d] += jnp.sum(dgl, axis=0, keepdims=True)
            dbd = (dmerged * gate).astype(BF16)
            dbd_ref[n] = dbd
            dy_ref[...] = _bdot(dbd, wb_ref[n], NT_DIMS)
        mg_ref[...] = merged.astype(BF16)

    row = pl.BlockSpec((tm, d), lambda i: (i, 0))
    yblk = pl.BlockSpec((tm, BW), lambda i: (i, 0))
    bgv = pl.BlockSpec((1, 3 * d), lambda i: (0, 0))

    def gl(n):
        return pl.BlockSpec((tm, d), lambda i: (i, CB_GATE + n))

    ysds = jax.ShapeDtypeStruct((t, BW), F32)
    return pl.pallas_call(
        body, grid=(t // tm,),
        in_specs=[gl(0), gl(1), gl(2), yblk, yblk, yblk, bgv,
                  pl.BlockSpec((3, BW, d), lambda i: (0, 0, 0)), pl.BlockSpec((d, d), lambda i: (0, 0)), row],
        out_specs=[yblk, yblk, yblk, pl.BlockSpec((tm, 3 * d), lambda i: (i, 0)), row, row,
                   pl.BlockSpec((3, tm, d), lambda i: (0, i, 0)), bgv],
        out_shape=[ysds, ysds, ysds, jax.ShapeDtypeStruct((t, 3 * d), BF16), jax.ShapeDtypeStruct((t, d), BF16),
                   jax.ShapeDtypeStruct((t, d), BF16), jax.ShapeDtypeStruct((3, t, d), BF16),
                   jax.ShapeDtypeStruct((1, 3 * d), F32)],
        compiler_params=_cp(("arbitrary",)), name=name)(proj, proj, proj, yp, yd, ys, b_gate, wb, wo, dx)


def _loss_head(x, g, target, *, name):
    t, d = x.shape
    tm = _pick(t, (512, 256, 128))

    def body(x_ref, g_ref, t_ref, dx_ref, dg_ref, loss_ref):
        xhat, rstd = _rms_stats(x_ref[...])
        err = xhat * g_ref[...] - t_ref[...]
        dx, dg = _rms_bwd_vals(err * (1.0 / d), xhat, rstd, g_ref[...])
        dx_ref[...] = dx

        @pl.when(pl.program_id(0) == 0)
        def _():
            dg_ref[...] = jnp.zeros_like(dg_ref)
            loss_ref[...] = jnp.zeros_like(loss_ref)

        dg_ref[...] += dg
        part = jnp.sum(jnp.sum(err * err, axis=1, keepdims=True), axis=0, keepdims=True) * (0.5 / d)
        loss_ref[...] += jnp.broadcast_to(part, (1, LANE))

    row = pl.BlockSpec((tm, d), lambda i: (i, 0))
    vec = pl.BlockSpec((1, d), lambda i: (0, 0))
    return pl.pallas_call(
        body, grid=(t // tm,), in_specs=[row, vec, row],
        out_specs=[row, vec, pl.BlockSpec((1, LANE), lambda i: (0, 0))],
        out_shape=[jax.ShapeDtypeStruct((t, d), F32), jax.ShapeDtypeStruct((1, d), F32),
                   jax.ShapeDtypeStruct((1, LANE), F32)],
        compiler_params=_cp(("arbitrary",)), name=name)(x, g.reshape(1, d), target)


def _adamw(w, g, m, v, *, name):
    rows, cols = w.shape
    fits = [c for c in (1024, 704, 512, 352, 256, 128, 64, 32, 16, 8) if c * cols * 4 * 14 <= VMEM_LIMIT // 2]
    tr = _pick(rows, fits)
    c1 = 1.0 / (1.0 - ADAM_B1 ** ADAM_STEP)
    c2 = 1.0 / (1.0 - ADAM_B2 ** ADAM_STEP)

    def body(w_ref, g_ref, m_ref, v_ref, d_ref, nm_ref, nv_ref):
        g = g_ref[...]
        nm = ADAM_B1 * m_ref[...] + (1.0 - ADAM_B1) * g
        nv = ADAM_B2 * v_ref[...] + (1.0 - ADAM_B2) * (g * g)
        nm_ref[...] = nm
        nv_ref[...] = nv
        d_ref[...] = -ADAM_LR * ((nm * c1) / (jnp.sqrt(nv * c2) + ADAM_EPS) + ADAM_WD * w_ref[...])

    blk = pl.BlockSpec((tr, cols), lambda i: (i, 0))
    sds = jax.ShapeDtypeStruct((rows, cols), F32)
    return pl.pallas_call(
        body, grid=(rows // tr,), in_specs=[blk] * 4, out_specs=[blk] * 3, out_shape=[sds] * 3,
        compiler_params=_cp(("parallel",)), name=name)(w, g, m, v)


MESH_ID = pl.DeviceIdType.MESH
HBM_SPEC = pl.BlockSpec(memory_space=pl.ANY)
OTHER_CHIPS = ((1, 0), (0, 1), (1, 1))


def _at_slot(ref, nl, slot):
    return ref.at[(slice(None),) * nl + (slot,)]


def _slotted(shape, nl, slots):
    return tuple(shape[:nl]) + (slots,) + tuple(shape[nl:])


def _flip(v, f):
    return 1 - v if f else v


def _comm_call(body, n, out_shapes, n_remote, args, name):
    return pl.pallas_call(
        body, out_shape=out_shapes, in_specs=[HBM_SPEC] * len(args), out_specs=[HBM_SPEC] * len(out_shapes),
        scratch_shapes=[pltpu.SemaphoreType.DMA((n * n_remote,)), pltpu.SemaphoreType.DMA((n * n_remote,)),
                        pltpu.SemaphoreType.DMA((n * 4,))],
        compiler_params=pltpu.CompilerParams(has_side_effects=True), name=name)(*args)


def _gather(xs, nls, *, name):
    n = len(xs)

    def body(*refs):
        start, finish = _gather_parts(refs[:n], refs[n:2 * n], nls, *refs[2 * n:])
        start()
        finish()

    return _comm_call(body, n, _gather_shapes(xs, nls), 7, xs, name)


def _gather_shapes(xs, nls):
    return [jax.ShapeDtypeStruct(_slotted(v.shape, nl, N_DEV), v.dtype) for v, nl in zip(xs, nls)]


def _comm_sems(n, n_remote):
    return [pltpu.SemaphoreType.DMA((n * n_remote,)), pltpu.SemaphoreType.DMA((n * n_remote,)),
            pltpu.SemaphoreType.DMA((n * 4,))]


def _gather_parts(x_refs, o_refs, nls, send_sems, recv_sems, local_sems):
    n = len(x_refs)
    x, y, c = lax.axis_index("x"), lax.axis_index("y"), lax.axis_index("c")
    me, sibling = (x, y, c), (x, y, 1 - c)
    chips = [(_flip(x, fx), _flip(y, fy)) for fx, fy in OTHER_CHIPS]

    def copy(a, k, block, to, src=None):
        dst = _at_slot(o_refs[a], nls[a], 4 * block[0] + 2 * block[1] + block[2])
        return pltpu.make_async_remote_copy(
            src_ref=dst if src is None else src, dst_ref=dst, send_sem=send_sems.at[a * 7 + k],
            recv_sem=recv_sems.at[a * 7 + k], device_id=to, device_id_type=MESH_ID)

    def mine(a):
        return pltpu.make_async_copy(x_refs[a], _at_slot(o_refs[a], nls[a], 4 * x + 2 * y + c), local_sems.at[a])

    def first(a):
        return ([copy(a, 0, me, sibling, src=x_refs[a])]
                + [copy(a, 1 + j, me, (*chip, c), src=x_refs[a]) for j, chip in enumerate(chips)])

    def start():
        for a in range(n):
            mine(a).start()
            for cp in first(a):
                cp.start()

    def finish():
        passed = []
        for j, chip in enumerate(chips):
            for a in range(n):
                copy(a, 1 + j, (*chip, c), me).wait_recv()
                passed.append(copy(a, 4 + j, (*chip, c), sibling))
                passed[-1].start()
        for a in range(n):
            copy(a, 0, sibling, me).wait_recv()
            for j, chip in enumerate(chips):
                copy(a, 4 + j, (*chip, 1 - c), me).wait_recv()
        for a in range(n):
            for cp in first(a):
                cp.wait_send()
        for cp in passed:
            cp.wait_send()
        for a in range(n):
            mine(a).wait()

    return start, finish


ALL_FLIPS = ((0, 0, 1), (0, 1, 0), (0, 1, 1), (1, 0, 0), (1, 0, 1), (1, 1, 0), (1, 1, 1))


def _all_to_all_parts(g_refs, r_refs, nls, send_sems, recv_sems, local_sems):
    del local_sems
    n = len(g_refs)
    x, y, c = lax.axis_index("x"), lax.axis_index("y"), lax.axis_index("c")

    def copies():
        out = []
        for a in range(n):
            for k, (fx, fy, fc) in enumerate(ALL_FLIPS):
                p = (_flip(x, fx), _flip(y, fy), _flip(c, fc))
                out.append(pltpu.make_async_remote_copy(
                    src_ref=_at_slot(g_refs[a], nls[a], 4 * p[0] + 2 * p[1] + p[2]), dst_ref=_at_slot(r_refs[a], nls[a], k),
                    send_sem=send_sems.at[a * 7 + k], recv_sem=recv_sems.at[a * 7 + k], device_id=p,
                    device_id_type=MESH_ID))
        return out

    def start():
        for cp in copies():
            cp.start()

    def finish():
        cps = copies()
        for cp in cps:
            cp.wait_recv()
        for cp in cps:
            cp.wait_send()

    return start, finish


def _all_to_all_shapes(gs, nls):
    return [jax.ShapeDtypeStruct(_slotted(v.shape[:nl] + v.shape[nl + 1:], nl, 7), v.dtype) for v, nl in zip(gs, nls)]


def _scatter_pair(gs, nls, *, name):
    n = len(gs)

    def body(*refs):
        g_refs, got_refs, (send_sems, recv_sems, _) = refs[:n], refs[n:2 * n], refs[2 * n:]
        x, y, c = lax.axis_index("x"), lax.axis_index("y"), lax.axis_index("c")
        remote = []
        for a in range(n):
            for q in range(4):
                rc = pltpu.make_async_remote_copy(
                    src_ref=_at_slot(g_refs[a], nls[a], 2 * q + 1 - c), dst_ref=_at_slot(got_refs[a], nls[a], q),
                    send_sem=send_sems.at[a * 4 + q], recv_sem=recv_sems.at[a * 4 + q], device_id=(x, y, 1 - c),
                    device_id_type=MESH_ID)
                rc.start()
                remote.append(rc)
        for rc in remote:
            rc.wait_recv()
        for rc in remote:
            rc.wait_send()

    outs = [jax.ShapeDtypeStruct(_slotted(v.shape[:nl] + v.shape[nl + 1:], nl, 4), v.dtype) for v, nl in zip(gs, nls)]
    return _comm_call(body, n, outs, 4, gs, name)


def _scatter_chips(ps, nls, *, name):
    n = len(ps)

    def body(*refs):
        p_refs, r_refs, (send_sems, recv_sems, _) = refs[:n], refs[n:2 * n], refs[2 * n:]
        x, y, c = lax.axis_index("x"), lax.axis_index("y"), lax.axis_index("c")
        remote = []
        for a in range(n):
            for k, (fx, fy) in enumerate(OTHER_CHIPS):
                tx, ty = _flip(x, fx), _flip(y, fy)
                rc = pltpu.make_async_remote_copy(
                    src_ref=_at_slot(p_refs[a], nls[a], 2 * tx + ty), dst_ref=_at_slot(r_refs[a], nls[a], k),
                    send_sem=send_sems.at[a * 3 + k], recv_sem=recv_sems.at[a * 3 + k], device_id=(tx, ty, c),
                    device_id_type=MESH_ID)
                rc.start()
                remote.append(rc)
        for rc in remote:
            rc.wait_recv()
        for rc in remote:
            rc.wait_send()

    outs = [jax.ShapeDtypeStruct(_slotted(v.shape[:nl] + v.shape[nl + 1:], nl, 3), v.dtype) for v, nl in zip(ps, nls)]
    return _comm_call(body, n, outs, 3, ps, name)


def _pair_add(g, got, core, *, name):
    rows, cols = g.shape[-2:]
    lf = math.prod(got.shape[:-3])
    tr = _pick(rows, (1024, 512, 352, 256, 128))

    def body(core_ref, g_ref, got_ref, o_ref):
        o_ref[...] = (g_ref[...].astype(F32) + got_ref[...].astype(F32)).astype(BF16)

    blk = pl.BlockSpec((None, None, tr, cols), lambda i, q, j, core_ref: (i, q, j, 0))
    out = pl.pallas_call(
        body, grid_spec=pltpu.PrefetchScalarGridSpec(
            num_scalar_prefetch=1, grid=(lf, 4, rows // tr),
            in_specs=[pl.BlockSpec((None, None, None, tr, cols), lambda i, q, j, core_ref: (i, q, core_ref[0], j, 0)),
                      blk], out_specs=blk),
        out_shape=jax.ShapeDtypeStruct((lf, 4, rows, cols), BF16),
        compiler_params=_cp(("parallel", "parallel", "parallel")), name=name)(
            core, g.reshape(lf, 4, 2, rows, cols), got.reshape(lf, 4, rows, cols))
    return out.reshape(got.shape)


def _sum_adamw(p, r, own, w, m, v, layer, prev, *, name):
    shape = w.shape[1:]
    rows, cols = shape[-2:]
    lf = math.prod(shape[:-2])
    np_, nk = p.shape[-3], r.shape[-3]
    fits = [c for c in (1024, 512, 352, 256, 128, 64, 32, 16) if c * cols * (7 * 4 + (nk + 1) * 2) * 2 <= VMEM_LIMIT // 2]
    tr = _pick(rows, fits)
    c1 = 1.0 / (1.0 - ADAM_B1 ** ADAM_STEP)
    c2 = 1.0 / (1.0 - ADAM_B2 ** ADAM_STEP)

    def body(own_ref, p_ref, r_ref, w_ref, m_ref, v_ref, *rest):
        g_ref, d_ref, nm_ref, nv_ref = rest[-4:]
        g = p_ref[...].astype(F32)
        for k in range(nk):
            g = g + r_ref[k].astype(F32)
        g_ref[...] = g
        nm = ADAM_B1 * m_ref[...] + (1.0 - ADAM_B1) * g
        nv = ADAM_B2 * v_ref[...] + (1.0 - ADAM_B2) * (g * g)
        nm_ref[...] = nm
        nv_ref[...] = nv
        d_ref[...] = -ADAM_LR * ((nm * c1) / (jnp.sqrt(nv * c2) + ADAM_EPS) + ADAM_WD * w_ref[...])

    wblk = pl.BlockSpec((None, None, tr, cols), lambda i, j, own_ref: (layer, i, j, 0))
    full = (w.shape[0], lf, rows, cols)
    sds = jax.ShapeDtypeStruct(full, F32)
    prev = [] if prev is None else [a.reshape(full) for a in prev]
    outs = pl.pallas_call(
        body, grid_spec=pltpu.PrefetchScalarGridSpec(
            num_scalar_prefetch=1, grid=(lf, rows // tr),
            in_specs=[pl.BlockSpec((None, None, tr, cols), lambda i, j, own_ref: (i, own_ref[0], j, 0)),
                      pl.BlockSpec((None, nk, tr, cols), lambda i, j, own_ref: (i, 0, j, 0))] + [wblk] * 3
            + [HBM_SPEC] * len(prev),
            out_specs=[wblk] * 4),
        out_shape=[sds] * 4, input_output_aliases={6 + i: i for i in range(len(prev))},
        compiler_params=_cp(("parallel", "parallel")), name=name)(
            own, p.reshape(lf, np_, rows, cols), r.reshape(lf, nk, rows, cols), w.reshape(full), m.reshape(full),
            v.reshape(full), *prev)
    return [o.reshape(w.shape) for o in outs]


def _sum_slots(x, *, name):
    nd, rows, cols = x.shape
    tr = _pick(rows, (512, 256, 128, 64, 32, 16, 8))

    def body(x_ref, o_ref):
        acc = x_ref[0].astype(F32)
        for j in range(1, nd):
            acc = acc + x_ref[j].astype(F32)
        o_ref[...] = acc

    return pl.pallas_call(
        body, grid=(rows // tr,), in_specs=[pl.BlockSpec((nd, tr, cols), lambda i: (0, i, 0))],
        out_specs=pl.BlockSpec((tr, cols), lambda i: (i, 0)), out_shape=jax.ShapeDtypeStruct((rows, cols), F32),
        compiler_params=_cp(("parallel",)), name=name)(x)


def _pad_rows(a, mult=8):
    r = (-a.shape[0]) % mult
    return jnp.pad(a, ((0, r), (0, 0))) if r else a


def _flat128(a):
    f = a.reshape(-1)
    return jnp.pad(f, (0, (-f.shape[0]) % LANE)).reshape(-1, LANE)


def _unshard(gathered, shape, axis):
    g = gathered.reshape((N_DEV,) + tuple(shape))
    g = jnp.moveaxis(g, 0, axis)
    full = list(shape)
    full[axis] *= N_DEV
    return g.reshape(full)


def _col_shards(full):
    rows, cols = full.shape
    return jnp.moveaxis(full.reshape(rows, N_DEV, cols // N_DEV), 1, 0)


BIG = (("ffn_w_gate", 2), ("ffn_w_up", 2), ("ffn_w_down", 2), ("w_in", 1), ("w_branch", 2), ("w_out", 1))


def kernel(x, ffn_norm, ffn_w_gate, ffn_w_up, ffn_w_down, mix_norm, w_in, b_gate, pool_w, pool_scale, dn_conv, dn_A_log, dn_dt_bias, dn_out_norm, w_branch, w_out, final_norm, loss_target, m_ffn_norm, m_ffn_w_gate, m_ffn_w_up, m_ffn_w_down, m_mix_norm, m_w_in, m_b_gate, m_pool_w, m_pool_scale, m_dn_conv, m_dn_A_log, m_dn_dt_bias, m_dn_out_norm, m_w_branch, m_w_out, m_final_norm, v_ffn_norm, v_ffn_w_gate, v_ffn_w_up, v_ffn_w_down, v_mix_norm, v_w_in, v_b_gate, v_pool_w, v_pool_scale, v_dn_conv, v_dn_A_log, v_dn_dt_bias, v_dn_out_norm, v_w_branch, v_w_out, v_final_norm):
    wts = dict(ffn_norm=ffn_norm, ffn_w_gate=ffn_w_gate, ffn_w_up=ffn_w_up, ffn_w_down=ffn_w_down, mix_norm=mix_norm,
               w_in=w_in, b_gate=b_gate, pool_w=pool_w, pool_scale=pool_scale, dn_conv=dn_conv, dn_A_log=dn_A_log,
               dn_dt_bias=dn_dt_bias, dn_out_norm=dn_out_norm, w_branch=w_branch, w_out=w_out, final_norm=final_norm)
    mom = dict(ffn_norm=m_ffn_norm, ffn_w_gate=m_ffn_w_gate, ffn_w_up=m_ffn_w_up, ffn_w_down=m_ffn_w_down,
               mix_norm=m_mix_norm, w_in=m_w_in, b_gate=m_b_gate, pool_w=m_pool_w, pool_scale=m_pool_scale,
               dn_conv=m_dn_conv, dn_A_log=m_dn_A_log, dn_dt_bias=m_dn_dt_bias, dn_out_norm=m_dn_out_norm,
               w_branch=m_w_branch, w_out=m_w_out, final_norm=m_final_norm)
    var = dict(ffn_norm=v_ffn_norm, ffn_w_gate=v_ffn_w_gate, ffn_w_up=v_ffn_w_up, ffn_w_down=v_ffn_w_down,
               mix_norm=v_mix_norm, w_in=v_w_in, b_gate=v_b_gate, pool_w=v_pool_w, pool_scale=v_pool_scale,
               dn_conv=v_dn_conv, dn_A_log=v_dn_A_log, dn_dt_bias=v_dn_dt_bias, dn_out_norm=v_dn_out_norm,
               w_branch=v_w_branch, w_out=v_w_out, final_norm=v_final_norm)
    nb, s, d = x.shape
    t = nb * s
    me = 4 * lax.axis_index("x") + 2 * lax.axis_index("y") + lax.axis_index("c")

    big = [n for n, _ in BIG]
    nls = [nl - 1 for _, nl in BIG]
    shards = lambda l: [wts[n][l].astype(BF16) for n in big]
    small_sh = jnp.concatenate([_flat128(ffn_norm), _flat128(dn_conv)], axis=0)
    ffn3 = big[:3]
    *pre0, small_g = _gather([wts[n][0, 0].astype(BF16) for n in ffn3] + [small_sh], [0] * 4, name="gather_weights")
    rest0 = [wts[n][0, 1].astype(BF16) for n in ffn3] + [wts[n][0].astype(BF16) for n in big[3:]]
    rest0_nls = [0] * 3 + nls[3:]
    full = [None] * DEPTH

    def mixer_weights(l):
        w_in_full = jnp.moveaxis(full[l]["w_in"], 0, 1).reshape(d, -1)
        w_main = jnp.concatenate([w_in_full[:, :AB_LO], w_in_full[:, AB_HI:]], axis=1)
        w_ab = jnp.pad(w_in_full[:, AB_LO:AB_HI], ((0, 0), (0, LANE - (AB_HI - AB_LO))))
        wb = jnp.moveaxis(full[l]["w_branch"], 1, 2).reshape(3, BW, d)
        return w_main, w_ab, wb, full[l]["w_out"].reshape(d, d)

    nfr = ffn_norm.size // LANE
    ffn_norm_full = _unshard(small_g[:, :nfr], ffn_norm.shape, 2)
    dn_conv_full = _unshard(small_g[:, nfr:], dn_conv.shape, 2)
    pool_w_h = pool_w.astype(BF16)

    xs = x.reshape(t, d)
    saved = []
    for l in range(DEPTH):
        sv = dict(x0=xs)
        if l == 0:
            xs, a0, b0, got = _ffn_fwd(xs, ffn_norm_full[0, 0], *pre0, name="ffn_fwd_gather", gather=(rest0, rest0_nls))
            full[0] = dict(zip(ffn3, zip(pre0, got[:3])), **dict(zip(big[3:], got[3:])))
        else:
            xs, a0, b0, _ = _ffn_fwd(xs, ffn_norm_full[l, 0], full[l]["ffn_w_gate"][0], full[l]["ffn_w_up"][0],
                                     full[l]["ffn_w_down"][0], name="ffn_fwd")
        sv["ab0"] = (a0, b0)
        sv["x1"] = xs
        w_main, w_ab, wb, wo = mixer_weights(l)
        h = _rms_fwd(xs, mix_norm[l], name="mix_rms")
        proj = _mm(h, w_main, name="proj")
        ab = _mm(h, w_ab, name="proj_ab")
        par = jnp.pad(jnp.stack([dn_A_log[l], dn_dt_bias[l]]), ((0, 6), (0, LANE - NH)))
        gain = dn_out_norm[l].reshape(1, HD)
        psc = pool_scale[l].reshape(1, BW)
        yp = _pool_fwd(proj, pool_w_h[l], psc, nb, s, name="pool_fwd")
        yd, o_pre, states, dn_local, gat = _dn_fwd(proj, ab, dn_conv_full[l], par, gain, nb, s,
                                         name="dn_fwd" if l == DEPTH - 1 else "dn_fwd_gather",
                                         gather=(shards(l + 1), nls) if l < DEPTH - 1 else None)
        if l < DEPTH - 1:
            full[l + 1] = dict(zip(big, gat))
        ys, sb_ctr = _sb_fwd(proj, nb, s, name="sb_fwd")
        bg = b_gate[l].reshape(1, 3 * d)
        xs = _merge_fwd(xs, proj, yp, yd, ys, bg, wb, wo, name="merge_fwd")
        sv.update(x2=xs, h=h, proj=proj, ab=ab, par=par, gain=gain, psc=psc, yp=yp, yd=yd, ys=ys, sb_ctr=sb_ctr, o_pre=o_pre,
                  states=states, dn_local=dn_local, bg=bg, w_main=w_main, w_ab=w_ab, wb=wb, wo=wo)
        xs, a1, b1, _ = _ffn_fwd(xs, ffn_norm_full[l, 1], full[l]["ffn_w_gate"][1], full[l]["ffn_w_up"][1],
                                 full[l]["ffn_w_down"][1], name="ffn_fwd")
        sv["ab1"] = (a1, b1)
        saved.append(sv)

    dx, g_final, loss_row = _loss_head(xs, final_norm, loss_target.reshape(t, d), name="loss_head")
    loss = lax.psum(loss_row[0, 0], ("x", "y", "c"))

    gw = {n: [None] * DEPTH for n in ("ffn_norm", "ffn_w_gate", "ffn_w_up", "ffn_w_down", "mix_norm", "w_in", "b_gate",
                                      "pool_w", "pool_scale", "dn_conv", "dn_A_log", "dn_dt_bias", "dn_out_norm",
                                      "w_branch", "w_out")}

    me_i = me.astype(jnp.int32).reshape(1)
    updated = {n: None for n in big}
    pending = None

    def finish_layer(l, own_blocks, arrived, own_slot):
        for n, p, r in zip(big, own_blocks, arrived):
            updated[n] = _sum_adamw(p, r, own_slot, wts[n], mom[n], var[n], l, updated[n], name=f"adamw_{n}_{l}")

    def ffn_back(l, i, x_in, dy):
        dxi, dg, hb, dyh, da, db, sact = _ffn_bwd(x_in, ffn_norm_full[l, i], full[l]["ffn_w_gate"][i],
                                                  full[l]["ffn_w_up"][i], full[l]["ffn_w_down"][i],
                                                  *saved[l][f"ab{i}"], dy, name="ffn_bwd")
        return dxi, dg, (_mm_slots(hb, da, name="dw_gate_up"), _mm_slots(hb, db, name="dw_gate_up"),
                         _mm_slots(sact, dyh, name="dw_down"))

    for l in reversed(range(DEPTH)):
        sv = saved[l]
        dx, dg1, (dwg1, dwu1, dwd1) = ffn_back(l, 1, sv["x2"], dx)
        dyp, dyd, dys, dgl, merged, dxh, dbd, dbg = _merge_bwd(sv["proj"], sv["yp"], sv["yd"], sv["ys"], sv["bg"],
                                                               sv["wb"], sv["wo"], dx, name="merge_bwd")
        gw["w_out"][l] = _mm(merged, dxh, ta=True, out_dtype=BF16, name="dw_out").reshape(N_DEV, d // N_DEV, d)
        gw["w_branch"][l] = jnp.stack([_col_shards(_mm(y, dbd[n], ta=True, out_dtype=BF16, name="dw_branch"))
                                       for n, y in enumerate((sv["yp"], sv["yd"], sv["ys"]))])
        gw["b_gate"][l] = dbg.reshape(3 * d)
        du, dpw, dps = _pool_bwd(sv["proj"], pool_w_h[l], sv["psc"], dyp, nb, s, name="pool_bwd")
        gw["pool_w"][l], gw["pool_scale"][l] = dpw, dps.reshape(BW)
        dqr, dkr, dvr, dz, dab4, dcq, dck, dcv, dpar, dgain, arrived = _dn_bwd(
            sv["proj"], sv["ab"], dn_conv_full[l], sv["par"], sv["gain"], sv["o_pre"], sv["states"], sv["dn_local"],
            dyd, nb, s, name="dn_bwd_scatter" if pending is not None else "dn_bwd",
            scatter=([gw[n][pending] for n in big], nls) if pending is not None else None)
        if pending is not None:
            finish_layer(pending, [gw[n][pending] for n in big], arrived, me_i)
        gw["dn_conv"][l] = jnp.concatenate([dcq, dck, dcv], axis=1)
        gw["dn_A_log"][l], gw["dn_dt_bias"][l], gw["dn_out_norm"][l] = dpar[:, 0, 0], dpar[:, 1, 0], dgain.reshape(HD)
        dsq, dsk, dsv = _sb_bwd(sv["proj"], sv["sb_ctr"], dys, nb, s, name="sb_bwd")
        dab = _sum_heads(dab4, name="sum_heads")
        dproj = jnp.concatenate([du.astype(BF16), dqr.astype(BF16), dkr.astype(BF16), dvr.astype(BF16),
                                 dz.astype(BF16), dsq.astype(BF16), dsk.astype(BF16), dsv.astype(BF16), dgl], axis=1)
        dw_main = _mm(sv["h"], dproj, ta=True, out_dtype=BF16, name="dw_in")
        dw_ab = _mm(sv["h"], dab, ta=True, out_dtype=BF16, name="dw_ab")
        gw["w_in"][l] = _col_shards(jnp.concatenate([dw_main[:, :AB_LO], dw_ab[:, :AB_HI - AB_LO],
                                                     dw_main[:, AB_LO:]], axis=1))
        dh_main = _mm(dproj, sv["w_main"], tb=True, name="dh_mix")
        dh_ab = _mm(dab, sv["w_ab"], tb=True, name="dh_mix_ab")
        dx, dgm = _rms_bwd(sv["x1"], mix_norm[l], dh_main, dh_ab, dx, name="mix_rms_bwd")
        gw["mix_norm"][l] = dgm.reshape(d)
        dx, dg0, (dwg0, dwu0, dwd0) = ffn_back(l, 0, sv["x0"], dx)
        gw["ffn_norm"][l] = jnp.stack([dg0.reshape(d), dg1.reshape(d)])
        gw["ffn_w_gate"][l] = jnp.stack([dwg0, dwg1])
        gw["ffn_w_up"][l] = jnp.stack([dwu0, dwu1])
        gw["ffn_w_down"][l] = jnp.stack([dwd0, dwd1])
        pending = l
    grad_x = dx.reshape(nb, s, d)

    core = lax.axis_index("c").astype(jnp.int32).reshape(1)
    chip = (2 * lax.axis_index("x") + lax.axis_index("y")).astype(jnp.int32).reshape(1)
    last = [gw[n][0] for n in big]
    got = _scatter_pair(last, nls, name="scatter_grads_pair")
    chip_sums = [_pair_add(g, b, core, name="add_pair_" + n) for n, g, b in zip(big, last, got)]
    finish_layer(0, chip_sums, _scatter_chips(chip_sums, nls, name="scatter_grads_chips"), chip)
    grads, delta, new_m, new_v = ({n: updated[n][i] for n in big} for i in range(4))
    gw = {n: jnp.stack(v) for n, v in gw.items() if n not in big}
    gw["final_norm"] = g_final.reshape(d)

    small = ("ffn_norm", "mix_norm", "b_gate", "pool_w", "pool_scale", "dn_conv", "dn_A_log", "dn_dt_bias",
             "dn_out_norm", "final_norm")
    sp = _pad_rows(jnp.concatenate([_flat128(gw[n]) for n in small], axis=0))
    ssum = _sum_slots(_gather([sp], [0], name="gather_small_grads")[0], name="sum_small_grads")
    off = 0
    for n in small:
        r = -(-gw[n].size // LANE)
        g = ssum[off:off + r].reshape(-1)[:gw[n].size].reshape(gw[n].shape)
        off += r
        if n in ("ffn_norm", "dn_conv"):
            w = wts[n].shape[2]
            g = lax.dynamic_slice_in_dim(g, me * w, w, axis=2)
        grads[n] = g

    pk = lambda src: _pad_rows(jnp.concatenate([_flat128(src[n]) for n in small], axis=0))
    dl, nm, nv = _adamw(pk(wts), pk(grads), pk(mom), pk(var), name="adamw_small")
    off = 0
    for n in small:
        r = -(-wts[n].size // LANE)
        for dst, src in ((delta, dl), (new_m, nm), (new_v, nv)):
            dst[n] = src[off:off + r].reshape(-1)[:wts[n].size].reshape(wts[n].shape)
        off += r

    order = ("ffn_norm", "ffn_w_gate", "ffn_w_up", "ffn_w_down", "mix_norm", "w_in", "b_gate", "pool_w", "pool_scale",
             "dn_conv", "dn_A_log", "dn_dt_bias", "dn_out_norm", "w_branch", "w_out", "final_norm")
    return (loss, grad_x, *[grads[n] for n in order], *[delta[n] for n in order], *[new_m[n] for n in order],
            *[new_v[n] for n in order])
```

```python
import functools
import math

import jax
import jax.numpy as jnp
from jax import lax
from jax.experimental import pallas as pl
from jax.experimental.pallas import tpu as pltpu

F32, BF16 = jnp.float32, jnp.bfloat16
D_MODEL, D_FF, DEPTH = 1024, 2816, 4
BW = 512
HD = 128
NH = 4
DN_CHUNK = 64
EPS = 1e-6
N_DEV = 8
LANE = 128
CB_POOL, CB_DNQ, CB_DNK, CB_DNV, CB_DNZ, CB_SBQ, CB_SBK, CB_SBV = 0, 4, 8, 12, 16, 20, 24, 28
CB_GATE = 4
P_MAIN = 7168
AB_LO, AB_HI = 2560, 2568
ADAM_LR, ADAM_B1, ADAM_B2, ADAM_EPS, ADAM_WD, ADAM_STEP = 0.001, 0.9, 0.999, 1e-08, 0.01, 10
VMEM_LIMIT = 56 * 1024 * 1024
HIGHEST = lax.Precision.HIGHEST
NT_DIMS = (((1,), (1,)), ((), ()))
TN_DIMS = (((0,), (0,)), ((), ()))
NN_DIMS = (((1,), (0,)), ((), ()))


def _cp(dims=None, vmem=VMEM_LIMIT):
    return pltpu.CompilerParams(dimension_semantics=dims, vmem_limit_bytes=vmem)


def _pick(n, cands):
    for c in cands:
        if n % c == 0:
            return c
    return n


def _bdot(a, b, dims=NN_DIMS):
    return lax.dot_general(a.astype(BF16), b.astype(BF16), dims, preferred_element_type=F32)


def _hdot(a, b, dims=NN_DIMS):
    return lax.dot_general(a, b, dims, precision=lax.Precision.HIGH, preferred_element_type=F32)


def _split_dot(x, m01):
    hi = x.astype(BF16)
    lo = (x - hi.astype(F32)).astype(BF16)
    return (lax.dot_general(hi, m01, NN_DIMS, preferred_element_type=F32)
            + lax.dot_general(lo, m01, NN_DIMS, preferred_element_type=F32))


def _sigmoid(x):
    return 1.0 / (1.0 + jnp.exp(-x))


def _log_sigmoid(x):
    return jnp.minimum(x, 0.0) - jnp.log1p(jnp.exp(-jnp.abs(x)))


def _softplus(x):
    return jnp.maximum(x, 0.0) + jnp.log1p(jnp.exp(-jnp.abs(x)))


def _shift_down(x, k):
    r = lax.broadcasted_iota(jnp.int32, x.shape, 0)
    return jnp.where(r >= k, pltpu.roll(x, k, 0), 0.0)


def _shift_up(x, k):
    n = x.shape[0]
    r = lax.broadcasted_iota(jnp.int32, x.shape, 0)
    return jnp.where(r < n - k, pltpu.roll(x, n - k, 0), 0.0)


def _mm(a, b, *, ta=False, tb=False, out_dtype=F32, name):
    (kk, m) = a.shape if ta else a.shape[::-1]
    (k2, n) = b.shape[::-1] if tb else b.shape
    assert kk == k2, (a.shape, b.shape, ta, tb)
    bm = _pick(m, (1024, 512, 256, 128))
    bn = _pick(n, (1024, 1408, 512, 256, 128))
    bk = _pick(kk, (512, 256, 128))
    nk = kk // bk
    dims = (((0 if ta else 1,), (1 if tb else 0,)), ((), ()))

    def body(a_ref, b_ref, o_ref, acc_ref):
        k = pl.program_id(2)

        @pl.when(k == 0)
        def _():
            acc_ref[...] = jnp.zeros_like(acc_ref)

        acc_ref[...] += lax.dot_general(a_ref[...].astype(BF16), b_ref[...].astype(BF16), dims,
                                        preferred_element_type=F32)

        @pl.when(k == nk - 1)
        def _():
            o_ref[...] = acc_ref[...].astype(out_dtype)

    a_spec = (pl.BlockSpec((bk, bm), lambda i, j, k: (k, i)) if ta else pl.BlockSpec((bm, bk), lambda i, j, k: (i, k)))
    b_spec = (pl.BlockSpec((bn, bk), lambda i, j, k: (j, k)) if tb else pl.BlockSpec((bk, bn), lambda i, j, k: (k, j)))
    return pl.pallas_call(
        body, grid=(m // bm, n // bn, nk), in_specs=[a_spec, b_spec],
        out_specs=pl.BlockSpec((bm, bn), lambda i, j, k: (i, j)),
        out_shape=jax.ShapeDtypeStruct((m, n), out_dtype),
        scratch_shapes=[pltpu.VMEM((bm, bn), F32)],
        compiler_params=_cp(("parallel", "parallel", "arbitrary")), name=name)(a, b)


def _mm_slots(a, b, *, name):
    a3, b3 = a.ndim == 3, b.ndim == 3
    ns = a.shape[0] if a3 else b.shape[0]
    m, t = a.shape[-2:]
    n = b.shape[-1]
    bk = _pick(t, (512, 256, 128))
    nk = t // bk

    def body(a_ref, b_ref, o_ref, acc_ref):
        k = pl.program_id(0)

        @pl.when(k == 0)
        def _():
            acc_ref[...] = jnp.zeros_like(acc_ref)

        for s in range(ns):
            acc_ref[s] += _bdot(a_ref[s] if a3 else a_ref[...], b_ref[s] if b3 else b_ref[...])

        @pl.when(k == nk - 1)
        def _():
            o_ref[...] = acc_ref[...].astype(BF16)

    a_spec = pl.BlockSpec((ns, m, bk), lambda k: (0, 0, k)) if a3 else pl.BlockSpec((m, bk), lambda k: (0, k))
    b_spec = pl.BlockSpec((ns, bk, n), lambda k: (0, k, 0)) if b3 else pl.BlockSpec((bk, n), lambda k: (k, 0))
    return pl.pallas_call(
        body, grid=(nk,), in_specs=[a_spec, b_spec], out_specs=pl.BlockSpec((ns, m, n), lambda k: (0, 0, 0)),
        out_shape=jax.ShapeDtypeStruct((ns, m, n), BF16), scratch_shapes=[pltpu.VMEM((ns, m, n), F32)],
        compiler_params=_cp(("arbitrary",)), name=name)(a, b)


def _rms_stats(x):
    rstd = lax.rsqrt(jnp.mean(x * x, axis=-1, keepdims=True) + EPS)
    return x * rstd, rstd


def _rms_bwd_vals(dh, xhat, rstd, g):
    dxh = dh * g
    dx = rstd * (dxh - xhat * jnp.mean(dxh * xhat, axis=-1, keepdims=True))
    return dx, jnp.sum(dh * xhat, axis=0, keepdims=True)


def _rms_fwd(x, g, *, name):
    t, d = x.shape
    tm = _pick(t, (512, 256, 128))

    def body(x_ref, g_ref, h_ref):
        xhat, _ = _rms_stats(x_ref[...])
        h_ref[...] = (xhat * g_ref[...]).astype(BF16)

    return pl.pallas_call(
        body, grid=(t // tm,),
        in_specs=[pl.BlockSpec((tm, d), lambda i: (i, 0)), pl.BlockSpec((1, d), lambda i: (0, 0))],
        out_specs=pl.BlockSpec((tm, d), lambda i: (i, 0)), out_shape=jax.ShapeDtypeStruct((t, d), BF16),
        compiler_params=_cp(("parallel",)), name=name)(x, g.reshape(1, d))


def _rms_bwd(x, g, dh_a, dh_b, dres, *, name):
    t, d = x.shape
    tm = _pick(t, (512, 256, 128))

    def body(x_ref, g_ref, dha_ref, dhb_ref, dres_ref, dx_ref, dg_ref):
        xhat, rstd = _rms_stats(x_ref[...])
        dx, dg = _rms_bwd_vals(dha_ref[...] + dhb_ref[...], xhat, rstd, g_ref[...])
        dx_ref[...] = dres_ref[...] + dx

        @pl.when(pl.program_id(0) == 0)
        def _():
            dg_ref[...] = jnp.zeros_like(dg_ref)

        dg_ref[...] += dg

    row = pl.BlockSpec((tm, d), lambda i: (i, 0))
    vec = pl.BlockSpec((1, d), lambda i: (0, 0))
    return pl.pallas_call(
        body, grid=(t // tm,), in_specs=[row, vec, row, row, row], out_specs=[row, vec],
        out_shape=[jax.ShapeDtypeStruct((t, d), F32), jax.ShapeDtypeStruct((1, d), F32)],
        compiler_params=_cp(("arbitrary",)), name=name)(x, g.reshape(1, d), dh_a, dh_b, dres)


FFN_TM = 512


def _ffn_fwd(x, g, wg, wu, wd, *, name, gather=None):
    t, d = x.shape
    nf, _, fc = wg.shape
    tm = _pick(t, (FFN_TM, 256, 128))
    gx, gnl = gather if gather else ([], [])

    def body(x_ref, g_ref, wg_ref, wu_ref, wd_ref, o_ref, a_ref, b_ref, h_ref, acc_ref):
        j = pl.program_id(1)

        @pl.when(j == 0)
        def _():
            xhat, _ = _rms_stats(x_ref[...])
            h_ref[...] = (xhat * g_ref[...]).astype(BF16)
            acc_ref[...] = jnp.zeros_like(acc_ref)

        h = h_ref[...]
        a = _bdot(h, wg_ref[...])
        b = _bdot(h, wu_ref[...])
        a_ref[...] = a.astype(BF16)
        b_ref[...] = b.astype(BF16)
        s = a * _sigmoid(a) * b
        acc_ref[...] += _bdot(s, wd_ref[...])

        @pl.when(j == nf - 1)
        def _():
            o_ref[...] = x_ref[...] + 0.5 * acc_ref[...]

    row = pl.BlockSpec((tm, d), lambda i, j: (i, 0))
    grid = (t // tm, nf)
    exchange = (_gather_parts, len(gx), gnl) if gx else None
    res = pl.pallas_call(
        _with_exchange(body, 5, 3, 2, exchange, grid), grid=grid,
        in_specs=[row, pl.BlockSpec((1, d), lambda i, j: (0, 0)),
                  pl.BlockSpec((None, d, fc), lambda i, j: (j, 0, 0)), pl.BlockSpec((None, d, fc), lambda i, j: (j, 0, 0)),
                  pl.BlockSpec((None, fc, d), lambda i, j: (j, 0, 0))] + [HBM_SPEC] * len(gx),
        out_specs=[row, pl.BlockSpec((None, tm, fc), lambda i, j: (j, i, 0)),
                   pl.BlockSpec((None, tm, fc), lambda i, j: (j, i, 0))] + [HBM_SPEC] * len(gx),
        out_shape=[jax.ShapeDtypeStruct((t, d), F32), jax.ShapeDtypeStruct((nf, t, fc), BF16),
                   jax.ShapeDtypeStruct((nf, t, fc), BF16)] + _gather_shapes(gx, gnl),
        scratch_shapes=[pltpu.VMEM((tm, d), BF16), pltpu.VMEM((tm, d), F32)] + (_comm_sems(len(gx), 7) if gx else []),
        compiler_params=_cp(("arbitrary", "arbitrary")), name=name)(x, g.reshape(1, d), wg, wu, wd, *gx)
    return res[0], res[1], res[2], list(res[3:])


def _ffn_bwd(x, g, wg, wu, wd, a_pre, b_pre, dy, *, name):
    t, d = x.shape
    nf, _, fc = wg.shape
    tm = _pick(t, (FFN_TM, 256, 128))

    def body(x_ref, g_ref, wg_ref, wu_ref, wd_ref, a_ref, b_ref, dy_ref,
             dx_ref, dg_ref, ht_ref, dyh_ref, da_ref, db_ref, st_ref, acc_ref):
        i, j = pl.program_id(0), pl.program_id(1)

        @pl.when(j == 0)
        def _():
            xhat, _ = _rms_stats(x_ref[...])
            ht_ref[...] = (xhat * g_ref[...]).T.astype(BF16)
            dyh_ref[...] = (0.5 * dy_ref[...]).astype(BF16)
            acc_ref[...] = jnp.zeros_like(acc_ref)

        a = a_ref[...].astype(F32)
        b = b_ref[...].astype(F32)
        sg = _sigmoid(a)
        silu = a * sg
        st_ref[...] = (silu * b).T.astype(BF16)
        ds = _bdot(dyh_ref[...], wd_ref[...], NT_DIMS)
        da = (ds * b * (sg * (1.0 + a * (1.0 - sg)))).astype(BF16)
        db = (ds * silu).astype(BF16)
        da_ref[...] = da
        db_ref[...] = db
        acc_ref[...] += _bdot(da, wg_ref[...], NT_DIMS) + _bdot(db, wu_ref[...], NT_DIMS)

        @pl.when((i == 0) & (j == 0))
        def _():
            dg_ref[...] = jnp.zeros_like(dg_ref)

        @pl.when(j == nf - 1)
        def _():
            xhat, rstd = _rms_stats(x_ref[...])
            dx, dg = _rms_bwd_vals(acc_ref[...], xhat, rstd, g_ref[...])
            dx_ref[...] = dy_ref[...] + dx
            dg_ref[...] += dg

    row = pl.BlockSpec((tm, d), lambda i, j: (i, 0))
    vec = pl.BlockSpec((1, d), lambda i, j: (0, 0))
    fblk = pl.BlockSpec((None, tm, fc), lambda i, j: (j, i, 0))
    return pl.pallas_call(
        body, grid=(t // tm, nf),
        in_specs=[row, vec, pl.BlockSpec((None, d, fc), lambda i, j: (j, 0, 0)),
                  pl.BlockSpec((None, d, fc), lambda i, j: (j, 0, 0)), pl.BlockSpec((None, fc, d), lambda i, j: (j, 0, 0)),
                  fblk, fblk, row],
        out_specs=[row, vec, pl.BlockSpec((d, tm), lambda i, j: (0, i)), row, fblk, fblk,
                   pl.BlockSpec((None, fc, tm), lambda i, j: (j, 0, i))],
        out_shape=[jax.ShapeDtypeStruct((t, d), F32), jax.ShapeDtypeStruct((1, d), F32),
                   jax.ShapeDtypeStruct((d, t), BF16), jax.ShapeDtypeStruct((t, d), BF16),
                   jax.ShapeDtypeStruct((nf, t, fc), BF16), jax.ShapeDtypeStruct((nf, t, fc), BF16),
                   jax.ShapeDtypeStruct((nf, fc, t), BF16)],
        scratch_shapes=[pltpu.VMEM((tm, d), F32)],
        compiler_params=_cp(("arbitrary", "arbitrary")), name=name)(x, g.reshape(1, d), wg, wu, wd, a_pre, b_pre, dy)


def _pool_core(u, grp):
    s = u.shape[0]
    w2 = u + _shift_down(u, 1)
    w4 = w2 + _shift_down(w2, 2)
    w8 = w4 + _shift_down(w4, 4)
    w16 = w8 + _shift_down(w8, 8)
    wsum = jnp.where(grp == 0, w2, jnp.where(grp == 1, w4, jnp.where(grp == 2, w8, w16)))
    win = jnp.left_shift(2, grp).astype(F32)
    t1 = (lax.broadcasted_iota(jnp.int32, (s, 1), 0) + 1).astype(F32)
    inv = 1.0 / jnp.minimum(t1, win)
    return wsum * inv - u, inv


def _pool_fwd(proj, pool_w, pool_scale, nb, s, *, name):
    def body(u_ref, w_ref, sc_ref, y_ref):
        pooled, _ = _pool_core(u_ref[...], pl.program_id(0))
        y_ref[...] = _bdot(pooled, w_ref[...]) * sc_ref[...]

    return pl.pallas_call(
        body, grid=(NH, nb),
        in_specs=[pl.BlockSpec((s, HD), lambda g, b: (b, CB_POOL + g)),
                  pl.BlockSpec((None, HD, HD), lambda g, b: (g, 0, 0)), pl.BlockSpec((1, HD), lambda g, b: (0, g))],
        out_specs=pl.BlockSpec((s, HD), lambda g, b: (b, g)),
        out_shape=jax.ShapeDtypeStruct((nb * s, BW), F32),
        compiler_params=_cp(("parallel", "parallel")), name=name)(proj, pool_w, pool_scale)


def _pool_bwd(proj, pool_w, pool_scale, dy, nb, s, *, name):
    def body(u_ref, w_ref, sc_ref, dy_ref, du_ref, dw_ref, dsc_ref):
        grp, b = pl.program_id(0), pl.program_id(1)
        pooled, inv = _pool_core(u_ref[...], grp)
        mixed = _bdot(pooled, w_ref[...])
        dy = dy_ref[...]
        dmixed = dy * sc_ref[...]
        dpooled = _bdot(dmixed, w_ref[...], NT_DIMS)
        r = dpooled * inv
        v2 = r + _shift_up(r, 1)
        v4 = v2 + _shift_up(v2, 2)
        v8 = v4 + _shift_up(v4, 4)
        v16 = v8 + _shift_up(v8, 8)
        vsum = jnp.where(grp == 0, v2, jnp.where(grp == 1, v4, jnp.where(grp == 2, v8, v16)))
        du_ref[...] = vsum - dpooled

        @pl.when(b == 0)
        def _():
            dw_ref[...] = jnp.zeros_like(dw_ref)
            dsc_ref[...] = jnp.zeros_like(dsc_ref)

        dw_ref[...] += _bdot(pooled, dmixed, TN_DIMS)
        dsc_ref[...] += jnp.sum(dy * mixed, axis=0, keepdims=True)

    return pl.pallas_call(
        body, grid=(NH, nb),
        in_specs=[pl.BlockSpec((s, HD), lambda g, b: (b, CB_POOL + g)),
                  pl.BlockSpec((None, HD, HD), lambda g, b: (g, 0, 0)), pl.BlockSpec((1, HD), lambda g, b: (0, g)),
                  pl.BlockSpec((s, HD), lambda g, b: (b, g))],
        out_specs=[pl.BlockSpec((s, HD), lambda g, b: (b, g)), pl.BlockSpec((None, HD, HD), lambda g, b: (g, 0, 0)),
                   pl.BlockSpec((1, HD), lambda g, b: (0, g))],
        out_shape=[jax.ShapeDtypeStruct((nb * s, BW), F32), jax.ShapeDtypeStruct((NH, HD, HD), F32),
                   jax.ShapeDtypeStruct((1, BW), F32)],
        compiler_params=_cp(("arbitrary", "arbitrary")), name=name)(proj, pool_w, pool_scale, dy)


SB_BLK = 128


SB_G = 4
SB_KG = SB_G * SB_BLK
SB_Q = 2 * SB_BLK


def _sb_block(qb, kg, q0, k0, diagonal):
    z = _bdot(qb, kg, NT_DIMS) * (HD ** -0.5)
    lsz = _log_sigmoid(z)
    if not diagonal:
        return lsz, lsz - z, None
    row = lax.broadcasted_iota(jnp.int32, z.shape, 0) + q0
    col = lax.broadcasted_iota(jnp.int32, z.shape, 1) + k0
    causal = col < row
    return lsz, jnp.where(causal, lsz - z, 0.0), causal


def _keep(causal, x):
    return x if causal is None else jnp.where(causal, x, 0.0)


def _sub(x, m):
    return x[:, m * SB_BLK:(m + 1) * SB_BLK]


def _sb_tails(lnm, after, ct):
    hi = lnm.astype(BF16)
    lo = (lnm - hi.astype(F32)).astype(BF16)
    tails = [None] * SB_G
    for m in reversed(range(SB_G)):
        tails[m] = (lax.dot_general(_sub(hi, m), after, NN_DIMS, preferred_element_type=F32)
                    + lax.dot_general(_sub(lo, m), after, NN_DIMS, preferred_element_type=F32)) + ct
        ct = ct + jnp.sum(_sub(lnm, m), axis=1, keepdims=True)
    ones = jnp.ones((8, lnm.shape[1]), BF16)
    rows = (lax.dot_general(ones, hi, NT_DIMS, preferred_element_type=F32)
            + lax.dot_general(ones, lo, NT_DIMS, preferred_element_type=F32))
    return jnp.concatenate(tails, axis=1), rows, ct


def _tri01(lower):
    r = lax.broadcasted_iota(jnp.int32, (SB_BLK, SB_BLK), 0)
    c = lax.broadcasted_iota(jnp.int32, (SB_BLK, SB_BLK), 1)
    return jnp.where((r < c) if lower else (r > c), 1.0, 0.0).astype(BF16)


def _split3(x):
    hi = x.astype(BF16)
    mid = (x - hi.astype(F32)).astype(BF16)
    lo = (x - hi.astype(F32) - mid.astype(F32)).astype(BF16)
    return hi, mid, lo


def _rows_to_cols(rows):
    eighth = jnp.full((8, LANE), 0.125, BF16)
    return sum(lax.dot_general(p, eighth, TN_DIMS, preferred_element_type=F32) for p in _split3(rows))


def _sb_fwd(proj, nb, s, *, name):
    nq = s // SB_Q
    ng = s // SB_KG

    def body(q_ref, k_ref, v_ref, o_ref, ctr_ref):
        after = _tri01(False)

        def qblock(i, _):
            q0 = pl.multiple_of(i * SB_Q, SB_Q)
            qb = q_ref[pl.ds(q0, SB_Q), :]

            def kgroup(g, carry, diagonal):
                acc, ct, ctr = carry
                k0 = pl.multiple_of(g * SB_KG, SB_KG)
                lsz, lnm, causal = _sb_block(qb, k_ref[pl.ds(k0, SB_KG), :], q0, k0, diagonal)
                ctr_ref[i * ng + g] = ctr
                tail, rows, ct = _sb_tails(lnm, after, ct)
                w = _keep(causal, jnp.exp(lsz + tail))
                return acc + _bdot(w, v_ref[pl.ds(k0, SB_KG), :]), ct, ctr + rows

            gd = (i * SB_Q) // SB_KG
            carry = kgroup(gd, (jnp.zeros((SB_Q, HD), F32), jnp.zeros((SB_Q, 1), F32), jnp.zeros((8, SB_Q), F32)), True)
            acc, _, _ = lax.fori_loop(0, gd, lambda jj, c: kgroup(gd - 1 - jj, c, False), carry)
            o_ref[pl.ds(q0, SB_Q), :] = acc
            return 0

        lax.fori_loop(0, nq, qblock, 0)

    def col(cb):
        return pl.BlockSpec((s, HD), lambda b, h: (b, cb + h))

    return pl.pallas_call(
        body, grid=(nb, NH), in_specs=[col(CB_SBQ), col(CB_SBK), col(CB_SBV)],
        out_specs=[pl.BlockSpec((s, HD), lambda b, h: (b, h)),
                   pl.BlockSpec((None, None, nq * ng, 8, SB_Q), lambda b, h: (b, h, 0, 0, 0))],
        out_shape=[jax.ShapeDtypeStruct((nb * s, BW), F32), jax.ShapeDtypeStruct((nb, NH, nq * ng, 8, SB_Q), F32)],
        compiler_params=_cp(("parallel", "parallel")), name=name)(proj, proj, proj)


def _sb_bwd(proj, ctr, dy, nb, s, *, name, scatter=None):
    nq = s // SB_Q
    ng = s // SB_KG
    scale = HD ** -0.5
    gx, gnl = scatter if scatter else ([], [])

    def body(q_ref, k_ref, v_ref, ctr_ref, do_ref, dq_ref, dk_ref, dv_ref):
        after = _tri01(False)
        before = _tri01(True)
        dk_ref[...] = jnp.zeros_like(dk_ref)
        dv_ref[...] = jnp.zeros_like(dv_ref)

        def qblock(i, _):
            q0 = pl.multiple_of(i * SB_Q, SB_Q)
            qb = q_ref[pl.ds(q0, SB_Q), :]
            dob = do_ref[pl.ds(q0, SB_Q), :]

            def kgroup(g, carry, diagonal):
                dq, ce = carry
                k0 = pl.multiple_of(g * SB_KG, SB_KG)
                kg = k_ref[pl.ds(k0, SB_KG), :]
                vg = v_ref[pl.ds(k0, SB_KG), :]
                lsz, lnm, causal = _sb_block(qb, kg, q0, k0, diagonal)
                tail, _, _ = _sb_tails(lnm, after, _rows_to_cols(ctr_ref[i * ng + g])[:, 0:1])
                w = _keep(causal, jnp.exp(lsz + tail))
                e = _bdot(dob, vg, NT_DIMS) * w
                pres = []
                for m in range(SB_G):
                    pres.append(_split_dot(_sub(e, m), before) + ce)
                    ce = ce + jnp.sum(_sub(e, m), axis=1, keepdims=True)
                sig = jnp.exp(lsz)
                dz = _keep(causal, e * (1.0 - sig) - jnp.concatenate(pres, axis=1) * sig) * scale
                dk_ref[pl.ds(k0, SB_KG), :] += _bdot(dz, qb, TN_DIMS)
                dv_ref[pl.ds(k0, SB_KG), :] += _bdot(w, dob, TN_DIMS)
                return dq + _bdot(dz, kg), ce

            gd = (i * SB_Q) // SB_KG
            carry = lax.fori_loop(0, gd, lambda g, c: kgroup(g, c, False),
                                  (jnp.zeros((SB_Q, HD), F32), jnp.zeros((SB_Q, 1), F32)))
            dq, _ = kgroup(gd, carry, True)
            dq_ref[pl.ds(q0, SB_Q), :] = dq
            return 0

        lax.fori_loop(0, nq, qblock, 0)

    def col(cb):
        return pl.BlockSpec((s, HD), lambda b, h: (b, cb + h))

    out = pl.BlockSpec((s, HD), lambda b, h: (b, h))
    sds = jax.ShapeDtypeStruct((nb * s, BW), F32)
    exchange = (_all_to_all_parts, len(gx), gnl) if gx else None
    res = pl.pallas_call(
        _with_exchange(body, 5, 3, 0, exchange, (nb, NH)), grid=(nb, NH),
        in_specs=[col(CB_SBQ), col(CB_SBK), col(CB_SBV),
                  pl.BlockSpec((None, None, nq * ng, 8, SB_Q), lambda b, h: (b, h, 0, 0, 0)), out]
        + [HBM_SPEC] * len(gx),
        out_specs=[out, out, out] + [HBM_SPEC] * len(gx), out_shape=[sds, sds, sds] + _all_to_all_shapes(gx, gnl),
        scratch_shapes=_comm_sems(len(gx), 7) if gx else [],
        compiler_params=_cp(("arbitrary", "arbitrary")), name=name)(proj, proj, proj, ctr, dy, *gx)
    return res[0], res[1], res[2], list(res[3:])


def _make_cdot(dims, dims_da, dims_db, swap_a=False, swap_b=False):
    @jax.custom_vjp
    def f(a, b):
        return _bdot(a, b, dims)

    def fwd(a, b):
        return _bdot(a, b, dims), (a, b)

    def bwd(res, g):
        a, b = res
        da = _bdot(b, g, dims_da) if swap_a else _bdot(g, b, dims_da)
        db = _bdot(g, a, dims_db) if swap_b else _bdot(a, g, dims_db)
        return da, db

    f.defvjp(fwd, bwd)
    return f


_cdot = _make_cdot(NN_DIMS, NT_DIMS, TN_DIMS)
_cdot_nt = _make_cdot(NT_DIMS, NN_DIMS, TN_DIMS, swap_b=True)
_cdot_tn = _make_cdot(TN_DIMS, NT_DIMS, NN_DIMS, swap_a=True)


DN_SUPER = 4 * DN_CHUNK


@jax.custom_vjp
def _unit_lower_inverse(lmat):
    n = lmat.shape[0]
    steps = int(math.log2(DN_CHUNK))
    eye = jnp.where(lax.broadcasted_iota(jnp.int32, (n, n), 0) == lax.broadcasted_iota(jnp.int32, (n, n), 1), 1.0, 0.0)
    inv = eye - lmat
    pw = _hdot(lmat, lmat)
    for it in range(steps - 1):
        inv = inv + _hdot(inv, pw)
        if it < steps - 2:
            pw = _hdot(pw, pw)
    return inv


def _unit_lower_inverse_fwd(lmat):
    inv = _unit_lower_inverse(lmat)
    return inv, inv


def _unit_lower_inverse_bwd(inv, g):
    return (-_hdot(_hdot(inv, g, TN_DIMS), inv, NT_DIMS),)


_unit_lower_inverse.defvjp(_unit_lower_inverse_fwd, _unit_lower_inverse_bwd)


@jax.custom_vjp
def _known_inverse(lmat, inv):
    return inv


def _known_inverse_fwd(lmat, inv):
    return inv, inv


def _known_inverse_bwd(inv, g):
    return -_hdot(_hdot(inv, g, TN_DIMS), inv, NT_DIMS), jnp.zeros_like(inv)


_known_inverse.defvjp(_known_inverse_fwd, _known_inverse_bwd)


def _dn_local(q, k, v, bb, gb, known_inv=None):
    n = q.shape[0]
    r = lax.broadcasted_iota(jnp.int32, (n, n), 0)
    cc = lax.broadcasted_iota(jnp.int32, (n, n), 1)
    shift = int(math.log2(DN_CHUNK))
    same = lax.shift_right_logical(r, shift) == lax.shift_right_logical(cc, shift)
    incl = jnp.where(same, jnp.where(r >= cc, 1.0, 0.0), 0.0)
    strict = jnp.where(same, jnp.where(r > cc, 1.0, 0.0), 0.0)
    gc = _hdot(incl, gb)
    gc_row = _hdot(jnp.full((n, HD), 1.0 / HD, F32), gc, NT_DIMS)
    diff = jnp.concatenate([gc] * (n // HD), axis=1) - gc_row
    decay = incl * jnp.exp(diff * incl)
    kb = k * bb
    lmat = _cdot_nt(kb, k) * (strict * decay)
    egc = jnp.exp(gc)
    inv = _unit_lower_inverse(lmat) if known_inv is None else _known_inverse(lmat, known_inv)
    u = _hdot(inv, v * bb)
    w = _hdot(inv, kb * egc)
    attn = _cdot_nt(q, k) * decay
    gl = _hdot(jnp.where(same, 1.0, 0.0), gb)
    return u, w, attn, q * egc, k * jnp.exp(gl - gc), jnp.exp(gl), inv


def _attn_pairs(attn):
    return jnp.concatenate([attn[:HD, :HD], attn[HD:, HD:]], axis=0)


def _attn_unpairs(a):
    z = jnp.zeros((HD, HD), F32)
    return jnp.concatenate([jnp.concatenate([a[:HD], z], axis=1), jnp.concatenate([z, a[HD:]], axis=1)], axis=0)


def _dn_step(u, w, a, qd, kd, cdrows, state, odd):
    v_new = u - _cdot(w, state)
    z = jnp.zeros_like(v_new)
    o = _cdot(qd, state) + _cdot(a, jnp.concatenate([z, v_new] if odd else [v_new, z], axis=0))
    return o, state * jnp.mean(cdrows, axis=0, keepdims=True) + _cdot_tn(kd, v_new)


def _dn_local_pass(fn, s, ins, outs):
    def step(it, _):
        sl = pl.ds(pl.multiple_of(it * DN_SUPER, DN_SUPER), DN_SUPER)
        res = fn(*[ref[sl, :] for ref in ins])
        for ref, val in zip(outs, res):
            ref[sl, :] = val
        return 0

    lax.fori_loop(0, s // DN_SUPER, step, 0)


def _lane_pick(row, idx):
    lane = lax.broadcasted_iota(jnp.int32, row.shape, 1)
    return jnp.sum(jnp.where(lane == idx, row, 0.0), axis=1, keepdims=True)


def _col_pick(x, idx):
    lane = lax.broadcasted_iota(jnp.int32, x.shape, 1)
    return jnp.sum(jnp.where(lane == idx, x, 0.0), axis=1, keepdims=True)


def _conv_silu(x, w):
    xc = (w[3:4, :] * x + w[2:3, :] * _shift_down(x, 1) + w[1:2, :] * _shift_down(x, 2)
          + w[0:1, :] * _shift_down(x, 3))
    return xc * _sigmoid(xc), xc


def _conv_silu_bwd(x, w, xc, dxs, dw_ref):
    sg = _sigmoid(xc)
    dxc = dxs * (sg * (1.0 + xc * (1.0 - sg)))
    dx = (w[3:4, :] * dxc + w[2:3, :] * _shift_up(dxc, 1) + w[1:2, :] * _shift_up(dxc, 2)
          + w[0:1, :] * _shift_up(dxc, 3))
    dw_ref[3:4, :] += jnp.sum(dxc * x, axis=0, keepdims=True)
    dw_ref[2:3, :] += jnp.sum(dxc * _shift_down(x, 1), axis=0, keepdims=True)
    dw_ref[1:2, :] += jnp.sum(dxc * _shift_down(x, 2), axis=0, keepdims=True)
    dw_ref[0:1, :] += jnp.sum(dxc * _shift_down(x, 3), axis=0, keepdims=True)
    return dx


def _dn_prep(qr_ref, kr_ref, vr_ref, ab_ref, cq_ref, ck_ref, cv_ref, par_ref, head):
    qs, qc = _conv_silu(qr_ref[...], cq_ref[...])
    ks, kc = _conv_silu(kr_ref[...], ck_ref[...])
    vs, vc = _conv_silu(vr_ref[...], cv_ref[...])
    rq = lax.rsqrt(jnp.sum(qs * qs, axis=1, keepdims=True) + EPS)
    rk = lax.rsqrt(jnp.sum(ks * ks, axis=1, keepdims=True) + EPS)
    ab = ab_ref[...]
    a_in = _col_pick(ab, head) + _lane_pick(par_ref[1:2, :], head)
    beta = _sigmoid(_col_pick(ab, NH + head))
    neg_ea = -jnp.exp(_lane_pick(par_ref[0:1, :], head))
    g = neg_ea * _softplus(a_in)
    return dict(q=qs * rq * (HD ** -0.5), k=ks * rk, v=vs, beta=beta, g=g, qs=qs, ks=ks, qc=qc, kc=kc, vc=vc,
                rq=rq, rk=rk, a_in=a_in, neg_ea=neg_ea)


ONE_BUF = pl.Buffered(1)
DN_BWD_VMEM = 62 * 1024 * 1024


def _dn_specs(nb, s):
    def col(cb):
        return pl.BlockSpec((s, HD), lambda h, b: (b, cb + h), pipeline_mode=ONE_BUF)

    def conv(cb):
        return pl.BlockSpec((DN_CONV_W, HD), lambda h, b: (0, cb + h))

    return col, conv


DN_CONV_W = 4


def _with_exchange(body, n_in, n_out, n_scratch, exchange, grid):
    if exchange is None:
        return body
    parts_fn, n, nls = exchange

    def wrapped(*refs):
        ins, xs = refs[:n_in], refs[n_in:n_in + n]
        outs, os = refs[n_in + n:n_in + n + n_out], refs[n_in + n + n_out:n_in + 2 * n + n_out]
        rest = refs[n_in + 2 * n + n_out:]
        scratch, sems = rest[:n_scratch], rest[n_scratch:]
        pos = [pl.program_id(k) for k in range(len(grid))]
        first = functools.reduce(jnp.logical_and, [p == 0 for p in pos])
        last = functools.reduce(jnp.logical_and, [p == g - 1 for p, g in zip(pos, grid)])
        start, finish = parts_fn(xs, os, nls, *sems)
        pl.when(first)(start)
        body(*ins, *outs, *scratch)
        pl.when(last)(finish)

    return wrapped


def _dn_fwd(proj, ab, conv_w, par, gain, nb, s, *, name, gather=None):
    nc = s // DN_CHUNK
    col, conv = _dn_specs(nb, s)
    gx, gnl = gather if gather else ([], [])

    def body(qr_ref, kr_ref, vr_ref, z_ref, ab_ref, cq_ref, ck_ref, cv_ref, par_ref, gain_ref,
             y_ref, o_ref, st_ref, u_ref, w_ref, at_ref, qd_ref, kd_ref, cd_ref, inv_ref, q_s, k_s, v_s, bb_s, gb_s):
        p = _dn_prep(qr_ref, kr_ref, vr_ref, ab_ref, cq_ref, ck_ref, cv_ref, par_ref, pl.program_id(0))
        q_s[...], k_s[...], v_s[...] = p["q"], p["k"], p["v"]
        bb_s[...] = jnp.broadcast_to(p["beta"], (s, HD))
        gb_s[...] = jnp.broadcast_to(p["g"], (s, HD))
        def local(*args):
            u, w, attn, qd, kd, cd, inv = _dn_local(*args)
            return u, w, _attn_pairs(attn), qd, kd, cd, inv

        _dn_local_pass(local, s, [q_s, k_s, v_s, bb_s, gb_s], [u_ref, w_ref, at_ref, qd_ref, kd_ref, cd_ref, inv_ref])

        def chunk_pair(pi, state):
            for odd in (0, 1):
                ci = 2 * pi + odd
                sl = pl.ds(pl.multiple_of(ci * DN_CHUNK, DN_CHUNK), DN_CHUNK)
                st_ref[ci] = state
                o, state = _dn_step(u_ref[sl, :], w_ref[sl, :], at_ref[sl, :], qd_ref[sl, :], kd_ref[sl, :],
                                    cd_ref[sl, :], state, odd)
                o_ref[sl, :] = o
            return state

        lax.fori_loop(0, nc // 2, chunk_pair, jnp.zeros((HD, HD), F32))
        o = o_ref[...]
        z = z_ref[...]
        on = o * lax.rsqrt(jnp.mean(o * o, axis=1, keepdims=True) + EPS) * gain_ref[...]
        y_ref[...] = on * (z * _sigmoid(z))

    out = pl.BlockSpec((s, HD), lambda h, b: (b, h))
    sds = jax.ShapeDtypeStruct((nb * s, BW), F32)
    exchange = (_gather_parts, len(gx), gnl) if gx else None
    res = pl.pallas_call(
        _with_exchange(body, 10, 10, 5, exchange, (NH, nb)), grid=(NH, nb),
        in_specs=[col(CB_DNQ), col(CB_DNK), col(CB_DNV), col(CB_DNZ), pl.BlockSpec((s, LANE), lambda h, b: (b, 0)),
                  conv(0), conv(NH), conv(2 * NH), pl.BlockSpec((8, LANE), lambda h, b: (0, 0)),
                  pl.BlockSpec((1, HD), lambda h, b: (0, 0))] + [HBM_SPEC] * len(gx),
        out_specs=[out, out, pl.BlockSpec((None, None, nc, HD, HD), lambda h, b: (b, h, 0, 0, 0))] + [out] * 6
        + [pl.BlockSpec((s, DN_SUPER), lambda h, b: (b, h))] + [HBM_SPEC] * len(gx),
        out_shape=[sds, sds, jax.ShapeDtypeStruct((nb, NH, nc, HD, HD), F32)] + [sds] * 6
        + [jax.ShapeDtypeStruct((nb * s, NH * DN_SUPER), F32)] + _gather_shapes(gx, gnl),
        scratch_shapes=[pltpu.VMEM((s, HD), F32)] * 5 + (_comm_sems(len(gx), 7) if gx else []),
        compiler_params=_cp(("arbitrary", "arbitrary")), name=name)(
            proj, proj, proj, proj, ab, conv_w, conv_w, conv_w, par, gain, *gx)
    return res[0], res[1], res[2], list(res[3:10]), list(res[10:])


def _dn_bwd(proj, ab, conv_w, par, gain, o_pre, states, local, dy, nb, s, *, name, scatter=None):
    nc = s // DN_CHUNK
    col, conv = _dn_specs(nb, s)
    gx, gnl = scatter if scatter else ([], [])

    def body(qr_ref, kr_ref, vr_ref, z_ref, ab_ref, cq_ref, ck_ref, cv_ref, par_ref, gain_ref, o_ref, st_ref, dy_ref,
             u_hbm, w_hbm, at_hbm, qd_hbm, kd_hbm, cd_hbm, inv_hbm,
             dqr_ref, dkr_ref, dvr_ref, dz_ref, dab_ref, dcq_ref, dck_ref, dcv_ref, dpar_ref, dgain_ref,
             q_s, k_s, v_s, bb_s, gb_s, do_s, u_s, w_s, qd_s, kd_s, at_s, cd_s, load_sems, inv_buf, inv_sems):
        head, b = pl.program_id(0), pl.program_id(1)
        local_refs = [u_s, w_s, at_s, qd_s, kd_s, cd_s]
        loads = [pltpu.make_async_copy(src.at[pl.ds(pl.multiple_of(b * s, s), s), pl.ds(pl.multiple_of(head * HD, HD), HD)],
                                       dst, load_sems.at[i])
                 for i, (src, dst) in enumerate(zip((u_hbm, w_hbm, at_hbm, qd_hbm, kd_hbm, cd_hbm), local_refs))]
        for cp in loads:
            cp.start()
        p = _dn_prep(qr_ref, kr_ref, vr_ref, ab_ref, cq_ref, ck_ref, cv_ref, par_ref, head)
        q_s[...], k_s[...], v_s[...] = p["q"], p["k"], p["v"]
        bb_s[...] = jnp.broadcast_to(p["beta"], (s, HD))
        gb_s[...] = jnp.broadcast_to(p["g"], (s, HD))

        @pl.when(b == 0)
        def _():
            for ref in (dcq_ref, dck_ref, dcv_ref, dpar_ref):
                ref[...] = jnp.zeros_like(ref)

        @pl.when((b == 0) & (head == 0))
        def _():
            dgain_ref[...] = jnp.zeros_like(dgain_ref)

        o, z, dy = o_ref[...], z_ref[...], dy_ref[...]
        rstd = lax.rsqrt(jnp.mean(o * o, axis=1, keepdims=True) + EPS)
        ohat = o * rstd
        sgz = _sigmoid(z)
        dz_ref[...] = dy * (ohat * gain_ref[...]) * (sgz * (1.0 + z * (1.0 - sgz)))
        don = dy * (z * sgz)
        dgain_ref[...] += jnp.sum(don * ohat, axis=0, keepdims=True)
        dxh = don * gain_ref[...]
        do_s[...] = rstd * (dxh - ohat * jnp.mean(dxh * ohat, axis=1, keepdims=True))

        for cp in loads:
            cp.wait()

        def chunk_pair(pr, dstate):
            for odd in (1, 0):
                ci = nc - 1 - 2 * pr - (1 - odd)
                sl = pl.ds(pl.multiple_of(ci * DN_CHUNK, DN_CHUNK), DN_CHUNK)
                _, vjp = jax.vjp(functools.partial(_dn_step, odd=odd), u_s[sl, :], w_s[sl, :], at_s[sl, :],
                                 qd_s[sl, :], kd_s[sl, :], cd_s[sl, :], st_ref[ci])
                du, dw, dat, dqd, dkd, dcd, dstate = vjp((do_s[sl, :], dstate))
                u_s[sl, :], w_s[sl, :], at_s[sl, :], qd_s[sl, :], kd_s[sl, :], cd_s[sl, :] = du, dw, dat, dqd, dkd, dcd
            return dstate

        lax.fori_loop(0, nc // 2, chunk_pair, jnp.zeros((HD, HD), F32))

        def inv_load(it, slot):
            rows = pl.ds(pl.multiple_of(b * s + it * DN_SUPER, DN_SUPER), DN_SUPER)
            cols = pl.ds(pl.multiple_of(head * DN_SUPER, DN_SUPER), DN_SUPER)
            return pltpu.make_async_copy(inv_hbm.at[rows, cols], inv_buf.at[slot], inv_sems.at[slot])

        def local_bwd(it, _):
            slot = lax.rem(it, 2)
            inv_load(it, slot).wait()

            @pl.when(it + 1 < s // DN_SUPER)
            def _():
                inv_load(it + 1, 1 - slot).start()

            sl = pl.ds(pl.multiple_of(it * DN_SUPER, DN_SUPER), DN_SUPER)
            ins = [ref[sl, :] for ref in (q_s, k_s, v_s, bb_s, gb_s)]
            du, dw, dat, dqd, dkd, dcd = [ref[sl, :] for ref in local_refs]
            _, vjp = jax.vjp(lambda *a: _dn_local(*a, known_inv=inv_buf[slot])[:6], *ins)
            dq, dk, dv, dbb, dgb = vjp((du, dw, _attn_unpairs(dat), dqd, dkd, dcd))
            q_s[sl, :], k_s[sl, :], v_s[sl, :] = dq, dk, dv
            bb_s[sl, :] = jnp.broadcast_to(jnp.sum(dbb, axis=1, keepdims=True), (DN_SUPER, HD))
            gb_s[sl, :] = jnp.broadcast_to(jnp.sum(dgb, axis=1, keepdims=True), (DN_SUPER, HD))
            return 0

        inv_load(0, 0).start()
        lax.fori_loop(0, s // DN_SUPER, local_bwd, 0)

        dq, dk, dv = q_s[...], k_s[...], v_s[...]
        qs, ks, rq, rk = p["qs"], p["ks"], p["rq"], p["rk"]
        dqs = (HD ** -0.5) * (rq * dq - qs * (rq * rq * rq) * jnp.sum(dq * qs, axis=1, keepdims=True))
        dks = rk * dk - ks * (rk * rk * rk) * jnp.sum(dk * ks, axis=1, keepdims=True)
        dqr_ref[...] = _conv_silu_bwd(qr_ref[...], cq_ref[...], p["qc"], dqs, dcq_ref)
        dkr_ref[...] = _conv_silu_bwd(kr_ref[...], ck_ref[...], p["kc"], dks, dck_ref)
        dvr_ref[...] = _conv_silu_bwd(vr_ref[...], cv_ref[...], p["vc"], dv, dcv_ref)

        dbeta, dg = bb_s[:, 0:1], gb_s[:, 0:1]
        beta = p["beta"]
        db_logit = dbeta * beta * (1.0 - beta)
        da = dg * p["neg_ea"] * _sigmoid(p["a_in"])
        lane = lax.broadcasted_iota(jnp.int32, (s, LANE), 1)
        dab_ref[...] = jnp.where(lane == head, da, 0.0) + jnp.where(lane == NH + head, db_logit, 0.0)
        dpar_ref[0:1, :] += jnp.broadcast_to(jnp.sum(dg * p["g"], axis=0, keepdims=True), (1, LANE))
        dpar_ref[1:2, :] += jnp.broadcast_to(jnp.sum(da, axis=0, keepdims=True), (1, LANE))

    out = pl.BlockSpec((s, HD), lambda h, b: (b, h))
    in_blk = pl.BlockSpec((s, HD), lambda h, b: (b, h), pipeline_mode=ONE_BUF)
    cblk = pl.BlockSpec((DN_CONV_W, HD), lambda h, b: (0, h))
    sds = jax.ShapeDtypeStruct((nb * s, BW), F32)
    csds = jax.ShapeDtypeStruct((DN_CONV_W, BW), F32)
    exchange = (_all_to_all_parts, len(gx), gnl) if gx else None
    res = pl.pallas_call(
        _with_exchange(body, 20, 10, 15, exchange, (NH, nb)), grid=(NH, nb),
        in_specs=[col(CB_DNQ), col(CB_DNK), col(CB_DNV), col(CB_DNZ),
                  pl.BlockSpec((s, LANE), lambda h, b: (b, 0), pipeline_mode=ONE_BUF),
                  conv(0), conv(NH), conv(2 * NH), pl.BlockSpec((8, LANE), lambda h, b: (0, 0)),
                  pl.BlockSpec((1, HD), lambda h, b: (0, 0)), in_blk,
                  pl.BlockSpec((None, None, nc, HD, HD), lambda h, b: (b, h, 0, 0, 0), pipeline_mode=ONE_BUF), in_blk]
        + [HBM_SPEC] * (7 + len(gx)),
        out_specs=[out, out, out, out, pl.BlockSpec((None, s, LANE), lambda h, b: (h, b, 0)), cblk, cblk, cblk,
                   pl.BlockSpec((None, 8, LANE), lambda h, b: (h, 0, 0)), pl.BlockSpec((1, HD), lambda h, b: (0, 0))]
        + [HBM_SPEC] * len(gx),
        out_shape=[sds, sds, sds, sds, jax.ShapeDtypeStruct((NH, nb * s, LANE), F32), csds, csds, csds,
                   jax.ShapeDtypeStruct((NH, 8, LANE), F32), jax.ShapeDtypeStruct((1, HD), F32)]
        + _all_to_all_shapes(gx, gnl),
        scratch_shapes=[pltpu.VMEM((s, HD), F32)] * 12 + [pltpu.SemaphoreType.DMA((6,)),
                                                           pltpu.VMEM((2, DN_SUPER, DN_SUPER), F32),
                                                           pltpu.SemaphoreType.DMA((2,))]
        + (_comm_sems(len(gx), 7) if gx else []),
        compiler_params=_cp(("arbitrary", "arbitrary"), DN_BWD_VMEM), name=name)(
            proj, proj, proj, proj, ab, conv_w, conv_w, conv_w, par, gain, o_pre, states, dy, *local, *gx)
    return tuple(res[:10]) + (list(res[10:]),)


def _sum_heads(x, *, name):
    nh, t, c = x.shape
    tm = _pick(t, (1024, 512, 256, 128))

    def body(x_ref, o_ref):
        o_ref[...] = (x_ref[0] + x_ref[1] + x_ref[2] + x_ref[3]).astype(BF16)

    return pl.pallas_call(
        body, grid=(t // tm,), in_specs=[pl.BlockSpec((nh, tm, c), lambda i: (0, i, 0))],
        out_specs=pl.BlockSpec((tm, c), lambda i: (i, 0)), out_shape=jax.ShapeDtypeStruct((t, c), BF16),
        compiler_params=_cp(("parallel",)), name=name)(x)


MERGE_TM = 256


def _merge_fwd(x, proj, yp, yd, ys, b_gate, wb, wo, *, name):
    t, d = x.shape
    tm = _pick(t, (MERGE_TM, 128))

    def body(x_ref, g0_ref, g1_ref, g2_ref, yp_ref, yd_ref, ys_ref, bg_ref, wb_ref, wo_ref, o_ref):
        merged = jnp.zeros((tm, d), F32)
        for n, (g_ref, y_ref) in enumerate(((g0_ref, yp_ref), (g1_ref, yd_ref), (g2_ref, ys_ref))):
            gate = _sigmoid(g_ref[...] + bg_ref[:, n * d:(n + 1) * d])
            merged = merged + gate * _bdot(y_ref[...], wb_ref[n])
        o_ref[...] = x_ref[...] + _bdot(merged, wo_ref[...])

    row = pl.BlockSpec((tm, d), lambda i: (i, 0))
    yblk = pl.BlockSpec((tm, BW), lambda i: (i, 0))

    def gl(n):
        return pl.BlockSpec((tm, d), lambda i: (i, CB_GATE + n))

    return pl.pallas_call(
        body, grid=(t // tm,),
        in_specs=[row, gl(0), gl(1), gl(2), yblk, yblk, yblk, pl.BlockSpec((1, 3 * d), lambda i: (0, 0)),
                  pl.BlockSpec((3, BW, d), lambda i: (0, 0, 0)), pl.BlockSpec((d, d), lambda i: (0, 0))],
        out_specs=row, out_shape=jax.ShapeDtypeStruct((t, d), F32),
        compiler_params=_cp(("parallel",)), name=name)(x, proj, proj, proj, yp, yd, ys, b_gate, wb, wo)


def _merge_bwd(proj, yp, yd, ys, b_gate, wb, wo, dx, *, name):
    t, d = dx.shape
    tm = _pick(t, (MERGE_TM, 128))

    def body(g0_ref, g1_ref, g2_ref, yp_ref, yd_ref, ys_ref, bg_ref, wb_ref, wo_ref, dx_ref,
             dyp_ref, dyd_ref, dys_ref, dgl_ref, mg_ref, dxh_ref, dbd_ref, dbg_ref):
        dxh = dx_ref[...].astype(BF16)
        dxh_ref[...] = dxh
        dmerged = _bdot(dxh, wo_ref[...], NT_DIMS)
        merged = jnp.zeros((tm, d), F32)

        @pl.when(pl.program_id(0) == 0)
        def _():
            dbg_ref[...] = jnp.zeros_like(dbg_ref)

        for n, (g_ref, y_ref, dy_ref) in enumerate(((g0_ref, yp_ref, dyp_ref), (g1_ref, yd_ref, dyd_ref),
                                                    (g2_ref, ys_ref, dys_ref))):
            gate = _sigmoid(g_ref[...] + bg_ref[:, n * d:(n + 1) * d])
            bd = _bdot(y_ref[...], wb_ref[n])
            merged = merged + gate * bd
            dgl = dmerged * bd * gate * (1.0 - gate)
            dgl_ref[:, n * d:(n + 1) * d] = dgl.astype(BF16)
            dbg_ref[:, n * d:(n + 1) * d] += jnp.sum(dgl, axis=0, keepdims=True)
            dbd = (dmerged * gate).astype(BF16)
            dbd_ref[n] = dbd
            dy_ref[...] = _bdot(dbd, wb_ref[n], NT_DIMS)
        mg_ref[...] = merged.astype(BF16)

    row = pl.BlockSpec((tm, d), lambda i: (i, 0))
    yblk = pl.BlockSpec((tm, BW), lambda i: (i, 0))
    bgv = pl.BlockSpec((1, 3 * d), lambda i: (0, 0))

    def gl(n):
        return pl.BlockSpec((tm, d), lambda i: (i, CB_GATE + n))

    ysds = jax.ShapeDtypeStruct((t, BW), F32)
    return pl.pallas_call(
        body, grid=(t // tm,),
        in_specs=[gl(0), gl(1), gl(2), yblk, yblk, yblk, bgv,
                  pl.BlockSpec((3, BW, d), lambda i: (0, 0, 0)), pl.BlockSpec((d, d), lambda i: (0, 0)), row],
        out_specs=[yblk, yblk, yblk, pl.BlockSpec((tm, 3 * d), lambda i: (i, 0)), row, row,
                   pl.BlockSpec((3, tm, d), lambda i: (0, i, 0)), bgv],
        out_shape=[ysds, ysds, ysds, jax.ShapeDtypeStruct((t, 3 * d), BF16), jax.ShapeDtypeStruct((t, d), BF16),
                   jax.ShapeDtypeStruct((t, d), BF16), jax.ShapeDtypeStruct((3, t, d), BF16),
                   jax.ShapeDtypeStruct((1, 3 * d), F32)],
        compiler_params=_cp(("arbitrary",)), name=name)(proj, proj, proj, yp, yd, ys, b_gate, wb, wo, dx)


def _loss_head(x, g, target, *, name):
    t, d = x.shape
    tm = _pick(t, (512, 256, 128))

    def body(x_ref, g_ref, t_ref, dx_ref, dg_ref, loss_ref):
        xhat, rstd = _rms_stats(x_ref[...])
        err = xhat * g_ref[...] - t_ref[...]
        dx, dg = _rms_bwd_vals(err * (1.0 / d), xhat, rstd, g_ref[...])
        dx_ref[...] = dx

        @pl.when(pl.program_id(0) == 0)
        def _():
            dg_ref[...] = jnp.zeros_like(dg_ref)
            loss_ref[...] = jnp.zeros_like(loss_ref)

        dg_ref[...] += dg
        part = jnp.sum(jnp.sum(err * err, axis=1, keepdims=True), axis=0, keepdims=True) * (0.5 / d)
        loss_ref[...] += jnp.broadcast_to(part, (1, LANE))

    row = pl.BlockSpec((tm, d), lambda i: (i, 0))
    vec = pl.BlockSpec((1, d), lambda i: (0, 0))
    return pl.pallas_call(
        body, grid=(t // tm,), in_specs=[row, vec, row],
        out_specs=[row, vec, pl.BlockSpec((1, LANE), lambda i: (0, 0))],
        out_shape=[jax.ShapeDtypeStruct((t, d), F32), jax.ShapeDtypeStruct((1, d), F32),
                   jax.ShapeDtypeStruct((1, LANE), F32)],
        compiler_params=_cp(("arbitrary",)), name=name)(x, g.reshape(1, d), target)


def _adamw(w, g, m, v, *, name):
    rows, cols = w.shape
    fits = [c for c in (1024, 704, 512, 352, 256, 128, 64, 32, 16, 8) if c * cols * 4 * 14 <= VMEM_LIMIT // 2]
    tr = _pick(rows, fits)
    c1 = 1.0 / (1.0 - ADAM_B1 ** ADAM_STEP)
    c2 = 1.0 / (1.0 - ADAM_B2 ** ADAM_STEP)

    def body(w_ref, g_ref, m_ref, v_ref, d_ref, nm_ref, nv_ref):
        g = g_ref[...]
        nm = ADAM_B1 * m_ref[...] + (1.0 - ADAM_B1) * g
        nv = ADAM_B2 * v_ref[...] + (1.0 - ADAM_B2) * (g * g)
        nm_ref[...] = nm
        nv_ref[...] = nv
        d_ref[...] = -ADAM_LR * ((nm * c1) / (jnp.sqrt(nv * c2) + ADAM_EPS) + ADAM_WD * w_ref[...])

    blk = pl.BlockSpec((tr, cols), lambda i: (i, 0))
    sds = jax.ShapeDtypeStruct((rows, cols), F32)
    return pl.pallas_call(
        body, grid=(rows // tr,), in_specs=[blk] * 4, out_specs=[blk] * 3, out_shape=[sds] * 3,
        compiler_params=_cp(("parallel",)), name=name)(w, g, m, v)


MESH_ID = pl.DeviceIdType.MESH
HBM_SPEC = pl.BlockSpec(memory_space=pl.ANY)
OTHER_CHIPS = ((1, 0), (0, 1), (1, 1))


def _at_slot(ref, nl, slot):
    return ref.at[(slice(None),) * nl + (slot,)]


def _slotted(shape, nl, slots):
    return tuple(shape[:nl]) + (slots,) + tuple(shape[nl:])


def _flip(v, f):
    return 1 - v if f else v


def _comm_call(body, n, out_shapes, n_remote, args, name):
    return pl.pallas_call(
        body, out_shape=out_shapes, in_specs=[HBM_SPEC] * len(args), out_specs=[HBM_SPEC] * len(out_shapes),
        scratch_shapes=[pltpu.SemaphoreType.DMA((n * n_remote,)), pltpu.SemaphoreType.DMA((n * n_remote,)),
                        pltpu.SemaphoreType.DMA((n * 4,))],
        compiler_params=pltpu.CompilerParams(has_side_effects=True), name=name)(*args)


def _gather(xs, nls, *, name):
    n = len(xs)

    def body(*refs):
        start, finish = _gather_parts(refs[:n], refs[n:2 * n], nls, *refs[2 * n:])
        start()
        finish()

    return _comm_call(body, n, _gather_shapes(xs, nls), 7, xs, name)


def _gather_shapes(xs, nls):
    return [jax.ShapeDtypeStruct(_slotted(v.shape, nl, N_DEV), v.dtype) for v, nl in zip(xs, nls)]


def _comm_sems(n, n_remote):
    return [pltpu.SemaphoreType.DMA((n * n_remote,)), pltpu.SemaphoreType.DMA((n * n_remote,)),
            pltpu.SemaphoreType.DMA((n * 4,))]


def _gather_parts(x_refs, o_refs, nls, send_sems, recv_sems, local_sems):
    n = len(x_refs)
    x, y, c = lax.axis_index("x"), lax.axis_index("y"), lax.axis_index("c")
    me, sibling = (x, y, c), (x, y, 1 - c)
    chips = [(_flip(x, fx), _flip(y, fy)) for fx, fy in OTHER_CHIPS]

    def copy(a, k, block, to, src=None):
        dst = _at_slot(o_refs[a], nls[a], 4 * block[0] + 2 * block[1] + block[2])
        return pltpu.make_async_remote_copy(
            src_ref=dst if src is None else src, dst_ref=dst, send_sem=send_sems.at[a * 7 + k],
            recv_sem=recv_sems.at[a * 7 + k], device_id=to, device_id_type=MESH_ID)

    def mine(a):
        return pltpu.make_async_copy(x_refs[a], _at_slot(o_refs[a], nls[a], 4 * x + 2 * y + c), local_sems.at[a])

    def first(a):
        return ([copy(a, 0, me, sibling, src=x_refs[a])]
                + [copy(a, 1 + j, me, (*chip, c), src=x_refs[a]) for j, chip in enumerate(chips)])

    def start():
        for a in range(n):
            mine(a).start()
            for cp in first(a):
                cp.start()

    def finish():
        passed = []
        for j, chip in enumerate(chips):
            for a in range(n):
                copy(a, 1 + j, (*chip, c), me).wait_recv()
                passed.append(copy(a, 4 + j, (*chip, c), sibling))
                passed[-1].start()
        for a in range(n):
            copy(a, 0, sibling, me).wait_recv()
            for j, chip in enumerate(chips):
                copy(a, 4 + j, (*chip, 1 - c), me).wait_recv()
        for a in range(n):
            for cp in first(a):
                cp.wait_send()
        for cp in passed:
            cp.wait_send()
        for a in range(n):
            mine(a).wait()

    return start, finish


ALL_FLIPS = ((0, 0, 1), (0, 1, 0), (0, 1, 1), (1, 0, 0), (1, 0, 1), (1, 1, 0), (1, 1, 1))


def _all_to_all_parts(g_refs, r_refs, nls, send_sems, recv_sems, local_sems):
    del local_sems
    n = len(g_refs)
    x, y, c = lax.axis_index("x"), lax.axis_index("y"), lax.axis_index("c")

    def copies():
        out = []
        for a in range(n):
            for k, (fx, fy, fc) in enumerate(ALL_FLIPS):
                p = (_flip(x, fx), _flip(y, fy), _flip(c, fc))
                out.append(pltpu.make_async_remote_copy(
                    src_ref=_at_slot(g_refs[a], nls[a], 4 * p[0] + 2 * p[1] + p[2]), dst_ref=_at_slot(r_refs[a], nls[a], k),
                    send_sem=send_sems.at[a * 7 + k], recv_sem=recv_sems.at[a * 7 + k], device_id=p,
                    device_id_type=MESH_ID))
        return out

    def start():
        for cp in copies():
            cp.start()

    def finish():
        cps = copies()
        for cp in cps:
            cp.wait_recv()
        for cp in cps:
            cp.wait_send()

    return start, finish


def _all_to_all_shapes(gs, nls):
    return [jax.ShapeDtypeStruct(_slotted(v.shape[:nl] + v.shape[nl + 1:], nl, 7), v.dtype) for v, nl in zip(gs, nls)]


def _scatter_pair(gs, nls, *, name):
    n = len(gs)

    def body(*refs):
        g_refs, got_refs, (send_sems, recv_sems, _) = refs[:n], refs[n:2 * n], refs[2 * n:]
        x, y, c = lax.axis_index("x"), lax.axis_index("y"), lax.axis_index("c")
        remote = []
        for a in range(n):
            for q in range(4):
                rc = pltpu.make_async_remote_copy(
                    src_ref=_at_slot(g_refs[a], nls[a], 2 * q + 1 - c), dst_ref=_at_slot(got_refs[a], nls[a], q),
                    send_sem=send_sems.at[a * 4 + q], recv_sem=recv_sems.at[a * 4 + q], device_id=(x, y, 1 - c),
                    device_id_type=MESH_ID)
                rc.start()
                remote.append(rc)
        for rc in remote:
            rc.wait_recv()
        for rc in remote:
            rc.wait_send()

    outs = [jax.ShapeDtypeStruct(_slotted(v.shape[:nl] + v.shape[nl + 1:], nl, 4), v.dtype) for v, nl in zip(gs, nls)]
    return _comm_call(body, n, outs, 4, gs, name)


def _scatter_chips(ps, nls, *, name):
    n = len(ps)

    def body(*refs):
        p_refs, r_refs, (send_sems, recv_sems, _) = refs[:n], refs[n:2 * n], refs[2 * n:]
        x, y, c = lax.axis_index("x"), lax.axis_index("y"), lax.axis_index("c")
        remote = []
        for a in range(n):
            for k, (fx, fy) in enumerate(OTHER_CHIPS):
                tx, ty = _flip(x, fx), _flip(y, fy)
                rc = pltpu.make_async_remote_copy(
                    src_ref=_at_slot(p_refs[a], nls[a], 2 * tx + ty), dst_ref=_at_slot(r_refs[a], nls[a], k),
                    send_sem=send_sems.at[a * 3 + k], recv_sem=recv_sems.at[a * 3 + k], device_id=(tx, ty, c),
                    device_id_type=MESH_ID)
                rc.start()
                remote.append(rc)
        for rc in remote:
            rc.wait_recv()
        for rc in remote:
            rc.wait_send()

    outs = [jax.ShapeDtypeStruct(_slotted(v.shape[:nl] + v.shape[nl + 1:], nl, 3), v.dtype) for v, nl in zip(ps, nls)]
    return _comm_call(body, n, outs, 3, ps, name)


def _pair_add(g, got, core, *, name):
    rows, cols = g.shape[-2:]
    lf = math.prod(got.shape[:-3])
    tr = _pick(rows, (1024, 512, 352, 256, 128))

    def body(core_ref, g_ref, got_ref, o_ref):
        o_ref[...] = (g_ref[...].astype(F32) + got_ref[...].astype(F32)).astype(BF16)

    blk = pl.BlockSpec((None, None, tr, cols), lambda i, q, j, core_ref: (i, q, j, 0))
    out = pl.pallas_call(
        body, grid_spec=pltpu.PrefetchScalarGridSpec(
            num_scalar_prefetch=1, grid=(lf, 4, rows // tr),
            in_specs=[pl.BlockSpec((None, None, None, tr, cols), lambda i, q, j, core_ref: (i, q, core_ref[0], j, 0)),
                      blk], out_specs=blk),
        out_shape=jax.ShapeDtypeStruct((lf, 4, rows, cols), BF16),
        compiler_params=_cp(("parallel", "parallel", "parallel")), name=name)(
            core, g.reshape(lf, 4, 2, rows, cols), got.reshape(lf, 4, rows, cols))
    return out.reshape(got.shape)


def _sum_adamw(p, r, own, w, m, v, layer, prev, *, name):
    shape = w.shape[1:]
    rows, cols = shape[-2:]
    lf = math.prod(shape[:-2])
    np_, nk = p.shape[-3], r.shape[-3]
    fits = [c for c in (1024, 512, 352, 256, 128, 64, 32, 16) if c * cols * (7 * 4 + (nk + 1) * 2) * 2 <= VMEM_LIMIT // 2]
    tr = _pick(rows, fits)
    c1 = 1.0 / (1.0 - ADAM_B1 ** ADAM_STEP)
    c2 = 1.0 / (1.0 - ADAM_B2 ** ADAM_STEP)

    def body(own_ref, p_ref, r_ref, w_ref, m_ref, v_ref, *rest):
        g_ref, d_ref, nm_ref, nv_ref = rest[-4:]
        g = p_ref[...].astype(F32)
        for k in range(nk):
            g = g + r_ref[k].astype(F32)
        g_ref[...] = g
        nm = ADAM_B1 * m_ref[...] + (1.0 - ADAM_B1) * g
        nv = ADAM_B2 * v_ref[...] + (1.0 - ADAM_B2) * (g * g)
        nm_ref[...] = nm
        nv_ref[...] = nv
        d_ref[...] = -ADAM_LR * ((nm * c1) / (jnp.sqrt(nv * c2) + ADAM_EPS) + ADAM_WD * w_ref[...])

    wblk = pl.BlockSpec((None, None, tr, cols), lambda i, j, own_ref: (layer, i, j, 0))
    full = (w.shape[0], lf, rows, cols)
    sds = jax.ShapeDtypeStruct(full, F32)
    prev = [] if prev is None else [a.reshape(full) for a in prev]
    outs = pl.pallas_call(
        body, grid_spec=pltpu.PrefetchScalarGridSpec(
            num_scalar_prefetch=1, grid=(lf, rows // tr),
            in_specs=[pl.BlockSpec((None, None, tr, cols), lambda i, j, own_ref: (i, own_ref[0], j, 0)),
                      pl.BlockSpec((None, nk, tr, cols), lambda i, j, own_ref: (i, 0, j, 0))] + [wblk] * 3
            + [HBM_SPEC] * len(prev),
            out_specs=[wblk] * 4),
        out_shape=[sds] * 4, input_output_aliases={6 + i: i for i in range(len(prev))},
        compiler_params=_cp(("parallel", "parallel")), name=name)(
            own, p.reshape(lf, np_, rows, cols), r.reshape(lf, nk, rows, cols), w.reshape(full), m.reshape(full),
            v.reshape(full), *prev)
    return [o.reshape(w.shape) for o in outs]


def _sum_slots(x, *, name):
    nd, rows, cols = x.shape
    tr = _pick(rows, (512, 256, 128, 64, 32, 16, 8))

    def body(x_ref, o_ref):
        acc = x_ref[0].astype(F32)
        for j in range(1, nd):
            acc = acc + x_ref[j].astype(F32)
        o_ref[...] = acc

    return pl.pallas_call(
        body, grid=(rows // tr,), in_specs=[pl.BlockSpec((nd, tr, cols), lambda i: (0, i, 0))],
        out_specs=pl.BlockSpec((tr, cols), lambda i: (i, 0)), out_shape=jax.ShapeDtypeStruct((rows, cols), F32),
        compiler_params=_cp(("parallel",)), name=name)(x)


def _pad_rows(a, mult=8):
    r = (-a.shape[0]) % mult
    return jnp.pad(a, ((0, r), (0, 0))) if r else a


def _flat128(a):
    f = a.reshape(-1)
    return jnp.pad(f, (0, (-f.shape[0]) % LANE)).reshape(-1, LANE)


def _unshard(gathered, shape, axis):
    g = gathered.reshape((N_DEV,) + tuple(shape))
    g = jnp.moveaxis(g, 0, axis)
    full = list(shape)
    full[axis] *= N_DEV
    return g.reshape(full)


def _col_shards(full):
    rows, cols = full.shape
    return jnp.moveaxis(full.reshape(rows, N_DEV, cols // N_DEV), 1, 0)


BIG = (("ffn_w_gate", 2), ("ffn_w_up", 2), ("ffn_w_down", 2), ("w_in", 1), ("w_branch", 2), ("w_out", 1))


def kernel(x, ffn_norm, ffn_w_gate, ffn_w_up, ffn_w_down, mix_norm, w_in, b_gate, pool_w, pool_scale, dn_conv, dn_A_log, dn_dt_bias, dn_out_norm, w_branch, w_out, final_norm, loss_target, m_ffn_norm, m_ffn_w_gate, m_ffn_w_up, m_ffn_w_down, m_mix_norm, m_w_in, m_b_gate, m_pool_w, m_pool_scale, m_dn_conv, m_dn_A_log, m_dn_dt_bias, m_dn_out_norm, m_w_branch, m_w_out, m_final_norm, v_ffn_norm, v_ffn_w_gate, v_ffn_w_up, v_ffn_w_down, v_mix_norm, v_w_in, v_b_gate, v_pool_w, v_pool_scale, v_dn_conv, v_dn_A_log, v_dn_dt_bias, v_dn_out_norm, v_w_branch, v_w_out, v_final_norm):
    wts = dict(ffn_norm=ffn_norm, ffn_w_gate=ffn_w_gate, ffn_w_up=ffn_w_up, ffn_w_down=ffn_w_down, mix_norm=mix_norm,
               w_in=w_in, b_gate=b_gate, pool_w=pool_w, pool_scale=pool_scale, dn_conv=dn_conv, dn_A_log=dn_A_log,
               dn_dt_bias=dn_dt_bias, dn_out_norm=dn_out_norm, w_branch=w_branch, w_out=w_out, final_norm=final_norm)
    mom = dict(ffn_norm=m_ffn_norm, ffn_w_gate=m_ffn_w_gate, ffn_w_up=m_ffn_w_up, ffn_w_down=m_ffn_w_down,
               mix_norm=m_mix_norm, w_in=m_w_in, b_gate=m_b_gate, pool_w=m_pool_w, pool_scale=m_pool_scale,
               dn_conv=m_dn_conv, dn_A_log=m_dn_A_log, dn_dt_bias=m_dn_dt_bias, dn_out_norm=m_dn_out_norm,
               w_branch=m_w_branch, w_out=m_w_out, final_norm=m_final_norm)
    var = dict(ffn_norm=v_ffn_norm, ffn_w_gate=v_ffn_w_gate, ffn_w_up=v_ffn_w_up, ffn_w_down=v_ffn_w_down,
               mix_norm=v_mix_norm, w_in=v_w_in, b_gate=v_b_gate, pool_w=v_pool_w, pool_scale=v_pool_scale,
               dn_conv=v_dn_conv, dn_A_log=v_dn_A_log, dn_dt_bias=v_dn_dt_bias, dn_out_norm=v_dn_out_norm,
               w_branch=v_w_branch, w_out=v_w_out, final_norm=v_final_norm)
    nb, s, d = x.shape
    t = nb * s
    me = 4 * lax.axis_index("x") + 2 * lax.axis_index("y") + lax.axis_index("c")

    big = [n for n, _ in BIG]
    nls = [nl - 1 for _, nl in BIG]
    shards = lambda l: [wts[n][l].astype(BF16) for n in big]
    small_sh = jnp.concatenate([_flat128(ffn_norm), _flat128(dn_conv)], axis=0)
    ffn3 = big[:3]
    *pre0, small_g = _gather([wts[n][0, 0].astype(BF16) for n in ffn3] + [small_sh], [0] * 4, name="gather_weights")
    rest0 = [wts[n][0, 1].astype(BF16) for n in ffn3] + [wts[n][0].astype(BF16) for n in big[3:]]
    rest0_nls = [0] * 3 + nls[3:]
    full = [None] * DEPTH

    def mixer_weights(l):
        w_in_full = jnp.moveaxis(full[l]["w_in"], 0, 1).reshape(d, -1)
        w_main = jnp.concatenate([w_in_full[:, :AB_LO], w_in_full[:, AB_HI:]], axis=1)
        w_ab = jnp.pad(w_in_full[:, AB_LO:AB_HI], ((0, 0), (0, LANE - (AB_HI - AB_LO))))
        wb = jnp.moveaxis(full[l]["w_branch"], 1, 2).reshape(3, BW, d)
        return w_main, w_ab, wb, full[l]["w_out"].reshape(d, d)

    nfr = ffn_norm.size // LANE
    ffn_norm_full = _unshard(small_g[:, :nfr], ffn_norm.shape, 2)
    dn_conv_full = _unshard(small_g[:, nfr:], dn_conv.shape, 2)
    pool_w_h = pool_w.astype(BF16)

    xs = x.reshape(t, d)
    saved = []
    for l in range(DEPTH):
        sv = dict(x0=xs)
        if l == 0:
            xs, a0, b0, got = _ffn_fwd(xs, ffn_norm_full[0, 0], *pre0, name="ffn_fwd_gather", gather=(rest0, rest0_nls))
            full[0] = dict(zip(ffn3, zip(pre0, got[:3])), **dict(zip(big[3:], got[3:])))
        else:
            xs, a0, b0, _ = _ffn_fwd(xs, ffn_norm_full[l, 0], full[l]["ffn_w_gate"][0], full[l]["ffn_w_up"][0],
                                     full[l]["ffn_w_down"][0], name="ffn_fwd")
        sv["ab0"] = (a0, b0)
        sv["x1"] = xs
        w_main, w_ab, wb, wo = mixer_weights(l)
        h = _rms_fwd(xs, mix_norm[l], name="mix_rms")
        proj = _mm(h, w_main, name="proj")
        ab = _mm(h, w_ab, name="proj_ab")
        par = jnp.pad(jnp.stack([dn_A_log[l], dn_dt_bias[l]]), ((0, 6), (0, LANE - NH)))
        gain = dn_out_norm[l].reshape(1, HD)
        psc = pool_scale[l].reshape(1, BW)
        yp = _pool_fwd(proj, pool_w_h[l], psc, nb, s, name="pool_fwd")
        yd, o_pre, states, dn_local, gat = _dn_fwd(proj, ab, dn_conv_full[l], par, gain, nb, s,
                                         name="dn_fwd" if l == DEPTH - 1 else "dn_fwd_gather",
                                         gather=(shards(l + 1), nls) if l < DEPTH - 1 else None)
        if l < DEPTH - 1:
            full[l + 1] = dict(zip(big, gat))
        ys, sb_ctr = _sb_fwd(proj, nb, s, name="sb_fwd")
        bg = b_gate[l].reshape(1, 3 * d)
        xs = _merge_fwd(xs, proj, yp, yd, ys, bg, wb, wo, name="merge_fwd")
        sv.update(x2=xs, h=h, proj=proj, ab=ab, par=par, gain=gain, psc=psc, yp=yp, yd=yd, ys=ys, sb_ctr=sb_ctr, o_pre=o_pre,
                  states=states, dn_local=dn_local, bg=bg, w_main=w_main, w_ab=w_ab, wb=wb, wo=wo)
        xs, a1, b1, _ = _ffn_fwd(xs, ffn_norm_full[l, 1], full[l]["ffn_w_gate"][1], full[l]["ffn_w_up"][1],
                                 full[l]["ffn_w_down"][1], name="ffn_fwd")
        sv["ab1"] = (a1, b1)
        saved.append(sv)

    dx, g_final, loss_row = _loss_head(xs, final_norm, loss_target.reshape(t, d), name="loss_head")
    loss = lax.psum(loss_row[0, 0], ("x", "y", "c"))

    gw = {n: [None] * DEPTH for n in ("ffn_norm", "ffn_w_gate", "ffn_w_up", "ffn_w_down", "mix_norm", "w_in", "b_gate",
                                      "pool_w", "pool_scale", "dn_conv", "dn_A_log", "dn_dt_bias", "dn_out_norm",
                                      "w_branch", "w_out")}

    me_i = me.astype(jnp.int32).reshape(1)
    updated = {n: None for n in big}
    pending = None

    def finish_layer(l, own_blocks, arrived, own_slot):
        for n, p, r in zip(big, own_blocks, arrived):
            updated[n] = _sum_adamw(p, r, own_slot, wts[n], mom[n], var[n], l, updated[n], name=f"adamw_{n}_{l}")

    def ffn_back(l, i, x_in, dy):
        dxi, dg, hb, dyh, da, db, sact = _ffn_bwd(x_in, ffn_norm_full[l, i], full[l]["ffn_w_gate"][i],
                                                  full[l]["ffn_w_up"][i], full[l]["ffn_w_down"][i],
                                                  *saved[l][f"ab{i}"], dy, name="ffn_bwd")
        return dxi, dg, (_mm_slots(hb, da, name="dw_gate_up"), _mm_slots(hb, db, name="dw_gate_up"),
                         _mm_slots(sact, dyh, name="dw_down"))

    for l in reversed(range(DEPTH)):
        sv = saved[l]
        dx, dg1, (dwg1, dwu1, dwd1) = ffn_back(l, 1, sv["x2"], dx)
        dyp, dyd, dys, dgl, merged, dxh, dbd, dbg = _merge_bwd(sv["proj"], sv["yp"], sv["yd"], sv["ys"], sv["bg"],
                                                               sv["wb"], sv["wo"], dx, name="merge_bwd")
        gw["w_out"][l] = _mm(merged, dxh, ta=True, out_dtype=BF16, name="dw_out").reshape(N_DEV, d // N_DEV, d)
        gw["w_branch"][l] = jnp.stack([_col_shards(_mm(y, dbd[n], ta=True, out_dtype=BF16, name="dw_branch"))
                                       for n, y in enumerate((sv["yp"], sv["yd"], sv["ys"]))])
        gw["b_gate"][l] = dbg.reshape(3 * d)
        du, dpw, dps = _pool_bwd(sv["proj"], pool_w_h[l], sv["psc"], dyp, nb, s, name="pool_bwd")
        gw["pool_w"][l], gw["pool_scale"][l] = dpw, dps.reshape(BW)
        own = [gw[n][pending] for n in big] if pending is not None else []
        dqr, dkr, dvr, dz, dab4, dcq, dck, dcv, dpar, dgain, arrived_ffn = _dn_bwd(
            sv["proj"], sv["ab"], dn_conv_full[l], sv["par"], sv["gain"], sv["o_pre"], sv["states"], sv["dn_local"],
            dyd, nb, s, name="dn_bwd_scatter" if own else "dn_bwd", scatter=(own[:3], nls[:3]) if own else None)
        gw["dn_conv"][l] = jnp.concatenate([dcq, dck, dcv], axis=1)
        gw["dn_A_log"][l], gw["dn_dt_bias"][l], gw["dn_out_norm"][l] = dpar[:, 0, 0], dpar[:, 1, 0], dgain.reshape(HD)
        dsq, dsk, dsv, arrived_rest = _sb_bwd(sv["proj"], sv["sb_ctr"], dys, nb, s,
                                              name="sb_bwd_scatter" if own else "sb_bwd",
                                              scatter=(own[3:], nls[3:]) if own else None)
        if own:
            finish_layer(pending, own, arrived_ffn + arrived_rest, me_i)
        dab = _sum_heads(dab4, name="sum_heads")
        dproj = jnp.concatenate([du.astype(BF16), dqr.astype(BF16), dkr.astype(BF16), dvr.astype(BF16),
                                 dz.astype(BF16), dsq.astype(BF16), dsk.astype(BF16), dsv.astype(BF16), dgl], axis=1)
        dw_main = _mm(sv["h"], dproj, ta=True, out_dtype=BF16, name="dw_in")
        dw_ab = _mm(sv["h"], dab, ta=True, out_dtype=BF16, name="dw_ab")
        gw["w_in"][l] = _col_shards(jnp.concatenate([dw_main[:, :AB_LO], dw_ab[:, :AB_HI - AB_LO],
                                                     dw_main[:, AB_LO:]], axis=1))
        dh_main = _mm(dproj, sv["w_main"], tb=True, name="dh_mix")
        dh_ab = _mm(dab, sv["w_ab"], tb=True, name="dh_mix_ab")
        dx, dgm = _rms_bwd(sv["x1"], mix_norm[l], dh_main, dh_ab, dx, name="mix_rms_bwd")
        gw["mix_norm"][l] = dgm.reshape(d)
        dx, dg0, (dwg0, dwu0, dwd0) = ffn_back(l, 0, sv["x0"], dx)
        gw["ffn_norm"][l] = jnp.stack([dg0.reshape(d), dg1.reshape(d)])
        gw["ffn_w_gate"][l] = jnp.stack([dwg0, dwg1])
        gw["ffn_w_up"][l] = jnp.stack([dwu0, dwu1])
        gw["ffn_w_down"][l] = jnp.stack([dwd0, dwd1])
        pending = l
    grad_x = dx.reshape(nb, s, d)

    core = lax.axis_index("c").astype(jnp.int32).reshape(1)
    chip = (2 * lax.axis_index("x") + lax.axis_index("y")).astype(jnp.int32).reshape(1)
    last = [gw[n][0] for n in big]
    got = _scatter_pair(last, nls, name="scatter_grads_pair")
    chip_sums = [_pair_add(g, b, core, name="add_pair_" + n) for n, g, b in zip(big, last, got)]
    finish_layer(0, chip_sums, _scatter_chips(chip_sums, nls, name="scatter_grads_chips"), chip)
    grads, delta, new_m, new_v = ({n: updated[n][i] for n in big} for i in range(4))
    gw = {n: jnp.stack(v) for n, v in gw.items() if n not in big}
    gw["final_norm"] = g_final.reshape(d)

    small = ("ffn_norm", "mix_norm", "b_gate", "pool_w", "pool_scale", "dn_conv", "dn_A_log", "dn_dt_bias",
             "dn_out_norm", "final_norm")
    sp = _pad_rows(jnp.concatenate([_flat128(gw[n]) for n in small], axis=0))
    ssum = _sum_slots(_gather([sp], [0], name="gather_small_grads")[0], name="sum_small_grads")
    off = 0
    for n in small:
        r = -(-gw[n].size // LANE)
        g = ssum[off:off + r].reshape(-1)[:gw[n].size].reshape(gw[n].shape)
        off += r
        if n in ("ffn_norm", "dn_conv"):
            w = wts[n].shape[2]
            g = lax.dynamic_slice_in_dim(g, me * w, w, axis=2)
        grads[n] = g

    pk = lambda src: _pad_rows(jnp.concatenate([_flat128(src[n]) for n in small], axis=0))
    dl, nm, nv = _adamw(pk(wts), pk(grads), pk(mom), pk(var), name="adamw_small")
    off = 0
    for n in small:
        r = -(-wts[n].size // LANE)
        for dst, src in ((delta, dl), (new_m, nm), (new_v, nv)):
            dst[n] = src[off:off + r].reshape(-1)[:wts[n].size].reshape(wts[n].shape)
        off += r

    order = ("ffn_norm", "ffn_w_gate", "ffn_w_up", "ffn_w_down", "mix_norm", "w_in", "b_gate", "pool_w", "pool_scale",
             "dn_conv", "dn_A_log", "dn_dt_bias", "dn_out_norm", "w_branch", "w_out", "final_norm")
    return (loss, grad_x, *[grads[n] for n in order], *[delta[n] for n in order], *[new_m[n] for n in order],
            *[new_v[n] for n in order])
```

```python
import functools
import math

import jax
import jax.numpy as jnp
from jax import lax
from jax.experimental import pallas as pl
from jax.experimental.pallas import tpu as pltpu

F32, BF16 = jnp.float32, jnp.bfloat16
D_MODEL, D_FF, DEPTH = 1024, 2816, 4
BW = 512
HD = 128
NH = 4
DN_CHUNK = 64
EPS = 1e-6
N_DEV = 8
LANE = 128
CB_POOL, CB_DNQ, CB_DNK, CB_DNV, CB_DNZ, CB_SBQ, CB_SBK, CB_SBV = 0, 4, 8, 12, 16, 20, 24, 28
CB_GATE = 4
P_MAIN = 7168
AB_LO, AB_HI = 2560, 2568
ADAM_LR, ADAM_B1, ADAM_B2, ADAM_EPS, ADAM_WD, ADAM_STEP = 0.001, 0.9, 0.999, 1e-08, 0.01, 10
VMEM_LIMIT = 56 * 1024 * 1024
HIGHEST = lax.Precision.HIGHEST
NT_DIMS = (((1,), (1,)), ((), ()))
TN_DIMS = (((0,), (0,)), ((), ()))
NN_DIMS = (((1,), (0,)), ((), ()))


def _cp(dims=None, vmem=VMEM_LIMIT):
    return pltpu.CompilerParams(dimension_semantics=dims, vmem_limit_bytes=vmem)


def _pick(n, cands):
    for c in cands:
        if n % c == 0:
            return c
    return n


def _bdot(a, b, dims=NN_DIMS):
    return lax.dot_general(a.astype(BF16), b.astype(BF16), dims, preferred_element_type=F32)


def _hdot(a, b, dims=NN_DIMS):
    return lax.dot_general(a, b, dims, precision=lax.Precision.HIGH, preferred_element_type=F32)


def _split_dot(x, m01):
    hi = x.astype(BF16)
    lo = (x - hi.astype(F32)).astype(BF16)
    return (lax.dot_general(hi, m01, NN_DIMS, preferred_element_type=F32)
            + lax.dot_general(lo, m01, NN_DIMS, preferred_element_type=F32))


def _sigmoid(x):
    return 1.0 / (1.0 + jnp.exp(-x))


def _log_sigmoid(x):
    return jnp.minimum(x, 0.0) - jnp.log1p(jnp.exp(-jnp.abs(x)))


def _softplus(x):
    return jnp.maximum(x, 0.0) + jnp.log1p(jnp.exp(-jnp.abs(x)))


def _shift_down(x, k):
    r = lax.broadcasted_iota(jnp.int32, x.shape, 0)
    return jnp.where(r >= k, pltpu.roll(x, k, 0), 0.0)


def _shift_up(x, k):
    n = x.shape[0]
    r = lax.broadcasted_iota(jnp.int32, x.shape, 0)
    return jnp.where(r < n - k, pltpu.roll(x, n - k, 0), 0.0)


def _mm(a, b, *, ta=False, tb=False, out_dtype=F32, name):
    (kk, m) = a.shape if ta else a.shape[::-1]
    (k2, n) = b.shape[::-1] if tb else b.shape
    assert kk == k2, (a.shape, b.shape, ta, tb)
    bm = _pick(m, (1024, 512, 256, 128))
    bn = _pick(n, (1024, 1408, 512, 256, 128))
    bk = _pick(kk, (512, 256, 128))
    nk = kk // bk
    dims = (((0 if ta else 1,), (1 if tb else 0,)), ((), ()))

    def body(a_ref, b_ref, o_ref, acc_ref):
        k = pl.program_id(2)

        @pl.when(k == 0)
        def _():
            acc_ref[...] = jnp.zeros_like(acc_ref)

        acc_ref[...] += lax.dot_general(a_ref[...].astype(BF16), b_ref[...].astype(BF16), dims,
                                        preferred_element_type=F32)

        @pl.when(k == nk - 1)
        def _():
            o_ref[...] = acc_ref[...].astype(out_dtype)

    a_spec = (pl.BlockSpec((bk, bm), lambda i, j, k: (k, i)) if ta else pl.BlockSpec((bm, bk), lambda i, j, k: (i, k)))
    b_spec = (pl.BlockSpec((bn, bk), lambda i, j, k: (j, k)) if tb else pl.BlockSpec((bk, bn), lambda i, j, k: (k, j)))
    return pl.pallas_call(
        body, grid=(m // bm, n // bn, nk), in_specs=[a_spec, b_spec],
        out_specs=pl.BlockSpec((bm, bn), lambda i, j, k: (i, j)),
        out_shape=jax.ShapeDtypeStruct((m, n), out_dtype),
        scratch_shapes=[pltpu.VMEM((bm, bn), F32)],
        compiler_params=_cp(("parallel", "parallel", "arbitrary")), name=name)(a, b)


def _mm_slots(a, b, *, name):
    a3, b3 = a.ndim == 3, b.ndim == 3
    ns = a.shape[0] if a3 else b.shape[0]
    m, t = a.shape[-2:]
    n = b.shape[-1]
    bk = _pick(t, (512, 256, 128))
    nk = t // bk

    def body(a_ref, b_ref, o_ref, acc_ref):
        k = pl.program_id(0)

        @pl.when(k == 0)
        def _():
            acc_ref[...] = jnp.zeros_like(acc_ref)

        for s in range(ns):
            acc_ref[s] += _bdot(a_ref[s] if a3 else a_ref[...], b_ref[s] if b3 else b_ref[...])

        @pl.when(k == nk - 1)
        def _():
            o_ref[...] = acc_ref[...].astype(BF16)

    a_spec = pl.BlockSpec((ns, m, bk), lambda k: (0, 0, k)) if a3 else pl.BlockSpec((m, bk), lambda k: (0, k))
    b_spec = pl.BlockSpec((ns, bk, n), lambda k: (0, k, 0)) if b3 else pl.BlockSpec((bk, n), lambda k: (k, 0))
    return pl.pallas_call(
        body, grid=(nk,), in_specs=[a_spec, b_spec], out_specs=pl.BlockSpec((ns, m, n), lambda k: (0, 0, 0)),
        out_shape=jax.ShapeDtypeStruct((ns, m, n), BF16), scratch_shapes=[pltpu.VMEM((ns, m, n), F32)],
        compiler_params=_cp(("arbitrary",)), name=name)(a, b)


def _rms_stats(x):
    rstd = lax.rsqrt(jnp.mean(x * x, axis=-1, keepdims=True) + EPS)
    return x * rstd, rstd


def _rms_bwd_vals(dh, xhat, rstd, g):
    dxh = dh * g
    dx = rstd * (dxh - xhat * jnp.mean(dxh * xhat, axis=-1, keepdims=True))
    return dx, jnp.sum(dh * xhat, axis=0, keepdims=True)


def _rms_fwd(x, g, *, name):
    t, d = x.shape
    tm = _pick(t, (512, 256, 128))

    def body(x_ref, g_ref, h_ref):
        xhat, _ = _rms_stats(x_ref[...])
        h_ref[...] = (xhat * g_ref[...]).astype(BF16)

    return pl.pallas_call(
        body, grid=(t // tm,),
        in_specs=[pl.BlockSpec((tm, d), lambda i: (i, 0)), pl.BlockSpec((1, d), lambda i: (0, 0))],
        out_specs=pl.BlockSpec((tm, d), lambda i: (i, 0)), out_shape=jax.ShapeDtypeStruct((t, d), BF16),
        compiler_params=_cp(("parallel",)), name=name)(x, g.reshape(1, d))


def _rms_bwd(x, g, dh_a, dh_b, dres, *, name):
    t, d = x.shape
    tm = _pick(t, (512, 256, 128))

    def body(x_ref, g_ref, dha_ref, dhb_ref, dres_ref, dx_ref, dg_ref):
        xhat, rstd = _rms_stats(x_ref[...])
        dx, dg = _rms_bwd_vals(dha_ref[...] + dhb_ref[...], xhat, rstd, g_ref[...])
        dx_ref[...] = dres_ref[...] + dx

        @pl.when(pl.program_id(0) == 0)
        def _():
            dg_ref[...] = jnp.zeros_like(dg_ref)

        dg_ref[...] += dg

    row = pl.BlockSpec((tm, d), lambda i: (i, 0))
    vec = pl.BlockSpec((1, d), lambda i: (0, 0))
    return pl.pallas_call(
        body, grid=(t // tm,), in_specs=[row, vec, row, row, row], out_specs=[row, vec],
        out_shape=[jax.ShapeDtypeStruct((t, d), F32), jax.ShapeDtypeStruct((1, d), F32)],
        compiler_params=_cp(("arbitrary",)), name=name)(x, g.reshape(1, d), dh_a, dh_b, dres)


FFN_TM = 512


def _ffn_fwd(x, g, wg, wu, wd, *, name, gather=None):
    t, d = x.shape
    nf, _, fc = wg.shape
    tm = _pick(t, (FFN_TM, 256, 128))
    gx, gnl = gather if gather else ([], [])

    def body(x_ref, g_ref, wg_ref, wu_ref, wd_ref, o_ref, a_ref, b_ref, h_ref, acc_ref):
        j = pl.program_id(1)

        @pl.when(j == 0)
        def _():
            xhat, _ = _rms_stats(x_ref[...])
            h_ref[...] = (xhat * g_ref[...]).astype(BF16)
            acc_ref[...] = jnp.zeros_like(acc_ref)

        h = h_ref[...]
        a = _bdot(h, wg_ref[...])
        b = _bdot(h, wu_ref[...])
        a_ref[...] = a.astype(BF16)
        b_ref[...] = b.astype(BF16)
        s = a * _sigmoid(a) * b
        acc_ref[...] += _bdot(s, wd_ref[...])

        @pl.when(j == nf - 1)
        def _():
            o_ref[...] = x_ref[...] + 0.5 * acc_ref[...]

    row = pl.BlockSpec((tm, d), lambda i, j: (i, 0))
    grid = (t // tm, nf)
    exchange = (_gather_parts, len(gx), gnl) if gx else None
    res = pl.pallas_call(
        _with_exchange(body, 5, 3, 2, exchange, grid), grid=grid,
        in_specs=[row, pl.BlockSpec((1, d), lambda i, j: (0, 0)),
                  pl.BlockSpec((None, d, fc), lambda i, j: (j, 0, 0)), pl.BlockSpec((None, d, fc), lambda i, j: (j, 0, 0)),
                  pl.BlockSpec((None, fc, d), lambda i, j: (j, 0, 0))] + [HBM_SPEC] * len(gx),
        out_specs=[row, pl.BlockSpec((None, tm, fc), lambda i, j: (j, i, 0)),
                   pl.BlockSpec((None, tm, fc), lambda i, j: (j, i, 0))] + [HBM_SPEC] * len(gx),
        out_shape=[jax.ShapeDtypeStruct((t, d), F32), jax.ShapeDtypeStruct((nf, t, fc), BF16),
                   jax.ShapeDtypeStruct((nf, t, fc), BF16)] + _gather_shapes(gx, gnl),
        scratch_shapes=[pltpu.VMEM((tm, d), BF16), pltpu.VMEM((tm, d), F32)] + (_comm_sems(len(gx), 7) if gx else []),
        compiler_params=_cp(("arbitrary", "arbitrary")), name=name)(x, g.reshape(1, d), wg, wu, wd, *gx)
    return res[0], res[1], res[2], list(res[3:])


def _ffn_bwd(x, g, wg, wu, wd, a_pre, b_pre, dy, *, name):
    t, d = x.shape
    nf, _, fc = wg.shape
    tm = _pick(t, (FFN_TM, 256, 128))

    def body(x_ref, g_ref, wg_ref, wu_ref, wd_ref, a_ref, b_ref, dy_ref,
             dx_ref, dg_ref, ht_ref, dyh_ref, da_ref, db_ref, st_ref, acc_ref):
        i, j = pl.program_id(0), pl.program_id(1)

        @pl.when(j == 0)
        def _():
            xhat, _ = _rms_stats(x_ref[...])
            ht_ref[...] = (xhat * g_ref[...]).T.astype(BF16)
            dyh_ref[...] = (0.5 * dy_ref[...]).astype(BF16)
            acc_ref[...] = jnp.zeros_like(acc_ref)

        a = a_ref[...].astype(F32)
        b = b_ref[...].astype(F32)
        sg = _sigmoid(a)
        silu = a * sg
        st_ref[...] = (silu * b).T.astype(BF16)
        ds = _bdot(dyh_ref[...], wd_ref[...], NT_DIMS)
        da = (ds * b * (sg * (1.0 + a * (1.0 - sg)))).astype(BF16)
        db = (ds * silu).astype(BF16)
        da_ref[...] = da
        db_ref[...] = db
        acc_ref[...] += _bdot(da, wg_ref[...], NT_DIMS) + _bdot(db, wu_ref[...], NT_DIMS)

        @pl.when((i == 0) & (j == 0))
        def _():
            dg_ref[...] = jnp.zeros_like(dg_ref)

        @pl.when(j == nf - 1)
        def _():
            xhat, rstd = _rms_stats(x_ref[...])
            dx, dg = _rms_bwd_vals(acc_ref[...], xhat, rstd, g_ref[...])
            dx_ref[...] = dy_ref[...] + dx
            dg_ref[...] += dg

    row = pl.BlockSpec((tm, d), lambda i, j: (i, 0))
    vec = pl.BlockSpec((1, d), lambda i, j: (0, 0))
    fblk = pl.BlockSpec((None, tm, fc), lambda i, j: (j, i, 0))
    return pl.pallas_call(
        body, grid=(t // tm, nf),
        in_specs=[row, vec, pl.BlockSpec((None, d, fc), lambda i, j: (j, 0, 0)),
                  pl.BlockSpec((None, d, fc), lambda i, j: (j, 0, 0)), pl.BlockSpec((None, fc, d), lambda i, j: (j, 0, 0)),
                  fblk, fblk, row],
        out_specs=[row, vec, pl.BlockSpec((d, tm), lambda i, j: (0, i)), row, fblk, fblk,
                   pl.BlockSpec((None, fc, tm), lambda i, j: (j, 0, i))],
        out_shape=[jax.ShapeDtypeStruct((t, d), F32), jax.ShapeDtypeStruct((1, d), F32),
                   jax.ShapeDtypeStruct((d, t), BF16), jax.ShapeDtypeStruct((t, d), BF16),
                   jax.ShapeDtypeStruct((nf, t, fc), BF16), jax.ShapeDtypeStruct((nf, t, fc), BF16),
                   jax.ShapeDtypeStruct((nf, fc, t), BF16)],
        scratch_shapes=[pltpu.VMEM((tm, d), F32)],
        compiler_params=_cp(("arbitrary", "arbitrary")), name=name)(x, g.reshape(1, d), wg, wu, wd, a_pre, b_pre, dy)


def _pool_core(u, grp):
    s = u.shape[0]
    w2 = u + _shift_down(u, 1)
    w4 = w2 + _shift_down(w2, 2)
    w8 = w4 + _shift_down(w4, 4)
    w16 = w8 + _shift_down(w8, 8)
    wsum = jnp.where(grp == 0, w2, jnp.where(grp == 1, w4, jnp.where(grp == 2, w8, w16)))
    win = jnp.left_shift(2, grp).astype(F32)
    t1 = (lax.broadcasted_iota(jnp.int32, (s, 1), 0) + 1).astype(F32)
    inv = 1.0 / jnp.minimum(t1, win)
    return wsum * inv - u, inv


def _pool_fwd(proj, pool_w, pool_scale, nb, s, *, name):
    def body(u_ref, w_ref, sc_ref, y_ref):
        pooled, _ = _pool_core(u_ref[...].astype(F32), pl.program_id(0))
        y_ref[...] = _bdot(pooled, w_ref[...]) * sc_ref[...]

    return pl.pallas_call(
        body, grid=(NH, nb),
        in_specs=[pl.BlockSpec((s, HD), lambda g, b: (b, CB_POOL + g)),
                  pl.BlockSpec((None, HD, HD), lambda g, b: (g, 0, 0)), pl.BlockSpec((1, HD), lambda g, b: (0, g))],
        out_specs=pl.BlockSpec((s, HD), lambda g, b: (b, g)),
        out_shape=jax.ShapeDtypeStruct((nb * s, BW), F32),
        compiler_params=_cp(("parallel", "parallel")), name=name)(proj, pool_w, pool_scale)


def _pool_bwd(proj, pool_w, pool_scale, dy, nb, s, *, name):
    def body(u_ref, w_ref, sc_ref, dy_ref, du_ref, dw_ref, dsc_ref):
        grp, b = pl.program_id(0), pl.program_id(1)
        pooled, inv = _pool_core(u_ref[...].astype(F32), grp)
        mixed = _bdot(pooled, w_ref[...])
        dy = dy_ref[...]
        dmixed = dy * sc_ref[...]
        dpooled = _bdot(dmixed, w_ref[...], NT_DIMS)
        r = dpooled * inv
        v2 = r + _shift_up(r, 1)
        v4 = v2 + _shift_up(v2, 2)
        v8 = v4 + _shift_up(v4, 4)
        v16 = v8 + _shift_up(v8, 8)
        vsum = jnp.where(grp == 0, v2, jnp.where(grp == 1, v4, jnp.where(grp == 2, v8, v16)))
        du_ref[...] = vsum - dpooled

        @pl.when(b == 0)
        def _():
            dw_ref[...] = jnp.zeros_like(dw_ref)
            dsc_ref[...] = jnp.zeros_like(dsc_ref)

        dw_ref[...] += _bdot(pooled, dmixed, TN_DIMS)
        dsc_ref[...] += jnp.sum(dy * mixed, axis=0, keepdims=True)

    return pl.pallas_call(
        body, grid=(NH, nb),
        in_specs=[pl.BlockSpec((s, HD), lambda g, b: (b, CB_POOL + g)),
                  pl.BlockSpec((None, HD, HD), lambda g, b: (g, 0, 0)), pl.BlockSpec((1, HD), lambda g, b: (0, g)),
                  pl.BlockSpec((s, HD), lambda g, b: (b, g))],
        out_specs=[pl.BlockSpec((s, HD), lambda g, b: (b, g)), pl.BlockSpec((None, HD, HD), lambda g, b: (g, 0, 0)),
                   pl.BlockSpec((1, HD), lambda g, b: (0, g))],
        out_shape=[jax.ShapeDtypeStruct((nb * s, BW), F32), jax.ShapeDtypeStruct((NH, HD, HD), F32),
                   jax.ShapeDtypeStruct((1, BW), F32)],
        compiler_params=_cp(("arbitrary", "arbitrary")), name=name)(proj, pool_w, pool_scale, dy)


SB_BLK = 128


SB_G = 4
SB_KG = SB_G * SB_BLK
SB_Q = 2 * SB_BLK


def _sb_block(qb, kg, q0, k0, diagonal):
    z = _bdot(qb, kg, NT_DIMS) * (HD ** -0.5)
    lsz = _log_sigmoid(z)
    if not diagonal:
        return lsz, lsz - z, None
    row = lax.broadcasted_iota(jnp.int32, z.shape, 0) + q0
    col = lax.broadcasted_iota(jnp.int32, z.shape, 1) + k0
    causal = col < row
    return lsz, jnp.where(causal, lsz - z, 0.0), causal


def _keep(causal, x):
    return x if causal is None else jnp.where(causal, x, 0.0)


def _sub(x, m):
    return x[:, m * SB_BLK:(m + 1) * SB_BLK]


def _sb_tails(lnm, after, ct):
    hi = lnm.astype(BF16)
    lo = (lnm - hi.astype(F32)).astype(BF16)
    tails = [None] * SB_G
    for m in reversed(range(SB_G)):
        tails[m] = (lax.dot_general(_sub(hi, m), after, NN_DIMS, preferred_element_type=F32)
                    + lax.dot_general(_sub(lo, m), after, NN_DIMS, preferred_element_type=F32)) + ct
        ct = ct + jnp.sum(_sub(lnm, m), axis=1, keepdims=True)
    ones = jnp.ones((8, lnm.shape[1]), BF16)
    rows = (lax.dot_general(ones, hi, NT_DIMS, preferred_element_type=F32)
            + lax.dot_general(ones, lo, NT_DIMS, preferred_element_type=F32))
    return jnp.concatenate(tails, axis=1), rows, ct


def _tri01(lower):
    r = lax.broadcasted_iota(jnp.int32, (SB_BLK, SB_BLK), 0)
    c = lax.broadcasted_iota(jnp.int32, (SB_BLK, SB_BLK), 1)
    return jnp.where((r < c) if lower else (r > c), 1.0, 0.0).astype(BF16)


def _split3(x):
    hi = x.astype(BF16)
    mid = (x - hi.astype(F32)).astype(BF16)
    lo = (x - hi.astype(F32) - mid.astype(F32)).astype(BF16)
    return hi, mid, lo


def _rows_to_cols(rows):
    eighth = jnp.full((8, LANE), 0.125, BF16)
    return sum(lax.dot_general(p, eighth, TN_DIMS, preferred_element_type=F32) for p in _split3(rows))


def _sb_fwd(proj, nb, s, *, name):
    nq = s // SB_Q
    ng = s // SB_KG

    def body(q_ref, k_ref, v_ref, o_ref, ctr_ref):
        after = _tri01(False)

        def qblock(i, _):
            q0 = pl.multiple_of(i * SB_Q, SB_Q)
            qb = q_ref[pl.ds(q0, SB_Q), :]

            def kgroup(g, carry, diagonal):
                acc, ct, ctr = carry
                k0 = pl.multiple_of(g * SB_KG, SB_KG)
                lsz, lnm, causal = _sb_block(qb, k_ref[pl.ds(k0, SB_KG), :], q0, k0, diagonal)
                ctr_ref[i * ng + g] = ctr
                tail, rows, ct = _sb_tails(lnm, after, ct)
                w = _keep(causal, jnp.exp(lsz + tail))
                return acc + _bdot(w, v_ref[pl.ds(k0, SB_KG), :]), ct, ctr + rows

            gd = (i * SB_Q) // SB_KG
            carry = kgroup(gd, (jnp.zeros((SB_Q, HD), F32), jnp.zeros((SB_Q, 1), F32), jnp.zeros((8, SB_Q), F32)), True)
            acc, _, _ = lax.fori_loop(0, gd, lambda jj, c: kgroup(gd - 1 - jj, c, False), carry)
            o_ref[pl.ds(q0, SB_Q), :] = acc
            return 0

        lax.fori_loop(0, nq, qblock, 0)

    def col(cb):
        return pl.BlockSpec((s, HD), lambda b, h: (b, cb + h))

    return pl.pallas_call(
        body, grid=(nb, NH), in_specs=[col(CB_SBQ), col(CB_SBK), col(CB_SBV)],
        out_specs=[pl.BlockSpec((s, HD), lambda b, h: (b, h)),
                   pl.BlockSpec((None, None, nq * ng, 8, SB_Q), lambda b, h: (b, h, 0, 0, 0))],
        out_shape=[jax.ShapeDtypeStruct((nb * s, BW), F32), jax.ShapeDtypeStruct((nb, NH, nq * ng, 8, SB_Q), F32)],
        compiler_params=_cp(("parallel", "parallel")), name=name)(proj, proj, proj)


def _sb_bwd(proj, ctr, dy, nb, s, *, name, scatter=None):
    nq = s // SB_Q
    ng = s // SB_KG
    scale = HD ** -0.5
    gx, gnl = scatter if scatter else ([], [])

    def body(q_ref, k_ref, v_ref, ctr_ref, do_ref, dq_ref, dk_ref, dv_ref):
        after = _tri01(False)
        before = _tri01(True)
        dk_ref[...] = jnp.zeros_like(dk_ref)
        dv_ref[...] = jnp.zeros_like(dv_ref)

        def qblock(i, _):
            q0 = pl.multiple_of(i * SB_Q, SB_Q)
            qb = q_ref[pl.ds(q0, SB_Q), :]
            dob = do_ref[pl.ds(q0, SB_Q), :]

            def kgroup(g, carry, diagonal):
                dq, ce = carry
                k0 = pl.multiple_of(g * SB_KG, SB_KG)
                kg = k_ref[pl.ds(k0, SB_KG), :]
                vg = v_ref[pl.ds(k0, SB_KG), :]
                lsz, lnm, causal = _sb_block(qb, kg, q0, k0, diagonal)
                tail, _, _ = _sb_tails(lnm, after, _rows_to_cols(ctr_ref[i * ng + g])[:, 0:1])
                w = _keep(causal, jnp.exp(lsz + tail))
                e = _bdot(dob, vg, NT_DIMS) * w
                pres = []
                for m in range(SB_G):
                    pres.append(_split_dot(_sub(e, m), before) + ce)
                    ce = ce + jnp.sum(_sub(e, m), axis=1, keepdims=True)
                sig = jnp.exp(lsz)
                dz = _keep(causal, e * (1.0 - sig) - jnp.concatenate(pres, axis=1) * sig) * scale
                dk_ref[pl.ds(k0, SB_KG), :] += _bdot(dz, qb, TN_DIMS)
                dv_ref[pl.ds(k0, SB_KG), :] += _bdot(w, dob, TN_DIMS)
                return dq + _bdot(dz, kg), ce

            gd = (i * SB_Q) // SB_KG
            carry = lax.fori_loop(0, gd, lambda g, c: kgroup(g, c, False),
                                  (jnp.zeros((SB_Q, HD), F32), jnp.zeros((SB_Q, 1), F32)))
            dq, _ = kgroup(gd, carry, True)
            dq_ref[pl.ds(q0, SB_Q), :] = dq
            return 0

        lax.fori_loop(0, nq, qblock, 0)

    def col(cb):
        return pl.BlockSpec((s, HD), lambda b, h: (b, cb + h))

    out = pl.BlockSpec((s, HD), lambda b, h: (b, h))
    sds = jax.ShapeDtypeStruct((nb * s, BW), F32)
    exchange = (_all_to_all_parts, len(gx), gnl) if gx else None
    res = pl.pallas_call(
        _with_exchange(body, 5, 3, 0, exchange, (nb, NH)), grid=(nb, NH),
        in_specs=[col(CB_SBQ), col(CB_SBK), col(CB_SBV),
                  pl.BlockSpec((None, None, nq * ng, 8, SB_Q), lambda b, h: (b, h, 0, 0, 0)), out]
        + [HBM_SPEC] * len(gx),
        out_specs=[out, out, out] + [HBM_SPEC] * len(gx), out_shape=[sds, sds, sds] + _all_to_all_shapes(gx, gnl),
        scratch_shapes=_comm_sems(len(gx), 7) if gx else [],
        compiler_params=_cp(("arbitrary", "arbitrary")), name=name)(proj, proj, proj, ctr, dy, *gx)
    return res[0], res[1], res[2], list(res[3:])


def _make_cdot(dims, dims_da, dims_db, swap_a=False, swap_b=False):
    @jax.custom_vjp
    def f(a, b):
        return _bdot(a, b, dims)

    def fwd(a, b):
        return _bdot(a, b, dims), (a, b)

    def bwd(res, g):
        a, b = res
        da = _bdot(b, g, dims_da) if swap_a else _bdot(g, b, dims_da)
        db = _bdot(g, a, dims_db) if swap_b else _bdot(a, g, dims_db)
        return da, db

    f.defvjp(fwd, bwd)
    return f


_cdot = _make_cdot(NN_DIMS, NT_DIMS, TN_DIMS)
_cdot_nt = _make_cdot(NT_DIMS, NN_DIMS, TN_DIMS, swap_b=True)
_cdot_tn = _make_cdot(TN_DIMS, NT_DIMS, NN_DIMS, swap_a=True)


DN_SUPER = 4 * DN_CHUNK


@jax.custom_vjp
def _unit_lower_inverse(lmat):
    n = lmat.shape[0]
    steps = int(math.log2(DN_CHUNK))
    eye = jnp.where(lax.broadcasted_iota(jnp.int32, (n, n), 0) == lax.broadcasted_iota(jnp.int32, (n, n), 1), 1.0, 0.0)
    inv = eye - lmat
    pw = _hdot(lmat, lmat)
    for it in range(steps - 1):
        inv = inv + _hdot(inv, pw)
        if it < steps - 2:
            pw = _hdot(pw, pw)
    return inv


def _unit_lower_inverse_fwd(lmat):
    inv = _unit_lower_inverse(lmat)
    return inv, inv


def _unit_lower_inverse_bwd(inv, g):
    return (-_hdot(_hdot(inv, g, TN_DIMS), inv, NT_DIMS),)


_unit_lower_inverse.defvjp(_unit_lower_inverse_fwd, _unit_lower_inverse_bwd)


@jax.custom_vjp
def _known_inverse(lmat, inv):
    return inv


def _known_inverse_fwd(lmat, inv):
    return inv, inv


def _known_inverse_bwd(inv, g):
    return -_hdot(_hdot(inv, g, TN_DIMS), inv, NT_DIMS), jnp.zeros_like(inv)


_known_inverse.defvjp(_known_inverse_fwd, _known_inverse_bwd)


def _dn_local(q, k, v, bb, gb, known_inv=None):
    n = q.shape[0]
    r = lax.broadcasted_iota(jnp.int32, (n, n), 0)
    cc = lax.broadcasted_iota(jnp.int32, (n, n), 1)
    shift = int(math.log2(DN_CHUNK))
    same = lax.shift_right_logical(r, shift) == lax.shift_right_logical(cc, shift)
    incl = jnp.where(same, jnp.where(r >= cc, 1.0, 0.0), 0.0)
    strict = jnp.where(same, jnp.where(r > cc, 1.0, 0.0), 0.0)
    gc = _hdot(incl, gb)
    gc_row = _hdot(jnp.full((n, HD), 1.0 / HD, F32), gc, NT_DIMS)
    diff = jnp.concatenate([gc] * (n // HD), axis=1) - gc_row
    decay = incl * jnp.exp(diff * incl)
    kb = k * bb
    lmat = _cdot_nt(kb, k) * (strict * decay)
    egc = jnp.exp(gc)
    inv = _unit_lower_inverse(lmat) if known_inv is None else _known_inverse(lmat, known_inv)
    u = _hdot(inv, v * bb)
    w = _hdot(inv, kb * egc)
    attn = _cdot_nt(q, k) * decay
    gl = _hdot(jnp.where(same, 1.0, 0.0), gb)
    return u, w, attn, q * egc, k * jnp.exp(gl - gc), jnp.exp(gl), inv


def _attn_pairs(attn):
    return jnp.concatenate([attn[:HD, :HD], attn[HD:, HD:]], axis=0)


def _attn_unpairs(a):
    z = jnp.zeros((HD, HD), F32)
    return jnp.concatenate([jnp.concatenate([a[:HD], z], axis=1), jnp.concatenate([z, a[HD:]], axis=1)], axis=0)


def _dn_step(u, w, a, qd, kd, cdrows, state, odd):
    v_new = u - _cdot(w, state)
    z = jnp.zeros_like(v_new)
    o = _cdot(qd, state) + _cdot(a, jnp.concatenate([z, v_new] if odd else [v_new, z], axis=0))
    return o, state * jnp.mean(cdrows, axis=0, keepdims=True) + _cdot_tn(kd, v_new)


def _dn_local_pass(fn, s, ins, outs):
    def step(it, _):
        sl = pl.ds(pl.multiple_of(it * DN_SUPER, DN_SUPER), DN_SUPER)
        res = fn(*[ref[sl, :] for ref in ins])
        for ref, val in zip(outs, res):
            ref[sl, :] = val
        return 0

    lax.fori_loop(0, s // DN_SUPER, step, 0)


def _lane_pick(row, idx):
    lane = lax.broadcasted_iota(jnp.int32, row.shape, 1)
    return jnp.sum(jnp.where(lane == idx, row, 0.0), axis=1, keepdims=True)


def _col_pick(x, idx):
    lane = lax.broadcasted_iota(jnp.int32, x.shape, 1)
    return jnp.sum(jnp.where(lane == idx, x, 0.0), axis=1, keepdims=True)


def _conv_silu(x, w):
    xc = (w[3:4, :] * x + w[2:3, :] * _shift_down(x, 1) + w[1:2, :] * _shift_down(x, 2)
          + w[0:1, :] * _shift_down(x, 3))
    return xc * _sigmoid(xc), xc


def _conv_silu_bwd(x, w, xc, dxs, dw_ref):
    sg = _sigmoid(xc)
    dxc = dxs * (sg * (1.0 + xc * (1.0 - sg)))
    dx = (w[3:4, :] * dxc + w[2:3, :] * _shift_up(dxc, 1) + w[1:2, :] * _shift_up(dxc, 2)
          + w[0:1, :] * _shift_up(dxc, 3))
    dw_ref[3:4, :] += jnp.sum(dxc * x, axis=0, keepdims=True)
    dw_ref[2:3, :] += jnp.sum(dxc * _shift_down(x, 1), axis=0, keepdims=True)
    dw_ref[1:2, :] += jnp.sum(dxc * _shift_down(x, 2), axis=0, keepdims=True)
    dw_ref[0:1, :] += jnp.sum(dxc * _shift_down(x, 3), axis=0, keepdims=True)
    return dx


def _dn_prep(qr_ref, kr_ref, vr_ref, ab_ref, cq_ref, ck_ref, cv_ref, par_ref, head):
    qs, qc = _conv_silu(qr_ref[...].astype(F32), cq_ref[...])
    ks, kc = _conv_silu(kr_ref[...].astype(F32), ck_ref[...])
    vs, vc = _conv_silu(vr_ref[...].astype(F32), cv_ref[...])
    rq = lax.rsqrt(jnp.sum(qs * qs, axis=1, keepdims=True) + EPS)
    rk = lax.rsqrt(jnp.sum(ks * ks, axis=1, keepdims=True) + EPS)
    ab = ab_ref[...]
    a_in = _col_pick(ab, head) + _lane_pick(par_ref[1:2, :], head)
    beta = _sigmoid(_col_pick(ab, NH + head))
    neg_ea = -jnp.exp(_lane_pick(par_ref[0:1, :], head))
    g = neg_ea * _softplus(a_in)
    return dict(q=qs * rq * (HD ** -0.5), k=ks * rk, v=vs, beta=beta, g=g, qs=qs, ks=ks, qc=qc, kc=kc, vc=vc,
                rq=rq, rk=rk, a_in=a_in, neg_ea=neg_ea)


ONE_BUF = pl.Buffered(1)
DN_BWD_VMEM = 62 * 1024 * 1024


def _dn_specs(nb, s):
    def col(cb):
        return pl.BlockSpec((s, HD), lambda h, b: (b, cb + h))

    def conv(cb):
        return pl.BlockSpec((DN_CONV_W, HD), lambda h, b: (0, cb + h))

    return col, conv


DN_CONV_W = 4


def _with_exchange(body, n_in, n_out, n_scratch, exchange, grid):
    if exchange is None:
        return body
    parts_fn, n, nls = exchange

    def wrapped(*refs):
        ins, xs = refs[:n_in], refs[n_in:n_in + n]
        outs, os = refs[n_in + n:n_in + n + n_out], refs[n_in + n + n_out:n_in + 2 * n + n_out]
        rest = refs[n_in + 2 * n + n_out:]
        scratch, sems = rest[:n_scratch], rest[n_scratch:]
        pos = [pl.program_id(k) for k in range(len(grid))]
        first = functools.reduce(jnp.logical_and, [p == 0 for p in pos])
        last = functools.reduce(jnp.logical_and, [p == g - 1 for p, g in zip(pos, grid)])
        start, finish = parts_fn(xs, os, nls, *sems)
        pl.when(first)(start)
        body(*ins, *outs, *scratch)
        pl.when(last)(finish)

    return wrapped


def _dn_fwd(proj, ab, conv_w, par, gain, nb, s, *, name, gather=None):
    nc = s // DN_CHUNK
    col, conv = _dn_specs(nb, s)
    gx, gnl = gather if gather else ([], [])

    def body(qr_ref, kr_ref, vr_ref, z_ref, ab_ref, cq_ref, ck_ref, cv_ref, par_ref, gain_ref,
             y_ref, o_ref, st_ref, u_ref, w_ref, at_ref, qd_ref, kd_ref, cd_ref, inv_ref, q_s, k_s, v_s, bb_s, gb_s):
        p = _dn_prep(qr_ref, kr_ref, vr_ref, ab_ref, cq_ref, ck_ref, cv_ref, par_ref, pl.program_id(0))
        q_s[...], k_s[...], v_s[...] = p["q"], p["k"], p["v"]
        bb_s[...] = jnp.broadcast_to(p["beta"], (s, HD))
        gb_s[...] = jnp.broadcast_to(p["g"], (s, HD))
        def local(*args):
            u, w, attn, qd, kd, cd, inv = _dn_local(*args)
            return u, w, _attn_pairs(attn), qd, kd, cd, inv

        _dn_local_pass(local, s, [q_s, k_s, v_s, bb_s, gb_s], [u_ref, w_ref, at_ref, qd_ref, kd_ref, cd_ref, inv_ref])

        def chunk_pair(pi, state):
            for odd in (0, 1):
                ci = 2 * pi + odd
                sl = pl.ds(pl.multiple_of(ci * DN_CHUNK, DN_CHUNK), DN_CHUNK)
                st_ref[ci] = state
                o, state = _dn_step(u_ref[sl, :], w_ref[sl, :], at_ref[sl, :], qd_ref[sl, :], kd_ref[sl, :],
                                    cd_ref[sl, :], state, odd)
                o_ref[sl, :] = o
            return state

        lax.fori_loop(0, nc // 2, chunk_pair, jnp.zeros((HD, HD), F32))
        o = o_ref[...]
        z = z_ref[...].astype(F32)
        on = o * lax.rsqrt(jnp.mean(o * o, axis=1, keepdims=True) + EPS) * gain_ref[...]
        y_ref[...] = on * (z * _sigmoid(z))

    out = pl.BlockSpec((s, HD), lambda h, b: (b, h))
    sds = jax.ShapeDtypeStruct((nb * s, BW), F32)
    exchange = (_gather_parts, len(gx), gnl) if gx else None
    res = pl.pallas_call(
        _with_exchange(body, 10, 10, 5, exchange, (NH, nb)), grid=(NH, nb),
        in_specs=[col(CB_DNQ), col(CB_DNK), col(CB_DNV), col(CB_DNZ), pl.BlockSpec((s, LANE), lambda h, b: (b, 0)),
                  conv(0), conv(NH), conv(2 * NH), pl.BlockSpec((8, LANE), lambda h, b: (0, 0)),
                  pl.BlockSpec((1, HD), lambda h, b: (0, 0))] + [HBM_SPEC] * len(gx),
        out_specs=[out, out, pl.BlockSpec((None, None, nc, HD, HD), lambda h, b: (b, h, 0, 0, 0))] + [out] * 6
        + [pl.BlockSpec((s, DN_SUPER), lambda h, b: (b, h))] + [HBM_SPEC] * len(gx),
        out_shape=[sds, sds, jax.ShapeDtypeStruct((nb, NH, nc, HD, HD), F32)] + [sds] * 6
        + [jax.ShapeDtypeStruct((nb * s, NH * DN_SUPER), F32)] + _gather_shapes(gx, gnl),
        scratch_shapes=[pltpu.VMEM((s, HD), F32)] * 5 + (_comm_sems(len(gx), 7) if gx else []),
        compiler_params=_cp(("arbitrary", "arbitrary")), name=name)(
            proj, proj, proj, proj, ab, conv_w, conv_w, conv_w, par, gain, *gx)
    return res[0], res[1], res[2], list(res[3:10]), list(res[10:])


def _dn_bwd(proj, ab, conv_w, par, gain, o_pre, states, local, dy, nb, s, *, name, scatter=None):
    nc = s // DN_CHUNK
    col, conv = _dn_specs(nb, s)
    gx, gnl = scatter if scatter else ([], [])

    def body(qr_ref, kr_ref, vr_ref, z_ref, ab_ref, cq_ref, ck_ref, cv_ref, par_ref, gain_ref, o_ref, st_ref, dy_ref,
             u_hbm, w_hbm, at_hbm, qd_hbm, kd_hbm, cd_hbm, inv_hbm,
             dqr_ref, dkr_ref, dvr_ref, dz_ref, dab_ref, dcq_ref, dck_ref, dcv_ref, dpar_ref, dgain_ref,
             q_s, k_s, v_s, bb_s, gb_s, do_s, u_s, w_s, qd_s, kd_s, at_s, cd_s, load_sems, inv_buf, inv_sems):
        head, b = pl.program_id(0), pl.program_id(1)
        local_refs = [u_s, w_s, at_s, qd_s, kd_s, cd_s]
        loads = [pltpu.make_async_copy(src.at[pl.ds(pl.multiple_of(b * s, s), s), pl.ds(pl.multiple_of(head * HD, HD), HD)],
                                       dst, load_sems.at[i])
                 for i, (src, dst) in enumerate(zip((u_hbm, w_hbm, at_hbm, qd_hbm, kd_hbm, cd_hbm), local_refs))]
        for cp in loads:
            cp.start()
        p = _dn_prep(qr_ref, kr_ref, vr_ref, ab_ref, cq_ref, ck_ref, cv_ref, par_ref, head)
        q_s[...], k_s[...], v_s[...] = p["q"], p["k"], p["v"]
        bb_s[...] = jnp.broadcast_to(p["beta"], (s, HD))
        gb_s[...] = jnp.broadcast_to(p["g"], (s, HD))

        @pl.when(b == 0)
        def _():
            for ref in (dcq_ref, dck_ref, dcv_ref, dpar_ref):
                ref[...] = jnp.zeros_like(ref)

        @pl.when((b == 0) & (head == 0))
        def _():
            dgain_ref[...] = jnp.zeros_like(dgain_ref)

        o, z, dy = o_ref[...], z_ref[...].astype(F32), dy_ref[...]
        rstd = lax.rsqrt(jnp.mean(o * o, axis=1, keepdims=True) + EPS)
        ohat = o * rstd
        sgz = _sigmoid(z)
        dz_ref[...] = dy * (ohat * gain_ref[...]) * (sgz * (1.0 + z * (1.0 - sgz)))
        don = dy * (z * sgz)
        dgain_ref[...] += jnp.sum(don * ohat, axis=0, keepdims=True)
        dxh = don * gain_ref[...]
        do_s[...] = rstd * (dxh - ohat * jnp.mean(dxh * ohat, axis=1, keepdims=True))

        for cp in loads:
            cp.wait()

        def chunk_pair(pr, dstate):
            for odd in (1, 0):
                ci = nc - 1 - 2 * pr - (1 - odd)
                sl = pl.ds(pl.multiple_of(ci * DN_CHUNK, DN_CHUNK), DN_CHUNK)
                _, vjp = jax.vjp(functools.partial(_dn_step, odd=odd), u_s[sl, :], w_s[sl, :], at_s[sl, :],
                                 qd_s[sl, :], kd_s[sl, :], cd_s[sl, :], st_ref[ci])
                du, dw, dat, dqd, dkd, dcd, dstate = vjp((do_s[sl, :], dstate))
                u_s[sl, :], w_s[sl, :], at_s[sl, :], qd_s[sl, :], kd_s[sl, :], cd_s[sl, :] = du, dw, dat, dqd, dkd, dcd
            return dstate

        lax.fori_loop(0, nc // 2, chunk_pair, jnp.zeros((HD, HD), F32))

        def inv_load(it, slot):
            rows = pl.ds(pl.multiple_of(b * s + it * DN_SUPER, DN_SUPER), DN_SUPER)
            cols = pl.ds(pl.multiple_of(head * DN_SUPER, DN_SUPER), DN_SUPER)
            return pltpu.make_async_copy(inv_hbm.at[rows, cols], inv_buf.at[slot], inv_sems.at[slot])

        def local_bwd(it, _):
            slot = lax.rem(it, 2)
            inv_load(it, slot).wait()

            @pl.when(it + 1 < s // DN_SUPER)
            def _():
                inv_load(it + 1, 1 - slot).start()

            sl = pl.ds(pl.multiple_of(it * DN_SUPER, DN_SUPER), DN_SUPER)
            ins = [ref[sl, :] for ref in (q_s, k_s, v_s, bb_s, gb_s)]
            du, dw, dat, dqd, dkd, dcd = [ref[sl, :] for ref in local_refs]
            _, vjp = jax.vjp(lambda *a: _dn_local(*a, known_inv=inv_buf[slot])[:6], *ins)
            dq, dk, dv, dbb, dgb = vjp((du, dw, _attn_unpairs(dat), dqd, dkd, dcd))
            q_s[sl, :], k_s[sl, :], v_s[sl, :] = dq, dk, dv
            bb_s[sl, :] = jnp.broadcast_to(jnp.sum(dbb, axis=1, keepdims=True), (DN_SUPER, HD))
            gb_s[sl, :] = jnp.broadcast_to(jnp.sum(dgb, axis=1, keepdims=True), (DN_SUPER, HD))
            return 0

        inv_load(0, 0).start()
        lax.fori_loop(0, s // DN_SUPER, local_bwd, 0)

        dq, dk, dv = q_s[...], k_s[...], v_s[...]
        qs, ks, rq, rk = p["qs"], p["ks"], p["rq"], p["rk"]
        dqs = (HD ** -0.5) * (rq * dq - qs * (rq * rq * rq) * jnp.sum(dq * qs, axis=1, keepdims=True))
        dks = rk * dk - ks * (rk * rk * rk) * jnp.sum(dk * ks, axis=1, keepdims=True)
        dqr_ref[...] = _conv_silu_bwd(qr_ref[...].astype(F32), cq_ref[...], p["qc"], dqs, dcq_ref)
        dkr_ref[...] = _conv_silu_bwd(kr_ref[...].astype(F32), ck_ref[...], p["kc"], dks, dck_ref)
        dvr_ref[...] = _conv_silu_bwd(vr_ref[...].astype(F32), cv_ref[...], p["vc"], dv, dcv_ref)

        dbeta, dg = bb_s[:, 0:1], gb_s[:, 0:1]
        beta = p["beta"]
        db_logit = dbeta * beta * (1.0 - beta)
        da = dg * p["neg_ea"] * _sigmoid(p["a_in"])
        lane = lax.broadcasted_iota(jnp.int32, (s, LANE), 1)
        dab_ref[...] = jnp.where(lane == head, da, 0.0) + jnp.where(lane == NH + head, db_logit, 0.0)
        dpar_ref[0:1, :] += jnp.broadcast_to(jnp.sum(dg * p["g"], axis=0, keepdims=True), (1, LANE))
        dpar_ref[1:2, :] += jnp.broadcast_to(jnp.sum(da, axis=0, keepdims=True), (1, LANE))

    out = pl.BlockSpec((s, HD), lambda h, b: (b, h))
    in_blk = pl.BlockSpec((s, HD), lambda h, b: (b, h), pipeline_mode=ONE_BUF)
    cblk = pl.BlockSpec((DN_CONV_W, HD), lambda h, b: (0, h))
    sds = jax.ShapeDtypeStruct((nb * s, BW), F32)
    csds = jax.ShapeDtypeStruct((DN_CONV_W, BW), F32)
    exchange = (_all_to_all_parts, len(gx), gnl) if gx else None
    res = pl.pallas_call(
        _with_exchange(body, 20, 10, 15, exchange, (NH, nb)), grid=(NH, nb),
        in_specs=[col(CB_DNQ), col(CB_DNK), col(CB_DNV), col(CB_DNZ),
                  pl.BlockSpec((s, LANE), lambda h, b: (b, 0), pipeline_mode=ONE_BUF),
                  conv(0), conv(NH), conv(2 * NH), pl.BlockSpec((8, LANE), lambda h, b: (0, 0)),
                  pl.BlockSpec((1, HD), lambda h, b: (0, 0)), in_blk,
                  pl.BlockSpec((None, None, nc, HD, HD), lambda h, b: (b, h, 0, 0, 0), pipeline_mode=ONE_BUF), in_blk]
        + [HBM_SPEC] * (7 + len(gx)),
        out_specs=[out, out, out, out, pl.BlockSpec((None, s, LANE), lambda h, b: (h, b, 0)), cblk, cblk, cblk,
                   pl.BlockSpec((None, 8, LANE), lambda h, b: (h, 0, 0)), pl.BlockSpec((1, HD), lambda h, b: (0, 0))]
        + [HBM_SPEC] * len(gx),
        out_shape=[sds, sds, sds, sds, jax.ShapeDtypeStruct((NH, nb * s, LANE), F32), csds, csds, csds,
                   jax.ShapeDtypeStruct((NH, 8, LANE), F32), jax.ShapeDtypeStruct((1, HD), F32)]
        + _all_to_all_shapes(gx, gnl),
        scratch_shapes=[pltpu.VMEM((s, HD), F32)] * 12 + [pltpu.SemaphoreType.DMA((6,)),
                                                           pltpu.VMEM((2, DN_SUPER, DN_SUPER), F32),
                                                           pltpu.SemaphoreType.DMA((2,))]
        + (_comm_sems(len(gx), 7) if gx else []),
        compiler_params=_cp(("arbitrary", "arbitrary"), DN_BWD_VMEM), name=name)(
            proj, proj, proj, proj, ab, conv_w, conv_w, conv_w, par, gain, o_pre, states, dy, *local, *gx)
    return tuple(res[:10]) + (list(res[10:]),)


def _sum_heads(x, *, name):
    nh, t, c = x.shape
    tm = _pick(t, (1024, 512, 256, 128))

    def body(x_ref, o_ref):
        o_ref[...] = (x_ref[0] + x_ref[1] + x_ref[2] + x_ref[3]).astype(BF16)

    return pl.pallas_call(
        body, grid=(t // tm,), in_specs=[pl.BlockSpec((nh, tm, c), lambda i: (0, i, 0))],
        out_specs=pl.BlockSpec((tm, c), lambda i: (i, 0)), out_shape=jax.ShapeDtypeStruct((t, c), BF16),
        compiler_params=_cp(("parallel",)), name=name)(x)


MERGE_TM = 256


def _merge_fwd(x, proj, yp, yd, ys, b_gate, wb, wo, *, name):
    t, d = x.shape
    tm = _pick(t, (MERGE_TM, 128))

    def body(x_ref, g0_ref, g1_ref, g2_ref, yp_ref, yd_ref, ys_ref, bg_ref, wb_ref, wo_ref, o_ref):
        merged = jnp.zeros((tm, d), F32)
        for n, (g_ref, y_ref) in enumerate(((g0_ref, yp_ref), (g1_ref, yd_ref), (g2_ref, ys_ref))):
            gate = _sigmoid(g_ref[...].astype(F32) + bg_ref[:, n * d:(n + 1) * d])
            merged = merged + gate * _bdot(y_ref[...], wb_ref[n])
        o_ref[...] = x_ref[...] + _bdot(merged, wo_ref[...])

    row = pl.BlockSpec((tm, d), lambda i: (i, 0))
    yblk = pl.BlockSpec((tm, BW), lambda i: (i, 0))

    def gl(n):
        return pl.BlockSpec((tm, d), lambda i: (i, CB_GATE + n))

    return pl.pallas_call(
        body, grid=(t // tm,),
        in_specs=[row, gl(0), gl(1), gl(2), yblk, yblk, yblk, pl.BlockSpec((1, 3 * d), lambda i: (0, 0)),
                  pl.BlockSpec((3, BW, d), lambda i: (0, 0, 0)), pl.BlockSpec((d, d), lambda i: (0, 0))],
        out_specs=row, out_shape=jax.ShapeDtypeStruct((t, d), F32),
        compiler_params=_cp(("parallel",)), name=name)(x, proj, proj, proj, yp, yd, ys, b_gate, wb, wo)


def _merge_bwd(proj, yp, yd, ys, b_gate, wb, wo, dx, *, name):
    t, d = dx.shape
    tm = _pick(t, (MERGE_TM, 128))

    def body(g0_ref, g1_ref, g2_ref, yp_ref, yd_ref, ys_ref, bg_ref, wb_ref, wo_ref, dx_ref,
             dyp_ref, dyd_ref, dys_ref, dgl_ref, mg_ref, dxh_ref, dbd_ref, dbg_ref):
        dxh = dx_ref[...].astype(BF16)
        dxh_ref[...] = dxh
        dmerged = _bdot(dxh, wo_ref[...], NT_DIMS)
        merged = jnp.zeros((tm, d), F32)

        @pl.when(pl.program_id(0) == 0)
        def _():
            dbg_ref[...] = jnp.zeros_like(dbg_ref)

        for n, (g_ref, y_ref, dy_ref) in enumerate(((g0_ref, yp_ref, dyp_ref), (g1_ref, yd_ref, dyd_ref),
                                                    (g2_ref, ys_ref, dys_ref))):
            gate = _sigmoid(g_ref[...].astype(F32) + bg_ref[:, n * d:(n + 1) * d])
            bd = _bdot(y_ref[...], wb_ref[n])
            merged = merged + gate * bd
            dgl = dmerged * bd * gate * (1.0 - gate)
            dgl_ref[:, n * d:(n + 1) * d] = dgl.astype(BF16)
            dbg_ref[:, n * d:(n + 1) * d] += jnp.sum(dgl, axis=0, keepdims=True)
            dbd = (dmerged * gate).astype(BF16)
            dbd_ref[n] = dbd
            dy_ref[...] = _bdot(dbd, wb_ref[n], NT_DIMS)
        mg_ref[...] = merged.astype(BF16)

    row = pl.BlockSpec((tm, d), lambda i: (i, 0))
    yblk = pl.BlockSpec((tm, BW), lambda i: (i, 0))
    bgv = pl.BlockSpec((1, 3 * d), lambda i: (0, 0))

    def gl(n):
        return pl.BlockSpec((tm, d), lambda i: (i, CB_GATE + n))

    ysds = jax.ShapeDtypeStruct((t, BW), F32)
    return pl.pallas_call(
        body, grid=(t // tm,),
        in_specs=[gl(0), gl(1), gl(2), yblk, yblk, yblk, bgv,
                  pl.BlockSpec((3, BW, d), lambda i: (0, 0, 0)), pl.BlockSpec((d, d), lambda i: (0, 0)), row],
        out_specs=[yblk, yblk, yblk, pl.BlockSpec((tm, 3 * d), lambda i: (i, 0)), row, row,
                   pl.BlockSpec((3, tm, d), lambda i: (0, i, 0)), bgv],
        out_shape=[ysds, ysds, ysds, jax.ShapeDtypeStruct((t, 3 * d), BF16), jax.ShapeDtypeStruct((t, d), BF16),
                   jax.ShapeDtypeStruct((t, d), BF16), jax.ShapeDtypeStruct((3, t, d), BF16),
                   jax.ShapeDtypeStruct((1, 3 * d), F32)],
        compiler_params=_cp(("arbitrary",)), name=name)(proj, proj, proj, yp, yd, ys, b_gate, wb, wo, dx)


def _loss_head(x, g, target, *, name):
    t, d = x.shape
    tm = _pick(t, (512, 256, 128))

    def body(x_ref, g_ref, t_ref, dx_ref, dg_ref, loss_ref):
        xhat, rstd = _rms_stats(x_ref[...])
        err = xhat * g_ref[...] - t_ref[...]
        dx, dg = _rms_bwd_vals(err * (1.0 / d), xhat, rstd, g_ref[...])
        dx_ref[...] = dx

        @pl.when(pl.program_id(0) == 0)
        def _():
            dg_ref[...] = jnp.zeros_like(dg_ref)
            loss_ref[...] = jnp.zeros_like(loss_ref)

        dg_ref[...] += dg
        part = jnp.sum(jnp.sum(err * err, axis=1, keepdims=True), axis=0, keepdims=True) * (0.5 / d)
        loss_ref[...] += jnp.broadcast_to(part, (1, LANE))

    row = pl.BlockSpec((tm, d), lambda i: (i, 0))
    vec = pl.BlockSpec((1, d), lambda i: (0, 0))
    return pl.pallas_call(
        body, grid=(t // tm,), in_specs=[row, vec, row],
        out_specs=[row, vec, pl.BlockSpec((1, LANE), lambda i: (0, 0))],
        out_shape=[jax.ShapeDtypeStruct((t, d), F32), jax.ShapeDtypeStruct((1, d), F32),
                   jax.ShapeDtypeStruct((1, LANE), F32)],
        compiler_params=_cp(("arbitrary",)), name=name)(x, g.reshape(1, d), target)


def _adamw(w, g, m, v, *, name):
    rows, cols = w.shape
    fits = [c for c in (1024, 704, 512, 352, 256, 128, 64, 32, 16, 8) if c * cols * 4 * 14 <= VMEM_LIMIT // 2]
    tr = _pick(rows, fits)
    c1 = 1.0 / (1.0 - ADAM_B1 ** ADAM_STEP)
    c2 = 1.0 / (1.0 - ADAM_B2 ** ADAM_STEP)

    def body(w_ref, g_ref, m_ref, v_ref, d_ref, nm_ref, nv_ref):
        g = g_ref[...]
        nm = ADAM_B1 * m_ref[...] + (1.0 - ADAM_B1) * g
        nv = ADAM_B2 * v_ref[...] + (1.0 - ADAM_B2) * (g * g)
        nm_ref[...] = nm
        nv_ref[...] = nv
        d_ref[...] = -ADAM_LR * ((nm * c1) / (jnp.sqrt(nv * c2) + ADAM_EPS) + ADAM_WD * w_ref[...])

    blk = pl.BlockSpec((tr, cols), lambda i: (i, 0))
    sds = jax.ShapeDtypeStruct((rows, cols), F32)
    return pl.pallas_call(
        body, grid=(rows // tr,), in_specs=[blk] * 4, out_specs=[blk] * 3, out_shape=[sds] * 3,
        compiler_params=_cp(("parallel",)), name=name)(w, g, m, v)


MESH_ID = pl.DeviceIdType.MESH
HBM_SPEC = pl.BlockSpec(memory_space=pl.ANY)
OTHER_CHIPS = ((1, 0), (0, 1), (1, 1))


def _at_slot(ref, nl, slot):
    return ref.at[(slice(None),) * nl + (slot,)]


def _slotted(shape, nl, slots):
    return tuple(shape[:nl]) + (slots,) + tuple(shape[nl:])


def _flip(v, f):
    return 1 - v if f else v


def _comm_call(body, n, out_shapes, n_remote, args, name):
    return pl.pallas_call(
        body, out_shape=out_shapes, in_specs=[HBM_SPEC] * len(args), out_specs=[HBM_SPEC] * len(out_shapes),
        scratch_shapes=[pltpu.SemaphoreType.DMA((n * n_remote,)), pltpu.SemaphoreType.DMA((n * n_remote,)),
                        pltpu.SemaphoreType.DMA((n * 4,))],
        compiler_params=pltpu.CompilerParams(has_side_effects=True), name=name)(*args)


def _gather(xs, nls, *, name):
    n = len(xs)

    def body(*refs):
        start, finish = _gather_parts(refs[:n], refs[n:2 * n], nls, *refs[2 * n:])
        start()
        finish()

    return _comm_call(body, n, _gather_shapes(xs, nls), 7, xs, name)


def _gather_shapes(xs, nls):
    return [jax.ShapeDtypeStruct(_slotted(v.shape, nl, N_DEV), v.dtype) for v, nl in zip(xs, nls)]


def _comm_sems(n, n_remote):
    return [pltpu.SemaphoreType.DMA((n * n_remote,)), pltpu.SemaphoreType.DMA((n * n_remote,)),
            pltpu.SemaphoreType.DMA((n * 4,))]


def _gather_parts(x_refs, o_refs, nls, send_sems, recv_sems, local_sems):
    n = len(x_refs)
    x, y, c = lax.axis_index("x"), lax.axis_index("y"), lax.axis_index("c")
    me, sibling = (x, y, c), (x, y, 1 - c)
    chips = [(_flip(x, fx), _flip(y, fy)) for fx, fy in OTHER_CHIPS]

    def copy(a, k, block, to, src=None):
        dst = _at_slot(o_refs[a], nls[a], 4 * block[0] + 2 * block[1] + block[2])
        return pltpu.make_async_remote_copy(
            src_ref=dst if src is None else src, dst_ref=dst, send_sem=send_sems.at[a * 7 + k],
            recv_sem=recv_sems.at[a * 7 + k], device_id=to, device_id_type=MESH_ID)

    def mine(a):
        return pltpu.make_async_copy(x_refs[a], _at_slot(o_refs[a], nls[a], 4 * x + 2 * y + c), local_sems.at[a])

    def first(a):
        return ([copy(a, 0, me, sibling, src=x_refs[a])]
                + [copy(a, 1 + j, me, (*chip, c), src=x_refs[a]) for j, chip in enumerate(chips)])

    def start():
        for a in range(n):
            mine(a).start()
            for cp in first(a):
                cp.start()

    def finish():
        passed = []
        for j, chip in enumerate(chips):
            for a in range(n):
                copy(a, 1 + j, (*chip, c), me).wait_recv()
                passed.append(copy(a, 4 + j, (*chip, c), sibling))
                passed[-1].start()
        for a in range(n):
            copy(a, 0, sibling, me).wait_recv()
            for j, chip in enumerate(chips):
                copy(a, 4 + j, (*chip, 1 - c), me).wait_recv()
        for a in range(n):
            for cp in first(a):
                cp.wait_send()
        for cp in passed:
            cp.wait_send()
        for a in range(n):
            mine(a).wait()

    return start, finish


ALL_FLIPS = ((0, 0, 1), (0, 1, 0), (0, 1, 1), (1, 0, 0), (1, 0, 1), (1, 1, 0), (1, 1, 1))


def _all_to_all_parts(g_refs, r_refs, nls, send_sems, recv_sems, local_sems):
    del local_sems
    n = len(g_refs)
    x, y, c = lax.axis_index("x"), lax.axis_index("y"), lax.axis_index("c")

    def copies():
        out = []
        for a in range(n):
            for k, (fx, fy, fc) in enumerate(ALL_FLIPS):
                p = (_flip(x, fx), _flip(y, fy), _flip(c, fc))
                out.append(pltpu.make_async_remote_copy(
                    src_ref=_at_slot(g_refs[a], nls[a], 4 * p[0] + 2 * p[1] + p[2]), dst_ref=_at_slot(r_refs[a], nls[a], k),
                    send_sem=send_sems.at[a * 7 + k], recv_sem=recv_sems.at[a * 7 + k], device_id=p,
                    device_id_type=MESH_ID))
        return out

    def start():
        for cp in copies():
            cp.start()

    def finish():
        cps = copies()
        for cp in cps:
            cp.wait_recv()
        for cp in cps:
            cp.wait_send()

    return start, finish


def _all_to_all_shapes(gs, nls):
    return [jax.ShapeDtypeStruct(_slotted(v.shape[:nl] + v.shape[nl + 1:], nl, 7), v.dtype) for v, nl in zip(gs, nls)]


def _scatter_pair(gs, nls, *, name):
    n = len(gs)

    def body(*refs):
        g_refs, got_refs, (send_sems, recv_sems, _) = refs[:n], refs[n:2 * n], refs[2 * n:]
        x, y, c = lax.axis_index("x"), lax.axis_index("y"), lax.axis_index("c")
        remote = []
        for a in range(n):
            for q in range(4):
                rc = pltpu.make_async_remote_copy(
                    src_ref=_at_slot(g_refs[a], nls[a], 2 * q + 1 - c), dst_ref=_at_slot(got_refs[a], nls[a], q),
                    send_sem=send_sems.at[a * 4 + q], recv_sem=recv_sems.at[a * 4 + q], device_id=(x, y, 1 - c),
                    device_id_type=MESH_ID)
                rc.start()
                remote.append(rc)
        for rc in remote:
            rc.wait_recv()
        for rc in remote:
            rc.wait_send()

    outs = [jax.ShapeDtypeStruct(_slotted(v.shape[:nl] + v.shape[nl + 1:], nl, 4), v.dtype) for v, nl in zip(gs, nls)]
    return _comm_call(body, n, outs, 4, gs, name)


def _scatter_chips(ps, nls, *, name):
    n = len(ps)

    def body(*refs):
        p_refs, r_refs, (send_sems, recv_sems, _) = refs[:n], refs[n:2 * n], refs[2 * n:]
        x, y, c = lax.axis_index("x"), lax.axis_index("y"), lax.axis_index("c")
        remote = []
        for a in range(n):
            for k, (fx, fy) in enumerate(OTHER_CHIPS):
                tx, ty = _flip(x, fx), _flip(y, fy)
                rc = pltpu.make_async_remote_copy(
                    src_ref=_at_slot(p_refs[a], nls[a], 2 * tx + ty), dst_ref=_at_slot(r_refs[a], nls[a], k),
                    send_sem=send_sems.at[a * 3 + k], recv_sem=recv_sems.at[a * 3 + k], device_id=(tx, ty, c),
                    device_id_type=MESH_ID)
                rc.start()
                remote.append(rc)
        for rc in remote:
            rc.wait_recv()
        for rc in remote:
            rc.wait_send()

    outs = [jax.ShapeDtypeStruct(_slotted(v.shape[:nl] + v.shape[nl + 1:], nl, 3), v.dtype) for v, nl in zip(ps, nls)]
    return _comm_call(body, n, outs, 3, ps, name)


def _pair_add(g, got, core, *, name):
    rows, cols = g.shape[-2:]
    lf = math.prod(got.shape[:-3])
    tr = _pick(rows, (1024, 512, 352, 256, 128))

    def body(core_ref, g_ref, got_ref, o_ref):
        o_ref[...] = (g_ref[...].astype(F32) + got_ref[...].astype(F32)).astype(BF16)

    blk = pl.BlockSpec((None, None, tr, cols), lambda i, q, j, core_ref: (i, q, j, 0))
    out = pl.pallas_call(
        body, grid_spec=pltpu.PrefetchScalarGridSpec(
            num_scalar_prefetch=1, grid=(lf, 4, rows // tr),
            in_specs=[pl.BlockSpec((None, None, None, tr, cols), lambda i, q, j, core_ref: (i, q, core_ref[0], j, 0)),
                      blk], out_specs=blk),
        out_shape=jax.ShapeDtypeStruct((lf, 4, rows, cols), BF16),
        compiler_params=_cp(("parallel", "parallel", "parallel")), name=name)(
            core, g.reshape(lf, 4, 2, rows, cols), got.reshape(lf, 4, rows, cols))
    return out.reshape(got.shape)


def _sum_adamw(p, r, own, w, m, v, layer, prev, *, name):
    shape = w.shape[1:]
    rows, cols = shape[-2:]
    lf = math.prod(shape[:-2])
    np_, nk = p.shape[-3], r.shape[-3]
    fits = [c for c in (1024, 512, 352, 256, 128, 64, 32, 16) if c * cols * (7 * 4 + (nk + 1) * 2) * 2 <= VMEM_LIMIT // 2]
    tr = _pick(rows, fits)
    c1 = 1.0 / (1.0 - ADAM_B1 ** ADAM_STEP)
    c2 = 1.0 / (1.0 - ADAM_B2 ** ADAM_STEP)

    def body(own_ref, p_ref, r_ref, w_ref, m_ref, v_ref, *rest):
        g_ref, d_ref, nm_ref, nv_ref = rest[-4:]
        g = p_ref[...].astype(F32)
        for k in range(nk):
            g = g + r_ref[k].astype(F32)
        g_ref[...] = g
        nm = ADAM_B1 * m_ref[...] + (1.0 - ADAM_B1) * g
        nv = ADAM_B2 * v_ref[...] + (1.0 - ADAM_B2) * (g * g)
        nm_ref[...] = nm
        nv_ref[...] = nv
        d_ref[...] = -ADAM_LR * ((nm * c1) / (jnp.sqrt(nv * c2) + ADAM_EPS) + ADAM_WD * w_ref[...])

    wblk = pl.BlockSpec((None, None, tr, cols), lambda i, j, own_ref: (layer, i, j, 0))
    full = (w.shape[0], lf, rows, cols)
    sds = jax.ShapeDtypeStruct(full, F32)
    prev = [] if prev is None else [a.reshape(full) for a in prev]
    outs = pl.pallas_call(
        body, grid_spec=pltpu.PrefetchScalarGridSpec(
            num_scalar_prefetch=1, grid=(lf, rows // tr),
            in_specs=[pl.BlockSpec((None, None, tr, cols), lambda i, j, own_ref: (i, own_ref[0], j, 0)),
                      pl.BlockSpec((None, nk, tr, cols), lambda i, j, own_ref: (i, 0, j, 0))] + [wblk] * 3
            + [HBM_SPEC] * len(prev),
            out_specs=[wblk] * 4),
        out_shape=[sds] * 4, input_output_aliases={6 + i: i for i in range(len(prev))},
        compiler_params=_cp(("parallel", "parallel")), name=name)(
            own, p.reshape(lf, np_, rows, cols), r.reshape(lf, nk, rows, cols), w.reshape(full), m.reshape(full),
            v.reshape(full), *prev)
    return [o.reshape(w.shape) for o in outs]


def _sum_slots(x, *, name):
    nd, rows, cols = x.shape
    tr = _pick(rows, (512, 256, 128, 64, 32, 16, 8))

    def body(x_ref, o_ref):
        acc = x_ref[0].astype(F32)
        for j in range(1, nd):
            acc = acc + x_ref[j].astype(F32)
        o_ref[...] = acc

    return pl.pallas_call(
        body, grid=(rows // tr,), in_specs=[pl.BlockSpec((nd, tr, cols), lambda i: (0, i, 0))],
        out_specs=pl.BlockSpec((tr, cols), lambda i: (i, 0)), out_shape=jax.ShapeDtypeStruct((rows, cols), F32),
        compiler_params=_cp(("parallel",)), name=name)(x)


def _pad_rows(a, mult=8):
    r = (-a.shape[0]) % mult
    return jnp.pad(a, ((0, r), (0, 0))) if r else a


def _flat128(a):
    f = a.reshape(-1)
    return jnp.pad(f, (0, (-f.shape[0]) % LANE)).reshape(-1, LANE)


def _unshard(gathered, shape, axis):
    g = gathered.reshape((N_DEV,) + tuple(shape))
    g = jnp.moveaxis(g, 0, axis)
    full = list(shape)
    full[axis] *= N_DEV
    return g.reshape(full)


def _col_shards(full):
    rows, cols = full.shape
    return jnp.moveaxis(full.reshape(rows, N_DEV, cols // N_DEV), 1, 0)


BIG = (("ffn_w_gate", 2), ("ffn_w_up", 2), ("ffn_w_down", 2), ("w_in", 1), ("w_branch", 2), ("w_out", 1))


def kernel(x, ffn_norm, ffn_w_gate, ffn_w_up, ffn_w_down, mix_norm, w_in, b_gate, pool_w, pool_scale, dn_conv, dn_A_log, dn_dt_bias, dn_out_norm, w_branch, w_out, final_norm, loss_target, m_ffn_norm, m_ffn_w_gate, m_ffn_w_up, m_ffn_w_down, m_mix_norm, m_w_in, m_b_gate, m_pool_w, m_pool_scale, m_dn_conv, m_dn_A_log, m_dn_dt_bias, m_dn_out_norm, m_w_branch, m_w_out, m_final_norm, v_ffn_norm, v_ffn_w_gate, v_ffn_w_up, v_ffn_w_down, v_mix_norm, v_w_in, v_b_gate, v_pool_w, v_pool_scale, v_dn_conv, v_dn_A_log, v_dn_dt_bias, v_dn_out_norm, v_w_branch, v_w_out, v_final_norm):
    wts = dict(ffn_norm=ffn_norm, ffn_w_gate=ffn_w_gate, ffn_w_up=ffn_w_up, ffn_w_down=ffn_w_down, mix_norm=mix_norm,
               w_in=w_in, b_gate=b_gate, pool_w=pool_w, pool_scale=pool_scale, dn_conv=dn_conv, dn_A_log=dn_A_log,
               dn_dt_bias=dn_dt_bias, dn_out_norm=dn_out_norm, w_branch=w_branch, w_out=w_out, final_norm=final_norm)
    mom = dict(ffn_norm=m_ffn_norm, ffn_w_gate=m_ffn_w_gate, ffn_w_up=m_ffn_w_up, ffn_w_down=m_ffn_w_down,
               mix_norm=m_mix_norm, w_in=m_w_in, b_gate=m_b_gate, pool_w=m_pool_w, pool_scale=m_pool_scale,
               dn_conv=m_dn_conv, dn_A_log=m_dn_A_log, dn_dt_bias=m_dn_dt_bias, dn_out_norm=m_dn_out_norm,
               w_branch=m_w_branch, w_out=m_w_out, final_norm=m_final_norm)
    var = dict(ffn_norm=v_ffn_norm, ffn_w_gate=v_ffn_w_gate, ffn_w_up=v_ffn_w_up, ffn_w_down=v_ffn_w_down,
               mix_norm=v_mix_norm, w_in=v_w_in, b_gate=v_b_gate, pool_w=v_pool_w, pool_scale=v_pool_scale,
               dn_conv=v_dn_conv, dn_A_log=v_dn_A_log, dn_dt_bias=v_dn_dt_bias, dn_out_norm=v_dn_out_norm,
               w_branch=v_w_branch, w_out=v_w_out, final_norm=v_final_norm)
    nb, s, d = x.shape
    t = nb * s
    me = 4 * lax.axis_index("x") + 2 * lax.axis_index("y") + lax.axis_index("c")

    big = [n for n, _ in BIG]
    nls = [nl - 1 for _, nl in BIG]
    shards = lambda l: [wts[n][l].astype(BF16) for n in big]
    small_sh = jnp.concatenate([_flat128(ffn_norm), _flat128(dn_conv)], axis=0)
    ffn3 = big[:3]
    *pre0, small_g = _gather([wts[n][0, 0].astype(BF16) for n in ffn3] + [small_sh], [0] * 4, name="gather_weights")
    rest0 = [wts[n][0, 1].astype(BF16) for n in ffn3] + [wts[n][0].astype(BF16) for n in big[3:]]
    rest0_nls = [0] * 3 + nls[3:]
    full = [None] * DEPTH

    def mixer_weights(l):
        w_in_full = jnp.moveaxis(full[l]["w_in"], 0, 1).reshape(d, -1)
        w_main = jnp.concatenate([w_in_full[:, :AB_LO], w_in_full[:, AB_HI:]], axis=1)
        w_ab = jnp.pad(w_in_full[:, AB_LO:AB_HI], ((0, 0), (0, LANE - (AB_HI - AB_LO))))
        wb = jnp.moveaxis(full[l]["w_branch"], 1, 2).reshape(3, BW, d)
        return w_main, w_ab, wb, full[l]["w_out"].reshape(d, d)

    nfr = ffn_norm.size // LANE
    ffn_norm_full = _unshard(small_g[:, :nfr], ffn_norm.shape, 2)
    dn_conv_full = _unshard(small_g[:, nfr:], dn_conv.shape, 2)
    pool_w_h = pool_w.astype(BF16)

    xs = x.reshape(t, d)
    saved = []
    for l in range(DEPTH):
        sv = dict(x0=xs)
        if l == 0:
            xs, a0, b0, got = _ffn_fwd(xs, ffn_norm_full[0, 0], *pre0, name="ffn_fwd_gather", gather=(rest0, rest0_nls))
            full[0] = dict(zip(ffn3, zip(pre0, got[:3])), **dict(zip(big[3:], got[3:])))
        else:
            xs, a0, b0, _ = _ffn_fwd(xs, ffn_norm_full[l, 0], full[l]["ffn_w_gate"][0], full[l]["ffn_w_up"][0],
                                     full[l]["ffn_w_down"][0], name="ffn_fwd")
        sv["ab0"] = (a0, b0)
        sv["x1"] = xs
        w_main, w_ab, wb, wo = mixer_weights(l)
        h = _rms_fwd(xs, mix_norm[l], name="mix_rms")
        proj = _mm(h, w_main, out_dtype=BF16, name="proj")
        ab = _mm(h, w_ab, name="proj_ab")
        par = jnp.pad(jnp.stack([dn_A_log[l], dn_dt_bias[l]]), ((0, 6), (0, LANE - NH)))
        gain = dn_out_norm[l].reshape(1, HD)
        psc = pool_scale[l].reshape(1, BW)
        yp = _pool_fwd(proj, pool_w_h[l], psc, nb, s, name="pool_fwd")
        yd, o_pre, states, dn_local, gat = _dn_fwd(proj, ab, dn_conv_full[l], par, gain, nb, s,
                                         name="dn_fwd" if l == DEPTH - 1 else "dn_fwd_gather",
                                         gather=(shards(l + 1), nls) if l < DEPTH - 1 else None)
        if l < DEPTH - 1:
            full[l + 1] = dict(zip(big, gat))
        ys, sb_ctr = _sb_fwd(proj, nb, s, name="sb_fwd")
        bg = b_gate[l].reshape(1, 3 * d)
        xs = _merge_fwd(xs, proj, yp, yd, ys, bg, wb, wo, name="merge_fwd")
        sv.update(x2=xs, h=h, proj=proj, ab=ab, par=par, gain=gain, psc=psc, yp=yp, yd=yd, ys=ys, sb_ctr=sb_ctr, o_pre=o_pre,
                  states=states, dn_local=dn_local, bg=bg, w_main=w_main, w_ab=w_ab, wb=wb, wo=wo)
        xs, a1, b1, _ = _ffn_fwd(xs, ffn_norm_full[l, 1], full[l]["ffn_w_gate"][1], full[l]["ffn_w_up"][1],
                                 full[l]["ffn_w_down"][1], name="ffn_fwd")
        sv["ab1"] = (a1, b1)
        saved.append(sv)

    dx, g_final, loss_row = _loss_head(xs, final_norm, loss_target.reshape(t, d), name="loss_head")
    loss = lax.psum(loss_row[0, 0], ("x", "y", "c"))

    gw = {n: [None] * DEPTH for n in ("ffn_norm", "ffn_w_gate", "ffn_w_up", "ffn_w_down", "mix_norm", "w_in", "b_gate",
                                      "pool_w", "pool_scale", "dn_conv", "dn_A_log", "dn_dt_bias", "dn_out_norm",
                                      "w_branch", "w_out")}

    me_i = me.astype(jnp.int32).reshape(1)
    updated = {n: None for n in big}
    pending = None

    def finish_layer(l, own_blocks, arrived, own_slot):
        for n, p, r in zip(big, own_blocks, arrived):
            updated[n] = _sum_adamw(p, r, own_slot, wts[n], mom[n], var[n], l, updated[n], name=f"adamw_{n}_{l}")

    def ffn_back(l, i, x_in, dy):
        dxi, dg, hb, dyh, da, db, sact = _ffn_bwd(x_in, ffn_norm_full[l, i], full[l]["ffn_w_gate"][i],
                                                  full[l]["ffn_w_up"][i], full[l]["ffn_w_down"][i],
                                                  *saved[l][f"ab{i}"], dy, name="ffn_bwd")
        return dxi, dg, (_mm_slots(hb, da, name="dw_gate_up"), _mm_slots(hb, db, name="dw_gate_up"),
                         _mm_slots(sact, dyh, name="dw_down"))

    for l in reversed(range(DEPTH)):
        sv = saved[l]
        dx, dg1, (dwg1, dwu1, dwd1) = ffn_back(l, 1, sv["x2"], dx)
        dyp, dyd, dys, dgl, merged, dxh, dbd, dbg = _merge_bwd(sv["proj"], sv["yp"], sv["yd"], sv["ys"], sv["bg"],
                                                               sv["wb"], sv["wo"], dx, name="merge_bwd")
        gw["w_out"][l] = _mm(merged, dxh, ta=True, out_dtype=BF16, name="dw_out").reshape(N_DEV, d // N_DEV, d)
        gw["w_branch"][l] = jnp.stack([_col_shards(_mm(y, dbd[n], ta=True, out_dtype=BF16, name="dw_branch"))
                                       for n, y in enumerate((sv["yp"], sv["yd"], sv["ys"]))])
        gw["b_gate"][l] = dbg.reshape(3 * d)
        du, dpw, dps = _pool_bwd(sv["proj"], pool_w_h[l], sv["psc"], dyp, nb, s, name="pool_bwd")
        gw["pool_w"][l], gw["pool_scale"][l] = dpw, dps.reshape(BW)
        own = [gw[n][pending] for n in big] if pending is not None else []
        dqr, dkr, dvr, dz, dab4, dcq, dck, dcv, dpar, dgain, arrived_ffn = _dn_bwd(
            sv["proj"], sv["ab"], dn_conv_full[l], sv["par"], sv["gain"], sv["o_pre"], sv["states"], sv["dn_local"],
            dyd, nb, s, name="dn_bwd_scatter" if own else "dn_bwd", scatter=(own[:3], nls[:3]) if own else None)
        gw["dn_conv"][l] = jnp.concatenate([dcq, dck, dcv], axis=1)
        gw["dn_A_log"][l], gw["dn_dt_bias"][l], gw["dn_out_norm"][l] = dpar[:, 0, 0], dpar[:, 1, 0], dgain.reshape(HD)
        dsq, dsk, dsv, arrived_rest = _sb_bwd(sv["proj"], sv["sb_ctr"], dys, nb, s,
                                              name="sb_bwd_scatter" if own else "sb_bwd",
                                              scatter=(own[3:], nls[3:]) if own else None)
        if own:
            finish_layer(pending, own, arrived_ffn + arrived_rest, me_i)
        dab = _sum_heads(dab4, name="sum_heads")
        dproj = jnp.concatenate([du.astype(BF16), dqr.astype(BF16), dkr.astype(BF16), dvr.astype(BF16),
                                 dz.astype(BF16), dsq.astype(BF16), dsk.astype(BF16), dsv.astype(BF16), dgl], axis=1)
        dw_main = _mm(sv["h"], dproj, ta=True, out_dtype=BF16, name="dw_in")
        dw_ab = _mm(sv["h"], dab, ta=True, out_dtype=BF16, name="dw_ab")
        gw["w_in"][l] = _col_shards(jnp.concatenate([dw_main[:, :AB_LO], dw_ab[:, :AB_HI - AB_LO],
                                                     dw_main[:, AB_LO:]], axis=1))
        dh_main = _mm(dproj, sv["w_main"], tb=True, name="dh_mix")
        dh_ab = _mm(dab, sv["w_ab"], tb=True, name="dh_mix_ab")
        dx, dgm = _rms_bwd(sv["x1"], mix_norm[l], dh_main, dh_ab, dx, name="mix_rms_bwd")
        gw["mix_norm"][l] = dgm.reshape(d)
        dx, dg0, (dwg0, dwu0, dwd0) = ffn_back(l, 0, sv["x0"], dx)
        gw["ffn_norm"][l] = jnp.stack([dg0.reshape(d), dg1.reshape(d)])
        gw["ffn_w_gate"][l] = jnp.stack([dwg0, dwg1])
        gw["ffn_w_up"][l] = jnp.stack([dwu0, dwu1])
        gw["ffn_w_down"][l] = jnp.stack([dwd0, dwd1])
        pending = l
    grad_x = dx.reshape(nb, s, d)

    core = lax.axis_index("c").astype(jnp.int32).reshape(1)
    chip = (2 * lax.axis_index("x") + lax.axis_index("y")).astype(jnp.int32).reshape(1)
    last = [gw[n][0] for n in big]
    got = _scatter_pair(last, nls, name="scatter_grads_pair")
    chip_sums = [_pair_add(g, b, core, name="add_pair_" + n) for n, g, b in zip(big, last, got)]
    finish_layer(0, chip_sums, _scatter_chips(chip_sums, nls, name="scatter_grads_chips"), chip)
    grads, delta, new_m, new_v = ({n: updated[n][i] for n in big} for i in range(4))
    gw = {n: jnp.stack(v) for n, v in gw.items() if n not in big}
    gw["final_norm"] = g_final.reshape(d)

    small = ("ffn_norm", "mix_norm", "b_gate", "pool_w", "pool_scale", "dn_conv", "dn_A_log", "dn_dt_bias",
             "dn_out_norm", "final_norm")
    sp = _pad_rows(jnp.concatenate([_flat128(gw[n]) for n in small], axis=0))
    ssum = _sum_slots(_gather([sp], [0], name="gather_small_grads")[0], name="sum_small_grads")
    off = 0
    for n in small:
        r = -(-gw[n].size // LANE)
        g = ssum[off:off + r].reshape(-1)[:gw[n].size].reshape(gw[n].shape)
        off += r
        if n in ("ffn_norm", "dn_conv"):
            w = wts[n].shape[2]
            g = lax.dynamic_slice_in_dim(g, me * w, w, axis=2)
        grads[n] = g

    pk = lambda src: _pad_rows(jnp.concatenate([_flat128(src[n]) for n in small], axis=0))
    dl, nm, nv = _adamw(pk(wts), pk(grads), pk(mom), pk(var), name="adamw_small")
    off = 0
    for n in small:
        r = -(-wts[n].size // LANE)
        for dst, src in ((delta, dl), (new_m, nm), (new_v, nv)):
            dst[n] = src[off:off + r].reshape(-1)[:wts[n].size].reshape(wts[n].shape)
        off += r

    order = ("ffn_norm", "ffn_w_gate", "ffn_w_up", "ffn_w_down", "mix_norm", "w_in", "b_gate", "pool_w", "pool_scale",
             "dn_conv", "dn_A_log", "dn_dt_bias", "dn_out_norm", "w_branch", "w_out", "final_norm")
    return (loss, grad_x, *[grads[n] for n in order], *[delta[n] for n in order], *[new_m[n] for n in order],
            *[new_v[n] for n in order])
```

```python
import functools
import math

import jax
import jax.numpy as jnp
from jax import lax
from jax.experimental import pallas as pl
from jax.experimental.pallas import tpu as pltpu

F32, BF16 = jnp.float32, jnp.bfloat16
D_MODEL, D_FF, DEPTH = 1024, 2816, 4
BW = 512
HD = 128
NH = 4
DN_CHUNK = 64
EPS = 1e-6
N_DEV = 8
LANE = 128
CB_POOL, CB_DNQ, CB_DNK, CB_DNV, CB_DNZ, CB_SBQ, CB_SBK, CB_SBV = 0, 4, 8, 12, 16, 20, 24, 28
CB_GATE = 4
P_MAIN = 7168
AB_LO, AB_HI = 2560, 2568
ADAM_LR, ADAM_B1, ADAM_B2, ADAM_EPS, ADAM_WD, ADAM_STEP = 0.001, 0.9, 0.999, 1e-08, 0.01, 10
VMEM_LIMIT = 56 * 1024 * 1024
HIGHEST = lax.Precision.HIGHEST
NT_DIMS = (((1,), (1,)), ((), ()))
TN_DIMS = (((0,), (0,)), ((), ()))
NN_DIMS = (((1,), (0,)), ((), ()))


def _cp(dims=None, vmem=VMEM_LIMIT):
    return pltpu.CompilerParams(dimension_semantics=dims, vmem_limit_bytes=vmem)


def _pick(n, cands):
    for c in cands:
        if n % c == 0:
            return c
    return n


def _bdot(a, b, dims=NN_DIMS):
    return lax.dot_general(a.astype(BF16), b.astype(BF16), dims, preferred_element_type=F32)


def _hdot(a, b, dims=NN_DIMS):
    return lax.dot_general(a, b, dims, precision=lax.Precision.HIGH, preferred_element_type=F32)


def _split_dot(x, m01):
    hi = x.astype(BF16)
    lo = (x - hi.astype(F32)).astype(BF16)
    return (lax.dot_general(hi, m01, NN_DIMS, preferred_element_type=F32)
            + lax.dot_general(lo, m01, NN_DIMS, preferred_element_type=F32))


def _sigmoid(x):
    return 1.0 / (1.0 + jnp.exp(-x))


def _log_sigmoid(x):
    return jnp.minimum(x, 0.0) - jnp.log1p(jnp.exp(-jnp.abs(x)))


def _softplus(x):
    return jnp.maximum(x, 0.0) + jnp.log1p(jnp.exp(-jnp.abs(x)))


def _shift_down(x, k):
    r = lax.broadcasted_iota(jnp.int32, x.shape, 0)
    return jnp.where(r >= k, pltpu.roll(x, k, 0), 0.0)


def _shift_up(x, k):
    n = x.shape[0]
    r = lax.broadcasted_iota(jnp.int32, x.shape, 0)
    return jnp.where(r < n - k, pltpu.roll(x, n - k, 0), 0.0)


def _mm(a, b, *, ta=False, tb=False, out_dtype=F32, name):
    (kk, m) = a.shape if ta else a.shape[::-1]
    (k2, n) = b.shape[::-1] if tb else b.shape
    assert kk == k2, (a.shape, b.shape, ta, tb)
    bm = _pick(m, (1024, 512, 256, 128))
    bn = _pick(n, (1024, 1408, 512, 256, 128))
    bk = _pick(kk, (1024, 512, 256, 128))
    nk = kk // bk
    dims = (((0 if ta else 1,), (1 if tb else 0,)), ((), ()))

    def body(a_ref, b_ref, o_ref, acc_ref):
        k = pl.program_id(2)

        @pl.when(k == 0)
        def _():
            acc_ref[...] = jnp.zeros_like(acc_ref)

        acc_ref[...] += lax.dot_general(a_ref[...].astype(BF16), b_ref[...].astype(BF16), dims,
                                        preferred_element_type=F32)

        @pl.when(k == nk - 1)
        def _():
            o_ref[...] = acc_ref[...].astype(out_dtype)

    a_spec = (pl.BlockSpec((bk, bm), lambda i, j, k: (k, i)) if ta else pl.BlockSpec((bm, bk), lambda i, j, k: (i, k)))
    b_spec = (pl.BlockSpec((bn, bk), lambda i, j, k: (j, k)) if tb else pl.BlockSpec((bk, bn), lambda i, j, k: (k, j)))
    return pl.pallas_call(
        body, grid=(m // bm, n // bn, nk), in_specs=[a_spec, b_spec],
        out_specs=pl.BlockSpec((bm, bn), lambda i, j, k: (i, j)),
        out_shape=jax.ShapeDtypeStruct((m, n), out_dtype),
        scratch_shapes=[pltpu.VMEM((bm, bn), F32)],
        compiler_params=_cp(("parallel", "parallel", "arbitrary")), name=name)(a, b)


def _mm_slots(a, b, *, name):
    a3, b3 = a.ndim == 3, b.ndim == 3
    ns = a.shape[0] if a3 else b.shape[0]
    m, t = a.shape[-2:]
    n = b.shape[-1]
    bk = _pick(t, (512, 256, 128))
    nk = t // bk

    def body(a_ref, b_ref, o_ref, acc_ref):
        k = pl.program_id(0)

        @pl.when(k == 0)
        def _():
            acc_ref[...] = jnp.zeros_like(acc_ref)

        for s in range(ns):
            acc_ref[s] += _bdot(a_ref[s] if a3 else a_ref[...], b_ref[s] if b3 else b_ref[...])

        @pl.when(k == nk - 1)
        def _():
            o_ref[...] = acc_ref[...].astype(BF16)

    a_spec = pl.BlockSpec((ns, m, bk), lambda k: (0, 0, k)) if a3 else pl.BlockSpec((m, bk), lambda k: (0, k))
    b_spec = pl.BlockSpec((ns, bk, n), lambda k: (0, k, 0)) if b3 else pl.BlockSpec((bk, n), lambda k: (k, 0))
    return pl.pallas_call(
        body, grid=(nk,), in_specs=[a_spec, b_spec], out_specs=pl.BlockSpec((ns, m, n), lambda k: (0, 0, 0)),
        out_shape=jax.ShapeDtypeStruct((ns, m, n), BF16), scratch_shapes=[pltpu.VMEM((ns, m, n), F32)],
        compiler_params=_cp(("arbitrary",)), name=name)(a, b)


def _rms_stats(x):
    rstd = lax.rsqrt(jnp.mean(x * x, axis=-1, keepdims=True) + EPS)
    return x * rstd, rstd


def _rms_bwd_vals(dh, xhat, rstd, g):
    dxh = dh * g
    dx = rstd * (dxh - xhat * jnp.mean(dxh * xhat, axis=-1, keepdims=True))
    return dx, jnp.sum(dh * xhat, axis=0, keepdims=True)


def _rms_fwd(x, g, *, name):
    t, d = x.shape
    tm = _pick(t, (512, 256, 128))

    def body(x_ref, g_ref, h_ref):
        xhat, _ = _rms_stats(x_ref[...])
        h_ref[...] = (xhat * g_ref[...]).astype(BF16)

    return pl.pallas_call(
        body, grid=(t // tm,),
        in_specs=[pl.BlockSpec((tm, d), lambda i: (i, 0)), pl.BlockSpec((1, d), lambda i: (0, 0))],
        out_specs=pl.BlockSpec((tm, d), lambda i: (i, 0)), out_shape=jax.ShapeDtypeStruct((t, d), BF16),
        compiler_params=_cp(("parallel",)), name=name)(x, g.reshape(1, d))


def _rms_bwd(x, g, dh_a, dh_b, dres, *, name):
    t, d = x.shape
    tm = _pick(t, (512, 256, 128))

    def body(x_ref, g_ref, dha_ref, dhb_ref, dres_ref, dx_ref, dg_ref):
        xhat, rstd = _rms_stats(x_ref[...])
        dx, dg = _rms_bwd_vals(dha_ref[...] + dhb_ref[...], xhat, rstd, g_ref[...])
        dx_ref[...] = dres_ref[...] + dx

        @pl.when(pl.program_id(0) == 0)
        def _():
            dg_ref[...] = jnp.zeros_like(dg_ref)

        dg_ref[...] += dg

    row = pl.BlockSpec((tm, d), lambda i: (i, 0))
    vec = pl.BlockSpec((1, d), lambda i: (0, 0))
    return pl.pallas_call(
        body, grid=(t // tm,), in_specs=[row, vec, row, row, row], out_specs=[row, vec],
        out_shape=[jax.ShapeDtypeStruct((t, d), F32), jax.ShapeDtypeStruct((1, d), F32)],
        compiler_params=_cp(("arbitrary",)), name=name)(x, g.reshape(1, d), dh_a, dh_b, dres)


FFN_TM = 512
FFN_SLOTS = 2


def _ffn_fwd(x, g, wg, wu, wd, *, name, gather=None):
    t, d = x.shape
    nf, _, fc = wg.shape
    tm = _pick(t, (FFN_TM, 256, 128))
    gx, gnl = gather if gather else ([], [])

    def body(x_ref, g_ref, wg_ref, wu_ref, wd_ref, o_ref, a_ref, b_ref, h_ref, acc_ref):
        j = pl.program_id(1)

        @pl.when(j == 0)
        def _():
            xhat, _ = _rms_stats(x_ref[...])
            h_ref[...] = (xhat * g_ref[...]).astype(BF16)
            acc_ref[...] = jnp.zeros_like(acc_ref)

        h = h_ref[...]
        part = jnp.zeros((tm, d), F32)
        for q in range(FFN_SLOTS):
            a = _bdot(h, wg_ref[q])
            b = _bdot(h, wu_ref[q])
            a_ref[q] = a.astype(BF16)
            b_ref[q] = b.astype(BF16)
            part = part + _bdot(a * _sigmoid(a) * b, wd_ref[q])
        acc_ref[...] += part

        @pl.when(j == nf // FFN_SLOTS - 1)
        def _():
            o_ref[...] = x_ref[...] + 0.5 * acc_ref[...]

    row = pl.BlockSpec((tm, d), lambda i, j: (i, 0))
    grid = (t // tm, nf // FFN_SLOTS)
    exchange = (_gather_parts, len(gx), gnl) if gx else None
    res = pl.pallas_call(
        _with_exchange(body, 5, 3, 2, exchange, grid), grid=grid,
        in_specs=[row, pl.BlockSpec((1, d), lambda i, j: (0, 0)),
                  pl.BlockSpec((FFN_SLOTS, d, fc), lambda i, j: (j, 0, 0)),
                  pl.BlockSpec((FFN_SLOTS, d, fc), lambda i, j: (j, 0, 0)),
                  pl.BlockSpec((FFN_SLOTS, fc, d), lambda i, j: (j, 0, 0))] + [HBM_SPEC] * len(gx),
        out_specs=[row, pl.BlockSpec((FFN_SLOTS, tm, fc), lambda i, j: (j, i, 0)),
                   pl.BlockSpec((FFN_SLOTS, tm, fc), lambda i, j: (j, i, 0))] + [HBM_SPEC] * len(gx),
        out_shape=[jax.ShapeDtypeStruct((t, d), F32), jax.ShapeDtypeStruct((nf, t, fc), BF16),
                   jax.ShapeDtypeStruct((nf, t, fc), BF16)] + _gather_shapes(gx, gnl),
        scratch_shapes=[pltpu.VMEM((tm, d), BF16), pltpu.VMEM((tm, d), F32)] + (_comm_sems(len(gx), 7) if gx else []),
        compiler_params=_cp(("arbitrary", "arbitrary")), name=name)(x, g.reshape(1, d), wg, wu, wd, *gx)
    return res[0], res[1], res[2], list(res[3:])


def _ffn_bwd(x, g, wg, wu, wd, a_pre, b_pre, dy, *, name):
    t, d = x.shape
    nf, _, fc = wg.shape
    tm = _pick(t, (FFN_TM, 256, 128))

    def body(x_ref, g_ref, wg_ref, wu_ref, wd_ref, a_ref, b_ref, dy_ref,
             dx_ref, dg_ref, ht_ref, dyh_ref, da_ref, db_ref, st_ref, acc_ref):
        i, j = pl.program_id(0), pl.program_id(1)

        @pl.when(j == 0)
        def _():
            xhat, _ = _rms_stats(x_ref[...])
            ht_ref[...] = (xhat * g_ref[...]).T.astype(BF16)
            dyh_ref[...] = (0.5 * dy_ref[...]).astype(BF16)
            acc_ref[...] = jnp.zeros_like(acc_ref)

        part = jnp.zeros((tm, d), F32)
        for q in range(FFN_SLOTS):
            a = a_ref[q].astype(F32)
            b = b_ref[q].astype(F32)
            sg = _sigmoid(a)
            silu = a * sg
            st_ref[q] = (silu * b).T.astype(BF16)
            ds = _bdot(dyh_ref[...], wd_ref[q], NT_DIMS)
            da = (ds * b * (sg * (1.0 + a * (1.0 - sg)))).astype(BF16)
            db = (ds * silu).astype(BF16)
            da_ref[q] = da
            db_ref[q] = db
            part = part + _bdot(da, wg_ref[q], NT_DIMS) + _bdot(db, wu_ref[q], NT_DIMS)
        acc_ref[...] += part

        @pl.when((i == 0) & (j == 0))
        def _():
            dg_ref[...] = jnp.zeros_like(dg_ref)

        @pl.when(j == nf // FFN_SLOTS - 1)
        def _():
            xhat, rstd = _rms_stats(x_ref[...])
            dx, dg = _rms_bwd_vals(acc_ref[...], xhat, rstd, g_ref[...])
            dx_ref[...] = dy_ref[...] + dx
            dg_ref[...] += dg

    row = pl.BlockSpec((tm, d), lambda i, j: (i, 0))
    vec = pl.BlockSpec((1, d), lambda i, j: (0, 0))
    fblk = pl.BlockSpec((FFN_SLOTS, tm, fc), lambda i, j: (j, i, 0))
    return pl.pallas_call(
        body, grid=(t // tm, nf // FFN_SLOTS),
        in_specs=[row, vec, pl.BlockSpec((FFN_SLOTS, d, fc), lambda i, j: (j, 0, 0)),
                  pl.BlockSpec((FFN_SLOTS, d, fc), lambda i, j: (j, 0, 0)),
                  pl.BlockSpec((FFN_SLOTS, fc, d), lambda i, j: (j, 0, 0)), fblk, fblk, row],
        out_specs=[row, vec, pl.BlockSpec((d, tm), lambda i, j: (0, i)), row, fblk, fblk,
                   pl.BlockSpec((FFN_SLOTS, fc, tm), lambda i, j: (j, 0, i))],
        out_shape=[jax.ShapeDtypeStruct((t, d), F32), jax.ShapeDtypeStruct((1, d), F32),
                   jax.ShapeDtypeStruct((d, t), BF16), jax.ShapeDtypeStruct((t, d), BF16),
                   jax.ShapeDtypeStruct((nf, t, fc), BF16), jax.ShapeDtypeStruct((nf, t, fc), BF16),
                   jax.ShapeDtypeStruct((nf, fc, t), BF16)],
        scratch_shapes=[pltpu.VMEM((tm, d), F32)],
        compiler_params=_cp(("arbitrary", "arbitrary")), name=name)(x, g.reshape(1, d), wg, wu, wd, a_pre, b_pre, dy)


def _pool_core(u, grp):
    s = u.shape[0]
    w2 = u + _shift_down(u, 1)
    w4 = w2 + _shift_down(w2, 2)
    w8 = w4 + _shift_down(w4, 4)
    w16 = w8 + _shift_down(w8, 8)
    wsum = jnp.where(grp == 0, w2, jnp.where(grp == 1, w4, jnp.where(grp == 2, w8, w16)))
    win = jnp.left_shift(2, grp).astype(F32)
    t1 = (lax.broadcasted_iota(jnp.int32, (s, 1), 0) + 1).astype(F32)
    inv = 1.0 / jnp.minimum(t1, win)
    return wsum * inv - u, inv


def _pool_fwd(proj, pool_w, pool_scale, nb, s, *, name):
    def body(u_ref, w_ref, sc_ref, y_ref):
        pooled, _ = _pool_core(u_ref[...].astype(F32), pl.program_id(0))
        y_ref[...] = _bdot(pooled, w_ref[...]) * sc_ref[...]

    return pl.pallas_call(
        body, grid=(NH, nb),
        in_specs=[pl.BlockSpec((s, HD), lambda g, b: (b, CB_POOL + g)),
                  pl.BlockSpec((None, HD, HD), lambda g, b: (g, 0, 0)), pl.BlockSpec((1, HD), lambda g, b: (0, g))],
        out_specs=pl.BlockSpec((s, HD), lambda g, b: (b, g)),
        out_shape=jax.ShapeDtypeStruct((nb * s, BW), F32),
        compiler_params=_cp(("parallel", "parallel")), name=name)(proj, pool_w, pool_scale)


def _pool_bwd(proj, pool_w, pool_scale, dy, nb, s, *, name):
    def body(u_ref, w_ref, sc_ref, dy_ref, du_ref, dw_ref, dsc_ref):
        grp, b = pl.program_id(0), pl.program_id(1)
        pooled, inv = _pool_core(u_ref[...].astype(F32), grp)
        mixed = _bdot(pooled, w_ref[...])
        dy = dy_ref[...]
        dmixed = dy * sc_ref[...]
        dpooled = _bdot(dmixed, w_ref[...], NT_DIMS)
        r = dpooled * inv
        v2 = r + _shift_up(r, 1)
        v4 = v2 + _shift_up(v2, 2)
        v8 = v4 + _shift_up(v4, 4)
        v16 = v8 + _shift_up(v8, 8)
        vsum = jnp.where(grp == 0, v2, jnp.where(grp == 1, v4, jnp.where(grp == 2, v8, v16)))
        du_ref[...] = vsum - dpooled

        @pl.when(b == 0)
        def _():
            dw_ref[...] = jnp.zeros_like(dw_ref)
            dsc_ref[...] = jnp.zeros_like(dsc_ref)

        dw_ref[...] += _bdot(pooled, dmixed, TN_DIMS)
        dsc_ref[...] += jnp.sum(dy * mixed, axis=0, keepdims=True)

    return pl.pallas_call(
        body, grid=(NH, nb),
        in_specs=[pl.BlockSpec((s, HD), lambda g, b: (b, CB_POOL + g)),
                  pl.BlockSpec((None, HD, HD), lambda g, b: (g, 0, 0)), pl.BlockSpec((1, HD), lambda g, b: (0, g)),
                  pl.BlockSpec((s, HD), lambda g, b: (b, g))],
        out_specs=[pl.BlockSpec((s, HD), lambda g, b: (b, g)), pl.BlockSpec((None, HD, HD), lambda g, b: (g, 0, 0)),
                   pl.BlockSpec((1, HD), lambda g, b: (0, g))],
        out_shape=[jax.ShapeDtypeStruct((nb * s, BW), F32), jax.ShapeDtypeStruct((NH, HD, HD), F32),
                   jax.ShapeDtypeStruct((1, BW), F32)],
        compiler_params=_cp(("arbitrary", "arbitrary")), name=name)(proj, pool_w, pool_scale, dy)


SB_BLK = 128


SB_G = 4
SB_KG = SB_G * SB_BLK
SB_Q = 2 * SB_BLK


def _sb_block(qb, kg, q0, k0, diagonal):
    z = _bdot(qb, kg, NT_DIMS) * (HD ** -0.5)
    lsz = _log_sigmoid(z)
    if not diagonal:
        return lsz, lsz - z, None
    row = lax.broadcasted_iota(jnp.int32, z.shape, 0) + q0
    col = lax.broadcasted_iota(jnp.int32, z.shape, 1) + k0
    causal = col < row
    return lsz, jnp.where(causal, lsz - z, 0.0), causal


def _keep(causal, x):
    return x if causal is None else jnp.where(causal, x, 0.0)


def _sub(x, m):
    return x[:, m * SB_BLK:(m + 1) * SB_BLK]


def _sb_tails(lnm, after, ct):
    hi = lnm.astype(BF16)
    lo = (lnm - hi.astype(F32)).astype(BF16)
    tails = [None] * SB_G
    for m in reversed(range(SB_G)):
        tails[m] = (lax.dot_general(_sub(hi, m), after, NN_DIMS, preferred_element_type=F32)
                    + lax.dot_general(_sub(lo, m), after, NN_DIMS, preferred_element_type=F32)) + ct
        ct = ct + jnp.sum(_sub(lnm, m), axis=1, keepdims=True)
    ones = jnp.ones((8, lnm.shape[1]), BF16)
    rows = (lax.dot_general(ones, hi, NT_DIMS, preferred_element_type=F32)
            + lax.dot_general(ones, lo, NT_DIMS, preferred_element_type=F32))
    return jnp.concatenate(tails, axis=1), rows, ct


def _tri01(lower):
    r = lax.broadcasted_iota(jnp.int32, (SB_BLK, SB_BLK), 0)
    c = lax.broadcasted_iota(jnp.int32, (SB_BLK, SB_BLK), 1)
    return jnp.where((r < c) if lower else (r > c), 1.0, 0.0).astype(BF16)


def _split3(x):
    hi = x.astype(BF16)
    mid = (x - hi.astype(F32)).astype(BF16)
    lo = (x - hi.astype(F32) - mid.astype(F32)).astype(BF16)
    return hi, mid, lo


def _rows_to_cols(rows):
    eighth = jnp.full((8, LANE), 0.125, BF16)
    return sum(lax.dot_general(p, eighth, TN_DIMS, preferred_element_type=F32) for p in _split3(rows))


def _sb_fwd(proj, nb, s, *, name):
    nq = s // SB_Q
    ng = s // SB_KG

    def body(q_ref, k_ref, v_ref, o_ref, ctr_ref):
        after = _tri01(False)

        def qblock(i, _):
            q0 = pl.multiple_of(i * SB_Q, SB_Q)
            qb = q_ref[pl.ds(q0, SB_Q), :]

            def kgroup(g, carry, diagonal):
                acc, ct, ctr = carry
                k0 = pl.multiple_of(g * SB_KG, SB_KG)
                lsz, lnm, causal = _sb_block(qb, k_ref[pl.ds(k0, SB_KG), :], q0, k0, diagonal)
                ctr_ref[i * ng + g] = ctr
                tail, rows, ct = _sb_tails(lnm, after, ct)
                w = _keep(causal, jnp.exp(lsz + tail))
                return acc + _bdot(w, v_ref[pl.ds(k0, SB_KG), :]), ct, ctr + rows

            gd = (i * SB_Q) // SB_KG
            carry = kgroup(gd, (jnp.zeros((SB_Q, HD), F32), jnp.zeros((SB_Q, 1), F32), jnp.zeros((8, SB_Q), F32)), True)
            acc, _, _ = lax.fori_loop(0, gd, lambda jj, c: kgroup(gd - 1 - jj, c, False), carry)
            o_ref[pl.ds(q0, SB_Q), :] = acc
            return 0

        lax.fori_loop(0, nq, qblock, 0)

    def col(cb):
        return pl.BlockSpec((s, HD), lambda b, h: (b, cb + h))

    return pl.pallas_call(
        body, grid=(nb, NH), in_specs=[col(CB_SBQ), col(CB_SBK), col(CB_SBV)],
        out_specs=[pl.BlockSpec((s, HD), lambda b, h: (b, h)),
                   pl.BlockSpec((None, None, nq * ng, 8, SB_Q), lambda b, h: (b, h, 0, 0, 0))],
        out_shape=[jax.ShapeDtypeStruct((nb * s, BW), F32), jax.ShapeDtypeStruct((nb, NH, nq * ng, 8, SB_Q), F32)],
        compiler_params=_cp(("parallel", "parallel")), name=name)(proj, proj, proj)


def _sb_bwd(proj, ctr, dy, nb, s, *, name, scatter=None):
    nq = s // SB_Q
    ng = s // SB_KG
    scale = HD ** -0.5
    gx, gnl = scatter if scatter else ([], [])

    def body(q_ref, k_ref, v_ref, ctr_ref, do_ref, dq_ref, dk_ref, dv_ref):
        after = _tri01(False)
        before = _tri01(True)
        dk_ref[...] = jnp.zeros_like(dk_ref)
        dv_ref[...] = jnp.zeros_like(dv_ref)

        def qblock(i, _):
            q0 = pl.multiple_of(i * SB_Q, SB_Q)
            qb = q_ref[pl.ds(q0, SB_Q), :]
            dob = do_ref[pl.ds(q0, SB_Q), :]

            def kgroup(g, carry, diagonal):
                dq, ce = carry
                k0 = pl.multiple_of(g * SB_KG, SB_KG)
                kg = k_ref[pl.ds(k0, SB_KG), :]
                vg = v_ref[pl.ds(k0, SB_KG), :]
                lsz, lnm, causal = _sb_block(qb, kg, q0, k0, diagonal)
                tail, _, _ = _sb_tails(lnm, after, _rows_to_cols(ctr_ref[i * ng + g])[:, 0:1])
                w = _keep(causal, jnp.exp(lsz + tail))
                e = _bdot(dob, vg, NT_DIMS) * w
                pres = []
                for m in range(SB_G):
                    pres.append(_split_dot(_sub(e, m), before) + ce)
                    ce = ce + jnp.sum(_sub(e, m), axis=1, keepdims=True)
                sig = jnp.exp(lsz)
                dz = _keep(causal, e * (1.0 - sig) - jnp.concatenate(pres, axis=1) * sig) * scale
                dk_ref[pl.ds(k0, SB_KG), :] += _bdot(dz, qb, TN_DIMS)
                dv_ref[pl.ds(k0, SB_KG), :] += _bdot(w, dob, TN_DIMS)
                return dq + _bdot(dz, kg), ce

            gd = (i * SB_Q) // SB_KG
            carry = lax.fori_loop(0, gd, lambda g, c: kgroup(g, c, False),
                                  (jnp.zeros((SB_Q, HD), F32), jnp.zeros((SB_Q, 1), F32)))
            dq, _ = kgroup(gd, carry, True)
            dq_ref[pl.ds(q0, SB_Q), :] = dq
            return 0

        lax.fori_loop(0, nq, qblock, 0)

    def col(cb):
        return pl.BlockSpec((s, HD), lambda b, h: (b, cb + h))

    out = pl.BlockSpec((s, HD), lambda b, h: (b, h))
    sds = jax.ShapeDtypeStruct((nb * s, BW), F32)
    exchange = (_all_to_all_parts, len(gx), gnl) if gx else None
    res = pl.pallas_call(
        _with_exchange(body, 5, 3, 0, exchange, (nb, NH)), grid=(nb, NH),
        in_specs=[col(CB_SBQ), col(CB_SBK), col(CB_SBV),
                  pl.BlockSpec((None, None, nq * ng, 8, SB_Q), lambda b, h: (b, h, 0, 0, 0)), out]
        + [HBM_SPEC] * len(gx),
        out_specs=[out, out, out] + [HBM_SPEC] * len(gx), out_shape=[sds, sds, sds] + _all_to_all_shapes(gx, gnl),
        scratch_shapes=_comm_sems(len(gx), 7) if gx else [],
        compiler_params=_cp(("arbitrary", "arbitrary")), name=name)(proj, proj, proj, ctr, dy, *gx)
    return res[0], res[1], res[2], list(res[3:])


def _make_cdot(dims, dims_da, dims_db, swap_a=False, swap_b=False):
    @jax.custom_vjp
    def f(a, b):
        return _bdot(a, b, dims)

    def fwd(a, b):
        return _bdot(a, b, dims), (a, b)

    def bwd(res, g):
        a, b = res
        da = _bdot(b, g, dims_da) if swap_a else _bdot(g, b, dims_da)
        db = _bdot(g, a, dims_db) if swap_b else _bdot(a, g, dims_db)
        return da, db

    f.defvjp(fwd, bwd)
    return f


_cdot = _make_cdot(NN_DIMS, NT_DIMS, TN_DIMS)
_cdot_nt = _make_cdot(NT_DIMS, NN_DIMS, TN_DIMS, swap_b=True)
_cdot_tn = _make_cdot(TN_DIMS, NT_DIMS, NN_DIMS, swap_a=True)


DN_SUPER = 4 * DN_CHUNK


@jax.custom_vjp
def _unit_lower_inverse(lmat):
    n = lmat.shape[0]
    steps = int(math.log2(DN_CHUNK))
    eye = jnp.where(lax.broadcasted_iota(jnp.int32, (n, n), 0) == lax.broadcasted_iota(jnp.int32, (n, n), 1), 1.0, 0.0)
    inv = eye - lmat
    pw = _hdot(lmat, lmat)
    for it in range(steps - 1):
        inv = inv + _hdot(inv, pw)
        if it < steps - 2:
            pw = _hdot(pw, pw)
    return inv


def _unit_lower_inverse_fwd(lmat):
    inv = _unit_lower_inverse(lmat)
    return inv, inv


def _unit_lower_inverse_bwd(inv, g):
    return (-_hdot(_hdot(inv, g, TN_DIMS), inv, NT_DIMS),)


_unit_lower_inverse.defvjp(_unit_lower_inverse_fwd, _unit_lower_inverse_bwd)


@jax.custom_vjp
def _known_inverse(lmat, inv):
    return inv


def _known_inverse_fwd(lmat, inv):
    return inv, inv


def _known_inverse_bwd(inv, g):
    return -_hdot(_hdot(inv, g, TN_DIMS), inv, NT_DIMS), jnp.zeros_like(inv)


_known_inverse.defvjp(_known_inverse_fwd, _known_inverse_bwd)


def _dn_local(q, k, v, bb, gb, known_inv=None):
    n = q.shape[0]
    r = lax.broadcasted_iota(jnp.int32, (n, n), 0)
    cc = lax.broadcasted_iota(jnp.int32, (n, n), 1)
    shift = int(math.log2(DN_CHUNK))
    same = lax.shift_right_logical(r, shift) == lax.shift_right_logical(cc, shift)
    incl = jnp.where(same, jnp.where(r >= cc, 1.0, 0.0), 0.0)
    strict = jnp.where(same, jnp.where(r > cc, 1.0, 0.0), 0.0)
    gc = _hdot(incl, gb)
    gc_row = _hdot(jnp.full((n, HD), 1.0 / HD, F32), gc, NT_DIMS)
    diff = jnp.concatenate([gc] * (n // HD), axis=1) - gc_row
    decay = incl * jnp.exp(diff * incl)
    kb = k * bb
    lmat = _cdot_nt(kb, k) * (strict * decay)
    egc = jnp.exp(gc)
    inv = _unit_lower_inverse(lmat) if known_inv is None else _known_inverse(lmat, known_inv)
    u = _hdot(inv, v * bb)
    w = _hdot(inv, kb * egc)
    attn = _cdot_nt(q, k) * decay
    gl = _hdot(jnp.where(same, 1.0, 0.0), gb)
    return u, w, attn, q * egc, k * jnp.exp(gl - gc), jnp.exp(gl), inv


def _attn_pairs(attn):
    return jnp.concatenate([attn[:HD, :HD], attn[HD:, HD:]], axis=0)


def _attn_unpairs(a):
    z = jnp.zeros((HD, HD), F32)
    return jnp.concatenate([jnp.concatenate([a[:HD], z], axis=1), jnp.concatenate([z, a[HD:]], axis=1)], axis=0)


def _dn_step(u, w, a, qd, kd, cdrows, state, odd):
    v_new = u - _cdot(w, state)
    z = jnp.zeros_like(v_new)
    o = _cdot(qd, state) + _cdot(a, jnp.concatenate([z, v_new] if odd else [v_new, z], axis=0))
    return o, state * jnp.mean(cdrows, axis=0, keepdims=True) + _cdot_tn(kd, v_new)


def _dn_local_pass(fn, s, ins, outs):
    def step(it, _):
        sl = pl.ds(pl.multiple_of(it * DN_SUPER, DN_SUPER), DN_SUPER)
        res = fn(*[ref[sl, :] for ref in ins])
        for ref, val in zip(outs, res):
            ref[sl, :] = val
        return 0

    lax.fori_loop(0, s // DN_SUPER, step, 0)


def _lane_pick(row, idx):
    lane = lax.broadcasted_iota(jnp.int32, row.shape, 1)
    return jnp.sum(jnp.where(lane == idx, row, 0.0), axis=1, keepdims=True)


def _col_pick(x, idx):
    lane = lax.broadcasted_iota(jnp.int32, x.shape, 1)
    return jnp.sum(jnp.where(lane == idx, x, 0.0), axis=1, keepdims=True)


def _conv_silu(x, w):
    xc = (w[3:4, :] * x + w[2:3, :] * _shift_down(x, 1) + w[1:2, :] * _shift_down(x, 2)
          + w[0:1, :] * _shift_down(x, 3))
    return xc * _sigmoid(xc), xc


def _conv_silu_bwd(x, w, xc, dxs, dw_ref):
    sg = _sigmoid(xc)
    dxc = dxs * (sg * (1.0 + xc * (1.0 - sg)))
    dx = (w[3:4, :] * dxc + w[2:3, :] * _shift_up(dxc, 1) + w[1:2, :] * _shift_up(dxc, 2)
          + w[0:1, :] * _shift_up(dxc, 3))
    dw_ref[3:4, :] += jnp.sum(dxc * x, axis=0, keepdims=True)
    dw_ref[2:3, :] += jnp.sum(dxc * _shift_down(x, 1), axis=0, keepdims=True)
    dw_ref[1:2, :] += jnp.sum(dxc * _shift_down(x, 2), axis=0, keepdims=True)
    dw_ref[0:1, :] += jnp.sum(dxc * _shift_down(x, 3), axis=0, keepdims=True)
    return dx


def _dn_prep(qr_ref, kr_ref, vr_ref, ab_ref, cq_ref, ck_ref, cv_ref, par_ref, head):
    qs, qc = _conv_silu(qr_ref[...].astype(F32), cq_ref[...])
    ks, kc = _conv_silu(kr_ref[...].astype(F32), ck_ref[...])
    vs, vc = _conv_silu(vr_ref[...].astype(F32), cv_ref[...])
    rq = lax.rsqrt(jnp.sum(qs * qs, axis=1, keepdims=True) + EPS)
    rk = lax.rsqrt(jnp.sum(ks * ks, axis=1, keepdims=True) + EPS)
    ab = ab_ref[...]
    a_in = _col_pick(ab, head) + _lane_pick(par_ref[1:2, :], head)
    beta = _sigmoid(_col_pick(ab, NH + head))
    neg_ea = -jnp.exp(_lane_pick(par_ref[0:1, :], head))
    g = neg_ea * _softplus(a_in)
    return dict(q=qs * rq * (HD ** -0.5), k=ks * rk, v=vs, beta=beta, g=g, qs=qs, ks=ks, qc=qc, kc=kc, vc=vc,
                rq=rq, rk=rk, a_in=a_in, neg_ea=neg_ea)


ONE_BUF = pl.Buffered(1)
DN_BWD_VMEM = 62 * 1024 * 1024


def _dn_specs(nb, s):
    def col(cb):
        return pl.BlockSpec((s, HD), lambda h, b: (b, cb + h))

    def conv(cb):
        return pl.BlockSpec((DN_CONV_W, HD), lambda h, b: (0, cb + h))

    return col, conv


DN_CONV_W = 4


def _with_exchange(body, n_in, n_out, n_scratch, exchange, grid):
    if exchange is None:
        return body
    parts_fn, n, nls = exchange

    def wrapped(*refs):
        ins, xs = refs[:n_in], refs[n_in:n_in + n]
        outs, os = refs[n_in + n:n_in + n + n_out], refs[n_in + n + n_out:n_in + 2 * n + n_out]
        rest = refs[n_in + 2 * n + n_out:]
        scratch, sems = rest[:n_scratch], rest[n_scratch:]
        pos = [pl.program_id(k) for k in range(len(grid))]
        first = functools.reduce(jnp.logical_and, [p == 0 for p in pos])
        last = functools.reduce(jnp.logical_and, [p == g - 1 for p, g in zip(pos, grid)])
        start, finish = parts_fn(xs, os, nls, *sems)
        pl.when(first)(start)
        body(*ins, *outs, *scratch)
        pl.when(last)(finish)

    return wrapped


def _dn_fwd(proj, ab, conv_w, par, gain, nb, s, *, name, gather=None):
    nc = s // DN_CHUNK
    col, conv = _dn_specs(nb, s)
    gx, gnl = gather if gather else ([], [])

    def body(qr_ref, kr_ref, vr_ref, z_ref, ab_ref, cq_ref, ck_ref, cv_ref, par_ref, gain_ref,
             y_ref, o_ref, st_ref, u_ref, w_ref, at_ref, qd_ref, kd_ref, cd_ref, inv_ref, q_s, k_s, v_s, bb_s, gb_s):
        p = _dn_prep(qr_ref, kr_ref, vr_ref, ab_ref, cq_ref, ck_ref, cv_ref, par_ref, pl.program_id(0))
        q_s[...], k_s[...], v_s[...] = p["q"], p["k"], p["v"]
        bb_s[...] = jnp.broadcast_to(p["beta"], (s, HD))
        gb_s[...] = jnp.broadcast_to(p["g"], (s, HD))
        def local(*args):
            u, w, attn, qd, kd, cd, inv = _dn_local(*args)
            return u, w, _attn_pairs(attn), qd, kd, cd, inv

        _dn_local_pass(local, s, [q_s, k_s, v_s, bb_s, gb_s], [u_ref, w_ref, at_ref, qd_ref, kd_ref, cd_ref, inv_ref])

        def chunk_pair(pi, state):
            for odd in (0, 1):
                ci = 2 * pi + odd
                sl = pl.ds(pl.multiple_of(ci * DN_CHUNK, DN_CHUNK), DN_CHUNK)
                st_ref[ci] = state
                o, state = _dn_step(u_ref[sl, :], w_ref[sl, :], at_ref[sl, :], qd_ref[sl, :], kd_ref[sl, :],
                                    cd_ref[sl, :], state, odd)
                o_ref[sl, :] = o
            return state

        lax.fori_loop(0, nc // 2, chunk_pair, jnp.zeros((HD, HD), F32))
        o = o_ref[...]
        z = z_ref[...].astype(F32)
        on = o * lax.rsqrt(jnp.mean(o * o, axis=1, keepdims=True) + EPS) * gain_ref[...]
        y_ref[...] = on * (z * _sigmoid(z))

    out = pl.BlockSpec((s, HD), lambda h, b: (b, h))
    sds = jax.ShapeDtypeStruct((nb * s, BW), F32)
    exchange = (_gather_parts, len(gx), gnl) if gx else None
    res = pl.pallas_call(
        _with_exchange(body, 10, 10, 5, exchange, (NH, nb)), grid=(NH, nb),
        in_specs=[col(CB_DNQ), col(CB_DNK), col(CB_DNV), col(CB_DNZ), pl.BlockSpec((s, LANE), lambda h, b: (b, 0)),
                  conv(0), conv(NH), conv(2 * NH), pl.BlockSpec((8, LANE), lambda h, b: (0, 0)),
                  pl.BlockSpec((1, HD), lambda h, b: (0, 0))] + [HBM_SPEC] * len(gx),
        out_specs=[out, out, pl.BlockSpec((None, None, nc, HD, HD), lambda h, b: (b, h, 0, 0, 0))] + [out] * 6
        + [pl.BlockSpec((s, DN_SUPER), lambda h, b: (b, h))] + [HBM_SPEC] * len(gx),
        out_shape=[sds, sds, jax.ShapeDtypeStruct((nb, NH, nc, HD, HD), F32)] + [sds] * 6
        + [jax.ShapeDtypeStruct((nb * s, NH * DN_SUPER), F32)] + _gather_shapes(gx, gnl),
        scratch_shapes=[pltpu.VMEM((s, HD), F32)] * 5 + (_comm_sems(len(gx), 7) if gx else []),
        compiler_params=_cp(("arbitrary", "arbitrary")), name=name)(
            proj, proj, proj, proj, ab, conv_w, conv_w, conv_w, par, gain, *gx)
    return res[0], res[1], res[2], list(res[3:10]), list(res[10:])


def _dn_bwd(proj, ab, conv_w, par, gain, o_pre, states, local, dy, nb, s, *, name, scatter=None):
    nc = s // DN_CHUNK
    col, conv = _dn_specs(nb, s)
    gx, gnl = scatter if scatter else ([], [])

    def body(qr_ref, kr_ref, vr_ref, z_ref, ab_ref, cq_ref, ck_ref, cv_ref, par_ref, gain_ref, o_ref, st_ref, dy_ref,
             u_hbm, w_hbm, at_hbm, qd_hbm, kd_hbm, cd_hbm, inv_hbm,
             dqr_ref, dkr_ref, dvr_ref, dz_ref, dab_ref, dcq_ref, dck_ref, dcv_ref, dpar_ref, dgain_ref,
             q_s, k_s, v_s, bb_s, gb_s, do_s, u_s, w_s, qd_s, kd_s, at_s, cd_s, load_sems, inv_buf, inv_sems):
        head, b = pl.program_id(0), pl.program_id(1)
        local_refs = [u_s, w_s, at_s, qd_s, kd_s, cd_s]
        loads = [pltpu.make_async_copy(src.at[pl.ds(pl.multiple_of(b * s, s), s), pl.ds(pl.multiple_of(head * HD, HD), HD)],
                                       dst, load_sems.at[i])
                 for i, (src, dst) in enumerate(zip((u_hbm, w_hbm, at_hbm, qd_hbm, kd_hbm, cd_hbm), local_refs))]
        for cp in loads:
            cp.start()
        p = _dn_prep(qr_ref, kr_ref, vr_ref, ab_ref, cq_ref, ck_ref, cv_ref, par_ref, head)
        q_s[...], k_s[...], v_s[...] = p["q"], p["k"], p["v"]
        bb_s[...] = jnp.broadcast_to(p["beta"], (s, HD))
        gb_s[...] = jnp.broadcast_to(p["g"], (s, HD))

        @pl.when(b == 0)
        def _():
            for ref in (dcq_ref, dck_ref, dcv_ref, dpar_ref):
                ref[...] = jnp.zeros_like(ref)

        @pl.when((b == 0) & (head == 0))
        def _():
            dgain_ref[...] = jnp.zeros_like(dgain_ref)

        o, z, dy = o_ref[...], z_ref[...].astype(F32), dy_ref[...]
        rstd = lax.rsqrt(jnp.mean(o * o, axis=1, keepdims=True) + EPS)
        ohat = o * rstd
        sgz = _sigmoid(z)
        dz_ref[...] = dy * (ohat * gain_ref[...]) * (sgz * (1.0 + z * (1.0 - sgz)))
        don = dy * (z * sgz)
        dgain_ref[...] += jnp.sum(don * ohat, axis=0, keepdims=True)
        dxh = don * gain_ref[...]
        do_s[...] = rstd * (dxh - ohat * jnp.mean(dxh * ohat, axis=1, keepdims=True))

        for cp in loads:
            cp.wait()

        def chunk_pair(pr, dstate):
            for odd in (1, 0):
                ci = nc - 1 - 2 * pr - (1 - odd)
                sl = pl.ds(pl.multiple_of(ci * DN_CHUNK, DN_CHUNK), DN_CHUNK)
                _, vjp = jax.vjp(functools.partial(_dn_step, odd=odd), u_s[sl, :], w_s[sl, :], at_s[sl, :],
                                 qd_s[sl, :], kd_s[sl, :], cd_s[sl, :], st_ref[ci])
                du, dw, dat, dqd, dkd, dcd, dstate = vjp((do_s[sl, :], dstate))
                u_s[sl, :], w_s[sl, :], at_s[sl, :], qd_s[sl, :], kd_s[sl, :], cd_s[sl, :] = du, dw, dat, dqd, dkd, dcd
            return dstate

        lax.fori_loop(0, nc // 2, chunk_pair, jnp.zeros((HD, HD), F32))

        def inv_load(it, slot):
            rows = pl.ds(pl.multiple_of(b * s + it * DN_SUPER, DN_SUPER), DN_SUPER)
            cols = pl.ds(pl.multiple_of(head * DN_SUPER, DN_SUPER), DN_SUPER)
            return pltpu.make_async_copy(inv_hbm.at[rows, cols], inv_buf.at[slot], inv_sems.at[slot])

        def local_bwd(it, _):
            slot = lax.rem(it, 2)
            inv_load(it, slot).wait()

            @pl.when(it + 1 < s // DN_SUPER)
            def _():
                inv_load(it + 1, 1 - slot).start()

            sl = pl.ds(pl.multiple_of(it * DN_SUPER, DN_SUPER), DN_SUPER)
            ins = [ref[sl, :] for ref in (q_s, k_s, v_s, bb_s, gb_s)]
            du, dw, dat, dqd, dkd, dcd = [ref[sl, :] for ref in local_refs]
            _, vjp = jax.vjp(lambda *a: _dn_local(*a, known_inv=inv_buf[slot])[:6], *ins)
            dq, dk, dv, dbb, dgb = vjp((du, dw, _attn_unpairs(dat), dqd, dkd, dcd))
            q_s[sl, :], k_s[sl, :], v_s[sl, :] = dq, dk, dv
            bb_s[sl, :] = jnp.broadcast_to(jnp.sum(dbb, axis=1, keepdims=True), (DN_SUPER, HD))
            gb_s[sl, :] = jnp.broadcast_to(jnp.sum(dgb, axis=1, keepdims=True), (DN_SUPER, HD))
            return 0

        inv_load(0, 0).start()
        lax.fori_loop(0, s // DN_SUPER, local_bwd, 0)

        dq, dk, dv = q_s[...], k_s[...], v_s[...]
        qs, ks, rq, rk = p["qs"], p["ks"], p["rq"], p["rk"]
        dqs = (HD ** -0.5) * (rq * dq - qs * (rq * rq * rq) * jnp.sum(dq * qs, axis=1, keepdims=True))
        dks = rk * dk - ks * (rk * rk * rk) * jnp.sum(dk * ks, axis=1, keepdims=True)
        dqr_ref[...] = _conv_silu_bwd(qr_ref[...].astype(F32), cq_ref[...], p["qc"], dqs, dcq_ref)
        dkr_ref[...] = _conv_silu_bwd(kr_ref[...].astype(F32), ck_ref[...], p["kc"], dks, dck_ref)
        dvr_ref[...] = _conv_silu_bwd(vr_ref[...].astype(F32), cv_ref[...], p["vc"], dv, dcv_ref)

        dbeta, dg = bb_s[:, 0:1], gb_s[:, 0:1]
        beta = p["beta"]
        db_logit = dbeta * beta * (1.0 - beta)
        da = dg * p["neg_ea"] * _sigmoid(p["a_in"])
        lane = lax.broadcasted_iota(jnp.int32, (s, LANE), 1)
        dab_ref[...] = jnp.where(lane == head, da, 0.0) + jnp.where(lane == NH + head, db_logit, 0.0)
        dpar_ref[0:1, :] += jnp.broadcast_to(jnp.sum(dg * p["g"], axis=0, keepdims=True), (1, LANE))
        dpar_ref[1:2, :] += jnp.broadcast_to(jnp.sum(da, axis=0, keepdims=True), (1, LANE))

    out = pl.BlockSpec((s, HD), lambda h, b: (b, h))
    in_blk = pl.BlockSpec((s, HD), lambda h, b: (b, h), pipeline_mode=ONE_BUF)
    cblk = pl.BlockSpec((DN_CONV_W, HD), lambda h, b: (0, h))
    sds = jax.ShapeDtypeStruct((nb * s, BW), F32)
    csds = jax.ShapeDtypeStruct((DN_CONV_W, BW), F32)
    exchange = (_all_to_all_parts, len(gx), gnl) if gx else None
    res = pl.pallas_call(
        _with_exchange(body, 20, 10, 15, exchange, (NH, nb)), grid=(NH, nb),
        in_specs=[col(CB_DNQ), col(CB_DNK), col(CB_DNV), col(CB_DNZ),
                  pl.BlockSpec((s, LANE), lambda h, b: (b, 0), pipeline_mode=ONE_BUF),
                  conv(0), conv(NH), conv(2 * NH), pl.BlockSpec((8, LANE), lambda h, b: (0, 0)),
                  pl.BlockSpec((1, HD), lambda h, b: (0, 0)), in_blk,
                  pl.BlockSpec((None, None, nc, HD, HD), lambda h, b: (b, h, 0, 0, 0), pipeline_mode=ONE_BUF), in_blk]
        + [HBM_SPEC] * (7 + len(gx)),
        out_specs=[out, out, out, out, pl.BlockSpec((None, s, LANE), lambda h, b: (h, b, 0)), cblk, cblk, cblk,
                   pl.BlockSpec((None, 8, LANE), lambda h, b: (h, 0, 0)), pl.BlockSpec((1, HD), lambda h, b: (0, 0))]
        + [HBM_SPEC] * len(gx),
        out_shape=[sds, sds, sds, sds, jax.ShapeDtypeStruct((NH, nb * s, LANE), F32), csds, csds, csds,
                   jax.ShapeDtypeStruct((NH, 8, LANE), F32), jax.ShapeDtypeStruct((1, HD), F32)]
        + _all_to_all_shapes(gx, gnl),
        scratch_shapes=[pltpu.VMEM((s, HD), F32)] * 12 + [pltpu.SemaphoreType.DMA((6,)),
                                                           pltpu.VMEM((2, DN_SUPER, DN_SUPER), F32),
                                                           pltpu.SemaphoreType.DMA((2,))]
        + (_comm_sems(len(gx), 7) if gx else []),
        compiler_params=_cp(("arbitrary", "arbitrary"), DN_BWD_VMEM), name=name)(
            proj, proj, proj, proj, ab, conv_w, conv_w, conv_w, par, gain, o_pre, states, dy, *local, *gx)
    return tuple(res[:10]) + (list(res[10:]),)


def _sum_heads(x, *, name):
    nh, t, c = x.shape
    tm = _pick(t, (1024, 512, 256, 128))

    def body(x_ref, o_ref):
        o_ref[...] = (x_ref[0] + x_ref[1] + x_ref[2] + x_ref[3]).astype(BF16)

    return pl.pallas_call(
        body, grid=(t // tm,), in_specs=[pl.BlockSpec((nh, tm, c), lambda i: (0, i, 0))],
        out_specs=pl.BlockSpec((tm, c), lambda i: (i, 0)), out_shape=jax.ShapeDtypeStruct((t, c), BF16),
        compiler_params=_cp(("parallel",)), name=name)(x)


MERGE_TM = 256


def _merge_fwd(x, proj, yp, yd, ys, b_gate, wb, wo, *, name):
    t, d = x.shape
    tm = _pick(t, (MERGE_TM, 128))

    def body(x_ref, g0_ref, g1_ref, g2_ref, yp_ref, yd_ref, ys_ref, bg_ref, wb_ref, wo_ref, o_ref):
        merged = jnp.zeros((tm, d), F32)
        for n, (g_ref, y_ref) in enumerate(((g0_ref, yp_ref), (g1_ref, yd_ref), (g2_ref, ys_ref))):
            gate = _sigmoid(g_ref[...].astype(F32) + bg_ref[:, n * d:(n + 1) * d])
            merged = merged + gate * _bdot(y_ref[...], wb_ref[n])
        o_ref[...] = x_ref[...] + _bdot(merged, wo_ref[...])

    row = pl.BlockSpec((tm, d), lambda i: (i, 0))
    yblk = pl.BlockSpec((tm, BW), lambda i: (i, 0))

    def gl(n):
        return pl.BlockSpec((tm, d), lambda i: (i, CB_GATE + n))

    return pl.pallas_call(
        body, grid=(t // tm,),
        in_specs=[row, gl(0), gl(1), gl(2), yblk, yblk, yblk, pl.BlockSpec((1, 3 * d), lambda i: (0, 0)),
                  pl.BlockSpec((3, BW, d), lambda i: (0, 0, 0)), pl.BlockSpec((d, d), lambda i: (0, 0))],
        out_specs=row, out_shape=jax.ShapeDtypeStruct((t, d), F32),
        compiler_params=_cp(("parallel",)), name=name)(x, proj, proj, proj, yp, yd, ys, b_gate, wb, wo)


def _merge_bwd(proj, yp, yd, ys, b_gate, wb, wo, dx, *, name):
    t, d = dx.shape
    tm = _pick(t, (MERGE_TM, 128))

    def body(g0_ref, g1_ref, g2_ref, yp_ref, yd_ref, ys_ref, bg_ref, wb_ref, wo_ref, dx_ref,
             dyp_ref, dyd_ref, dys_ref, dgl_ref, mg_ref, dxh_ref, dbd_ref, dbg_ref):
        dxh = dx_ref[...].astype(BF16)
        dxh_ref[...] = dxh
        dmerged = _bdot(dxh, wo_ref[...], NT_DIMS)
        merged = jnp.zeros((tm, d), F32)

        @pl.when(pl.program_id(0) == 0)
        def _():
            dbg_ref[...] = jnp.zeros_like(dbg_ref)

        for n, (g_ref, y_ref, dy_ref) in enumerate(((g0_ref, yp_ref, dyp_ref), (g1_ref, yd_ref, dyd_ref),
                                                    (g2_ref, ys_ref, dys_ref))):
            gate = _sigmoid(g_ref[...].astype(F32) + bg_ref[:, n * d:(n + 1) * d])
            bd = _bdot(y_ref[...], wb_ref[n])
            merged = merged + gate * bd
            dgl = dmerged * bd * gate * (1.0 - gate)
            dgl_ref[:, n * d:(n + 1) * d] = dgl.astype(BF16)
            dbg_ref[:, n * d:(n + 1) * d] += jnp.sum(dgl, axis=0, keepdims=True)
            dbd = (dmerged * gate).astype(BF16)
            dbd_ref[n] = dbd
            dy_ref[...] = _bdot(dbd, wb_ref[n], NT_DIMS)
        mg_ref[...] = merged.astype(BF16)

    row = pl.BlockSpec((tm, d), lambda i: (i, 0))
    yblk = pl.BlockSpec((tm, BW), lambda i: (i, 0))
    bgv = pl.BlockSpec((1, 3 * d), lambda i: (0, 0))

    def gl(n):
        return pl.BlockSpec((tm, d), lambda i: (i, CB_GATE + n))

    ysds = jax.ShapeDtypeStruct((t, BW), F32)
    return pl.pallas_call(
        body, grid=(t // tm,),
        in_specs=[gl(0), gl(1), gl(2), yblk, yblk, yblk, bgv,
                  pl.BlockSpec((3, BW, d), lambda i: (0, 0, 0)), pl.BlockSpec((d, d), lambda i: (0, 0)), row],
        out_specs=[yblk, yblk, yblk, pl.BlockSpec((tm, 3 * d), lambda i: (i, 0)), row, row,
                   pl.BlockSpec((3, tm, d), lambda i: (0, i, 0)), bgv],
        out_shape=[ysds, ysds, ysds, jax.ShapeDtypeStruct((t, 3 * d), BF16), jax.ShapeDtypeStruct((t, d), BF16),
                   jax.ShapeDtypeStruct((t, d), BF16), jax.ShapeDtypeStruct((3, t, d), BF16),
                   jax.ShapeDtypeStruct((1, 3 * d), F32)],
        compiler_params=_cp(("arbitrary",)), name=name)(proj, proj, proj, yp, yd, ys, b_gate, wb, wo, dx)


def _loss_head(x, g, target, *, name):
    t, d = x.shape
    tm = _pick(t, (512, 256, 128))

    def body(x_ref, g_ref, t_ref, dx_ref, dg_ref, loss_ref):
        xhat, rstd = _rms_stats(x_ref[...])
        err = xhat * g_ref[...] - t_ref[...]
        dx, dg = _rms_bwd_vals(err * (1.0 / d), xhat, rstd, g_ref[...])
        dx_ref[...] = dx

        @pl.when(pl.program_id(0) == 0)
        def _():
            dg_ref[...] = jnp.zeros_like(dg_ref)
            loss_ref[...] = jnp.zeros_like(loss_ref)

        dg_ref[...] += dg
        part = jnp.sum(jnp.sum(err * err, axis=1, keepdims=True), axis=0, keepdims=True) * (0.5 / d)
        loss_ref[...] += jnp.broadcast_to(part, (1, LANE))

    row = pl.BlockSpec((tm, d), lambda i: (i, 0))
    vec = pl.BlockSpec((1, d), lambda i: (0, 0))
    return pl.pallas_call(
        body, grid=(t // tm,), in_specs=[row, vec, row],
        out_specs=[row, vec, pl.BlockSpec((1, LANE), lambda i: (0, 0))],
        out_shape=[jax.ShapeDtypeStruct((t, d), F32), jax.ShapeDtypeStruct((1, d), F32),
                   jax.ShapeDtypeStruct((1, LANE), F32)],
        compiler_params=_cp(("arbitrary",)), name=name)(x, g.reshape(1, d), target)


def _adamw(w, g, m, v, *, name):
    rows, cols = w.shape
    fits = [c for c in (1024, 704, 512, 352, 256, 128, 64, 32, 16, 8) if c * cols * 4 * 14 <= VMEM_LIMIT // 2]
    tr = _pick(rows, fits)
    c1 = 1.0 / (1.0 - ADAM_B1 ** ADAM_STEP)
    c2 = 1.0 / (1.0 - ADAM_B2 ** ADAM_STEP)

    def body(w_ref, g_ref, m_ref, v_ref, d_ref, nm_ref, nv_ref):
        g = g_ref[...]
        nm = ADAM_B1 * m_ref[...] + (1.0 - ADAM_B1) * g
        nv = ADAM_B2 * v_ref[...] + (1.0 - ADAM_B2) * (g * g)
        nm_ref[...] = nm
        nv_ref[...] = nv
        d_ref[...] = -ADAM_LR * ((nm * c1) / (jnp.sqrt(nv * c2) + ADAM_EPS) + ADAM_WD * w_ref[...])

    blk = pl.BlockSpec((tr, cols), lambda i: (i, 0))
    sds = jax.ShapeDtypeStruct((rows, cols), F32)
    return pl.pallas_call(
        body, grid=(rows // tr,), in_specs=[blk] * 4, out_specs=[blk] * 3, out_shape=[sds] * 3,
        compiler_params=_cp(("parallel",)), name=name)(w, g, m, v)


MESH_ID = pl.DeviceIdType.MESH
HBM_SPEC = pl.BlockSpec(memory_space=pl.ANY)
OTHER_CHIPS = ((1, 0), (0, 1), (1, 1))


def _at_slot(ref, nl, slot):
    return ref.at[(slice(None),) * nl + (slot,)]


def _slotted(shape, nl, slots):
    return tuple(shape[:nl]) + (slots,) + tuple(shape[nl:])


def _flip(v, f):
    return 1 - v if f else v


def _comm_call(body, n, out_shapes, n_remote, args, name):
    return pl.pallas_call(
        body, out_shape=out_shapes, in_specs=[HBM_SPEC] * len(args), out_specs=[HBM_SPEC] * len(out_shapes),
        scratch_shapes=[pltpu.SemaphoreType.DMA((n * n_remote,)), pltpu.SemaphoreType.DMA((n * n_remote,)),
                        pltpu.SemaphoreType.DMA((n * 4,))],
        compiler_params=pltpu.CompilerParams(has_side_effects=True), name=name)(*args)


def _gather(xs, nls, *, name):
    n = len(xs)

    def body(*refs):
        start, finish = _gather_parts(refs[:n], refs[n:2 * n], nls, *refs[2 * n:])
        start()
        finish()

    return _comm_call(body, n, _gather_shapes(xs, nls), 7, xs, name)


def _gather_shapes(xs, nls):
    return [jax.ShapeDtypeStruct(_slotted(v.shape, nl, N_DEV), v.dtype) for v, nl in zip(xs, nls)]


def _comm_sems(n, n_remote):
    return [pltpu.SemaphoreType.DMA((n * n_remote,)), pltpu.SemaphoreType.DMA((n * n_remote,)),
            pltpu.SemaphoreType.DMA((n * 4,))]


def _gather_parts(x_refs, o_refs, nls, send_sems, recv_sems, local_sems):
    n = len(x_refs)
    x, y, c = lax.axis_index("x"), lax.axis_index("y"), lax.axis_index("c")
    me, sibling = (x, y, c), (x, y, 1 - c)
    chips = [(_flip(x, fx), _flip(y, fy)) for fx, fy in OTHER_CHIPS]

    def copy(a, k, block, to, src=None):
        dst = _at_slot(o_refs[a], nls[a], 4 * block[0] + 2 * block[1] + block[2])
        return pltpu.make_async_remote_copy(
            src_ref=dst if src is None else src, dst_ref=dst, send_sem=send_sems.at[a * 7 + k],
            recv_sem=recv_sems.at[a * 7 + k], device_id=to, device_id_type=MESH_ID)

    def mine(a):
        return pltpu.make_async_copy(x_refs[a], _at_slot(o_refs[a], nls[a], 4 * x + 2 * y + c), local_sems.at[a])

    def first(a):
        return ([copy(a, 0, me, sibling, src=x_refs[a])]
                + [copy(a, 1 + j, me, (*chip, c), src=x_refs[a]) for j, chip in enumerate(chips)])

    def start():
        for a in range(n):
            mine(a).start()
            for cp in first(a):
                cp.start()

    def finish():
        passed = []
        for j, chip in enumerate(chips):
            for a in range(n):
                copy(a, 1 + j, (*chip, c), me).wait_recv()
                passed.append(copy(a, 4 + j, (*chip, c), sibling))
                passed[-1].start()
        for a in range(n):
            copy(a, 0, sibling, me).wait_recv()
            for j, chip in enumerate(chips):
                copy(a, 4 + j, (*chip, 1 - c), me).wait_recv()
        for a in range(n):
            for cp in first(a):
                cp.wait_send()
        for cp in passed:
            cp.wait_send()
        for a in range(n):
            mine(a).wait()

    return start, finish


ALL_FLIPS = ((0, 0, 1), (0, 1, 0), (0, 1, 1), (1, 0, 0), (1, 0, 1), (1, 1, 0), (1, 1, 1))


def _all_to_all_parts(g_refs, r_refs, nls, send_sems, recv_sems, local_sems):
    del local_sems
    n = len(g_refs)
    x, y, c = lax.axis_index("x"), lax.axis_index("y"), lax.axis_index("c")

    def copies():
        out = []
        for a in range(n):
            for k, (fx, fy, fc) in enumerate(ALL_FLIPS):
                p = (_flip(x, fx), _flip(y, fy), _flip(c, fc))
                out.append(pltpu.make_async_remote_copy(
                    src_ref=_at_slot(g_refs[a], nls[a], 4 * p[0] + 2 * p[1] + p[2]), dst_ref=_at_slot(r_refs[a], nls[a], k),
                    send_sem=send_sems.at[a * 7 + k], recv_sem=recv_sems.at[a * 7 + k], device_id=p,
                    device_id_type=MESH_ID))
        return out

    def start():
        for cp in copies():
            cp.start()

    def finish():
        cps = copies()
        for cp in cps:
            cp.wait_recv()
        for cp in cps:
            cp.wait_send()

    return start, finish


def _all_to_all_shapes(gs, nls):
    return [jax.ShapeDtypeStruct(_slotted(v.shape[:nl] + v.shape[nl + 1:], nl, 7), v.dtype) for v, nl in zip(gs, nls)]


def _scatter_pair(gs, nls, *, name):
    n = len(gs)

    def body(*refs):
        g_refs, got_refs, (send_sems, recv_sems, _) = refs[:n], refs[n:2 * n], refs[2 * n:]
        x, y, c = lax.axis_index("x"), lax.axis_index("y"), lax.axis_index("c")
        remote = []
        for a in range(n):
            for q in range(4):
                rc = pltpu.make_async_remote_copy(
                    src_ref=_at_slot(g_refs[a], nls[a], 2 * q + 1 - c), dst_ref=_at_slot(got_refs[a], nls[a], q),
                    send_sem=send_sems.at[a * 4 + q], recv_sem=recv_sems.at[a * 4 + q], device_id=(x, y, 1 - c),
                    device_id_type=MESH_ID)
                rc.start()
                remote.append(rc)
        for rc in remote:
            rc.wait_recv()
        for rc in remote:
            rc.wait_send()

    outs = [jax.ShapeDtypeStruct(_slotted(v.shape[:nl] + v.shape[nl + 1:], nl, 4), v.dtype) for v, nl in zip(gs, nls)]
    return _comm_call(body, n, outs, 4, gs, name)


def _scatter_chips(ps, nls, *, name):
    n = len(ps)

    def body(*refs):
        p_refs, r_refs, (send_sems, recv_sems, _) = refs[:n], refs[n:2 * n], refs[2 * n:]
        x, y, c = lax.axis_index("x"), lax.axis_index("y"), lax.axis_index("c")
        remote = []
        for a in range(n):
            for k, (fx, fy) in enumerate(OTHER_CHIPS):
                tx, ty = _flip(x, fx), _flip(y, fy)
                rc = pltpu.make_async_remote_copy(
                    src_ref=_at_slot(p_refs[a], nls[a], 2 * tx + ty), dst_ref=_at_slot(r_refs[a], nls[a], k),
                    send_sem=send_sems.at[a * 3 + k], recv_sem=recv_sems.at[a * 3 + k], device_id=(tx, ty, c),
                    device_id_type=MESH_ID)
                rc.start()
                remote.append(rc)
        for rc in remote:
            rc.wait_recv()
        for rc in remote:
            rc.wait_send()

    outs = [jax.ShapeDtypeStruct(_slotted(v.shape[:nl] + v.shape[nl + 1:], nl, 3), v.dtype) for v, nl in zip(ps, nls)]
    return _comm_call(body, n, outs, 3, ps, name)


def _pair_add(g, got, core, *, name):
    rows, cols = g.shape[-2:]
    lf = math.prod(got.shape[:-3])
    tr = _pick(rows, (1024, 512, 352, 256, 128))

    def body(core_ref, g_ref, got_ref, o_ref):
        o_ref[...] = (g_ref[...].astype(F32) + got_ref[...].astype(F32)).astype(BF16)

    blk = pl.BlockSpec((None, None, tr, cols), lambda i, q, j, core_ref: (i, q, j, 0))
    out = pl.pallas_call(
        body, grid_spec=pltpu.PrefetchScalarGridSpec(
            num_scalar_prefetch=1, grid=(lf, 4, rows // tr),
            in_specs=[pl.BlockSpec((None, None, None, tr, cols), lambda i, q, j, core_ref: (i, q, core_ref[0], j, 0)),
                      blk], out_specs=blk),
        out_shape=jax.ShapeDtypeStruct((lf, 4, rows, cols), BF16),
        compiler_params=_cp(("parallel", "parallel", "parallel")), name=name)(
            core, g.reshape(lf, 4, 2, rows, cols), got.reshape(lf, 4, rows, cols))
    return out.reshape(got.shape)


def _sum_adamw(p, r, own, w, m, v, layer, prev, *, name):
    shape = w.shape[1:]
    rows, cols = shape[-2:]
    lf = math.prod(shape[:-2])
    np_, nk = p.shape[-3], r.shape[-3]
    fits = [c for c in (1024, 512, 352, 256, 128, 64, 32, 16) if c * cols * (7 * 4 + (nk + 1) * 2) * 2 <= VMEM_LIMIT // 2]
    tr = _pick(rows, fits)
    c1 = 1.0 / (1.0 - ADAM_B1 ** ADAM_STEP)
    c2 = 1.0 / (1.0 - ADAM_B2 ** ADAM_STEP)

    def body(own_ref, p_ref, r_ref, w_ref, m_ref, v_ref, *rest):
        g_ref, d_ref, nm_ref, nv_ref = rest[-4:]
        g = p_ref[...].astype(F32)
        for k in range(nk):
            g = g + r_ref[k].astype(F32)
        g_ref[...] = g
        nm = ADAM_B1 * m_ref[...] + (1.0 - ADAM_B1) * g
        nv = ADAM_B2 * v_ref[...] + (1.0 - ADAM_B2) * (g * g)
        nm_ref[...] = nm
        nv_ref[...] = nv
        d_ref[...] = -ADAM_LR * ((nm * c1) / (jnp.sqrt(nv * c2) + ADAM_EPS) + ADAM_WD * w_ref[...])

    wblk = pl.BlockSpec((None, None, tr, cols), lambda i, j, own_ref: (layer, i, j, 0))
    full = (w.shape[0], lf, rows, cols)
    sds = jax.ShapeDtypeStruct(full, F32)
    prev = [] if prev is None else [a.reshape(full) for a in prev]
    outs = pl.pallas_call(
        body, grid_spec=pltpu.PrefetchScalarGridSpec(
            num_scalar_prefetch=1, grid=(lf, rows // tr),
            in_specs=[pl.BlockSpec((None, None, tr, cols), lambda i, j, own_ref: (i, own_ref[0], j, 0)),
                      pl.BlockSpec((None, nk, tr, cols), lambda i, j, own_ref: (i, 0, j, 0))] + [wblk] * 3
            + [HBM_SPEC] * len(prev),
            out_specs=[wblk] * 4),
        out_shape=[sds] * 4, input_output_aliases={6 + i: i for i in range(len(prev))},
        compiler_params=_cp(("parallel", "parallel")), name=name)(
            own, p.reshape(lf, np_, rows, cols), r.reshape(lf, nk, rows, cols), w.reshape(full), m.reshape(full),
            v.reshape(full), *prev)
    return [o.reshape(w.shape) for o in outs]


def _sum_slots(x, *, name):
    nd, rows, cols = x.shape
    tr = _pick(rows, (512, 256, 128, 64, 32, 16, 8))

    def body(x_ref, o_ref):
        acc = x_ref[0].astype(F32)
        for j in range(1, nd):
            acc = acc + x_ref[j].astype(F32)
        o_ref[...] = acc

    return pl.pallas_call(
        body, grid=(rows // tr,), in_specs=[pl.BlockSpec((nd, tr, cols), lambda i: (0, i, 0))],
        out_specs=pl.BlockSpec((tr, cols), lambda i: (i, 0)), out_shape=jax.ShapeDtypeStruct((rows, cols), F32),
        compiler_params=_cp(("parallel",)), name=name)(x)


def _pad_rows(a, mult=8):
    r = (-a.shape[0]) % mult
    return jnp.pad(a, ((0, r), (0, 0))) if r else a


def _flat128(a):
    f = a.reshape(-1)
    return jnp.pad(f, (0, (-f.shape[0]) % LANE)).reshape(-1, LANE)


def _unshard(gathered, shape, axis):
    g = gathered.reshape((N_DEV,) + tuple(shape))
    g = jnp.moveaxis(g, 0, axis)
    full = list(shape)
    full[axis] *= N_DEV
    return g.reshape(full)


def _col_shards(full):
    rows, cols = full.shape
    return jnp.moveaxis(full.reshape(rows, N_DEV, cols // N_DEV), 1, 0)


BIG = (("ffn_w_gate", 2), ("ffn_w_up", 2), ("ffn_w_down", 2), ("w_in", 1), ("w_branch", 2), ("w_out", 1))


def kernel(x, ffn_norm, ffn_w_gate, ffn_w_up, ffn_w_down, mix_norm, w_in, b_gate, pool_w, pool_scale, dn_conv, dn_A_log, dn_dt_bias, dn_out_norm, w_branch, w_out, final_norm, loss_target, m_ffn_norm, m_ffn_w_gate, m_ffn_w_up, m_ffn_w_down, m_mix_norm, m_w_in, m_b_gate, m_pool_w, m_pool_scale, m_dn_conv, m_dn_A_log, m_dn_dt_bias, m_dn_out_norm, m_w_branch, m_w_out, m_final_norm, v_ffn_norm, v_ffn_w_gate, v_ffn_w_up, v_ffn_w_down, v_mix_norm, v_w_in, v_b_gate, v_pool_w, v_pool_scale, v_dn_conv, v_dn_A_log, v_dn_dt_bias, v_dn_out_norm, v_w_branch, v_w_out, v_final_norm):
    wts = dict(ffn_norm=ffn_norm, ffn_w_gate=ffn_w_gate, ffn_w_up=ffn_w_up, ffn_w_down=ffn_w_down, mix_norm=mix_norm,
               w_in=w_in, b_gate=b_gate, pool_w=pool_w, pool_scale=pool_scale, dn_conv=dn_conv, dn_A_log=dn_A_log,
               dn_dt_bias=dn_dt_bias, dn_out_norm=dn_out_norm, w_branch=w_branch, w_out=w_out, final_norm=final_norm)
    mom = dict(ffn_norm=m_ffn_norm, ffn_w_gate=m_ffn_w_gate, ffn_w_up=m_ffn_w_up, ffn_w_down=m_ffn_w_down,
               mix_norm=m_mix_norm, w_in=m_w_in, b_gate=m_b_gate, pool_w=m_pool_w, pool_scale=m_pool_scale,
               dn_conv=m_dn_conv, dn_A_log=m_dn_A_log, dn_dt_bias=m_dn_dt_bias, dn_out_norm=m_dn_out_norm,
               w_branch=m_w_branch, w_out=m_w_out, final_norm=m_final_norm)
    var = dict(ffn_norm=v_ffn_norm, ffn_w_gate=v_ffn_w_gate, ffn_w_up=v_ffn_w_up, ffn_w_down=v_ffn_w_down,
               mix_norm=v_mix_norm, w_in=v_w_in, b_gate=v_b_gate, pool_w=v_pool_w, pool_scale=v_pool_scale,
               dn_conv=v_dn_conv, dn_A_log=v_dn_A_log, dn_dt_bias=v_dn_dt_bias, dn_out_norm=v_dn_out_norm,
               w_branch=v_w_branch, w_out=v_w_out, final_norm=v_final_norm)
    nb, s, d = x.shape
    t = nb * s
    me = 4 * lax.axis_index("x") + 2 * lax.axis_index("y") + lax.axis_index("c")

    big = [n for n, _ in BIG]
    nls = [nl - 1 for _, nl in BIG]
    shards = lambda l: [wts[n][l].astype(BF16) for n in big]
    small_sh = jnp.concatenate([_flat128(ffn_norm), _flat128(dn_conv)], axis=0)
    ffn3 = big[:3]
    *pre0, small_g = _gather([wts[n][0, 0].astype(BF16) for n in ffn3] + [small_sh], [0] * 4, name="gather_weights")
    rest0 = [wts[n][0, 1].astype(BF16) for n in ffn3] + [wts[n][0].astype(BF16) for n in big[3:]]
    rest0_nls = [0] * 3 + nls[3:]
    full = [None] * DEPTH

    def mixer_weights(l):
        w_in_full = jnp.moveaxis(full[l]["w_in"], 0, 1).reshape(d, -1)
        w_main = jnp.concatenate([w_in_full[:, :AB_LO], w_in_full[:, AB_HI:]], axis=1)
        w_ab = jnp.pad(w_in_full[:, AB_LO:AB_HI], ((0, 0), (0, LANE - (AB_HI - AB_LO))))
        wb = jnp.moveaxis(full[l]["w_branch"], 1, 2).reshape(3, BW, d)
        return w_main, w_ab, wb, full[l]["w_out"].reshape(d, d)

    nfr = ffn_norm.size // LANE
    ffn_norm_full = _unshard(small_g[:, :nfr], ffn_norm.shape, 2)
    dn_conv_full = _unshard(small_g[:, nfr:], dn_conv.shape, 2)
    pool_w_h = pool_w.astype(BF16)

    xs = x.reshape(t, d)
    saved = []
    for l in range(DEPTH):
        sv = dict(x0=xs)
        if l == 0:
            xs, a0, b0, got = _ffn_fwd(xs, ffn_norm_full[0, 0], *pre0, name="ffn_fwd_gather", gather=(rest0, rest0_nls))
            full[0] = dict(zip(ffn3, zip(pre0, got[:3])), **dict(zip(big[3:], got[3:])))
        else:
            xs, a0, b0, _ = _ffn_fwd(xs, ffn_norm_full[l, 0], full[l]["ffn_w_gate"][0], full[l]["ffn_w_up"][0],
                                     full[l]["ffn_w_down"][0], name="ffn_fwd")
        sv["ab0"] = (a0, b0)
        sv["x1"] = xs
        w_main, w_ab, wb, wo = mixer_weights(l)
        h = _rms_fwd(xs, mix_norm[l], name="mix_rms")
        proj = _mm(h, w_main, out_dtype=BF16, name="proj")
        ab = _mm(h, w_ab, name="proj_ab")
        par = jnp.pad(jnp.stack([dn_A_log[l], dn_dt_bias[l]]), ((0, 6), (0, LANE - NH)))
        gain = dn_out_norm[l].reshape(1, HD)
        psc = pool_scale[l].reshape(1, BW)
        yp = _pool_fwd(proj, pool_w_h[l], psc, nb, s, name="pool_fwd")
        yd, o_pre, states, dn_local, gat = _dn_fwd(proj, ab, dn_conv_full[l], par, gain, nb, s,
                                         name="dn_fwd" if l == DEPTH - 1 else "dn_fwd_gather",
                                         gather=(shards(l + 1), nls) if l < DEPTH - 1 else None)
        if l < DEPTH - 1:
            full[l + 1] = dict(zip(big, gat))
        ys, sb_ctr = _sb_fwd(proj, nb, s, name="sb_fwd")
        bg = b_gate[l].reshape(1, 3 * d)
        xs = _merge_fwd(xs, proj, yp, yd, ys, bg, wb, wo, name="merge_fwd")
        sv.update(x2=xs, h=h, proj=proj, ab=ab, par=par, gain=gain, psc=psc, yp=yp, yd=yd, ys=ys, sb_ctr=sb_ctr, o_pre=o_pre,
                  states=states, dn_local=dn_local, bg=bg, w_main=w_main, w_ab=w_ab, wb=wb, wo=wo)
        xs, a1, b1, _ = _ffn_fwd(xs, ffn_norm_full[l, 1], full[l]["ffn_w_gate"][1], full[l]["ffn_w_up"][1],
                                 full[l]["ffn_w_down"][1], name="ffn_fwd")
        sv["ab1"] = (a1, b1)
        saved.append(sv)

    dx, g_final, loss_row = _loss_head(xs, final_norm, loss_target.reshape(t, d), name="loss_head")
    loss = lax.psum(loss_row[0, 0], ("x", "y", "c"))

    gw = {n: [None] * DEPTH for n in ("ffn_norm", "ffn_w_gate", "ffn_w_up", "ffn_w_down", "mix_norm", "w_in", "b_gate",
                                      "pool_w", "pool_scale", "dn_conv", "dn_A_log", "dn_dt_bias", "dn_out_norm",
                                      "w_branch", "w_out")}

    me_i = me.astype(jnp.int32).reshape(1)
    updated = {n: None for n in big}
    pending = None

    def finish_layer(l, own_blocks, arrived, own_slot):
        for n, p, r in zip(big, own_blocks, arrived):
            updated[n] = _sum_adamw(p, r, own_slot, wts[n], mom[n], var[n], l, updated[n], name=f"adamw_{n}_{l}")

    def ffn_back(l, i, x_in, dy):
        dxi, dg, hb, dyh, da, db, sact = _ffn_bwd(x_in, ffn_norm_full[l, i], full[l]["ffn_w_gate"][i],
                                                  full[l]["ffn_w_up"][i], full[l]["ffn_w_down"][i],
                                                  *saved[l][f"ab{i}"], dy, name="ffn_bwd")
        return dxi, dg, (_mm_slots(hb, da, name="dw_gate_up"), _mm_slots(hb, db, name="dw_gate_up"),
                         _mm_slots(sact, dyh, name="dw_down"))

    for l in reversed(range(DEPTH)):
        sv = saved[l]
        dx, dg1, (dwg1, dwu1, dwd1) = ffn_back(l, 1, sv["x2"], dx)
        dyp, dyd, dys, dgl, merged, dxh, dbd, dbg = _merge_bwd(sv["proj"], sv["yp"], sv["yd"], sv["ys"], sv["bg"],
                                                               sv["wb"], sv["wo"], dx, name="merge_bwd")
        gw["w_out"][l] = _mm(merged, dxh, ta=True, out_dtype=BF16, name="dw_out").reshape(N_DEV, d // N_DEV, d)
        gw["w_branch"][l] = jnp.stack([_col_shards(_mm(y, dbd[n], ta=True, out_dtype=BF16, name="dw_branch"))
                                       for n, y in enumerate((sv["yp"], sv["yd"], sv["ys"]))])
        gw["b_gate"][l] = dbg.reshape(3 * d)
        du, dpw, dps = _pool_bwd(sv["proj"], pool_w_h[l], sv["psc"], dyp, nb, s, name="pool_bwd")
        gw["pool_w"][l], gw["pool_scale"][l] = dpw, dps.reshape(BW)
        own = [gw[n][pending] for n in big] if pending is not None else []
        dqr, dkr, dvr, dz, dab4, dcq, dck, dcv, dpar, dgain, arrived_ffn = _dn_bwd(
            sv["proj"], sv["ab"], dn_conv_full[l], sv["par"], sv["gain"], sv["o_pre"], sv["states"], sv["dn_local"],
            dyd, nb, s, name="dn_bwd_scatter" if own else "dn_bwd", scatter=(own[:3], nls[:3]) if own else None)
        gw["dn_conv"][l] = jnp.concatenate([dcq, dck, dcv], axis=1)
        gw["dn_A_log"][l], gw["dn_dt_bias"][l], gw["dn_out_norm"][l] = dpar[:, 0, 0], dpar[:, 1, 0], dgain.reshape(HD)
        dsq, dsk, dsv, arrived_rest = _sb_bwd(sv["proj"], sv["sb_ctr"], dys, nb, s,
                                              name="sb_bwd_scatter" if own else "sb_bwd",
                                              scatter=(own[3:], nls[3:]) if own else None)
        if own:
            finish_layer(pending, own, arrived_ffn + arrived_rest, me_i)
        dab = _sum_heads(dab4, name="sum_heads")
        dproj = jnp.concatenate([du.astype(BF16), dqr.astype(BF16), dkr.astype(BF16), dvr.astype(BF16),
                                 dz.astype(BF16), dsq.astype(BF16), dsk.astype(BF16), dsv.astype(BF16), dgl], axis=1)
        dw_main = _mm(sv["h"], dproj, ta=True, out_dtype=BF16, name="dw_in")
        dw_ab = _mm(sv["h"], dab, ta=True, out_dtype=BF16, name="dw_ab")
        gw["w_in"][l] = _col_shards(jnp.concatenate([dw_main[:, :AB_LO], dw_ab[:, :AB_HI - AB_LO],
                                                     dw_main[:, AB_LO:]], axis=1))
        dh_main = _mm(dproj, sv["w_main"], tb=True, name="dh_mix")
        dh_ab = _mm(dab, sv["w_ab"], tb=True, name="dh_mix_ab")
        dx, dgm = _rms_bwd(sv["x1"], mix_norm[l], dh_main, dh_ab, dx, name="mix_rms_bwd")
        gw["mix_norm"][l] = dgm.reshape(d)
        dx, dg0, (dwg0, dwu0, dwd0) = ffn_back(l, 0, sv["x0"], dx)
        gw["ffn_norm"][l] = jnp.stack([dg0.reshape(d), dg1.reshape(d)])
        gw["ffn_w_gate"][l] = jnp.stack([dwg0, dwg1])
        gw["ffn_w_up"][l] = jnp.stack([dwu0, dwu1])
        gw["ffn_w_down"][l] = jnp.stack([dwd0, dwd1])
        pending = l
    grad_x = dx.reshape(nb, s, d)

    core = lax.axis_index("c").astype(jnp.int32).reshape(1)
    chip = (2 * lax.axis_index("x") + lax.axis_index("y")).astype(jnp.int32).reshape(1)
    last = [gw[n][0] for n in big]
    got = _scatter_pair(last, nls, name="scatter_grads_pair")
    chip_sums = [_pair_add(g, b, core, name="add_pair_" + n) for n, g, b in zip(big, last, got)]
    finish_layer(0, chip_sums, _scatter_chips(chip_sums, nls, name="scatter_grads_chips"), chip)
    grads, delta, new_m, new_v = ({n: updated[n][i] for n in big} for i in range(4))
    gw = {n: jnp.stack(v) for n, v in gw.items() if n not in big}
    gw["final_norm"] = g_final.reshape(d)

    small = ("ffn_norm", "mix_norm", "b_gate", "pool_w", "pool_scale", "dn_conv", "dn_A_log", "dn_dt_bias",
             "dn_out_norm", "final_norm")
    sp = _pad_rows(jnp.concatenate([_flat128(gw[n]) for n in small], axis=0))
    ssum = _sum_slots(_gather([sp], [0], name="gather_small_grads")[0], name="sum_small_grads")
    off = 0
    for n in small:
        r = -(-gw[n].size // LANE)
        g = ssum[off:off + r].reshape(-1)[:gw[n].size].reshape(gw[n].shape)
        off += r
        if n in ("ffn_norm", "dn_conv"):
            w = wts[n].shape[2]
            g = lax.dynamic_slice_in_dim(g, me * w, w, axis=2)
        grads[n] = g

    pk = lambda src: _pad_rows(jnp.concatenate([_flat128(src[n]) for n in small], axis=0))
    dl, nm, nv = _adamw(pk(wts), pk(grads), pk(mom), pk(var), name="adamw_small")
    off = 0
    for n in small:
        r = -(-wts[n].size // LANE)
        for dst, src in ((delta, dl), (new_m, nm), (new_v, nv)):
            dst[n] = src[off:off + r].reshape(-1)[:wts[n].size].reshape(wts[n].shape)
        off += r

    order = ("ffn_norm", "ffn_w_gate", "ffn_w_up", "ffn_w_down", "mix_norm", "w_in", "b_gate", "pool_w", "pool_scale",
             "dn_conv", "dn_A_log", "dn_dt_bias", "dn_out_norm", "w_branch", "w_out", "final_norm")
    return (loss, grad_x, *[grads[n] for n in order], *[delta[n] for n in order], *[new_m[n] for n in order],
            *[new_v[n] for n in order])
```

```python
import functools
import math

import jax
import jax.numpy as jnp
from jax import lax
from jax.experimental import pallas as pl
from jax.experimental.pallas import tpu as pltpu

F32, BF16 = jnp.float32, jnp.bfloat16
D_MODEL, D_FF, DEPTH = 1024, 2816, 4
BW = 512
HD = 128
NH = 4
DN_CHUNK = 64
EPS = 1e-6
N_DEV = 8
LANE = 128
CB_POOL, CB_DNQ, CB_DNK, CB_DNV, CB_DNZ, CB_SBQ, CB_SBK, CB_SBV = 0, 4, 8, 12, 16, 20, 24, 28
CB_GATE = 4
P_MAIN = 7168
AB_LO, AB_HI = 2560, 2568
ADAM_LR, ADAM_B1, ADAM_B2, ADAM_EPS, ADAM_WD, ADAM_STEP = 0.001, 0.9, 0.999, 1e-08, 0.01, 10
VMEM_LIMIT = 56 * 1024 * 1024
HIGHEST = lax.Precision.HIGHEST
NT_DIMS = (((1,), (1,)), ((), ()))
TN_DIMS = (((0,), (0,)), ((), ()))
NN_DIMS = (((1,), (0,)), ((), ()))


def _cp(dims=None, vmem=VMEM_LIMIT):
    return pltpu.CompilerParams(dimension_semantics=dims, vmem_limit_bytes=vmem)


def _pick(n, cands):
    for c in cands:
        if n % c == 0:
            return c
    return n


def _bdot(a, b, dims=NN_DIMS):
    return lax.dot_general(a.astype(BF16), b.astype(BF16), dims, preferred_element_type=F32)


def _hdot(a, b, dims=NN_DIMS):
    return lax.dot_general(a, b, dims, precision=lax.Precision.HIGH, preferred_element_type=F32)


def _split_dot(x, m01):
    hi = x.astype(BF16)
    lo = (x - hi.astype(F32)).astype(BF16)
    return (lax.dot_general(hi, m01, NN_DIMS, preferred_element_type=F32)
            + lax.dot_general(lo, m01, NN_DIMS, preferred_element_type=F32))


def _sigmoid(x):
    return 1.0 / (1.0 + jnp.exp(-x))


def _log_sigmoid(x):
    return jnp.minimum(x, 0.0) - jnp.log1p(jnp.exp(-jnp.abs(x)))


def _softplus(x):
    return jnp.maximum(x, 0.0) + jnp.log1p(jnp.exp(-jnp.abs(x)))


def _shift_down(x, k):
    r = lax.broadcasted_iota(jnp.int32, x.shape, 0)
    return jnp.where(r >= k, pltpu.roll(x, k, 0), 0.0)


def _shift_up(x, k):
    n = x.shape[0]
    r = lax.broadcasted_iota(jnp.int32, x.shape, 0)
    return jnp.where(r < n - k, pltpu.roll(x, n - k, 0), 0.0)


def _mm(a, b, *, ta=False, tb=False, out_dtype=F32, name):
    (kk, m) = a.shape if ta else a.shape[::-1]
    (k2, n) = b.shape[::-1] if tb else b.shape
    assert kk == k2, (a.shape, b.shape, ta, tb)
    bm = _pick(m, (1024, 512, 256, 128))
    bn = _pick(n, (1024, 1408, 512, 256, 128))
    bk = _pick(kk, (1024, 512, 256, 128))
    nk = kk // bk
    dims = (((0 if ta else 1,), (1 if tb else 0,)), ((), ()))

    def body(a_ref, b_ref, o_ref, acc_ref):
        k = pl.program_id(2)

        @pl.when(k == 0)
        def _():
            acc_ref[...] = jnp.zeros_like(acc_ref)

        acc_ref[...] += lax.dot_general(a_ref[...].astype(BF16), b_ref[...].astype(BF16), dims,
                                        preferred_element_type=F32)

        @pl.when(k == nk - 1)
        def _():
            o_ref[...] = acc_ref[...].astype(out_dtype)

    a_spec = (pl.BlockSpec((bk, bm), lambda i, j, k: (k, i)) if ta else pl.BlockSpec((bm, bk), lambda i, j, k: (i, k)))
    b_spec = (pl.BlockSpec((bn, bk), lambda i, j, k: (j, k)) if tb else pl.BlockSpec((bk, bn), lambda i, j, k: (k, j)))
    return pl.pallas_call(
        body, grid=(m // bm, n // bn, nk), in_specs=[a_spec, b_spec],
        out_specs=pl.BlockSpec((bm, bn), lambda i, j, k: (i, j)),
        out_shape=jax.ShapeDtypeStruct((m, n), out_dtype),
        scratch_shapes=[pltpu.VMEM((bm, bn), F32)],
        compiler_params=_cp(("parallel", "parallel", "arbitrary")), name=name)(a, b)


def _mm_slots(a, b, *, name):
    a3, b3 = a.ndim == 3, b.ndim == 3
    ns = a.shape[0] if a3 else b.shape[0]
    m, t = a.shape[-2:]
    n = b.shape[-1]
    bk = _pick(t, (1024, 512, 256, 128))
    nk = t // bk

    def body(a_ref, b_ref, o_ref, acc_ref):
        k = pl.program_id(0)

        @pl.when(k == 0)
        def _():
            acc_ref[...] = jnp.zeros_like(acc_ref)

        for s in range(ns):
            acc_ref[s] += _bdot(a_ref[s] if a3 else a_ref[...], b_ref[s] if b3 else b_ref[...])

        @pl.when(k == nk - 1)
        def _():
            o_ref[...] = acc_ref[...].astype(BF16)

    a_spec = pl.BlockSpec((ns, m, bk), lambda k: (0, 0, k)) if a3 else pl.BlockSpec((m, bk), lambda k: (0, k))
    b_spec = pl.BlockSpec((ns, bk, n), lambda k: (0, k, 0)) if b3 else pl.BlockSpec((bk, n), lambda k: (k, 0))
    return pl.pallas_call(
        body, grid=(nk,), in_specs=[a_spec, b_spec], out_specs=pl.BlockSpec((ns, m, n), lambda k: (0, 0, 0)),
        out_shape=jax.ShapeDtypeStruct((ns, m, n), BF16), scratch_shapes=[pltpu.VMEM((ns, m, n), F32)],
        compiler_params=_cp(("arbitrary",)), name=name)(a, b)


def _rms_stats(x):
    rstd = lax.rsqrt(jnp.mean(x * x, axis=-1, keepdims=True) + EPS)
    return x * rstd, rstd


def _rms_bwd_vals(dh, xhat, rstd, g):
    dxh = dh * g
    dx = rstd * (dxh - xhat * jnp.mean(dxh * xhat, axis=-1, keepdims=True))
    return dx, jnp.sum(dh * xhat, axis=0, keepdims=True)


def _rms_fwd(x, g, *, name):
    t, d = x.shape
    tm = _pick(t, (512, 256, 128))

    def body(x_ref, g_ref, h_ref):
        xhat, _ = _rms_stats(x_ref[...])
        h_ref[...] = (xhat * g_ref[...]).astype(BF16)

    return pl.pallas_call(
        body, grid=(t // tm,),
        in_specs=[pl.BlockSpec((tm, d), lambda i: (i, 0)), pl.BlockSpec((1, d), lambda i: (0, 0))],
        out_specs=pl.BlockSpec((tm, d), lambda i: (i, 0)), out_shape=jax.ShapeDtypeStruct((t, d), BF16),
        compiler_params=_cp(("parallel",)), name=name)(x, g.reshape(1, d))


def _rms_bwd(x, g, dh_a, dh_b, dres, *, name):
    t, d = x.shape
    tm = _pick(t, (512, 256, 128))

    def body(x_ref, g_ref, dha_ref, dhb_ref, dres_ref, dx_ref, dg_ref):
        xhat, rstd = _rms_stats(x_ref[...])
        dx, dg = _rms_bwd_vals(dha_ref[...] + dhb_ref[...], xhat, rstd, g_ref[...])
        dx_ref[...] = dres_ref[...] + dx

        @pl.when(pl.program_id(0) == 0)
        def _():
            dg_ref[...] = jnp.zeros_like(dg_ref)

        dg_ref[...] += dg

    row = pl.BlockSpec((tm, d), lambda i: (i, 0))
    vec = pl.BlockSpec((1, d), lambda i: (0, 0))
    return pl.pallas_call(
        body, grid=(t // tm,), in_specs=[row, vec, row, row, row], out_specs=[row, vec],
        out_shape=[jax.ShapeDtypeStruct((t, d), F32), jax.ShapeDtypeStruct((1, d), F32)],
        compiler_params=_cp(("arbitrary",)), name=name)(x, g.reshape(1, d), dh_a, dh_b, dres)


FFN_TM = 512
FFN_SLOTS = 2


def _ffn_fwd(x, g, wg, wu, wd, *, name, gather=None):
    t, d = x.shape
    nf, _, fc = wg.shape
    tm = _pick(t, (FFN_TM, 256, 128))
    gx, gnl = gather if gather else ([], [])

    def body(x_ref, g_ref, wg_ref, wu_ref, wd_ref, o_ref, a_ref, b_ref, h_ref, acc_ref):
        j = pl.program_id(1)

        @pl.when(j == 0)
        def _():
            xhat, _ = _rms_stats(x_ref[...])
            h_ref[...] = (xhat * g_ref[...]).astype(BF16)
            acc_ref[...] = jnp.zeros_like(acc_ref)

        h = h_ref[...]
        part = jnp.zeros((tm, d), F32)
        for q in range(FFN_SLOTS):
            a = _bdot(h, wg_ref[q])
            b = _bdot(h, wu_ref[q])
            a_ref[q] = a.astype(BF16)
            b_ref[q] = b.astype(BF16)
            part = part + _bdot(a * _sigmoid(a) * b, wd_ref[q])
        acc_ref[...] += part

        @pl.when(j == nf // FFN_SLOTS - 1)
        def _():
            o_ref[...] = x_ref[...] + 0.5 * acc_ref[...]

    row = pl.BlockSpec((tm, d), lambda i, j: (i, 0))
    grid = (t // tm, nf // FFN_SLOTS)
    exchange = (_gather_parts, len(gx), gnl) if gx else None
    res = pl.pallas_call(
        _with_exchange(body, 5, 3, 2, exchange, grid), grid=grid,
        in_specs=[row, pl.BlockSpec((1, d), lambda i, j: (0, 0)),
                  pl.BlockSpec((FFN_SLOTS, d, fc), lambda i, j: (j, 0, 0)),
                  pl.BlockSpec((FFN_SLOTS, d, fc), lambda i, j: (j, 0, 0)),
                  pl.BlockSpec((FFN_SLOTS, fc, d), lambda i, j: (j, 0, 0))] + [HBM_SPEC] * len(gx),
        out_specs=[row, pl.BlockSpec((FFN_SLOTS, tm, fc), lambda i, j: (j, i, 0)),
                   pl.BlockSpec((FFN_SLOTS, tm, fc), lambda i, j: (j, i, 0))] + [HBM_SPEC] * len(gx),
        out_shape=[jax.ShapeDtypeStruct((t, d), F32), jax.ShapeDtypeStruct((nf, t, fc), BF16),
                   jax.ShapeDtypeStruct((nf, t, fc), BF16)] + _gather_shapes(gx, gnl),
        scratch_shapes=[pltpu.VMEM((tm, d), BF16), pltpu.VMEM((tm, d), F32)] + (_comm_sems(len(gx), 7) if gx else []),
        compiler_params=_cp(("arbitrary", "arbitrary")), name=name)(x, g.reshape(1, d), wg, wu, wd, *gx)
    return res[0], res[1], res[2], list(res[3:])


def _ffn_bwd(x, g, wg, wu, wd, a_pre, b_pre, dy, *, name):
    t, d = x.shape
    nf, _, fc = wg.shape
    tm = _pick(t, (FFN_TM, 256, 128))

    def body(x_ref, g_ref, wg_ref, wu_ref, wd_ref, a_ref, b_ref, dy_ref,
             dx_ref, dg_ref, ht_ref, dyh_ref, da_ref, db_ref, st_ref, acc_ref):
        i, j = pl.program_id(0), pl.program_id(1)

        @pl.when(j == 0)
        def _():
            xhat, _ = _rms_stats(x_ref[...])
            ht_ref[...] = (xhat * g_ref[...]).T.astype(BF16)
            dyh_ref[...] = (0.5 * dy_ref[...]).astype(BF16)
            acc_ref[...] = jnp.zeros_like(acc_ref)

        part = jnp.zeros((tm, d), F32)
        for q in range(FFN_SLOTS):
            a = a_ref[q].astype(F32)
            b = b_ref[q].astype(F32)
            sg = _sigmoid(a)
            silu = a * sg
            st_ref[q] = (silu * b).T.astype(BF16)
            ds = _bdot(dyh_ref[...], wd_ref[q], NT_DIMS)
            da = (ds * b * (sg * (1.0 + a * (1.0 - sg)))).astype(BF16)
            db = (ds * silu).astype(BF16)
            da_ref[q] = da
            db_ref[q] = db
            part = part + _bdot(da, wg_ref[q], NT_DIMS) + _bdot(db, wu_ref[q], NT_DIMS)
        acc_ref[...] += part

        @pl.when((i == 0) & (j == 0))
        def _():
            dg_ref[...] = jnp.zeros_like(dg_ref)

        @pl.when(j == nf // FFN_SLOTS - 1)
        def _():
            xhat, rstd = _rms_stats(x_ref[...])
            dx, dg = _rms_bwd_vals(acc_ref[...], xhat, rstd, g_ref[...])
            dx_ref[...] = dy_ref[...] + dx
            dg_ref[...] += dg

    row = pl.BlockSpec((tm, d), lambda i, j: (i, 0))
    vec = pl.BlockSpec((1, d), lambda i, j: (0, 0))
    fblk = pl.BlockSpec((FFN_SLOTS, tm, fc), lambda i, j: (j, i, 0))
    return pl.pallas_call(
        body, grid=(t // tm, nf // FFN_SLOTS),
        in_specs=[row, vec, pl.BlockSpec((FFN_SLOTS, d, fc), lambda i, j: (j, 0, 0)),
                  pl.BlockSpec((FFN_SLOTS, d, fc), lambda i, j: (j, 0, 0)),
                  pl.BlockSpec((FFN_SLOTS, fc, d), lambda i, j: (j, 0, 0)), fblk, fblk, row],
        out_specs=[row, vec, pl.BlockSpec((d, tm), lambda i, j: (0, i)), row, fblk, fblk,
                   pl.BlockSpec((FFN_SLOTS, fc, tm), lambda i, j: (j, 0, i))],
        out_shape=[jax.ShapeDtypeStruct((t, d), F32), jax.ShapeDtypeStruct((1, d), F32),
                   jax.ShapeDtypeStruct((d, t), BF16), jax.ShapeDtypeStruct((t, d), BF16),
                   jax.ShapeDtypeStruct((nf, t, fc), BF16), jax.ShapeDtypeStruct((nf, t, fc), BF16),
                   jax.ShapeDtypeStruct((nf, fc, t), BF16)],
        scratch_shapes=[pltpu.VMEM((tm, d), F32)],
        compiler_params=_cp(("arbitrary", "arbitrary")), name=name)(x, g.reshape(1, d), wg, wu, wd, a_pre, b_pre, dy)


def _pool_core(u, grp):
    s = u.shape[0]
    w2 = u + _shift_down(u, 1)
    w4 = w2 + _shift_down(w2, 2)
    w8 = w4 + _shift_down(w4, 4)
    w16 = w8 + _shift_down(w8, 8)
    wsum = jnp.where(grp == 0, w2, jnp.where(grp == 1, w4, jnp.where(grp == 2, w8, w16)))
    win = jnp.left_shift(2, grp).astype(F32)
    t1 = (lax.broadcasted_iota(jnp.int32, (s, 1), 0) + 1).astype(F32)
    inv = 1.0 / jnp.minimum(t1, win)
    return wsum * inv - u, inv


def _pool_fwd(proj, pool_w, pool_scale, nb, s, *, name):
    def body(u_ref, w_ref, sc_ref, y_ref):
        pooled, _ = _pool_core(u_ref[...].astype(F32), pl.program_id(0))
        y_ref[...] = _bdot(pooled, w_ref[...]) * sc_ref[...]

    return pl.pallas_call(
        body, grid=(NH, nb),
        in_specs=[pl.BlockSpec((s, HD), lambda g, b: (b, CB_POOL + g)),
                  pl.BlockSpec((None, HD, HD), lambda g, b: (g, 0, 0)), pl.BlockSpec((1, HD), lambda g, b: (0, g))],
        out_specs=pl.BlockSpec((s, HD), lambda g, b: (b, g)),
        out_shape=jax.ShapeDtypeStruct((nb * s, BW), F32),
        compiler_params=_cp(("parallel", "parallel")), name=name)(proj, pool_w, pool_scale)


def _pool_bwd(proj, pool_w, pool_scale, dy, nb, s, *, name):
    def body(u_ref, w_ref, sc_ref, dy_ref, du_ref, dw_ref, dsc_ref):
        grp, b = pl.program_id(0), pl.program_id(1)
        pooled, inv = _pool_core(u_ref[...].astype(F32), grp)
        mixed = _bdot(pooled, w_ref[...])
        dy = dy_ref[...]
        dmixed = dy * sc_ref[...]
        dpooled = _bdot(dmixed, w_ref[...], NT_DIMS)
        r = dpooled * inv
        v2 = r + _shift_up(r, 1)
        v4 = v2 + _shift_up(v2, 2)
        v8 = v4 + _shift_up(v4, 4)
        v16 = v8 + _shift_up(v8, 8)
        vsum = jnp.where(grp == 0, v2, jnp.where(grp == 1, v4, jnp.where(grp == 2, v8, v16)))
        du_ref[...] = vsum - dpooled

        @pl.when(b == 0)
        def _():
            dw_ref[...] = jnp.zeros_like(dw_ref)
            dsc_ref[...] = jnp.zeros_like(dsc_ref)

        dw_ref[...] += _bdot(pooled, dmixed, TN_DIMS)
        dsc_ref[...] += jnp.sum(dy * mixed, axis=0, keepdims=True)

    return pl.pallas_call(
        body, grid=(NH, nb),
        in_specs=[pl.BlockSpec((s, HD), lambda g, b: (b, CB_POOL + g)),
                  pl.BlockSpec((None, HD, HD), lambda g, b: (g, 0, 0)), pl.BlockSpec((1, HD), lambda g, b: (0, g)),
                  pl.BlockSpec((s, HD), lambda g, b: (b, g))],
        out_specs=[pl.BlockSpec((s, HD), lambda g, b: (b, g)), pl.BlockSpec((None, HD, HD), lambda g, b: (g, 0, 0)),
                   pl.BlockSpec((1, HD), lambda g, b: (0, g))],
        out_shape=[jax.ShapeDtypeStruct((nb * s, BW), F32), jax.ShapeDtypeStruct((NH, HD, HD), F32),
                   jax.ShapeDtypeStruct((1, BW), F32)],
        compiler_params=_cp(("arbitrary", "arbitrary")), name=name)(proj, pool_w, pool_scale, dy)


SB_BLK = 128


SB_G = 4
SB_KG = SB_G * SB_BLK
SB_Q = 2 * SB_BLK


def _sb_block(qb, kg, q0, k0, diagonal):
    z = _bdot(qb, kg, NT_DIMS) * (HD ** -0.5)
    lsz = _log_sigmoid(z)
    if not diagonal:
        return lsz, lsz - z, None
    row = lax.broadcasted_iota(jnp.int32, z.shape, 0) + q0
    col = lax.broadcasted_iota(jnp.int32, z.shape, 1) + k0
    causal = col < row
    return lsz, jnp.where(causal, lsz - z, 0.0), causal


def _keep(causal, x):
    return x if causal is None else jnp.where(causal, x, 0.0)


def _sub(x, m):
    return x[:, m * SB_BLK:(m + 1) * SB_BLK]


def _sb_tails(lnm, after, ct):
    hi = lnm.astype(BF16)
    lo = (lnm - hi.astype(F32)).astype(BF16)
    tails = [None] * SB_G
    for m in reversed(range(SB_G)):
        tails[m] = (lax.dot_general(_sub(hi, m), after, NN_DIMS, preferred_element_type=F32)
                    + lax.dot_general(_sub(lo, m), after, NN_DIMS, preferred_element_type=F32)) + ct
        ct = ct + jnp.sum(_sub(lnm, m), axis=1, keepdims=True)
    ones = jnp.ones((8, lnm.shape[1]), BF16)
    rows = (lax.dot_general(ones, hi, NT_DIMS, preferred_element_type=F32)
            + lax.dot_general(ones, lo, NT_DIMS, preferred_element_type=F32))
    return jnp.concatenate(tails, axis=1), rows, ct


def _tri01(lower):
    r = lax.broadcasted_iota(jnp.int32, (SB_BLK, SB_BLK), 0)
    c = lax.broadcasted_iota(jnp.int32, (SB_BLK, SB_BLK), 1)
    return jnp.where((r < c) if lower else (r > c), 1.0, 0.0).astype(BF16)


def _split3(x):
    hi = x.astype(BF16)
    mid = (x - hi.astype(F32)).astype(BF16)
    lo = (x - hi.astype(F32) - mid.astype(F32)).astype(BF16)
    return hi, mid, lo


def _rows_to_cols(rows):
    eighth = jnp.full((8, LANE), 0.125, BF16)
    return sum(lax.dot_general(p, eighth, TN_DIMS, preferred_element_type=F32) for p in _split3(rows))


def _sb_fwd(proj, nb, s, *, name):
    nq = s // SB_Q
    ng = s // SB_KG

    def body(q_ref, k_ref, v_ref, o_ref, ctr_ref):
        after = _tri01(False)

        def qblock(i, _):
            q0 = pl.multiple_of(i * SB_Q, SB_Q)
            qb = q_ref[pl.ds(q0, SB_Q), :]

            def kgroup(g, carry, diagonal):
                acc, ct, ctr = carry
                k0 = pl.multiple_of(g * SB_KG, SB_KG)
                lsz, lnm, causal = _sb_block(qb, k_ref[pl.ds(k0, SB_KG), :], q0, k0, diagonal)
                ctr_ref[i * ng + g] = ctr
                tail, rows, ct = _sb_tails(lnm, after, ct)
                w = _keep(causal, jnp.exp(lsz + tail))
                return acc + _bdot(w, v_ref[pl.ds(k0, SB_KG), :]), ct, ctr + rows

            gd = (i * SB_Q) // SB_KG
            carry = kgroup(gd, (jnp.zeros((SB_Q, HD), F32), jnp.zeros((SB_Q, 1), F32), jnp.zeros((8, SB_Q), F32)), True)
            acc, _, _ = lax.fori_loop(0, gd, lambda jj, c: kgroup(gd - 1 - jj, c, False), carry)
            o_ref[pl.ds(q0, SB_Q), :] = acc
            return 0

        lax.fori_loop(0, nq, qblock, 0)

    def col(cb):
        return pl.BlockSpec((s, HD), lambda b, h: (b, cb + h))

    return pl.pallas_call(
        body, grid=(nb, NH), in_specs=[col(CB_SBQ), col(CB_SBK), col(CB_SBV)],
        out_specs=[pl.BlockSpec((s, HD), lambda b, h: (b, h)),
                   pl.BlockSpec((None, None, nq * ng, 8, SB_Q), lambda b, h: (b, h, 0, 0, 0))],
        out_shape=[jax.ShapeDtypeStruct((nb * s, BW), F32), jax.ShapeDtypeStruct((nb, NH, nq * ng, 8, SB_Q), F32)],
        compiler_params=_cp(("parallel", "parallel")), name=name)(proj, proj, proj)


def _sb_bwd(proj, ctr, dy, nb, s, *, name, scatter=None):
    nq = s // SB_Q
    ng = s // SB_KG
    scale = HD ** -0.5
    gx, gnl = scatter if scatter else ([], [])

    def body(q_ref, k_ref, v_ref, ctr_ref, do_ref, dq_ref, dk_ref, dv_ref):
        after = _tri01(False)
        before = _tri01(True)
        dk_ref[...] = jnp.zeros_like(dk_ref)
        dv_ref[...] = jnp.zeros_like(dv_ref)

        def qblock(i, _):
            q0 = pl.multiple_of(i * SB_Q, SB_Q)
            qb = q_ref[pl.ds(q0, SB_Q), :]
            dob = do_ref[pl.ds(q0, SB_Q), :]

            def kgroup(g, carry, diagonal):
                dq, ce = carry
                k0 = pl.multiple_of(g * SB_KG, SB_KG)
                kg = k_ref[pl.ds(k0, SB_KG), :]
                vg = v_ref[pl.ds(k0, SB_KG), :]
                lsz, lnm, causal = _sb_block(qb, kg, q0, k0, diagonal)
                tail, _, _ = _sb_tails(lnm, after, _rows_to_cols(ctr_ref[i * ng + g])[:, 0:1])
                w = _keep(causal, jnp.exp(lsz + tail))
                e = _bdot(dob, vg, NT_DIMS) * w
                pres = []
                for m in range(SB_G):
                    pres.append(_split_dot(_sub(e, m), before) + ce)
                    ce = ce + jnp.sum(_sub(e, m), axis=1, keepdims=True)
                sig = jnp.exp(lsz)
                dz = _keep(causal, e * (1.0 - sig) - jnp.concatenate(pres, axis=1) * sig) * scale
                dk_ref[pl.ds(k0, SB_KG), :] += _bdot(dz, qb, TN_DIMS)
                dv_ref[pl.ds(k0, SB_KG), :] += _bdot(w, dob, TN_DIMS)
                return dq + _bdot(dz, kg), ce

            gd = (i * SB_Q) // SB_KG
            carry = lax.fori_loop(0, gd, lambda g, c: kgroup(g, c, False),
                                  (jnp.zeros((SB_Q, HD), F32), jnp.zeros((SB_Q, 1), F32)))
            dq, _ = kgroup(gd, carry, True)
            dq_ref[pl.ds(q0, SB_Q), :] = dq
            return 0

        lax.fori_loop(0, nq, qblock, 0)

    def col(cb):
        return pl.BlockSpec((s, HD), lambda b, h: (b, cb + h))

    out = pl.BlockSpec((s, HD), lambda b, h: (b, h))
    sds = jax.ShapeDtypeStruct((nb * s, BW), F32)
    exchange = (_all_to_all_parts, len(gx), gnl) if gx else None
    res = pl.pallas_call(
        _with_exchange(body, 5, 3, 0, exchange, (nb, NH)), grid=(nb, NH),
        in_specs=[col(CB_SBQ), col(CB_SBK), col(CB_SBV),
                  pl.BlockSpec((None, None, nq * ng, 8, SB_Q), lambda b, h: (b, h, 0, 0, 0)), out]
        + [HBM_SPEC] * len(gx),
        out_specs=[out, out, out] + [HBM_SPEC] * len(gx), out_shape=[sds, sds, sds] + _all_to_all_shapes(gx, gnl),
        scratch_shapes=_comm_sems(len(gx), 7) if gx else [],
        compiler_params=_cp(("arbitrary", "arbitrary")), name=name)(proj, proj, proj, ctr, dy, *gx)
    return res[0], res[1], res[2], list(res[3:])


def _make_cdot(dims, dims_da, dims_db, swap_a=False, swap_b=False):
    @jax.custom_vjp
    def f(a, b):
        return _bdot(a, b, dims)

    def fwd(a, b):
        return _bdot(a, b, dims), (a, b)

    def bwd(res, g):
        a, b = res
        da = _bdot(b, g, dims_da) if swap_a else _bdot(g, b, dims_da)
        db = _bdot(g, a, dims_db) if swap_b else _bdot(a, g, dims_db)
        return da, db

    f.defvjp(fwd, bwd)
    return f


_cdot = _make_cdot(NN_DIMS, NT_DIMS, TN_DIMS)
_cdot_nt = _make_cdot(NT_DIMS, NN_DIMS, TN_DIMS, swap_b=True)
_cdot_tn = _make_cdot(TN_DIMS, NT_DIMS, NN_DIMS, swap_a=True)


DN_SUPER = 4 * DN_CHUNK


@jax.custom_vjp
def _unit_lower_inverse(lmat):
    n = lmat.shape[0]
    steps = int(math.log2(DN_CHUNK))
    eye = jnp.where(lax.broadcasted_iota(jnp.int32, (n, n), 0) == lax.broadcasted_iota(jnp.int32, (n, n), 1), 1.0, 0.0)
    inv = eye - lmat
    pw = _hdot(lmat, lmat)
    for it in range(steps - 1):
        inv = inv + _hdot(inv, pw)
        if it < steps - 2:
            pw = _hdot(pw, pw)
    return inv


def _unit_lower_inverse_fwd(lmat):
    inv = _unit_lower_inverse(lmat)
    return inv, inv


def _unit_lower_inverse_bwd(inv, g):
    return (-_hdot(_hdot(inv, g, TN_DIMS), inv, NT_DIMS),)


_unit_lower_inverse.defvjp(_unit_lower_inverse_fwd, _unit_lower_inverse_bwd)


@jax.custom_vjp
def _known_inverse(lmat, inv):
    return inv


def _known_inverse_fwd(lmat, inv):
    return inv, inv


def _known_inverse_bwd(inv, g):
    return -_hdot(_hdot(inv, g, TN_DIMS), inv, NT_DIMS), jnp.zeros_like(inv)


_known_inverse.defvjp(_known_inverse_fwd, _known_inverse_bwd)


def _dn_local(q, k, v, bb, gb, known_inv=None):
    n = q.shape[0]
    r = lax.broadcasted_iota(jnp.int32, (n, n), 0)
    cc = lax.broadcasted_iota(jnp.int32, (n, n), 1)
    shift = int(math.log2(DN_CHUNK))
    same = lax.shift_right_logical(r, shift) == lax.shift_right_logical(cc, shift)
    incl = jnp.where(same, jnp.where(r >= cc, 1.0, 0.0), 0.0)
    strict = jnp.where(same, jnp.where(r > cc, 1.0, 0.0), 0.0)
    gc = _hdot(incl, gb)
    gc_row = _hdot(jnp.full((n, HD), 1.0 / HD, F32), gc, NT_DIMS)
    diff = jnp.concatenate([gc] * (n // HD), axis=1) - gc_row
    decay = incl * jnp.exp(diff * incl)
    kb = k * bb
    lmat = _cdot_nt(kb, k) * (strict * decay)
    egc = jnp.exp(gc)
    inv = _unit_lower_inverse(lmat) if known_inv is None else _known_inverse(lmat, known_inv)
    u = _hdot(inv, v * bb)
    w = _hdot(inv, kb * egc)
    attn = _cdot_nt(q, k) * decay
    gl = _hdot(jnp.where(same, 1.0, 0.0), gb)
    return u, w, attn, q * egc, k * jnp.exp(gl - gc), jnp.exp(gl), inv


def _attn_pairs(attn):
    return jnp.concatenate([attn[:HD, :HD], attn[HD:, HD:]], axis=0)


def _attn_unpairs(a):
    z = jnp.zeros((HD, HD), F32)
    return jnp.concatenate([jnp.concatenate([a[:HD], z], axis=1), jnp.concatenate([z, a[HD:]], axis=1)], axis=0)


def _dn_step(u, w, a, qd, kd, cdrows, state, odd):
    v_new = u - _cdot(w, state)
    z = jnp.zeros_like(v_new)
    o = _cdot(qd, state) + _cdot(a, jnp.concatenate([z, v_new] if odd else [v_new, z], axis=0))
    return o, state * jnp.mean(cdrows, axis=0, keepdims=True) + _cdot_tn(kd, v_new)


def _dn_local_pass(fn, s, ins, outs):
    def step(it, _):
        sl = pl.ds(pl.multiple_of(it * DN_SUPER, DN_SUPER), DN_SUPER)
        res = fn(*[ref[sl, :] for ref in ins])
        for ref, val in zip(outs, res):
            ref[sl, :] = val
        return 0

    lax.fori_loop(0, s // DN_SUPER, step, 0)


def _lane_pick(row, idx):
    lane = lax.broadcasted_iota(jnp.int32, row.shape, 1)
    return jnp.sum(jnp.where(lane == idx, row, 0.0), axis=1, keepdims=True)


def _col_pick(x, idx):
    lane = lax.broadcasted_iota(jnp.int32, x.shape, 1)
    return jnp.sum(jnp.where(lane == idx, x, 0.0), axis=1, keepdims=True)


def _conv_silu(x, w):
    xc = (w[3:4, :] * x + w[2:3, :] * _shift_down(x, 1) + w[1:2, :] * _shift_down(x, 2)
          + w[0:1, :] * _shift_down(x, 3))
    return xc * _sigmoid(xc), xc


def _conv_silu_bwd(x, w, xc, dxs, dw_ref):
    sg = _sigmoid(xc)
    dxc = dxs * (sg * (1.0 + xc * (1.0 - sg)))
    dx = (w[3:4, :] * dxc + w[2:3, :] * _shift_up(dxc, 1) + w[1:2, :] * _shift_up(dxc, 2)
          + w[0:1, :] * _shift_up(dxc, 3))
    dw_ref[3:4, :] += jnp.sum(dxc * x, axis=0, keepdims=True)
    dw_ref[2:3, :] += jnp.sum(dxc * _shift_down(x, 1), axis=0, keepdims=True)
    dw_ref[1:2, :] += jnp.sum(dxc * _shift_down(x, 2), axis=0, keepdims=True)
    dw_ref[0:1, :] += jnp.sum(dxc * _shift_down(x, 3), axis=0, keepdims=True)
    return dx


def _dn_prep(qr_ref, kr_ref, vr_ref, ab_ref, cq_ref, ck_ref, cv_ref, par_ref, head):
    qs, qc = _conv_silu(qr_ref[...].astype(F32), cq_ref[...])
    ks, kc = _conv_silu(kr_ref[...].astype(F32), ck_ref[...])
    vs, vc = _conv_silu(vr_ref[...].astype(F32), cv_ref[...])
    rq = lax.rsqrt(jnp.sum(qs * qs, axis=1, keepdims=True) + EPS)
    rk = lax.rsqrt(jnp.sum(ks * ks, axis=1, keepdims=True) + EPS)
    ab = ab_ref[...]
    a_in = _col_pick(ab, head) + _lane_pick(par_ref[1:2, :], head)
    beta = _sigmoid(_col_pick(ab, NH + head))
    neg_ea = -jnp.exp(_lane_pick(par_ref[0:1, :], head))
    g = neg_ea * _softplus(a_in)
    return dict(q=qs * rq * (HD ** -0.5), k=ks * rk, v=vs, beta=beta, g=g, qs=qs, ks=ks, qc=qc, kc=kc, vc=vc,
                rq=rq, rk=rk, a_in=a_in, neg_ea=neg_ea)


ONE_BUF = pl.Buffered(1)
DN_BWD_VMEM = 62 * 1024 * 1024


def _dn_specs(nb, s):
    def col(cb):
        return pl.BlockSpec((s, HD), lambda h, b: (b, cb + h))

    def conv(cb):
        return pl.BlockSpec((DN_CONV_W, HD), lambda h, b: (0, cb + h))

    return col, conv


DN_CONV_W = 4


def _with_exchange(body, n_in, n_out, n_scratch, exchange, grid):
    if exchange is None:
        return body
    parts_fn, n, nls = exchange

    def wrapped(*refs):
        ins, xs = refs[:n_in], refs[n_in:n_in + n]
        outs, os = refs[n_in + n:n_in + n + n_out], refs[n_in + n + n_out:n_in + 2 * n + n_out]
        rest = refs[n_in + 2 * n + n_out:]
        scratch, sems = rest[:n_scratch], rest[n_scratch:]
        pos = [pl.program_id(k) for k in range(len(grid))]
        first = functools.reduce(jnp.logical_and, [p == 0 for p in pos])
        last = functools.reduce(jnp.logical_and, [p == g - 1 for p, g in zip(pos, grid)])
        start, forward, finish = parts_fn(xs, os, nls, *sems)
        pl.when(first)(start)
        pl.when(last)(forward)
        body(*ins, *outs, *scratch)
        pl.when(last)(finish)

    return wrapped


def _dn_fwd(proj, ab, conv_w, par, gain, nb, s, *, name, gather=None):
    nc = s // DN_CHUNK
    col, conv = _dn_specs(nb, s)
    gx, gnl = gather if gather else ([], [])

    def body(qr_ref, kr_ref, vr_ref, z_ref, ab_ref, cq_ref, ck_ref, cv_ref, par_ref, gain_ref,
             y_ref, o_ref, st_ref, u_ref, w_ref, at_ref, qd_ref, kd_ref, cd_ref, inv_ref, q_s, k_s, v_s, bb_s, gb_s):
        p = _dn_prep(qr_ref, kr_ref, vr_ref, ab_ref, cq_ref, ck_ref, cv_ref, par_ref, pl.program_id(0))
        q_s[...], k_s[...], v_s[...] = p["q"], p["k"], p["v"]
        bb_s[...] = jnp.broadcast_to(p["beta"], (s, HD))
        gb_s[...] = jnp.broadcast_to(p["g"], (s, HD))
        def local(*args):
            u, w, attn, qd, kd, cd, inv = _dn_local(*args)
            return u, w, _attn_pairs(attn), qd, kd, cd, inv

        _dn_local_pass(local, s, [q_s, k_s, v_s, bb_s, gb_s], [u_ref, w_ref, at_ref, qd_ref, kd_ref, cd_ref, inv_ref])

        def chunk_pair(pi, state):
            for odd in (0, 1):
                ci = 2 * pi + odd
                sl = pl.ds(pl.multiple_of(ci * DN_CHUNK, DN_CHUNK), DN_CHUNK)
                st_ref[ci] = state
                o, state = _dn_step(u_ref[sl, :], w_ref[sl, :], at_ref[sl, :], qd_ref[sl, :], kd_ref[sl, :],
                                    cd_ref[sl, :], state, odd)
                o_ref[sl, :] = o
            return state

        lax.fori_loop(0, nc // 2, chunk_pair, jnp.zeros((HD, HD), F32))
        o = o_ref[...]
        z = z_ref[...].astype(F32)
        on = o * lax.rsqrt(jnp.mean(o * o, axis=1, keepdims=True) + EPS) * gain_ref[...]
        y_ref[...] = on * (z * _sigmoid(z))

    out = pl.BlockSpec((s, HD), lambda h, b: (b, h))
    sds = jax.ShapeDtypeStruct((nb * s, BW), F32)
    exchange = (_gather_parts, len(gx), gnl) if gx else None
    res = pl.pallas_call(
        _with_exchange(body, 10, 10, 5, exchange, (NH, nb)), grid=(NH, nb),
        in_specs=[col(CB_DNQ), col(CB_DNK), col(CB_DNV), col(CB_DNZ), pl.BlockSpec((s, LANE), lambda h, b: (b, 0)),
                  conv(0), conv(NH), conv(2 * NH), pl.BlockSpec((8, LANE), lambda h, b: (0, 0)),
                  pl.BlockSpec((1, HD), lambda h, b: (0, 0))] + [HBM_SPEC] * len(gx),
        out_specs=[out, out, pl.BlockSpec((None, None, nc, HD, HD), lambda h, b: (b, h, 0, 0, 0))] + [out] * 6
        + [pl.BlockSpec((s, DN_SUPER), lambda h, b: (b, h))] + [HBM_SPEC] * len(gx),
        out_shape=[sds, sds, jax.ShapeDtypeStruct((nb, NH, nc, HD, HD), F32)] + [sds] * 6
        + [jax.ShapeDtypeStruct((nb * s, NH * DN_SUPER), F32)] + _gather_shapes(gx, gnl),
        scratch_shapes=[pltpu.VMEM((s, HD), F32)] * 5 + (_comm_sems(len(gx), 7) if gx else []),
        compiler_params=_cp(("arbitrary", "arbitrary")), name=name)(
            proj, proj, proj, proj, ab, conv_w, conv_w, conv_w, par, gain, *gx)
    return res[0], res[1], res[2], list(res[3:10]), list(res[10:])


def _dn_bwd(proj, ab, conv_w, par, gain, o_pre, states, local, dy, nb, s, *, name, scatter=None):
    nc = s // DN_CHUNK
    col, conv = _dn_specs(nb, s)
    gx, gnl = scatter if scatter else ([], [])

    def body(qr_ref, kr_ref, vr_ref, z_ref, ab_ref, cq_ref, ck_ref, cv_ref, par_ref, gain_ref, o_ref, st_ref, dy_ref,
             u_hbm, w_hbm, at_hbm, qd_hbm, kd_hbm, cd_hbm, inv_hbm,
             dqr_ref, dkr_ref, dvr_ref, dz_ref, dab_ref, dcq_ref, dck_ref, dcv_ref, dpar_ref, dgain_ref,
             q_s, k_s, v_s, bb_s, gb_s, do_s, u_s, w_s, qd_s, kd_s, at_s, cd_s, load_sems, inv_buf, inv_sems):
        head, b = pl.program_id(0), pl.program_id(1)
        local_refs = [u_s, w_s, at_s, qd_s, kd_s, cd_s]
        loads = [pltpu.make_async_copy(src.at[pl.ds(pl.multiple_of(b * s, s), s), pl.ds(pl.multiple_of(head * HD, HD), HD)],
                                       dst, load_sems.at[i])
                 for i, (src, dst) in enumerate(zip((u_hbm, w_hbm, at_hbm, qd_hbm, kd_hbm, cd_hbm), local_refs))]
        for cp in loads:
            cp.start()
        p = _dn_prep(qr_ref, kr_ref, vr_ref, ab_ref, cq_ref, ck_ref, cv_ref, par_ref, head)
        q_s[...], k_s[...], v_s[...] = p["q"], p["k"], p["v"]
        bb_s[...] = jnp.broadcast_to(p["beta"], (s, HD))
        gb_s[...] = jnp.broadcast_to(p["g"], (s, HD))

        @pl.when(b == 0)
        def _():
            for ref in (dcq_ref, dck_ref, dcv_ref, dpar_ref):
                ref[...] = jnp.zeros_like(ref)

        @pl.when((b == 0) & (head == 0))
        def _():
            dgain_ref[...] = jnp.zeros_like(dgain_ref)

        o, z, dy = o_ref[...], z_ref[...].astype(F32), dy_ref[...]
        rstd = lax.rsqrt(jnp.mean(o * o, axis=1, keepdims=True) + EPS)
        ohat = o * rstd
        sgz = _sigmoid(z)
        dz_ref[...] = dy * (ohat * gain_ref[...]) * (sgz * (1.0 + z * (1.0 - sgz)))
        don = dy * (z * sgz)
        dgain_ref[...] += jnp.sum(don * ohat, axis=0, keepdims=True)
        dxh = don * gain_ref[...]
        do_s[...] = rstd * (dxh - ohat * jnp.mean(dxh * ohat, axis=1, keepdims=True))

        for cp in loads:
            cp.wait()

        def chunk_pair(pr, dstate):
            for odd in (1, 0):
                ci = nc - 1 - 2 * pr - (1 - odd)
                sl = pl.ds(pl.multiple_of(ci * DN_CHUNK, DN_CHUNK), DN_CHUNK)
                _, vjp = jax.vjp(functools.partial(_dn_step, odd=odd), u_s[sl, :], w_s[sl, :], at_s[sl, :],
                                 qd_s[sl, :], kd_s[sl, :], cd_s[sl, :], st_ref[ci])
                du, dw, dat, dqd, dkd, dcd, dstate = vjp((do_s[sl, :], dstate))
                u_s[sl, :], w_s[sl, :], at_s[sl, :], qd_s[sl, :], kd_s[sl, :], cd_s[sl, :] = du, dw, dat, dqd, dkd, dcd
            return dstate

        lax.fori_loop(0, nc // 2, chunk_pair, jnp.zeros((HD, HD), F32))

        def inv_load(it, slot):
            rows = pl.ds(pl.multiple_of(b * s + it * DN_SUPER, DN_SUPER), DN_SUPER)
            cols = pl.ds(pl.multiple_of(head * DN_SUPER, DN_SUPER), DN_SUPER)
            return pltpu.make_async_copy(inv_hbm.at[rows, cols], inv_buf.at[slot], inv_sems.at[slot])

        def local_bwd(it, _):
            slot = lax.rem(it, 2)
            inv_load(it, slot).wait()

            @pl.when(it + 1 < s // DN_SUPER)
            def _():
                inv_load(it + 1, 1 - slot).start()

            sl = pl.ds(pl.multiple_of(it * DN_SUPER, DN_SUPER), DN_SUPER)
            ins = [ref[sl, :] for ref in (q_s, k_s, v_s, bb_s, gb_s)]
            du, dw, dat, dqd, dkd, dcd = [ref[sl, :] for ref in local_refs]
            _, vjp = jax.vjp(lambda *a: _dn_local(*a, known_inv=inv_buf[slot])[:6], *ins)
            dq, dk, dv, dbb, dgb = vjp((du, dw, _attn_unpairs(dat), dqd, dkd, dcd))
            q_s[sl, :], k_s[sl, :], v_s[sl, :] = dq, dk, dv
            bb_s[sl, :] = jnp.broadcast_to(jnp.sum(dbb, axis=1, keepdims=True), (DN_SUPER, HD))
            gb_s[sl, :] = jnp.broadcast_to(jnp.sum(dgb, axis=1, keepdims=True), (DN_SUPER, HD))
            return 0

        inv_load(0, 0).start()
        lax.fori_loop(0, s // DN_SUPER, local_bwd, 0)

        dq, dk, dv = q_s[...], k_s[...], v_s[...]
        qs, ks, rq, rk = p["qs"], p["ks"], p["rq"], p["rk"]
        dqs = (HD ** -0.5) * (rq * dq - qs * (rq * rq * rq) * jnp.sum(dq * qs, axis=1, keepdims=True))
        dks = rk * dk - ks * (rk * rk * rk) * jnp.sum(dk * ks, axis=1, keepdims=True)
        dqr_ref[...] = _conv_silu_bwd(qr_ref[...].astype(F32), cq_ref[...], p["qc"], dqs, dcq_ref)
        dkr_ref[...] = _conv_silu_bwd(kr_ref[...].astype(F32), ck_ref[...], p["kc"], dks, dck_ref)
        dvr_ref[...] = _conv_silu_bwd(vr_ref[...].astype(F32), cv_ref[...], p["vc"], dv, dcv_ref)

        dbeta, dg = bb_s[:, 0:1], gb_s[:, 0:1]
        beta = p["beta"]
        db_logit = dbeta * beta * (1.0 - beta)
        da = dg * p["neg_ea"] * _sigmoid(p["a_in"])
        lane = lax.broadcasted_iota(jnp.int32, (s, LANE), 1)
        dab_ref[...] = jnp.where(lane == head, da, 0.0) + jnp.where(lane == NH + head, db_logit, 0.0)
        dpar_ref[0:1, :] += jnp.broadcast_to(jnp.sum(dg * p["g"], axis=0, keepdims=True), (1, LANE))
        dpar_ref[1:2, :] += jnp.broadcast_to(jnp.sum(da, axis=0, keepdims=True), (1, LANE))

    out = pl.BlockSpec((s, HD), lambda h, b: (b, h))
    in_blk = pl.BlockSpec((s, HD), lambda h, b: (b, h), pipeline_mode=ONE_BUF)
    cblk = pl.BlockSpec((DN_CONV_W, HD), lambda h, b: (0, h))
    sds = jax.ShapeDtypeStruct((nb * s, BW), F32)
    csds = jax.ShapeDtypeStruct((DN_CONV_W, BW), F32)
    exchange = (_all_to_all_parts, len(gx), gnl) if gx else None
    res = pl.pallas_call(
        _with_exchange(body, 20, 10, 15, exchange, (NH, nb)), grid=(NH, nb),
        in_specs=[col(CB_DNQ), col(CB_DNK), col(CB_DNV), col(CB_DNZ),
                  pl.BlockSpec((s, LANE), lambda h, b: (b, 0), pipeline_mode=ONE_BUF),
                  conv(0), conv(NH), conv(2 * NH), pl.BlockSpec((8, LANE), lambda h, b: (0, 0)),
                  pl.BlockSpec((1, HD), lambda h, b: (0, 0)), in_blk,
                  pl.BlockSpec((None, None, nc, HD, HD), lambda h, b: (b, h, 0, 0, 0), pipeline_mode=ONE_BUF), in_blk]
        + [HBM_SPEC] * (7 + len(gx)),
        out_specs=[out, out, out, out, pl.BlockSpec((None, s, LANE), lambda h, b: (h, b, 0)), cblk, cblk, cblk,
                   pl.BlockSpec((None, 8, LANE), lambda h, b: (h, 0, 0)), pl.BlockSpec((1, HD), lambda h, b: (0, 0))]
        + [HBM_SPEC] * len(gx),
        out_shape=[sds, sds, sds, sds, jax.ShapeDtypeStruct((NH, nb * s, LANE), F32), csds, csds, csds,
                   jax.ShapeDtypeStruct((NH, 8, LANE), F32), jax.ShapeDtypeStruct((1, HD), F32)]
        + _all_to_all_shapes(gx, gnl),
        scratch_shapes=[pltpu.VMEM((s, HD), F32)] * 12 + [pltpu.SemaphoreType.DMA((6,)),
                                                           pltpu.VMEM((2, DN_SUPER, DN_SUPER), F32),
                                                           pltpu.SemaphoreType.DMA((2,))]
        + (_comm_sems(len(gx), 7) if gx else []),
        compiler_params=_cp(("arbitrary", "arbitrary"), DN_BWD_VMEM), name=name)(
            proj, proj, proj, proj, ab, conv_w, conv_w, conv_w, par, gain, o_pre, states, dy, *local, *gx)
    return tuple(res[:10]) + (list(res[10:]),)


def _sum_heads(x, *, name):
    nh, t, c = x.shape
    tm = _pick(t, (1024, 512, 256, 128))

    def body(x_ref, o_ref):
        o_ref[...] = (x_ref[0] + x_ref[1] + x_ref[2] + x_ref[3]).astype(BF16)

    return pl.pallas_call(
        body, grid=(t // tm,), in_specs=[pl.BlockSpec((nh, tm, c), lambda i: (0, i, 0))],
        out_specs=pl.BlockSpec((tm, c), lambda i: (i, 0)), out_shape=jax.ShapeDtypeStruct((t, c), BF16),
        compiler_params=_cp(("parallel",)), name=name)(x)


MERGE_TM = 256


def _merge_fwd(x, proj, yp, yd, ys, b_gate, wb, wo, *, name):
    t, d = x.shape
    tm = _pick(t, (MERGE_TM, 128))

    def body(x_ref, g0_ref, g1_ref, g2_ref, yp_ref, yd_ref, ys_ref, bg_ref, wb_ref, wo_ref, o_ref):
        merged = jnp.zeros((tm, d), F32)
        for n, (g_ref, y_ref) in enumerate(((g0_ref, yp_ref), (g1_ref, yd_ref), (g2_ref, ys_ref))):
            gate = _sigmoid(g_ref[...].astype(F32) + bg_ref[:, n * d:(n + 1) * d])
            merged = merged + gate * _bdot(y_ref[...], wb_ref[n])
        o_ref[...] = x_ref[...] + _bdot(merged, wo_ref[...])

    row = pl.BlockSpec((tm, d), lambda i: (i, 0))
    yblk = pl.BlockSpec((tm, BW), lambda i: (i, 0))

    def gl(n):
        return pl.BlockSpec((tm, d), lambda i: (i, CB_GATE + n))

    return pl.pallas_call(
        body, grid=(t // tm,),
        in_specs=[row, gl(0), gl(1), gl(2), yblk, yblk, yblk, pl.BlockSpec((1, 3 * d), lambda i: (0, 0)),
                  pl.BlockSpec((3, BW, d), lambda i: (0, 0, 0)), pl.BlockSpec((d, d), lambda i: (0, 0))],
        out_specs=row, out_shape=jax.ShapeDtypeStruct((t, d), F32),
        compiler_params=_cp(("parallel",)), name=name)(x, proj, proj, proj, yp, yd, ys, b_gate, wb, wo)


def _merge_bwd(proj, yp, yd, ys, b_gate, wb, wo, dx, *, name):
    t, d = dx.shape
    tm = _pick(t, (MERGE_TM, 128))

    def body(g0_ref, g1_ref, g2_ref, yp_ref, yd_ref, ys_ref, bg_ref, wb_ref, wo_ref, dx_ref,
             dyp_ref, dyd_ref, dys_ref, dgl_ref, mg_ref, dxh_ref, dbd_ref, dbg_ref):
        dxh = dx_ref[...].astype(BF16)
        dxh_ref[...] = dxh
        dmerged = _bdot(dxh, wo_ref[...], NT_DIMS)
        merged = jnp.zeros((tm, d), F32)

        @pl.when(pl.program_id(0) == 0)
        def _():
            dbg_ref[...] = jnp.zeros_like(dbg_ref)

        for n, (g_ref, y_ref, dy_ref) in enumerate(((g0_ref, yp_ref, dyp_ref), (g1_ref, yd_ref, dyd_ref),
                                                    (g2_ref, ys_ref, dys_ref))):
            gate = _sigmoid(g_ref[...].astype(F32) + bg_ref[:, n * d:(n + 1) * d])
            bd = _bdot(y_ref[...], wb_ref[n])
            merged = merged + gate * bd
            dgl = dmerged * bd * gate * (1.0 - gate)
            dgl_ref[:, n * d:(n + 1) * d] = dgl.astype(BF16)
            dbg_ref[:, n * d:(n + 1) * d] += jnp.sum(dgl, axis=0, keepdims=True)
            dbd = (dmerged * gate).astype(BF16)
            dbd_ref[n] = dbd
            dy_ref[...] = _bdot(dbd, wb_ref[n], NT_DIMS)
        mg_ref[...] = merged.astype(BF16)

    row = pl.BlockSpec((tm, d), lambda i: (i, 0))
    yblk = pl.BlockSpec((tm, BW), lambda i: (i, 0))
    bgv = pl.BlockSpec((1, 3 * d), lambda i: (0, 0))

    def gl(n):
        return pl.BlockSpec((tm, d), lambda i: (i, CB_GATE + n))

    ysds = jax.ShapeDtypeStruct((t, BW), F32)
    return pl.pallas_call(
        body, grid=(t // tm,),
        in_specs=[gl(0), gl(1), gl(2), yblk, yblk, yblk, bgv,
                  pl.BlockSpec((3, BW, d), lambda i: (0, 0, 0)), pl.BlockSpec((d, d), lambda i: (0, 0)), row],
        out_specs=[yblk, yblk, yblk, pl.BlockSpec((tm, 3 * d), lambda i: (i, 0)), row, row,
                   pl.BlockSpec((3, tm, d), lambda i: (0, i, 0)), bgv],
        out_shape=[ysds, ysds, ysds, jax.ShapeDtypeStruct((t, 3 * d), BF16), jax.ShapeDtypeStruct((t, d), BF16),
                   jax.ShapeDtypeStruct((t, d), BF16), jax.ShapeDtypeStruct((3, t, d), BF16),
                   jax.ShapeDtypeStruct((1, 3 * d), F32)],
        compiler_params=_cp(("arbitrary",)), name=name)(proj, proj, proj, yp, yd, ys, b_gate, wb, wo, dx)


def _loss_head(x, g, target, *, name):
    t, d = x.shape
    tm = _pick(t, (512, 256, 128))

    def body(x_ref, g_ref, t_ref, dx_ref, dg_ref, loss_ref):
        xhat, rstd = _rms_stats(x_ref[...])
        err = xhat * g_ref[...] - t_ref[...]
        dx, dg = _rms_bwd_vals(err * (1.0 / d), xhat, rstd, g_ref[...])
        dx_ref[...] = dx

        @pl.when(pl.program_id(0) == 0)
        def _():
            dg_ref[...] = jnp.zeros_like(dg_ref)
            loss_ref[...] = jnp.zeros_like(loss_ref)

        dg_ref[...] += dg
        part = jnp.sum(jnp.sum(err * err, axis=1, keepdims=True), axis=0, keepdims=True) * (0.5 / d)
        loss_ref[...] += jnp.broadcast_to(part, (1, LANE))

    row = pl.BlockSpec((tm, d), lambda i: (i, 0))
    vec = pl.BlockSpec((1, d), lambda i: (0, 0))
    return pl.pallas_call(
        body, grid=(t // tm,), in_specs=[row, vec, row],
        out_specs=[row, vec, pl.BlockSpec((1, LANE), lambda i: (0, 0))],
        out_shape=[jax.ShapeDtypeStruct((t, d), F32), jax.ShapeDtypeStruct((1, d), F32),
                   jax.ShapeDtypeStruct((1, LANE), F32)],
        compiler_params=_cp(("arbitrary",)), name=name)(x, g.reshape(1, d), target)


def _adamw(w, g, m, v, *, name):
    rows, cols = w.shape
    fits = [c for c in (1024, 704, 512, 352, 256, 128, 64, 32, 16, 8) if c * cols * 4 * 14 <= VMEM_LIMIT // 2]
    tr = _pick(rows, fits)
    c1 = 1.0 / (1.0 - ADAM_B1 ** ADAM_STEP)
    c2 = 1.0 / (1.0 - ADAM_B2 ** ADAM_STEP)

    def body(w_ref, g_ref, m_ref, v_ref, d_ref, nm_ref, nv_ref):
        g = g_ref[...]
        nm = ADAM_B1 * m_ref[...] + (1.0 - ADAM_B1) * g
        nv = ADAM_B2 * v_ref[...] + (1.0 - ADAM_B2) * (g * g)
        nm_ref[...] = nm
        nv_ref[...] = nv
        d_ref[...] = -ADAM_LR * ((nm * c1) / (jnp.sqrt(nv * c2) + ADAM_EPS) + ADAM_WD * w_ref[...])

    blk = pl.BlockSpec((tr, cols), lambda i: (i, 0))
    sds = jax.ShapeDtypeStruct((rows, cols), F32)
    return pl.pallas_call(
        body, grid=(rows // tr,), in_specs=[blk] * 4, out_specs=[blk] * 3, out_shape=[sds] * 3,
        compiler_params=_cp(("parallel",)), name=name)(w, g, m, v)


MESH_ID = pl.DeviceIdType.MESH
HBM_SPEC = pl.BlockSpec(memory_space=pl.ANY)
OTHER_CHIPS = ((1, 0), (0, 1), (1, 1))


def _at_slot(ref, nl, slot):
    return ref.at[(slice(None),) * nl + (slot,)]


def _slotted(shape, nl, slots):
    return tuple(shape[:nl]) + (slots,) + tuple(shape[nl:])


def _flip(v, f):
    return 1 - v if f else v


def _comm_call(body, n, out_shapes, n_remote, args, name):
    return pl.pallas_call(
        body, out_shape=out_shapes, in_specs=[HBM_SPEC] * len(args), out_specs=[HBM_SPEC] * len(out_shapes),
        scratch_shapes=[pltpu.SemaphoreType.DMA((n * n_remote,)), pltpu.SemaphoreType.DMA((n * n_remote,)),
                        pltpu.SemaphoreType.DMA((n * 4,))],
        compiler_params=pltpu.CompilerParams(has_side_effects=True), name=name)(*args)


def _gather(xs, nls, *, name):
    n = len(xs)

    def body(*refs):
        start, forward, finish = _gather_parts(refs[:n], refs[n:2 * n], nls, *refs[2 * n:])
        start()
        forward()
        finish()

    return _comm_call(body, n, _gather_shapes(xs, nls), 7, xs, name)


def _gather_shapes(xs, nls):
    return [jax.ShapeDtypeStruct(_slotted(v.shape, nl, N_DEV), v.dtype) for v, nl in zip(xs, nls)]


def _comm_sems(n, n_remote):
    return [pltpu.SemaphoreType.DMA((n * n_remote,)), pltpu.SemaphoreType.DMA((n * n_remote,)),
            pltpu.SemaphoreType.DMA((n * 4,))]


def _gather_parts(x_refs, o_refs, nls, send_sems, recv_sems, local_sems):
    n = len(x_refs)
    x, y, c = lax.axis_index("x"), lax.axis_index("y"), lax.axis_index("c")
    me, sibling = (x, y, c), (x, y, 1 - c)
    chips = [(_flip(x, fx), _flip(y, fy)) for fx, fy in OTHER_CHIPS]

    def copy(a, k, block, to, src=None):
        dst = _at_slot(o_refs[a], nls[a], 4 * block[0] + 2 * block[1] + block[2])
        return pltpu.make_async_remote_copy(
            src_ref=dst if src is None else src, dst_ref=dst, send_sem=send_sems.at[a * 7 + k],
            recv_sem=recv_sems.at[a * 7 + k], device_id=to, device_id_type=MESH_ID)

    def mine(a):
        return pltpu.make_async_copy(x_refs[a], _at_slot(o_refs[a], nls[a], 4 * x + 2 * y + c), local_sems.at[a])

    def first(a):
        return ([copy(a, 0, me, sibling, src=x_refs[a])]
                + [copy(a, 1 + j, me, (*chip, c), src=x_refs[a]) for j, chip in enumerate(chips)])

    def start():
        for a in range(n):
            mine(a).start()
            for cp in first(a):
                cp.start()

    def forward():
        for j, chip in enumerate(chips):
            for a in range(n):
                copy(a, 1 + j, (*chip, c), me).wait_recv()
                copy(a, 4 + j, (*chip, c), sibling).start()

    def finish():
        for a in range(n):
            copy(a, 0, sibling, me).wait_recv()
            for j, chip in enumerate(chips):
                copy(a, 4 + j, (*chip, 1 - c), me).wait_recv()
        for a in range(n):
            for cp in first(a):
                cp.wait_send()
            for j, chip in enumerate(chips):
                copy(a, 4 + j, (*chip, c), sibling).wait_send()
        for a in range(n):
            mine(a).wait()

    return start, forward, finish


ALL_FLIPS = ((0, 0, 1), (0, 1, 0), (0, 1, 1), (1, 0, 0), (1, 0, 1), (1, 1, 0), (1, 1, 1))


def _all_to_all_parts(g_refs, r_refs, nls, send_sems, recv_sems, local_sems):
    del local_sems
    n = len(g_refs)
    x, y, c = lax.axis_index("x"), lax.axis_index("y"), lax.axis_index("c")

    def copies():
        out = []
        for a in range(n):
            for k, (fx, fy, fc) in enumerate(ALL_FLIPS):
                p = (_flip(x, fx), _flip(y, fy), _flip(c, fc))
                out.append(pltpu.make_async_remote_copy(
                    src_ref=_at_slot(g_refs[a], nls[a], 4 * p[0] + 2 * p[1] + p[2]), dst_ref=_at_slot(r_refs[a], nls[a], k),
                    send_sem=send_sems.at[a * 7 + k], recv_sem=recv_sems.at[a * 7 + k], device_id=p,
                    device_id_type=MESH_ID))
        return out

    def start():
        for cp in copies():
            cp.start()

    def finish():
        cps = copies()
        for cp in cps:
            cp.wait_recv()
        for cp in cps:
            cp.wait_send()

    return start, lambda: None, finish


def _all_to_all_shapes(gs, nls):
    return [jax.ShapeDtypeStruct(_slotted(v.shape[:nl] + v.shape[nl + 1:], nl, 7), v.dtype) for v, nl in zip(gs, nls)]


def _scatter_pair(gs, nls, *, name):
    n = len(gs)

    def body(*refs):
        g_refs, got_refs, (send_sems, recv_sems, _) = refs[:n], refs[n:2 * n], refs[2 * n:]
        x, y, c = lax.axis_index("x"), lax.axis_index("y"), lax.axis_index("c")
        remote = []
        for a in range(n):
            for q in range(4):
                rc = pltpu.make_async_remote_copy(
                    src_ref=_at_slot(g_refs[a], nls[a], 2 * q + 1 - c), dst_ref=_at_slot(got_refs[a], nls[a], q),
                    send_sem=send_sems.at[a * 4 + q], recv_sem=recv_sems.at[a * 4 + q], device_id=(x, y, 1 - c),
                    device_id_type=MESH_ID)
                rc.start()
                remote.append(rc)
        for rc in remote:
            rc.wait_recv()
        for rc in remote:
            rc.wait_send()

    outs = [jax.ShapeDtypeStruct(_slotted(v.shape[:nl] + v.shape[nl + 1:], nl, 4), v.dtype) for v, nl in zip(gs, nls)]
    return _comm_call(body, n, outs, 4, gs, name)


def _scatter_chips(ps, nls, *, name):
    n = len(ps)

    def body(*refs):
        p_refs, r_refs, (send_sems, recv_sems, _) = refs[:n], refs[n:2 * n], refs[2 * n:]
        x, y, c = lax.axis_index("x"), lax.axis_index("y"), lax.axis_index("c")
        remote = []
        for a in range(n):
            for k, (fx, fy) in enumerate(OTHER_CHIPS):
                tx, ty = _flip(x, fx), _flip(y, fy)
                rc = pltpu.make_async_remote_copy(
                    src_ref=_at_slot(p_refs[a], nls[a], 2 * tx + ty), dst_ref=_at_slot(r_refs[a], nls[a], k),
                    send_sem=send_sems.at[a * 3 + k], recv_sem=recv_sems.at[a * 3 + k], device_id=(tx, ty, c),
                    device_id_type=MESH_ID)
                rc.start()
                remote.append(rc)
        for rc in remote:
            rc.wait_recv()
        for rc in remote:
            rc.wait_send()

    outs = [jax.ShapeDtypeStruct(_slotted(v.shape[:nl] + v.shape[nl + 1:], nl, 3), v.dtype) for v, nl in zip(ps, nls)]
    return _comm_call(body, n, outs, 3, ps, name)


def _pair_add(g, got, core, *, name):
    rows, cols = g.shape[-2:]
    lf = math.prod(got.shape[:-3])
    tr = _pick(rows, (1024, 512, 352, 256, 128))

    def body(core_ref, g_ref, got_ref, o_ref):
        o_ref[...] = (g_ref[...].astype(F32) + got_ref[...].astype(F32)).astype(BF16)

    blk = pl.BlockSpec((None, None, tr, cols), lambda i, q, j, core_ref: (i, q, j, 0))
    out = pl.pallas_call(
        body, grid_spec=pltpu.PrefetchScalarGridSpec(
            num_scalar_prefetch=1, grid=(lf, 4, rows // tr),
            in_specs=[pl.BlockSpec((None, None, None, tr, cols), lambda i, q, j, core_ref: (i, q, core_ref[0], j, 0)),
                      blk], out_specs=blk),
        out_shape=jax.ShapeDtypeStruct((lf, 4, rows, cols), BF16),
        compiler_params=_cp(("parallel", "parallel", "parallel")), name=name)(
            core, g.reshape(lf, 4, 2, rows, cols), got.reshape(lf, 4, rows, cols))
    return out.reshape(got.shape)


def _sum_adamw(p, r, own, w, m, v, layer, prev, *, name):
    shape = w.shape[1:]
    rows, cols = shape[-2:]
    lf = math.prod(shape[:-2])
    np_, nk = p.shape[-3], r.shape[-3]
    fits = [c for c in (1024, 512, 352, 256, 128, 64, 32, 16) if c * cols * (7 * 4 + (nk + 1) * 2) * 2 <= VMEM_LIMIT // 2]
    tr = _pick(rows, fits)
    c1 = 1.0 / (1.0 - ADAM_B1 ** ADAM_STEP)
    c2 = 1.0 / (1.0 - ADAM_B2 ** ADAM_STEP)

    def body(own_ref, p_ref, r_ref, w_ref, m_ref, v_ref, *rest):
        g_ref, d_ref, nm_ref, nv_ref = rest[-4:]
        g = p_ref[...].astype(F32)
        for k in range(nk):
            g = g + r_ref[k].astype(F32)
        g_ref[...] = g
        nm = ADAM_B1 * m_ref[...] + (1.0 - ADAM_B1) * g
        nv = ADAM_B2 * v_ref[...] + (1.0 - ADAM_B2) * (g * g)
        nm_ref[...] = nm
        nv_ref[...] = nv
        d_ref[...] = -ADAM_LR * ((nm * c1) / (jnp.sqrt(nv * c2) + ADAM_EPS) + ADAM_WD * w_ref[...])

    wblk = pl.BlockSpec((None, None, tr, cols), lambda i, j, own_ref: (layer, i, j, 0))
    full = (w.shape[0], lf, rows, cols)
    sds = jax.ShapeDtypeStruct(full, F32)
    prev = [] if prev is None else [a.reshape(full) for a in prev]
    outs = pl.pallas_call(
        body, grid_spec=pltpu.PrefetchScalarGridSpec(
            num_scalar_prefetch=1, grid=(lf, rows // tr),
            in_specs=[pl.BlockSpec((None, None, tr, cols), lambda i, j, own_ref: (i, own_ref[0], j, 0)),
                      pl.BlockSpec((None, nk, tr, cols), lambda i, j, own_ref: (i, 0, j, 0))] + [wblk] * 3
            + [HBM_SPEC] * len(prev),
            out_specs=[wblk] * 4),
        out_shape=[sds] * 4, input_output_aliases={6 + i: i for i in range(len(prev))},
        compiler_params=_cp(("parallel", "parallel")), name=name)(
            own, p.reshape(lf, np_, rows, cols), r.reshape(lf, nk, rows, cols), w.reshape(full), m.reshape(full),
            v.reshape(full), *prev)
    return [o.reshape(w.shape) for o in outs]


def _sum_slots(x, *, name):
    nd, rows, cols = x.shape
    tr = _pick(rows, (512, 256, 128, 64, 32, 16, 8))

    def body(x_ref, o_ref):
        acc = x_ref[0].astype(F32)
        for j in range(1, nd):
            acc = acc + x_ref[j].astype(F32)
        o_ref[...] = acc

    return pl.pallas_call(
        body, grid=(rows // tr,), in_specs=[pl.BlockSpec((nd, tr, cols), lambda i: (0, i, 0))],
        out_specs=pl.BlockSpec((tr, cols), lambda i: (i, 0)), out_shape=jax.ShapeDtypeStruct((rows, cols), F32),
        compiler_params=_cp(("parallel",)), name=name)(x)


def _pad_rows(a, mult=8):
    r = (-a.shape[0]) % mult
    return jnp.pad(a, ((0, r), (0, 0))) if r else a


def _flat128(a):
    f = a.reshape(-1)
    return jnp.pad(f, (0, (-f.shape[0]) % LANE)).reshape(-1, LANE)


def _unshard(gathered, shape, axis):
    g = gathered.reshape((N_DEV,) + tuple(shape))
    g = jnp.moveaxis(g, 0, axis)
    full = list(shape)
    full[axis] *= N_DEV
    return g.reshape(full)


def _col_shards(full):
    rows, cols = full.shape
    return jnp.moveaxis(full.reshape(rows, N_DEV, cols // N_DEV), 1, 0)


BIG = (("ffn_w_gate", 2), ("ffn_w_up", 2), ("ffn_w_down", 2), ("w_in", 1), ("w_branch", 2), ("w_out", 1))


def kernel(x, ffn_norm, ffn_w_gate, ffn_w_up, ffn_w_down, mix_norm, w_in, b_gate, pool_w, pool_scale, dn_conv, dn_A_log, dn_dt_bias, dn_out_norm, w_branch, w_out, final_norm, loss_target, m_ffn_norm, m_ffn_w_gate, m_ffn_w_up, m_ffn_w_down, m_mix_norm, m_w_in, m_b_gate, m_pool_w, m_pool_scale, m_dn_conv, m_dn_A_log, m_dn_dt_bias, m_dn_out_norm, m_w_branch, m_w_out, m_final_norm, v_ffn_norm, v_ffn_w_gate, v_ffn_w_up, v_ffn_w_down, v_mix_norm, v_w_in, v_b_gate, v_pool_w, v_pool_scale, v_dn_conv, v_dn_A_log, v_dn_dt_bias, v_dn_out_norm, v_w_branch, v_w_out, v_final_norm):
    wts = dict(ffn_norm=ffn_norm, ffn_w_gate=ffn_w_gate, ffn_w_up=ffn_w_up, ffn_w_down=ffn_w_down, mix_norm=mix_norm,
               w_in=w_in, b_gate=b_gate, pool_w=pool_w, pool_scale=pool_scale, dn_conv=dn_conv, dn_A_log=dn_A_log,
               dn_dt_bias=dn_dt_bias, dn_out_norm=dn_out_norm, w_branch=w_branch, w_out=w_out, final_norm=final_norm)
    mom = dict(ffn_norm=m_ffn_norm, ffn_w_gate=m_ffn_w_gate, ffn_w_up=m_ffn_w_up, ffn_w_down=m_ffn_w_down,
               mix_norm=m_mix_norm, w_in=m_w_in, b_gate=m_b_gate, pool_w=m_pool_w, pool_scale=m_pool_scale,
               dn_conv=m_dn_conv, dn_A_log=m_dn_A_log, dn_dt_bias=m_dn_dt_bias, dn_out_norm=m_dn_out_norm,
               w_branch=m_w_branch, w_out=m_w_out, final_norm=m_final_norm)
    var = dict(ffn_norm=v_ffn_norm, ffn_w_gate=v_ffn_w_gate, ffn_w_up=v_ffn_w_up, ffn_w_down=v_ffn_w_down,
               mix_norm=v_mix_norm, w_in=v_w_in, b_gate=v_b_gate, pool_w=v_pool_w, pool_scale=v_pool_scale,
               dn_conv=v_dn_conv, dn_A_log=v_dn_A_log, dn_dt_bias=v_dn_dt_bias, dn_out_norm=v_dn_out_norm,
               w_branch=v_w_branch, w_out=v_w_out, final_norm=v_final_norm)
    nb, s, d = x.shape
    t = nb * s
    me = 4 * lax.axis_index("x") + 2 * lax.axis_index("y") + lax.axis_index("c")

    big = [n for n, _ in BIG]
    nls = [nl - 1 for _, nl in BIG]
    shards = lambda l: [wts[n][l].astype(BF16) for n in big]
    small_sh = jnp.concatenate([_flat128(ffn_norm), _flat128(dn_conv)], axis=0)
    ffn3 = big[:3]
    *pre0, small_g = _gather([wts[n][0, 0].astype(BF16) for n in ffn3] + [small_sh], [0] * 4, name="gather_weights")
    rest0 = [wts[n][0, 1].astype(BF16) for n in ffn3] + [wts[n][0].astype(BF16) for n in big[3:]]
    rest0_nls = [0] * 3 + nls[3:]
    full = [None] * DEPTH

    def mixer_weights(l):
        w_in_full = jnp.moveaxis(full[l]["w_in"], 0, 1).reshape(d, -1)
        w_main = jnp.concatenate([w_in_full[:, :AB_LO], w_in_full[:, AB_HI:]], axis=1)
        w_ab = jnp.pad(w_in_full[:, AB_LO:AB_HI], ((0, 0), (0, LANE - (AB_HI - AB_LO))))
        wb = jnp.moveaxis(full[l]["w_branch"], 1, 2).reshape(3, BW, d)
        return w_main, w_ab, wb, full[l]["w_out"].reshape(d, d)

    nfr = ffn_norm.size // LANE
    ffn_norm_full = _unshard(small_g[:, :nfr], ffn_norm.shape, 2)
    dn_conv_full = _unshard(small_g[:, nfr:], dn_conv.shape, 2)
    pool_w_h = pool_w.astype(BF16)

    xs = x.reshape(t, d)
    saved = []
    for l in range(DEPTH):
        sv = dict(x0=xs)
        if l == 0:
            xs, a0, b0, got = _ffn_fwd(xs, ffn_norm_full[0, 0], *pre0, name="ffn_fwd_gather", gather=(rest0, rest0_nls))
            full[0] = dict(zip(ffn3, zip(pre0, got[:3])), **dict(zip(big[3:], got[3:])))
        else:
            xs, a0, b0, _ = _ffn_fwd(xs, ffn_norm_full[l, 0], full[l]["ffn_w_gate"][0], full[l]["ffn_w_up"][0],
                                     full[l]["ffn_w_down"][0], name="ffn_fwd")
        sv["ab0"] = (a0, b0)
        sv["x1"] = xs
        w_main, w_ab, wb, wo = mixer_weights(l)
        h = _rms_fwd(xs, mix_norm[l], name="mix_rms")
        proj = _mm(h, w_main, out_dtype=BF16, name="proj")
        ab = _mm(h, w_ab, name="proj_ab")
        par = jnp.pad(jnp.stack([dn_A_log[l], dn_dt_bias[l]]), ((0, 6), (0, LANE - NH)))
        gain = dn_out_norm[l].reshape(1, HD)
        psc = pool_scale[l].reshape(1, BW)
        yp = _pool_fwd(proj, pool_w_h[l], psc, nb, s, name="pool_fwd")
        yd, o_pre, states, dn_local, gat = _dn_fwd(proj, ab, dn_conv_full[l], par, gain, nb, s,
                                         name="dn_fwd" if l == DEPTH - 1 else "dn_fwd_gather",
                                         gather=(shards(l + 1), nls) if l < DEPTH - 1 else None)
        if l < DEPTH - 1:
            full[l + 1] = dict(zip(big, gat))
        ys, sb_ctr = _sb_fwd(proj, nb, s, name="sb_fwd")
        bg = b_gate[l].reshape(1, 3 * d)
        xs = _merge_fwd(xs, proj, yp, yd, ys, bg, wb, wo, name="merge_fwd")
        sv.update(x2=xs, h=h, proj=proj, ab=ab, par=par, gain=gain, psc=psc, yp=yp, yd=yd, ys=ys, sb_ctr=sb_ctr, o_pre=o_pre,
                  states=states, dn_local=dn_local, bg=bg, w_main=w_main, w_ab=w_ab, wb=wb, wo=wo)
        xs, a1, b1, _ = _ffn_fwd(xs, ffn_norm_full[l, 1], full[l]["ffn_w_gate"][1], full[l]["ffn_w_up"][1],
                                 full[l]["ffn_w_down"][1], name="ffn_fwd")
        sv["ab1"] = (a1, b1)
        saved.append(sv)

    dx, g_final, loss_row = _loss_head(xs, final_norm, loss_target.reshape(t, d), name="loss_head")
    loss = lax.psum(loss_row[0, 0], ("x", "y", "c"))

    gw = {n: [None] * DEPTH for n in ("ffn_norm", "ffn_w_gate", "ffn_w_up", "ffn_w_down", "mix_norm", "w_in", "b_gate",
                                      "pool_w", "pool_scale", "dn_conv", "dn_A_log", "dn_dt_bias", "dn_out_norm",
                                      "w_branch", "w_out")}

    me_i = me.astype(jnp.int32).reshape(1)
    updated = {n: None for n in big}
    pending = None

    def finish_layer(l, own_blocks, arrived, own_slot):
        for n, p, r in zip(big, own_blocks, arrived):
            updated[n] = _sum_adamw(p, r, own_slot, wts[n], mom[n], var[n], l, updated[n], name=f"adamw_{n}_{l}")

    def ffn_back(l, i, x_in, dy):
        dxi, dg, hb, dyh, da, db, sact = _ffn_bwd(x_in, ffn_norm_full[l, i], full[l]["ffn_w_gate"][i],
                                                  full[l]["ffn_w_up"][i], full[l]["ffn_w_down"][i],
                                                  *saved[l][f"ab{i}"], dy, name="ffn_bwd")
        return dxi, dg, (_mm_slots(hb, da, name="dw_gate_up"), _mm_slots(hb, db, name="dw_gate_up"),
                         _mm_slots(sact, dyh, name="dw_down"))

    for l in reversed(range(DEPTH)):
        sv = saved[l]
        dx, dg1, (dwg1, dwu1, dwd1) = ffn_back(l, 1, sv["x2"], dx)
        dyp, dyd, dys, dgl, merged, dxh, dbd, dbg = _merge_bwd(sv["proj"], sv["yp"], sv["yd"], sv["ys"], sv["bg"],
                                                               sv["wb"], sv["wo"], dx, name="merge_bwd")
        gw["w_out"][l] = _mm(merged, dxh, ta=True, out_dtype=BF16, name="dw_out").reshape(N_DEV, d // N_DEV, d)
        gw["w_branch"][l] = jnp.stack([_col_shards(_mm(y, dbd[n], ta=True, out_dtype=BF16, name="dw_branch"))
                                       for n, y in enumerate((sv["yp"], sv["yd"], sv["ys"]))])
        gw["b_gate"][l] = dbg.reshape(3 * d)
        du, dpw, dps = _pool_bwd(sv["proj"], pool_w_h[l], sv["psc"], dyp, nb, s, name="pool_bwd")
        gw["pool_w"][l], gw["pool_scale"][l] = dpw, dps.reshape(BW)
        own = [gw[n][pending] for n in big] if pending is not None else []
        dqr, dkr, dvr, dz, dab4, dcq, dck, dcv, dpar, dgain, arrived_ffn = _dn_bwd(
            sv["proj"], sv["ab"], dn_conv_full[l], sv["par"], sv["gain"], sv["o_pre"], sv["states"], sv["dn_local"],
            dyd, nb, s, name="dn_bwd_scatter" if own else "dn_bwd", scatter=(own[:3], nls[:3]) if own else None)
        gw["dn_conv"][l] = jnp.concatenate([dcq, dck, dcv], axis=1)
        gw["dn_A_log"][l], gw["dn_dt_bias"][l], gw["dn_out_norm"][l] = dpar[:, 0, 0], dpar[:, 1, 0], dgain.reshape(HD)
        dsq, dsk, dsv, arrived_rest = _sb_bwd(sv["proj"], sv["sb_ctr"], dys, nb, s,
                                              name="sb_bwd_scatter" if own else "sb_bwd",
                                              scatter=(own[3:], nls[3:]) if own else None)
        if own:
            finish_layer(pending, own, arrived_ffn + arrived_rest, me_i)
        dab = _sum_heads(dab4, name="sum_heads")
        dproj = jnp.concatenate([du.astype(BF16), dqr.astype(BF16), dkr.astype(BF16), dvr.astype(BF16),
                                 dz.astype(BF16), dsq.astype(BF16), dsk.astype(BF16), dsv.astype(BF16), dgl], axis=1)
        dw_main = _mm(sv["h"], dproj, ta=True, out_dtype=BF16, name="dw_in")
        dw_ab = _mm(sv["h"], dab, ta=True, out_dtype=BF16, name="dw_ab")
        gw["w_in"][l] = _col_shards(jnp.concatenate([dw_main[:, :AB_LO], dw_ab[:, :AB_HI - AB_LO],
                                                     dw_main[:, AB_LO:]], axis=1))
        dh_main = _mm(dproj, sv["w_main"], tb=True, name="dh_mix")
        dh_ab = _mm(dab, sv["w_ab"], tb=True, name="dh_mix_ab")
        dx, dgm = _rms_bwd(sv["x1"], mix_norm[l], dh_main, dh_ab, dx, name="mix_rms_bwd")
        gw["mix_norm"][l] = dgm.reshape(d)
        dx, dg0, (dwg0, dwu0, dwd0) = ffn_back(l, 0, sv["x0"], dx)
        gw["ffn_norm"][l] = jnp.stack([dg0.reshape(d), dg1.reshape(d)])
        gw["ffn_w_gate"][l] = jnp.stack([dwg0, dwg1])
        gw["ffn_w_up"][l] = jnp.stack([dwu0, dwu1])
        gw["ffn_w_down"][l] = jnp.stack([dwd0, dwd1])
        pending = l
    grad_x = dx.reshape(nb, s, d)

    core = lax.axis_index("c").astype(jnp.int32).reshape(1)
    chip = (2 * lax.axis_index("x") + lax.axis_index("y")).astype(jnp.int32).reshape(1)
    last = [gw[n][0] for n in big]
    got = _scatter_pair(last, nls, name="scatter_grads_pair")
    chip_sums = [_pair_add(g, b, core, name="add_pair_" + n) for n, g, b in zip(big, last, got)]
    finish_layer(0, chip_sums, _scatter_chips(chip_sums, nls, name="scatter_grads_chips"), chip)
    grads, delta, new_m, new_v = ({n: updated[n][i] for n in big} for i in range(4))
    gw = {n: jnp.stack(v) for n, v in gw.items() if n not in big}
    gw["final_norm"] = g_final.reshape(d)

    small = ("ffn_norm", "mix_norm", "b_gate", "pool_w", "pool_scale", "dn_conv", "dn_A_log", "dn_dt_bias",
             "dn_out_norm", "final_norm")
    sp = _pad_rows(jnp.concatenate([_flat128(gw[n]) for n in small], axis=0))
    ssum = _sum_slots(_gather([sp], [0], name="gather_small_grads")[0], name="sum_small_grads")
    off = 0
    for n in small:
        r = -(-gw[n].size // LANE)
        g = ssum[off:off + r].reshape(-1)[:gw[n].size].reshape(gw[n].shape)
        off += r
        if n in ("ffn_norm", "dn_conv"):
            w = wts[n].shape[2]
            g = lax.dynamic_slice_in_dim(g, me * w, w, axis=2)
        grads[n] = g

    pk = lambda src: _pad_rows(jnp.concatenate([_flat128(src[n]) for n in small], axis=0))
    dl, nm, nv = _adamw(pk(wts), pk(grads), pk(mom), pk(var), name="adamw_small")
    off = 0
    for n in small:
        r = -(-wts[n].size // LANE)
        for dst, src in ((delta, dl), (new_m, nm), (new_v, nv)):
            dst[n] = src[off:off + r].reshape(-1)[:wts[n].size].reshape(wts[n].shape)
        off += r

    order = ("ffn_norm", "ffn_w_gate", "ffn_w_up", "ffn_w_down", "mix_norm", "w_in", "b_gate", "pool_w", "pool_scale",
             "dn_conv", "dn_A_log", "dn_dt_bias", "dn_out_norm", "w_branch", "w_out", "final_norm")
    return (loss, grad_x, *[grads[n] for n in order], *[delta[n] for n in order], *[new_m[n] for n in order],
            *[new_v[n] for n in order])
```

```python
import functools
import math

import jax
import jax.numpy as jnp
from jax import lax
from jax.experimental import pallas as pl
from jax.experimental.pallas import tpu as pltpu

F32, BF16 = jnp.float32, jnp.bfloat16
D_MODEL, D_FF, DEPTH = 1024, 2816, 4
BW = 512
HD = 128
NH = 4
DN_CHUNK = 64
EPS = 1e-6
N_DEV = 8
LANE = 128
CB_POOL, CB_DNQ, CB_DNK, CB_DNV, CB_DNZ, CB_SBQ, CB_SBK, CB_SBV = 0, 4, 8, 12, 16, 20, 24, 28
CB_GATE = 4
P_MAIN = 7168
AB_LO, AB_HI = 2560, 2568
ADAM_LR, ADAM_B1, ADAM_B2, ADAM_EPS, ADAM_WD, ADAM_STEP = 0.001, 0.9, 0.999, 1e-08, 0.01, 10
VMEM_LIMIT = 56 * 1024 * 1024
HIGHEST = lax.Precision.HIGHEST
NT_DIMS = (((1,), (1,)), ((), ()))
TN_DIMS = (((0,), (0,)), ((), ()))
NN_DIMS = (((1,), (0,)), ((), ()))


def _cp(dims=None, vmem=VMEM_LIMIT):
    return pltpu.CompilerParams(dimension_semantics=dims, vmem_limit_bytes=vmem)


def _pick(n, cands):
    for c in cands:
        if n % c == 0:
            return c
    return n


def _bdot(a, b, dims=NN_DIMS):
    return lax.dot_general(a.astype(BF16), b.astype(BF16), dims, preferred_element_type=F32)


def _hdot(a, b, dims=NN_DIMS):
    return lax.dot_general(a, b, dims, precision=lax.Precision.HIGH, preferred_element_type=F32)


def _split_dot(x, m01):
    hi = x.astype(BF16)
    lo = (x - hi.astype(F32)).astype(BF16)
    return (lax.dot_general(hi, m01, NN_DIMS, preferred_element_type=F32)
            + lax.dot_general(lo, m01, NN_DIMS, preferred_element_type=F32))


def _sigmoid(x):
    return 1.0 / (1.0 + jnp.exp(-x))


def _log_sigmoid(x):
    return jnp.minimum(x, 0.0) - jnp.log1p(jnp.exp(-jnp.abs(x)))


def _softplus(x):
    return jnp.maximum(x, 0.0) + jnp.log1p(jnp.exp(-jnp.abs(x)))


def _shift_down(x, k):
    r = lax.broadcasted_iota(jnp.int32, x.shape, 0)
    return jnp.where(r >= k, pltpu.roll(x, k, 0), 0.0)


def _shift_up(x, k):
    n = x.shape[0]
    r = lax.broadcasted_iota(jnp.int32, x.shape, 0)
    return jnp.where(r < n - k, pltpu.roll(x, n - k, 0), 0.0)


def _mm(a, b, *, ta=False, tb=False, out_dtype=F32, name):
    (kk, m) = a.shape if ta else a.shape[::-1]
    (k2, n) = b.shape[::-1] if tb else b.shape
    assert kk == k2, (a.shape, b.shape, ta, tb)
    bm = _pick(m, (1024, 512, 256, 128))
    bn = _pick(n, (1024, 1408, 512, 256, 128))
    bk = _pick(kk, (1024, 512, 256, 128))
    nk = kk // bk
    dims = (((0 if ta else 1,), (1 if tb else 0,)), ((), ()))

    def body(a_ref, b_ref, o_ref, acc_ref):
        k = pl.program_id(2)

        @pl.when(k == 0)
        def _():
            acc_ref[...] = jnp.zeros_like(acc_ref)

        acc_ref[...] += lax.dot_general(a_ref[...].astype(BF16), b_ref[...].astype(BF16), dims,
                                        preferred_element_type=F32)

        @pl.when(k == nk - 1)
        def _():
            o_ref[...] = acc_ref[...].astype(out_dtype)

    a_spec = (pl.BlockSpec((bk, bm), lambda i, j, k: (k, i)) if ta else pl.BlockSpec((bm, bk), lambda i, j, k: (i, k)))
    b_spec = (pl.BlockSpec((bn, bk), lambda i, j, k: (j, k)) if tb else pl.BlockSpec((bk, bn), lambda i, j, k: (k, j)))
    return pl.pallas_call(
        body, grid=(m // bm, n // bn, nk), in_specs=[a_spec, b_spec],
        out_specs=pl.BlockSpec((bm, bn), lambda i, j, k: (i, j)),
        out_shape=jax.ShapeDtypeStruct((m, n), out_dtype),
        scratch_shapes=[pltpu.VMEM((bm, bn), F32)],
        compiler_params=_cp(("parallel", "parallel", "arbitrary")), name=name)(a, b)


def _mm_slots(a, b, *, name):
    a3, b3 = a.ndim == 3, b.ndim == 3
    ns = a.shape[0] if a3 else b.shape[0]
    m, t = a.shape[-2:]
    n = b.shape[-1]
    bk = _pick(t, (1024, 512, 256, 128))
    nk = t // bk

    def body(a_ref, b_ref, o_ref, acc_ref):
        k = pl.program_id(0)

        @pl.when(k == 0)
        def _():
            acc_ref[...] = jnp.zeros_like(acc_ref)

        for s in range(ns):
            acc_ref[s] += _bdot(a_ref[s] if a3 else a_ref[...], b_ref[s] if b3 else b_ref[...])

        @pl.when(k == nk - 1)
        def _():
            o_ref[...] = acc_ref[...].astype(BF16)

    a_spec = pl.BlockSpec((ns, m, bk), lambda k: (0, 0, k)) if a3 else pl.BlockSpec((m, bk), lambda k: (0, k))
    b_spec = pl.BlockSpec((ns, bk, n), lambda k: (0, k, 0)) if b3 else pl.BlockSpec((bk, n), lambda k: (k, 0))
    return pl.pallas_call(
        body, grid=(nk,), in_specs=[a_spec, b_spec], out_specs=pl.BlockSpec((ns, m, n), lambda k: (0, 0, 0)),
        out_shape=jax.ShapeDtypeStruct((ns, m, n), BF16), scratch_shapes=[pltpu.VMEM((ns, m, n), F32)],
        compiler_params=_cp(("arbitrary",)), name=name)(a, b)


def _rms_stats(x):
    rstd = lax.rsqrt(jnp.mean(x * x, axis=-1, keepdims=True) + EPS)
    return x * rstd, rstd


def _rms_bwd_vals(dh, xhat, rstd, g):
    dxh = dh * g
    dx = rstd * (dxh - xhat * jnp.mean(dxh * xhat, axis=-1, keepdims=True))
    return dx, jnp.sum(dh * xhat, axis=0, keepdims=True)


def _rms_fwd(x, g, *, name):
    t, d = x.shape
    tm = _pick(t, (512, 256, 128))

    def body(x_ref, g_ref, h_ref):
        xhat, _ = _rms_stats(x_ref[...])
        h_ref[...] = (xhat * g_ref[...]).astype(BF16)

    return pl.pallas_call(
        body, grid=(t // tm,),
        in_specs=[pl.BlockSpec((tm, d), lambda i: (i, 0)), pl.BlockSpec((1, d), lambda i: (0, 0))],
        out_specs=pl.BlockSpec((tm, d), lambda i: (i, 0)), out_shape=jax.ShapeDtypeStruct((t, d), BF16),
        compiler_params=_cp(("parallel",)), name=name)(x, g.reshape(1, d))


def _rms_bwd(x, g, dh_a, dh_b, dres, *, name):
    t, d = x.shape
    tm = _pick(t, (512, 256, 128))

    def body(x_ref, g_ref, dha_ref, dhb_ref, dres_ref, dx_ref, dg_ref):
        xhat, rstd = _rms_stats(x_ref[...])
        dx, dg = _rms_bwd_vals(dha_ref[...] + dhb_ref[...], xhat, rstd, g_ref[...])
        dx_ref[...] = dres_ref[...] + dx

        @pl.when(pl.program_id(0) == 0)
        def _():
            dg_ref[...] = jnp.zeros_like(dg_ref)

        dg_ref[...] += dg

    row = pl.BlockSpec((tm, d), lambda i: (i, 0))
    vec = pl.BlockSpec((1, d), lambda i: (0, 0))
    return pl.pallas_call(
        body, grid=(t // tm,), in_specs=[row, vec, row, row, row], out_specs=[row, vec],
        out_shape=[jax.ShapeDtypeStruct((t, d), F32), jax.ShapeDtypeStruct((1, d), F32)],
        compiler_params=_cp(("arbitrary",)), name=name)(x, g.reshape(1, d), dh_a, dh_b, dres)


FFN_TM = 512
FFN_SLOTS = 2


def _ffn_fwd(x, g, wg, wu, wd, *, name, gather=None):
    t, d = x.shape
    nf, _, fc = wg.shape
    tm = _pick(t, (FFN_TM, 256, 128))
    gx, gnl = gather if gather else ([], [])

    def body(x_ref, g_ref, wg_ref, wu_ref, wd_ref, o_ref, a_ref, b_ref, h_ref, acc_ref):
        j = pl.program_id(1)

        @pl.when(j == 0)
        def _():
            xhat, _ = _rms_stats(x_ref[...])
            h_ref[...] = (xhat * g_ref[...]).astype(BF16)
            acc_ref[...] = jnp.zeros_like(acc_ref)

        h = h_ref[...]
        part = jnp.zeros((tm, d), F32)
        for q in range(FFN_SLOTS):
            a = _bdot(h, wg_ref[q])
            b = _bdot(h, wu_ref[q])
            a_ref[q] = a.astype(BF16)
            b_ref[q] = b.astype(BF16)
            part = part + _bdot(a * _sigmoid(a) * b, wd_ref[q])
        acc_ref[...] += part

        @pl.when(j == nf // FFN_SLOTS - 1)
        def _():
            o_ref[...] = x_ref[...] + 0.5 * acc_ref[...]

    row = pl.BlockSpec((tm, d), lambda i, j: (i, 0))
    grid = (t // tm, nf // FFN_SLOTS)
    exchange = (_gather_parts, len(gx), gnl) if gx else None
    res = pl.pallas_call(
        _with_exchange(body, 5, 3, 2, exchange, grid), grid=grid,
        in_specs=[row, pl.BlockSpec((1, d), lambda i, j: (0, 0)),
                  pl.BlockSpec((FFN_SLOTS, d, fc), lambda i, j: (j, 0, 0)),
                  pl.BlockSpec((FFN_SLOTS, d, fc), lambda i, j: (j, 0, 0)),
                  pl.BlockSpec((FFN_SLOTS, fc, d), lambda i, j: (j, 0, 0))] + [HBM_SPEC] * len(gx),
        out_specs=[row, pl.BlockSpec((FFN_SLOTS, tm, fc), lambda i, j: (j, i, 0)),
                   pl.BlockSpec((FFN_SLOTS, tm, fc), lambda i, j: (j, i, 0))] + [HBM_SPEC] * len(gx),
        out_shape=[jax.ShapeDtypeStruct((t, d), F32), jax.ShapeDtypeStruct((nf, t, fc), BF16),
                   jax.ShapeDtypeStruct((nf, t, fc), BF16)] + _gather_shapes(gx, gnl),
        scratch_shapes=[pltpu.VMEM((tm, d), BF16), pltpu.VMEM((tm, d), F32)] + (_comm_sems(len(gx), 7) if gx else []),
        compiler_params=_cp(("arbitrary", "arbitrary")), name=name)(x, g.reshape(1, d), wg, wu, wd, *gx)
    return res[0], res[1], res[2], list(res[3:])


def _ffn_bwd(x, g, wg, wu, wd, a_pre, b_pre, dy, *, name):
    t, d = x.shape
    nf, _, fc = wg.shape
    tm = _pick(t, (FFN_TM, 256, 128))

    def body(x_ref, g_ref, wg_ref, wu_ref, wd_ref, a_ref, b_ref, dy_ref,
             dx_ref, dg_ref, ht_ref, dyh_ref, da_ref, db_ref, st_ref, acc_ref):
        i, j = pl.program_id(0), pl.program_id(1)

        @pl.when(j == 0)
        def _():
            xhat, _ = _rms_stats(x_ref[...])
            ht_ref[...] = (xhat * g_ref[...]).T.astype(BF16)
            dyh_ref[...] = (0.5 * dy_ref[...]).astype(BF16)
            acc_ref[...] = jnp.zeros_like(acc_ref)

        part = jnp.zeros((tm, d), F32)
        for q in range(FFN_SLOTS):
            a = a_ref[q].astype(F32)
            b = b_ref[q].astype(F32)
            sg = _sigmoid(a)
            silu = a * sg
            st_ref[q] = (silu * b).T.astype(BF16)
            ds = _bdot(dyh_ref[...], wd_ref[q], NT_DIMS)
            da = (ds * b * (sg * (1.0 + a * (1.0 - sg)))).astype(BF16)
            db = (ds * silu).astype(BF16)
            da_ref[q] = da
            db_ref[q] = db
            part = part + _bdot(da, wg_ref[q], NT_DIMS) + _bdot(db, wu_ref[q], NT_DIMS)
        acc_ref[...] += part

        @pl.when((i == 0) & (j == 0))
        def _():
            dg_ref[...] = jnp.zeros_like(dg_ref)

        @pl.when(j == nf // FFN_SLOTS - 1)
        def _():
            xhat, rstd = _rms_stats(x_ref[...])
            dx, dg = _rms_bwd_vals(acc_ref[...], xhat, rstd, g_ref[...])
            dx_ref[...] = dy_ref[...] + dx
            dg_ref[...] += dg

    row = pl.BlockSpec((tm, d), lambda i, j: (i, 0))
    vec = pl.BlockSpec((1, d), lambda i, j: (0, 0))
    fblk = pl.BlockSpec((FFN_SLOTS, tm, fc), lambda i, j: (j, i, 0))
    return pl.pallas_call(
        body, grid=(t // tm, nf // FFN_SLOTS),
        in_specs=[row, vec, pl.BlockSpec((FFN_SLOTS, d, fc), lambda i, j: (j, 0, 0)),
                  pl.BlockSpec((FFN_SLOTS, d, fc), lambda i, j: (j, 0, 0)),
                  pl.BlockSpec((FFN_SLOTS, fc, d), lambda i, j: (j, 0, 0)), fblk, fblk, row],
        out_specs=[row, vec, pl.BlockSpec((d, tm), lambda i, j: (0, i)), row, fblk, fblk,
                   pl.BlockSpec((FFN_SLOTS, fc, tm), lambda i, j: (j, 0, i))],
        out_shape=[jax.ShapeDtypeStruct((t, d), F32), jax.ShapeDtypeStruct((1, d), F32),
                   jax.ShapeDtypeStruct((d, t), BF16), jax.ShapeDtypeStruct((t, d), BF16),
                   jax.ShapeDtypeStruct((nf, t, fc), BF16), jax.ShapeDtypeStruct((nf, t, fc), BF16),
                   jax.ShapeDtypeStruct((nf, fc, t), BF16)],
        scratch_shapes=[pltpu.VMEM((tm, d), F32)],
        compiler_params=_cp(("arbitrary", "arbitrary")), name=name)(x, g.reshape(1, d), wg, wu, wd, a_pre, b_pre, dy)


def _pool_core(u, grp):
    s = u.shape[0]
    w2 = u + _shift_down(u, 1)
    w4 = w2 + _shift_down(w2, 2)
    w8 = w4 + _shift_down(w4, 4)
    w16 = w8 + _shift_down(w8, 8)
    wsum = jnp.where(grp == 0, w2, jnp.where(grp == 1, w4, jnp.where(grp == 2, w8, w16)))
    win = jnp.left_shift(2, grp).astype(F32)
    t1 = (lax.broadcasted_iota(jnp.int32, (s, 1), 0) + 1).astype(F32)
    inv = 1.0 / jnp.minimum(t1, win)
    return wsum * inv - u, inv


def _pool_fwd(proj, pool_w, pool_scale, nb, s, *, name):
    def body(u_ref, w_ref, sc_ref, y_ref):
        pooled, _ = _pool_core(u_ref[...].astype(F32), pl.program_id(0))
        y_ref[...] = _bdot(pooled, w_ref[...]) * sc_ref[...]

    return pl.pallas_call(
        body, grid=(NH, nb),
        in_specs=[pl.BlockSpec((s, HD), lambda g, b: (b, CB_POOL + g)),
                  pl.BlockSpec((None, HD, HD), lambda g, b: (g, 0, 0)), pl.BlockSpec((1, HD), lambda g, b: (0, g))],
        out_specs=pl.BlockSpec((s, HD), lambda g, b: (b, g)),
        out_shape=jax.ShapeDtypeStruct((nb * s, BW), F32),
        compiler_params=_cp(("parallel", "parallel")), name=name)(proj, pool_w, pool_scale)


def _pool_bwd(proj, pool_w, pool_scale, dy, nb, s, *, name):
    def body(u_ref, w_ref, sc_ref, dy_ref, du_ref, dw_ref, dsc_ref):
        grp, b = pl.program_id(0), pl.program_id(1)
        pooled, inv = _pool_core(u_ref[...].astype(F32), grp)
        mixed = _bdot(pooled, w_ref[...])
        dy = dy_ref[...]
        dmixed = dy * sc_ref[...]
        dpooled = _bdot(dmixed, w_ref[...], NT_DIMS)
        r = dpooled * inv
        v2 = r + _shift_up(r, 1)
        v4 = v2 + _shift_up(v2, 2)
        v8 = v4 + _shift_up(v4, 4)
        v16 = v8 + _shift_up(v8, 8)
        vsum = jnp.where(grp == 0, v2, jnp.where(grp == 1, v4, jnp.where(grp == 2, v8, v16)))
        du_ref[...] = vsum - dpooled

        @pl.when(b == 0)
        def _():
            dw_ref[...] = jnp.zeros_like(dw_ref)
            dsc_ref[...] = jnp.zeros_like(dsc_ref)

        dw_ref[...] += _bdot(pooled, dmixed, TN_DIMS)
        dsc_ref[...] += jnp.sum(dy * mixed, axis=0, keepdims=True)

    return pl.pallas_call(
        body, grid=(NH, nb),
        in_specs=[pl.BlockSpec((s, HD), lambda g, b: (b, CB_POOL + g)),
                  pl.BlockSpec((None, HD, HD), lambda g, b: (g, 0, 0)), pl.BlockSpec((1, HD), lambda g, b: (0, g)),
                  pl.BlockSpec((s, HD), lambda g, b: (b, g))],
        out_specs=[pl.BlockSpec((s, HD), lambda g, b: (b, g)), pl.BlockSpec((None, HD, HD), lambda g, b: (g, 0, 0)),
                   pl.BlockSpec((1, HD), lambda g, b: (0, g))],
        out_shape=[jax.ShapeDtypeStruct((nb * s, BW), F32), jax.ShapeDtypeStruct((NH, HD, HD), F32),
                   jax.ShapeDtypeStruct((1, BW), F32)],
        compiler_params=_cp(("arbitrary", "arbitrary")), name=name)(proj, pool_w, pool_scale, dy)


SB_BLK = 128


SB_G = 4
SB_KG = SB_G * SB_BLK
SB_Q = 2 * SB_BLK


def _sb_block(qb, kg, q0, k0, diagonal):
    z = _bdot(qb, kg, NT_DIMS) * (HD ** -0.5)
    lsz = _log_sigmoid(z)
    if not diagonal:
        return lsz, lsz - z, None
    row = lax.broadcasted_iota(jnp.int32, z.shape, 0) + q0
    col = lax.broadcasted_iota(jnp.int32, z.shape, 1) + k0
    causal = col < row
    return lsz, jnp.where(causal, lsz - z, 0.0), causal


def _keep(causal, x):
    return x if causal is None else jnp.where(causal, x, 0.0)


def _sub(x, m):
    return x[:, m * SB_BLK:(m + 1) * SB_BLK]


def _sb_tails(lnm, after, ct):
    hi = lnm.astype(BF16)
    lo = (lnm - hi.astype(F32)).astype(BF16)
    tails = [None] * SB_G
    for m in reversed(range(SB_G)):
        tails[m] = (lax.dot_general(_sub(hi, m), after, NN_DIMS, preferred_element_type=F32)
                    + lax.dot_general(_sub(lo, m), after, NN_DIMS, preferred_element_type=F32)) + ct
        ct = ct + jnp.sum(_sub(lnm, m), axis=1, keepdims=True)
    ones = jnp.ones((8, lnm.shape[1]), BF16)
    rows = (lax.dot_general(ones, hi, NT_DIMS, preferred_element_type=F32)
            + lax.dot_general(ones, lo, NT_DIMS, preferred_element_type=F32))
    return jnp.concatenate(tails, axis=1), rows, ct


def _tri01(lower):
    r = lax.broadcasted_iota(jnp.int32, (SB_BLK, SB_BLK), 0)
    c = lax.broadcasted_iota(jnp.int32, (SB_BLK, SB_BLK), 1)
    return jnp.where((r < c) if lower else (r > c), 1.0, 0.0).astype(BF16)


def _split3(x):
    hi = x.astype(BF16)
    mid = (x - hi.astype(F32)).astype(BF16)
    lo = (x - hi.astype(F32) - mid.astype(F32)).astype(BF16)
    return hi, mid, lo


def _rows_to_cols(rows):
    eighth = jnp.full((8, LANE), 0.125, BF16)
    return sum(lax.dot_general(p, eighth, TN_DIMS, preferred_element_type=F32) for p in _split3(rows))


def _sb_fwd(proj, nb, s, *, name, gather=None):
    nq = s // SB_Q
    ng = s // SB_KG
    gx, gnl = gather if gather else ([], [])

    def body(q_ref, k_ref, v_ref, o_ref, ctr_ref):
        after = _tri01(False)

        def qblock(i, _):
            q0 = pl.multiple_of(i * SB_Q, SB_Q)
            qb = q_ref[pl.ds(q0, SB_Q), :]

            def kgroup(g, carry, diagonal):
                acc, ct, ctr = carry
                k0 = pl.multiple_of(g * SB_KG, SB_KG)
                lsz, lnm, causal = _sb_block(qb, k_ref[pl.ds(k0, SB_KG), :], q0, k0, diagonal)
                ctr_ref[i * ng + g] = ctr
                tail, rows, ct = _sb_tails(lnm, after, ct)
                w = _keep(causal, jnp.exp(lsz + tail))
                return acc + _bdot(w, v_ref[pl.ds(k0, SB_KG), :]), ct, ctr + rows

            gd = (i * SB_Q) // SB_KG
            carry = kgroup(gd, (jnp.zeros((SB_Q, HD), F32), jnp.zeros((SB_Q, 1), F32), jnp.zeros((8, SB_Q), F32)), True)
            acc, _, _ = lax.fori_loop(0, gd, lambda jj, c: kgroup(gd - 1 - jj, c, False), carry)
            o_ref[pl.ds(q0, SB_Q), :] = acc
            return 0

        lax.fori_loop(0, nq, qblock, 0)

    def col(cb):
        return pl.BlockSpec((s, HD), lambda b, h: (b, cb + h))

    exchange = (_gather_parts, len(gx), gnl) if gx else None
    res = pl.pallas_call(
        _with_exchange(body, 3, 2, 0, exchange, (nb, NH)), grid=(nb, NH),
        in_specs=[col(CB_SBQ), col(CB_SBK), col(CB_SBV)] + [HBM_SPEC] * len(gx),
        out_specs=[pl.BlockSpec((s, HD), lambda b, h: (b, h)),
                   pl.BlockSpec((None, None, nq * ng, 8, SB_Q), lambda b, h: (b, h, 0, 0, 0))] + [HBM_SPEC] * len(gx),
        out_shape=[jax.ShapeDtypeStruct((nb * s, BW), F32), jax.ShapeDtypeStruct((nb, NH, nq * ng, 8, SB_Q), F32)]
        + _gather_shapes(gx, gnl),
        scratch_shapes=_comm_sems(len(gx), 7) if gx else [],
        compiler_params=_cp(("arbitrary", "arbitrary")), name=name)(proj, proj, proj, *gx)
    return res[0], res[1], list(res[2:])


def _sb_bwd(proj, ctr, dy, nb, s, *, name, scatter=None):
    nq = s // SB_Q
    ng = s // SB_KG
    scale = HD ** -0.5
    gx, gnl = scatter if scatter else ([], [])

    def body(q_ref, k_ref, v_ref, ctr_ref, do_ref, dq_ref, dk_ref, dv_ref):
        after = _tri01(False)
        before = _tri01(True)
        dk_ref[...] = jnp.zeros_like(dk_ref)
        dv_ref[...] = jnp.zeros_like(dv_ref)

        def qblock(i, _):
            q0 = pl.multiple_of(i * SB_Q, SB_Q)
            qb = q_ref[pl.ds(q0, SB_Q), :]
            dob = do_ref[pl.ds(q0, SB_Q), :]

            def kgroup(g, carry, diagonal):
                dq, ce = carry
                k0 = pl.multiple_of(g * SB_KG, SB_KG)
                kg = k_ref[pl.ds(k0, SB_KG), :]
                vg = v_ref[pl.ds(k0, SB_KG), :]
                lsz, lnm, causal = _sb_block(qb, kg, q0, k0, diagonal)
                tail, _, _ = _sb_tails(lnm, after, _rows_to_cols(ctr_ref[i * ng + g])[:, 0:1])
                w = _keep(causal, jnp.exp(lsz + tail))
                e = _bdot(dob, vg, NT_DIMS) * w
                pres = []
                for m in range(SB_G):
                    pres.append(_split_dot(_sub(e, m), before) + ce)
                    ce = ce + jnp.sum(_sub(e, m), axis=1, keepdims=True)
                sig = jnp.exp(lsz)
                dz = _keep(causal, e * (1.0 - sig) - jnp.concatenate(pres, axis=1) * sig) * scale
                dk_ref[pl.ds(k0, SB_KG), :] += _bdot(dz, qb, TN_DIMS)
                dv_ref[pl.ds(k0, SB_KG), :] += _bdot(w, dob, TN_DIMS)
                return dq + _bdot(dz, kg), ce

            gd = (i * SB_Q) // SB_KG
            carry = lax.fori_loop(0, gd, lambda g, c: kgroup(g, c, False),
                                  (jnp.zeros((SB_Q, HD), F32), jnp.zeros((SB_Q, 1), F32)))
            dq, _ = kgroup(gd, carry, True)
            dq_ref[pl.ds(q0, SB_Q), :] = dq
            return 0

        lax.fori_loop(0, nq, qblock, 0)

    def col(cb):
        return pl.BlockSpec((s, HD), lambda b, h: (b, cb + h))

    out = pl.BlockSpec((s, HD), lambda b, h: (b, h))
    sds = jax.ShapeDtypeStruct((nb * s, BW), F32)
    exchange = (_all_to_all_parts, len(gx), gnl) if gx else None
    res = pl.pallas_call(
        _with_exchange(body, 5, 3, 0, exchange, (nb, NH)), grid=(nb, NH),
        in_specs=[col(CB_SBQ), col(CB_SBK), col(CB_SBV),
                  pl.BlockSpec((None, None, nq * ng, 8, SB_Q), lambda b, h: (b, h, 0, 0, 0)), out]
        + [HBM_SPEC] * len(gx),
        out_specs=[out, out, out] + [HBM_SPEC] * len(gx), out_shape=[sds, sds, sds] + _all_to_all_shapes(gx, gnl),
        scratch_shapes=_comm_sems(len(gx), 7) if gx else [],
        compiler_params=_cp(("arbitrary", "arbitrary")), name=name)(proj, proj, proj, ctr, dy, *gx)
    return res[0], res[1], res[2], list(res[3:])


def _make_cdot(dims, dims_da, dims_db, swap_a=False, swap_b=False):
    @jax.custom_vjp
    def f(a, b):
        return _bdot(a, b, dims)

    def fwd(a, b):
        return _bdot(a, b, dims), (a, b)

    def bwd(res, g):
        a, b = res
        da = _bdot(b, g, dims_da) if swap_a else _bdot(g, b, dims_da)
        db = _bdot(g, a, dims_db) if swap_b else _bdot(a, g, dims_db)
        return da, db

    f.defvjp(fwd, bwd)
    return f


_cdot = _make_cdot(NN_DIMS, NT_DIMS, TN_DIMS)
_cdot_nt = _make_cdot(NT_DIMS, NN_DIMS, TN_DIMS, swap_b=True)
_cdot_tn = _make_cdot(TN_DIMS, NT_DIMS, NN_DIMS, swap_a=True)


DN_SUPER = 4 * DN_CHUNK


@jax.custom_vjp
def _unit_lower_inverse(lmat):
    n = lmat.shape[0]
    steps = int(math.log2(DN_CHUNK))
    eye = jnp.where(lax.broadcasted_iota(jnp.int32, (n, n), 0) == lax.broadcasted_iota(jnp.int32, (n, n), 1), 1.0, 0.0)
    inv = eye - lmat
    pw = _hdot(lmat, lmat)
    for it in range(steps - 1):
        inv = inv + _hdot(inv, pw)
        if it < steps - 2:
            pw = _hdot(pw, pw)
    return inv


def _unit_lower_inverse_fwd(lmat):
    inv = _unit_lower_inverse(lmat)
    return inv, inv


def _unit_lower_inverse_bwd(inv, g):
    return (-_hdot(_hdot(inv, g, TN_DIMS), inv, NT_DIMS),)


_unit_lower_inverse.defvjp(_unit_lower_inverse_fwd, _unit_lower_inverse_bwd)


@jax.custom_vjp
def _known_inverse(lmat, inv):
    return inv


def _known_inverse_fwd(lmat, inv):
    return inv, inv


def _known_inverse_bwd(inv, g):
    return -_hdot(_hdot(inv, g, TN_DIMS), inv, NT_DIMS), jnp.zeros_like(inv)


_known_inverse.defvjp(_known_inverse_fwd, _known_inverse_bwd)


def _dn_local(q, k, v, bb, gb, known_inv=None):
    n = q.shape[0]
    r = lax.broadcasted_iota(jnp.int32, (n, n), 0)
    cc = lax.broadcasted_iota(jnp.int32, (n, n), 1)
    shift = int(math.log2(DN_CHUNK))
    same = lax.shift_right_logical(r, shift) == lax.shift_right_logical(cc, shift)
    incl = jnp.where(same, jnp.where(r >= cc, 1.0, 0.0), 0.0)
    strict = jnp.where(same, jnp.where(r > cc, 1.0, 0.0), 0.0)
    gc = _hdot(incl, gb)
    gc_row = _hdot(jnp.full((n, HD), 1.0 / HD, F32), gc, NT_DIMS)
    diff = jnp.concatenate([gc] * (n // HD), axis=1) - gc_row
    decay = incl * jnp.exp(diff * incl)
    kb = k * bb
    lmat = _cdot_nt(kb, k) * (strict * decay)
    egc = jnp.exp(gc)
    inv = _unit_lower_inverse(lmat) if known_inv is None else _known_inverse(lmat, known_inv)
    u = _hdot(inv, v * bb)
    w = _hdot(inv, kb * egc)
    attn = _cdot_nt(q, k) * decay
    gl = _hdot(jnp.where(same, 1.0, 0.0), gb)
    return u, w, attn, q * egc, k * jnp.exp(gl - gc), jnp.exp(gl), inv


def _attn_pairs(attn):
    return jnp.concatenate([attn[:HD, :HD], attn[HD:, HD:]], axis=0)


def _attn_unpairs(a):
    z = jnp.zeros((HD, HD), F32)
    return jnp.concatenate([jnp.concatenate([a[:HD], z], axis=1), jnp.concatenate([z, a[HD:]], axis=1)], axis=0)


def _dn_step(u, w, a, qd, kd, cdrows, state, odd):
    v_new = u - _cdot(w, state)
    z = jnp.zeros_like(v_new)
    o = _cdot(qd, state) + _cdot(a, jnp.concatenate([z, v_new] if odd else [v_new, z], axis=0))
    return o, state * jnp.mean(cdrows, axis=0, keepdims=True) + _cdot_tn(kd, v_new)


def _dn_local_pass(fn, s, ins, outs):
    def step(it, _):
        sl = pl.ds(pl.multiple_of(it * DN_SUPER, DN_SUPER), DN_SUPER)
        res = fn(*[ref[sl, :] for ref in ins])
        for ref, val in zip(outs, res):
            ref[sl, :] = val
        return 0

    lax.fori_loop(0, s // DN_SUPER, step, 0)


def _lane_pick(row, idx):
    lane = lax.broadcasted_iota(jnp.int32, row.shape, 1)
    return jnp.sum(jnp.where(lane == idx, row, 0.0), axis=1, keepdims=True)


def _col_pick(x, idx):
    lane = lax.broadcasted_iota(jnp.int32, x.shape, 1)
    return jnp.sum(jnp.where(lane == idx, x, 0.0), axis=1, keepdims=True)


def _conv_silu(x, w):
    xc = (w[3:4, :] * x + w[2:3, :] * _shift_down(x, 1) + w[1:2, :] * _shift_down(x, 2)
          + w[0:1, :] * _shift_down(x, 3))
    return xc * _sigmoid(xc), xc


def _conv_silu_bwd(x, w, xc, dxs, dw_ref):
    sg = _sigmoid(xc)
    dxc = dxs * (sg * (1.0 + xc * (1.0 - sg)))
    dx = (w[3:4, :] * dxc + w[2:3, :] * _shift_up(dxc, 1) + w[1:2, :] * _shift_up(dxc, 2)
          + w[0:1, :] * _shift_up(dxc, 3))
    dw_ref[3:4, :] += jnp.sum(dxc * x, axis=0, keepdims=True)
    dw_ref[2:3, :] += jnp.sum(dxc * _shift_down(x, 1), axis=0, keepdims=True)
    dw_ref[1:2, :] += jnp.sum(dxc * _shift_down(x, 2), axis=0, keepdims=True)
    dw_ref[0:1, :] += jnp.sum(dxc * _shift_down(x, 3), axis=0, keepdims=True)
    return dx


def _dn_prep(qr_ref, kr_ref, vr_ref, ab_ref, cq_ref, ck_ref, cv_ref, par_ref, head):
    qs, qc = _conv_silu(qr_ref[...].astype(F32), cq_ref[...])
    ks, kc = _conv_silu(kr_ref[...].astype(F32), ck_ref[...])
    vs, vc = _conv_silu(vr_ref[...].astype(F32), cv_ref[...])
    rq = lax.rsqrt(jnp.sum(qs * qs, axis=1, keepdims=True) + EPS)
    rk = lax.rsqrt(jnp.sum(ks * ks, axis=1, keepdims=True) + EPS)
    ab = ab_ref[...]
    a_in = _col_pick(ab, head) + _lane_pick(par_ref[1:2, :], head)
    beta = _sigmoid(_col_pick(ab, NH + head))
    neg_ea = -jnp.exp(_lane_pick(par_ref[0:1, :], head))
    g = neg_ea * _softplus(a_in)
    return dict(q=qs * rq * (HD ** -0.5), k=ks * rk, v=vs, beta=beta, g=g, qs=qs, ks=ks, qc=qc, kc=kc, vc=vc,
                rq=rq, rk=rk, a_in=a_in, neg_ea=neg_ea)


ONE_BUF = pl.Buffered(1)
DN_BWD_VMEM = 62 * 1024 * 1024


def _dn_specs(nb, s):
    def col(cb):
        return pl.BlockSpec((s, HD), lambda h, b: (b, cb + h))

    def conv(cb):
        return pl.BlockSpec((DN_CONV_W, HD), lambda h, b: (0, cb + h))

    return col, conv


DN_CONV_W = 4


def _with_exchange(body, n_in, n_out, n_scratch, exchange, grid):
    if exchange is None:
        return body
    parts_fn, n, nls = exchange

    def wrapped(*refs):
        ins, xs = refs[:n_in], refs[n_in:n_in + n]
        outs, os = refs[n_in + n:n_in + n + n_out], refs[n_in + n + n_out:n_in + 2 * n + n_out]
        rest = refs[n_in + 2 * n + n_out:]
        scratch, sems = rest[:n_scratch], rest[n_scratch:]
        pos = [pl.program_id(k) for k in range(len(grid))]
        first = functools.reduce(jnp.logical_and, [p == 0 for p in pos])
        last = functools.reduce(jnp.logical_and, [p == g - 1 for p, g in zip(pos, grid)])
        start, forward, finish = parts_fn(xs, os, nls, *sems)
        pl.when(first)(start)
        pl.when(last)(forward)
        body(*ins, *outs, *scratch)
        pl.when(last)(finish)

    return wrapped


def _dn_fwd(proj, ab, conv_w, par, gain, nb, s, *, name, gather=None):
    nc = s // DN_CHUNK
    col, conv = _dn_specs(nb, s)
    gx, gnl = gather if gather else ([], [])

    def body(qr_ref, kr_ref, vr_ref, z_ref, ab_ref, cq_ref, ck_ref, cv_ref, par_ref, gain_ref,
             y_ref, o_ref, st_ref, u_ref, w_ref, at_ref, qd_ref, kd_ref, cd_ref, inv_ref, q_s, k_s, v_s, bb_s, gb_s):
        p = _dn_prep(qr_ref, kr_ref, vr_ref, ab_ref, cq_ref, ck_ref, cv_ref, par_ref, pl.program_id(0))
        q_s[...], k_s[...], v_s[...] = p["q"], p["k"], p["v"]
        bb_s[...] = jnp.broadcast_to(p["beta"], (s, HD))
        gb_s[...] = jnp.broadcast_to(p["g"], (s, HD))
        def local(*args):
            u, w, attn, qd, kd, cd, inv = _dn_local(*args)
            return u, w, _attn_pairs(attn), qd, kd, cd, inv

        _dn_local_pass(local, s, [q_s, k_s, v_s, bb_s, gb_s], [u_ref, w_ref, at_ref, qd_ref, kd_ref, cd_ref, inv_ref])

        def chunk_pair(pi, state):
            for odd in (0, 1):
                ci = 2 * pi + odd
                sl = pl.ds(pl.multiple_of(ci * DN_CHUNK, DN_CHUNK), DN_CHUNK)
                st_ref[ci] = state
                o, state = _dn_step(u_ref[sl, :], w_ref[sl, :], at_ref[sl, :], qd_ref[sl, :], kd_ref[sl, :],
                                    cd_ref[sl, :], state, odd)
                o_ref[sl, :] = o
            return state

        lax.fori_loop(0, nc // 2, chunk_pair, jnp.zeros((HD, HD), F32))
        o = o_ref[...]
        z = z_ref[...].astype(F32)
        on = o * lax.rsqrt(jnp.mean(o * o, axis=1, keepdims=True) + EPS) * gain_ref[...]
        y_ref[...] = on * (z * _sigmoid(z))

    out = pl.BlockSpec((s, HD), lambda h, b: (b, h))
    sds = jax.ShapeDtypeStruct((nb * s, BW), F32)
    exchange = (_gather_parts, len(gx), gnl) if gx else None
    res = pl.pallas_call(
        _with_exchange(body, 10, 10, 5, exchange, (NH, nb)), grid=(NH, nb),
        in_specs=[col(CB_DNQ), col(CB_DNK), col(CB_DNV), col(CB_DNZ), pl.BlockSpec((s, LANE), lambda h, b: (b, 0)),
                  conv(0), conv(NH), conv(2 * NH), pl.BlockSpec((8, LANE), lambda h, b: (0, 0)),
                  pl.BlockSpec((1, HD), lambda h, b: (0, 0))] + [HBM_SPEC] * len(gx),
        out_specs=[out, out, pl.BlockSpec((None, None, nc, HD, HD), lambda h, b: (b, h, 0, 0, 0))] + [out] * 6
        + [pl.BlockSpec((s, DN_SUPER), lambda h, b: (b, h))] + [HBM_SPEC] * len(gx),
        out_shape=[sds, sds, jax.ShapeDtypeStruct((nb, NH, nc, HD, HD), F32)] + [sds] * 6
        + [jax.ShapeDtypeStruct((nb * s, NH * DN_SUPER), F32)] + _gather_shapes(gx, gnl),
        scratch_shapes=[pltpu.VMEM((s, HD), F32)] * 5 + (_comm_sems(len(gx), 7) if gx else []),
        compiler_params=_cp(("arbitrary", "arbitrary")), name=name)(
            proj, proj, proj, proj, ab, conv_w, conv_w, conv_w, par, gain, *gx)
    return res[0], res[1], res[2], list(res[3:10]), list(res[10:])


def _dn_bwd(proj, ab, conv_w, par, gain, o_pre, states, local, dy, nb, s, *, name, scatter=None):
    nc = s // DN_CHUNK
    col, conv = _dn_specs(nb, s)
    gx, gnl = scatter if scatter else ([], [])

    def body(qr_ref, kr_ref, vr_ref, z_ref, ab_ref, cq_ref, ck_ref, cv_ref, par_ref, gain_ref, o_ref, st_ref, dy_ref,
             u_hbm, w_hbm, at_hbm, qd_hbm, kd_hbm, cd_hbm, inv_hbm,
             dqr_ref, dkr_ref, dvr_ref, dz_ref, dab_ref, dcq_ref, dck_ref, dcv_ref, dpar_ref, dgain_ref,
             q_s, k_s, v_s, bb_s, gb_s, do_s, u_s, w_s, qd_s, kd_s, at_s, cd_s, load_sems, inv_buf, inv_sems):
        head, b = pl.program_id(0), pl.program_id(1)
        local_refs = [u_s, w_s, at_s, qd_s, kd_s, cd_s]
        loads = [pltpu.make_async_copy(src.at[pl.ds(pl.multiple_of(b * s, s), s), pl.ds(pl.multiple_of(head * HD, HD), HD)],
                                       dst, load_sems.at[i])
                 for i, (src, dst) in enumerate(zip((u_hbm, w_hbm, at_hbm, qd_hbm, kd_hbm, cd_hbm), local_refs))]
        for cp in loads:
            cp.start()
        p = _dn_prep(qr_ref, kr_ref, vr_ref, ab_ref, cq_ref, ck_ref, cv_ref, par_ref, head)
        q_s[...], k_s[...], v_s[...] = p["q"], p["k"], p["v"]
        bb_s[...] = jnp.broadcast_to(p["beta"], (s, HD))
        gb_s[...] = jnp.broadcast_to(p["g"], (s, HD))

        @pl.when(b == 0)
        def _():
            for ref in (dcq_ref, dck_ref, dcv_ref, dpar_ref):
                ref[...] = jnp.zeros_like(ref)

        @pl.when((b == 0) & (head == 0))
        def _():
            dgain_ref[...] = jnp.zeros_like(dgain_ref)

        o, z, dy = o_ref[...], z_ref[...].astype(F32), dy_ref[...]
        rstd = lax.rsqrt(jnp.mean(o * o, axis=1, keepdims=True) + EPS)
        ohat = o * rstd
        sgz = _sigmoid(z)
        dz_ref[...] = dy * (ohat * gain_ref[...]) * (sgz * (1.0 + z * (1.0 - sgz)))
        don = dy * (z * sgz)
        dgain_ref[...] += jnp.sum(don * ohat, axis=0, keepdims=True)
        dxh = don * gain_ref[...]
        do_s[...] = rstd * (dxh - ohat * jnp.mean(dxh * ohat, axis=1, keepdims=True))

        for cp in loads:
            cp.wait()

        def chunk_pair(pr, dstate):
            for odd in (1, 0):
                ci = nc - 1 - 2 * pr - (1 - odd)
                sl = pl.ds(pl.multiple_of(ci * DN_CHUNK, DN_CHUNK), DN_CHUNK)
                _, vjp = jax.vjp(functools.partial(_dn_step, odd=odd), u_s[sl, :], w_s[sl, :], at_s[sl, :],
                                 qd_s[sl, :], kd_s[sl, :], cd_s[sl, :], st_ref[ci])
                du, dw, dat, dqd, dkd, dcd, dstate = vjp((do_s[sl, :], dstate))
                u_s[sl, :], w_s[sl, :], at_s[sl, :], qd_s[sl, :], kd_s[sl, :], cd_s[sl, :] = du, dw, dat, dqd, dkd, dcd
            return dstate

        lax.fori_loop(0, nc // 2, chunk_pair, jnp.zeros((HD, HD), F32))

        def inv_load(it, slot):
            rows = pl.ds(pl.multiple_of(b * s + it * DN_SUPER, DN_SUPER), DN_SUPER)
            cols = pl.ds(pl.multiple_of(head * DN_SUPER, DN_SUPER), DN_SUPER)
            return pltpu.make_async_copy(inv_hbm.at[rows, cols], inv_buf.at[slot], inv_sems.at[slot])

        def local_bwd(it, _):
            slot = lax.rem(it, 2)
            inv_load(it, slot).wait()

            @pl.when(it + 1 < s // DN_SUPER)
            def _():
                inv_load(it + 1, 1 - slot).start()

            sl = pl.ds(pl.multiple_of(it * DN_SUPER, DN_SUPER), DN_SUPER)
            ins = [ref[sl, :] for ref in (q_s, k_s, v_s, bb_s, gb_s)]
            du, dw, dat, dqd, dkd, dcd = [ref[sl, :] for ref in local_refs]
            _, vjp = jax.vjp(lambda *a: _dn_local(*a, known_inv=inv_buf[slot])[:6], *ins)
            dq, dk, dv, dbb, dgb = vjp((du, dw, _attn_unpairs(dat), dqd, dkd, dcd))
            q_s[sl, :], k_s[sl, :], v_s[sl, :] = dq, dk, dv
            bb_s[sl, :] = jnp.broadcast_to(jnp.sum(dbb, axis=1, keepdims=True), (DN_SUPER, HD))
            gb_s[sl, :] = jnp.broadcast_to(jnp.sum(dgb, axis=1, keepdims=True), (DN_SUPER, HD))
            return 0

        inv_load(0, 0).start()
        lax.fori_loop(0, s // DN_SUPER, local_bwd, 0)

        dq, dk, dv = q_s[...], k_s[...], v_s[...]
        qs, ks, rq, rk = p["qs"], p["ks"], p["rq"], p["rk"]
        dqs = (HD ** -0.5) * (rq * dq - qs * (rq * rq * rq) * jnp.sum(dq * qs, axis=1, keepdims=True))
        dks = rk * dk - ks * (rk * rk * rk) * jnp.sum(dk * ks, axis=1, keepdims=True)
        dqr_ref[...] = _conv_silu_bwd(qr_ref[...].astype(F32), cq_ref[...], p["qc"], dqs, dcq_ref)
        dkr_ref[...] = _conv_silu_bwd(kr_ref[...].astype(F32), ck_ref[...], p["kc"], dks, dck_ref)
        dvr_ref[...] = _conv_silu_bwd(vr_ref[...].astype(F32), cv_ref[...], p["vc"], dv, dcv_ref)

        dbeta, dg = bb_s[:, 0:1], gb_s[:, 0:1]
        beta = p["beta"]
        db_logit = dbeta * beta * (1.0 - beta)
        da = dg * p["neg_ea"] * _sigmoid(p["a_in"])
        lane = lax.broadcasted_iota(jnp.int32, (s, LANE), 1)
        dab_ref[...] = jnp.where(lane == head, da, 0.0) + jnp.where(lane == NH + head, db_logit, 0.0)
        dpar_ref[0:1, :] += jnp.broadcast_to(jnp.sum(dg * p["g"], axis=0, keepdims=True), (1, LANE))
        dpar_ref[1:2, :] += jnp.broadcast_to(jnp.sum(da, axis=0, keepdims=True), (1, LANE))

    out = pl.BlockSpec((s, HD), lambda h, b: (b, h))
    in_blk = pl.BlockSpec((s, HD), lambda h, b: (b, h), pipeline_mode=ONE_BUF)
    cblk = pl.BlockSpec((DN_CONV_W, HD), lambda h, b: (0, h))
    sds = jax.ShapeDtypeStruct((nb * s, BW), F32)
    csds = jax.ShapeDtypeStruct((DN_CONV_W, BW), F32)
    exchange = (_all_to_all_parts, len(gx), gnl) if gx else None
    res = pl.pallas_call(
        _with_exchange(body, 20, 10, 15, exchange, (NH, nb)), grid=(NH, nb),
        in_specs=[col(CB_DNQ), col(CB_DNK), col(CB_DNV), col(CB_DNZ),
                  pl.BlockSpec((s, LANE), lambda h, b: (b, 0), pipeline_mode=ONE_BUF),
                  conv(0), conv(NH), conv(2 * NH), pl.BlockSpec((8, LANE), lambda h, b: (0, 0)),
                  pl.BlockSpec((1, HD), lambda h, b: (0, 0)), in_blk,
                  pl.BlockSpec((None, None, nc, HD, HD), lambda h, b: (b, h, 0, 0, 0), pipeline_mode=ONE_BUF), in_blk]
        + [HBM_SPEC] * (7 + len(gx)),
        out_specs=[out, out, out, out, pl.BlockSpec((None, s, LANE), lambda h, b: (h, b, 0)), cblk, cblk, cblk,
                   pl.BlockSpec((None, 8, LANE), lambda h, b: (h, 0, 0)), pl.BlockSpec((1, HD), lambda h, b: (0, 0))]
        + [HBM_SPEC] * len(gx),
        out_shape=[sds, sds, sds, sds, jax.ShapeDtypeStruct((NH, nb * s, LANE), F32), csds, csds, csds,
                   jax.ShapeDtypeStruct((NH, 8, LANE), F32), jax.ShapeDtypeStruct((1, HD), F32)]
        + _all_to_all_shapes(gx, gnl),
        scratch_shapes=[pltpu.VMEM((s, HD), F32)] * 12 + [pltpu.SemaphoreType.DMA((6,)),
                                                           pltpu.VMEM((2, DN_SUPER, DN_SUPER), F32),
                                                           pltpu.SemaphoreType.DMA((2,))]
        + (_comm_sems(len(gx), 7) if gx else []),
        compiler_params=_cp(("arbitrary", "arbitrary"), DN_BWD_VMEM), name=name)(
            proj, proj, proj, proj, ab, conv_w, conv_w, conv_w, par, gain, o_pre, states, dy, *local, *gx)
    return tuple(res[:10]) + (list(res[10:]),)


def _sum_heads(x, *, name):
    nh, t, c = x.shape
    tm = _pick(t, (1024, 512, 256, 128))

    def body(x_ref, o_ref):
        o_ref[...] = (x_ref[0] + x_ref[1] + x_ref[2] + x_ref[3]).astype(BF16)

    return pl.pallas_call(
        body, grid=(t // tm,), in_specs=[pl.BlockSpec((nh, tm, c), lambda i: (0, i, 0))],
        out_specs=pl.BlockSpec((tm, c), lambda i: (i, 0)), out_shape=jax.ShapeDtypeStruct((t, c), BF16),
        compiler_params=_cp(("parallel",)), name=name)(x)


MERGE_TM = 256


def _merge_fwd(x, proj, yp, yd, ys, b_gate, wb, wo, *, name):
    t, d = x.shape
    tm = _pick(t, (MERGE_TM, 128))

    def body(x_ref, g0_ref, g1_ref, g2_ref, yp_ref, yd_ref, ys_ref, bg_ref, wb_ref, wo_ref, o_ref):
        merged = jnp.zeros((tm, d), F32)
        for n, (g_ref, y_ref) in enumerate(((g0_ref, yp_ref), (g1_ref, yd_ref), (g2_ref, ys_ref))):
            gate = _sigmoid(g_ref[...].astype(F32) + bg_ref[:, n * d:(n + 1) * d])
            merged = merged + gate * _bdot(y_ref[...], wb_ref[n])
        o_ref[...] = x_ref[...] + _bdot(merged, wo_ref[...])

    row = pl.BlockSpec((tm, d), lambda i: (i, 0))
    yblk = pl.BlockSpec((tm, BW), lambda i: (i, 0))

    def gl(n):
        return pl.BlockSpec((tm, d), lambda i: (i, CB_GATE + n))

    return pl.pallas_call(
        body, grid=(t // tm,),
        in_specs=[row, gl(0), gl(1), gl(2), yblk, yblk, yblk, pl.BlockSpec((1, 3 * d), lambda i: (0, 0)),
                  pl.BlockSpec((3, BW, d), lambda i: (0, 0, 0)), pl.BlockSpec((d, d), lambda i: (0, 0))],
        out_specs=row, out_shape=jax.ShapeDtypeStruct((t, d), F32),
        compiler_params=_cp(("parallel",)), name=name)(x, proj, proj, proj, yp, yd, ys, b_gate, wb, wo)


def _merge_bwd(proj, yp, yd, ys, b_gate, wb, wo, dx, *, name):
    t, d = dx.shape
    tm = _pick(t, (MERGE_TM, 128))

    def body(g0_ref, g1_ref, g2_ref, yp_ref, yd_ref, ys_ref, bg_ref, wb_ref, wo_ref, dx_ref,
             dyp_ref, dyd_ref, dys_ref, dgl_ref, mg_ref, dxh_ref, dbd_ref, dbg_ref):
        dxh = dx_ref[...].astype(BF16)
        dxh_ref[...] = dxh
        dmerged = _bdot(dxh, wo_ref[...], NT_DIMS)
        merged = jnp.zeros((tm, d), F32)

        @pl.when(pl.program_id(0) == 0)
        def _():
            dbg_ref[...] = jnp.zeros_like(dbg_ref)

        for n, (g_ref, y_ref, dy_ref) in enumerate(((g0_ref, yp_ref, dyp_ref), (g1_ref, yd_ref, dyd_ref),
                                                    (g2_ref, ys_ref, dys_ref))):
            gate = _sigmoid(g_ref[...].astype(F32) + bg_ref[:, n * d:(n + 1) * d])
            bd = _bdot(y_ref[...], wb_ref[n])
            merged = merged + gate * bd
            dgl = dmerged * bd * gate * (1.0 - gate)
            dgl_ref[:, n * d:(n + 1) * d] = dgl.astype(BF16)
            dbg_ref[:, n * d:(n + 1) * d] += jnp.sum(dgl, axis=0, keepdims=True)
            dbd = (dmerged * gate).astype(BF16)
            dbd_ref[n] = dbd
            dy_ref[...] = _bdot(dbd, wb_ref[n], NT_DIMS)
        mg_ref[...] = merged.astype(BF16)

    row = pl.BlockSpec((tm, d), lambda i: (i, 0))
    yblk = pl.BlockSpec((tm, BW), lambda i: (i, 0))
    bgv = pl.BlockSpec((1, 3 * d), lambda i: (0, 0))

    def gl(n):
        return pl.BlockSpec((tm, d), lambda i: (i, CB_GATE + n))

    ysds = jax.ShapeDtypeStruct((t, BW), F32)
    return pl.pallas_call(
        body, grid=(t // tm,),
        in_specs=[gl(0), gl(1), gl(2), yblk, yblk, yblk, bgv,
                  pl.BlockSpec((3, BW, d), lambda i: (0, 0, 0)), pl.BlockSpec((d, d), lambda i: (0, 0)), row],
        out_specs=[yblk, yblk, yblk, pl.BlockSpec((tm, 3 * d), lambda i: (i, 0)), row, row,
                   pl.BlockSpec((3, tm, d), lambda i: (0, i, 0)), bgv],
        out_shape=[ysds, ysds, ysds, jax.ShapeDtypeStruct((t, 3 * d), BF16), jax.ShapeDtypeStruct((t, d), BF16),
                   jax.ShapeDtypeStruct((t, d), BF16), jax.ShapeDtypeStruct((3, t, d), BF16),
                   jax.ShapeDtypeStruct((1, 3 * d), F32)],
        compiler_params=_cp(("arbitrary",)), name=name)(proj, proj, proj, yp, yd, ys, b_gate, wb, wo, dx)


def _loss_head(x, g, target, *, name):
    t, d = x.shape
    tm = _pick(t, (512, 256, 128))

    def body(x_ref, g_ref, t_ref, dx_ref, dg_ref, loss_ref):
        xhat, rstd = _rms_stats(x_ref[...])
        err = xhat * g_ref[...] - t_ref[...]
        dx, dg = _rms_bwd_vals(err * (1.0 / d), xhat, rstd, g_ref[...])
        dx_ref[...] = dx

        @pl.when(pl.program_id(0) == 0)
        def _():
            dg_ref[...] = jnp.zeros_like(dg_ref)
            loss_ref[...] = jnp.zeros_like(loss_ref)

        dg_ref[...] += dg
        part = jnp.sum(jnp.sum(err * err, axis=1, keepdims=True), axis=0, keepdims=True) * (0.5 / d)
        loss_ref[...] += jnp.broadcast_to(part, (1, LANE))

    row = pl.BlockSpec((tm, d), lambda i: (i, 0))
    vec = pl.BlockSpec((1, d), lambda i: (0, 0))
    return pl.pallas_call(
        body, grid=(t // tm,), in_specs=[row, vec, row],
        out_specs=[row, vec, pl.BlockSpec((1, LANE), lambda i: (0, 0))],
        out_shape=[jax.ShapeDtypeStruct((t, d), F32), jax.ShapeDtypeStruct((1, d), F32),
                   jax.ShapeDtypeStruct((1, LANE), F32)],
        compiler_params=_cp(("arbitrary",)), name=name)(x, g.reshape(1, d), target)


def _adamw(w, g, m, v, *, name):
    rows, cols = w.shape
    fits = [c for c in (1024, 704, 512, 352, 256, 128, 64, 32, 16, 8) if c * cols * 4 * 14 <= VMEM_LIMIT // 2]
    tr = _pick(rows, fits)
    c1 = 1.0 / (1.0 - ADAM_B1 ** ADAM_STEP)
    c2 = 1.0 / (1.0 - ADAM_B2 ** ADAM_STEP)

    def body(w_ref, g_ref, m_ref, v_ref, d_ref, nm_ref, nv_ref):
        g = g_ref[...]
        nm = ADAM_B1 * m_ref[...] + (1.0 - ADAM_B1) * g
        nv = ADAM_B2 * v_ref[...] + (1.0 - ADAM_B2) * (g * g)
        nm_ref[...] = nm
        nv_ref[...] = nv
        d_ref[...] = -ADAM_LR * ((nm * c1) / (jnp.sqrt(nv * c2) + ADAM_EPS) + ADAM_WD * w_ref[...])

    blk = pl.BlockSpec((tr, cols), lambda i: (i, 0))
    sds = jax.ShapeDtypeStruct((rows, cols), F32)
    return pl.pallas_call(
        body, grid=(rows // tr,), in_specs=[blk] * 4, out_specs=[blk] * 3, out_shape=[sds] * 3,
        compiler_params=_cp(("parallel",)), name=name)(w, g, m, v)


MESH_ID = pl.DeviceIdType.MESH
HBM_SPEC = pl.BlockSpec(memory_space=pl.ANY)
OTHER_CHIPS = ((1, 0), (0, 1), (1, 1))


def _at_slot(ref, nl, slot):
    return ref.at[(slice(None),) * nl + (slot,)]


def _slotted(shape, nl, slots):
    return tuple(shape[:nl]) + (slots,) + tuple(shape[nl:])


def _flip(v, f):
    return 1 - v if f else v


def _comm_call(body, n, out_shapes, n_remote, args, name):
    return pl.pallas_call(
        body, out_shape=out_shapes, in_specs=[HBM_SPEC] * len(args), out_specs=[HBM_SPEC] * len(out_shapes),
        scratch_shapes=[pltpu.SemaphoreType.DMA((n * n_remote,)), pltpu.SemaphoreType.DMA((n * n_remote,)),
                        pltpu.SemaphoreType.DMA((n * 4,))],
        compiler_params=pltpu.CompilerParams(has_side_effects=True), name=name)(*args)


def _gather(xs, nls, *, name):
    n = len(xs)

    def body(*refs):
        start, forward, finish = _gather_parts(refs[:n], refs[n:2 * n], nls, *refs[2 * n:])
        start()
        forward()
        finish()

    return _comm_call(body, n, _gather_shapes(xs, nls), 7, xs, name)


def _gather_shapes(xs, nls):
    return [jax.ShapeDtypeStruct(_slotted(v.shape, nl, N_DEV), v.dtype) for v, nl in zip(xs, nls)]


def _comm_sems(n, n_remote):
    return [pltpu.SemaphoreType.DMA((n * n_remote,)), pltpu.SemaphoreType.DMA((n * n_remote,)),
            pltpu.SemaphoreType.DMA((n * 4,))]


def _gather_parts(x_refs, o_refs, nls, send_sems, recv_sems, local_sems):
    n = len(x_refs)
    x, y, c = lax.axis_index("x"), lax.axis_index("y"), lax.axis_index("c")
    me, sibling = (x, y, c), (x, y, 1 - c)
    chips = [(_flip(x, fx), _flip(y, fy)) for fx, fy in OTHER_CHIPS]

    def copy(a, k, block, to, src=None):
        dst = _at_slot(o_refs[a], nls[a], 4 * block[0] + 2 * block[1] + block[2])
        return pltpu.make_async_remote_copy(
            src_ref=dst if src is None else src, dst_ref=dst, send_sem=send_sems.at[a * 7 + k],
            recv_sem=recv_sems.at[a * 7 + k], device_id=to, device_id_type=MESH_ID)

    def mine(a):
        return pltpu.make_async_copy(x_refs[a], _at_slot(o_refs[a], nls[a], 4 * x + 2 * y + c), local_sems.at[a])

    def first(a):
        return ([copy(a, 0, me, sibling, src=x_refs[a])]
                + [copy(a, 1 + j, me, (*chip, c), src=x_refs[a]) for j, chip in enumerate(chips)])

    def start():
        for a in range(n):
            mine(a).start()
            for cp in first(a):
                cp.start()

    def forward():
        for j, chip in enumerate(chips):
            for a in range(n):
                copy(a, 1 + j, (*chip, c), me).wait_recv()
                copy(a, 4 + j, (*chip, c), sibling).start()

    def finish():
        for a in range(n):
            copy(a, 0, sibling, me).wait_recv()
            for j, chip in enumerate(chips):
                copy(a, 4 + j, (*chip, 1 - c), me).wait_recv()
        for a in range(n):
            for cp in first(a):
                cp.wait_send()
            for j, chip in enumerate(chips):
                copy(a, 4 + j, (*chip, c), sibling).wait_send()
        for a in range(n):
            mine(a).wait()

    return start, forward, finish


ALL_FLIPS = ((0, 0, 1), (0, 1, 0), (0, 1, 1), (1, 0, 0), (1, 0, 1), (1, 1, 0), (1, 1, 1))


def _all_to_all_parts(g_refs, r_refs, nls, send_sems, recv_sems, local_sems):
    del local_sems
    n = len(g_refs)
    x, y, c = lax.axis_index("x"), lax.axis_index("y"), lax.axis_index("c")

    def copies():
        out = []
        for a in range(n):
            for k, (fx, fy, fc) in enumerate(ALL_FLIPS):
                p = (_flip(x, fx), _flip(y, fy), _flip(c, fc))
                out.append(pltpu.make_async_remote_copy(
                    src_ref=_at_slot(g_refs[a], nls[a], 4 * p[0] + 2 * p[1] + p[2]), dst_ref=_at_slot(r_refs[a], nls[a], k),
                    send_sem=send_sems.at[a * 7 + k], recv_sem=recv_sems.at[a * 7 + k], device_id=p,
                    device_id_type=MESH_ID))
        return out

    def start():
        for cp in copies():
            cp.start()

    def finish():
        cps = copies()
        for cp in cps:
            cp.wait_recv()
        for cp in cps:
            cp.wait_send()

    return start, lambda: None, finish


def _all_to_all_shapes(gs, nls):
    return [jax.ShapeDtypeStruct(_slotted(v.shape[:nl] + v.shape[nl + 1:], nl, 7), v.dtype) for v, nl in zip(gs, nls)]


def _scatter_pair(gs, nls, *, name):
    n = len(gs)

    def body(*refs):
        g_refs, got_refs, (send_sems, recv_sems, _) = refs[:n], refs[n:2 * n], refs[2 * n:]
        x, y, c = lax.axis_index("x"), lax.axis_index("y"), lax.axis_index("c")
        remote = []
        for a in range(n):
            for q in range(4):
                rc = pltpu.make_async_remote_copy(
                    src_ref=_at_slot(g_refs[a], nls[a], 2 * q + 1 - c), dst_ref=_at_slot(got_refs[a], nls[a], q),
                    send_sem=send_sems.at[a * 4 + q], recv_sem=recv_sems.at[a * 4 + q], device_id=(x, y, 1 - c),
                    device_id_type=MESH_ID)
                rc.start()
                remote.append(rc)
        for rc in remote:
            rc.wait_recv()
        for rc in remote:
            rc.wait_send()

    outs = [jax.ShapeDtypeStruct(_slotted(v.shape[:nl] + v.shape[nl + 1:], nl, 4), v.dtype) for v, nl in zip(gs, nls)]
    return _comm_call(body, n, outs, 4, gs, name)


def _scatter_chips(ps, nls, *, name):
    n = len(ps)

    def body(*refs):
        p_refs, r_refs, (send_sems, recv_sems, _) = refs[:n], refs[n:2 * n], refs[2 * n:]
        x, y, c = lax.axis_index("x"), lax.axis_index("y"), lax.axis_index("c")
        remote = []
        for a in range(n):
            for k, (fx, fy) in enumerate(OTHER_CHIPS):
                tx, ty = _flip(x, fx), _flip(y, fy)
                rc = pltpu.make_async_remote_copy(
                    src_ref=_at_slot(p_refs[a], nls[a], 2 * tx + ty), dst_ref=_at_slot(r_refs[a], nls[a], k),
                    send_sem=send_sems.at[a * 3 + k], recv_sem=recv_sems.at[a * 3 + k], device_id=(tx, ty, c),
                    device_id_type=MESH_ID)
                rc.start()
                remote.append(rc)
        for rc in remote:
            rc.wait_recv()
        for rc in remote:
            rc.wait_send()

    outs = [jax.ShapeDtypeStruct(_slotted(v.shape[:nl] + v.shape[nl + 1:], nl, 3), v.dtype) for v, nl in zip(ps, nls)]
    return _comm_call(body, n, outs, 3, ps, name)


def _pair_add(g, got, core, *, name):
    rows, cols = g.shape[-2:]
    lf = math.prod(got.shape[:-3])
    tr = _pick(rows, (1024, 512, 352, 256, 128))

    def body(core_ref, g_ref, got_ref, o_ref):
        o_ref[...] = (g_ref[...].astype(F32) + got_ref[...].astype(F32)).astype(BF16)

    blk = pl.BlockSpec((None, None, tr, cols), lambda i, q, j, core_ref: (i, q, j, 0))
    out = pl.pallas_call(
        body, grid_spec=pltpu.PrefetchScalarGridSpec(
            num_scalar_prefetch=1, grid=(lf, 4, rows // tr),
            in_specs=[pl.BlockSpec((None, None, None, tr, cols), lambda i, q, j, core_ref: (i, q, core_ref[0], j, 0)),
                      blk], out_specs=blk),
        out_shape=jax.ShapeDtypeStruct((lf, 4, rows, cols), BF16),
        compiler_params=_cp(("parallel", "parallel", "parallel")), name=name)(
            core, g.reshape(lf, 4, 2, rows, cols), got.reshape(lf, 4, rows, cols))
    return out.reshape(got.shape)


def _sum_adamw(p, r, own, w, m, v, layer, prev, *, name):
    shape = w.shape[1:]
    rows, cols = shape[-2:]
    lf = math.prod(shape[:-2])
    np_, nk = p.shape[-3], r.shape[-3]
    fits = [c for c in (1024, 512, 352, 256, 128, 64, 32, 16) if c * cols * (7 * 4 + (nk + 1) * 2) * 2 <= VMEM_LIMIT // 2]
    tr = _pick(rows, fits)
    c1 = 1.0 / (1.0 - ADAM_B1 ** ADAM_STEP)
    c2 = 1.0 / (1.0 - ADAM_B2 ** ADAM_STEP)

    def body(own_ref, p_ref, r_ref, w_ref, m_ref, v_ref, *rest):
        g_ref, d_ref, nm_ref, nv_ref = rest[-4:]
        g = p_ref[...].astype(F32)
        for k in range(nk):
            g = g + r_ref[k].astype(F32)
        g_ref[...] = g
        nm = ADAM_B1 * m_ref[...] + (1.0 - ADAM_B1) * g
        nv = ADAM_B2 * v_ref[...] + (1.0 - ADAM_B2) * (g * g)
        nm_ref[...] = nm
        nv_ref[...] = nv
        d_ref[...] = -ADAM_LR * ((nm * c1) / (jnp.sqrt(nv * c2) + ADAM_EPS) + ADAM_WD * w_ref[...])

    wblk = pl.BlockSpec((None, None, tr, cols), lambda i, j, own_ref: (layer, i, j, 0))
    full = (w.shape[0], lf, rows, cols)
    sds = jax.ShapeDtypeStruct(full, F32)
    prev = [] if prev is None else [a.reshape(full) for a in prev]
    outs = pl.pallas_call(
        body, grid_spec=pltpu.PrefetchScalarGridSpec(
            num_scalar_prefetch=1, grid=(lf, rows // tr),
            in_specs=[pl.BlockSpec((None, None, tr, cols), lambda i, j, own_ref: (i, own_ref[0], j, 0)),
                      pl.BlockSpec((None, nk, tr, cols), lambda i, j, own_ref: (i, 0, j, 0))] + [wblk] * 3
            + [HBM_SPEC] * len(prev),
            out_specs=[wblk] * 4),
        out_shape=[sds] * 4, input_output_aliases={6 + i: i for i in range(len(prev))},
        compiler_params=_cp(("parallel", "parallel")), name=name)(
            own, p.reshape(lf, np_, rows, cols), r.reshape(lf, nk, rows, cols), w.reshape(full), m.reshape(full),
            v.reshape(full), *prev)
    return [o.reshape(w.shape) for o in outs]


def _sum_slots(x, *, name):
    nd, rows, cols = x.shape
    tr = _pick(rows, (512, 256, 128, 64, 32, 16, 8))

    def body(x_ref, o_ref):
        acc = x_ref[0].astype(F32)
        for j in range(1, nd):
            acc = acc + x_ref[j].astype(F32)
        o_ref[...] = acc

    return pl.pallas_call(
        body, grid=(rows // tr,), in_specs=[pl.BlockSpec((nd, tr, cols), lambda i: (0, i, 0))],
        out_specs=pl.BlockSpec((tr, cols), lambda i: (i, 0)), out_shape=jax.ShapeDtypeStruct((rows, cols), F32),
        compiler_params=_cp(("parallel",)), name=name)(x)


def _pad_rows(a, mult=8):
    r = (-a.shape[0]) % mult
    return jnp.pad(a, ((0, r), (0, 0))) if r else a


def _flat128(a):
    f = a.reshape(-1)
    return jnp.pad(f, (0, (-f.shape[0]) % LANE)).reshape(-1, LANE)


def _unshard(gathered, shape, axis):
    g = gathered.reshape((N_DEV,) + tuple(shape))
    g = jnp.moveaxis(g, 0, axis)
    full = list(shape)
    full[axis] *= N_DEV
    return g.reshape(full)


def _col_shards(full):
    rows, cols = full.shape
    return jnp.moveaxis(full.reshape(rows, N_DEV, cols // N_DEV), 1, 0)


BIG = (("ffn_w_gate", 2), ("ffn_w_up", 2), ("ffn_w_down", 2), ("w_in", 1), ("w_branch", 2), ("w_out", 1))


def kernel(x, ffn_norm, ffn_w_gate, ffn_w_up, ffn_w_down, mix_norm, w_in, b_gate, pool_w, pool_scale, dn_conv, dn_A_log, dn_dt_bias, dn_out_norm, w_branch, w_out, final_norm, loss_target, m_ffn_norm, m_ffn_w_gate, m_ffn_w_up, m_ffn_w_down, m_mix_norm, m_w_in, m_b_gate, m_pool_w, m_pool_scale, m_dn_conv, m_dn_A_log, m_dn_dt_bias, m_dn_out_norm, m_w_branch, m_w_out, m_final_norm, v_ffn_norm, v_ffn_w_gate, v_ffn_w_up, v_ffn_w_down, v_mix_norm, v_w_in, v_b_gate, v_pool_w, v_pool_scale, v_dn_conv, v_dn_A_log, v_dn_dt_bias, v_dn_out_norm, v_w_branch, v_w_out, v_final_norm):
    wts = dict(ffn_norm=ffn_norm, ffn_w_gate=ffn_w_gate, ffn_w_up=ffn_w_up, ffn_w_down=ffn_w_down, mix_norm=mix_norm,
               w_in=w_in, b_gate=b_gate, pool_w=pool_w, pool_scale=pool_scale, dn_conv=dn_conv, dn_A_log=dn_A_log,
               dn_dt_bias=dn_dt_bias, dn_out_norm=dn_out_norm, w_branch=w_branch, w_out=w_out, final_norm=final_norm)
    mom = dict(ffn_norm=m_ffn_norm, ffn_w_gate=m_ffn_w_gate, ffn_w_up=m_ffn_w_up, ffn_w_down=m_ffn_w_down,
               mix_norm=m_mix_norm, w_in=m_w_in, b_gate=m_b_gate, pool_w=m_pool_w, pool_scale=m_pool_scale,
               dn_conv=m_dn_conv, dn_A_log=m_dn_A_log, dn_dt_bias=m_dn_dt_bias, dn_out_norm=m_dn_out_norm,
               w_branch=m_w_branch, w_out=m_w_out, final_norm=m_final_norm)
    var = dict(ffn_norm=v_ffn_norm, ffn_w_gate=v_ffn_w_gate, ffn_w_up=v_ffn_w_up, ffn_w_down=v_ffn_w_down,
               mix_norm=v_mix_norm, w_in=v_w_in, b_gate=v_b_gate, pool_w=v_pool_w, pool_scale=v_pool_scale,
               dn_conv=v_dn_conv, dn_A_log=v_dn_A_log, dn_dt_bias=v_dn_dt_bias, dn_out_norm=v_dn_out_norm,
               w_branch=v_w_branch, w_out=v_w_out, final_norm=v_final_norm)
    nb, s, d = x.shape
    t = nb * s
    me = 4 * lax.axis_index("x") + 2 * lax.axis_index("y") + lax.axis_index("c")

    big = [n for n, _ in BIG]
    nls = [nl - 1 for _, nl in BIG]
    shards = lambda l: [wts[n][l].astype(BF16) for n in big]
    small_sh = jnp.concatenate([_flat128(ffn_norm), _flat128(dn_conv)], axis=0)
    ffn3 = big[:3]
    *pre0, small_g = _gather([wts[n][0, 0].astype(BF16) for n in ffn3] + [small_sh], [0] * 4, name="gather_weights")
    rest0 = [wts[n][0, 1].astype(BF16) for n in ffn3] + [wts[n][0].astype(BF16) for n in big[3:]]
    rest0_nls = [0] * 3 + nls[3:]
    full = [None] * DEPTH

    def mixer_weights(l):
        w_in_full = jnp.moveaxis(full[l]["w_in"], 0, 1).reshape(d, -1)
        w_main = jnp.concatenate([w_in_full[:, :AB_LO], w_in_full[:, AB_HI:]], axis=1)
        w_ab = jnp.pad(w_in_full[:, AB_LO:AB_HI], ((0, 0), (0, LANE - (AB_HI - AB_LO))))
        wb = jnp.moveaxis(full[l]["w_branch"], 1, 2).reshape(3, BW, d)
        return w_main, w_ab, wb, full[l]["w_out"].reshape(d, d)

    nfr = ffn_norm.size // LANE
    ffn_norm_full = _unshard(small_g[:, :nfr], ffn_norm.shape, 2)
    dn_conv_full = _unshard(small_g[:, nfr:], dn_conv.shape, 2)
    pool_w_h = pool_w.astype(BF16)

    xs = x.reshape(t, d)
    saved = []
    for l in range(DEPTH):
        sv = dict(x0=xs)
        if l == 0:
            xs, a0, b0, got = _ffn_fwd(xs, ffn_norm_full[0, 0], *pre0, name="ffn_fwd_gather", gather=(rest0, rest0_nls))
            full[0] = dict(zip(ffn3, zip(pre0, got[:3])), **dict(zip(big[3:], got[3:])))
        else:
            xs, a0, b0, _ = _ffn_fwd(xs, ffn_norm_full[l, 0], full[l]["ffn_w_gate"][0], full[l]["ffn_w_up"][0],
                                     full[l]["ffn_w_down"][0], name="ffn_fwd")
        sv["ab0"] = (a0, b0)
        sv["x1"] = xs
        w_main, w_ab, wb, wo = mixer_weights(l)
        h = _rms_fwd(xs, mix_norm[l], name="mix_rms")
        proj = _mm(h, w_main, out_dtype=BF16, name="proj")
        ab = _mm(h, w_ab, name="proj_ab")
        par = jnp.pad(jnp.stack([dn_A_log[l], dn_dt_bias[l]]), ((0, 6), (0, LANE - NH)))
        gain = dn_out_norm[l].reshape(1, HD)
        psc = pool_scale[l].reshape(1, BW)
        yp = _pool_fwd(proj, pool_w_h[l], psc, nb, s, name="pool_fwd")
        yd, o_pre, states, dn_local, gat = _dn_fwd(proj, ab, dn_conv_full[l], par, gain, nb, s,
                                         name="dn_fwd" if l == DEPTH - 1 else "dn_fwd_gather",
                                         gather=(shards(l + 1)[:3], nls[:3]) if l < DEPTH - 1 else None)
        ys, sb_ctr, gat2 = _sb_fwd(proj, nb, s, name="sb_fwd" if l == DEPTH - 1 else "sb_fwd_gather",
                                   gather=(shards(l + 1)[3:], nls[3:]) if l < DEPTH - 1 else None)
        if l < DEPTH - 1:
            full[l + 1] = dict(zip(big, gat + gat2))
        bg = b_gate[l].reshape(1, 3 * d)
        xs = _merge_fwd(xs, proj, yp, yd, ys, bg, wb, wo, name="merge_fwd")
        sv.update(x2=xs, h=h, proj=proj, ab=ab, par=par, gain=gain, psc=psc, yp=yp, yd=yd, ys=ys, sb_ctr=sb_ctr, o_pre=o_pre,
                  states=states, dn_local=dn_local, bg=bg, w_main=w_main, w_ab=w_ab, wb=wb, wo=wo)
        xs, a1, b1, _ = _ffn_fwd(xs, ffn_norm_full[l, 1], full[l]["ffn_w_gate"][1], full[l]["ffn_w_up"][1],
                                 full[l]["ffn_w_down"][1], name="ffn_fwd")
        sv["ab1"] = (a1, b1)
        saved.append(sv)

    dx, g_final, loss_row = _loss_head(xs, final_norm, loss_target.reshape(t, d), name="loss_head")
    loss = lax.psum(loss_row[0, 0], ("x", "y", "c"))

    gw = {n: [None] * DEPTH for n in ("ffn_norm", "ffn_w_gate", "ffn_w_up", "ffn_w_down", "mix_norm", "w_in", "b_gate",
                                      "pool_w", "pool_scale", "dn_conv", "dn_A_log", "dn_dt_bias", "dn_out_norm",
                                      "w_branch", "w_out")}

    me_i = me.astype(jnp.int32).reshape(1)
    updated = {n: None for n in big}
    pending = None

    def finish_layer(l, own_blocks, arrived, own_slot):
        for n, p, r in zip(big, own_blocks, arrived):
            updated[n] = _sum_adamw(p, r, own_slot, wts[n], mom[n], var[n], l, updated[n], name=f"adamw_{n}_{l}")

    def ffn_back(l, i, x_in, dy):
        dxi, dg, hb, dyh, da, db, sact = _ffn_bwd(x_in, ffn_norm_full[l, i], full[l]["ffn_w_gate"][i],
                                                  full[l]["ffn_w_up"][i], full[l]["ffn_w_down"][i],
                                                  *saved[l][f"ab{i}"], dy, name="ffn_bwd")
        return dxi, dg, (_mm_slots(hb, da, name="dw_gate_up"), _mm_slots(hb, db, name="dw_gate_up"),
                         _mm_slots(sact, dyh, name="dw_down"))

    for l in reversed(range(DEPTH)):
        sv = saved[l]
        dx, dg1, (dwg1, dwu1, dwd1) = ffn_back(l, 1, sv["x2"], dx)
        dyp, dyd, dys, dgl, merged, dxh, dbd, dbg = _merge_bwd(sv["proj"], sv["yp"], sv["yd"], sv["ys"], sv["bg"],
                                                               sv["wb"], sv["wo"], dx, name="merge_bwd")
        gw["w_out"][l] = _mm(merged, dxh, ta=True, out_dtype=BF16, name="dw_out").reshape(N_DEV, d // N_DEV, d)
        gw["w_branch"][l] = jnp.stack([_col_shards(_mm(y, dbd[n], ta=True, out_dtype=BF16, name="dw_branch"))
                                       for n, y in enumerate((sv["yp"], sv["yd"], sv["ys"]))])
        gw["b_gate"][l] = dbg.reshape(3 * d)
        du, dpw, dps = _pool_bwd(sv["proj"], pool_w_h[l], sv["psc"], dyp, nb, s, name="pool_bwd")
        gw["pool_w"][l], gw["pool_scale"][l] = dpw, dps.reshape(BW)
        own = [gw[n][pending] for n in big] if pending is not None else []
        dqr, dkr, dvr, dz, dab4, dcq, dck, dcv, dpar, dgain, arrived_ffn = _dn_bwd(
            sv["proj"], sv["ab"], dn_conv_full[l], sv["par"], sv["gain"], sv["o_pre"], sv["states"], sv["dn_local"],
            dyd, nb, s, name="dn_bwd_scatter" if own else "dn_bwd", scatter=(own[:3], nls[:3]) if own else None)
        gw["dn_conv"][l] = jnp.concatenate([dcq, dck, dcv], axis=1)
        gw["dn_A_log"][l], gw["dn_dt_bias"][l], gw["dn_out_norm"][l] = dpar[:, 0, 0], dpar[:, 1, 0], dgain.reshape(HD)
        dsq, dsk, dsv, arrived_rest = _sb_bwd(sv["proj"], sv["sb_ctr"], dys, nb, s,
                                              name="sb_bwd_scatter" if own else "sb_bwd",
                                              scatter=(own[3:], nls[3:]) if own else None)
        if own:
            finish_layer(pending, own, arrived_ffn + arrived_rest, me_i)
        dab = _sum_heads(dab4, name="sum_heads")
        dproj = jnp.concatenate([du.astype(BF16), dqr.astype(BF16), dkr.astype(BF16), dvr.astype(BF16),
                                 dz.astype(BF16), dsq.astype(BF16), dsk.astype(BF16), dsv.astype(BF16), dgl], axis=1)
        dw_main = _mm(sv["h"], dproj, ta=True, out_dtype=BF16, name="dw_in")
        dw_ab = _mm(sv["h"], dab, ta=True, out_dtype=BF16, name="dw_ab")
        gw["w_in"][l] = _col_shards(jnp.concatenate([dw_main[:, :AB_LO], dw_ab[:, :AB_HI - AB_LO],
                                                     dw_main[:, AB_LO:]], axis=1))
        dh_main = _mm(dproj, sv["w_main"], tb=True, name="dh_mix")
        dh_ab = _mm(dab, sv["w_ab"], tb=True, name="dh_mix_ab")
        dx, dgm = _rms_bwd(sv["x1"], mix_norm[l], dh_main, dh_ab, dx, name="mix_rms_bwd")
        gw["mix_norm"][l] = dgm.reshape(d)
        dx, dg0, (dwg0, dwu0, dwd0) = ffn_back(l, 0, sv["x0"], dx)
        gw["ffn_norm"][l] = jnp.stack([dg0.reshape(d), dg1.reshape(d)])
        gw["ffn_w_gate"][l] = jnp.stack([dwg0, dwg1])
        gw["ffn_w_up"][l] = jnp.stack([dwu0, dwu1])
        gw["ffn_w_down"][l] = jnp.stack([dwd0, dwd1])
        pending = l
    grad_x = dx.reshape(nb, s, d)

    core = lax.axis_index("c").astype(jnp.int32).reshape(1)
    chip = (2 * lax.axis_index("x") + lax.axis_index("y")).astype(jnp.int32).reshape(1)
    last = [gw[n][0] for n in big]
    got = _scatter_pair(last, nls, name="scatter_grads_pair")
    chip_sums = [_pair_add(g, b, core, name="add_pair_" + n) for n, g, b in zip(big, last, got)]
    finish_layer(0, chip_sums, _scatter_chips(chip_sums, nls, name="scatter_grads_chips"), chip)
    grads, delta, new_m, new_v = ({n: updated[n][i] for n in big} for i in range(4))
    gw = {n: jnp.stack(v) for n, v in gw.items() if n not in big}
    gw["final_norm"] = g_final.reshape(d)

    small = ("ffn_norm", "mix_norm", "b_gate", "pool_w", "pool_scale", "dn_conv", "dn_A_log", "dn_dt_bias",
             "dn_out_norm", "final_norm")
    sp = _pad_rows(jnp.concatenate([_flat128(gw[n]) for n in small], axis=0))
    ssum = _sum_slots(_gather([sp], [0], name="gather_small_grads")[0], name="sum_small_grads")
    off = 0
    for n in small:
        r = -(-gw[n].size // LANE)
        g = ssum[off:off + r].reshape(-1)[:gw[n].size].reshape(gw[n].shape)
        off += r
        if n in ("ffn_norm", "dn_conv"):
            w = wts[n].shape[2]
            g = lax.dynamic_slice_in_dim(g, me * w, w, axis=2)
        grads[n] = g

    pk = lambda src: _pad_rows(jnp.concatenate([_flat128(src[n]) for n in small], axis=0))
    dl, nm, nv = _adamw(pk(wts), pk(grads), pk(mom), pk(var), name="adamw_small")
    off = 0
    for n in small:
        r = -(-wts[n].size // LANE)
        for dst, src in ((delta, dl), (new_m, nm), (new_v, nv)):
            dst[n] = src[off:off + r].reshape(-1)[:wts[n].size].reshape(wts[n].shape)
        off += r

    order = ("ffn_norm", "ffn_w_gate", "ffn_w_up", "ffn_w_down", "mix_norm", "w_in", "b_gate", "pool_w", "pool_scale",
             "dn_conv", "dn_A_log", "dn_dt_bias", "dn_out_norm", "w_branch", "w_out", "final_norm")
    return (loss, grad_x, *[grads[n] for n in order], *[delta[n] for n in order], *[new_m[n] for n in order],
            *[new_v[n] for n in order])
```

```python
import functools
import math

import jax
import jax.numpy as jnp
from jax import lax
from jax.experimental import pallas as pl
from jax.experimental.pallas import tpu as pltpu

F32, BF16 = jnp.float32, jnp.bfloat16
D_MODEL, D_FF, DEPTH = 1024, 2816, 4
BW = 512
HD = 128
NH = 4
DN_CHUNK = 64
EPS = 1e-6
N_DEV = 8
LANE = 128
CB_POOL, CB_DNQ, CB_DNK, CB_DNV, CB_DNZ, CB_SBQ, CB_SBK, CB_SBV = 0, 4, 8, 12, 16, 20, 24, 28
CB_GATE = 4
P_MAIN = 7168
AB_LO, AB_HI = 2560, 2568
ADAM_LR, ADAM_B1, ADAM_B2, ADAM_EPS, ADAM_WD, ADAM_STEP = 0.001, 0.9, 0.999, 1e-08, 0.01, 10
VMEM_LIMIT = 56 * 1024 * 1024
HIGHEST = lax.Precision.HIGHEST
NT_DIMS = (((1,), (1,)), ((), ()))
TN_DIMS = (((0,), (0,)), ((), ()))
NN_DIMS = (((1,), (0,)), ((), ()))


def _cp(dims=None, vmem=VMEM_LIMIT):
    return pltpu.CompilerParams(dimension_semantics=dims, vmem_limit_bytes=vmem)


def _pick(n, cands):
    for c in cands:
        if n % c == 0:
            return c
    return n


def _bdot(a, b, dims=NN_DIMS):
    return lax.dot_general(a.astype(BF16), b.astype(BF16), dims, preferred_element_type=F32)


def _hdot(a, b, dims=NN_DIMS):
    return lax.dot_general(a, b, dims, precision=lax.Precision.HIGH, preferred_element_type=F32)


def _split_dot(x, m01):
    hi = x.astype(BF16)
    lo = (x - hi.astype(F32)).astype(BF16)
    return (lax.dot_general(hi, m01, NN_DIMS, preferred_element_type=F32)
            + lax.dot_general(lo, m01, NN_DIMS, preferred_element_type=F32))


def _sigmoid(x):
    return 1.0 / (1.0 + jnp.exp(-x))


def _log_sigmoid(x):
    return jnp.minimum(x, 0.0) - jnp.log1p(jnp.exp(-jnp.abs(x)))


def _softplus(x):
    return jnp.maximum(x, 0.0) + jnp.log1p(jnp.exp(-jnp.abs(x)))


def _shift_down(x, k):
    r = lax.broadcasted_iota(jnp.int32, x.shape, 0)
    return jnp.where(r >= k, pltpu.roll(x, k, 0), 0.0)


def _shift_up(x, k):
    n = x.shape[0]
    r = lax.broadcasted_iota(jnp.int32, x.shape, 0)
    return jnp.where(r < n - k, pltpu.roll(x, n - k, 0), 0.0)


def _mm(a, b, *, ta=False, tb=False, out_dtype=F32, name):
    (kk, m) = a.shape if ta else a.shape[::-1]
    (k2, n) = b.shape[::-1] if tb else b.shape
    assert kk == k2, (a.shape, b.shape, ta, tb)
    bm = _pick(m, (1024, 512, 256, 128))
    bn = _pick(n, (1024, 1408, 512, 256, 128))
    bk = _pick(kk, (1024, 512, 256, 128))
    nk = kk // bk
    dims = (((0 if ta else 1,), (1 if tb else 0,)), ((), ()))

    def body(a_ref, b_ref, o_ref, acc_ref):
        k = pl.program_id(2)

        @pl.when(k == 0)
        def _():
            acc_ref[...] = jnp.zeros_like(acc_ref)

        acc_ref[...] += lax.dot_general(a_ref[...].astype(BF16), b_ref[...].astype(BF16), dims,
                                        preferred_element_type=F32)

        @pl.when(k == nk - 1)
        def _():
            o_ref[...] = acc_ref[...].astype(out_dtype)

    a_spec = (pl.BlockSpec((bk, bm), lambda i, j, k: (k, i)) if ta else pl.BlockSpec((bm, bk), lambda i, j, k: (i, k)))
    b_spec = (pl.BlockSpec((bn, bk), lambda i, j, k: (j, k)) if tb else pl.BlockSpec((bk, bn), lambda i, j, k: (k, j)))
    return pl.pallas_call(
        body, grid=(m // bm, n // bn, nk), in_specs=[a_spec, b_spec],
        out_specs=pl.BlockSpec((bm, bn), lambda i, j, k: (i, j)),
        out_shape=jax.ShapeDtypeStruct((m, n), out_dtype),
        scratch_shapes=[pltpu.VMEM((bm, bn), F32)],
        compiler_params=_cp(("parallel", "parallel", "arbitrary")), name=name)(a, b)


def _mm_slots(a, b, *, name):
    a3, b3 = a.ndim == 3, b.ndim == 3
    ns = a.shape[0] if a3 else b.shape[0]
    m, t = a.shape[-2:]
    n = b.shape[-1]
    bk = _pick(t, (1024, 512, 256, 128))
    nk = t // bk

    def body(a_ref, b_ref, o_ref, acc_ref):
        k = pl.program_id(0)

        @pl.when(k == 0)
        def _():
            acc_ref[...] = jnp.zeros_like(acc_ref)

        for s in range(ns):
            acc_ref[s] += _bdot(a_ref[s] if a3 else a_ref[...], b_ref[s] if b3 else b_ref[...])

        @pl.when(k == nk - 1)
        def _():
            o_ref[...] = acc_ref[...].astype(BF16)

    a_spec = pl.BlockSpec((ns, m, bk), lambda k: (0, 0, k)) if a3 else pl.BlockSpec((m, bk), lambda k: (0, k))
    b_spec = pl.BlockSpec((ns, bk, n), lambda k: (0, k, 0)) if b3 else pl.BlockSpec((bk, n), lambda k: (k, 0))
    return pl.pallas_call(
        body, grid=(nk,), in_specs=[a_spec, b_spec], out_specs=pl.BlockSpec((ns, m, n), lambda k: (0, 0, 0)),
        out_shape=jax.ShapeDtypeStruct((ns, m, n), BF16), scratch_shapes=[pltpu.VMEM((ns, m, n), F32)],
        compiler_params=_cp(("arbitrary",)), name=name)(a, b)


def _rms_stats(x):
    rstd = lax.rsqrt(jnp.mean(x * x, axis=-1, keepdims=True) + EPS)
    return x * rstd, rstd


def _rms_bwd_vals(dh, xhat, rstd, g):
    dxh = dh * g
    dx = rstd * (dxh - xhat * jnp.mean(dxh * xhat, axis=-1, keepdims=True))
    return dx, jnp.sum(dh * xhat, axis=0, keepdims=True)


def _rms_fwd(x, g, *, name):
    t, d = x.shape
    tm = _pick(t, (512, 256, 128))

    def body(x_ref, g_ref, h_ref):
        xhat, _ = _rms_stats(x_ref[...])
        h_ref[...] = (xhat * g_ref[...]).astype(BF16)

    return pl.pallas_call(
        body, grid=(t // tm,),
        in_specs=[pl.BlockSpec((tm, d), lambda i: (i, 0)), pl.BlockSpec((1, d), lambda i: (0, 0))],
        out_specs=pl.BlockSpec((tm, d), lambda i: (i, 0)), out_shape=jax.ShapeDtypeStruct((t, d), BF16),
        compiler_params=_cp(("parallel",)), name=name)(x, g.reshape(1, d))


def _rms_bwd(x, g, dh_a, dh_b, dres, *, name):
    t, d = x.shape
    tm = _pick(t, (512, 256, 128))

    def body(x_ref, g_ref, dha_ref, dhb_ref, dres_ref, dx_ref, dg_ref):
        xhat, rstd = _rms_stats(x_ref[...])
        dx, dg = _rms_bwd_vals(dha_ref[...] + dhb_ref[...], xhat, rstd, g_ref[...])
        dx_ref[...] = dres_ref[...] + dx

        @pl.when(pl.program_id(0) == 0)
        def _():
            dg_ref[...] = jnp.zeros_like(dg_ref)

        dg_ref[...] += dg

    row = pl.BlockSpec((tm, d), lambda i: (i, 0))
    vec = pl.BlockSpec((1, d), lambda i: (0, 0))
    return pl.pallas_call(
        body, grid=(t // tm,), in_specs=[row, vec, row, row, row], out_specs=[row, vec],
        out_shape=[jax.ShapeDtypeStruct((t, d), F32), jax.ShapeDtypeStruct((1, d), F32)],
        compiler_params=_cp(("arbitrary",)), name=name)(x, g.reshape(1, d), dh_a, dh_b, dres)


FFN_TM = 512
FFN_SLOTS = 2
FFN_SLOTS_FWD = 4


def _ffn_fwd(x, g, wg, wu, wd, *, name, gather=None):
    t, d = x.shape
    nf, _, fc = wg.shape
    tm = _pick(t, (FFN_TM, 256, 128))
    slots = FFN_SLOTS_FWD
    gx, gnl = gather if gather else ([], [])

    def body(x_ref, g_ref, wg_ref, wu_ref, wd_ref, o_ref, a_ref, b_ref, h_ref, acc_ref):
        j = pl.program_id(1)

        @pl.when(j == 0)
        def _():
            xhat, _ = _rms_stats(x_ref[...])
            h_ref[...] = (xhat * g_ref[...]).astype(BF16)
            acc_ref[...] = jnp.zeros_like(acc_ref)

        h = h_ref[...]
        part = jnp.zeros((tm, d), F32)
        for q in range(slots):
            a = _bdot(h, wg_ref[q])
            b = _bdot(h, wu_ref[q])
            a_ref[q] = a.astype(BF16)
            b_ref[q] = b.astype(BF16)
            part = part + _bdot(a * _sigmoid(a) * b, wd_ref[q])
        acc_ref[...] += part

        @pl.when(j == nf // slots - 1)
        def _():
            o_ref[...] = x_ref[...] + 0.5 * acc_ref[...]

    row = pl.BlockSpec((tm, d), lambda i, j: (i, 0))
    grid = (t // tm, nf // slots)
    exchange = (_gather_parts, len(gx), gnl) if gx else None
    res = pl.pallas_call(
        _with_exchange(body, 5, 3, 2, exchange, grid), grid=grid,
        in_specs=[row, pl.BlockSpec((1, d), lambda i, j: (0, 0)),
                  pl.BlockSpec((slots, d, fc), lambda i, j: (j, 0, 0)),
                  pl.BlockSpec((slots, d, fc), lambda i, j: (j, 0, 0)),
                  pl.BlockSpec((slots, fc, d), lambda i, j: (j, 0, 0))] + [HBM_SPEC] * len(gx),
        out_specs=[row, pl.BlockSpec((slots, tm, fc), lambda i, j: (j, i, 0)),
                   pl.BlockSpec((slots, tm, fc), lambda i, j: (j, i, 0))] + [HBM_SPEC] * len(gx),
        out_shape=[jax.ShapeDtypeStruct((t, d), F32), jax.ShapeDtypeStruct((nf, t, fc), BF16),
                   jax.ShapeDtypeStruct((nf, t, fc), BF16)] + _gather_shapes(gx, gnl),
        scratch_shapes=[pltpu.VMEM((tm, d), BF16), pltpu.VMEM((tm, d), F32)] + (_comm_sems(len(gx), 7) if gx else []),
        compiler_params=_cp(("arbitrary", "arbitrary")), name=name)(x, g.reshape(1, d), wg, wu, wd, *gx)
    return res[0], res[1], res[2], list(res[3:])


def _ffn_bwd(x, g, wg, wu, wd, a_pre, b_pre, dy, *, name):
    t, d = x.shape
    nf, _, fc = wg.shape
    tm = _pick(t, (FFN_TM, 256, 128))

    def body(x_ref, g_ref, wg_ref, wu_ref, wd_ref, a_ref, b_ref, dy_ref,
             dx_ref, dg_ref, ht_ref, dyh_ref, da_ref, db_ref, st_ref, acc_ref):
        i, j = pl.program_id(0), pl.program_id(1)

        @pl.when(j == 0)
        def _():
            xhat, _ = _rms_stats(x_ref[...])
            ht_ref[...] = (xhat * g_ref[...]).T.astype(BF16)
            dyh_ref[...] = (0.5 * dy_ref[...]).astype(BF16)
            acc_ref[...] = jnp.zeros_like(acc_ref)

        part = jnp.zeros((tm, d), F32)
        for q in range(FFN_SLOTS):
            a = a_ref[q].astype(F32)
            b = b_ref[q].astype(F32)
            sg = _sigmoid(a)
            silu = a * sg
            st_ref[q] = (silu * b).T.astype(BF16)
            ds = _bdot(dyh_ref[...], wd_ref[q], NT_DIMS)
            da = (ds * b * (sg * (1.0 + a * (1.0 - sg)))).astype(BF16)
            db = (ds * silu).astype(BF16)
            da_ref[q] = da
            db_ref[q] = db
            part = part + _bdot(da, wg_ref[q], NT_DIMS) + _bdot(db, wu_ref[q], NT_DIMS)
        acc_ref[...] += part

        @pl.when((i == 0) & (j == 0))
        def _():
            dg_ref[...] = jnp.zeros_like(dg_ref)

        @pl.when(j == nf // FFN_SLOTS - 1)
        def _():
            xhat, rstd = _rms_stats(x_ref[...])
            dx, dg = _rms_bwd_vals(acc_ref[...], xhat, rstd, g_ref[...])
            dx_ref[...] = dy_ref[...] + dx
            dg_ref[...] += dg

    row = pl.BlockSpec((tm, d), lambda i, j: (i, 0))
    vec = pl.BlockSpec((1, d), lambda i, j: (0, 0))
    fblk = pl.BlockSpec((FFN_SLOTS, tm, fc), lambda i, j: (j, i, 0))
    return pl.pallas_call(
        body, grid=(t // tm, nf // FFN_SLOTS),
        in_specs=[row, vec, pl.BlockSpec((FFN_SLOTS, d, fc), lambda i, j: (j, 0, 0)),
                  pl.BlockSpec((FFN_SLOTS, d, fc), lambda i, j: (j, 0, 0)),
                  pl.BlockSpec((FFN_SLOTS, fc, d), lambda i, j: (j, 0, 0)), fblk, fblk, row],
        out_specs=[row, vec, pl.BlockSpec((d, tm), lambda i, j: (0, i)), row, fblk, fblk,
                   pl.BlockSpec((FFN_SLOTS, fc, tm), lambda i, j: (j, 0, i))],
        out_shape=[jax.ShapeDtypeStruct((t, d), F32), jax.ShapeDtypeStruct((1, d), F32),
                   jax.ShapeDtypeStruct((d, t), BF16), jax.ShapeDtypeStruct((t, d), BF16),
                   jax.ShapeDtypeStruct((nf, t, fc), BF16), jax.ShapeDtypeStruct((nf, t, fc), BF16),
                   jax.ShapeDtypeStruct((nf, fc, t), BF16)],
        scratch_shapes=[pltpu.VMEM((tm, d), F32)],
        compiler_params=_cp(("arbitrary", "arbitrary")), name=name)(x, g.reshape(1, d), wg, wu, wd, a_pre, b_pre, dy)


def _pool_core(u, grp):
    s = u.shape[0]
    w2 = u + _shift_down(u, 1)
    w4 = w2 + _shift_down(w2, 2)
    w8 = w4 + _shift_down(w4, 4)
    w16 = w8 + _shift_down(w8, 8)
    wsum = jnp.where(grp == 0, w2, jnp.where(grp == 1, w4, jnp.where(grp == 2, w8, w16)))
    win = jnp.left_shift(2, grp).astype(F32)
    t1 = (lax.broadcasted_iota(jnp.int32, (s, 1), 0) + 1).astype(F32)
    inv = 1.0 / jnp.minimum(t1, win)
    return wsum * inv - u, inv


def _pool_fwd(proj, pool_w, pool_scale, nb, s, *, name):
    def body(u_ref, w_ref, sc_ref, y_ref):
        pooled, _ = _pool_core(u_ref[...].astype(F32), pl.program_id(0))
        y_ref[...] = _bdot(pooled, w_ref[...]) * sc_ref[...]

    return pl.pallas_call(
        body, grid=(NH, nb),
        in_specs=[pl.BlockSpec((s, HD), lambda g, b: (b, CB_POOL + g)),
                  pl.BlockSpec((None, HD, HD), lambda g, b: (g, 0, 0)), pl.BlockSpec((1, HD), lambda g, b: (0, g))],
        out_specs=pl.BlockSpec((s, HD), lambda g, b: (b, g)),
        out_shape=jax.ShapeDtypeStruct((nb * s, BW), F32),
        compiler_params=_cp(("parallel", "parallel")), name=name)(proj, pool_w, pool_scale)


def _pool_bwd(proj, pool_w, pool_scale, dy, nb, s, *, name):
    def body(u_ref, w_ref, sc_ref, dy_ref, du_ref, dw_ref, dsc_ref):
        grp, b = pl.program_id(0), pl.program_id(1)
        pooled, inv = _pool_core(u_ref[...].astype(F32), grp)
        mixed = _bdot(pooled, w_ref[...])
        dy = dy_ref[...]
        dmixed = dy * sc_ref[...]
        dpooled = _bdot(dmixed, w_ref[...], NT_DIMS)
        r = dpooled * inv
        v2 = r + _shift_up(r, 1)
        v4 = v2 + _shift_up(v2, 2)
        v8 = v4 + _shift_up(v4, 4)
        v16 = v8 + _shift_up(v8, 8)
        vsum = jnp.where(grp == 0, v2, jnp.where(grp == 1, v4, jnp.where(grp == 2, v8, v16)))
        du_ref[...] = vsum - dpooled

        @pl.when(b == 0)
        def _():
            dw_ref[...] = jnp.zeros_like(dw_ref)
            dsc_ref[...] = jnp.zeros_like(dsc_ref)

        dw_ref[...] += _bdot(pooled, dmixed, TN_DIMS)
        dsc_ref[...] += jnp.sum(dy * mixed, axis=0, keepdims=True)

    return pl.pallas_call(
        body, grid=(NH, nb),
        in_specs=[pl.BlockSpec((s, HD), lambda g, b: (b, CB_POOL + g)),
                  pl.BlockSpec((None, HD, HD), lambda g, b: (g, 0, 0)), pl.BlockSpec((1, HD), lambda g, b: (0, g)),
                  pl.BlockSpec((s, HD), lambda g, b: (b, g))],
        out_specs=[pl.BlockSpec((s, HD), lambda g, b: (b, g)), pl.BlockSpec((None, HD, HD), lambda g, b: (g, 0, 0)),
                   pl.BlockSpec((1, HD), lambda g, b: (0, g))],
        out_shape=[jax.ShapeDtypeStruct((nb * s, BW), F32), jax.ShapeDtypeStruct((NH, HD, HD), F32),
                   jax.ShapeDtypeStruct((1, BW), F32)],
        compiler_params=_cp(("arbitrary", "arbitrary")), name=name)(proj, pool_w, pool_scale, dy)


SB_BLK = 128


SB_G = 4
SB_KG = SB_G * SB_BLK
SB_Q = 2 * SB_BLK


def _sb_block(qb, kg, q0, k0, diagonal):
    z = _bdot(qb, kg, NT_DIMS) * (HD ** -0.5)
    lsz = _log_sigmoid(z)
    if not diagonal:
        return lsz, lsz - z, None
    row = lax.broadcasted_iota(jnp.int32, z.shape, 0) + q0
    col = lax.broadcasted_iota(jnp.int32, z.shape, 1) + k0
    causal = col < row
    return lsz, jnp.where(causal, lsz - z, 0.0), causal


def _keep(causal, x):
    return x if causal is None else jnp.where(causal, x, 0.0)


def _sub(x, m):
    return x[:, m * SB_BLK:(m + 1) * SB_BLK]


def _sb_tails(lnm, after, ct):
    hi = lnm.astype(BF16)
    lo = (lnm - hi.astype(F32)).astype(BF16)
    tails = [None] * SB_G
    for m in reversed(range(SB_G)):
        tails[m] = (lax.dot_general(_sub(hi, m), after, NN_DIMS, preferred_element_type=F32)
                    + lax.dot_general(_sub(lo, m), after, NN_DIMS, preferred_element_type=F32)) + ct
        ct = ct + jnp.sum(_sub(lnm, m), axis=1, keepdims=True)
    ones = jnp.ones((8, lnm.shape[1]), BF16)
    rows = (lax.dot_general(ones, hi, NT_DIMS, preferred_element_type=F32)
            + lax.dot_general(ones, lo, NT_DIMS, preferred_element_type=F32))
    return jnp.concatenate(tails, axis=1), rows, ct


def _tri01(lower):
    r = lax.broadcasted_iota(jnp.int32, (SB_BLK, SB_BLK), 0)
    c = lax.broadcasted_iota(jnp.int32, (SB_BLK, SB_BLK), 1)
    return jnp.where((r < c) if lower else (r > c), 1.0, 0.0).astype(BF16)


def _split3(x):
    hi = x.astype(BF16)
    mid = (x - hi.astype(F32)).astype(BF16)
    lo = (x - hi.astype(F32) - mid.astype(F32)).astype(BF16)
    return hi, mid, lo


def _rows_to_cols(rows):
    eighth = jnp.full((8, LANE), 0.125, BF16)
    return sum(lax.dot_general(p, eighth, TN_DIMS, preferred_element_type=F32) for p in _split3(rows))


def _sb_fwd(proj, nb, s, *, name, gather=None):
    nq = s // SB_Q
    ng = s // SB_KG
    gx, gnl = gather if gather else ([], [])

    def body(q_ref, k_ref, v_ref, o_ref, ctr_ref):
        after = _tri01(False)

        def qblock(i, _):
            q0 = pl.multiple_of(i * SB_Q, SB_Q)
            qb = q_ref[pl.ds(q0, SB_Q), :]

            def kgroup(g, carry, diagonal):
                acc, ct, ctr = carry
                k0 = pl.multiple_of(g * SB_KG, SB_KG)
                lsz, lnm, causal = _sb_block(qb, k_ref[pl.ds(k0, SB_KG), :], q0, k0, diagonal)
                ctr_ref[i * ng + g] = ctr
                tail, rows, ct = _sb_tails(lnm, after, ct)
                w = _keep(causal, jnp.exp(lsz + tail))
                return acc + _bdot(w, v_ref[pl.ds(k0, SB_KG), :]), ct, ctr + rows

            gd = (i * SB_Q) // SB_KG
            carry = kgroup(gd, (jnp.zeros((SB_Q, HD), F32), jnp.zeros((SB_Q, 1), F32), jnp.zeros((8, SB_Q), F32)), True)
            acc, _, _ = lax.fori_loop(0, gd, lambda jj, c: kgroup(gd - 1 - jj, c, False), carry)
            o_ref[pl.ds(q0, SB_Q), :] = acc
            return 0

        lax.fori_loop(0, nq, qblock, 0)

    def col(cb):
        return pl.BlockSpec((s, HD), lambda b, h: (b, cb + h))

    exchange = (_gather_parts, len(gx), gnl) if gx else None
    res = pl.pallas_call(
        _with_exchange(body, 3, 2, 0, exchange, (nb, NH)), grid=(nb, NH),
        in_specs=[col(CB_SBQ), col(CB_SBK), col(CB_SBV)] + [HBM_SPEC] * len(gx),
        out_specs=[pl.BlockSpec((s, HD), lambda b, h: (b, h)),
                   pl.BlockSpec((None, None, nq * ng, 8, SB_Q), lambda b, h: (b, h, 0, 0, 0))] + [HBM_SPEC] * len(gx),
        out_shape=[jax.ShapeDtypeStruct((nb * s, BW), F32), jax.ShapeDtypeStruct((nb, NH, nq * ng, 8, SB_Q), F32)]
        + _gather_shapes(gx, gnl),
        scratch_shapes=_comm_sems(len(gx), 7) if gx else [],
        compiler_params=_cp(("arbitrary", "arbitrary")), name=name)(proj, proj, proj, *gx)
    return res[0], res[1], list(res[2:])


def _sb_bwd(proj, ctr, dy, nb, s, *, name, scatter=None):
    nq = s // SB_Q
    ng = s // SB_KG
    scale = HD ** -0.5
    gx, gnl = scatter if scatter else ([], [])

    def body(q_ref, k_ref, v_ref, ctr_ref, do_ref, dq_ref, dk_ref, dv_ref):
        after = _tri01(False)
        before = _tri01(True)
        dk_ref[...] = jnp.zeros_like(dk_ref)
        dv_ref[...] = jnp.zeros_like(dv_ref)

        def qblock(i, _):
            q0 = pl.multiple_of(i * SB_Q, SB_Q)
            qb = q_ref[pl.ds(q0, SB_Q), :]
            dob = do_ref[pl.ds(q0, SB_Q), :]

            def kgroup(g, carry, diagonal):
                dq, ce = carry
                k0 = pl.multiple_of(g * SB_KG, SB_KG)
                kg = k_ref[pl.ds(k0, SB_KG), :]
                vg = v_ref[pl.ds(k0, SB_KG), :]
                lsz, lnm, causal = _sb_block(qb, kg, q0, k0, diagonal)
                tail, _, _ = _sb_tails(lnm, after, _rows_to_cols(ctr_ref[i * ng + g])[:, 0:1])
                w = _keep(causal, jnp.exp(lsz + tail))
                e = _bdot(dob, vg, NT_DIMS) * w
                pres = []
                for m in range(SB_G):
                    pres.append(_split_dot(_sub(e, m), before) + ce)
                    ce = ce + jnp.sum(_sub(e, m), axis=1, keepdims=True)
                sig = jnp.exp(lsz)
                dz = _keep(causal, e * (1.0 - sig) - jnp.concatenate(pres, axis=1) * sig) * scale
                dk_ref[pl.ds(k0, SB_KG), :] += _bdot(dz, qb, TN_DIMS)
                dv_ref[pl.ds(k0, SB_KG), :] += _bdot(w, dob, TN_DIMS)
                return dq + _bdot(dz, kg), ce

            gd = (i * SB_Q) // SB_KG
            carry = lax.fori_loop(0, gd, lambda g, c: kgroup(g, c, False),
                                  (jnp.zeros((SB_Q, HD), F32), jnp.zeros((SB_Q, 1), F32)))
            dq, _ = kgroup(gd, carry, True)
            dq_ref[pl.ds(q0, SB_Q), :] = dq
            return 0

        lax.fori_loop(0, nq, qblock, 0)

    def col(cb):
        return pl.BlockSpec((s, HD), lambda b, h: (b, cb + h))

    out = pl.BlockSpec((s, HD), lambda b, h: (b, h))
    sds = jax.ShapeDtypeStruct((nb * s, BW), F32)
    exchange = (_all_to_all_parts, len(gx), gnl) if gx else None
    res = pl.pallas_call(
        _with_exchange(body, 5, 3, 0, exchange, (nb, NH)), grid=(nb, NH),
        in_specs=[col(CB_SBQ), col(CB_SBK), col(CB_SBV),
                  pl.BlockSpec((None, None, nq * ng, 8, SB_Q), lambda b, h: (b, h, 0, 0, 0)), out]
        + [HBM_SPEC] * len(gx),
        out_specs=[out, out, out] + [HBM_SPEC] * len(gx), out_shape=[sds, sds, sds] + _all_to_all_shapes(gx, gnl),
        scratch_shapes=_comm_sems(len(gx), 7) if gx else [],
        compiler_params=_cp(("arbitrary", "arbitrary")), name=name)(proj, proj, proj, ctr, dy, *gx)
    return res[0], res[1], res[2], list(res[3:])


def _make_cdot(dims, dims_da, dims_db, swap_a=False, swap_b=False):
    @jax.custom_vjp
    def f(a, b):
        return _bdot(a, b, dims)

    def fwd(a, b):
        return _bdot(a, b, dims), (a, b)

    def bwd(res, g):
        a, b = res
        da = _bdot(b, g, dims_da) if swap_a else _bdot(g, b, dims_da)
        db = _bdot(g, a, dims_db) if swap_b else _bdot(a, g, dims_db)
        return da, db

    f.defvjp(fwd, bwd)
    return f


_cdot = _make_cdot(NN_DIMS, NT_DIMS, TN_DIMS)
_cdot_nt = _make_cdot(NT_DIMS, NN_DIMS, TN_DIMS, swap_b=True)
_cdot_tn = _make_cdot(TN_DIMS, NT_DIMS, NN_DIMS, swap_a=True)


DN_SUPER = 4 * DN_CHUNK


@jax.custom_vjp
def _unit_lower_inverse(lmat):
    n = lmat.shape[0]
    steps = int(math.log2(DN_CHUNK))
    eye = jnp.where(lax.broadcasted_iota(jnp.int32, (n, n), 0) == lax.broadcasted_iota(jnp.int32, (n, n), 1), 1.0, 0.0)
    inv = eye - lmat
    pw = _hdot(lmat, lmat)
    for it in range(steps - 1):
        inv = inv + _hdot(inv, pw)
        if it < steps - 2:
            pw = _hdot(pw, pw)
    return inv


def _unit_lower_inverse_fwd(lmat):
    inv = _unit_lower_inverse(lmat)
    return inv, inv


def _unit_lower_inverse_bwd(inv, g):
    return (-_hdot(_hdot(inv, g, TN_DIMS), inv, NT_DIMS),)


_unit_lower_inverse.defvjp(_unit_lower_inverse_fwd, _unit_lower_inverse_bwd)


@jax.custom_vjp
def _known_inverse(lmat, inv):
    return inv


def _known_inverse_fwd(lmat, inv):
    return inv, inv


def _known_inverse_bwd(inv, g):
    return -_hdot(_hdot(inv, g, TN_DIMS), inv, NT_DIMS), jnp.zeros_like(inv)


_known_inverse.defvjp(_known_inverse_fwd, _known_inverse_bwd)


def _dn_local(q, k, v, bb, gb, known_inv=None):
    n = q.shape[0]
    r = lax.broadcasted_iota(jnp.int32, (n, n), 0)
    cc = lax.broadcasted_iota(jnp.int32, (n, n), 1)
    shift = int(math.log2(DN_CHUNK))
    same = lax.shift_right_logical(r, shift) == lax.shift_right_logical(cc, shift)
    incl = jnp.where(same, jnp.where(r >= cc, 1.0, 0.0), 0.0)
    strict = jnp.where(same, jnp.where(r > cc, 1.0, 0.0), 0.0)
    gc = _hdot(incl, gb)
    gc_row = _hdot(jnp.full((n, HD), 1.0 / HD, F32), gc, NT_DIMS)
    diff = jnp.concatenate([gc] * (n // HD), axis=1) - gc_row
    decay = incl * jnp.exp(diff * incl)
    kb = k * bb
    lmat = _cdot_nt(kb, k) * (strict * decay)
    egc = jnp.exp(gc)
    inv = _unit_lower_inverse(lmat) if known_inv is None else _known_inverse(lmat, known_inv)
    u = _hdot(inv, v * bb)
    w = _hdot(inv, kb * egc)
    attn = _cdot_nt(q, k) * decay
    gl = _hdot(jnp.where(same, 1.0, 0.0), gb)
    return u, w, attn, q * egc, k * jnp.exp(gl - gc), jnp.exp(gl), inv


def _attn_pairs(attn):
    return jnp.concatenate([attn[:HD, :HD], attn[HD:, HD:]], axis=0)


def _attn_unpairs(a):
    z = jnp.zeros((HD, HD), F32)
    return jnp.concatenate([jnp.concatenate([a[:HD], z], axis=1), jnp.concatenate([z, a[HD:]], axis=1)], axis=0)


def _dn_step(u, w, a, qd, kd, cdrows, state, odd):
    v_new = u - _cdot(w, state)
    z = jnp.zeros_like(v_new)
    o = _cdot(qd, state) + _cdot(a, jnp.concatenate([z, v_new] if odd else [v_new, z], axis=0))
    return o, state * jnp.mean(cdrows, axis=0, keepdims=True) + _cdot_tn(kd, v_new)


def _dn_local_pass(fn, s, ins, outs):
    def step(it, _):
        sl = pl.ds(pl.multiple_of(it * DN_SUPER, DN_SUPER), DN_SUPER)
        res = fn(*[ref[sl, :] for ref in ins])
        for ref, val in zip(outs, res):
            ref[sl, :] = val
        return 0

    lax.fori_loop(0, s // DN_SUPER, step, 0)


def _lane_pick(row, idx):
    lane = lax.broadcasted_iota(jnp.int32, row.shape, 1)
    return jnp.sum(jnp.where(lane == idx, row, 0.0), axis=1, keepdims=True)


def _col_pick(x, idx):
    lane = lax.broadcasted_iota(jnp.int32, x.shape, 1)
    return jnp.sum(jnp.where(lane == idx, x, 0.0), axis=1, keepdims=True)


def _conv_silu(x, w):
    xc = (w[3:4, :] * x + w[2:3, :] * _shift_down(x, 1) + w[1:2, :] * _shift_down(x, 2)
          + w[0:1, :] * _shift_down(x, 3))
    return xc * _sigmoid(xc), xc


def _conv_silu_bwd(x, w, xc, dxs, dw_ref):
    sg = _sigmoid(xc)
    dxc = dxs * (sg * (1.0 + xc * (1.0 - sg)))
    dx = (w[3:4, :] * dxc + w[2:3, :] * _shift_up(dxc, 1) + w[1:2, :] * _shift_up(dxc, 2)
          + w[0:1, :] * _shift_up(dxc, 3))
    dw_ref[3:4, :] += jnp.sum(dxc * x, axis=0, keepdims=True)
    dw_ref[2:3, :] += jnp.sum(dxc * _shift_down(x, 1), axis=0, keepdims=True)
    dw_ref[1:2, :] += jnp.sum(dxc * _shift_down(x, 2), axis=0, keepdims=True)
    dw_ref[0:1, :] += jnp.sum(dxc * _shift_down(x, 3), axis=0, keepdims=True)
    return dx


def _dn_prep(qr_ref, kr_ref, vr_ref, ab_ref, cq_ref, ck_ref, cv_ref, par_ref, head):
    qs, qc = _conv_silu(qr_ref[...].astype(F32), cq_ref[...])
    ks, kc = _conv_silu(kr_ref[...].astype(F32), ck_ref[...])
    vs, vc = _conv_silu(vr_ref[...].astype(F32), cv_ref[...])
    rq = lax.rsqrt(jnp.sum(qs * qs, axis=1, keepdims=True) + EPS)
    rk = lax.rsqrt(jnp.sum(ks * ks, axis=1, keepdims=True) + EPS)
    ab = ab_ref[...]
    a_in = _col_pick(ab, head) + _lane_pick(par_ref[1:2, :], head)
    beta = _sigmoid(_col_pick(ab, NH + head))
    neg_ea = -jnp.exp(_lane_pick(par_ref[0:1, :], head))
    g = neg_ea * _softplus(a_in)
    return dict(q=qs * rq * (HD ** -0.5), k=ks * rk, v=vs, beta=beta, g=g, qs=qs, ks=ks, qc=qc, kc=kc, vc=vc,
                rq=rq, rk=rk, a_in=a_in, neg_ea=neg_ea)


ONE_BUF = pl.Buffered(1)
DN_BWD_VMEM = 62 * 1024 * 1024


def _dn_specs(nb, s):
    def col(cb):
        return pl.BlockSpec((s, HD), lambda h, b: (b, cb + h))

    def conv(cb):
        return pl.BlockSpec((DN_CONV_W, HD), lambda h, b: (0, cb + h))

    return col, conv


DN_CONV_W = 4


def _with_exchange(body, n_in, n_out, n_scratch, exchange, grid):
    if exchange is None:
        return body
    parts_fn, n, nls = exchange

    def wrapped(*refs):
        ins, xs = refs[:n_in], refs[n_in:n_in + n]
        outs, os = refs[n_in + n:n_in + n + n_out], refs[n_in + n + n_out:n_in + 2 * n + n_out]
        rest = refs[n_in + 2 * n + n_out:]
        scratch, sems = rest[:n_scratch], rest[n_scratch:]
        pos = [pl.program_id(k) for k in range(len(grid))]
        first = functools.reduce(jnp.logical_and, [p == 0 for p in pos])
        last = functools.reduce(jnp.logical_and, [p == g - 1 for p, g in zip(pos, grid)])
        start, forward, finish = parts_fn(xs, os, nls, *sems)
        pl.when(first)(start)
        pl.when(last)(forward)
        body(*ins, *outs, *scratch)
        pl.when(last)(finish)

    return wrapped


def _dn_fwd(proj, ab, conv_w, par, gain, nb, s, *, name, gather=None):
    nc = s // DN_CHUNK
    col, conv = _dn_specs(nb, s)
    gx, gnl = gather if gather else ([], [])

    def body(qr_ref, kr_ref, vr_ref, z_ref, ab_ref, cq_ref, ck_ref, cv_ref, par_ref, gain_ref,
             y_ref, o_ref, st_ref, u_ref, w_ref, at_ref, qd_ref, kd_ref, cd_ref, inv_ref, q_s, k_s, v_s, bb_s, gb_s):
        p = _dn_prep(qr_ref, kr_ref, vr_ref, ab_ref, cq_ref, ck_ref, cv_ref, par_ref, pl.program_id(0))
        q_s[...], k_s[...], v_s[...] = p["q"], p["k"], p["v"]
        bb_s[...] = jnp.broadcast_to(p["beta"], (s, HD))
        gb_s[...] = jnp.broadcast_to(p["g"], (s, HD))
        def local(*args):
            u, w, attn, qd, kd, cd, inv = _dn_local(*args)
            return u, w, _attn_pairs(attn), qd, kd, cd, inv

        _dn_local_pass(local, s, [q_s, k_s, v_s, bb_s, gb_s], [u_ref, w_ref, at_ref, qd_ref, kd_ref, cd_ref, inv_ref])

        def chunk_pair(pi, state):
            for odd in (0, 1):
                ci = 2 * pi + odd
                sl = pl.ds(pl.multiple_of(ci * DN_CHUNK, DN_CHUNK), DN_CHUNK)
                st_ref[ci] = state
                o, state = _dn_step(u_ref[sl, :], w_ref[sl, :], at_ref[sl, :], qd_ref[sl, :], kd_ref[sl, :],
                                    cd_ref[sl, :], state, odd)
                o_ref[sl, :] = o
            return state

        lax.fori_loop(0, nc // 2, chunk_pair, jnp.zeros((HD, HD), F32))
        o = o_ref[...]
        z = z_ref[...].astype(F32)
        on = o * lax.rsqrt(jnp.mean(o * o, axis=1, keepdims=True) + EPS) * gain_ref[...]
        y_ref[...] = on * (z * _sigmoid(z))

    out = pl.BlockSpec((s, HD), lambda h, b: (b, h))
    sds = jax.ShapeDtypeStruct((nb * s, BW), F32)
    exchange = (_gather_parts, len(gx), gnl) if gx else None
    res = pl.pallas_call(
        _with_exchange(body, 10, 10, 5, exchange, (NH, nb)), grid=(NH, nb),
        in_specs=[col(CB_DNQ), col(CB_DNK), col(CB_DNV), col(CB_DNZ), pl.BlockSpec((s, LANE), lambda h, b: (b, 0)),
                  conv(0), conv(NH), conv(2 * NH), pl.BlockSpec((8, LANE), lambda h, b: (0, 0)),
                  pl.BlockSpec((1, HD), lambda h, b: (0, 0))] + [HBM_SPEC] * len(gx),
        out_specs=[out, out, pl.BlockSpec((None, None, nc, HD, HD), lambda h, b: (b, h, 0, 0, 0))] + [out] * 6
        + [pl.BlockSpec((s, DN_SUPER), lambda h, b: (b, h))] + [HBM_SPEC] * len(gx),
        out_shape=[sds, sds, jax.ShapeDtypeStruct((nb, NH, nc, HD, HD), F32)] + [sds] * 6
        + [jax.ShapeDtypeStruct((nb * s, NH * DN_SUPER), F32)] + _gather_shapes(gx, gnl),
        scratch_shapes=[pltpu.VMEM((s, HD), F32)] * 5 + (_comm_sems(len(gx), 7) if gx else []),
        compiler_params=_cp(("arbitrary", "arbitrary")), name=name)(
            proj, proj, proj, proj, ab, conv_w, conv_w, conv_w, par, gain, *gx)
    return res[0], res[1], res[2], list(res[3:10]), list(res[10:])


def _dn_bwd(proj, ab, conv_w, par, gain, o_pre, states, local, dy, nb, s, *, name, scatter=None):
    nc = s // DN_CHUNK
    col, conv = _dn_specs(nb, s)
    gx, gnl = scatter if scatter else ([], [])

    def body(qr_ref, kr_ref, vr_ref, z_ref, ab_ref, cq_ref, ck_ref, cv_ref, par_ref, gain_ref, o_ref, st_ref, dy_ref,
             u_hbm, w_hbm, at_hbm, qd_hbm, kd_hbm, cd_hbm, inv_hbm,
             dqr_ref, dkr_ref, dvr_ref, dz_ref, dab_ref, dcq_ref, dck_ref, dcv_ref, dpar_ref, dgain_ref,
             q_s, k_s, v_s, bb_s, gb_s, do_s, u_s, w_s, qd_s, kd_s, at_s, cd_s, load_sems, inv_buf, inv_sems):
        head, b = pl.program_id(0), pl.program_id(1)
        local_refs = [u_s, w_s, at_s, qd_s, kd_s, cd_s]
        loads = [pltpu.make_async_copy(src.at[pl.ds(pl.multiple_of(b * s, s), s), pl.ds(pl.multiple_of(head * HD, HD), HD)],
                                       dst, load_sems.at[i])
                 for i, (src, dst) in enumerate(zip((u_hbm, w_hbm, at_hbm, qd_hbm, kd_hbm, cd_hbm), local_refs))]
        for cp in loads:
            cp.start()
        p = _dn_prep(qr_ref, kr_ref, vr_ref, ab_ref, cq_ref, ck_ref, cv_ref, par_ref, head)
        q_s[...], k_s[...], v_s[...] = p["q"], p["k"], p["v"]
        bb_s[...] = jnp.broadcast_to(p["beta"], (s, HD))
        gb_s[...] = jnp.broadcast_to(p["g"], (s, HD))

        @pl.when(b == 0)
        def _():
            for ref in (dcq_ref, dck_ref, dcv_ref, dpar_ref):
                ref[...] = jnp.zeros_like(ref)

        @pl.when((b == 0) & (head == 0))
        def _():
            dgain_ref[...] = jnp.zeros_like(dgain_ref)

        o, z, dy = o_ref[...], z_ref[...].astype(F32), dy_ref[...]
        rstd = lax.rsqrt(jnp.mean(o * o, axis=1, keepdims=True) + EPS)
        ohat = o * rstd
        sgz = _sigmoid(z)
        dz_ref[...] = dy * (ohat * gain_ref[...]) * (sgz * (1.0 + z * (1.0 - sgz)))
        don = dy * (z * sgz)
        dgain_ref[...] += jnp.sum(don * ohat, axis=0, keepdims=True)
        dxh = don * gain_ref[...]
        do_s[...] = rstd * (dxh - ohat * jnp.mean(dxh * ohat, axis=1, keepdims=True))

        for cp in loads:
            cp.wait()

        def chunk_pair(pr, dstate):
            for odd in (1, 0):
                ci = nc - 1 - 2 * pr - (1 - odd)
                sl = pl.ds(pl.multiple_of(ci * DN_CHUNK, DN_CHUNK), DN_CHUNK)
                _, vjp = jax.vjp(functools.partial(_dn_step, odd=odd), u_s[sl, :], w_s[sl, :], at_s[sl, :],
                                 qd_s[sl, :], kd_s[sl, :], cd_s[sl, :], st_ref[ci])
                du, dw, dat, dqd, dkd, dcd, dstate = vjp((do_s[sl, :], dstate))
                u_s[sl, :], w_s[sl, :], at_s[sl, :], qd_s[sl, :], kd_s[sl, :], cd_s[sl, :] = du, dw, dat, dqd, dkd, dcd
            return dstate

        lax.fori_loop(0, nc // 2, chunk_pair, jnp.zeros((HD, HD), F32))

        def inv_load(it, slot):
            rows = pl.ds(pl.multiple_of(b * s + it * DN_SUPER, DN_SUPER), DN_SUPER)
            cols = pl.ds(pl.multiple_of(head * DN_SUPER, DN_SUPER), DN_SUPER)
            return pltpu.make_async_copy(inv_hbm.at[rows, cols], inv_buf.at[slot], inv_sems.at[slot])

        def local_bwd(it, _):
            slot = lax.rem(it, 2)
            inv_load(it, slot).wait()

            @pl.when(it + 1 < s // DN_SUPER)
            def _():
                inv_load(it + 1, 1 - slot).start()

            sl = pl.ds(pl.multiple_of(it * DN_SUPER, DN_SUPER), DN_SUPER)
            ins = [ref[sl, :] for ref in (q_s, k_s, v_s, bb_s, gb_s)]
            du, dw, dat, dqd, dkd, dcd = [ref[sl, :] for ref in local_refs]
            _, vjp = jax.vjp(lambda *a: _dn_local(*a, known_inv=inv_buf[slot])[:6], *ins)
            dq, dk, dv, dbb, dgb = vjp((du, dw, _attn_unpairs(dat), dqd, dkd, dcd))
            q_s[sl, :], k_s[sl, :], v_s[sl, :] = dq, dk, dv
            bb_s[sl, :] = jnp.broadcast_to(jnp.sum(dbb, axis=1, keepdims=True), (DN_SUPER, HD))
            gb_s[sl, :] = jnp.broadcast_to(jnp.sum(dgb, axis=1, keepdims=True), (DN_SUPER, HD))
            return 0

        inv_load(0, 0).start()
        lax.fori_loop(0, s // DN_SUPER, local_bwd, 0)

        dq, dk, dv = q_s[...], k_s[...], v_s[...]
        qs, ks, rq, rk = p["qs"], p["ks"], p["rq"], p["rk"]
        dqs = (HD ** -0.5) * (rq * dq - qs * (rq * rq * rq) * jnp.sum(dq * qs, axis=1, keepdims=True))
        dks = rk * dk - ks * (rk * rk * rk) * jnp.sum(dk * ks, axis=1, keepdims=True)
        dqr_ref[...] = _conv_silu_bwd(qr_ref[...].astype(F32), cq_ref[...], p["qc"], dqs, dcq_ref)
        dkr_ref[...] = _conv_silu_bwd(kr_ref[...].astype(F32), ck_ref[...], p["kc"], dks, dck_ref)
        dvr_ref[...] = _conv_silu_bwd(vr_ref[...].astype(F32), cv_ref[...], p["vc"], dv, dcv_ref)

        dbeta, dg = bb_s[:, 0:1], gb_s[:, 0:1]
        beta = p["beta"]
        db_logit = dbeta * beta * (1.0 - beta)
        da = dg * p["neg_ea"] * _sigmoid(p["a_in"])
        lane = lax.broadcasted_iota(jnp.int32, (s, LANE), 1)
        dab_ref[...] = jnp.where(lane == head, da, 0.0) + jnp.where(lane == NH + head, db_logit, 0.0)
        dpar_ref[0:1, :] += jnp.broadcast_to(jnp.sum(dg * p["g"], axis=0, keepdims=True), (1, LANE))
        dpar_ref[1:2, :] += jnp.broadcast_to(jnp.sum(da, axis=0, keepdims=True), (1, LANE))

    out = pl.BlockSpec((s, HD), lambda h, b: (b, h))
    in_blk = pl.BlockSpec((s, HD), lambda h, b: (b, h), pipeline_mode=ONE_BUF)
    cblk = pl.BlockSpec((DN_CONV_W, HD), lambda h, b: (0, h))
    sds = jax.ShapeDtypeStruct((nb * s, BW), F32)
    csds = jax.ShapeDtypeStruct((DN_CONV_W, BW), F32)
    exchange = (_all_to_all_parts, len(gx), gnl) if gx else None
    res = pl.pallas_call(
        _with_exchange(body, 20, 10, 15, exchange, (NH, nb)), grid=(NH, nb),
        in_specs=[col(CB_DNQ), col(CB_DNK), col(CB_DNV), col(CB_DNZ),
                  pl.BlockSpec((s, LANE), lambda h, b: (b, 0), pipeline_mode=ONE_BUF),
                  conv(0), conv(NH), conv(2 * NH), pl.BlockSpec((8, LANE), lambda h, b: (0, 0)),
                  pl.BlockSpec((1, HD), lambda h, b: (0, 0)), in_blk,
                  pl.BlockSpec((None, None, nc, HD, HD), lambda h, b: (b, h, 0, 0, 0), pipeline_mode=ONE_BUF), in_blk]
        + [HBM_SPEC] * (7 + len(gx)),
        out_specs=[out, out, out, out, pl.BlockSpec((None, s, LANE), lambda h, b: (h, b, 0)), cblk, cblk, cblk,
                   pl.BlockSpec((None, 8, LANE), lambda h, b: (h, 0, 0)), pl.BlockSpec((1, HD), lambda h, b: (0, 0))]
        + [HBM_SPEC] * len(gx),
        out_shape=[sds, sds, sds, sds, jax.ShapeDtypeStruct((NH, nb * s, LANE), F32), csds, csds, csds,
                   jax.ShapeDtypeStruct((NH, 8, LANE), F32), jax.ShapeDtypeStruct((1, HD), F32)]
        + _all_to_all_shapes(gx, gnl),
        scratch_shapes=[pltpu.VMEM((s, HD), F32)] * 12 + [pltpu.SemaphoreType.DMA((6,)),
                                                           pltpu.VMEM((2, DN_SUPER, DN_SUPER), F32),
                                                           pltpu.SemaphoreType.DMA((2,))]
        + (_comm_sems(len(gx), 7) if gx else []),
        compiler_params=_cp(("arbitrary", "arbitrary"), DN_BWD_VMEM), name=name)(
            proj, proj, proj, proj, ab, conv_w, conv_w, conv_w, par, gain, o_pre, states, dy, *local, *gx)
    return tuple(res[:10]) + (list(res[10:]),)


def _sum_heads(x, *, name):
    nh, t, c = x.shape
    tm = _pick(t, (1024, 512, 256, 128))

    def body(x_ref, o_ref):
        o_ref[...] = (x_ref[0] + x_ref[1] + x_ref[2] + x_ref[3]).astype(BF16)

    return pl.pallas_call(
        body, grid=(t // tm,), in_specs=[pl.BlockSpec((nh, tm, c), lambda i: (0, i, 0))],
        out_specs=pl.BlockSpec((tm, c), lambda i: (i, 0)), out_shape=jax.ShapeDtypeStruct((t, c), BF16),
        compiler_params=_cp(("parallel",)), name=name)(x)


MERGE_TM = 256


def _merge_fwd(x, proj, yp, yd, ys, b_gate, wb, wo, *, name):
    t, d = x.shape
    tm = _pick(t, (MERGE_TM, 128))

    def body(x_ref, g0_ref, g1_ref, g2_ref, yp_ref, yd_ref, ys_ref, bg_ref, wb_ref, wo_ref, o_ref):
        merged = jnp.zeros((tm, d), F32)
        for n, (g_ref, y_ref) in enumerate(((g0_ref, yp_ref), (g1_ref, yd_ref), (g2_ref, ys_ref))):
            gate = _sigmoid(g_ref[...].astype(F32) + bg_ref[:, n * d:(n + 1) * d])
            merged = merged + gate * _bdot(y_ref[...], wb_ref[n])
        o_ref[...] = x_ref[...] + _bdot(merged, wo_ref[...])

    row = pl.BlockSpec((tm, d), lambda i: (i, 0))
    yblk = pl.BlockSpec((tm, BW), lambda i: (i, 0))

    def gl(n):
        return pl.BlockSpec((tm, d), lambda i: (i, CB_GATE + n))

    return pl.pallas_call(
        body, grid=(t // tm,),
        in_specs=[row, gl(0), gl(1), gl(2), yblk, yblk, yblk, pl.BlockSpec((1, 3 * d), lambda i: (0, 0)),
                  pl.BlockSpec((3, BW, d), lambda i: (0, 0, 0)), pl.BlockSpec((d, d), lambda i: (0, 0))],
        out_specs=row, out_shape=jax.ShapeDtypeStruct((t, d), F32),
        compiler_params=_cp(("parallel",)), name=name)(x, proj, proj, proj, yp, yd, ys, b_gate, wb, wo)


def _merge_bwd(proj, yp, yd, ys, b_gate, wb, wo, dx, *, name):
    t, d = dx.shape
    tm = _pick(t, (MERGE_TM, 128))

    def body(g0_ref, g1_ref, g2_ref, yp_ref, yd_ref, ys_ref, bg_ref, wb_ref, wo_ref, dx_ref,
             dyp_ref, dyd_ref, dys_ref, dgl_ref, mg_ref, dxh_ref, dbd_ref, dbg_ref):
        dxh = dx_ref[...].astype(BF16)
        dxh_ref[...] = dxh
        dmerged = _bdot(dxh, wo_ref[...], NT_DIMS)
        merged = jnp.zeros((tm, d), F32)

        @pl.when(pl.program_id(0) == 0)
        def _():
            dbg_ref[...] = jnp.zeros_like(dbg_ref)

        for n, (g_ref, y_ref, dy_ref) in enumerate(((g0_ref, yp_ref, dyp_ref), (g1_ref, yd_ref, dyd_ref),
                                                    (g2_ref, ys_ref, dys_ref))):
            gate = _sigmoid(g_ref[...].astype(F32) + bg_ref[:, n * d:(n + 1) * d])
            bd = _bdot(y_ref[...], wb_ref[n])
            merged = merged + gate * bd
            dgl = dmerged * bd * gate * (1.0 - gate)
            dgl_ref[:, n * d:(n + 1) * d] = dgl.astype(BF16)
            dbg_ref[:, n * d:(n + 1) * d] += jnp.sum(dgl, axis=0, keepdims=True)
            dbd = (dmerged * gate).astype(BF16)
            dbd_ref[n] = dbd
            dy_ref[...] = _bdot(dbd, wb_ref[n], NT_DIMS)
        mg_ref[...] = merged.astype(BF16)

    row = pl.BlockSpec((tm, d), lambda i: (i, 0))
    yblk = pl.BlockSpec((tm, BW), lambda i: (i, 0))
    bgv = pl.BlockSpec((1, 3 * d), lambda i: (0, 0))

    def gl(n):
        return pl.BlockSpec((tm, d), lambda i: (i, CB_GATE + n))

    ysds = jax.ShapeDtypeStruct((t, BW), F32)
    return pl.pallas_call(
        body, grid=(t // tm,),
        in_specs=[gl(0), gl(1), gl(2), yblk, yblk, yblk, bgv,
                  pl.BlockSpec((3, BW, d), lambda i: (0, 0, 0)), pl.BlockSpec((d, d), lambda i: (0, 0)), row],
        out_specs=[yblk, yblk, yblk, pl.BlockSpec((tm, 3 * d), lambda i: (i, 0)), row, row,
                   pl.BlockSpec((3, tm, d), lambda i: (0, i, 0)), bgv],
        out_shape=[ysds, ysds, ysds, jax.ShapeDtypeStruct((t, 3 * d), BF16), jax.ShapeDtypeStruct((t, d), BF16),
                   jax.ShapeDtypeStruct((t, d), BF16), jax.ShapeDtypeStruct((3, t, d), BF16),
                   jax.ShapeDtypeStruct((1, 3 * d), F32)],
        compiler_params=_cp(("arbitrary",)), name=name)(proj, proj, proj, yp, yd, ys, b_gate, wb, wo, dx)


def _loss_head(x, g, target, *, name):
    t, d = x.shape
    tm = _pick(t, (512, 256, 128))

    def body(x_ref, g_ref, t_ref, dx_ref, dg_ref, loss_ref):
        xhat, rstd = _rms_stats(x_ref[...])
        err = xhat * g_ref[...] - t_ref[...]
        dx, dg = _rms_bwd_vals(err * (1.0 / d), xhat, rstd, g_ref[...])
        dx_ref[...] = dx

        @pl.when(pl.program_id(0) == 0)
        def _():
            dg_ref[...] = jnp.zeros_like(dg_ref)
            loss_ref[...] = jnp.zeros_like(loss_ref)

        dg_ref[...] += dg
        part = jnp.sum(jnp.sum(err * err, axis=1, keepdims=True), axis=0, keepdims=True) * (0.5 / d)
        loss_ref[...] += jnp.broadcast_to(part, (1, LANE))

    row = pl.BlockSpec((tm, d), lambda i: (i, 0))
    vec = pl.BlockSpec((1, d), lambda i: (0, 0))
    return pl.pallas_call(
        body, grid=(t // tm,), in_specs=[row, vec, row],
        out_specs=[row, vec, pl.BlockSpec((1, LANE), lambda i: (0, 0))],
        out_shape=[jax.ShapeDtypeStruct((t, d), F32), jax.ShapeDtypeStruct((1, d), F32),
                   jax.ShapeDtypeStruct((1, LANE), F32)],
        compiler_params=_cp(("arbitrary",)), name=name)(x, g.reshape(1, d), target)


def _adamw(w, g, m, v, *, name):
    rows, cols = w.shape
    fits = [c for c in (1024, 704, 512, 352, 256, 128, 64, 32, 16, 8) if c * cols * 4 * 14 <= VMEM_LIMIT // 2]
    tr = _pick(rows, fits)
    c1 = 1.0 / (1.0 - ADAM_B1 ** ADAM_STEP)
    c2 = 1.0 / (1.0 - ADAM_B2 ** ADAM_STEP)

    def body(w_ref, g_ref, m_ref, v_ref, d_ref, nm_ref, nv_ref):
        g = g_ref[...]
        nm = ADAM_B1 * m_ref[...] + (1.0 - ADAM_B1) * g
        nv = ADAM_B2 * v_ref[...] + (1.0 - ADAM_B2) * (g * g)
        nm_ref[...] = nm
        nv_ref[...] = nv
        d_ref[...] = -ADAM_LR * ((nm * c1) / (jnp.sqrt(nv * c2) + ADAM_EPS) + ADAM_WD * w_ref[...])

    blk = pl.BlockSpec((tr, cols), lambda i: (i, 0))
    sds = jax.ShapeDtypeStruct((rows, cols), F32)
    return pl.pallas_call(
        body, grid=(rows // tr,), in_specs=[blk] * 4, out_specs=[blk] * 3, out_shape=[sds] * 3,
        compiler_params=_cp(("parallel",)), name=name)(w, g, m, v)


MESH_ID = pl.DeviceIdType.MESH
HBM_SPEC = pl.BlockSpec(memory_space=pl.ANY)
OTHER_CHIPS = ((1, 0), (0, 1), (1, 1))


def _at_slot(ref, nl, slot):
    return ref.at[(slice(None),) * nl + (slot,)]


def _slotted(shape, nl, slots):
    return tuple(shape[:nl]) + (slots,) + tuple(shape[nl:])


def _flip(v, f):
    return 1 - v if f else v


def _comm_call(body, n, out_shapes, n_remote, args, name):
    return pl.pallas_call(
        body, out_shape=out_shapes, in_specs=[HBM_SPEC] * len(args), out_specs=[HBM_SPEC] * len(out_shapes),
        scratch_shapes=[pltpu.SemaphoreType.DMA((n * n_remote,)), pltpu.SemaphoreType.DMA((n * n_remote,)),
                        pltpu.SemaphoreType.DMA((n * 4,))],
        compiler_params=pltpu.CompilerParams(has_side_effects=True), name=name)(*args)


def _gather(xs, nls, *, name):
    n = len(xs)

    def body(*refs):
        start, forward, finish = _gather_parts(refs[:n], refs[n:2 * n], nls, *refs[2 * n:])
        start()
        forward()
        finish()

    return _comm_call(body, n, _gather_shapes(xs, nls), 7, xs, name)


def _gather_shapes(xs, nls):
    return [jax.ShapeDtypeStruct(_slotted(v.shape, nl, N_DEV), v.dtype) for v, nl in zip(xs, nls)]


def _comm_sems(n, n_remote):
    return [pltpu.SemaphoreType.DMA((n * n_remote,)), pltpu.SemaphoreType.DMA((n * n_remote,)),
            pltpu.SemaphoreType.DMA((n * 4,))]


def _gather_parts(x_refs, o_refs, nls, send_sems, recv_sems, local_sems):
    n = len(x_refs)
    x, y, c = lax.axis_index("x"), lax.axis_index("y"), lax.axis_index("c")
    me, sibling = (x, y, c), (x, y, 1 - c)
    chips = [(_flip(x, fx), _flip(y, fy)) for fx, fy in OTHER_CHIPS]

    def copy(a, k, block, to, src=None):
        dst = _at_slot(o_refs[a], nls[a], 4 * block[0] + 2 * block[1] + block[2])
        return pltpu.make_async_remote_copy(
            src_ref=dst if src is None else src, dst_ref=dst, send_sem=send_sems.at[a * 7 + k],
            recv_sem=recv_sems.at[a * 7 + k], device_id=to, device_id_type=MESH_ID)

    def mine(a):
        return pltpu.make_async_copy(x_refs[a], _at_slot(o_refs[a], nls[a], 4 * x + 2 * y + c), local_sems.at[a])

    def first(a):
        return ([copy(a, 0, me, sibling, src=x_refs[a])]
                + [copy(a, 1 + j, me, (*chip, c), src=x_refs[a]) for j, chip in enumerate(chips)])

    def start():
        for a in range(n):
            mine(a).start()
            for cp in first(a):
                cp.start()

    def forward():
        for j, chip in enumerate(chips):
            for a in range(n):
                copy(a, 1 + j, (*chip, c), me).wait_recv()
                copy(a, 4 + j, (*chip, c), sibling).start()

    def finish():
        for a in range(n):
            copy(a, 0, sibling, me).wait_recv()
            for j, chip in enumerate(chips):
                copy(a, 4 + j, (*chip, 1 - c), me).wait_recv()
        for a in range(n):
            for cp in first(a):
                cp.wait_send()
            for j, chip in enumerate(chips):
                copy(a, 4 + j, (*chip, c), sibling).wait_send()
        for a in range(n):
            mine(a).wait()

    return start, forward, finish


ALL_FLIPS = ((0, 0, 1), (0, 1, 0), (0, 1, 1), (1, 0, 0), (1, 0, 1), (1, 1, 0), (1, 1, 1))


def _all_to_all_parts(g_refs, r_refs, nls, send_sems, recv_sems, local_sems):
    del local_sems
    n = len(g_refs)
    x, y, c = lax.axis_index("x"), lax.axis_index("y"), lax.axis_index("c")

    def copies():
        out = []
        for a in range(n):
            for k, (fx, fy, fc) in enumerate(ALL_FLIPS):
                p = (_flip(x, fx), _flip(y, fy), _flip(c, fc))
                out.append(pltpu.make_async_remote_copy(
                    src_ref=_at_slot(g_refs[a], nls[a], 4 * p[0] + 2 * p[1] + p[2]), dst_ref=_at_slot(r_refs[a], nls[a], k),
                    send_sem=send_sems.at[a * 7 + k], recv_sem=recv_sems.at[a * 7 + k], device_id=p,
                    device_id_type=MESH_ID))
        return out

    def start():
        for cp in copies():
            cp.start()

    def finish():
        cps = copies()
        for cp in cps:
            cp.wait_recv()
        for cp in cps:
            cp.wait_send()

    return start, lambda: None, finish


def _all_to_all_shapes(gs, nls):
    return [jax.ShapeDtypeStruct(_slotted(v.shape[:nl] + v.shape[nl + 1:], nl, 7), v.dtype) for v, nl in zip(gs, nls)]


def _scatter_pair(gs, nls, *, name):
    n = len(gs)

    def body(*refs):
        g_refs, got_refs, (send_sems, recv_sems, _) = refs[:n], refs[n:2 * n], refs[2 * n:]
        x, y, c = lax.axis_index("x"), lax.axis_index("y"), lax.axis_index("c")
        remote = []
        for a in range(n):
            for q in range(4):
                rc = pltpu.make_async_remote_copy(
                    src_ref=_at_slot(g_refs[a], nls[a], 2 * q + 1 - c), dst_ref=_at_slot(got_refs[a], nls[a], q),
                    send_sem=send_sems.at[a * 4 + q], recv_sem=recv_sems.at[a * 4 + q], device_id=(x, y, 1 - c),
                    device_id_type=MESH_ID)
                rc.start()
                remote.append(rc)
        for rc in remote:
            rc.wait_recv()
        for rc in remote:
            rc.wait_send()

    outs = [jax.ShapeDtypeStruct(_slotted(v.shape[:nl] + v.shape[nl + 1:], nl, 4), v.dtype) for v, nl in zip(gs, nls)]
    return _comm_call(body, n, outs, 4, gs, name)


def _scatter_chips(ps, nls, *, name):
    n = len(ps)

    def body(*refs):
        p_refs, r_refs, (send_sems, recv_sems, _) = refs[:n], refs[n:2 * n], refs[2 * n:]
        x, y, c = lax.axis_index("x"), lax.axis_index("y"), lax.axis_index("c")
        remote = []
        for a in range(n):
            for k, (fx, fy) in enumerate(OTHER_CHIPS):
                tx, ty = _flip(x, fx), _flip(y, fy)
                rc = pltpu.make_async_remote_copy(
                    src_ref=_at_slot(p_refs[a], nls[a], 2 * tx + ty), dst_ref=_at_slot(r_refs[a], nls[a], k),
                    send_sem=send_sems.at[a * 3 + k], recv_sem=recv_sems.at[a * 3 + k], device_id=(tx, ty, c),
                    device_id_type=MESH_ID)
                rc.start()
                remote.append(rc)
        for rc in remote:
            rc.wait_recv()
        for rc in remote:
            rc.wait_send()

    outs = [jax.ShapeDtypeStruct(_slotted(v.shape[:nl] + v.shape[nl + 1:], nl, 3), v.dtype) for v, nl in zip(ps, nls)]
    return _comm_call(body, n, outs, 3, ps, name)


def _pair_add(g, got, core, *, name):
    rows, cols = g.shape[-2:]
    lf = math.prod(got.shape[:-3])
    tr = _pick(rows, (1024, 512, 352, 256, 128))

    def body(core_ref, g_ref, got_ref, o_ref):
        o_ref[...] = (g_ref[...].astype(F32) + got_ref[...].astype(F32)).astype(BF16)

    blk = pl.BlockSpec((None, None, tr, cols), lambda i, q, j, core_ref: (i, q, j, 0))
    out = pl.pallas_call(
        body, grid_spec=pltpu.PrefetchScalarGridSpec(
            num_scalar_prefetch=1, grid=(lf, 4, rows // tr),
            in_specs=[pl.BlockSpec((None, None, None, tr, cols), lambda i, q, j, core_ref: (i, q, core_ref[0], j, 0)),
                      blk], out_specs=blk),
        out_shape=jax.ShapeDtypeStruct((lf, 4, rows, cols), BF16),
        compiler_params=_cp(("parallel", "parallel", "parallel")), name=name)(
            core, g.reshape(lf, 4, 2, rows, cols), got.reshape(lf, 4, rows, cols))
    return out.reshape(got.shape)


def _sum_adamw(p, r, own, w, m, v, layer, prev, *, name):
    shape = w.shape[1:]
    rows, cols = shape[-2:]
    lf = math.prod(shape[:-2])
    np_, nk = p.shape[-3], r.shape[-3]
    fits = [c for c in (1024, 512, 352, 256, 128, 64, 32, 16) if c * cols * (7 * 4 + (nk + 1) * 2) * 2 <= VMEM_LIMIT // 2]
    tr = _pick(rows, fits)
    c1 = 1.0 / (1.0 - ADAM_B1 ** ADAM_STEP)
    c2 = 1.0 / (1.0 - ADAM_B2 ** ADAM_STEP)

    def body(own_ref, p_ref, r_ref, w_ref, m_ref, v_ref, *rest):
        g_ref, d_ref, nm_ref, nv_ref = rest[-4:]
        g = p_ref[...].astype(F32)
        for k in range(nk):
            g = g + r_ref[k].astype(F32)
        g_ref[...] = g
        nm = ADAM_B1 * m_ref[...] + (1.0 - ADAM_B1) * g
        nv = ADAM_B2 * v_ref[...] + (1.0 - ADAM_B2) * (g * g)
        nm_ref[...] = nm
        nv_ref[...] = nv
        d_ref[...] = -ADAM_LR * ((nm * c1) / (jnp.sqrt(nv * c2) + ADAM_EPS) + ADAM_WD * w_ref[...])

    wblk = pl.BlockSpec((None, None, tr, cols), lambda i, j, own_ref: (layer, i, j, 0))
    full = (w.shape[0], lf, rows, cols)
    sds = jax.ShapeDtypeStruct(full, F32)
    prev = [] if prev is None else [a.reshape(full) for a in prev]
    outs = pl.pallas_call(
        body, grid_spec=pltpu.PrefetchScalarGridSpec(
            num_scalar_prefetch=1, grid=(lf, rows // tr),
            in_specs=[pl.BlockSpec((None, None, tr, cols), lambda i, j, own_ref: (i, own_ref[0], j, 0)),
                      pl.BlockSpec((None, nk, tr, cols), lambda i, j, own_ref: (i, 0, j, 0))] + [wblk] * 3
            + [HBM_SPEC] * len(prev),
            out_specs=[wblk] * 4),
        out_shape=[sds] * 4, input_output_aliases={6 + i: i for i in range(len(prev))},
        compiler_params=_cp(("parallel", "parallel")), name=name)(
            own, p.reshape(lf, np_, rows, cols), r.reshape(lf, nk, rows, cols), w.reshape(full), m.reshape(full),
            v.reshape(full), *prev)
    return [o.reshape(w.shape) for o in outs]


def _sum_slots(x, *, name):
    nd, rows, cols = x.shape
    tr = _pick(rows, (512, 256, 128, 64, 32, 16, 8))

    def body(x_ref, o_ref):
        acc = x_ref[0].astype(F32)
        for j in range(1, nd):
            acc = acc + x_ref[j].astype(F32)
        o_ref[...] = acc

    return pl.pallas_call(
        body, grid=(rows // tr,), in_specs=[pl.BlockSpec((nd, tr, cols), lambda i: (0, i, 0))],
        out_specs=pl.BlockSpec((tr, cols), lambda i: (i, 0)), out_shape=jax.ShapeDtypeStruct((rows, cols), F32),
        compiler_params=_cp(("parallel",)), name=name)(x)


def _pad_rows(a, mult=8):
    r = (-a.shape[0]) % mult
    return jnp.pad(a, ((0, r), (0, 0))) if r else a


def _flat128(a):
    f = a.reshape(-1)
    return jnp.pad(f, (0, (-f.shape[0]) % LANE)).reshape(-1, LANE)


def _unshard(gathered, shape, axis):
    g = gathered.reshape((N_DEV,) + tuple(shape))
    g = jnp.moveaxis(g, 0, axis)
    full = list(shape)
    full[axis] *= N_DEV
    return g.reshape(full)


def _col_shards(full):
    rows, cols = full.shape
    return jnp.moveaxis(full.reshape(rows, N_DEV, cols // N_DEV), 1, 0)


BIG = (("ffn_w_gate", 2), ("ffn_w_up", 2), ("ffn_w_down", 2), ("w_in", 1), ("w_branch", 2), ("w_out", 1))


def kernel(x, ffn_norm, ffn_w_gate, ffn_w_up, ffn_w_down, mix_norm, w_in, b_gate, pool_w, pool_scale, dn_conv, dn_A_log, dn_dt_bias, dn_out_norm, w_branch, w_out, final_norm, loss_target, m_ffn_norm, m_ffn_w_gate, m_ffn_w_up, m_ffn_w_down, m_mix_norm, m_w_in, m_b_gate, m_pool_w, m_pool_scale, m_dn_conv, m_dn_A_log, m_dn_dt_bias, m_dn_out_norm, m_w_branch, m_w_out, m_final_norm, v_ffn_norm, v_ffn_w_gate, v_ffn_w_up, v_ffn_w_down, v_mix_norm, v_w_in, v_b_gate, v_pool_w, v_pool_scale, v_dn_conv, v_dn_A_log, v_dn_dt_bias, v_dn_out_norm, v_w_branch, v_w_out, v_final_norm):
    wts = dict(ffn_norm=ffn_norm, ffn_w_gate=ffn_w_gate, ffn_w_up=ffn_w_up, ffn_w_down=ffn_w_down, mix_norm=mix_norm,
               w_in=w_in, b_gate=b_gate, pool_w=pool_w, pool_scale=pool_scale, dn_conv=dn_conv, dn_A_log=dn_A_log,
               dn_dt_bias=dn_dt_bias, dn_out_norm=dn_out_norm, w_branch=w_branch, w_out=w_out, final_norm=final_norm)
    mom = dict(ffn_norm=m_ffn_norm, ffn_w_gate=m_ffn_w_gate, ffn_w_up=m_ffn_w_up, ffn_w_down=m_ffn_w_down,
               mix_norm=m_mix_norm, w_in=m_w_in, b_gate=m_b_gate, pool_w=m_pool_w, pool_scale=m_pool_scale,
               dn_conv=m_dn_conv, dn_A_log=m_dn_A_log, dn_dt_bias=m_dn_dt_bias, dn_out_norm=m_dn_out_norm,
               w_branch=m_w_branch, w_out=m_w_out, final_norm=m_final_norm)
    var = dict(ffn_norm=v_ffn_norm, ffn_w_gate=v_ffn_w_gate, ffn_w_up=v_ffn_w_up, ffn_w_down=v_ffn_w_down,
               mix_norm=v_mix_norm, w_in=v_w_in, b_gate=v_b_gate, pool_w=v_pool_w, pool_scale=v_pool_scale,
               dn_conv=v_dn_conv, dn_A_log=v_dn_A_log, dn_dt_bias=v_dn_dt_bias, dn_out_norm=v_dn_out_norm,
               w_branch=v_w_branch, w_out=v_w_out, final_norm=v_final_norm)
    nb, s, d = x.shape
    t = nb * s
    me = 4 * lax.axis_index("x") + 2 * lax.axis_index("y") + lax.axis_index("c")

    big = [n for n, _ in BIG]
    nls = [nl - 1 for _, nl in BIG]
    shards = lambda l: [wts[n][l].astype(BF16) for n in big]
    small_sh = jnp.concatenate([_flat128(ffn_norm), _flat128(dn_conv)], axis=0)
    ffn3 = big[:3]
    *pre0, small_g = _gather([wts[n][0, 0].astype(BF16) for n in ffn3] + [small_sh], [0] * 4, name="gather_weights")
    rest0 = [wts[n][0, 1].astype(BF16) for n in ffn3] + [wts[n][0].astype(BF16) for n in big[3:]]
    rest0_nls = [0] * 3 + nls[3:]
    full = [None] * DEPTH

    def mixer_weights(l):
        w_in_full = jnp.moveaxis(full[l]["w_in"], 0, 1).reshape(d, -1)
        w_main = jnp.concatenate([w_in_full[:, :AB_LO], w_in_full[:, AB_HI:]], axis=1)
        w_ab = jnp.pad(w_in_full[:, AB_LO:AB_HI], ((0, 0), (0, LANE - (AB_HI - AB_LO))))
        wb = jnp.moveaxis(full[l]["w_branch"], 1, 2).reshape(3, BW, d)
        return w_main, w_ab, wb, full[l]["w_out"].reshape(d, d)

    nfr = ffn_norm.size // LANE
    ffn_norm_full = _unshard(small_g[:, :nfr], ffn_norm.shape, 2)
    dn_conv_full = _unshard(small_g[:, nfr:], dn_conv.shape, 2)
    pool_w_h = pool_w.astype(BF16)

    xs = x.reshape(t, d)
    saved = []
    for l in range(DEPTH):
        sv = dict(x0=xs)
        if l == 0:
            xs, a0, b0, got = _ffn_fwd(xs, ffn_norm_full[0, 0], *pre0, name="ffn_fwd_gather", gather=(rest0, rest0_nls))
            full[0] = dict(zip(ffn3, zip(pre0, got[:3])), **dict(zip(big[3:], got[3:])))
        else:
            xs, a0, b0, _ = _ffn_fwd(xs, ffn_norm_full[l, 0], full[l]["ffn_w_gate"][0], full[l]["ffn_w_up"][0],
                                     full[l]["ffn_w_down"][0], name="ffn_fwd")
        sv["ab0"] = (a0, b0)
        sv["x1"] = xs
        w_main, w_ab, wb, wo = mixer_weights(l)
        h = _rms_fwd(xs, mix_norm[l], name="mix_rms")
        proj = _mm(h, w_main, out_dtype=BF16, name="proj")
        ab = _mm(h, w_ab, name="proj_ab")
        par = jnp.pad(jnp.stack([dn_A_log[l], dn_dt_bias[l]]), ((0, 6), (0, LANE - NH)))
        gain = dn_out_norm[l].reshape(1, HD)
        psc = pool_scale[l].reshape(1, BW)
        yp = _pool_fwd(proj, pool_w_h[l], psc, nb, s, name="pool_fwd")
        yd, o_pre, states, dn_local, gat = _dn_fwd(proj, ab, dn_conv_full[l], par, gain, nb, s,
                                         name="dn_fwd" if l == DEPTH - 1 else "dn_fwd_gather",
                                         gather=(shards(l + 1)[:3], nls[:3]) if l < DEPTH - 1 else None)
        ys, sb_ctr, gat2 = _sb_fwd(proj, nb, s, name="sb_fwd" if l == DEPTH - 1 else "sb_fwd_gather",
                                   gather=(shards(l + 1)[3:], nls[3:]) if l < DEPTH - 1 else None)
        if l < DEPTH - 1:
            full[l + 1] = dict(zip(big, gat + gat2))
        bg = b_gate[l].reshape(1, 3 * d)
        xs = _merge_fwd(xs, proj, yp, yd, ys, bg, wb, wo, name="merge_fwd")
        sv.update(x2=xs, h=h, proj=proj, ab=ab, par=par, gain=gain, psc=psc, yp=yp, yd=yd, ys=ys, sb_ctr=sb_ctr, o_pre=o_pre,
                  states=states, dn_local=dn_local, bg=bg, w_main=w_main, w_ab=w_ab, wb=wb, wo=wo)
        xs, a1, b1, _ = _ffn_fwd(xs, ffn_norm_full[l, 1], full[l]["ffn_w_gate"][1], full[l]["ffn_w_up"][1],
                                 full[l]["ffn_w_down"][1], name="ffn_fwd")
        sv["ab1"] = (a1, b1)
        saved.append(sv)

    dx, g_final, loss_row = _loss_head(xs, final_norm, loss_target.reshape(t, d), name="loss_head")
    loss = lax.psum(loss_row[0, 0], ("x", "y", "c"))

    gw = {n: [None] * DEPTH for n in ("ffn_norm", "ffn_w_gate", "ffn_w_up", "ffn_w_down", "mix_norm", "w_in", "b_gate",
                                      "pool_w", "pool_scale", "dn_conv", "dn_A_log", "dn_dt_bias", "dn_out_norm",
                                      "w_branch", "w_out")}

    me_i = me.astype(jnp.int32).reshape(1)
    updated = {n: None for n in big}
    pending = None

    def finish_layer(l, own_blocks, arrived, own_slot):
        for n, p, r in zip(big, own_blocks, arrived):
            updated[n] = _sum_adamw(p, r, own_slot, wts[n], mom[n], var[n], l, updated[n], name=f"adamw_{n}_{l}")

    def ffn_back(l, i, x_in, dy):
        dxi, dg, hb, dyh, da, db, sact = _ffn_bwd(x_in, ffn_norm_full[l, i], full[l]["ffn_w_gate"][i],
                                                  full[l]["ffn_w_up"][i], full[l]["ffn_w_down"][i],
                                                  *saved[l][f"ab{i}"], dy, name="ffn_bwd")
        return dxi, dg, (_mm_slots(hb, da, name="dw_gate_up"), _mm_slots(hb, db, name="dw_gate_up"),
                         _mm_slots(sact, dyh, name="dw_down"))

    for l in reversed(range(DEPTH)):
        sv = saved[l]
        dx, dg1, (dwg1, dwu1, dwd1) = ffn_back(l, 1, sv["x2"], dx)
        dyp, dyd, dys, dgl, merged, dxh, dbd, dbg = _merge_bwd(sv["proj"], sv["yp"], sv["yd"], sv["ys"], sv["bg"],
                                                               sv["wb"], sv["wo"], dx, name="merge_bwd")
        gw["w_out"][l] = _mm(merged, dxh, ta=True, out_dtype=BF16, name="dw_out").reshape(N_DEV, d // N_DEV, d)
        gw["w_branch"][l] = jnp.stack([_col_shards(_mm(y, dbd[n], ta=True, out_dtype=BF16, name="dw_branch"))
                                       for n, y in enumerate((sv["yp"], sv["yd"], sv["ys"]))])
        gw["b_gate"][l] = dbg.reshape(3 * d)
        du, dpw, dps = _pool_bwd(sv["proj"], pool_w_h[l], sv["psc"], dyp, nb, s, name="pool_bwd")
        gw["pool_w"][l], gw["pool_scale"][l] = dpw, dps.reshape(BW)
        own = [gw[n][pending] for n in big] if pending is not None else []
        dqr, dkr, dvr, dz, dab4, dcq, dck, dcv, dpar, dgain, arrived_ffn = _dn_bwd(
            sv["proj"], sv["ab"], dn_conv_full[l], sv["par"], sv["gain"], sv["o_pre"], sv["states"], sv["dn_local"],
            dyd, nb, s, name="dn_bwd_scatter" if own else "dn_bwd", scatter=(own[:3], nls[:3]) if own else None)
        gw["dn_conv"][l] = jnp.concatenate([dcq, dck, dcv], axis=1)
        gw["dn_A_log"][l], gw["dn_dt_bias"][l], gw["dn_out_norm"][l] = dpar[:, 0, 0], dpar[:, 1, 0], dgain.reshape(HD)
        dsq, dsk, dsv, arrived_rest = _sb_bwd(sv["proj"], sv["sb_ctr"], dys, nb, s,
                                              name="sb_bwd_scatter" if own else "sb_bwd",
                                              scatter=(own[3:], nls[3:]) if own else None)
        if own:
            finish_layer(pending, own, arrived_ffn + arrived_rest, me_i)
        dab = _sum_heads(dab4, name="sum_heads")
        dproj = jnp.concatenate([du.astype(BF16), dqr.astype(BF16), dkr.astype(BF16), dvr.astype(BF16),
                                 dz.astype(BF16), dsq.astype(BF16), dsk.astype(BF16), dsv.astype(BF16), dgl], axis=1)
        dw_main = _mm(sv["h"], dproj, ta=True, out_dtype=BF16, name="dw_in")
        dw_ab = _mm(sv["h"], dab, ta=True, out_dtype=BF16, name="dw_ab")
        gw["w_in"][l] = _col_shards(jnp.concatenate([dw_main[:, :AB_LO], dw_ab[:, :AB_HI - AB_LO],
                                                     dw_main[:, AB_LO:]], axis=1))
        dh_main = _mm(dproj, sv["w_main"], tb=True, name="dh_mix")
        dh_ab = _mm(dab, sv["w_ab"], tb=True, name="dh_mix_ab")
        dx, dgm = _rms_bwd(sv["x1"], mix_norm[l], dh_main, dh_ab, dx, name="mix_rms_bwd")
        gw["mix_norm"][l] = dgm.reshape(d)
        dx, dg0, (dwg0, dwu0, dwd0) = ffn_back(l, 0, sv["x0"], dx)
        gw["ffn_norm"][l] = jnp.stack([dg0.reshape(d), dg1.reshape(d)])
        gw["ffn_w_gate"][l] = jnp.stack([dwg0, dwg1])
        gw["ffn_w_up"][l] = jnp.stack([dwu0, dwu1])
        gw["ffn_w_down"][l] = jnp.stack([dwd0, dwd1])
        pending = l
    grad_x = dx.reshape(nb, s, d)

    core = lax.axis_index("c").astype(jnp.int32).reshape(1)
    chip = (2 * lax.axis_index("x") + lax.axis_index("y")).astype(jnp.int32).reshape(1)
    last = [gw[n][0] for n in big]
    got = _scatter_pair(last, nls, name="scatter_grads_pair")
    chip_sums = [_pair_add(g, b, core, name="add_pair_" + n) for n, g, b in zip(big, last, got)]
    finish_layer(0, chip_sums, _scatter_chips(chip_sums, nls, name="scatter_grads_chips"), chip)
    grads, delta, new_m, new_v = ({n: updated[n][i] for n in big} for i in range(4))
    gw = {n: jnp.stack(v) for n, v in gw.items() if n not in big}
    gw["final_norm"] = g_final.reshape(d)

    small = ("ffn_norm", "mix_norm", "b_gate", "pool_w", "pool_scale", "dn_conv", "dn_A_log", "dn_dt_bias",
             "dn_out_norm", "final_norm")
    sp = _pad_rows(jnp.concatenate([_flat128(gw[n]) for n in small], axis=0))
    ssum = _sum_slots(_gather([sp], [0], name="gather_small_grads")[0], name="sum_small_grads")
    off = 0
    for n in small:
        r = -(-gw[n].size // LANE)
        g = ssum[off:off + r].reshape(-1)[:gw[n].size].reshape(gw[n].shape)
        off += r
        if n in ("ffn_norm", "dn_conv"):
            w = wts[n].shape[2]
            g = lax.dynamic_slice_in_dim(g, me * w, w, axis=2)
        grads[n] = g

    pk = lambda src: _pad_rows(jnp.concatenate([_flat128(src[n]) for n in small], axis=0))
    dl, nm, nv = _adamw(pk(wts), pk(grads), pk(mom), pk(var), name="adamw_small")
    off = 0
    for n in small:
        r = -(-wts[n].size // LANE)
        for dst, src in ((delta, dl), (new_m, nm), (new_v, nv)):
            dst[n] = src[off:off + r].reshape(-1)[:wts[n].size].reshape(wts[n].shape)
        off += r

    order = ("ffn_norm", "ffn_w_gate", "ffn_w_up", "ffn_w_down", "mix_norm", "w_in", "b_gate", "pool_w", "pool_scale",
             "dn_conv", "dn_A_log", "dn_dt_bias", "dn_out_norm", "w_branch", "w_out", "final_norm")
    return (loss, grad_x, *[grads[n] for n in order], *[delta[n] for n in order], *[new_m[n] for n in order],
            *[new_v[n] for n in order])
```

```python
import functools
import math

import jax
import jax.numpy as jnp
from jax import lax
from jax.experimental import pallas as pl
from jax.experimental.pallas import tpu as pltpu

F32, BF16 = jnp.float32, jnp.bfloat16
D_MODEL, D_FF, DEPTH = 1024, 2816, 4
BW = 512
HD = 128
NH = 4
DN_CHUNK = 64
EPS = 1e-6
N_DEV = 8
LANE = 128
CB_POOL, CB_DNQ, CB_DNK, CB_DNV, CB_DNZ, CB_SBQ, CB_SBK, CB_SBV = 0, 4, 8, 12, 16, 20, 24, 28
CB_GATE = 4
P_MAIN = 7168
AB_LO, AB_HI = 2560, 2568
ADAM_LR, ADAM_B1, ADAM_B2, ADAM_EPS, ADAM_WD, ADAM_STEP = 0.001, 0.9, 0.999, 1e-08, 0.01, 10
VMEM_LIMIT = 56 * 1024 * 1024
HIGHEST = lax.Precision.HIGHEST
NT_DIMS = (((1,), (1,)), ((), ()))
TN_DIMS = (((0,), (0,)), ((), ()))
NN_DIMS = (((1,), (0,)), ((), ()))


def _cp(dims=None, vmem=VMEM_LIMIT):
    return pltpu.CompilerParams(dimension_semantics=dims, vmem_limit_bytes=vmem)


def _pick(n, cands):
    for c in cands:
        if n % c == 0:
            return c
    return n


def _bdot(a, b, dims=NN_DIMS):
    return lax.dot_general(a.astype(BF16), b.astype(BF16), dims, preferred_element_type=F32)


def _hdot(a, b, dims=NN_DIMS):
    return lax.dot_general(a, b, dims, precision=lax.Precision.HIGH, preferred_element_type=F32)


def _split_dot(x, m01):
    hi = x.astype(BF16)
    lo = (x - hi.astype(F32)).astype(BF16)
    return (lax.dot_general(hi, m01, NN_DIMS, preferred_element_type=F32)
            + lax.dot_general(lo, m01, NN_DIMS, preferred_element_type=F32))


def _sigmoid(x):
    return 1.0 / (1.0 + jnp.exp(-x))


def _log_sigmoid(x):
    return jnp.minimum(x, 0.0) - jnp.log1p(jnp.exp(-jnp.abs(x)))


def _softplus(x):
    return jnp.maximum(x, 0.0) + jnp.log1p(jnp.exp(-jnp.abs(x)))


def _shift_down(x, k):
    r = lax.broadcasted_iota(jnp.int32, x.shape, 0)
    return jnp.where(r >= k, pltpu.roll(x, k, 0), 0.0)


def _shift_up(x, k):
    n = x.shape[0]
    r = lax.broadcasted_iota(jnp.int32, x.shape, 0)
    return jnp.where(r < n - k, pltpu.roll(x, n - k, 0), 0.0)


def _mm(a, b, *, ta=False, tb=False, out_dtype=F32, name):
    (kk, m) = a.shape if ta else a.shape[::-1]
    (k2, n) = b.shape[::-1] if tb else b.shape
    assert kk == k2, (a.shape, b.shape, ta, tb)
    bm = _pick(m, (1024, 512, 256, 128))
    bn = _pick(n, (1024, 1408, 512, 256, 128))
    bk = _pick(kk, (1024, 512, 256, 128))
    nk = kk // bk
    dims = (((0 if ta else 1,), (1 if tb else 0,)), ((), ()))

    def body(a_ref, b_ref, o_ref, acc_ref):
        k = pl.program_id(2)

        @pl.when(k == 0)
        def _():
            acc_ref[...] = jnp.zeros_like(acc_ref)

        acc_ref[...] += lax.dot_general(a_ref[...].astype(BF16), b_ref[...].astype(BF16), dims,
                                        preferred_element_type=F32)

        @pl.when(k == nk - 1)
        def _():
            o_ref[...] = acc_ref[...].astype(out_dtype)

    a_spec = (pl.BlockSpec((bk, bm), lambda i, j, k: (k, i)) if ta else pl.BlockSpec((bm, bk), lambda i, j, k: (i, k)))
    b_spec = (pl.BlockSpec((bn, bk), lambda i, j, k: (j, k)) if tb else pl.BlockSpec((bk, bn), lambda i, j, k: (k, j)))
    return pl.pallas_call(
        body, grid=(m // bm, n // bn, nk), in_specs=[a_spec, b_spec],
        out_specs=pl.BlockSpec((bm, bn), lambda i, j, k: (i, j)),
        out_shape=jax.ShapeDtypeStruct((m, n), out_dtype),
        scratch_shapes=[pltpu.VMEM((bm, bn), F32)],
        compiler_params=_cp(("parallel", "parallel", "arbitrary")), name=name)(a, b)


def _mm_slots(a, b, *, name):
    a3, b3 = a.ndim == 3, b.ndim == 3
    ns = a.shape[0] if a3 else b.shape[0]
    m, t = a.shape[-2:]
    n = b.shape[-1]
    bk = _pick(t, (1024, 512, 256, 128))
    nk = t // bk

    def body(a_ref, b_ref, o_ref, acc_ref):
        k = pl.program_id(0)

        @pl.when(k == 0)
        def _():
            acc_ref[...] = jnp.zeros_like(acc_ref)

        for s in range(ns):
            acc_ref[s] += _bdot(a_ref[s] if a3 else a_ref[...], b_ref[s] if b3 else b_ref[...])

        @pl.when(k == nk - 1)
        def _():
            o_ref[...] = acc_ref[...].astype(BF16)

    a_spec = pl.BlockSpec((ns, m, bk), lambda k: (0, 0, k)) if a3 else pl.BlockSpec((m, bk), lambda k: (0, k))
    b_spec = pl.BlockSpec((ns, bk, n), lambda k: (0, k, 0)) if b3 else pl.BlockSpec((bk, n), lambda k: (k, 0))
    return pl.pallas_call(
        body, grid=(nk,), in_specs=[a_spec, b_spec], out_specs=pl.BlockSpec((ns, m, n), lambda k: (0, 0, 0)),
        out_shape=jax.ShapeDtypeStruct((ns, m, n), BF16), scratch_shapes=[pltpu.VMEM((ns, m, n), F32)],
        compiler_params=_cp(("arbitrary",)), name=name)(a, b)


def _rms_stats(x):
    rstd = lax.rsqrt(jnp.mean(x * x, axis=-1, keepdims=True) + EPS)
    return x * rstd, rstd


def _rms_bwd_vals(dh, xhat, rstd, g):
    dxh = dh * g
    dx = rstd * (dxh - xhat * jnp.mean(dxh * xhat, axis=-1, keepdims=True))
    return dx, jnp.sum(dh * xhat, axis=0, keepdims=True)


def _rms_fwd(x, g, *, name):
    t, d = x.shape
    tm = _pick(t, (512, 256, 128))

    def body(x_ref, g_ref, h_ref):
        xhat, _ = _rms_stats(x_ref[...])
        h_ref[...] = (xhat * g_ref[...]).astype(BF16)

    return pl.pallas_call(
        body, grid=(t // tm,),
        in_specs=[pl.BlockSpec((tm, d), lambda i: (i, 0)), pl.BlockSpec((1, d), lambda i: (0, 0))],
        out_specs=pl.BlockSpec((tm, d), lambda i: (i, 0)), out_shape=jax.ShapeDtypeStruct((t, d), BF16),
        compiler_params=_cp(("parallel",)), name=name)(x, g.reshape(1, d))


def _rms_bwd(x, g, dh_a, dh_b, dres, *, name):
    t, d = x.shape
    tm = _pick(t, (512, 256, 128))

    def body(x_ref, g_ref, dha_ref, dhb_ref, dres_ref, dx_ref, dg_ref):
        xhat, rstd = _rms_stats(x_ref[...])
        dx, dg = _rms_bwd_vals(dha_ref[...] + dhb_ref[...], xhat, rstd, g_ref[...])
        dx_ref[...] = dres_ref[...] + dx

        @pl.when(pl.program_id(0) == 0)
        def _():
            dg_ref[...] = jnp.zeros_like(dg_ref)

        dg_ref[...] += dg

    row = pl.BlockSpec((tm, d), lambda i: (i, 0))
    vec = pl.BlockSpec((1, d), lambda i: (0, 0))
    return pl.pallas_call(
        body, grid=(t // tm,), in_specs=[row, vec, row, row, row], out_specs=[row, vec],
        out_shape=[jax.ShapeDtypeStruct((t, d), F32), jax.ShapeDtypeStruct((1, d), F32)],
        compiler_params=_cp(("arbitrary",)), name=name)(x, g.reshape(1, d), dh_a, dh_b, dres)


FFN_TM = 512
FFN_SLOTS = 2
FFN_SLOTS_FWD = 4


def _ffn_fwd(x, g, wg, wu, wd, *, name, gather=None):
    t, d = x.shape
    nf, _, fc = wg.shape
    tm = _pick(t, (FFN_TM, 256, 128))
    slots = FFN_SLOTS_FWD
    gx, gnl = gather if gather else ([], [])

    def body(x_ref, g_ref, wg_ref, wu_ref, wd_ref, o_ref, a_ref, b_ref, h_ref, acc_ref):
        j = pl.program_id(1)

        @pl.when(j == 0)
        def _():
            xhat, _ = _rms_stats(x_ref[...])
            h_ref[...] = (xhat * g_ref[...]).astype(BF16)
            acc_ref[...] = jnp.zeros_like(acc_ref)

        h = h_ref[...]
        part = jnp.zeros((tm, d), F32)
        for q in range(slots):
            a = _bdot(h, wg_ref[q])
            b = _bdot(h, wu_ref[q])
            a_ref[q] = a.astype(BF16)
            b_ref[q] = b.astype(BF16)
            part = part + _bdot(a * _sigmoid(a) * b, wd_ref[q])
        acc_ref[...] += part

        @pl.when(j == nf // slots - 1)
        def _():
            o_ref[...] = x_ref[...] + 0.5 * acc_ref[...]

    row = pl.BlockSpec((tm, d), lambda i, j: (i, 0))
    grid = (t // tm, nf // slots)
    exchange = (_gather_parts, len(gx), gnl) if gx else None
    res = pl.pallas_call(
        _with_exchange(body, 5, 3, 2, exchange, grid), grid=grid,
        in_specs=[row, pl.BlockSpec((1, d), lambda i, j: (0, 0)),
                  pl.BlockSpec((slots, d, fc), lambda i, j: (j, 0, 0)),
                  pl.BlockSpec((slots, d, fc), lambda i, j: (j, 0, 0)),
                  pl.BlockSpec((slots, fc, d), lambda i, j: (j, 0, 0))] + [HBM_SPEC] * len(gx),
        out_specs=[row, pl.BlockSpec((slots, tm, fc), lambda i, j: (j, i, 0)),
                   pl.BlockSpec((slots, tm, fc), lambda i, j: (j, i, 0))] + [HBM_SPEC] * len(gx),
        out_shape=[jax.ShapeDtypeStruct((t, d), F32), jax.ShapeDtypeStruct((nf, t, fc), BF16),
                   jax.ShapeDtypeStruct((nf, t, fc), BF16)] + _gather_shapes(gx, gnl),
        scratch_shapes=[pltpu.VMEM((tm, d), BF16), pltpu.VMEM((tm, d), F32)] + (_comm_sems(len(gx), 7) if gx else []),
        compiler_params=_cp(("arbitrary", "arbitrary")), name=name)(x, g.reshape(1, d), wg, wu, wd, *gx)
    return res[0], res[1], res[2], list(res[3:])


def _ffn_bwd(x, g, wg, wu, wd, a_pre, b_pre, dy, *, name):
    t, d = x.shape
    nf, _, fc = wg.shape
    tm = _pick(t, (FFN_TM, 256, 128))

    def body(x_ref, g_ref, wg_ref, wu_ref, wd_ref, a_ref, b_ref, dy_ref,
             dx_ref, dg_ref, ht_ref, dyh_ref, da_ref, db_ref, st_ref, acc_ref):
        i, j = pl.program_id(0), pl.program_id(1)

        @pl.when(j == 0)
        def _():
            xhat, _ = _rms_stats(x_ref[...])
            ht_ref[...] = (xhat * g_ref[...]).T.astype(BF16)
            dyh_ref[...] = (0.5 * dy_ref[...]).astype(BF16)
            acc_ref[...] = jnp.zeros_like(acc_ref)

        part = jnp.zeros((tm, d), F32)
        for q in range(FFN_SLOTS):
            a = a_ref[q].astype(F32)
            b = b_ref[q].astype(F32)
            sg = _sigmoid(a)
            silu = a * sg
            st_ref[q] = (silu * b).T.astype(BF16)
            ds = _bdot(dyh_ref[...], wd_ref[q], NT_DIMS)
            da = (ds * b * (sg * (1.0 + a * (1.0 - sg)))).astype(BF16)
            db = (ds * silu).astype(BF16)
            da_ref[q] = da
            db_ref[q] = db
            part = part + _bdot(da, wg_ref[q], NT_DIMS) + _bdot(db, wu_ref[q], NT_DIMS)
        acc_ref[...] += part

        @pl.when((i == 0) & (j == 0))
        def _():
            dg_ref[...] = jnp.zeros_like(dg_ref)

        @pl.when(j == nf // FFN_SLOTS - 1)
        def _():
            xhat, rstd = _rms_stats(x_ref[...])
            dx, dg = _rms_bwd_vals(acc_ref[...], xhat, rstd, g_ref[...])
            dx_ref[...] = dy_ref[...] + dx
            dg_ref[...] += dg

    row = pl.BlockSpec((tm, d), lambda i, j: (i, 0))
    vec = pl.BlockSpec((1, d), lambda i, j: (0, 0))
    fblk = pl.BlockSpec((FFN_SLOTS, tm, fc), lambda i, j: (j, i, 0))
    return pl.pallas_call(
        body, grid=(t // tm, nf // FFN_SLOTS),
        in_specs=[row, vec, pl.BlockSpec((FFN_SLOTS, d, fc), lambda i, j: (j, 0, 0)),
                  pl.BlockSpec((FFN_SLOTS, d, fc), lambda i, j: (j, 0, 0)),
                  pl.BlockSpec((FFN_SLOTS, fc, d), lambda i, j: (j, 0, 0)), fblk, fblk, row],
        out_specs=[row, vec, pl.BlockSpec((d, tm), lambda i, j: (0, i)), row, fblk, fblk,
                   pl.BlockSpec((FFN_SLOTS, fc, tm), lambda i, j: (j, 0, i))],
        out_shape=[jax.ShapeDtypeStruct((t, d), F32), jax.ShapeDtypeStruct((1, d), F32),
                   jax.ShapeDtypeStruct((d, t), BF16), jax.ShapeDtypeStruct((t, d), BF16),
                   jax.ShapeDtypeStruct((nf, t, fc), BF16), jax.ShapeDtypeStruct((nf, t, fc), BF16),
                   jax.ShapeDtypeStruct((nf, fc, t), BF16)],
        scratch_shapes=[pltpu.VMEM((tm, d), F32)],
        compiler_params=_cp(("arbitrary", "arbitrary")), name=name)(x, g.reshape(1, d), wg, wu, wd, a_pre, b_pre, dy)


def _pool_core(u, grp):
    s = u.shape[0]
    w2 = u + _shift_down(u, 1)
    w4 = w2 + _shift_down(w2, 2)
    w8 = w4 + _shift_down(w4, 4)
    w16 = w8 + _shift_down(w8, 8)
    wsum = jnp.where(grp == 0, w2, jnp.where(grp == 1, w4, jnp.where(grp == 2, w8, w16)))
    win = jnp.left_shift(2, grp).astype(F32)
    t1 = (lax.broadcasted_iota(jnp.int32, (s, 1), 0) + 1).astype(F32)
    inv = 1.0 / jnp.minimum(t1, win)
    return wsum * inv - u, inv


def _pool_fwd(proj, pool_w, pool_scale, nb, s, *, name):
    def body(u_ref, w_ref, sc_ref, y_ref):
        pooled, _ = _pool_core(u_ref[...].astype(F32), pl.program_id(0))
        y_ref[...] = _bdot(pooled, w_ref[...]) * sc_ref[...]

    return pl.pallas_call(
        body, grid=(NH, nb),
        in_specs=[pl.BlockSpec((s, HD), lambda g, b: (b, CB_POOL + g)),
                  pl.BlockSpec((None, HD, HD), lambda g, b: (g, 0, 0)), pl.BlockSpec((1, HD), lambda g, b: (0, g))],
        out_specs=pl.BlockSpec((s, HD), lambda g, b: (b, g)),
        out_shape=jax.ShapeDtypeStruct((nb * s, BW), F32),
        compiler_params=_cp(("parallel", "parallel")), name=name)(proj, pool_w, pool_scale)


def _pool_bwd(proj, pool_w, pool_scale, dy, nb, s, *, name):
    def body(u_ref, w_ref, sc_ref, dy_ref, du_ref, dw_ref, dsc_ref):
        grp, b = pl.program_id(0), pl.program_id(1)
        pooled, inv = _pool_core(u_ref[...].astype(F32), grp)
        mixed = _bdot(pooled, w_ref[...])
        dy = dy_ref[...]
        dmixed = dy * sc_ref[...]
        dpooled = _bdot(dmixed, w_ref[...], NT_DIMS)
        r = dpooled * inv
        v2 = r + _shift_up(r, 1)
        v4 = v2 + _shift_up(v2, 2)
        v8 = v4 + _shift_up(v4, 4)
        v16 = v8 + _shift_up(v8, 8)
        vsum = jnp.where(grp == 0, v2, jnp.where(grp == 1, v4, jnp.where(grp == 2, v8, v16)))
        du_ref[...] = (vsum - dpooled).astype(BF16)

        @pl.when(b == 0)
        def _():
            dw_ref[...] = jnp.zeros_like(dw_ref)
            dsc_ref[...] = jnp.zeros_like(dsc_ref)

        dw_ref[...] += _bdot(pooled, dmixed, TN_DIMS)
        dsc_ref[...] += jnp.sum(dy * mixed, axis=0, keepdims=True)

    return pl.pallas_call(
        body, grid=(NH, nb),
        in_specs=[pl.BlockSpec((s, HD), lambda g, b: (b, CB_POOL + g)),
                  pl.BlockSpec((None, HD, HD), lambda g, b: (g, 0, 0)), pl.BlockSpec((1, HD), lambda g, b: (0, g)),
                  pl.BlockSpec((s, HD), lambda g, b: (b, g))],
        out_specs=[pl.BlockSpec((s, HD), lambda g, b: (b, g)), pl.BlockSpec((None, HD, HD), lambda g, b: (g, 0, 0)),
                   pl.BlockSpec((1, HD), lambda g, b: (0, g))],
        out_shape=[jax.ShapeDtypeStruct((nb * s, BW), BF16), jax.ShapeDtypeStruct((NH, HD, HD), F32),
                   jax.ShapeDtypeStruct((1, BW), F32)],
        compiler_params=_cp(("arbitrary", "arbitrary")), name=name)(proj, pool_w, pool_scale, dy)


SB_BLK = 128


SB_G = 4
SB_KG = SB_G * SB_BLK
SB_Q = 2 * SB_BLK


def _sb_block(qb, kg, q0, k0, diagonal):
    z = _bdot(qb, kg, NT_DIMS) * (HD ** -0.5)
    lsz = _log_sigmoid(z)
    if not diagonal:
        return lsz, lsz - z, None
    row = lax.broadcasted_iota(jnp.int32, z.shape, 0) + q0
    col = lax.broadcasted_iota(jnp.int32, z.shape, 1) + k0
    causal = col < row
    return lsz, jnp.where(causal, lsz - z, 0.0), causal


def _keep(causal, x):
    return x if causal is None else jnp.where(causal, x, 0.0)


def _sub(x, m):
    return x[:, m * SB_BLK:(m + 1) * SB_BLK]


def _sb_tails(lnm, after, ct):
    hi = lnm.astype(BF16)
    lo = (lnm - hi.astype(F32)).astype(BF16)
    tails = [None] * SB_G
    for m in reversed(range(SB_G)):
        tails[m] = (lax.dot_general(_sub(hi, m), after, NN_DIMS, preferred_element_type=F32)
                    + lax.dot_general(_sub(lo, m), after, NN_DIMS, preferred_element_type=F32)) + ct
        ct = ct + jnp.sum(_sub(lnm, m), axis=1, keepdims=True)
    ones = jnp.ones((8, lnm.shape[1]), BF16)
    rows = (lax.dot_general(ones, hi, NT_DIMS, preferred_element_type=F32)
            + lax.dot_general(ones, lo, NT_DIMS, preferred_element_type=F32))
    return jnp.concatenate(tails, axis=1), rows, ct


def _tri01(lower):
    r = lax.broadcasted_iota(jnp.int32, (SB_BLK, SB_BLK), 0)
    c = lax.broadcasted_iota(jnp.int32, (SB_BLK, SB_BLK), 1)
    return jnp.where((r < c) if lower else (r > c), 1.0, 0.0).astype(BF16)


def _split3(x):
    hi = x.astype(BF16)
    mid = (x - hi.astype(F32)).astype(BF16)
    lo = (x - hi.astype(F32) - mid.astype(F32)).astype(BF16)
    return hi, mid, lo


def _rows_to_cols(rows):
    eighth = jnp.full((8, LANE), 0.125, BF16)
    return sum(lax.dot_general(p, eighth, TN_DIMS, preferred_element_type=F32) for p in _split3(rows))


def _sb_fwd(proj, nb, s, *, name, gather=None):
    nq = s // SB_Q
    ng = s // SB_KG
    gx, gnl = gather if gather else ([], [])

    def body(q_ref, k_ref, v_ref, o_ref, ctr_ref):
        after = _tri01(False)

        def qblock(i, _):
            q0 = pl.multiple_of(i * SB_Q, SB_Q)
            qb = q_ref[pl.ds(q0, SB_Q), :]

            def kgroup(g, carry, diagonal):
                acc, ct, ctr = carry
                k0 = pl.multiple_of(g * SB_KG, SB_KG)
                lsz, lnm, causal = _sb_block(qb, k_ref[pl.ds(k0, SB_KG), :], q0, k0, diagonal)
                ctr_ref[i * ng + g] = ctr
                tail, rows, ct = _sb_tails(lnm, after, ct)
                w = _keep(causal, jnp.exp(lsz + tail))
                return acc + _bdot(w, v_ref[pl.ds(k0, SB_KG), :]), ct, ctr + rows

            gd = (i * SB_Q) // SB_KG
            carry = kgroup(gd, (jnp.zeros((SB_Q, HD), F32), jnp.zeros((SB_Q, 1), F32), jnp.zeros((8, SB_Q), F32)), True)
            acc, _, _ = lax.fori_loop(0, gd, lambda jj, c: kgroup(gd - 1 - jj, c, False), carry)
            o_ref[pl.ds(q0, SB_Q), :] = acc
            return 0

        lax.fori_loop(0, nq, qblock, 0)

    def col(cb):
        return pl.BlockSpec((s, HD), lambda b, h: (b, cb + h))

    exchange = (_gather_parts, len(gx), gnl) if gx else None
    res = pl.pallas_call(
        _with_exchange(body, 3, 2, 0, exchange, (nb, NH)), grid=(nb, NH),
        in_specs=[col(CB_SBQ), col(CB_SBK), col(CB_SBV)] + [HBM_SPEC] * len(gx),
        out_specs=[pl.BlockSpec((s, HD), lambda b, h: (b, h)),
                   pl.BlockSpec((None, None, nq * ng, 8, SB_Q), lambda b, h: (b, h, 0, 0, 0))] + [HBM_SPEC] * len(gx),
        out_shape=[jax.ShapeDtypeStruct((nb * s, BW), F32), jax.ShapeDtypeStruct((nb, NH, nq * ng, 8, SB_Q), F32)]
        + _gather_shapes(gx, gnl),
        scratch_shapes=_comm_sems(len(gx), 7) if gx else [],
        compiler_params=_cp(("arbitrary", "arbitrary")), name=name)(proj, proj, proj, *gx)
    return res[0], res[1], list(res[2:])


def _sb_bwd(proj, ctr, dy, nb, s, *, name, scatter=None):
    nq = s // SB_Q
    ng = s // SB_KG
    scale = HD ** -0.5
    gx, gnl = scatter if scatter else ([], [])

    def body(q_ref, k_ref, v_ref, ctr_ref, do_ref, dq_ref, dk_ref, dv_ref):
        after = _tri01(False)
        before = _tri01(True)
        dk_ref[...] = jnp.zeros_like(dk_ref)
        dv_ref[...] = jnp.zeros_like(dv_ref)

        def qblock(i, _):
            q0 = pl.multiple_of(i * SB_Q, SB_Q)
            qb = q_ref[pl.ds(q0, SB_Q), :]
            dob = do_ref[pl.ds(q0, SB_Q), :]

            def kgroup(g, carry, diagonal):
                dq, ce = carry
                k0 = pl.multiple_of(g * SB_KG, SB_KG)
                kg = k_ref[pl.ds(k0, SB_KG), :]
                vg = v_ref[pl.ds(k0, SB_KG), :]
                lsz, lnm, causal = _sb_block(qb, kg, q0, k0, diagonal)
                tail, _, _ = _sb_tails(lnm, after, _rows_to_cols(ctr_ref[i * ng + g])[:, 0:1])
                w = _keep(causal, jnp.exp(lsz + tail))
                e = _bdot(dob, vg, NT_DIMS) * w
                pres = []
                for m in range(SB_G):
                    pres.append(_split_dot(_sub(e, m), before) + ce)
                    ce = ce + jnp.sum(_sub(e, m), axis=1, keepdims=True)
                sig = jnp.exp(lsz)
                dz = _keep(causal, e * (1.0 - sig) - jnp.concatenate(pres, axis=1) * sig) * scale
                dk_ref[pl.ds(k0, SB_KG), :] += _bdot(dz, qb, TN_DIMS)
                dv_ref[pl.ds(k0, SB_KG), :] += _bdot(w, dob, TN_DIMS)
                return dq + _bdot(dz, kg), ce

            gd = (i * SB_Q) // SB_KG
            carry = lax.fori_loop(0, gd, lambda g, c: kgroup(g, c, False),
                                  (jnp.zeros((SB_Q, HD), F32), jnp.zeros((SB_Q, 1), F32)))
            dq, _ = kgroup(gd, carry, True)
            dq_ref[pl.ds(q0, SB_Q), :] = dq.astype(BF16)
            return 0

        lax.fori_loop(0, nq, qblock, 0)

    def col(cb):
        return pl.BlockSpec((s, HD), lambda b, h: (b, cb + h))

    out = pl.BlockSpec((s, HD), lambda b, h: (b, h))
    sds = jax.ShapeDtypeStruct((nb * s, BW), F32)
    exchange = (_all_to_all_parts, len(gx), gnl) if gx else None
    res = pl.pallas_call(
        _with_exchange(body, 5, 3, 0, exchange, (nb, NH)), grid=(nb, NH),
        in_specs=[col(CB_SBQ), col(CB_SBK), col(CB_SBV),
                  pl.BlockSpec((None, None, nq * ng, 8, SB_Q), lambda b, h: (b, h, 0, 0, 0)), out]
        + [HBM_SPEC] * len(gx),
        out_specs=[out, out, out] + [HBM_SPEC] * len(gx),
        out_shape=[jax.ShapeDtypeStruct((nb * s, BW), BF16), sds, sds] + _all_to_all_shapes(gx, gnl),
        scratch_shapes=_comm_sems(len(gx), 7) if gx else [],
        compiler_params=_cp(("arbitrary", "arbitrary")), name=name)(proj, proj, proj, ctr, dy, *gx)
    return res[0], res[1], res[2], list(res[3:])


def _make_cdot(dims, dims_da, dims_db, swap_a=False, swap_b=False):
    @jax.custom_vjp
    def f(a, b):
        return _bdot(a, b, dims)

    def fwd(a, b):
        return _bdot(a, b, dims), (a, b)

    def bwd(res, g):
        a, b = res
        da = _bdot(b, g, dims_da) if swap_a else _bdot(g, b, dims_da)
        db = _bdot(g, a, dims_db) if swap_b else _bdot(a, g, dims_db)
        return da, db

    f.defvjp(fwd, bwd)
    return f


_cdot = _make_cdot(NN_DIMS, NT_DIMS, TN_DIMS)
_cdot_nt = _make_cdot(NT_DIMS, NN_DIMS, TN_DIMS, swap_b=True)
_cdot_tn = _make_cdot(TN_DIMS, NT_DIMS, NN_DIMS, swap_a=True)


DN_SUPER = 4 * DN_CHUNK


@jax.custom_vjp
def _unit_lower_inverse(lmat):
    n = lmat.shape[0]
    steps = int(math.log2(DN_CHUNK))
    eye = jnp.where(lax.broadcasted_iota(jnp.int32, (n, n), 0) == lax.broadcasted_iota(jnp.int32, (n, n), 1), 1.0, 0.0)
    inv = eye - lmat
    pw = _hdot(lmat, lmat)
    for it in range(steps - 1):
        inv = inv + _hdot(inv, pw)
        if it < steps - 2:
            pw = _hdot(pw, pw)
    return inv


def _unit_lower_inverse_fwd(lmat):
    inv = _unit_lower_inverse(lmat)
    return inv, inv


def _unit_lower_inverse_bwd(inv, g):
    return (-_hdot(_hdot(inv, g, TN_DIMS), inv, NT_DIMS),)


_unit_lower_inverse.defvjp(_unit_lower_inverse_fwd, _unit_lower_inverse_bwd)


@jax.custom_vjp
def _known_inverse(lmat, inv):
    return inv


def _known_inverse_fwd(lmat, inv):
    return inv, inv


def _known_inverse_bwd(inv, g):
    return -_hdot(_hdot(inv, g, TN_DIMS), inv, NT_DIMS), jnp.zeros_like(inv)


_known_inverse.defvjp(_known_inverse_fwd, _known_inverse_bwd)


def _dn_local(q, k, v, bb, gb, known_inv=None):
    n = q.shape[0]
    r = lax.broadcasted_iota(jnp.int32, (n, n), 0)
    cc = lax.broadcasted_iota(jnp.int32, (n, n), 1)
    shift = int(math.log2(DN_CHUNK))
    same = lax.shift_right_logical(r, shift) == lax.shift_right_logical(cc, shift)
    incl = jnp.where(same, jnp.where(r >= cc, 1.0, 0.0), 0.0)
    strict = jnp.where(same, jnp.where(r > cc, 1.0, 0.0), 0.0)
    gc = _hdot(incl, gb)
    gc_row = _hdot(jnp.full((n, HD), 1.0 / HD, F32), gc, NT_DIMS)
    diff = jnp.concatenate([gc] * (n // HD), axis=1) - gc_row
    decay = incl * jnp.exp(diff * incl)
    kb = k * bb
    lmat = _cdot_nt(kb, k) * (strict * decay)
    egc = jnp.exp(gc)
    inv = _unit_lower_inverse(lmat) if known_inv is None else _known_inverse(lmat, known_inv)
    u = _hdot(inv, v * bb)
    w = _hdot(inv, kb * egc)
    attn = _cdot_nt(q, k) * decay
    gl = _hdot(jnp.where(same, 1.0, 0.0), gb)
    return u, w, attn, q * egc, k * jnp.exp(gl - gc), jnp.exp(gl), inv


def _attn_pairs(attn):
    return jnp.concatenate([attn[:HD, :HD], attn[HD:, HD:]], axis=0)


def _attn_unpairs(a):
    z = jnp.zeros((HD, HD), F32)
    return jnp.concatenate([jnp.concatenate([a[:HD], z], axis=1), jnp.concatenate([z, a[HD:]], axis=1)], axis=0)


def _dn_step(u, w, a, qd, kd, cdrows, state, odd):
    v_new = u - _cdot(w, state)
    z = jnp.zeros_like(v_new)
    o = _cdot(qd, state) + _cdot(a, jnp.concatenate([z, v_new] if odd else [v_new, z], axis=0))
    return o, state * jnp.mean(cdrows, axis=0, keepdims=True) + _cdot_tn(kd, v_new)


def _dn_local_pass(fn, s, ins, outs):
    def step(it, _):
        sl = pl.ds(pl.multiple_of(it * DN_SUPER, DN_SUPER), DN_SUPER)
        res = fn(*[ref[sl, :] for ref in ins])
        for ref, val in zip(outs, res):
            ref[sl, :] = val
        return 0

    lax.fori_loop(0, s // DN_SUPER, step, 0)


def _lane_pick(row, idx):
    lane = lax.broadcasted_iota(jnp.int32, row.shape, 1)
    return jnp.sum(jnp.where(lane == idx, row, 0.0), axis=1, keepdims=True)


def _col_pick(x, idx):
    lane = lax.broadcasted_iota(jnp.int32, x.shape, 1)
    return jnp.sum(jnp.where(lane == idx, x, 0.0), axis=1, keepdims=True)


def _conv_silu(x, w):
    xc = (w[3:4, :] * x + w[2:3, :] * _shift_down(x, 1) + w[1:2, :] * _shift_down(x, 2)
          + w[0:1, :] * _shift_down(x, 3))
    return xc * _sigmoid(xc), xc


def _conv_silu_bwd(x, w, xc, dxs, dw_ref):
    sg = _sigmoid(xc)
    dxc = dxs * (sg * (1.0 + xc * (1.0 - sg)))
    dx = (w[3:4, :] * dxc + w[2:3, :] * _shift_up(dxc, 1) + w[1:2, :] * _shift_up(dxc, 2)
          + w[0:1, :] * _shift_up(dxc, 3))
    dw_ref[3:4, :] += jnp.sum(dxc * x, axis=0, keepdims=True)
    dw_ref[2:3, :] += jnp.sum(dxc * _shift_down(x, 1), axis=0, keepdims=True)
    dw_ref[1:2, :] += jnp.sum(dxc * _shift_down(x, 2), axis=0, keepdims=True)
    dw_ref[0:1, :] += jnp.sum(dxc * _shift_down(x, 3), axis=0, keepdims=True)
    return dx


def _dn_prep(qr_ref, kr_ref, vr_ref, ab_ref, cq_ref, ck_ref, cv_ref, par_ref, head):
    qs, qc = _conv_silu(qr_ref[...].astype(F32), cq_ref[...])
    ks, kc = _conv_silu(kr_ref[...].astype(F32), ck_ref[...])
    vs, vc = _conv_silu(vr_ref[...].astype(F32), cv_ref[...])
    rq = lax.rsqrt(jnp.sum(qs * qs, axis=1, keepdims=True) + EPS)
    rk = lax.rsqrt(jnp.sum(ks * ks, axis=1, keepdims=True) + EPS)
    ab = ab_ref[...]
    a_in = _col_pick(ab, head) + _lane_pick(par_ref[1:2, :], head)
    beta = _sigmoid(_col_pick(ab, NH + head))
    neg_ea = -jnp.exp(_lane_pick(par_ref[0:1, :], head))
    g = neg_ea * _softplus(a_in)
    return dict(q=qs * rq * (HD ** -0.5), k=ks * rk, v=vs, beta=beta, g=g, qs=qs, ks=ks, qc=qc, kc=kc, vc=vc,
                rq=rq, rk=rk, a_in=a_in, neg_ea=neg_ea)


ONE_BUF = pl.Buffered(1)
DN_BWD_VMEM = 62 * 1024 * 1024


def _dn_specs(nb, s):
    def col(cb):
        return pl.BlockSpec((s, HD), lambda h, b: (b, cb + h))

    def conv(cb):
        return pl.BlockSpec((DN_CONV_W, HD), lambda h, b: (0, cb + h))

    return col, conv


DN_CONV_W = 4


def _with_exchange(body, n_in, n_out, n_scratch, exchange, grid):
    if exchange is None:
        return body
    parts_fn, n, nls = exchange

    def wrapped(*refs):
        ins, xs = refs[:n_in], refs[n_in:n_in + n]
        outs, os = refs[n_in + n:n_in + n + n_out], refs[n_in + n + n_out:n_in + 2 * n + n_out]
        rest = refs[n_in + 2 * n + n_out:]
        scratch, sems = rest[:n_scratch], rest[n_scratch:]
        pos = [pl.program_id(k) for k in range(len(grid))]
        first = functools.reduce(jnp.logical_and, [p == 0 for p in pos])
        last = functools.reduce(jnp.logical_and, [p == g - 1 for p, g in zip(pos, grid)])
        start, forward, finish = parts_fn(xs, os, nls, *sems)
        pl.when(first)(start)
        pl.when(last)(forward)
        body(*ins, *outs, *scratch)
        pl.when(last)(finish)

    return wrapped


def _dn_fwd(proj, ab, conv_w, par, gain, nb, s, *, name, gather=None):
    nc = s // DN_CHUNK
    col, conv = _dn_specs(nb, s)
    gx, gnl = gather if gather else ([], [])

    def body(qr_ref, kr_ref, vr_ref, z_ref, ab_ref, cq_ref, ck_ref, cv_ref, par_ref, gain_ref,
             y_ref, o_ref, st_ref, u_ref, w_ref, at_ref, qd_ref, kd_ref, cd_ref, inv_ref, q_s, k_s, v_s, bb_s, gb_s):
        p = _dn_prep(qr_ref, kr_ref, vr_ref, ab_ref, cq_ref, ck_ref, cv_ref, par_ref, pl.program_id(0))
        q_s[...], k_s[...], v_s[...] = p["q"], p["k"], p["v"]
        bb_s[...] = jnp.broadcast_to(p["beta"], (s, HD))
        gb_s[...] = jnp.broadcast_to(p["g"], (s, HD))
        def local(*args):
            u, w, attn, qd, kd, cd, inv = _dn_local(*args)
            return u, w, _attn_pairs(attn), qd, kd, cd, inv

        _dn_local_pass(local, s, [q_s, k_s, v_s, bb_s, gb_s], [u_ref, w_ref, at_ref, qd_ref, kd_ref, cd_ref, inv_ref])

        def chunk_pair(pi, state):
            for odd in (0, 1):
                ci = 2 * pi + odd
                sl = pl.ds(pl.multiple_of(ci * DN_CHUNK, DN_CHUNK), DN_CHUNK)
                st_ref[ci] = state
                o, state = _dn_step(u_ref[sl, :], w_ref[sl, :], at_ref[sl, :], qd_ref[sl, :], kd_ref[sl, :],
                                    cd_ref[sl, :], state, odd)
                o_ref[sl, :] = o
            return state

        lax.fori_loop(0, nc // 2, chunk_pair, jnp.zeros((HD, HD), F32))
        o = o_ref[...]
        z = z_ref[...].astype(F32)
        on = o * lax.rsqrt(jnp.mean(o * o, axis=1, keepdims=True) + EPS) * gain_ref[...]
        y_ref[...] = on * (z * _sigmoid(z))

    out = pl.BlockSpec((s, HD), lambda h, b: (b, h))
    sds = jax.ShapeDtypeStruct((nb * s, BW), F32)
    exchange = (_gather_parts, len(gx), gnl) if gx else None
    res = pl.pallas_call(
        _with_exchange(body, 10, 10, 5, exchange, (NH, nb)), grid=(NH, nb),
        in_specs=[col(CB_DNQ), col(CB_DNK), col(CB_DNV), col(CB_DNZ), pl.BlockSpec((s, LANE), lambda h, b: (b, 0)),
                  conv(0), conv(NH), conv(2 * NH), pl.BlockSpec((8, LANE), lambda h, b: (0, 0)),
                  pl.BlockSpec((1, HD), lambda h, b: (0, 0))] + [HBM_SPEC] * len(gx),
        out_specs=[out, out, pl.BlockSpec((None, None, nc, HD, HD), lambda h, b: (b, h, 0, 0, 0))] + [out] * 6
        + [pl.BlockSpec((s, DN_SUPER), lambda h, b: (b, h))] + [HBM_SPEC] * len(gx),
        out_shape=[sds, sds, jax.ShapeDtypeStruct((nb, NH, nc, HD, HD), F32)] + [sds] * 6
        + [jax.ShapeDtypeStruct((nb * s, NH * DN_SUPER), F32)] + _gather_shapes(gx, gnl),
        scratch_shapes=[pltpu.VMEM((s, HD), F32)] * 5 + (_comm_sems(len(gx), 7) if gx else []),
        compiler_params=_cp(("arbitrary", "arbitrary")), name=name)(
            proj, proj, proj, proj, ab, conv_w, conv_w, conv_w, par, gain, *gx)
    return res[0], res[1], res[2], list(res[3:10]), list(res[10:])


def _dn_bwd(proj, ab, conv_w, par, gain, o_pre, states, local, dy, nb, s, *, name, scatter=None):
    nc = s // DN_CHUNK
    col, conv = _dn_specs(nb, s)
    gx, gnl = scatter if scatter else ([], [])

    def body(qr_ref, kr_ref, vr_ref, z_ref, ab_ref, cq_ref, ck_ref, cv_ref, par_ref, gain_ref, o_ref, st_ref, dy_ref,
             u_hbm, w_hbm, at_hbm, qd_hbm, kd_hbm, cd_hbm, inv_hbm,
             dqr_ref, dkr_ref, dvr_ref, dz_ref, dab_ref, dcq_ref, dck_ref, dcv_ref, dpar_ref, dgain_ref,
             q_s, k_s, v_s, bb_s, gb_s, do_s, u_s, w_s, qd_s, kd_s, at_s, cd_s, load_sems, inv_buf, inv_sems):
        head, b = pl.program_id(0), pl.program_id(1)
        local_refs = [u_s, w_s, at_s, qd_s, kd_s, cd_s]
        loads = [pltpu.make_async_copy(src.at[pl.ds(pl.multiple_of(b * s, s), s), pl.ds(pl.multiple_of(head * HD, HD), HD)],
                                       dst, load_sems.at[i])
                 for i, (src, dst) in enumerate(zip((u_hbm, w_hbm, at_hbm, qd_hbm, kd_hbm, cd_hbm), local_refs))]
        for cp in loads:
            cp.start()
        p = _dn_prep(qr_ref, kr_ref, vr_ref, ab_ref, cq_ref, ck_ref, cv_ref, par_ref, head)
        q_s[...], k_s[...], v_s[...] = p["q"], p["k"], p["v"]
        bb_s[...] = jnp.broadcast_to(p["beta"], (s, HD))
        gb_s[...] = jnp.broadcast_to(p["g"], (s, HD))

        @pl.when(b == 0)
        def _():
            for ref in (dcq_ref, dck_ref, dcv_ref, dpar_ref):
                ref[...] = jnp.zeros_like(ref)

        @pl.when((b == 0) & (head == 0))
        def _():
            dgain_ref[...] = jnp.zeros_like(dgain_ref)

        o, z, dy = o_ref[...], z_ref[...].astype(F32), dy_ref[...]
        rstd = lax.rsqrt(jnp.mean(o * o, axis=1, keepdims=True) + EPS)
        ohat = o * rstd
        sgz = _sigmoid(z)
        dz_ref[...] = (dy * (ohat * gain_ref[...]) * (sgz * (1.0 + z * (1.0 - sgz)))).astype(BF16)
        don = dy * (z * sgz)
        dgain_ref[...] += jnp.sum(don * ohat, axis=0, keepdims=True)
        dxh = don * gain_ref[...]
        do_s[...] = rstd * (dxh - ohat * jnp.mean(dxh * ohat, axis=1, keepdims=True))

        for cp in loads:
            cp.wait()

        def chunk_pair(pr, dstate):
            for odd in (1, 0):
                ci = nc - 1 - 2 * pr - (1 - odd)
                sl = pl.ds(pl.multiple_of(ci * DN_CHUNK, DN_CHUNK), DN_CHUNK)
                _, vjp = jax.vjp(functools.partial(_dn_step, odd=odd), u_s[sl, :], w_s[sl, :], at_s[sl, :],
                                 qd_s[sl, :], kd_s[sl, :], cd_s[sl, :], st_ref[ci])
                du, dw, dat, dqd, dkd, dcd, dstate = vjp((do_s[sl, :], dstate))
                u_s[sl, :], w_s[sl, :], at_s[sl, :], qd_s[sl, :], kd_s[sl, :], cd_s[sl, :] = du, dw, dat, dqd, dkd, dcd
            return dstate

        lax.fori_loop(0, nc // 2, chunk_pair, jnp.zeros((HD, HD), F32))

        def inv_load(it, slot):
            rows = pl.ds(pl.multiple_of(b * s + it * DN_SUPER, DN_SUPER), DN_SUPER)
            cols = pl.ds(pl.multiple_of(head * DN_SUPER, DN_SUPER), DN_SUPER)
            return pltpu.make_async_copy(inv_hbm.at[rows, cols], inv_buf.at[slot], inv_sems.at[slot])

        def local_bwd(it, _):
            slot = lax.rem(it, 2)
            inv_load(it, slot).wait()

            @pl.when(it + 1 < s // DN_SUPER)
            def _():
                inv_load(it + 1, 1 - slot).start()

            sl = pl.ds(pl.multiple_of(it * DN_SUPER, DN_SUPER), DN_SUPER)
            ins = [ref[sl, :] for ref in (q_s, k_s, v_s, bb_s, gb_s)]
            du, dw, dat, dqd, dkd, dcd = [ref[sl, :] for ref in local_refs]
            _, vjp = jax.vjp(lambda *a: _dn_local(*a, known_inv=inv_buf[slot])[:6], *ins)
            dq, dk, dv, dbb, dgb = vjp((du, dw, _attn_unpairs(dat), dqd, dkd, dcd))
            q_s[sl, :], k_s[sl, :], v_s[sl, :] = dq, dk, dv
            bb_s[sl, :] = jnp.broadcast_to(jnp.sum(dbb, axis=1, keepdims=True), (DN_SUPER, HD))
            gb_s[sl, :] = jnp.broadcast_to(jnp.sum(dgb, axis=1, keepdims=True), (DN_SUPER, HD))
            return 0

        inv_load(0, 0).start()
        lax.fori_loop(0, s // DN_SUPER, local_bwd, 0)

        dq, dk, dv = q_s[...], k_s[...], v_s[...]
        qs, ks, rq, rk = p["qs"], p["ks"], p["rq"], p["rk"]
        dqs = (HD ** -0.5) * (rq * dq - qs * (rq * rq * rq) * jnp.sum(dq * qs, axis=1, keepdims=True))
        dks = rk * dk - ks * (rk * rk * rk) * jnp.sum(dk * ks, axis=1, keepdims=True)
        dqr_ref[...] = _conv_silu_bwd(qr_ref[...].astype(F32), cq_ref[...], p["qc"], dqs, dcq_ref).astype(BF16)
        dkr_ref[...] = _conv_silu_bwd(kr_ref[...].astype(F32), ck_ref[...], p["kc"], dks, dck_ref).astype(BF16)
        dvr_ref[...] = _conv_silu_bwd(vr_ref[...].astype(F32), cv_ref[...], p["vc"], dv, dcv_ref).astype(BF16)

        dbeta, dg = bb_s[:, 0:1], gb_s[:, 0:1]
        beta = p["beta"]
        db_logit = dbeta * beta * (1.0 - beta)
        da = dg * p["neg_ea"] * _sigmoid(p["a_in"])
        lane = lax.broadcasted_iota(jnp.int32, (s, LANE), 1)
        dab_ref[...] = jnp.where(lane == head, da, 0.0) + jnp.where(lane == NH + head, db_logit, 0.0)
        dpar_ref[0:1, :] += jnp.broadcast_to(jnp.sum(dg * p["g"], axis=0, keepdims=True), (1, LANE))
        dpar_ref[1:2, :] += jnp.broadcast_to(jnp.sum(da, axis=0, keepdims=True), (1, LANE))

    out = pl.BlockSpec((s, HD), lambda h, b: (b, h))
    in_blk = pl.BlockSpec((s, HD), lambda h, b: (b, h), pipeline_mode=ONE_BUF)
    cblk = pl.BlockSpec((DN_CONV_W, HD), lambda h, b: (0, h))
    sds = jax.ShapeDtypeStruct((nb * s, BW), F32)
    csds = jax.ShapeDtypeStruct((DN_CONV_W, BW), F32)
    exchange = (_all_to_all_parts, len(gx), gnl) if gx else None
    res = pl.pallas_call(
        _with_exchange(body, 20, 10, 15, exchange, (NH, nb)), grid=(NH, nb),
        in_specs=[col(CB_DNQ), col(CB_DNK), col(CB_DNV), col(CB_DNZ),
                  pl.BlockSpec((s, LANE), lambda h, b: (b, 0), pipeline_mode=ONE_BUF),
                  conv(0), conv(NH), conv(2 * NH), pl.BlockSpec((8, LANE), lambda h, b: (0, 0)),
                  pl.BlockSpec((1, HD), lambda h, b: (0, 0)), in_blk,
                  pl.BlockSpec((None, None, nc, HD, HD), lambda h, b: (b, h, 0, 0, 0), pipeline_mode=ONE_BUF), in_blk]
        + [HBM_SPEC] * (7 + len(gx)),
        out_specs=[out, out, out, out, pl.BlockSpec((None, s, LANE), lambda h, b: (h, b, 0)), cblk, cblk, cblk,
                   pl.BlockSpec((None, 8, LANE), lambda h, b: (h, 0, 0)), pl.BlockSpec((1, HD), lambda h, b: (0, 0))]
        + [HBM_SPEC] * len(gx),
        out_shape=[jax.ShapeDtypeStruct((nb * s, BW), BF16)] * 4 + [jax.ShapeDtypeStruct((NH, nb * s, LANE), F32), csds, csds, csds,
                   jax.ShapeDtypeStruct((NH, 8, LANE), F32), jax.ShapeDtypeStruct((1, HD), F32)]
        + _all_to_all_shapes(gx, gnl),
        scratch_shapes=[pltpu.VMEM((s, HD), F32)] * 12 + [pltpu.SemaphoreType.DMA((6,)),
                                                           pltpu.VMEM((2, DN_SUPER, DN_SUPER), F32),
                                                           pltpu.SemaphoreType.DMA((2,))]
        + (_comm_sems(len(gx), 7) if gx else []),
        compiler_params=_cp(("arbitrary", "arbitrary"), DN_BWD_VMEM), name=name)(
            proj, proj, proj, proj, ab, conv_w, conv_w, conv_w, par, gain, o_pre, states, dy, *local, *gx)
    return tuple(res[:10]) + (list(res[10:]),)


def _sum_heads(x, *, name):
    nh, t, c = x.shape
    tm = _pick(t, (1024, 512, 256, 128))

    def body(x_ref, o_ref):
        o_ref[...] = (x_ref[0] + x_ref[1] + x_ref[2] + x_ref[3]).astype(BF16)

    return pl.pallas_call(
        body, grid=(t // tm,), in_specs=[pl.BlockSpec((nh, tm, c), lambda i: (0, i, 0))],
        out_specs=pl.BlockSpec((tm, c), lambda i: (i, 0)), out_shape=jax.ShapeDtypeStruct((t, c), BF16),
        compiler_params=_cp(("parallel",)), name=name)(x)


MERGE_TM = 256


def _merge_fwd(x, proj, yp, yd, ys, b_gate, wb, wo, *, name):
    t, d = x.shape
    tm = _pick(t, (MERGE_TM, 128))

    def body(x_ref, g0_ref, g1_ref, g2_ref, yp_ref, yd_ref, ys_ref, bg_ref, wb_ref, wo_ref, o_ref):
        merged = jnp.zeros((tm, d), F32)
        for n, (g_ref, y_ref) in enumerate(((g0_ref, yp_ref), (g1_ref, yd_ref), (g2_ref, ys_ref))):
            gate = _sigmoid(g_ref[...].astype(F32) + bg_ref[:, n * d:(n + 1) * d])
            merged = merged + gate * _bdot(y_ref[...], wb_ref[n])
        o_ref[...] = x_ref[...] + _bdot(merged, wo_ref[...])

    row = pl.BlockSpec((tm, d), lambda i: (i, 0))
    yblk = pl.BlockSpec((tm, BW), lambda i: (i, 0))

    def gl(n):
        return pl.BlockSpec((tm, d), lambda i: (i, CB_GATE + n))

    return pl.pallas_call(
        body, grid=(t // tm,),
        in_specs=[row, gl(0), gl(1), gl(2), yblk, yblk, yblk, pl.BlockSpec((1, 3 * d), lambda i: (0, 0)),
                  pl.BlockSpec((3, BW, d), lambda i: (0, 0, 0)), pl.BlockSpec((d, d), lambda i: (0, 0))],
        out_specs=row, out_shape=jax.ShapeDtypeStruct((t, d), F32),
        compiler_params=_cp(("parallel",)), name=name)(x, proj, proj, proj, yp, yd, ys, b_gate, wb, wo)


def _merge_bwd(proj, yp, yd, ys, b_gate, wb, wo, dx, *, name):
    t, d = dx.shape
    tm = _pick(t, (MERGE_TM, 128))

    def body(g0_ref, g1_ref, g2_ref, yp_ref, yd_ref, ys_ref, bg_ref, wb_ref, wo_ref, dx_ref,
             dyp_ref, dyd_ref, dys_ref, dgl_ref, mg_ref, dxh_ref, dbd_ref, dbg_ref):
        dxh = dx_ref[...].astype(BF16)
        dxh_ref[...] = dxh
        dmerged = _bdot(dxh, wo_ref[...], NT_DIMS)
        merged = jnp.zeros((tm, d), F32)

        @pl.when(pl.program_id(0) == 0)
        def _():
            dbg_ref[...] = jnp.zeros_like(dbg_ref)

        for n, (g_ref, y_ref, dy_ref) in enumerate(((g0_ref, yp_ref, dyp_ref), (g1_ref, yd_ref, dyd_ref),
                                                    (g2_ref, ys_ref, dys_ref))):
            gate = _sigmoid(g_ref[...].astype(F32) + bg_ref[:, n * d:(n + 1) * d])
            bd = _bdot(y_ref[...], wb_ref[n])
            merged = merged + gate * bd
            dgl = dmerged * bd * gate * (1.0 - gate)
            dgl_ref[:, n * d:(n + 1) * d] = dgl.astype(BF16)
            dbg_ref[:, n * d:(n + 1) * d] += jnp.sum(dgl, axis=0, keepdims=True)
            dbd = (dmerged * gate).astype(BF16)
            dbd_ref[n] = dbd
            dy_ref[...] = _bdot(dbd, wb_ref[n], NT_DIMS)
        mg_ref[...] = merged.astype(BF16)

    row = pl.BlockSpec((tm, d), lambda i: (i, 0))
    yblk = pl.BlockSpec((tm, BW), lambda i: (i, 0))
    bgv = pl.BlockSpec((1, 3 * d), lambda i: (0, 0))

    def gl(n):
        return pl.BlockSpec((tm, d), lambda i: (i, CB_GATE + n))

    ysds = jax.ShapeDtypeStruct((t, BW), F32)
    return pl.pallas_call(
        body, grid=(t // tm,),
        in_specs=[gl(0), gl(1), gl(2), yblk, yblk, yblk, bgv,
                  pl.BlockSpec((3, BW, d), lambda i: (0, 0, 0)), pl.BlockSpec((d, d), lambda i: (0, 0)), row],
        out_specs=[yblk, yblk, yblk, pl.BlockSpec((tm, 3 * d), lambda i: (i, 0)), row, row,
                   pl.BlockSpec((3, tm, d), lambda i: (0, i, 0)), bgv],
        out_shape=[ysds, ysds, ysds, jax.ShapeDtypeStruct((t, 3 * d), BF16), jax.ShapeDtypeStruct((t, d), BF16),
                   jax.ShapeDtypeStruct((t, d), BF16), jax.ShapeDtypeStruct((3, t, d), BF16),
                   jax.ShapeDtypeStruct((1, 3 * d), F32)],
        compiler_params=_cp(("arbitrary",)), name=name)(proj, proj, proj, yp, yd, ys, b_gate, wb, wo, dx)


def _loss_head(x, g, target, *, name):
    t, d = x.shape
    tm = _pick(t, (512, 256, 128))

    def body(x_ref, g_ref, t_ref, dx_ref, dg_ref, loss_ref):
        xhat, rstd = _rms_stats(x_ref[...])
        err = xhat * g_ref[...] - t_ref[...]
        dx, dg = _rms_bwd_vals(err * (1.0 / d), xhat, rstd, g_ref[...])
        dx_ref[...] = dx

        @pl.when(pl.program_id(0) == 0)
        def _():
            dg_ref[...] = jnp.zeros_like(dg_ref)
            loss_ref[...] = jnp.zeros_like(loss_ref)

        dg_ref[...] += dg
        part = jnp.sum(jnp.sum(err * err, axis=1, keepdims=True), axis=0, keepdims=True) * (0.5 / d)
        loss_ref[...] += jnp.broadcast_to(part, (1, LANE))

    row = pl.BlockSpec((tm, d), lambda i: (i, 0))
    vec = pl.BlockSpec((1, d), lambda i: (0, 0))
    return pl.pallas_call(
        body, grid=(t // tm,), in_specs=[row, vec, row],
        out_specs=[row, vec, pl.BlockSpec((1, LANE), lambda i: (0, 0))],
        out_shape=[jax.ShapeDtypeStruct((t, d), F32), jax.ShapeDtypeStruct((1, d), F32),
                   jax.ShapeDtypeStruct((1, LANE), F32)],
        compiler_params=_cp(("arbitrary",)), name=name)(x, g.reshape(1, d), target)


def _adamw(w, g, m, v, *, name):
    rows, cols = w.shape
    fits = [c for c in (1024, 704, 512, 352, 256, 128, 64, 32, 16, 8) if c * cols * 4 * 14 <= VMEM_LIMIT // 2]
    tr = _pick(rows, fits)
    c1 = 1.0 / (1.0 - ADAM_B1 ** ADAM_STEP)
    c2 = 1.0 / (1.0 - ADAM_B2 ** ADAM_STEP)

    def body(w_ref, g_ref, m_ref, v_ref, d_ref, nm_ref, nv_ref):
        g = g_ref[...]
        nm = ADAM_B1 * m_ref[...] + (1.0 - ADAM_B1) * g
        nv = ADAM_B2 * v_ref[...] + (1.0 - ADAM_B2) * (g * g)
        nm_ref[...] = nm
        nv_ref[...] = nv
        d_ref[...] = -ADAM_LR * ((nm * c1) / (jnp.sqrt(nv * c2) + ADAM_EPS) + ADAM_WD * w_ref[...])

    blk = pl.BlockSpec((tr, cols), lambda i: (i, 0))
    sds = jax.ShapeDtypeStruct((rows, cols), F32)
    return pl.pallas_call(
        body, grid=(rows // tr,), in_specs=[blk] * 4, out_specs=[blk] * 3, out_shape=[sds] * 3,
        compiler_params=_cp(("parallel",)), name=name)(w, g, m, v)


MESH_ID = pl.DeviceIdType.MESH
HBM_SPEC = pl.BlockSpec(memory_space=pl.ANY)
OTHER_CHIPS = ((1, 0), (0, 1), (1, 1))


def _at_slot(ref, nl, slot):
    return ref.at[(slice(None),) * nl + (slot,)]


def _slotted(shape, nl, slots):
    return tuple(shape[:nl]) + (slots,) + tuple(shape[nl:])


def _flip(v, f):
    return 1 - v if f else v


def _comm_call(body, n, out_shapes, n_remote, args, name):
    return pl.pallas_call(
        body, out_shape=out_shapes, in_specs=[HBM_SPEC] * len(args), out_specs=[HBM_SPEC] * len(out_shapes),
        scratch_shapes=[pltpu.SemaphoreType.DMA((n * n_remote,)), pltpu.SemaphoreType.DMA((n * n_remote,)),
                        pltpu.SemaphoreType.DMA((n * 4,))],
        compiler_params=pltpu.CompilerParams(has_side_effects=True), name=name)(*args)


def _gather(xs, nls, *, name):
    n = len(xs)

    def body(*refs):
        start, forward, finish = _gather_parts(refs[:n], refs[n:2 * n], nls, *refs[2 * n:])
        start()
        forward()
        finish()

    return _comm_call(body, n, _gather_shapes(xs, nls), 7, xs, name)


def _gather_shapes(xs, nls):
    return [jax.ShapeDtypeStruct(_slotted(v.shape, nl, N_DEV), v.dtype) for v, nl in zip(xs, nls)]


def _comm_sems(n, n_remote):
    return [pltpu.SemaphoreType.DMA((n * n_remote,)), pltpu.SemaphoreType.DMA((n * n_remote,)),
            pltpu.SemaphoreType.DMA((n * 4,))]


def _gather_parts(x_refs, o_refs, nls, send_sems, recv_sems, local_sems):
    n = len(x_refs)
    x, y, c = lax.axis_index("x"), lax.axis_index("y"), lax.axis_index("c")
    me, sibling = (x, y, c), (x, y, 1 - c)
    chips = [(_flip(x, fx), _flip(y, fy)) for fx, fy in OTHER_CHIPS]

    def copy(a, k, block, to, src=None):
        dst = _at_slot(o_refs[a], nls[a], 4 * block[0] + 2 * block[1] + block[2])
        return pltpu.make_async_remote_copy(
            src_ref=dst if src is None else src, dst_ref=dst, send_sem=send_sems.at[a * 7 + k],
            recv_sem=recv_sems.at[a * 7 + k], device_id=to, device_id_type=MESH_ID)

    def mine(a):
        return pltpu.make_async_copy(x_refs[a], _at_slot(o_refs[a], nls[a], 4 * x + 2 * y + c), local_sems.at[a])

    def first(a):
        return ([copy(a, 0, me, sibling, src=x_refs[a])]
                + [copy(a, 1 + j, me, (*chip, c), src=x_refs[a]) for j, chip in enumerate(chips)])

    def start():
        for a in range(n):
            mine(a).start()
            for cp in first(a):
                cp.start()

    def forward():
        for j, chip in enumerate(chips):
            for a in range(n):
                copy(a, 1 + j, (*chip, c), me).wait_recv()
                copy(a, 4 + j, (*chip, c), sibling).start()

    def finish():
        for a in range(n):
            copy(a, 0, sibling, me).wait_recv()
            for j, chip in enumerate(chips):
                copy(a, 4 + j, (*chip, 1 - c), me).wait_recv()
        for a in range(n):
            for cp in first(a):
                cp.wait_send()
            for j, chip in enumerate(chips):
                copy(a, 4 + j, (*chip, c), sibling).wait_send()
        for a in range(n):
            mine(a).wait()

    return start, forward, finish


ALL_FLIPS = ((0, 0, 1), (0, 1, 0), (0, 1, 1), (1, 0, 0), (1, 0, 1), (1, 1, 0), (1, 1, 1))


def _all_to_all_parts(g_refs, r_refs, nls, send_sems, recv_sems, local_sems):
    del local_sems
    n = len(g_refs)
    x, y, c = lax.axis_index("x"), lax.axis_index("y"), lax.axis_index("c")

    def copies():
        out = []
        for a in range(n):
            for k, (fx, fy, fc) in enumerate(ALL_FLIPS):
                p = (_flip(x, fx), _flip(y, fy), _flip(c, fc))
                out.append(pltpu.make_async_remote_copy(
                    src_ref=_at_slot(g_refs[a], nls[a], 4 * p[0] + 2 * p[1] + p[2]), dst_ref=_at_slot(r_refs[a], nls[a], k),
                    send_sem=send_sems.at[a * 7 + k], recv_sem=recv_sems.at[a * 7 + k], device_id=p,
                    device_id_type=MESH_ID))
        return out

    def start():
        for cp in copies():
            cp.start()

    def finish():
        cps = copies()
        for cp in cps:
            cp.wait_recv()
        for cp in cps:
            cp.wait_send()

    return start, lambda: None, finish


def _all_to_all_shapes(gs, nls):
    return [jax.ShapeDtypeStruct(_slotted(v.shape[:nl] + v.shape[nl + 1:], nl, 7), v.dtype) for v, nl in zip(gs, nls)]


def _scatter_pair(gs, nls, *, name):
    n = len(gs)

    def body(*refs):
        g_refs, got_refs, (send_sems, recv_sems, _) = refs[:n], refs[n:2 * n], refs[2 * n:]
        x, y, c = lax.axis_index("x"), lax.axis_index("y"), lax.axis_index("c")
        remote = []
        for a in range(n):
            for q in range(4):
                rc = pltpu.make_async_remote_copy(
                    src_ref=_at_slot(g_refs[a], nls[a], 2 * q + 1 - c), dst_ref=_at_slot(got_refs[a], nls[a], q),
                    send_sem=send_sems.at[a * 4 + q], recv_sem=recv_sems.at[a * 4 + q], device_id=(x, y, 1 - c),
                    device_id_type=MESH_ID)
                rc.start()
                remote.append(rc)
        for rc in remote:
            rc.wait_recv()
        for rc in remote:
            rc.wait_send()

    outs = [jax.ShapeDtypeStruct(_slotted(v.shape[:nl] + v.shape[nl + 1:], nl, 4), v.dtype) for v, nl in zip(gs, nls)]
    return _comm_call(body, n, outs, 4, gs, name)


def _scatter_chips(ps, nls, *, name):
    n = len(ps)

    def body(*refs):
        p_refs, r_refs, (send_sems, recv_sems, _) = refs[:n], refs[n:2 * n], refs[2 * n:]
        x, y, c = lax.axis_index("x"), lax.axis_index("y"), lax.axis_index("c")
        remote = []
        for a in range(n):
            for k, (fx, fy) in enumerate(OTHER_CHIPS):
                tx, ty = _flip(x, fx), _flip(y, fy)
                rc = pltpu.make_async_remote_copy(
                    src_ref=_at_slot(p_refs[a], nls[a], 2 * tx + ty), dst_ref=_at_slot(r_refs[a], nls[a], k),
                    send_sem=send_sems.at[a * 3 + k], recv_sem=recv_sems.at[a * 3 + k], device_id=(tx, ty, c),
                    device_id_type=MESH_ID)
                rc.start()
                remote.append(rc)
        for rc in remote:
            rc.wait_recv()
        for rc in remote:
            rc.wait_send()

    outs = [jax.ShapeDtypeStruct(_slotted(v.shape[:nl] + v.shape[nl + 1:], nl, 3), v.dtype) for v, nl in zip(ps, nls)]
    return _comm_call(body, n, outs, 3, ps, name)


def _pair_add(g, got, core, *, name):
    rows, cols = g.shape[-2:]
    lf = math.prod(got.shape[:-3])
    tr = _pick(rows, (1024, 512, 352, 256, 128))

    def body(core_ref, g_ref, got_ref, o_ref):
        o_ref[...] = (g_ref[...].astype(F32) + got_ref[...].astype(F32)).astype(BF16)

    blk = pl.BlockSpec((None, None, tr, cols), lambda i, q, j, core_ref: (i, q, j, 0))
    out = pl.pallas_call(
        body, grid_spec=pltpu.PrefetchScalarGridSpec(
            num_scalar_prefetch=1, grid=(lf, 4, rows // tr),
            in_specs=[pl.BlockSpec((None, None, None, tr, cols), lambda i, q, j, core_ref: (i, q, core_ref[0], j, 0)),
                      blk], out_specs=blk),
        out_shape=jax.ShapeDtypeStruct((lf, 4, rows, cols), BF16),
        compiler_params=_cp(("parallel", "parallel", "parallel")), name=name)(
            core, g.reshape(lf, 4, 2, rows, cols), got.reshape(lf, 4, rows, cols))
    return out.reshape(got.shape)


def _sum_adamw(p, r, own, w, m, v, layer, prev, *, name):
    shape = w.shape[1:]
    rows, cols = shape[-2:]
    lf = math.prod(shape[:-2])
    np_, nk = p.shape[-3], r.shape[-3]
    fits = [c for c in (1024, 512, 352, 256, 128, 64, 32, 16) if c * cols * (7 * 4 + (nk + 1) * 2) * 2 <= VMEM_LIMIT // 2]
    tr = _pick(rows, fits)
    c1 = 1.0 / (1.0 - ADAM_B1 ** ADAM_STEP)
    c2 = 1.0 / (1.0 - ADAM_B2 ** ADAM_STEP)

    def body(own_ref, p_ref, r_ref, w_ref, m_ref, v_ref, *rest):
        g_ref, d_ref, nm_ref, nv_ref = rest[-4:]
        g = p_ref[...].astype(F32)
        for k in range(nk):
            g = g + r_ref[k].astype(F32)
        g_ref[...] = g
        nm = ADAM_B1 * m_ref[...] + (1.0 - ADAM_B1) * g
        nv = ADAM_B2 * v_ref[...] + (1.0 - ADAM_B2) * (g * g)
        nm_ref[...] = nm
        nv_ref[...] = nv
        d_ref[...] = -ADAM_LR * ((nm * c1) / (jnp.sqrt(nv * c2) + ADAM_EPS) + ADAM_WD * w_ref[...])

    wblk = pl.BlockSpec((None, None, tr, cols), lambda i, j, own_ref: (layer, i, j, 0))
    full = (w.shape[0], lf, rows, cols)
    sds = jax.ShapeDtypeStruct(full, F32)
    prev = [] if prev is None else [a.reshape(full) for a in prev]
    outs = pl.pallas_call(
        body, grid_spec=pltpu.PrefetchScalarGridSpec(
            num_scalar_prefetch=1, grid=(lf, rows // tr),
            in_specs=[pl.BlockSpec((None, None, tr, cols), lambda i, j, own_ref: (i, own_ref[0], j, 0)),
                      pl.BlockSpec((None, nk, tr, cols), lambda i, j, own_ref: (i, 0, j, 0))] + [wblk] * 3
            + [HBM_SPEC] * len(prev),
            out_specs=[wblk] * 4),
        out_shape=[sds] * 4, input_output_aliases={6 + i: i for i in range(len(prev))},
        compiler_params=_cp(("parallel", "parallel")), name=name)(
            own, p.reshape(lf, np_, rows, cols), r.reshape(lf, nk, rows, cols), w.reshape(full), m.reshape(full),
            v.reshape(full), *prev)
    return [o.reshape(w.shape) for o in outs]


def _sum_slots(x, *, name):
    nd, rows, cols = x.shape
    tr = _pick(rows, (512, 256, 128, 64, 32, 16, 8))

    def body(x_ref, o_ref):
        acc = x_ref[0].astype(F32)
        for j in range(1, nd):
            acc = acc + x_ref[j].astype(F32)
        o_ref[...] = acc

    return pl.pallas_call(
        body, grid=(rows // tr,), in_specs=[pl.BlockSpec((nd, tr, cols), lambda i: (0, i, 0))],
        out_specs=pl.BlockSpec((tr, cols), lambda i: (i, 0)), out_shape=jax.ShapeDtypeStruct((rows, cols), F32),
        compiler_params=_cp(("parallel",)), name=name)(x)


def _pad_rows(a, mult=8):
    r = (-a.shape[0]) % mult
    return jnp.pad(a, ((0, r), (0, 0))) if r else a


def _flat128(a):
    f = a.reshape(-1)
    return jnp.pad(f, (0, (-f.shape[0]) % LANE)).reshape(-1, LANE)


def _unshard(gathered, shape, axis):
    g = gathered.reshape((N_DEV,) + tuple(shape))
    g = jnp.moveaxis(g, 0, axis)
    full = list(shape)
    full[axis] *= N_DEV
    return g.reshape(full)


def _col_shards(full):
    rows, cols = full.shape
    return jnp.moveaxis(full.reshape(rows, N_DEV, cols // N_DEV), 1, 0)


BIG = (("ffn_w_gate", 2), ("ffn_w_up", 2), ("ffn_w_down", 2), ("w_in", 1), ("w_branch", 2), ("w_out", 1))


def kernel(x, ffn_norm, ffn_w_gate, ffn_w_up, ffn_w_down, mix_norm, w_in, b_gate, pool_w, pool_scale, dn_conv, dn_A_log, dn_dt_bias, dn_out_norm, w_branch, w_out, final_norm, loss_target, m_ffn_norm, m_ffn_w_gate, m_ffn_w_up, m_ffn_w_down, m_mix_norm, m_w_in, m_b_gate, m_pool_w, m_pool_scale, m_dn_conv, m_dn_A_log, m_dn_dt_bias, m_dn_out_norm, m_w_branch, m_w_out, m_final_norm, v_ffn_norm, v_ffn_w_gate, v_ffn_w_up, v_ffn_w_down, v_mix_norm, v_w_in, v_b_gate, v_pool_w, v_pool_scale, v_dn_conv, v_dn_A_log, v_dn_dt_bias, v_dn_out_norm, v_w_branch, v_w_out, v_final_norm):
    wts = dict(ffn_norm=ffn_norm, ffn_w_gate=ffn_w_gate, ffn_w_up=ffn_w_up, ffn_w_down=ffn_w_down, mix_norm=mix_norm,
               w_in=w_in, b_gate=b_gate, pool_w=pool_w, pool_scale=pool_scale, dn_conv=dn_conv, dn_A_log=dn_A_log,
               dn_dt_bias=dn_dt_bias, dn_out_norm=dn_out_norm, w_branch=w_branch, w_out=w_out, final_norm=final_norm)
    mom = dict(ffn_norm=m_ffn_norm, ffn_w_gate=m_ffn_w_gate, ffn_w_up=m_ffn_w_up, ffn_w_down=m_ffn_w_down,
               mix_norm=m_mix_norm, w_in=m_w_in, b_gate=m_b_gate, pool_w=m_pool_w, pool_scale=m_pool_scale,
               dn_conv=m_dn_conv, dn_A_log=m_dn_A_log, dn_dt_bias=m_dn_dt_bias, dn_out_norm=m_dn_out_norm,
               w_branch=m_w_branch, w_out=m_w_out, final_norm=m_final_norm)
    var = dict(ffn_norm=v_ffn_norm, ffn_w_gate=v_ffn_w_gate, ffn_w_up=v_ffn_w_up, ffn_w_down=v_ffn_w_down,
               mix_norm=v_mix_norm, w_in=v_w_in, b_gate=v_b_gate, pool_w=v_pool_w, pool_scale=v_pool_scale,
               dn_conv=v_dn_conv, dn_A_log=v_dn_A_log, dn_dt_bias=v_dn_dt_bias, dn_out_norm=v_dn_out_norm,
               w_branch=v_w_branch, w_out=v_w_out, final_norm=v_final_norm)
    nb, s, d = x.shape
    t = nb * s
    me = 4 * lax.axis_index("x") + 2 * lax.axis_index("y") + lax.axis_index("c")

    big = [n for n, _ in BIG]
    nls = [nl - 1 for _, nl in BIG]
    shards = lambda l: [wts[n][l].astype(BF16) for n in big]
    small_sh = jnp.concatenate([_flat128(ffn_norm), _flat128(dn_conv)], axis=0)
    ffn3 = big[:3]
    *pre0, small_g = _gather([wts[n][0, 0].astype(BF16) for n in ffn3] + [small_sh], [0] * 4, name="gather_weights")
    rest0 = [wts[n][0, 1].astype(BF16) for n in ffn3] + [wts[n][0].astype(BF16) for n in big[3:]]
    rest0_nls = [0] * 3 + nls[3:]
    full = [None] * DEPTH

    def mixer_weights(l):
        w_in_full = jnp.moveaxis(full[l]["w_in"], 0, 1).reshape(d, -1)
        w_main = jnp.concatenate([w_in_full[:, :AB_LO], w_in_full[:, AB_HI:]], axis=1)
        w_ab = jnp.pad(w_in_full[:, AB_LO:AB_HI], ((0, 0), (0, LANE - (AB_HI - AB_LO))))
        wb = jnp.moveaxis(full[l]["w_branch"], 1, 2).reshape(3, BW, d)
        return w_main, w_ab, wb, full[l]["w_out"].reshape(d, d)

    nfr = ffn_norm.size // LANE
    ffn_norm_full = _unshard(small_g[:, :nfr], ffn_norm.shape, 2)
    dn_conv_full = _unshard(small_g[:, nfr:], dn_conv.shape, 2)
    pool_w_h = pool_w.astype(BF16)

    xs = x.reshape(t, d)
    saved = []
    for l in range(DEPTH):
        sv = dict(x0=xs)
        if l == 0:
            xs, a0, b0, got = _ffn_fwd(xs, ffn_norm_full[0, 0], *pre0, name="ffn_fwd_gather", gather=(rest0, rest0_nls))
            full[0] = dict(zip(ffn3, zip(pre0, got[:3])), **dict(zip(big[3:], got[3:])))
        else:
            xs, a0, b0, _ = _ffn_fwd(xs, ffn_norm_full[l, 0], full[l]["ffn_w_gate"][0], full[l]["ffn_w_up"][0],
                                     full[l]["ffn_w_down"][0], name="ffn_fwd")
        sv["ab0"] = (a0, b0)
        sv["x1"] = xs
        w_main, w_ab, wb, wo = mixer_weights(l)
        h = _rms_fwd(xs, mix_norm[l], name="mix_rms")
        proj = _mm(h, w_main, out_dtype=BF16, name="proj")
        ab = _mm(h, w_ab, name="proj_ab")
        par = jnp.pad(jnp.stack([dn_A_log[l], dn_dt_bias[l]]), ((0, 6), (0, LANE - NH)))
        gain = dn_out_norm[l].reshape(1, HD)
        psc = pool_scale[l].reshape(1, BW)
        yp = _pool_fwd(proj, pool_w_h[l], psc, nb, s, name="pool_fwd")
        yd, o_pre, states, dn_local, gat = _dn_fwd(proj, ab, dn_conv_full[l], par, gain, nb, s,
                                         name="dn_fwd" if l == DEPTH - 1 else "dn_fwd_gather",
                                         gather=(shards(l + 1)[:3], nls[:3]) if l < DEPTH - 1 else None)
        ys, sb_ctr, gat2 = _sb_fwd(proj, nb, s, name="sb_fwd" if l == DEPTH - 1 else "sb_fwd_gather",
                                   gather=(shards(l + 1)[3:], nls[3:]) if l < DEPTH - 1 else None)
        if l < DEPTH - 1:
            full[l + 1] = dict(zip(big, gat + gat2))
        bg = b_gate[l].reshape(1, 3 * d)
        xs = _merge_fwd(xs, proj, yp, yd, ys, bg, wb, wo, name="merge_fwd")
        sv.update(x2=xs, h=h, proj=proj, ab=ab, par=par, gain=gain, psc=psc, yp=yp, yd=yd, ys=ys, sb_ctr=sb_ctr, o_pre=o_pre,
                  states=states, dn_local=dn_local, bg=bg, w_main=w_main, w_ab=w_ab, wb=wb, wo=wo)
        xs, a1, b1, _ = _ffn_fwd(xs, ffn_norm_full[l, 1], full[l]["ffn_w_gate"][1], full[l]["ffn_w_up"][1],
                                 full[l]["ffn_w_down"][1], name="ffn_fwd")
        sv["ab1"] = (a1, b1)
        saved.append(sv)

    dx, g_final, loss_row = _loss_head(xs, final_norm, loss_target.reshape(t, d), name="loss_head")
    loss = lax.psum(loss_row[0, 0], ("x", "y", "c"))

    gw = {n: [None] * DEPTH for n in ("ffn_norm", "ffn_w_gate", "ffn_w_up", "ffn_w_down", "mix_norm", "w_in", "b_gate",
                                      "pool_w", "pool_scale", "dn_conv", "dn_A_log", "dn_dt_bias", "dn_out_norm",
                                      "w_branch", "w_out")}

    me_i = me.astype(jnp.int32).reshape(1)
    updated = {n: None for n in big}
    pending = None

    def finish_layer(l, own_blocks, arrived, own_slot):
        for n, p, r in zip(big, own_blocks, arrived):
            updated[n] = _sum_adamw(p, r, own_slot, wts[n], mom[n], var[n], l, updated[n], name=f"adamw_{n}_{l}")

    def ffn_back(l, i, x_in, dy):
        dxi, dg, hb, dyh, da, db, sact = _ffn_bwd(x_in, ffn_norm_full[l, i], full[l]["ffn_w_gate"][i],
                                                  full[l]["ffn_w_up"][i], full[l]["ffn_w_down"][i],
                                                  *saved[l][f"ab{i}"], dy, name="ffn_bwd")
        return dxi, dg, (_mm_slots(hb, da, name="dw_gate_up"), _mm_slots(hb, db, name="dw_gate_up"),
                         _mm_slots(sact, dyh, name="dw_down"))

    for l in reversed(range(DEPTH)):
        sv = saved[l]
        dx, dg1, (dwg1, dwu1, dwd1) = ffn_back(l, 1, sv["x2"], dx)
        dyp, dyd, dys, dgl, merged, dxh, dbd, dbg = _merge_bwd(sv["proj"], sv["yp"], sv["yd"], sv["ys"], sv["bg"],
                                                               sv["wb"], sv["wo"], dx, name="merge_bwd")
        gw["w_out"][l] = _mm(merged, dxh, ta=True, out_dtype=BF16, name="dw_out").reshape(N_DEV, d // N_DEV, d)
        gw["w_branch"][l] = jnp.stack([_col_shards(_mm(y, dbd[n], ta=True, out_dtype=BF16, name="dw_branch"))
                                       for n, y in enumerate((sv["yp"], sv["yd"], sv["ys"]))])
        gw["b_gate"][l] = dbg.reshape(3 * d)
        du, dpw, dps = _pool_bwd(sv["proj"], pool_w_h[l], sv["psc"], dyp, nb, s, name="pool_bwd")
        gw["pool_w"][l], gw["pool_scale"][l] = dpw, dps.reshape(BW)
        own = [gw[n][pending] for n in big] if pending is not None else []
        dqr, dkr, dvr, dz, dab4, dcq, dck, dcv, dpar, dgain, arrived_ffn = _dn_bwd(
            sv["proj"], sv["ab"], dn_conv_full[l], sv["par"], sv["gain"], sv["o_pre"], sv["states"], sv["dn_local"],
            dyd, nb, s, name="dn_bwd_scatter" if own else "dn_bwd", scatter=(own[:3], nls[:3]) if own else None)
        gw["dn_conv"][l] = jnp.concatenate([dcq, dck, dcv], axis=1)
        gw["dn_A_log"][l], gw["dn_dt_bias"][l], gw["dn_out_norm"][l] = dpar[:, 0, 0], dpar[:, 1, 0], dgain.reshape(HD)
        dsq, dsk, dsv, arrived_rest = _sb_bwd(sv["proj"], sv["sb_ctr"], dys, nb, s,
                                              name="sb_bwd_scatter" if own else "sb_bwd",
                                              scatter=(own[3:], nls[3:]) if own else None)
        if own:
            finish_layer(pending, own, arrived_ffn + arrived_rest, me_i)
        dab = _sum_heads(dab4, name="sum_heads")
        dproj = jnp.concatenate([du.astype(BF16), dqr.astype(BF16), dkr.astype(BF16), dvr.astype(BF16),
                                 dz.astype(BF16), dsq.astype(BF16), dsk.astype(BF16), dsv.astype(BF16), dgl], axis=1)
        dw_main = _mm(sv["h"], dproj, ta=True, out_dtype=BF16, name="dw_in")
        dw_ab = _mm(sv["h"], dab, ta=True, out_dtype=BF16, name="dw_ab")
        gw["w_in"][l] = _col_shards(jnp.concatenate([dw_main[:, :AB_LO], dw_ab[:, :AB_HI - AB_LO],
                                                     dw_main[:, AB_LO:]], axis=1))
        dh_main = _mm(dproj, sv["w_main"], tb=True, name="dh_mix")
        dh_ab = _mm(dab, sv["w_ab"], tb=True, name="dh_mix_ab")
        dx, dgm = _rms_bwd(sv["x1"], mix_norm[l], dh_main, dh_ab, dx, name="mix_rms_bwd")
        gw["mix_norm"][l] = dgm.reshape(d)
        dx, dg0, (dwg0, dwu0, dwd0) = ffn_back(l, 0, sv["x0"], dx)
        gw["ffn_norm"][l] = jnp.stack([dg0.reshape(d), dg1.reshape(d)])
        gw["ffn_w_gate"][l] = jnp.stack([dwg0, dwg1])
        gw["ffn_w_up"][l] = jnp.stack([dwu0, dwu1])
        gw["ffn_w_down"][l] = jnp.stack([dwd0, dwd1])
        pending = l
    grad_x = dx.reshape(nb, s, d)

    core = lax.axis_index("c").astype(jnp.int32).reshape(1)
    chip = (2 * lax.axis_index("x") + lax.axis_index("y")).astype(jnp.int32).reshape(1)
    last = [gw[n][0] for n in big]
    got = _scatter_pair(last, nls, name="scatter_grads_pair")
    chip_sums = [_pair_add(g, b, core, name="add_pair_" + n) for n, g, b in zip(big, last, got)]
    finish_layer(0, chip_sums, _scatter_chips(chip_sums, nls, name="scatter_grads_chips"), chip)
    grads, delta, new_m, new_v = ({n: updated[n][i] for n in big} for i in range(4))
    gw = {n: jnp.stack(v) for n, v in gw.items() if n not in big}
    gw["final_norm"] = g_final.reshape(d)

    small = ("ffn_norm", "mix_norm", "b_gate", "pool_w", "pool_scale", "dn_conv", "dn_A_log", "dn_dt_bias",
             "dn_out_norm", "final_norm")
    sp = _pad_rows(jnp.concatenate([_flat128(gw[n]) for n in small], axis=0))
    ssum = _sum_slots(_gather([sp], [0], name="gather_small_grads")[0], name="sum_small_grads")
    off = 0
    for n in small:
        r = -(-gw[n].size // LANE)
        g = ssum[off:off + r].reshape(-1)[:gw[n].size].reshape(gw[n].shape)
        off += r
        if n in ("ffn_norm", "dn_conv"):
            w = wts[n].shape[2]
            g = lax.dynamic_slice_in_dim(g, me * w, w, axis=2)
        grads[n] = g

    pk = lambda src: _pad_rows(jnp.concatenate([_flat128(src[n]) for n in small], axis=0))
    dl, nm, nv = _adamw(pk(wts), pk(grads), pk(mom), pk(var), name="adamw_small")
    off = 0
    for n in small:
        r = -(-wts[n].size // LANE)
        for dst, src in ((delta, dl), (new_m, nm), (new_v, nv)):
            dst[n] = src[off:off + r].reshape(-1)[:wts[n].size].reshape(wts[n].shape)
        off += r

    order = ("ffn_norm", "ffn_w_gate", "ffn_w_up", "ffn_w_down", "mix_norm", "w_in", "b_gate", "pool_w", "pool_scale",
             "dn_conv", "dn_A_log", "dn_dt_bias", "dn_out_norm", "w_branch", "w_out", "final_norm")
    return (loss, grad_x, *[grads[n] for n in order], *[delta[n] for n in order], *[new_m[n] for n in order],
            *[new_v[n] for n in order])
```
